```python
import math
import jax, jax.numpy as jnp
from jax import lax
import numpy as np

D_MODEL = 1024
BATCH = 2
SEQ = 8192
DEPTH = 2

P_DIM = 256
N_BRANCH = 4
W_MIX = D_MODEL // 4
HEAD_DIM = 64
N_HEADS = W_MIX // HEAD_DIM
DIFF_DH = HEAD_DIM // 2
ROPE_THETA = 500000.0
ROPE_DIMS = DIFF_DH // 4
Q_BLOCK = 128
RWKV_LORA_W = 64
RWKV_LORA_A = 64
RWKV_LORA_G = 128
RWKV_GN_EPS = 64e-5
DIFF_LN_EPS = 1e-5
GDN_CONV = 4
GDN_CHUNK = 64
FFN_DENSE = 2816
N_EXPERTS = 8
TOP_K = 2
FFN_EXPERT = 3584
MOE_BLOCK = 256
NORM_EPS = 1e-6
N_DENSE = (DEPTH + 1) // 2
N_MOE = DEPTH // 2

RWKV_COLS = 3 * W_MIX + RWKV_LORA_W + RWKV_LORA_A + RWKV_LORA_G
DIFF_COLS = 3 * W_MIX
FOX_COLS = 3 * W_MIX + N_HEADS
GDN_COLS = 4 * W_MIX + 2 * N_HEADS
GATE_COLS = N_BRANCH * D_MODEL
IN_COLS = RWKV_COLS + DIFF_COLS + FOX_COLS + GDN_COLS + GATE_COLS

kernel_name = "hybrid_rwkv7_diffattn_fox_gdn_moe"


def _split(t, sizes):
    return jnp.split(t, [int(c) for c in np.cumsum(sizes)[:-1]], axis=-1)


def rmsnorm(x, g, eps=NORM_EPS):
    xf = x.astype(jnp.float32)
    y = xf * lax.rsqrt(jnp.mean(xf * xf, axis=-1, keepdims=True) + eps)
    return (y * g.astype(jnp.float32)).astype(x.dtype)


def l2norm(x, eps=1e-6):
    xf = x.astype(jnp.float32)
    return xf * lax.rsqrt(jnp.sum(xf * xf, axis=-1, keepdims=True) + eps)


def causal_mask(i, seq):
    qpos = i * Q_BLOCK + jnp.arange(Q_BLOCK)
    return qpos[:, None] >= jnp.arange(seq)[None, :]


def sweep_query_blocks(block_fn, seq):
    out = lax.map(block_fn, jnp.arange(seq // Q_BLOCK))
    nb, b, qb, h, d = out.shape
    return out.transpose(1, 0, 2, 3, 4).reshape(b, nb * qb, h, d)


def partial_rope(x, pos):
    half = ROPE_DIMS // 2
    inv = ROPE_THETA ** (-jnp.arange(half, dtype=jnp.float32) * 2.0 / ROPE_DIMS)
    ang = pos.astype(jnp.float32)[:, None] * inv[None, :]
    shape = (1, pos.shape[0]) + (1,) * (x.ndim - 3) + (half,)
    cos = jnp.cos(ang).reshape(shape)
    sin = jnp.sin(ang).reshape(shape)
    xf = x.astype(jnp.float32)
    x1, x2, rest = xf[..., :half], xf[..., half:ROPE_DIMS], xf[..., ROPE_DIMS:]
    return jnp.concatenate([x1 * cos - x2 * sin, x2 * cos + x1 * sin, rest], axis=-1).astype(x.dtype)


def causal_conv(u, w):
    seq = u.shape[1]
    k = w.shape[-1]
    up = jnp.pad(u, ((0, 0), (k - 1, 0), (0, 0)))
    return sum(up[:, j:j + seq, :] * w[:, j] for j in range(k))


def rwkv7_scan(r, w, k, v, kk, a):
    b, s, h, n = r.shape

    def step(state, inp):
        r_t, w_t, k_t, v_t, kk_t, a_t = inp
        sa = jnp.einsum('bhvk,bhk->bhv', state, -kk_t)
        state = (state * w_t[:, :, None, :]
                 + sa[..., None] * (kk_t * a_t)[:, :, None, :]
                 + v_t[..., None] * k_t[:, :, None, :])
        return state, jnp.einsum('bhvk,bhk->bhv', state, r_t)

    xs = tuple(t.transpose(1, 0, 2, 3) for t in (r, w, k, v, kk, a))
    _, o = lax.scan(step, jnp.zeros((b, h, n, n), jnp.float32), xs)
    return o.transpose(1, 0, 2, 3)


def rwkv7_branch(u, mu, w0, w2, a0, a2, g2, k_k, k_a, r_k, ln_w, ln_b):
    b, s, _ = u.shape
    u = u.astype(jnp.float32)
    u_prev = jnp.pad(u, ((0, 0), (1, 0), (0, 0)))[:, :s]
    xm = u + (u_prev - u) * mu
    r, k, v, xw, xa, xg = _split(xm, [W_MIX, W_MIX, W_MIX, RWKV_LORA_W, RWKV_LORA_A, RWKV_LORA_G])
    logw = -jax.nn.softplus(-(w0 + jnp.tanh(xw) @ w2)) - 0.5
    decay = jnp.exp(-jnp.exp(logw))
    a = jax.nn.sigmoid(a0 + xa @ a2)
    g = jax.nn.sigmoid(xg) @ g2
    heads = lambda t: t.reshape(b, s, N_HEADS, HEAD_DIM)
    kk = l2norm(heads(k * k_k))
    k = k * (1.0 + (a - 1.0) * k_a)
    r_h, k_h, v_h, w_h, a_h = map(heads, (r, k, v, decay, a))
    o = rwkv7_scan(r_h, w_h, k_h, v_h, kk, a_h)
    mean = jnp.mean(o, axis=-1, keepdims=True)
    var = jnp.mean((o - mean) ** 2, axis=-1, keepdims=True)
    o = ((o - mean) * lax.rsqrt(var + RWKV_GN_EPS)).reshape(b, s, W_MIX) * ln_w + ln_b
    bonus = jnp.sum(r_h * k_h * r_k, axis=-1, keepdims=True) * v_h
    return (o + bonus.reshape(b, s, W_MIX)) * g


def diff_branch(u, pos, lam_p, subln_w, layer_idx):
    b, s, _ = u.shape
    q, k, v = _split(u, [W_MIX, W_MIX, W_MIX])
    q = partial_rope(q.reshape(b, s, N_HEADS, 2, DIFF_DH), pos)
    k = partial_rope(k.reshape(b, s, N_HEADS, 2, DIFF_DH), pos)
    v = v.reshape(b, s, N_HEADS, HEAD_DIM)
    lam_init = 0.8 - 0.6 * math.exp(-0.3 * layer_idx)
    lp = lam_p.astype(jnp.float32)
    lam = jnp.exp(jnp.sum(lp[0] * lp[1])) - jnp.exp(jnp.sum(lp[2] * lp[3])) + lam_init
    scale = DIFF_DH ** -0.5

    def blk(i):
        qs = lax.dynamic_slice_in_dim(q, i * Q_BLOCK, Q_BLOCK, axis=1)
        sc = jnp.einsum('bqhcd,bkhcd->bhcqk', qs, k).astype(jnp.float32) * scale
        sc = jnp.where(causal_mask(i, s)[None, None, None], sc, -jnp.inf)
        pm = jax.nn.softmax(sc, axis=-1)
        pd = pm[:, :, 0] - lam * pm[:, :, 1]
        return jnp.einsum('bhqk,bkhd->bqhd', pd.astype(v.dtype), v)

    o = sweep_query_blocks(blk, s)
    o = rmsnorm(o, subln_w, DIFF_LN_EPS) * (1.0 - lam_init)
    return o.reshape(b, s, W_MIX)


def fox_branch(u, f_bias):
    b, s, _ = u.shape
    q, k, v, fl = _split(u, [W_MIX, W_MIX, W_MIX, N_HEADS])
    q = q.reshape(b, s, N_HEADS, HEAD_DIM)
    k = k.reshape(b, s, N_HEADS, HEAD_DIM)
    v = v.reshape(b, s, N_HEADS, HEAD_DIM)
    logf = jax.nn.log_sigmoid(fl.astype(jnp.float32) + f_bias.astype(jnp.float32))
    c = jnp.cumsum(logf, axis=1).transpose(0, 2, 1)
    scale = HEAD_DIM ** -0.5

    def blk(i):
        qs = lax.dynamic_slice_in_dim(q, i * Q_BLOCK, Q_BLOCK, axis=1)
        cq = lax.dynamic_slice_in_dim(c, i * Q_BLOCK, Q_BLOCK, axis=2)
        sc = jnp.einsum('bqhd,bkhd->bhqk', qs, k).astype(jnp.float32) * scale
        sc = sc + (cq[..., :, None] - c[:, :, None, :])
        sc = jnp.where(causal_mask(i, s)[None, None], sc, -jnp.inf)
        pm = jax.nn.softmax(sc, axis=-1)
        return jnp.einsum('bhqk,bkhd->bqhd', pm.astype(v.dtype), v)

    return sweep_query_blocks(blk, s).reshape(b, s, W_MIX)


def gated_delta_chunked(q, k, v, g, beta):
    b, s, h, dk = q.shape
    dv = v.shape[-1]
    c = GDN_CHUNK
    nc = s // c
    chunks = lambda t: t.reshape(b, nc, c, h, t.shape[-1]).transpose(1, 0, 3, 2, 4)
    q, k, v = chunks(q), chunks(k), chunks(v)
    g = g.reshape(b, nc, c, h).transpose(1, 0, 3, 2)
    beta = beta.reshape(b, nc, c, h).transpose(1, 0, 3, 2)
    gam = jnp.cumsum(g, axis=-1)
    idx = jnp.arange(c)
    incl = idx[:, None] >= idx[None, :]
    strict = idx[:, None] > idx[None, :]
    decay = jnp.exp(jnp.where(incl, gam[..., :, None] - gam[..., None, :], -jnp.inf))
    kk = jnp.einsum('nbhik,nbhjk->nbhij', k, k)
    a_mat = jnp.where(strict, beta[..., :, None] * decay * kk, 0.0) + jnp.eye(c, dtype=jnp.float32)
    rhs = jnp.concatenate([beta[..., None] * v, (beta * jnp.exp(gam))[..., None] * k], axis=-1)
    sol = lax.linalg.triangular_solve(a_mat, rhs, left_side=True, lower=True, unit_diagonal=True)
    u0, wm = sol[..., :dv], sol[..., dv:]
    qk = jnp.einsum('nbhik,nbhjk->nbhij', q, k) * decay

    def step(state, xs):
        qc, kc, u0c, wc, qkc, gc = xs
        uc = u0c - jnp.einsum('bhck,bhkv->bhcv', wc, state)
        o = jnp.exp(gc)[..., None] * jnp.einsum('bhck,bhkv->bhcv', qc, state) + jnp.einsum('bhij,bhjv->bhiv', qkc, uc)
        gl = gc[..., -1:]
        state = jnp.exp(gl)[..., None] * state + jnp.einsum('bhjk,bhjv->bhkv', kc * jnp.exp(gl - gc)[..., None], uc)
        return state, o

    _, o = lax.scan(step, jnp.zeros((b, h, dk, dv), jnp.float32), (q, k, u0, wm, qk, gam))
    return o.transpose(1, 0, 3, 2, 4).reshape(b, s, h, dv)


def gdn_branch(u, conv_w, a_log, dt_bias, norm_w):
    b, s, _ = u.shape
    qkv, b_l, a_l, gate = _split(u, [3 * W_MIX, N_HEADS, N_HEADS, W_MIX])
    qkv = jax.nn.silu(causal_conv(qkv, conv_w))
    q, k, v = _split(qkv, [W_MIX, W_MIX, W_MIX])
    heads = lambda t: t.reshape(b, s, N_HEADS, HEAD_DIM)
    q = l2norm(heads(q)) * (HEAD_DIM ** -0.5)
    k = l2norm(heads(k))
    v = heads(v).astype(jnp.float32)
    beta = jax.nn.sigmoid(b_l.astype(jnp.float32))
    g = -jnp.exp(a_log.astype(jnp.float32)) * jax.nn.softplus(a_l.astype(jnp.float32) + dt_bias)
    o = gated_delta_chunked(q, k, v, g, beta)
    o = rmsnorm(o, norm_w) * jax.nn.silu(heads(gate).astype(jnp.float32))
    return o.reshape(b, s, W_MIX)


def hybrid_mixer(h, pos, layer_idx, w_in, w_bo, w_out,
                 rwkv_mu, rwkv_w0, rwkv_w2, rwkv_a0, rwkv_a2, rwkv_g2, rwkv_kk, rwkv_ka, rwkv_rk,
                 rwkv_ln_w, rwkv_ln_b, diff_lam, diff_subln, fox_fbias,
                 gdn_conv, gdn_a_log, gdn_dt_bias, gdn_norm):
    b, s, _ = h.shape
    u = h @ w_in
    u_a, u_b, u_c, u_d, u_g = _split(u, [RWKV_COLS, DIFF_COLS, FOX_COLS, GDN_COLS, GATE_COLS])
    y_a = rwkv7_branch(u_a, rwkv_mu, rwkv_w0, rwkv_w2, rwkv_a0, rwkv_a2, rwkv_g2,
                       rwkv_kk, rwkv_ka, rwkv_rk, rwkv_ln_w, rwkv_ln_b).astype(h.dtype)
    y_b = diff_branch(u_b, pos, diff_lam, diff_subln, layer_idx).astype(h.dtype)
    y_c = fox_branch(u_c, fox_fbias).astype(h.dtype)
    y_d = gdn_branch(u_d, gdn_conv, gdn_a_log, gdn_dt_bias, gdn_norm).astype(h.dtype)
    ys = jnp.stack([y_a, y_b, y_c, y_d], axis=2)
    proj = jnp.einsum('bsnw,nwd->bsnd', ys, w_bo)
    gates = jax.nn.sigmoid(u_g.reshape(b, s, N_BRANCH, D_MODEL))
    return jnp.sum(gates * proj, axis=2) @ w_out


def swiglu(h, wg, wu, wd):
    return (jax.nn.silu(h @ wg) * (h @ wu)) @ wd


def moe_swiglu(h, router, wg, wu, wd):
    b, s, d = h.shape
    n = b * s
    xf = h.reshape(n, d)
    logits = (xf @ router).astype(jnp.float32)
    top_l, top_i = lax.top_k(logits, TOP_K)
    gate = jax.nn.softmax(top_l, axis=-1)
    na = n * TOP_K
    e = top_i.reshape(na)
    tok = jnp.arange(na) // TOP_K
    order = jnp.argsort(e)
    e_s, tok_s, w_s = e[order], tok[order], gate.reshape(na)[order]
    counts = jnp.zeros((N_EXPERTS,), jnp.int32).at[e].add(1)
    pcounts = (counts + MOE_BLOCK - 1) // MOE_BLOCK * MOE_BLOCK
    offs = jnp.cumsum(counts) - counts
    pends = jnp.cumsum(pcounts)
    poffs = pends - pcounts
    dest = poffs[e_s] + (jnp.arange(na) - offs[e_s])
    nb = (na + MOE_BLOCK - 1) // MOE_BLOCK + N_EXPERTS
    xbuf = jnp.zeros((nb * MOE_BLOCK, d), xf.dtype).at[dest].set(xf[tok_s])
    starts = jnp.arange(nb) * MOE_BLOCK
    blk_e = jnp.minimum(jnp.sum(pends[None, :] <= starts[:, None], axis=1), N_EXPERTS - 1)

    def run(args):
        xb, eb = args
        return (jax.nn.silu(xb @ wg[eb]) * (xb @ wu[eb])) @ wd[eb]

    ybuf = lax.map(run, (xbuf.reshape(nb, MOE_BLOCK, d), blk_e)).reshape(nb * MOE_BLOCK, d)
    y = jax.ops.segment_sum(ybuf[dest] * w_s[:, None].astype(xf.dtype), tok_s, num_segments=n)
    return y.reshape(b, s, d)


def setup_inputs(seed: int = 0) -> dict:
    key = jax.random.key(seed)
    ks = jax.random.split(key, 36)
    nrm = lambda i, shape, sc: sc * jax.random.normal(ks[i], shape, jnp.float32)
    uni = lambda i, shape, lo, hi: jax.random.uniform(ks[i], shape, jnp.float32, lo, hi)
    dt = jnp.exp(uni(24, (DEPTH, N_HEADS), math.log(1e-3), math.log(1e-1)))
    return {
        "x": nrm(0, (BATCH, SEQ, D_MODEL), 1.0),
        "p": nrm(1, (DEPTH, BATCH, SEQ, P_DIM), 1.0),
        "norm_mix": 1.0 + nrm(2, (DEPTH, D_MODEL), 0.02),
        "norm_ffn": 1.0 + nrm(3, (DEPTH, D_MODEL), 0.02),
        "norm_ple": 1.0 + nrm(4, (DEPTH, D_MODEL), 0.02),
        "w_in": nrm(5, (DEPTH, D_MODEL, IN_COLS), D_MODEL ** -0.5),
        "w_bo": nrm(6, (DEPTH, N_BRANCH, W_MIX, D_MODEL), W_MIX ** -0.5),
        "w_out": nrm(7, (DEPTH, D_MODEL, D_MODEL), D_MODEL ** -0.5),
        "rwkv_mu": uni(8, (DEPTH, RWKV_COLS), 0.0, 1.0),
        "rwkv_w0": uni(9, (DEPTH, W_MIX), -4.0, 0.0),
        "rwkv_w2": nrm(10, (DEPTH, RWKV_LORA_W, W_MIX), 0.5 * RWKV_LORA_W ** -0.5),
        "rwkv_a0": nrm(11, (DEPTH, W_MIX), 0.1),
        "rwkv_a2": nrm(12, (DEPTH, RWKV_LORA_A, W_MIX), RWKV_LORA_A ** -0.5),
        "rwkv_g2": nrm(13, (DEPTH, RWKV_LORA_G, W_MIX), RWKV_LORA_G ** -0.5),
        "rwkv_kk": 0.85 + nrm(14, (DEPTH, W_MIX), 0.05),
        "rwkv_ka": 1.0 + nrm(15, (DEPTH, W_MIX), 0.05),
        "rwkv_rk": nrm(16, (DEPTH, N_HEADS, HEAD_DIM), 0.1),
        "rwkv_ln_w": 1.0 + nrm(17, (DEPTH, W_MIX), 0.02),
        "rwkv_ln_b": nrm(18, (DEPTH, W_MIX), 0.02),
        "diff_lam": nrm(19, (DEPTH, 4, DIFF_DH), 0.1),
        "diff_subln": 1.0 + nrm(20, (DEPTH, HEAD_DIM), 0.02),
        "fox_fbias": 3.0 + nrm(21, (DEPTH, N_HEADS), 0.5),
        "gdn_conv": nrm(22, (DEPTH, 3 * W_MIX, GDN_CONV), GDN_CONV ** -0.5),
        "gdn_a_log": jnp.log(uni(23, (DEPTH, N_HEADS), 1.0, 16.0)),
        "gdn_dt_bias": dt + jnp.log(-jnp.expm1(-dt)),
        "gdn_norm": 1.0 + nrm(25, (DEPTH, HEAD_DIM), 0.02),
        "ffn_w_gate": nrm(26, (N_DENSE, D_MODEL, FFN_DENSE), D_MODEL ** -0.5),
        "ffn_w_up": nrm(27, (N_DENSE, D_MODEL, FFN_DENSE), D_MODEL ** -0.5),
        "ffn_w_down": nrm(28, (N_DENSE, FFN_DENSE, D_MODEL), FFN_DENSE ** -0.5),
        "moe_router": nrm(29, (N_MOE, D_MODEL, N_EXPERTS), D_MODEL ** -0.5),
        "moe_w_gate": nrm(30, (N_MOE, N_EXPERTS, D_MODEL, FFN_EXPERT), D_MODEL ** -0.5),
        "moe_w_up": nrm(31, (N_MOE, N_EXPERTS, D_MODEL, FFN_EXPERT), D_MODEL ** -0.5),
        "moe_w_down": nrm(32, (N_MOE, N_EXPERTS, FFN_EXPERT, D_MODEL), FFN_EXPERT ** -0.5),
        "ple_proj": nrm(33, (DEPTH, P_DIM, D_MODEL), P_DIM ** -0.5),
        "ple_gate": nrm(34, (DEPTH, D_MODEL, D_MODEL), D_MODEL ** -0.5),
        "final_norm": 1.0 + nrm(35, (D_MODEL,), 0.02),
    }


def reference(x, p, norm_mix, norm_ffn, norm_ple, w_in, w_bo, w_out,
              rwkv_mu, rwkv_w0, rwkv_w2, rwkv_a0, rwkv_a2, rwkv_g2, rwkv_kk, rwkv_ka, rwkv_rk,
              rwkv_ln_w, rwkv_ln_b, diff_lam, diff_subln, fox_fbias,
              gdn_conv, gdn_a_log, gdn_dt_bias, gdn_norm,
              ffn_w_gate, ffn_w_up, ffn_w_down,
              moe_router, moe_w_gate, moe_w_up, moe_w_down,
              ple_proj, ple_gate, final_norm):
    pos = jnp.arange(x.shape[1])
    for i in range(DEPTH):
        h = rmsnorm(x, norm_mix[i])
        x = x + hybrid_mixer(h, pos, i, w_in[i], w_bo[i], w_out[i],
                             rwkv_mu[i], rwkv_w0[i], rwkv_w2[i], rwkv_a0[i], rwkv_a2[i], rwkv_g2[i],
                             rwkv_kk[i], rwkv_ka[i], rwkv_rk[i], rwkv_ln_w[i], rwkv_ln_b[i],
                             diff_lam[i], diff_subln[i], fox_fbias[i],
                             gdn_conv[i], gdn_a_log[i], gdn_dt_bias[i], gdn_norm[i])
        h = rmsnorm(x, norm_ffn[i])
        j = i // 2
        if i % 2 == 0:
            x = x + swiglu(h, ffn_w_gate[j], ffn_w_up[j], ffn_w_down[j])
        else:
            x = x + moe_swiglu(h, moe_router[j], moe_w_gate[j], moe_w_up[j], moe_w_down[j])
        h = rmsnorm(x, norm_ple[i])
        x = x + jax.nn.sigmoid(h @ ple_gate[i]) * (p[i] @ ple_proj[i])
    return rmsnorm(x, final_norm)
```

```python
import functools
import math

import jax
import jax.numpy as jnp
from jax import lax
from jax.experimental import pallas as pl
from jax.experimental.pallas import tpu as pltpu

F32 = jnp.float32
BF16 = jnp.bfloat16
HIGHEST = lax.Precision.HIGHEST

D_MODEL = 1024
P_DIM = 256
W_MIX = 256
HEAD_DIM = 64
N_HEADS = 4
DIFF_DH = 32
ROPE_THETA = 500000.0
ROPE_DIMS = 8
RWKV_GN_EPS = 64e-5
DIFF_LN_EPS = 1e-5
GDN_CONV = 4
CHUNK = 64
N_EXPERTS = 8
NORM_EPS = 1e-6
L2_EPS = 1e-6
LANES = 128
BF16_ROWS = 16

U_GATE = 0
U_RWKV = 4096
U_GDN = 5120
U_DIFF = 6144
U_FOX = 6912
U_COLS = 7680
SM_FOX, SM_BETA, SM_DEC = 0, 4, 8


def _cparams(semantics, vmem_mb=48):
    return pltpu.CompilerParams(dimension_semantics=semantics, vmem_limit_bytes=vmem_mb * 1024 * 1024)


def _mm(a, b):
    return jnp.dot(a.astype(BF16), b.astype(BF16), preferred_element_type=F32)


def _mm_nt(a, b):
    return lax.dot_general(a.astype(BF16), b.astype(BF16), (((1,), (1,)), ((), ())), preferred_element_type=F32)


def _mm_tn(a, b):
    return lax.dot_general(a.astype(BF16), b.astype(BF16), (((0,), (0,)), ((), ())), preferred_element_type=F32)


def _mm_f32(a, b):
    return jnp.dot(a, b, preferred_element_type=F32, precision=HIGHEST)


def _rms(x, g, eps):
    return x * lax.rsqrt(jnp.mean(x * x, axis=-1, keepdims=True) + eps) * g


def _sigmoid(x):
    return 1.0 / (1.0 + jnp.exp(-x))


def _silu(x):
    return x * _sigmoid(x)


def _softplus(x):
    return jnp.maximum(x, 0.0) + jnp.log(1.0 + jnp.exp(-jnp.abs(x)))


def _tri_masks(c):
    ii = lax.broadcasted_iota(jnp.int32, (c, c), 0)
    jj = lax.broadcasted_iota(jnp.int32, (c, c), 1)
    return ii > jj, ii >= jj, ii == jj


def _unit_lower_inverse(n, eye):
    r = eye + n
    p = n
    steps = int(math.log2(n.shape[0])) - 1
    for _ in range(steps):
        p = _mm_f32(p, p)
        r = r + _mm_f32(r, p)
    return r


def _head_block_diag():
    ii = lax.broadcasted_iota(jnp.int32, (W_MIX, W_MIX), 0) // HEAD_DIM
    jj = lax.broadcasted_iota(jnp.int32, (W_MIX, W_MIX), 1) // HEAD_DIM
    return (ii == jj).astype(F32)


def _inproj_kernel(x_ref, g_ref, w_ref, ws_ref, wst_ref, u_ref, scol_ref, srow_ref, h_scr):
    @pl.when(pl.program_id(1) == 0)
    def _():
        hb = _rms(x_ref[...], g_ref[...], NORM_EPS).astype(BF16)
        h_scr[...] = hb
        scol_ref[...] = jnp.dot(hb, ws_ref[...], preferred_element_type=F32)
        srow_ref[...] = lax.dot_general(wst_ref[...], hb, (((1,), (1,)), ((), ())), preferred_element_type=F32)

    u_ref[...] = jnp.dot(h_scr[...], w_ref[...], preferred_element_type=F32).astype(BF16)


def _inproj(x, g, w, ws, wst, tm, tn):
    n = x.shape[0]
    return pl.pallas_call(
        _inproj_kernel,
        grid=(n // tm, U_COLS // tn),
        in_specs=[
            pl.BlockSpec((tm, D_MODEL), lambda i, j: (i, 0)),
            pl.BlockSpec((1, D_MODEL), lambda i, j: (0, 0)),
            pl.BlockSpec((D_MODEL, tn), lambda i, j: (0, j)),
            pl.BlockSpec((D_MODEL, LANES), lambda i, j: (0, 0)),
            pl.BlockSpec((BF16_ROWS, D_MODEL), lambda i, j: (0, 0)),
        ],
        out_specs=[
            pl.BlockSpec((tm, tn), lambda i, j: (i, j)),
            pl.BlockSpec((tm, LANES), lambda i, j: (i, 0)),
            pl.BlockSpec((BF16_ROWS, tm), lambda i, j: (0, i)),
        ],
        out_shape=[
            jax.ShapeDtypeStruct((n, U_COLS), BF16),
            jax.ShapeDtypeStruct((n, LANES), F32),
            jax.ShapeDtypeStruct((BF16_ROWS, n), F32),
        ],
        scratch_shapes=[pltpu.VMEM((tm, D_MODEL), BF16)],
        compiler_params=_cparams(("parallel", "arbitrary")),
        name="inproj",
    )(x, g, w, ws, wst)


def _small_prep_kernel(scol_ref, srow_ref, bcol_ref, brow_ref, acol_ref, arow_ref, ocol_ref, orow_ref,
                       ccol_scr, crow_scr):
    @pl.when(pl.program_id(1) == 0)
    def _():
        ccol_scr[...] = jnp.zeros_like(ccol_scr)
        crow_scr[...] = jnp.zeros_like(crow_scr)

    tm = scol_ref.shape[0]
    _, incl, _ = _tri_masks(tm)
    lower = incl.astype(F32)
    upper = jnp.logical_not(_tri_masks(tm)[0]).astype(F32)

    def funcs(z, idx, neg_a):
        logf = jnp.minimum(z, 0.0) - jnp.log(1.0 + jnp.exp(-jnp.abs(z)))
        beta = _sigmoid(z)
        dec = neg_a * _softplus(z)
        is_f = idx < SM_BETA
        is_b = jnp.logical_and(idx >= SM_BETA, idx < SM_DEC)
        is_d = jnp.logical_and(idx >= SM_DEC, idx < SM_DEC + N_HEADS)
        return jnp.where(is_f, logf, 0.0), jnp.where(is_b, beta, jnp.where(is_d, dec, 0.0)), is_f

    zc = scol_ref[...] + bcol_ref[...]
    lane = lax.broadcasted_iota(jnp.int32, zc.shape, 1)
    logf_c, rest_c, is_f_c = funcs(zc, lane, acol_ref[...])
    cum_c = _mm_f32(lower, logf_c) + ccol_scr[...]
    ccol_scr[...] = cum_c[tm - 1:tm, :]
    ocol_ref[...] = jnp.where(is_f_c, cum_c, rest_c)

    zr = srow_ref[...] + brow_ref[...]
    sub = lax.broadcasted_iota(jnp.int32, zr.shape, 0)
    logf_r, rest_r, is_f_r = funcs(zr, sub, arow_ref[...])
    cum_r = _mm_f32(logf_r, upper) + crow_scr[...]
    crow_scr[...] = cum_r[:, tm - 1:tm]
    orow_ref[...] = jnp.where(is_f_r, cum_r, rest_r)


def _small_prep(scol, srow, bcol, brow, acol, arow, batch, seq, tm):
    n = batch * seq
    nt = seq // tm
    return pl.pallas_call(
        _small_prep_kernel,
        grid=(batch, nt),
        in_specs=[
            pl.BlockSpec((tm, LANES), lambda b, j: (b * nt + j, 0)),
            pl.BlockSpec((BF16_ROWS, tm), lambda b, j: (0, b * nt + j)),
            pl.BlockSpec((1, LANES), lambda b, j: (0, 0)),
            pl.BlockSpec((BF16_ROWS, 1), lambda b, j: (0, 0)),
            pl.BlockSpec((1, LANES), lambda b, j: (0, 0)),
            pl.BlockSpec((BF16_ROWS, 1), lambda b, j: (0, 0)),
        ],
        out_specs=[
            pl.BlockSpec((tm, LANES), lambda b, j: (b * nt + j, 0)),
            pl.BlockSpec((BF16_ROWS, tm), lambda b, j: (0, b * nt + j)),
        ],
        out_shape=[jax.ShapeDtypeStruct((n, LANES), F32), jax.ShapeDtypeStruct((BF16_ROWS, n), F32)],
        scratch_shapes=[pltpu.VMEM((1, LANES), F32), pltpu.VMEM((BF16_ROWS, 1), F32)],
        compiler_params=_cparams(("parallel", "arbitrary")),
        name="small_prep",
    )(scol, srow, bcol, brow, acol, arow)


def _rwkv_prep_kernel(u_ref, up_ref, mu_ref, w0_ref, w2_ref, a0_ref, a2_ref, g2_ref, kk_ref, ka_ref, rk_ref,
                      scan_ref, post_ref, *, tiles_per_seq):
    tm = u_ref.shape[0]
    u = u_ref[...].astype(F32)
    prev = up_ref[...].astype(F32)[BF16_ROWS - 1:BF16_ROWS, :]
    prev = jnp.where(pl.program_id(0) % tiles_per_seq == 0, 0.0, prev)
    rows = lax.broadcasted_iota(jnp.int32, (tm, 1), 0)
    u_prev = jnp.where(rows == 0, prev, pltpu.roll(u, 1, 0))
    xm = u + (u_prev - u) * mu_ref[...]
    r = xm[:, 0:W_MIX]
    k = xm[:, W_MIX:2 * W_MIX]
    v = xm[:, 2 * W_MIX:3 * W_MIX]
    x_lora = xm[:, 3 * W_MIX:3 * W_MIX + LANES]
    xg = xm[:, 3 * W_MIX + LANES:]
    logw = -_softplus(-(w0_ref[...] + _mm(jnp.tanh(x_lora), w2_ref[...]))) - 0.5
    log_decay = -jnp.exp(logw)
    a = _sigmoid(a0_ref[...] + _mm(x_lora, a2_ref[...]))
    g = _mm(_sigmoid(xg), g2_ref[...])
    bd = _head_block_diag()
    kk_raw = k * kk_ref[...]
    kk = kk_raw * lax.rsqrt(_mm_f32(kk_raw * kk_raw, bd) + L2_EPS)
    k2 = k * (1.0 + (a - 1.0) * ka_ref[...])
    bonus = _mm_f32(r * k2 * rk_ref[...], bd) * v
    scan_ref[0] = r
    scan_ref[1] = log_decay
    scan_ref[2] = k2
    scan_ref[3] = v
    scan_ref[4] = kk
    scan_ref[5] = kk * a
    post_ref[0] = g
    post_ref[1] = bonus


def _rwkv_prep(u, mu, w0, w2p, a0, a2p, g2, k_k, k_a, r_k, seq, tm):
    n = u.shape[0]
    ublk = U_RWKV // D_MODEL
    row = lambda c: pl.BlockSpec((1, c), lambda i: (0, 0))
    mat = lambda r: pl.BlockSpec((r, W_MIX), lambda i: (0, 0))
    return pl.pallas_call(
        functools.partial(_rwkv_prep_kernel, tiles_per_seq=seq // tm),
        grid=(n // tm,),
        in_specs=[
            pl.BlockSpec((tm, D_MODEL), lambda i: (i, ublk)),
            pl.BlockSpec((BF16_ROWS, D_MODEL), lambda i: (jnp.maximum(i * (tm // BF16_ROWS) - 1, 0), ublk)),
            row(D_MODEL), row(W_MIX), mat(LANES), row(W_MIX), mat(LANES), mat(LANES), row(W_MIX), row(W_MIX), row(W_MIX),
        ],
        out_specs=[
            pl.BlockSpec((6, tm, W_MIX), lambda i: (0, i, 0)),
            pl.BlockSpec((2, tm, W_MIX), lambda i: (0, i, 0)),
        ],
        out_shape=[jax.ShapeDtypeStruct((6, n, W_MIX), F32), jax.ShapeDtypeStruct((2, n, W_MIX), F32)],
        compiler_params=_cparams(("parallel",)),
        name="rwkv_prep",
    )(u, u, mu, w0, w2p, a0, a2p, g2, k_k, k_a, r_k)


def _rwkv_scan_kernel(x_ref, o_ref, s_scr, *, nchunk):
    @pl.when(pl.program_id(1) == 0)
    def _():
        s_scr[...] = jnp.zeros_like(s_scr)

    c = CHUNK
    strict, incl, diag = _tri_masks(c)
    lower = incl.astype(F32)
    eye = diag.astype(F32)

    def chunk(ci, carry):
        sl = pl.ds(pl.multiple_of(ci * c, c), c)
        for h in range(N_HEADS):
            r = x_ref[0, 0, h, sl, :]
            lw = x_ref[1, 0, h, sl, :]
            k = x_ref[2, 0, h, sl, :]
            v = x_ref[3, 0, h, sl, :]
            kk = x_ref[4, 0, h, sl, :]
            b = x_ref[5, 0, h, sl, :]
            cum = _mm_f32(lower, lw)
            a_t = -kk * jnp.exp(cum - lw)
            inv = jnp.exp(-cum)
            b_t = b * inv
            k_t = k * inv
            r_t = r * jnp.exp(cum)
            m_ab = jnp.where(strict, _mm_nt(a_t, b_t), 0.0)
            m_ak = jnp.where(strict, _mm_nt(a_t, k_t), 0.0)
            n_rb = jnp.where(incl, _mm_nt(r_t, b_t), 0.0)
            n_rk = jnp.where(incl, _mm_nt(r_t, k_t), 0.0)
            t_inv = _unit_lower_inverse(m_ab, eye)
            s = s_scr[h]
            u = _mm_f32(t_inv, _mm_nt(a_t, s) + _mm(m_ak, v))
            o = _mm_nt(r_t, s) + _mm(n_rb, u) + _mm(n_rk, v)
            to_end = jnp.exp(cum[c - 1:c, :] - cum)
            s_scr[h] = s * jnp.exp(cum[c - 1:c, :]) + _mm_tn(u, b * to_end) + _mm_tn(v, k * to_end)
            mean = jnp.mean(o, axis=-1, keepdims=True)
            var = jnp.mean((o - mean) ** 2, axis=-1, keepdims=True)
            o_ref[0, h, sl, :] = (o - mean) * lax.rsqrt(var + RWKV_GN_EPS)
        return carry

    lax.fori_loop(0, nchunk, chunk, 0)


def _rwkv_scan(xs, cblk):
    _, batch, _, seq, _ = xs.shape
    return pl.pallas_call(
        functools.partial(_rwkv_scan_kernel, nchunk=cblk // CHUNK),
        grid=(batch, seq // cblk),
        in_specs=[pl.BlockSpec((6, 1, N_HEADS, cblk, HEAD_DIM), lambda b, j: (0, b, 0, j, 0))],
        out_specs=pl.BlockSpec((1, N_HEADS, cblk, HEAD_DIM), lambda b, j: (b, 0, j, 0)),
        out_shape=jax.ShapeDtypeStruct((batch, N_HEADS, seq, HEAD_DIM), F32),
        scratch_shapes=[pltpu.VMEM((N_HEADS, HEAD_DIM, HEAD_DIM), F32)],
        compiler_params=_cparams(("parallel", "arbitrary")),
        name="rwkv_scan",
    )(xs)


def _gdn_prep_kernel(u_ref, up_ref, cw_ref, o_ref, ext_scr, *, tiles_per_seq):
    tm = u_ref.shape[0]
    c3 = 3 * W_MIX
    prev = up_ref[...].astype(F32)[:, :c3]
    ext_scr[0:BF16_ROWS, :] = jnp.where(pl.program_id(0) % tiles_per_seq == 0, 0.0, prev)
    ext_scr[BF16_ROWS:, :] = u_ref[...].astype(F32)[:, :c3]
    y = jnp.zeros((tm, c3), F32)
    for j in range(GDN_CONV):
        y = y + ext_scr[pl.ds(BF16_ROWS - (GDN_CONV - 1) + j, tm), :] * cw_ref[j:j + 1, :]
    y = _silu(y)
    bd = _head_block_diag()
    q = y[:, 0:W_MIX]
    k = y[:, W_MIX:2 * W_MIX]
    o_ref[0] = q * lax.rsqrt(_mm_f32(q * q, bd) + L2_EPS) * (HEAD_DIM ** -0.5)
    o_ref[1] = k * lax.rsqrt(_mm_f32(k * k, bd) + L2_EPS)
    o_ref[2] = y[:, 2 * W_MIX:]


def _gdn_prep(u, conv_w, seq, tm):
    n = u.shape[0]
    ublk = U_GDN // D_MODEL
    return pl.pallas_call(
        functools.partial(_gdn_prep_kernel, tiles_per_seq=seq // tm),
        grid=(n // tm,),
        in_specs=[
            pl.BlockSpec((tm, D_MODEL), lambda i: (i, ublk)),
            pl.BlockSpec((BF16_ROWS, D_MODEL), lambda i: (jnp.maximum(i * (tm // BF16_ROWS) - 1, 0), ublk)),
            pl.BlockSpec((GDN_CONV, 3 * W_MIX), lambda i: (0, 0)),
        ],
        out_specs=pl.BlockSpec((3, tm, W_MIX), lambda i: (0, i, 0)),
        out_shape=jax.ShapeDtypeStruct((3, n, W_MIX), F32),
        scratch_shapes=[pltpu.VMEM((tm + BF16_ROWS, 3 * W_MIX), F32)],
        compiler_params=_cparams(("parallel",)),
        name="gdn_prep",
    )(u, u, conv_w)


def _gdn_scan_kernel(x_ref, col_ref, row_ref, nw_ref, o_ref, s_scr, *, nchunk):
    @pl.when(pl.program_id(1) == 0)
    def _():
        s_scr[...] = jnp.zeros_like(s_scr)

    c = CHUNK
    strict, incl, diag = _tri_masks(c)
    lower = incl.astype(F32)
    upper = jnp.logical_not(strict).astype(F32)
    eye = diag.astype(F32)

    def chunk(ci, carry):
        sl = pl.ds(pl.multiple_of(ci * c, c), c)
        col = col_ref[sl, :]
        row = row_ref[ci]
        gam_col = _mm_f32(lower, col)
        gam_row = _mm_f32(row, upper)
        for h in range(N_HEADS):
            q = x_ref[0, 0, h, sl, :]
            k = x_ref[1, 0, h, sl, :]
            v = x_ref[2, 0, h, sl, :]
            beta = col[:, SM_BETA + h:SM_BETA + h + 1]
            gc = gam_col[:, SM_DEC + h:SM_DEC + h + 1]
            gr = gam_row[SM_DEC + h:SM_DEC + h + 1, :]
            decay = jnp.exp(jnp.where(incl, gc - gr, -jnp.inf))
            a_mat = jnp.where(strict, beta * decay * _mm_nt(k, k), 0.0)
            t_inv = _unit_lower_inverse(-a_mat, eye)
            u0 = _mm_f32(t_inv, beta * v)
            wm = _mm_f32(t_inv, (beta * jnp.exp(gc)) * k)
            qk = _mm_nt(q, k) * decay
            s = s_scr[h]
            uc = u0 - _mm(wm, s)
            o = jnp.exp(gc) * _mm(q, s) + _mm(qk, uc)
            gl = gc[c - 1:c, :]
            s_scr[h] = jnp.exp(gl) * s + _mm_tn(k * jnp.exp(gl - gc), uc)
            o_ref[0, h, sl, :] = _rms(o, nw_ref[...], NORM_EPS)
        return carry

    lax.fori_loop(0, nchunk, chunk, 0)


def _gdn_scan(xs, col, row, norm_w, cblk):
    _, batch, _, seq, _ = xs.shape
    nb = seq // cblk
    row = row.reshape(BF16_ROWS, batch * seq // CHUNK, CHUNK).transpose(1, 0, 2)
    return pl.pallas_call(
        functools.partial(_gdn_scan_kernel, nchunk=cblk // CHUNK),
        grid=(batch, nb),
        in_specs=[
            pl.BlockSpec((3, 1, N_HEADS, cblk, HEAD_DIM), lambda b, j: (0, b, 0, j, 0)),
            pl.BlockSpec((cblk, LANES), lambda b, j: (b * nb + j, 0)),
            pl.BlockSpec((cblk // CHUNK, BF16_ROWS, CHUNK), lambda b, j: (b * nb + j, 0, 0)),
            pl.BlockSpec((1, HEAD_DIM), lambda b, j: (0, 0)),
        ],
        out_specs=pl.BlockSpec((1, N_HEADS, cblk, HEAD_DIM), lambda b, j: (b, 0, j, 0)),
        out_shape=jax.ShapeDtypeStruct((batch, N_HEADS, seq, HEAD_DIM), F32),
        scratch_shapes=[pltpu.VMEM((N_HEADS, HEAD_DIM, HEAD_DIM), F32)],
        compiler_params=_cparams(("parallel", "arbitrary")),
        name="gdn_scan",
    )(xs, col, row, norm_w)


def _causal(s, qi, kj, tq, tk):
    rows = qi * tq + lax.broadcasted_iota(jnp.int32, s.shape, 0)
    cols = kj * tk + lax.broadcasted_iota(jnp.int32, s.shape, 1)
    return jnp.where(rows >= cols, s, -jnp.inf)


def _online_softmax_step(s, v, m_scr, l_scr, acc_scr):
    m_old = m_scr[...]
    m_new = jnp.maximum(m_old, jnp.max(s, axis=-1, keepdims=True))
    alpha = jnp.exp(m_old - m_new)
    p = jnp.exp(s - m_new)
    l_scr[...] = alpha * l_scr[...] + jnp.sum(p, axis=-1, keepdims=True)
    acc_scr[...] = alpha * acc_scr[...] + jnp.dot(p.astype(BF16), v, preferred_element_type=F32)
    m_scr[...] = m_new


def _fox_kernel(q_ref, k_ref, v_ref, ccol_ref, crow_ref, o_ref, m_scr, l_scr, acc_scr, cq_scr):
    h = pl.program_id(0) % N_HEADS
    qi = pl.program_id(1)
    kj = pl.program_id(2)
    tq, tk = q_ref.shape[1], k_ref.shape[1]

    @pl.when(kj == 0)
    def _():
        m_scr[...] = jnp.full_like(m_scr, -jnp.inf)
        l_scr[...] = jnp.zeros_like(l_scr)
        acc_scr[...] = jnp.zeros_like(acc_scr)
        blk = ccol_ref[...]
        lane = lax.broadcasted_iota(jnp.int32, blk.shape, 1)
        cq_scr[...] = jnp.sum(jnp.where(lane == SM_FOX + h, blk, 0.0), axis=-1, keepdims=True)

    @pl.when(kj <= qi)
    def _():
        blk = crow_ref[...]
        sub = lax.broadcasted_iota(jnp.int32, blk.shape, 0)
        ck = jnp.sum(jnp.where(sub == SM_FOX + h, blk, 0.0), axis=0, keepdims=True)
        q = q_ref[0] * (HEAD_DIM ** -0.5)
        s = lax.dot_general(q, k_ref[0], (((1,), (1,)), ((), ())), preferred_element_type=F32)
        s = _causal(s + (cq_scr[...] - ck), qi, kj, tq, tk)
        _online_softmax_step(s, v_ref[0], m_scr, l_scr, acc_scr)

    @pl.when(kj == qi)
    def _():
        o_ref[0] = acc_scr[...] / l_scr[...]


def _fox_attention(q, k, v, ccol, crow, seq, tq):
    bh = q.shape[0]
    nq = seq // tq
    kv_map = lambda g, i, j: (g, jnp.minimum(i, j), 0)
    return pl.pallas_call(
        _fox_kernel,
        grid=(bh, nq, nq),
        in_specs=[
            pl.BlockSpec((1, tq, HEAD_DIM), lambda g, i, j: (g, i, 0)),
            pl.BlockSpec((1, tq, HEAD_DIM), kv_map),
            pl.BlockSpec((1, tq, HEAD_DIM), kv_map),
            pl.BlockSpec((tq, LANES), lambda g, i, j: ((g // N_HEADS) * nq + i, 0)),
            pl.BlockSpec((BF16_ROWS, tq), lambda g, i, j: (0, (g // N_HEADS) * nq + jnp.minimum(i, j))),
        ],
        out_specs=pl.BlockSpec((1, tq, HEAD_DIM), lambda g, i, j: (g, i, 0)),
        out_shape=jax.ShapeDtypeStruct((bh, seq, HEAD_DIM), F32),
        scratch_shapes=[pltpu.VMEM((tq, 1), F32), pltpu.VMEM((tq, 1), F32), pltpu.VMEM((tq, HEAD_DIM), F32),
                        pltpu.VMEM((tq, 1), F32)],
        compiler_params=_cparams(("parallel", "parallel", "arbitrary")),
        name="fox_attention",
    )(q, k, v, ccol, crow)


def _diff_kernel(q_ref, k_ref, v_ref, cq_ref, sq_ref, ck_ref, sk_ref, rot_ref, lam_ref, ln_ref, o_ref,
                 q0_scr, q1_scr, m0_scr, l0_scr, acc0_scr, m1_scr, l1_scr, acc1_scr, *, lam_init):
    qi = pl.program_id(1)
    kj = pl.program_id(2)
    tq, tk = q_ref.shape[1], k_ref.shape[1]

    def rope(x, cos, sin):
        return x.astype(F32) * cos + jnp.dot(x, rot_ref[...], preferred_element_type=F32) * sin

    @pl.when(kj == 0)
    def _():
        for m_scr, l_scr, acc_scr in ((m0_scr, l0_scr, acc0_scr), (m1_scr, l1_scr, acc1_scr)):
            m_scr[...] = jnp.full_like(m_scr, -jnp.inf)
            l_scr[...] = jnp.zeros_like(l_scr)
            acc_scr[...] = jnp.zeros_like(acc_scr)
        q = rope(q_ref[0], cq_ref[...], sq_ref[...]) * (DIFF_DH ** -0.5)
        lane = lax.broadcasted_iota(jnp.int32, q.shape, 1)
        q0_scr[...] = jnp.where(lane < DIFF_DH, q, 0.0).astype(BF16)
        q1_scr[...] = jnp.where(lane >= DIFF_DH, q, 0.0).astype(BF16)

    @pl.when(kj <= qi)
    def _():
        k = rope(k_ref[0], ck_ref[...], sk_ref[...]).astype(BF16)
        v = v_ref[0]
        for q_scr, m_scr, l_scr, acc_scr in ((q0_scr, m0_scr, l0_scr, acc0_scr), (q1_scr, m1_scr, l1_scr, acc1_scr)):
            s = lax.dot_general(q_scr[...], k, (((1,), (1,)), ((), ())), preferred_element_type=F32)
            _online_softmax_step(_causal(s, qi, kj, tq, tk), v, m_scr, l_scr, acc_scr)

    @pl.when(kj == qi)
    def _():
        lp = lam_ref[...]
        lam = (jnp.exp(jnp.sum(lp[0:1] * lp[1:2], axis=-1, keepdims=True))
               - jnp.exp(jnp.sum(lp[2:3] * lp[3:4], axis=-1, keepdims=True)) + lam_init)
        o = acc0_scr[...] / l0_scr[...] - lam * (acc1_scr[...] / l1_scr[...])
        o_ref[0] = _rms(o, ln_ref[...], DIFF_LN_EPS) * (1.0 - lam_init)


def _diff_attention(q, k, v, cos, sin, rot, lam_p, subln, seq, tq, lam_init):
    bh = q.shape[0]
    nq = seq // tq
    kv_map = lambda g, i, j: (g, jnp.minimum(i, j), 0)
    tab_q = pl.BlockSpec((tq, HEAD_DIM), lambda g, i, j: (i, 0))
    tab_k = pl.BlockSpec((tq, HEAD_DIM), lambda g, i, j: (jnp.minimum(i, j), 0))
    stat = lambda: pltpu.VMEM((tq, 1), F32)
    return pl.pallas_call(
        functools.partial(_diff_kernel, lam_init=lam_init),
        grid=(bh, nq, nq),
        in_specs=[
            pl.BlockSpec((1, tq, HEAD_DIM), lambda g, i, j: (g, i, 0)),
            pl.BlockSpec((1, tq, HEAD_DIM), kv_map),
            pl.BlockSpec((1, tq, HEAD_DIM), kv_map),
            tab_q, tab_q, tab_k, tab_k,
            pl.BlockSpec((HEAD_DIM, HEAD_DIM), lambda g, i, j: (0, 0)),
            pl.BlockSpec((4, DIFF_DH), lambda g, i, j: (0, 0)),
            pl.BlockSpec((1, HEAD_DIM), lambda g, i, j: (0, 0)),
        ],
        out_specs=pl.BlockSpec((1, tq, HEAD_DIM), lambda g, i, j: (g, i, 0)),
        out_shape=jax.ShapeDtypeStruct((bh, seq, HEAD_DIM), F32),
        scratch_shapes=[pltpu.VMEM((tq, HEAD_DIM), BF16), pltpu.VMEM((tq, HEAD_DIM), BF16),
                        stat(), stat(), pltpu.VMEM((tq, HEAD_DIM), F32),
                        stat(), stat(), pltpu.VMEM((tq, HEAD_DIM), F32)],
        compiler_params=_cparams(("parallel", "parallel", "arbitrary")),
        name="diff_attention",
    )(q, k, v, cos, sin, cos, sin, rot, lam_p, subln)


def _merge_kernel(x_ref, gate_ref, oa_ref, post_ref, yb_ref, yc_ref, od_ref, gd_ref, lnw_ref, lnb_ref,
                  wbo_ref, wout_ref, o_ref):
    y_a = (oa_ref[...] * lnw_ref[...] + lnb_ref[...] + post_ref[1]) * post_ref[0]
    y_d = od_ref[...] * _silu(gd_ref[...].astype(F32))
    acc = jnp.zeros(x_ref.shape, F32)
    for b, y in enumerate((y_a, yb_ref[...], yc_ref[...], y_d)):
        gate = _sigmoid(gate_ref[:, b * D_MODEL:(b + 1) * D_MODEL].astype(F32))
        acc = acc + gate * _mm(y, wbo_ref[b])
    o_ref[...] = x_ref[...] + _mm(acc, wout_ref[...])


def _merge(x, u, o_a, post, y_b, y_c, o_d, ln_w, ln_b, w_bo, w_out, tm):
    n = x.shape[0]
    tok = lambda c: pl.BlockSpec((tm, c), lambda i: (i, 0))
    return pl.pallas_call(
        _merge_kernel,
        grid=(n // tm,),
        in_specs=[
            tok(D_MODEL),
            pl.BlockSpec((tm, 4 * D_MODEL), lambda i: (i, U_GATE // (4 * D_MODEL))),
            tok(W_MIX),
            pl.BlockSpec((2, tm, W_MIX), lambda i: (0, i, 0)),
            tok(W_MIX), tok(W_MIX), tok(W_MIX),
            pl.BlockSpec((tm, W_MIX), lambda i: (i, (U_GDN + 3 * W_MIX) // W_MIX)),
            pl.BlockSpec((1, W_MIX), lambda i: (0, 0)),
            pl.BlockSpec((1, W_MIX), lambda i: (0, 0)),
            pl.BlockSpec((4, W_MIX, D_MODEL), lambda i: (0, 0, 0)),
            pl.BlockSpec((D_MODEL, D_MODEL), lambda i: (0, 0)),
        ],
        out_specs=tok(D_MODEL),
        out_shape=jax.ShapeDtypeStruct((n, D_MODEL), F32),
        compiler_params=_cparams(("parallel",)),
        name="merge",
    )(x, u, o_a, post, y_b, y_c, o_d, u, ln_w, ln_b, w_bo, w_out)


def _ffn_kernel(x_ref, g_ref, wg_ref, wu_ref, wd_ref, o_ref, h_scr, acc_scr):
    f = pl.program_id(1)

    @pl.when(f == 0)
    def _():
        h_scr[...] = _rms(x_ref[...], g_ref[...], NORM_EPS).astype(BF16)
        acc_scr[...] = jnp.zeros_like(acc_scr)

    h = h_scr[...]
    act = _silu(jnp.dot(h, wg_ref[...], preferred_element_type=F32)) * jnp.dot(h, wu_ref[...], preferred_element_type=F32)
    acc_scr[...] += _mm(act, wd_ref[...])

    @pl.when(f == pl.num_programs(1) - 1)
    def _():
        o_ref[...] = x_ref[...] + acc_scr[...]


def _ffn(x, g, wg, wu, wd, tm, tf):
    n = x.shape[0]
    ff = wg.shape[1]
    return pl.pallas_call(
        _ffn_kernel,
        grid=(n // tm, ff // tf),
        in_specs=[
            pl.BlockSpec((tm, D_MODEL), lambda i, f: (i, 0)),
            pl.BlockSpec((1, D_MODEL), lambda i, f: (0, 0)),
            pl.BlockSpec((D_MODEL, tf), lambda i, f: (0, f)),
            pl.BlockSpec((D_MODEL, tf), lambda i, f: (0, f)),
            pl.BlockSpec((tf, D_MODEL), lambda i, f: (f, 0)),
        ],
        out_specs=pl.BlockSpec((tm, D_MODEL), lambda i, f: (i, 0)),
        out_shape=jax.ShapeDtypeStruct((n, D_MODEL), F32),
        scratch_shapes=[pltpu.VMEM((tm, D_MODEL), BF16), pltpu.VMEM((tm, D_MODEL), F32)],
        compiler_params=_cparams(("parallel", "arbitrary")),
        name="ffn",
    )(x, g, wg, wu, wd)


def _moe_kernel(x_ref, g_ref, router_ref, wg_ref, wu_ref, wd_ref, o_ref, h_scr, c_scr, acc_scr):
    e = pl.program_id(1)
    f = pl.program_id(2)

    @pl.when(jnp.logical_and(e == 0, f == 0))
    def _():
        h = _rms(x_ref[...], g_ref[...], NORM_EPS)
        h_scr[...] = h.astype(BF16)
        acc_scr[...] = jnp.zeros_like(acc_scr)
        logits = _mm_f32(h, router_ref[...])
        lane = lax.broadcasted_iota(jnp.int32, logits.shape, 1).astype(F32)
        lg = jnp.where(lane < N_EXPERTS, logits, -jnp.inf)
        m1 = jnp.max(lg, axis=-1, keepdims=True)
        i1 = jnp.min(jnp.where(lg == m1, lane, float(LANES)), axis=-1, keepdims=True)
        lg2 = jnp.where(lane == i1, -jnp.inf, lg)
        m2 = jnp.max(lg2, axis=-1, keepdims=True)
        i2 = jnp.min(jnp.where(lg2 == m2, lane, float(LANES)), axis=-1, keepdims=True)
        e2 = jnp.exp(m2 - m1)
        c_scr[...] = jnp.where(lane == i1, 1.0 / (1.0 + e2), 0.0) + jnp.where(lane == i2, e2 / (1.0 + e2), 0.0)

    c = c_scr[...]
    lane = lax.broadcasted_iota(jnp.int32, c.shape, 1)
    ce = jnp.sum(jnp.where(lane == e, c, 0.0), axis=-1, keepdims=True)
    h = h_scr[...]
    act = _silu(jnp.dot(h, wg_ref[0], preferred_element_type=F32)) * jnp.dot(h, wu_ref[0], preferred_element_type=F32)
    acc_scr[...] += _mm(act * ce, wd_ref[0])

    @pl.when(jnp.logical_and(e == pl.num_programs(1) - 1, f == pl.num_programs(2) - 1))
    def _():
        o_ref[...] = x_ref[...] + acc_scr[...]


def _moe(x, g, router, wg, wu, wd, tm, tf):
    n = x.shape[0]
    ff = wg.shape[2]
    return pl.pallas_call(
        _moe_kernel,
        grid=(n // tm, N_EXPERTS, ff // tf),
        in_specs=[
            pl.BlockSpec((tm, D_MODEL), lambda i, e, f: (i, 0)),
            pl.BlockSpec((1, D_MODEL), lambda i, e, f: (0, 0)),
            pl.BlockSpec((D_MODEL, LANES), lambda i, e, f: (0, 0)),
            pl.BlockSpec((1, D_MODEL, tf), lambda i, e, f: (e, 0, f)),
            pl.BlockSpec((1, D_MODEL, tf), lambda i, e, f: (e, 0, f)),
            pl.BlockSpec((1, tf, D_MODEL), lambda i, e, f: (e, f, 0)),
        ],
        out_specs=pl.BlockSpec((tm, D_MODEL), lambda i, e, f: (i, 0)),
        out_shape=jax.ShapeDtypeStruct((n, D_MODEL), F32),
        scratch_shapes=[pltpu.VMEM((tm, D_MODEL), BF16), pltpu.VMEM((tm, LANES), F32), pltpu.VMEM((tm, D_MODEL), F32)],
        compiler_params=_cparams(("parallel", "arbitrary", "arbitrary")),
        name="moe",
    )(x, g, router, wg, wu, wd)


def _ple_kernel(x_ref, p_ref, g_ref, wgate_ref, wproj_ref, fin_ref, o_ref, *, final):
    x = x_ref[...]
    h = _rms(x, g_ref[...], NORM_EPS)
    y = x + _sigmoid(_mm(h, wgate_ref[...])) * _mm(p_ref[...], wproj_ref[...])
    o_ref[...] = _rms(y, fin_ref[...], NORM_EPS) if final else y


def _ple(x, p, g, wgate, wproj, fin, tm, final):
    n = x.shape[0]
    return pl.pallas_call(
        functools.partial(_ple_kernel, final=final),
        grid=(n // tm,),
        in_specs=[
            pl.BlockSpec((tm, D_MODEL), lambda i: (i, 0)),
            pl.BlockSpec((tm, P_DIM), lambda i: (i, 0)),
            pl.BlockSpec((1, D_MODEL), lambda i: (0, 0)),
            pl.BlockSpec((D_MODEL, D_MODEL), lambda i: (0, 0)),
            pl.BlockSpec((P_DIM, D_MODEL), lambda i: (0, 0)),
            pl.BlockSpec((1, D_MODEL), lambda i: (0, 0)),
        ],
        out_specs=pl.BlockSpec((tm, D_MODEL), lambda i: (i, 0)),
        out_shape=jax.ShapeDtypeStruct((n, D_MODEL), F32),
        compiler_params=_cparams(("parallel",)),
        name="ple",
    )(x, p, g, wgate, wproj, fin)


def _tiles(seq):
    return dict(tm=min(512, seq), tq=min(512, seq), cblk=min(512, seq))


def _rope_tables(seq):
    half = ROPE_DIMS // 2
    inv = ROPE_THETA ** (-jnp.arange(half, dtype=F32) * 2.0 / ROPE_DIMS)
    ang = jnp.arange(seq, dtype=F32)[:, None] * inv[None, :]
    pad = jnp.zeros((seq, DIFF_DH - ROPE_DIMS), F32)
    cos = jnp.concatenate([jnp.cos(ang), jnp.cos(ang), pad + 1.0], axis=-1)
    sin = jnp.concatenate([jnp.sin(ang), jnp.sin(ang), pad], axis=-1)
    d = jnp.arange(HEAD_DIM)
    dd = d % DIFF_DH
    src = jnp.where(dd < half, d + half, d - half)
    sign = jnp.where(dd < half, -1.0, jnp.where(dd < ROPE_DIMS, 1.0, 0.0))
    rot = jnp.zeros((HEAD_DIM, HEAD_DIM), F32).at[src, d].set(sign)
    return jnp.tile(cos, (1, 2)), jnp.tile(sin, (1, 2)), rot.astype(BF16)


def _to_heads(t, batch, seq):
    lead = t.shape[:-2]
    t = t.reshape(lead + (batch, seq, N_HEADS, HEAD_DIM))
    nl = len(lead)
    return t.transpose(tuple(range(nl)) + (nl, nl + 2, nl + 1, nl + 3))


def _from_heads(t, batch, seq):
    return t.transpose(0, 2, 1, 3).reshape(batch * seq, N_HEADS * HEAD_DIM)


def _split_w_in(w):
    a0 = 0
    b0 = a0 + 4 * W_MIX
    c0 = b0 + 3 * W_MIX
    d0 = c0 + 3 * W_MIX + N_HEADS
    g0 = d0 + 4 * W_MIX + 2 * N_HEADS
    d_small = d0 + 3 * W_MIX
    main = jnp.concatenate([
        w[:, g0:], w[:, a0:b0], w[:, d0:d_small], w[:, d_small + 2 * N_HEADS:g0], w[:, b0:c0], w[:, c0:c0 + 3 * W_MIX],
    ], axis=1).astype(BF16)
    small = jnp.concatenate([
        w[:, c0 + 3 * W_MIX:d0], w[:, d_small:d_small + 2 * N_HEADS],
        jnp.zeros((D_MODEL, LANES - 3 * N_HEADS), w.dtype),
    ], axis=1).astype(BF16)
    return main, small, small[:, :BF16_ROWS].T


def _small_params(fbias, a_log, dt_bias):
    zeros = jnp.zeros((N_HEADS,), F32)
    bias = jnp.concatenate([fbias, zeros, dt_bias, jnp.zeros((LANES - 3 * N_HEADS,), F32)])
    neg_a = jnp.concatenate([zeros, zeros, -jnp.exp(a_log), jnp.zeros((LANES - 3 * N_HEADS,), F32)])
    return (bias.reshape(1, LANES), bias[:BF16_ROWS].reshape(BF16_ROWS, 1),
            neg_a.reshape(1, LANES), neg_a[:BF16_ROWS].reshape(BF16_ROWS, 1))


def _pad_rows(w, top, total):
    return jnp.concatenate([jnp.zeros((top, w.shape[1]), w.dtype), w,
                            jnp.zeros((total - top - w.shape[0], w.shape[1]), w.dtype)], axis=0)


def kernel(x, p, norm_mix, norm_ffn, norm_ple, w_in, w_bo, w_out, rwkv_mu, rwkv_w0, rwkv_w2, rwkv_a0, rwkv_a2, rwkv_g2, rwkv_kk, rwkv_ka, rwkv_rk, rwkv_ln_w, rwkv_ln_b, diff_lam, diff_subln, fox_fbias, gdn_conv, gdn_a_log, gdn_dt_bias, gdn_norm, ffn_w_gate, ffn_w_up, ffn_w_down, moe_router, moe_w_gate, moe_w_up, moe_w_down, ple_proj, ple_gate, final_norm):
    batch, seq, _ = x.shape
    depth = w_in.shape[0]
    n = batch * seq
    t = _tiles(seq)
    tm, tq, cblk = t["tm"], t["tq"], t["cblk"]
    row = lambda v: v.reshape(1, -1).astype(F32)
    cos, sin, rot = _rope_tables(seq)
    xf = x.reshape(n, D_MODEL)
    pf = p.reshape(depth, n, P_DIM)

    for i in range(depth):
        w_main, w_small, w_small_t = _split_w_in(w_in[i])
        u, scol, srow = _inproj(xf, row(norm_mix[i]), w_main, w_small, w_small_t, tm, 512)
        bcol, brow, acol, arow = _small_params(fox_fbias[i], gdn_a_log[i], gdn_dt_bias[i])
        hcol, hrow = _small_prep(scol, srow, bcol, brow, acol, arow, batch, seq, tm)

        scan_in, post = _rwkv_prep(
            u, row(rwkv_mu[i]), row(rwkv_w0[i]), _pad_rows(rwkv_w2[i], 0, LANES), row(rwkv_a0[i]),
            _pad_rows(rwkv_a2[i], LANES // 2, LANES), rwkv_g2[i], row(rwkv_kk[i]), row(rwkv_ka[i]), row(rwkv_rk[i]),
            seq, tm)
        o_a = _from_heads(_rwkv_scan(_to_heads(scan_in, batch, seq), cblk), batch, seq)

        gdn_in = _gdn_prep(u, gdn_conv[i].T, seq, tm)
        o_d = _from_heads(_gdn_scan(_to_heads(gdn_in, batch, seq), hcol, hrow, row(gdn_norm[i]), cblk), batch, seq)

        def heads3(c0):
            qkv = u[:, c0:c0 + 3 * W_MIX].reshape(n, 3, W_MIX).transpose(1, 0, 2)
            return _to_heads(qkv, batch, seq).reshape(3, batch * N_HEADS, seq, HEAD_DIM)

        qb, kb, vb = heads3(U_DIFF)
        lam_init = 0.8 - 0.6 * math.exp(-0.3 * i)
        y_b = _diff_attention(qb, kb, vb, cos, sin, rot, diff_lam[i].astype(F32), row(diff_subln[i]), seq, tq, lam_init)
        y_b = _from_heads(y_b.reshape(batch, N_HEADS, seq, HEAD_DIM), batch, seq)
        qc, kc, vc = heads3(U_FOX)
        y_c = _fox_attention(qc, kc, vc, hcol, hrow, seq, tq)
        y_c = _from_heads(y_c.reshape(batch, N_HEADS, seq, HEAD_DIM), batch, seq)

        xf = _merge(xf, u, o_a, post, y_b, y_c, o_d, row(rwkv_ln_w[i]), row(rwkv_ln_b[i]),
                    w_bo[i].astype(BF16), w_out[i].astype(BF16), tm)

        j = i // 2
        if i % 2 == 0:
            xf = _ffn(xf, row(norm_ffn[i]), ffn_w_gate[j].astype(BF16), ffn_w_up[j].astype(BF16),
                      ffn_w_down[j].astype(BF16), tm, ffn_w_gate.shape[2] // 2)
        else:
            router = jnp.concatenate([moe_router[j], jnp.zeros((D_MODEL, LANES - N_EXPERTS), F32)], axis=1)
            xf = _moe(xf, row(norm_ffn[i]), router, moe_w_gate[j].astype(BF16), moe_w_up[j].astype(BF16),
                      moe_w_down[j].astype(BF16), tm, moe_w_gate.shape[3] // 2)
        xf = _ple(xf, pf[i], row(norm_ple[i]), ple_gate[i].astype(BF16), ple_proj[i].astype(BF16),
                  row(final_norm), tm, i == depth - 1)
    return xf.reshape(batch, seq, D_MODEL)
```

```python
import functools
import math

import jax
import jax.numpy as jnp
import numpy as np
from jax import lax
from jax.experimental import pallas as pl
from jax.experimental.pallas import tpu as pltpu

F32 = jnp.float32
BF16 = jnp.bfloat16
HIGHEST = lax.Precision.HIGHEST

D_MODEL = 1024
P_DIM = 256
W_MIX = 256
HEAD_DIM = 64
N_HEADS = 4
DIFF_DH = 32
ROPE_THETA = 500000.0
ROPE_DIMS = 8
RWKV_GN_EPS = 64e-5
DIFF_LN_EPS = 1e-5
GDN_CONV = 4
CHUNK = 64
N_EXPERTS = 8
NORM_EPS = 1e-6
L2_EPS = 1e-6
LOG2E = math.log2(math.e)
LANES = 128
BF16_ROWS = 16
assert CHUNK == HEAD_DIM

U_GATE = 0
U_RWKV = 4096
U_GDN = 5120
U_DIFF = 6144
U_FOX = 6912
U_COLS = 7680
SM_FOX, SM_BETA, SM_DEC = 0, 4, 8


def _cparams(semantics, vmem_mb=48):
    return pltpu.CompilerParams(dimension_semantics=semantics, vmem_limit_bytes=vmem_mb * 1024 * 1024)


def _mm(a, b):
    return jnp.dot(a.astype(BF16), b.astype(BF16), preferred_element_type=F32)


def _mm_nt(a, b):
    return lax.dot_general(a.astype(BF16), b.astype(BF16), (((1,), (1,)), ((), ())), preferred_element_type=F32)


def _mm_tn(a, b):
    return lax.dot_general(a.astype(BF16), b.astype(BF16), (((0,), (0,)), ((), ())), preferred_element_type=F32)


def _mm_f32(a, b):
    return jnp.dot(a, b, preferred_element_type=F32, precision=HIGHEST)


def _mm_mask(mask, x):
    hi = x.astype(BF16)
    r1 = x - hi.astype(F32)
    mid = r1.astype(BF16)
    lo = (r1 - mid.astype(F32)).astype(BF16)
    dot = lambda t: jnp.dot(mask, t, preferred_element_type=F32)
    return dot(hi) + dot(mid) + dot(lo)


def _rms(x, g, eps):
    return x * lax.rsqrt(jnp.mean(x * x, axis=-1, keepdims=True) + eps) * g


def _sigmoid(x):
    return 1.0 / (1.0 + jnp.exp(-x))


def _silu(x):
    return x * _sigmoid(x)


def _softplus(x):
    return jnp.maximum(x, 0.0) + jnp.log(1.0 + jnp.exp(-jnp.abs(x)))


def _tri_masks(c):
    ii = lax.broadcasted_iota(jnp.int32, (c, c), 0)
    jj = lax.broadcasted_iota(jnp.int32, (c, c), 1)
    return ii > jj, ii >= jj, ii == jj


def _block_masks():
    ii = lax.broadcasted_iota(jnp.int32, (W_MIX, W_MIX), 0)
    jj = lax.broadcasted_iota(jnp.int32, (W_MIX, W_MIX), 1)
    same = (ii // HEAD_DIM) == (jj // HEAD_DIM)
    return same, jnp.logical_and(same, ii > jj), jnp.logical_and(same, ii >= jj), ii == jj


def _unit_lower_inverse(n, eye):
    r = eye + n
    p = n
    for _ in range(int(math.log2(CHUNK)) - 1):
        p = _mm(p, p)
        r = r + _mm(r, p)
    return r


def _stack_heads(x, same):
    return jnp.where(same, jnp.concatenate([x, x, x, x], axis=0), jnp.zeros((), x.dtype))


def _unstack_heads(x):
    return x[0:CHUNK] + x[CHUNK:2 * CHUNK] + x[2 * CHUNK:3 * CHUNK] + x[3 * CHUNK:4 * CHUNK]


def _inproj_kernel(x_ref, g_ref, w_ref, ws_ref, wst_ref, wvt_ref, u_ref, scol_ref, srow_ref, vt_ref, h_scr):
    @pl.when(pl.program_id(1) == 0)
    def _():
        hb = _rms(x_ref[...], g_ref[...], NORM_EPS).astype(BF16)
        h_scr[...] = hb
        nt = (((1,), (1,)), ((), ()))
        scol_ref[...] = jnp.dot(hb, ws_ref[...], preferred_element_type=F32)
        srow_ref[...] = lax.dot_general(wst_ref[...], hb, nt, preferred_element_type=F32)
        vt_ref[...] = lax.dot_general(wvt_ref[...], hb, nt, preferred_element_type=F32).astype(BF16)

    u_ref[...] = jnp.dot(h_scr[...], w_ref[...], preferred_element_type=F32).astype(BF16)


def _inproj(x, g, w, ws, wst, wvt, tm, tn):
    n = x.shape[0]
    return pl.pallas_call(
        _inproj_kernel,
        grid=(n // tm, U_COLS // tn),
        in_specs=[
            pl.BlockSpec((tm, D_MODEL), lambda i, j: (i, 0)),
            pl.BlockSpec((1, D_MODEL), lambda i, j: (0, 0)),
            pl.BlockSpec((D_MODEL, tn), lambda i, j: (0, j)),
            pl.BlockSpec((D_MODEL, LANES), lambda i, j: (0, 0)),
            pl.BlockSpec((BF16_ROWS, D_MODEL), lambda i, j: (0, 0)),
            pl.BlockSpec((2 * W_MIX, D_MODEL), lambda i, j: (0, 0)),
        ],
        out_specs=[
            pl.BlockSpec((tm, tn), lambda i, j: (i, j)),
            pl.BlockSpec((tm, LANES), lambda i, j: (i, 0)),
            pl.BlockSpec((BF16_ROWS, tm), lambda i, j: (0, i)),
            pl.BlockSpec((2 * W_MIX, tm), lambda i, j: (0, i)),
        ],
        out_shape=[
            jax.ShapeDtypeStruct((n, U_COLS), BF16),
            jax.ShapeDtypeStruct((n, LANES), F32),
            jax.ShapeDtypeStruct((BF16_ROWS, n), F32),
            jax.ShapeDtypeStruct((2 * W_MIX, n), BF16),
        ],
        scratch_shapes=[pltpu.VMEM((tm, D_MODEL), BF16)],
        compiler_params=_cparams(("parallel", "arbitrary")),
        name="inproj",
    )(x, g, w, ws, wst, wvt)


def _small_prep_kernel(scol_ref, srow_ref, bcol_ref, brow_ref, acol_ref, arow_ref, ocol_ref, orow_ref, orep_ref,
                       ccol_scr, crow_scr):
    @pl.when(pl.program_id(1) == 0)
    def _():
        ccol_scr[...] = jnp.zeros_like(ccol_scr)
        crow_scr[...] = jnp.zeros_like(crow_scr)

    tm = scol_ref.shape[0]
    strict, incl, _ = _tri_masks(tm)
    lower = incl.astype(BF16)
    upper = jnp.logical_not(strict).astype(F32)

    def funcs(z, idx, neg_a):
        logf = jnp.minimum(z, 0.0) - jnp.log(1.0 + jnp.exp(-jnp.abs(z)))
        beta = _sigmoid(z)
        dec = neg_a * _softplus(z)
        is_f = idx < SM_BETA
        is_b = jnp.logical_and(idx >= SM_BETA, idx < SM_DEC)
        is_d = jnp.logical_and(idx >= SM_DEC, idx < SM_DEC + N_HEADS)
        return jnp.where(is_f, logf, 0.0), jnp.where(is_b, beta, jnp.where(is_d, dec, 0.0)), is_f

    zc = scol_ref[...] + bcol_ref[...]
    lane = lax.broadcasted_iota(jnp.int32, zc.shape, 1)
    logf_c, rest_c, is_f_c = funcs(zc, lane, acol_ref[...])
    cum_c = _mm_mask(lower, logf_c) + ccol_scr[...]
    ccol_scr[...] = cum_c[tm - 1:tm, :]
    ocol_ref[...] = rest_c
    for h in range(N_HEADS):
        orep_ref[:, h * LANES:(h + 1) * LANES] = jnp.broadcast_to(
            cum_c[:, SM_FOX + h:SM_FOX + h + 1] * LOG2E, (tm, LANES))

    zr = srow_ref[...] + brow_ref[...]
    sub = lax.broadcasted_iota(jnp.int32, zr.shape, 0)
    logf_r, _, _ = funcs(zr, sub, arow_ref[...])
    cum_r = _mm_f32(logf_r, upper) + crow_scr[...]
    crow_scr[...] = cum_r[:, tm - 1:tm]
    orow_ref[...] = cum_r * LOG2E


def _small_prep(scol, srow, bcol, brow, acol, arow, batch, seq, tm):
    n = batch * seq
    nt = seq // tm
    return pl.pallas_call(
        _small_prep_kernel,
        grid=(batch, nt),
        in_specs=[
            pl.BlockSpec((tm, LANES), lambda b, j: (b * nt + j, 0)),
            pl.BlockSpec((BF16_ROWS, tm), lambda b, j: (0, b * nt + j)),
            pl.BlockSpec((1, LANES), lambda b, j: (0, 0)),
            pl.BlockSpec((BF16_ROWS, 1), lambda b, j: (0, 0)),
            pl.BlockSpec((1, LANES), lambda b, j: (0, 0)),
            pl.BlockSpec((BF16_ROWS, 1), lambda b, j: (0, 0)),
        ],
        out_specs=[
            pl.BlockSpec((tm, LANES), lambda b, j: (b * nt + j, 0)),
            pl.BlockSpec((BF16_ROWS, tm), lambda b, j: (0, b * nt + j)),
            pl.BlockSpec((tm, N_HEADS * LANES), lambda b, j: (b * nt + j, 0)),
        ],
        out_shape=[jax.ShapeDtypeStruct((n, LANES), F32), jax.ShapeDtypeStruct((BF16_ROWS, n), F32),
                   jax.ShapeDtypeStruct((n, N_HEADS * LANES), F32)],
        scratch_shapes=[pltpu.VMEM((1, LANES), F32), pltpu.VMEM((BF16_ROWS, 1), F32)],
        compiler_params=_cparams(("parallel", "arbitrary")),
        name="small_prep",
    )(scol, srow, bcol, brow, acol, arow)


def _rwkv_prep_kernel(u_ref, up_ref, mu_ref, w0_ref, w2_ref, a0_ref, a2_ref, g2_ref, kk_ref, ka_ref, rk_ref,
                      scan_ref, pc_ref, post_ref, *, tiles_per_seq):
    tm = u_ref.shape[0]
    u = u_ref[...].astype(F32)
    prev = up_ref[...].astype(F32)[BF16_ROWS - 1:BF16_ROWS, :]
    prev = jnp.where(pl.program_id(0) % tiles_per_seq == 0, 0.0, prev)
    rows = lax.broadcasted_iota(jnp.int32, (tm, 1), 0)
    u_prev = jnp.where(rows == 0, prev, pltpu.roll(u, 1, 0))
    xm = u + (u_prev - u) * mu_ref[...]
    r = xm[:, 0:W_MIX]
    k = xm[:, W_MIX:2 * W_MIX]
    v = xm[:, 2 * W_MIX:3 * W_MIX]
    x_lora = xm[:, 3 * W_MIX:3 * W_MIX + LANES]
    xg = xm[:, 3 * W_MIX + LANES:]
    logw = -_softplus(-(w0_ref[...] + _mm(jnp.tanh(x_lora), w2_ref[...]))) - 0.5
    log_decay = -jnp.exp(logw)
    a = _sigmoid(a0_ref[...] + _mm(x_lora, a2_ref[...]))
    g = _mm(_sigmoid(xg), g2_ref[...])
    same = _block_masks()[0].astype(F32)
    kk_raw = k * kk_ref[...]
    kk = kk_raw * lax.rsqrt(_mm_f32(kk_raw * kk_raw, same) + L2_EPS)
    k2 = k * (1.0 + (a - 1.0) * ka_ref[...])
    bonus = _mm_f32(r * k2 * rk_ref[...], same) * v
    ti = lax.broadcasted_iota(jnp.int32, (tm, tm), 0)
    tj = lax.broadcasted_iota(jnp.int32, (tm, tm), 1)
    in_chunk = jnp.logical_and(ti // CHUNK == tj // CHUNK, ti >= tj).astype(BF16)
    ci = lax.broadcasted_iota(jnp.int32, (tm // CHUNK, tm), 0)
    cj = lax.broadcasted_iota(jnp.int32, (tm // CHUNK, tm), 1)
    cum = _mm_mask(in_chunk, log_decay)
    cum_end = _mm_mask((ci == cj // CHUNK).astype(BF16), log_decay)
    inv = jnp.exp(-cum)
    scan_ref[0] = (-kk * jnp.exp(cum - log_decay)).astype(BF16)
    scan_ref[1] = (kk * a * inv).astype(BF16)
    scan_ref[2] = (k2 * inv).astype(BF16)
    scan_ref[3] = (r * jnp.exp(cum)).astype(BF16)
    scan_ref[4] = v.astype(BF16)
    pc_ref[...] = jnp.exp(cum_end)
    post_ref[0] = g
    post_ref[1] = bonus


def _rwkv_prep(u, mu, w0, w2p, a0, a2p, g2, k_k, k_a, r_k, seq, tm):
    n = u.shape[0]
    ublk = U_RWKV // D_MODEL
    row = lambda c: pl.BlockSpec((1, c), lambda i: (0, 0))
    mat = lambda r: pl.BlockSpec((r, W_MIX), lambda i: (0, 0))
    return pl.pallas_call(
        functools.partial(_rwkv_prep_kernel, tiles_per_seq=seq // tm),
        grid=(n // tm,),
        in_specs=[
            pl.BlockSpec((tm, D_MODEL), lambda i: (i, ublk)),
            pl.BlockSpec((BF16_ROWS, D_MODEL), lambda i: (jnp.maximum(i * (tm // BF16_ROWS) - 1, 0), ublk)),
            row(D_MODEL), row(W_MIX), mat(LANES), row(W_MIX), mat(LANES), mat(LANES), row(W_MIX), row(W_MIX), row(W_MIX),
        ],
        out_specs=[
            pl.BlockSpec((5, tm, W_MIX), lambda i: (0, i, 0)),
            pl.BlockSpec((tm // CHUNK, W_MIX), lambda i: (i, 0)),
            pl.BlockSpec((2, tm, W_MIX), lambda i: (0, i, 0)),
        ],
        out_shape=[jax.ShapeDtypeStruct((5, n, W_MIX), BF16), jax.ShapeDtypeStruct((n // CHUNK, W_MIX), F32),
                   jax.ShapeDtypeStruct((2, n, W_MIX), F32)],
        compiler_params=_cparams(("parallel",)),
        name="rwkv_prep",
    )(u, u, mu, w0, w2p, a0, a2p, g2, k_k, k_a, r_k)


def _rwkv_chunk_kernel(x_ref, pc_ref, o_ref, s_scr, *, nchunk, nbatch):
    @pl.when(pl.program_id(0) == 0)
    def _():
        s_scr[...] = jnp.zeros_like(s_scr)

    same, strict, incl, diag = _block_masks()
    eye = diag.astype(F32)

    def chunk(ci, carry):
        sl = pl.ds(pl.multiple_of(ci * CHUNK, CHUNK), CHUNK)
        for b in range(nbatch):
            a_s, b_s, k_s, r_s, v_s = (_stack_heads(x_ref[i, b, sl, :], same) for i in range(5))
            pc = pc_ref[b, pl.ds(ci, 1), :]
            m_ab = jnp.where(strict, _mm_nt(a_s, b_s), 0.0)
            m_ak = jnp.where(strict, _mm_nt(a_s, k_s), 0.0)
            n_rb = jnp.where(incl, _mm_nt(r_s, b_s), 0.0)
            n_rk = jnp.where(incl, _mm_nt(r_s, k_s), 0.0)
            t_inv = _unit_lower_inverse(m_ab, eye)
            a2 = _mm(t_inv, a_s)
            u0 = _mm(t_inv, _mm(m_ak, v_s))
            r2 = r_s.astype(F32) + _mm(n_rb, a2)
            o0 = _mm(n_rb, u0) + _mm(n_rk, v_s)
            b_end = b_s.astype(F32) * pc
            k_end = k_s.astype(F32) * pc
            s = s_scr[b]
            o = _mm_nt(r2, s) + o0
            s_scr[b] = _mm(s, eye * pc + _mm_tn(a2, b_end)) + _mm_tn(u0, b_end) + _mm_tn(v_s, k_end)
            mean = jnp.sum(o, axis=-1, keepdims=True) * (1.0 / HEAD_DIM)
            cen = jnp.where(same, o - mean, 0.0)
            var = jnp.sum(cen * cen, axis=-1, keepdims=True) * (1.0 / HEAD_DIM)
            o_ref[b, sl, :] = _unstack_heads(cen * lax.rsqrt(var + RWKV_GN_EPS))
        return carry

    lax.fori_loop(0, nchunk, chunk, 0)


def _rwkv_chunk(xs, pc, batch, seq, cblk):
    xs = xs.reshape(5, batch, seq, W_MIX)
    pc = pc.reshape(batch, seq // CHUNK, W_MIX)
    out = pl.pallas_call(
        functools.partial(_rwkv_chunk_kernel, nchunk=cblk // CHUNK, nbatch=batch),
        grid=(seq // cblk,),
        in_specs=[
            pl.BlockSpec((5, batch, cblk, W_MIX), lambda j: (0, 0, j, 0)),
            pl.BlockSpec((batch, cblk // CHUNK, W_MIX), lambda j: (0, j, 0)),
        ],
        out_specs=pl.BlockSpec((batch, cblk, W_MIX), lambda j: (0, j, 0)),
        out_shape=jax.ShapeDtypeStruct((batch, seq, W_MIX), F32),
        scratch_shapes=[pltpu.VMEM((batch, W_MIX, W_MIX), F32)],
        compiler_params=_cparams(("arbitrary",)),
        name="rwkv_chunk",
    )(xs, pc)
    return out.reshape(batch * seq, W_MIX)


def _gdn_prep_kernel(u_ref, up_ref, cw_ref, o_ref, ext_scr, *, tiles_per_seq):
    tm = u_ref.shape[0]
    c3 = 3 * W_MIX
    prev = up_ref[...].astype(F32)[:, :c3]
    ext_scr[0:BF16_ROWS, :] = jnp.where(pl.program_id(0) % tiles_per_seq == 0, 0.0, prev)
    ext_scr[BF16_ROWS:, :] = u_ref[...].astype(F32)[:, :c3]
    y = jnp.zeros((tm, c3), F32)
    for j in range(GDN_CONV):
        y = y + ext_scr[pl.ds(BF16_ROWS - (GDN_CONV - 1) + j, tm), :] * cw_ref[j:j + 1, :]
    y = _silu(y)
    same = _block_masks()[0].astype(F32)
    q = y[:, 0:W_MIX]
    k = y[:, W_MIX:2 * W_MIX]
    o_ref[0] = (q * lax.rsqrt(_mm_f32(q * q, same) + L2_EPS) * (HEAD_DIM ** -0.5)).astype(BF16)
    o_ref[1] = (k * lax.rsqrt(_mm_f32(k * k, same) + L2_EPS)).astype(BF16)
    o_ref[2] = y[:, 2 * W_MIX:].astype(BF16)


def _gdn_prep(u, conv_w, seq, tm):
    n = u.shape[0]
    ublk = U_GDN // D_MODEL
    return pl.pallas_call(
        functools.partial(_gdn_prep_kernel, tiles_per_seq=seq // tm),
        grid=(n // tm,),
        in_specs=[
            pl.BlockSpec((tm, D_MODEL), lambda i: (i, ublk)),
            pl.BlockSpec((BF16_ROWS, D_MODEL), lambda i: (jnp.maximum(i * (tm // BF16_ROWS) - 1, 0), ublk)),
            pl.BlockSpec((GDN_CONV, 3 * W_MIX), lambda i: (0, 0)),
        ],
        out_specs=pl.BlockSpec((3, tm, W_MIX), lambda i: (0, i, 0)),
        out_shape=jax.ShapeDtypeStruct((3, n, W_MIX), BF16),
        scratch_shapes=[pltpu.VMEM((tm + BF16_ROWS, 3 * W_MIX), F32)],
        compiler_params=_cparams(("parallel",)),
        name="gdn_prep",
    )(u, u, conv_w)


def _gdn_chunk_kernel(x_ref, col_ref, nw_ref, o_ref, s_scr, *, nchunk, nbatch):
    @pl.when(pl.program_id(0) == 0)
    def _():
        s_scr[...] = jnp.zeros_like(s_scr)

    same, strict, incl, diag = _block_masks()
    eye = diag.astype(F32)
    lower = incl.astype(BF16)
    ones = same.astype(BF16)

    def head_col(col, lane0):
        return jnp.concatenate([col[:, lane0 + h:lane0 + h + 1] for h in range(N_HEADS)], axis=0)

    def chunk(ci, carry):
        sl = pl.ds(pl.multiple_of(ci * CHUNK, CHUNK), CHUNK)
        for b in range(nbatch):
            q_s, k_s, v_s = (_stack_heads(x_ref[i, b, sl, :], same) for i in range(3))
            col = col_ref[b, sl, :]
            beta = head_col(col, SM_BETA)
            g = head_col(col, SM_DEC)
            g_wide = jnp.broadcast_to(g, (W_MIX, LANES))
            gam = _mm_mask(lower, g_wide)[:, 0:1]
            gam_end = _mm_mask(ones, g_wide)[:, 0:1]
            gdiff = _mm_mask(lower, jnp.where(strict, g, 0.0))
            decay = jnp.exp(jnp.where(incl, gdiff, -jnp.inf))
            a_mat = jnp.where(strict, beta * decay * _mm_nt(k_s, k_s), 0.0)
            t_inv = _unit_lower_inverse(-a_mat, eye)
            e_gam = jnp.exp(gam)
            u0 = _mm(t_inv, beta * v_s.astype(F32))
            wm = _mm(t_inv, (beta * e_gam) * k_s.astype(F32))
            qk = _mm_nt(q_s, k_s) * decay
            q2 = e_gam * q_s.astype(F32) - _mm(qk, wm)
            o0 = _mm(qk, u0)
            k_end = k_s.astype(F32) * jnp.exp(gam_end - gam)
            s = s_scr[b]
            o = _mm(q2, s) + o0
            s_scr[b] = _mm(eye * jnp.exp(gam_end) - _mm_tn(k_end, wm), s) + _mm_tn(k_end, u0)
            ms = jnp.sum(o * o, axis=-1, keepdims=True) * (1.0 / HEAD_DIM)
            o_ref[b, sl, :] = _unstack_heads(o * lax.rsqrt(ms + NORM_EPS)) * nw_ref[...]
        return carry

    lax.fori_loop(0, nchunk, chunk, 0)


def _gdn_chunk(xs, col, norm_w, batch, seq, cblk):
    xs = xs.reshape(3, batch, seq, W_MIX)
    col = col.reshape(batch, seq, LANES)
    out = pl.pallas_call(
        functools.partial(_gdn_chunk_kernel, nchunk=cblk // CHUNK, nbatch=batch),
        grid=(seq // cblk,),
        in_specs=[
            pl.BlockSpec((3, batch, cblk, W_MIX), lambda j: (0, 0, j, 0)),
            pl.BlockSpec((batch, cblk, LANES), lambda j: (0, j, 0)),
            pl.BlockSpec((1, W_MIX), lambda j: (0, 0)),
        ],
        out_specs=pl.BlockSpec((batch, cblk, W_MIX), lambda j: (0, j, 0)),
        out_shape=jax.ShapeDtypeStruct((batch, seq, W_MIX), F32),
        scratch_shapes=[pltpu.VMEM((batch, W_MIX, W_MIX), F32)],
        compiler_params=_cparams(("arbitrary",)),
        name="gdn_chunk",
    )(xs, col, norm_w)
    return out.reshape(batch * seq, W_MIX)


def _causal_pairs(nq):
    pairs = [(i, j) for i in range(nq) for j in range(i + 1)]
    return jnp.asarray(np.array([p[0] for p in pairs], np.int32)), jnp.asarray(np.array([p[1] for p in pairs], np.int32))


def _softmax_update(s, vt, m_ref, l_ref, acc_ref):
    m_old = m_ref[...]
    m_new = jnp.maximum(m_old, jnp.max(s, axis=0, keepdims=True))
    alpha = jnp.exp2(m_old - m_new)
    p = jnp.exp2(s - m_new)
    l_ref[...] = alpha * l_ref[...] + jnp.sum(p, axis=0, keepdims=True)
    acc_ref[...] = alpha * acc_ref[...] + jnp.dot(vt, p.astype(BF16), preferred_element_type=F32)
    m_ref[...] = m_new


def _pair_lanes(h):
    p = h // 2
    return slice(p * LANES, (p + 1) * LANES), h % 2 == 0


def _key_after_query(tq):
    return lax.broadcasted_iota(jnp.int32, (tq, tq), 0) > lax.broadcasted_iota(jnp.int32, (tq, tq), 1)


def _fox_kernel(qi_ref, kj_ref, q_ref, k_ref, vt_ref, crow_ref, crep_ref, o_ref, qm_scr, m_scr, l_scr, acc_scr):
    t = pl.program_id(1)
    qi = qi_ref[t]
    kj = kj_ref[t]
    tq = q_ref.shape[0]

    @pl.when(kj == 0)
    def _():
        m_scr[...] = jnp.full_like(m_scr, -jnp.inf)
        l_scr[...] = jnp.zeros_like(l_scr)
        acc_scr[...] = jnp.zeros_like(acc_scr)
        q = q_ref[...]
        lane = lax.broadcasted_iota(jnp.int32, (tq, LANES), 1)
        for h in range(N_HEADS):
            slab, low = _pair_lanes(h)
            mine = lane < HEAD_DIM if low else lane >= HEAD_DIM
            qm_scr[h] = jnp.where(mine, q[:, slab], jnp.zeros((), BF16))

    def step(diagonal):
        k = k_ref[...]
        vt = vt_ref[...]
        cq = crow_ref[...]
        if diagonal:
            masked = _key_after_query(tq)
        for h in range(N_HEADS):
            slab, _ = _pair_lanes(h)
            s = lax.dot_general(k[:, slab], qm_scr[h], (((1,), (1,)), ((), ())), preferred_element_type=F32)
            ck = crep_ref[:, h * LANES:(h + 1) * LANES]
            s = (s - jnp.concatenate([ck] * (tq // LANES), axis=1)) + cq[SM_FOX + h:SM_FOX + h + 1, :]
            if diagonal:
                s = jnp.where(masked, -jnp.inf, s)
            _softmax_update(s, vt[slab, :], m_scr.at[h], l_scr.at[h], acc_scr.at[h])

    @pl.when(kj < qi)
    def _():
        step(False)

    @pl.when(kj == qi)
    def _():
        step(True)
        row = lax.broadcasted_iota(jnp.int32, (LANES, tq), 0)
        for p in range(N_HEADS // 2):
            lo = acc_scr[2 * p] / l_scr[2 * p]
            hi = acc_scr[2 * p + 1] / l_scr[2 * p + 1]
            o_ref[:, p * LANES:(p + 1) * LANES] = jnp.where(row < HEAD_DIM, lo, hi).T


def _attn_specs(nq, tq, ucol, vt_rows):
    cb = ucol // W_MIX
    q_spec = pl.BlockSpec((tq, W_MIX), lambda b, t, qi, kj: (b * nq + qi[t], cb))
    k_spec = pl.BlockSpec((tq, W_MIX), lambda b, t, qi, kj: (b * nq + kj[t], cb + 1))
    vt_spec = pl.BlockSpec((W_MIX, tq), lambda b, t, qi, kj: (vt_rows // W_MIX, b * nq + kj[t]))
    return q_spec, k_spec, vt_spec


def _fox_attention(u, vt, crow, crep, batch, seq, tq):
    nq = seq // tq
    qi, kj = _causal_pairs(nq)
    q_spec, k_spec, vt_spec = _attn_specs(nq, tq, U_FOX, W_MIX)
    stat = lambda: pltpu.VMEM((N_HEADS, 1, tq), F32)
    return pl.pallas_call(
        _fox_kernel,
        grid_spec=pltpu.PrefetchScalarGridSpec(
            num_scalar_prefetch=2,
            grid=(batch, qi.shape[0]),
            in_specs=[
                q_spec, k_spec, vt_spec,
                pl.BlockSpec((BF16_ROWS, tq), lambda b, t, qi, kj: (0, b * nq + qi[t])),
                pl.BlockSpec((tq, N_HEADS * LANES), lambda b, t, qi, kj: (b * nq + kj[t], 0)),
            ],
            out_specs=pl.BlockSpec((tq, W_MIX), lambda b, t, qi, kj: (b * nq + qi[t], 0)),
            scratch_shapes=[pltpu.VMEM((N_HEADS, tq, LANES), BF16), stat(), stat(),
                            pltpu.VMEM((N_HEADS, LANES, tq), F32)],
        ),
        out_shape=jax.ShapeDtypeStruct((batch * seq, W_MIX), F32),
        compiler_params=_cparams(("parallel", "arbitrary")),
        name="fox_attention",
    )(qi, kj, u, u, vt, crow, crep)


def _diff_kernel(qi_ref, kj_ref, q_ref, k_ref, vt_ref, cq_ref, sq_ref, ck_ref, sk_ref, rot_ref, lam_ref, ln_ref, o_ref,
                 qm_scr, m_scr, l_scr, acc_scr, *, lam_init):
    t = pl.program_id(1)
    qi = qi_ref[t]
    kj = kj_ref[t]
    tq = q_ref.shape[0]
    lane = lax.broadcasted_iota(jnp.int32, (tq, LANES), 1)

    def rope(x, cos, sin):
        return x.astype(F32) * cos + jnp.dot(x, rot_ref[...], preferred_element_type=F32) * sin

    @pl.when(kj == 0)
    def _():
        m_scr[...] = jnp.full_like(m_scr, -jnp.inf)
        l_scr[...] = jnp.zeros_like(l_scr)
        acc_scr[...] = jnp.zeros_like(acc_scr)
        q = rope(q_ref[...], cq_ref[...], sq_ref[...])
        for h in range(N_HEADS):
            slab, low = _pair_lanes(h)
            base = 0 if low else HEAD_DIM
            for c in range(2):
                lo = base + c * DIFF_DH
                sel = jnp.logical_and(lane >= lo, lane < lo + DIFF_DH)
                qm_scr[2 * h + c] = jnp.where(sel, q[:, slab], 0.0).astype(BF16)

    def step(diagonal):
        k = rope(k_ref[...], ck_ref[...], sk_ref[...]).astype(BF16)
        vt = vt_ref[...]
        if diagonal:
            masked = _key_after_query(tq)
        for h in range(N_HEADS):
            slab, _ = _pair_lanes(h)
            for c in range(2):
                i = 2 * h + c
                s = lax.dot_general(k[:, slab], qm_scr[i], (((1,), (1,)), ((), ())), preferred_element_type=F32)
                if diagonal:
                    s = jnp.where(masked, -jnp.inf, s)
                _softmax_update(s, vt[slab, :], m_scr.at[i], l_scr.at[i], acc_scr.at[i])

    @pl.when(kj < qi)
    def _():
        step(False)

    @pl.when(kj == qi)
    def _():
        step(True)
        lp = lam_ref[...]
        lam = (jnp.exp(jnp.sum(lp[0:1] * lp[1:2], axis=-1, keepdims=True))
               - jnp.exp(jnp.sum(lp[2:3] * lp[3:4], axis=-1, keepdims=True)) + lam_init)
        head = lambda h: acc_scr[2 * h] / l_scr[2 * h] - lam * (acc_scr[2 * h + 1] / l_scr[2 * h + 1])
        row = lax.broadcasted_iota(jnp.int32, (LANES, tq), 0)
        is_lo = lane < HEAD_DIM
        for p in range(N_HEADS // 2):
            o = jnp.where(row < HEAD_DIM, head(2 * p), head(2 * p + 1)).T
            sq = o * o
            ms_lo = jnp.sum(jnp.where(is_lo, sq, 0.0), axis=-1, keepdims=True)
            ms_hi = jnp.sum(jnp.where(is_lo, 0.0, sq), axis=-1, keepdims=True)
            ms = jnp.where(is_lo, ms_lo, ms_hi) * (1.0 / HEAD_DIM)
            o_ref[:, p * LANES:(p + 1) * LANES] = o * lax.rsqrt(ms + DIFF_LN_EPS) * ln_ref[...] * (1.0 - lam_init)


def _diff_attention(u, vt, cos, sin, rot, lam_p, subln, batch, seq, tq, lam_init):
    nq = seq // tq
    qi, kj = _causal_pairs(nq)
    q_spec, k_spec, vt_spec = _attn_specs(nq, tq, U_DIFF, 0)
    tab_q = pl.BlockSpec((tq, W_MIX), lambda b, t, qi, kj: (qi[t], 0))
    tab_k = pl.BlockSpec((tq, W_MIX), lambda b, t, qi, kj: (kj[t], 0))
    const = lambda r, c: pl.BlockSpec((r, c), lambda b, t, qi, kj: (0, 0))
    stat = lambda: pltpu.VMEM((2 * N_HEADS, 1, tq), F32)
    return pl.pallas_call(
        functools.partial(_diff_kernel, lam_init=lam_init),
        grid_spec=pltpu.PrefetchScalarGridSpec(
            num_scalar_prefetch=2,
            grid=(batch, qi.shape[0]),
            in_specs=[q_spec, k_spec, vt_spec, tab_q, tab_q, tab_k, tab_k,
                      const(W_MIX, W_MIX), const(4, DIFF_DH), const(1, LANES)],
            out_specs=pl.BlockSpec((tq, W_MIX), lambda b, t, qi, kj: (b * nq + qi[t], 0)),
            scratch_shapes=[pltpu.VMEM((2 * N_HEADS, tq, LANES), BF16), stat(), stat(),
                            pltpu.VMEM((2 * N_HEADS, LANES, tq), F32)],
        ),
        out_shape=jax.ShapeDtypeStruct((batch * seq, W_MIX), F32),
        compiler_params=_cparams(("parallel", "arbitrary")),
        name="diff_attention",
    )(qi, kj, u, u, vt, cos, sin, cos, sin, rot, lam_p, subln)


def _merge_kernel(x_ref, gate_ref, oa_ref, post_ref, yb_ref, yc_ref, od_ref, gd_ref, lnw_ref, lnb_ref,
                  wbo_ref, wout_ref, o_ref):
    y_a = (oa_ref[...] * lnw_ref[...] + lnb_ref[...] + post_ref[1]) * post_ref[0]
    y_d = od_ref[...] * _silu(gd_ref[...].astype(F32))
    acc = jnp.zeros(x_ref.shape, F32)
    for b, y in enumerate((y_a, yb_ref[...], yc_ref[...], y_d)):
        gate = _sigmoid(gate_ref[:, b * D_MODEL:(b + 1) * D_MODEL].astype(F32))
        acc = acc + gate * _mm(y, wbo_ref[b])
    o_ref[...] = x_ref[...] + _mm(acc, wout_ref[...])


def _merge(x, u, o_a, post, y_b, y_c, o_d, ln_w, ln_b, w_bo, w_out, tm):
    n = x.shape[0]
    tok = lambda c: pl.BlockSpec((tm, c), lambda i: (i, 0))
    return pl.pallas_call(
        _merge_kernel,
        grid=(n // tm,),
        in_specs=[
            tok(D_MODEL),
            pl.BlockSpec((tm, 4 * D_MODEL), lambda i: (i, U_GATE // (4 * D_MODEL))),
            tok(W_MIX),
            pl.BlockSpec((2, tm, W_MIX), lambda i: (0, i, 0)),
            tok(W_MIX), tok(W_MIX), tok(W_MIX),
            pl.BlockSpec((tm, W_MIX), lambda i: (i, (U_GDN + 3 * W_MIX) // W_MIX)),
            pl.BlockSpec((1, W_MIX), lambda i: (0, 0)),
            pl.BlockSpec((1, W_MIX), lambda i: (0, 0)),
            pl.BlockSpec((4, W_MIX, D_MODEL), lambda i: (0, 0, 0)),
            pl.BlockSpec((D_MODEL, D_MODEL), lambda i: (0, 0)),
        ],
        out_specs=tok(D_MODEL),
        out_shape=jax.ShapeDtypeStruct((n, D_MODEL), F32),
        compiler_params=_cparams(("parallel",)),
        name="merge",
    )(x, u, o_a, post, y_b, y_c, o_d, u, ln_w, ln_b, w_bo, w_out)


def _ffn_kernel(x_ref, g_ref, wg_ref, wu_ref, wd_ref, o_ref, h_scr, acc_scr):
    f = pl.program_id(1)

    @pl.when(f == 0)
    def _():
        h_scr[...] = _rms(x_ref[...], g_ref[...], NORM_EPS).astype(BF16)
        acc_scr[...] = jnp.zeros_like(acc_scr)

    h = h_scr[...]
    act = _silu(jnp.dot(h, wg_ref[...], preferred_element_type=F32)) * jnp.dot(h, wu_ref[...], preferred_element_type=F32)
    acc_scr[...] += _mm(act, wd_ref[...])

    @pl.when(f == pl.num_programs(1) - 1)
    def _():
        o_ref[...] = x_ref[...] + acc_scr[...]


def _ffn(x, g, wg, wu, wd, tm, tf):
    n = x.shape[0]
    ff = wg.shape[1]
    return pl.pallas_call(
        _ffn_kernel,
        grid=(n // tm, ff // tf),
        in_specs=[
            pl.BlockSpec((tm, D_MODEL), lambda i, f: (i, 0)),
            pl.BlockSpec((1, D_MODEL), lambda i, f: (0, 0)),
            pl.BlockSpec((D_MODEL, tf), lambda i, f: (0, f)),
            pl.BlockSpec((D_MODEL, tf), lambda i, f: (0, f)),
            pl.BlockSpec((tf, D_MODEL), lambda i, f: (f, 0)),
        ],
        out_specs=pl.BlockSpec((tm, D_MODEL), lambda i, f: (i, 0)),
        out_shape=jax.ShapeDtypeStruct((n, D_MODEL), F32),
        scratch_shapes=[pltpu.VMEM((tm, D_MODEL), BF16), pltpu.VMEM((tm, D_MODEL), F32)],
        compiler_params=_cparams(("parallel", "arbitrary")),
        name="ffn",
    )(x, g, wg, wu, wd)


def _moe_kernel(x_ref, g_ref, router_ref, wg_ref, wu_ref, wd_ref, o_ref, h_scr, c_scr, acc_scr):
    e = pl.program_id(1)
    f = pl.program_id(2)

    @pl.when(jnp.logical_and(e == 0, f == 0))
    def _():
        h = _rms(x_ref[...], g_ref[...], NORM_EPS)
        h_scr[...] = h.astype(BF16)
        acc_scr[...] = jnp.zeros_like(acc_scr)
        logits = _mm_f32(h, router_ref[...])
        lane = lax.broadcasted_iota(jnp.int32, logits.shape, 1).astype(F32)
        lg = jnp.where(lane < N_EXPERTS, logits, -jnp.inf)
        m1 = jnp.max(lg, axis=-1, keepdims=True)
        i1 = jnp.min(jnp.where(lg == m1, lane, float(LANES)), axis=-1, keepdims=True)
        lg2 = jnp.where(lane == i1, -jnp.inf, lg)
        m2 = jnp.max(lg2, axis=-1, keepdims=True)
        i2 = jnp.min(jnp.where(lg2 == m2, lane, float(LANES)), axis=-1, keepdims=True)
        e2 = jnp.exp(m2 - m1)
        c_scr[...] = jnp.where(lane == i1, 1.0 / (1.0 + e2), 0.0) + jnp.where(lane == i2, e2 / (1.0 + e2), 0.0)

    c = c_scr[...]
    lane = lax.broadcasted_iota(jnp.int32, c.shape, 1)
    ce = jnp.sum(jnp.where(lane == e, c, 0.0), axis=-1, keepdims=True)
    h = h_scr[...]
    act = _silu(jnp.dot(h, wg_ref[0], preferred_element_type=F32)) * jnp.dot(h, wu_ref[0], preferred_element_type=F32)
    acc_scr[...] += _mm(act * ce, wd_ref[0])

    @pl.when(jnp.logical_and(e == pl.num_programs(1) - 1, f == pl.num_programs(2) - 1))
    def _():
        o_ref[...] = x_ref[...] + acc_scr[...]


def _moe(x, g, router, wg, wu, wd, tm, tf):
    n = x.shape[0]
    ff = wg.shape[2]
    return pl.pallas_call(
        _moe_kernel,
        grid=(n // tm, N_EXPERTS, ff // tf),
        in_specs=[
            pl.BlockSpec((tm, D_MODEL), lambda i, e, f: (i, 0)),
            pl.BlockSpec((1, D_MODEL), lambda i, e, f: (0, 0)),
            pl.BlockSpec((D_MODEL, LANES), lambda i, e, f: (0, 0)),
            pl.BlockSpec((1, D_MODEL, tf), lambda i, e, f: (e, 0, f)),
            pl.BlockSpec((1, D_MODEL, tf), lambda i, e, f: (e, 0, f)),
            pl.BlockSpec((1, tf, D_MODEL), lambda i, e, f: (e, f, 0)),
        ],
        out_specs=pl.BlockSpec((tm, D_MODEL), lambda i, e, f: (i, 0)),
        out_shape=jax.ShapeDtypeStruct((n, D_MODEL), F32),
        scratch_shapes=[pltpu.VMEM((tm, D_MODEL), BF16), pltpu.VMEM((tm, LANES), F32), pltpu.VMEM((tm, D_MODEL), F32)],
        compiler_params=_cparams(("parallel", "arbitrary", "arbitrary"), vmem_mb=56),
        name="moe",
    )(x, g, router, wg, wu, wd)


def _ple_kernel(x_ref, p_ref, g_ref, wgate_ref, wproj_ref, fin_ref, o_ref, *, final):
    x = x_ref[...]
    h = _rms(x, g_ref[...], NORM_EPS)
    y = x + _sigmoid(_mm(h, wgate_ref[...])) * _mm(p_ref[...], wproj_ref[...])
    o_ref[...] = _rms(y, fin_ref[...], NORM_EPS) if final else y


def _ple(x, p, g, wgate, wproj, fin, tm, final):
    n = x.shape[0]
    return pl.pallas_call(
        functools.partial(_ple_kernel, final=final),
        grid=(n // tm,),
        in_specs=[
            pl.BlockSpec((tm, D_MODEL), lambda i: (i, 0)),
            pl.BlockSpec((tm, P_DIM), lambda i: (i, 0)),
            pl.BlockSpec((1, D_MODEL), lambda i: (0, 0)),
            pl.BlockSpec((D_MODEL, D_MODEL), lambda i: (0, 0)),
            pl.BlockSpec((P_DIM, D_MODEL), lambda i: (0, 0)),
            pl.BlockSpec((1, D_MODEL), lambda i: (0, 0)),
        ],
        out_specs=pl.BlockSpec((tm, D_MODEL), lambda i: (i, 0)),
        out_shape=jax.ShapeDtypeStruct((n, D_MODEL), F32),
        compiler_params=_cparams(("parallel",)),
        name="ple",
    )(x, p, g, wgate, wproj, fin)


def _tiles(n, seq):
    tm = min(512, seq)
    tm_in = 1024 if n % 1024 == 0 else tm
    return dict(tm=tm, tm_in=tm_in, tn_in=1536, tq=min(512, seq), cblk=min(512, seq))


def _rope_tables(seq):
    half = ROPE_DIMS // 2
    inv = ROPE_THETA ** (-jnp.arange(half, dtype=F32) * 2.0 / ROPE_DIMS)
    ang = jnp.arange(seq, dtype=F32)[:, None] * inv[None, :]
    pad = jnp.zeros((seq, DIFF_DH - ROPE_DIMS), F32)
    cos = jnp.concatenate([jnp.cos(ang), jnp.cos(ang), pad + 1.0], axis=-1)
    sin = jnp.concatenate([jnp.sin(ang), jnp.sin(ang), pad], axis=-1)
    d = jnp.arange(W_MIX)
    dd = d % DIFF_DH
    src = jnp.where(dd < half, d + half, d - half)
    sign = jnp.where(dd < half, -1.0, jnp.where(dd < ROPE_DIMS, 1.0, 0.0))
    rot = jnp.zeros((W_MIX, W_MIX), F32).at[src, d].set(sign)
    reps = W_MIX // DIFF_DH
    return jnp.tile(cos, (1, reps)), jnp.tile(sin, (1, reps)), rot.astype(BF16)


def _split_w_in(w):
    a0 = 0
    b0 = a0 + 4 * W_MIX
    c0 = b0 + 3 * W_MIX
    d0 = c0 + 3 * W_MIX + N_HEADS
    g0 = d0 + 4 * W_MIX + 2 * N_HEADS
    d_small = d0 + 3 * W_MIX
    diff_q = w[:, b0:b0 + W_MIX] * (DIFF_DH ** -0.5 * LOG2E)
    fox_q = w[:, c0:c0 + W_MIX] * (HEAD_DIM ** -0.5 * LOG2E)
    main = jnp.concatenate([
        w[:, g0:], w[:, a0:b0], w[:, d0:d_small], w[:, d_small + 2 * N_HEADS:g0],
        diff_q, w[:, b0 + W_MIX:c0], fox_q, w[:, c0 + W_MIX:c0 + 3 * W_MIX],
    ], axis=1).astype(BF16)
    small = jnp.concatenate([
        w[:, c0 + 3 * W_MIX:d0], w[:, d_small:d_small + 2 * N_HEADS],
        jnp.zeros((D_MODEL, LANES - 3 * N_HEADS), w.dtype),
    ], axis=1).astype(BF16)
    v_t = jnp.concatenate([w[:, b0 + 2 * W_MIX:c0], w[:, c0 + 2 * W_MIX:c0 + 3 * W_MIX]], axis=1).T.astype(BF16)
    return main, small, small[:, :BF16_ROWS].T, v_t


def _small_params(fbias, a_log, dt_bias):
    zeros = jnp.zeros((N_HEADS,), F32)
    bias = jnp.concatenate([fbias, zeros, dt_bias, jnp.zeros((LANES - 3 * N_HEADS,), F32)])
    neg_a = jnp.concatenate([zeros, zeros, -jnp.exp(a_log), jnp.zeros((LANES - 3 * N_HEADS,), F32)])
    return (bias.reshape(1, LANES), bias[:BF16_ROWS].reshape(BF16_ROWS, 1),
            neg_a.reshape(1, LANES), neg_a[:BF16_ROWS].reshape(BF16_ROWS, 1))


def _pad_rows(w, top, total):
    return jnp.concatenate([jnp.zeros((top, w.shape[1]), w.dtype), w,
                            jnp.zeros((total - top - w.shape[0], w.shape[1]), w.dtype)], axis=0)


def kernel(x, p, norm_mix, norm_ffn, norm_ple, w_in, w_bo, w_out, rwkv_mu, rwkv_w0, rwkv_w2, rwkv_a0, rwkv_a2, rwkv_g2, rwkv_kk, rwkv_ka, rwkv_rk, rwkv_ln_w, rwkv_ln_b, diff_lam, diff_subln, fox_fbias, gdn_conv, gdn_a_log, gdn_dt_bias, gdn_norm, ffn_w_gate, ffn_w_up, ffn_w_down, moe_router, moe_w_gate, moe_w_up, moe_w_down, ple_proj, ple_gate, final_norm):
    batch, seq, _ = x.shape
    depth = w_in.shape[0]
    n = batch * seq
    t = _tiles(n, seq)
    tm, tq, cblk = t["tm"], t["tq"], t["cblk"]
    row = lambda v: v.reshape(1, -1).astype(F32)
    cos, sin, rot = _rope_tables(seq)
    xf = x.reshape(n, D_MODEL)
    pf = p.reshape(depth, n, P_DIM)

    for i in range(depth):
        w_main, w_small, w_small_t, w_vt = _split_w_in(w_in[i])
        u, scol, srow, vt = _inproj(xf, row(norm_mix[i]), w_main, w_small, w_small_t, w_vt, t["tm_in"], t["tn_in"])
        bcol, brow, acol, arow = _small_params(fox_fbias[i], gdn_a_log[i], gdn_dt_bias[i])
        hcol, hrow, hrep = _small_prep(scol, srow, bcol, brow, acol, arow, batch, seq, tm)

        scan_in, pc, post = _rwkv_prep(
            u, row(rwkv_mu[i]), row(rwkv_w0[i]), _pad_rows(rwkv_w2[i], 0, LANES), row(rwkv_a0[i]),
            _pad_rows(rwkv_a2[i], LANES // 2, LANES), rwkv_g2[i], row(rwkv_kk[i]), row(rwkv_ka[i]), row(rwkv_rk[i]),
            seq, tm)
        o_a = _rwkv_chunk(scan_in, pc, batch, seq, cblk)

        o_d = _gdn_chunk(_gdn_prep(u, gdn_conv[i].T, seq, tm), hcol, jnp.tile(row(gdn_norm[i]), (1, N_HEADS)),
                         batch, seq, cblk)

        lam_init = 0.8 - 0.6 * math.exp(-0.3 * i)
        y_b = _diff_attention(u, vt, cos, sin, rot, diff_lam[i].astype(F32), jnp.tile(row(diff_subln[i]), (1, 2)),
                              batch, seq, tq, lam_init)
        y_c = _fox_attention(u, vt, hrow, hrep, batch, seq, tq)

        xf = _merge(xf, u, o_a, post, y_b, y_c, o_d, row(rwkv_ln_w[i]), row(rwkv_ln_b[i]),
                    w_bo[i].astype(BF16), w_out[i].astype(BF16), tm)

        j = i // 2
        if i % 2 == 0:
            xf = _ffn(xf, row(norm_ffn[i]), ffn_w_gate[j].astype(BF16), ffn_w_up[j].astype(BF16),
                      ffn_w_down[j].astype(BF16), tm, ffn_w_gate.shape[2] // 2)
        else:
            router = jnp.concatenate([moe_router[j], jnp.zeros((D_MODEL, LANES - N_EXPERTS), F32)], axis=1)
            xf = _moe(xf, row(norm_ffn[i]), router, moe_w_gate[j].astype(BF16), moe_w_up[j].astype(BF16),
                      moe_w_down[j].astype(BF16), tm, moe_w_gate.shape[3] // 2)
        xf = _ple(xf, pf[i], row(norm_ple[i]), ple_gate[i].astype(BF16), ple_proj[i].astype(BF16),
                  row(final_norm), tm, i == depth - 1)
    return xf.reshape(batch, seq, D_MODEL)
```

```python
import functools
import math

import jax
import jax.numpy as jnp
import numpy as np
from jax import lax
from jax.experimental import pallas as pl
from jax.experimental.pallas import tpu as pltpu

F32 = jnp.float32
BF16 = jnp.bfloat16
HIGHEST = lax.Precision.HIGHEST

D_MODEL = 1024
P_DIM = 256
W_MIX = 256
HEAD_DIM = 64
N_HEADS = 4
DIFF_DH = 32
ROPE_THETA = 500000.0
ROPE_DIMS = 8
RWKV_GN_EPS = 64e-5
DIFF_LN_EPS = 1e-5
GDN_CONV = 4
CHUNK = 64
N_EXPERTS = 8
NORM_EPS = 1e-6
L2_EPS = 1e-6
LOG2E = math.log2(math.e)
LANES = 128
BF16_ROWS = 16
assert CHUNK == HEAD_DIM

U_GATE = 0
U_RWKV = 4096
U_GDN = 5120
U_DIFF = 6144
U_FOX = 6912
U_COLS = 7680
SM_FOX, SM_BETA, SM_DEC = 0, 4, 8


def _cparams(semantics, vmem_mb=48):
    return pltpu.CompilerParams(dimension_semantics=semantics, vmem_limit_bytes=vmem_mb * 1024 * 1024)


def _mm(a, b):
    return jnp.dot(a.astype(BF16), b.astype(BF16), preferred_element_type=F32)


def _mm_nt(a, b):
    return lax.dot_general(a.astype(BF16), b.astype(BF16), (((1,), (1,)), ((), ())), preferred_element_type=F32)


def _mm_tn(a, b):
    return lax.dot_general(a.astype(BF16), b.astype(BF16), (((0,), (0,)), ((), ())), preferred_element_type=F32)


def _mm_f32(a, b):
    return jnp.dot(a, b, preferred_element_type=F32, precision=HIGHEST)


def _mm_mask(mask, x):
    hi = x.astype(BF16)
    r1 = x - hi.astype(F32)
    mid = r1.astype(BF16)
    lo = (r1 - mid.astype(F32)).astype(BF16)
    dot = lambda t: jnp.dot(mask, t, preferred_element_type=F32)
    return dot(hi) + dot(mid) + dot(lo)


def _rms(x, g, eps):
    return x * lax.rsqrt(jnp.mean(x * x, axis=-1, keepdims=True) + eps) * g


def _sigmoid(x):
    return 1.0 / (1.0 + jnp.exp(-x))


def _silu(x):
    return x * _sigmoid(x)


def _softplus(x):
    return jnp.maximum(x, 0.0) + jnp.log(1.0 + jnp.exp(-jnp.abs(x)))


def _tri_masks(c):
    ii = lax.broadcasted_iota(jnp.int32, (c, c), 0)
    jj = lax.broadcasted_iota(jnp.int32, (c, c), 1)
    return ii > jj, ii >= jj, ii == jj


def _block_masks():
    ii = lax.broadcasted_iota(jnp.int32, (W_MIX, W_MIX), 0)
    jj = lax.broadcasted_iota(jnp.int32, (W_MIX, W_MIX), 1)
    same = (ii // HEAD_DIM) == (jj // HEAD_DIM)
    return same, jnp.logical_and(same, ii > jj), jnp.logical_and(same, ii >= jj), ii == jj


def _unit_lower_inverse(n, eye):
    r = eye + n
    p = n
    for _ in range(int(math.log2(CHUNK)) - 1):
        p = _mm(p, p)
        r = r + _mm(r, p)
    return r


def _stack_heads(x, same):
    return jnp.where(same, jnp.concatenate([x, x, x, x], axis=0), jnp.zeros((), x.dtype))


def _unstack_heads(x):
    return x[0:CHUNK] + x[CHUNK:2 * CHUNK] + x[2 * CHUNK:3 * CHUNK] + x[3 * CHUNK:4 * CHUNK]


def _inproj_kernel(x_ref, g_ref, w_ref, ws_ref, wst_ref, wvt_ref, u_ref, scol_ref, srow_ref, vt_ref, h_scr):
    @pl.when(pl.program_id(1) == 0)
    def _():
        hb = _rms(x_ref[...], g_ref[...], NORM_EPS).astype(BF16)
        h_scr[...] = hb
        nt = (((1,), (1,)), ((), ()))
        scol_ref[...] = jnp.dot(hb, ws_ref[...], preferred_element_type=F32)
        srow_ref[...] = lax.dot_general(wst_ref[...], hb, nt, preferred_element_type=F32)
        vt_ref[...] = lax.dot_general(wvt_ref[...], hb, nt, preferred_element_type=F32).astype(BF16)

    u_ref[...] = jnp.dot(h_scr[...], w_ref[...], preferred_element_type=F32).astype(BF16)


def _inproj(x, g, w, ws, wst, wvt, tm, tn):
    n = x.shape[0]
    return pl.pallas_call(
        _inproj_kernel,
        grid=(n // tm, U_COLS // tn),
        in_specs=[
            pl.BlockSpec((tm, D_MODEL), lambda i, j: (i, 0)),
            pl.BlockSpec((1, D_MODEL), lambda i, j: (0, 0)),
            pl.BlockSpec((D_MODEL, tn), lambda i, j: (0, j)),
            pl.BlockSpec((D_MODEL, LANES), lambda i, j: (0, 0)),
            pl.BlockSpec((BF16_ROWS, D_MODEL), lambda i, j: (0, 0)),
            pl.BlockSpec((2 * W_MIX, D_MODEL), lambda i, j: (0, 0)),
        ],
        out_specs=[
            pl.BlockSpec((tm, tn), lambda i, j: (i, j)),
            pl.BlockSpec((tm, LANES), lambda i, j: (i, 0)),
            pl.BlockSpec((BF16_ROWS, tm), lambda i, j: (0, i)),
            pl.BlockSpec((2 * W_MIX, tm), lambda i, j: (0, i)),
        ],
        out_shape=[
            jax.ShapeDtypeStruct((n, U_COLS), BF16),
            jax.ShapeDtypeStruct((n, LANES), F32),
            jax.ShapeDtypeStruct((BF16_ROWS, n), F32),
            jax.ShapeDtypeStruct((2 * W_MIX, n), BF16),
        ],
        scratch_shapes=[pltpu.VMEM((tm, D_MODEL), BF16)],
        compiler_params=_cparams(("parallel", "arbitrary")),
        name="inproj",
    )(x, g, w, ws, wst, wvt)


def _small_prep_kernel(scol_ref, srow_ref, bcol_ref, brow_ref, acol_ref, arow_ref, ocol_ref, orow_ref, orep_ref,
                       ccol_scr, crow_scr):
    @pl.when(pl.program_id(1) == 0)
    def _():
        ccol_scr[...] = jnp.zeros_like(ccol_scr)
        crow_scr[...] = jnp.zeros_like(crow_scr)

    tm = scol_ref.shape[0]
    strict, incl, _ = _tri_masks(tm)
    lower = incl.astype(BF16)
    upper = jnp.logical_not(strict).astype(F32)

    def funcs(z, idx, neg_a):
        logf = jnp.minimum(z, 0.0) - jnp.log(1.0 + jnp.exp(-jnp.abs(z)))
        beta = _sigmoid(z)
        dec = neg_a * _softplus(z)
        is_f = idx < SM_BETA
        is_b = jnp.logical_and(idx >= SM_BETA, idx < SM_DEC)
        is_d = jnp.logical_and(idx >= SM_DEC, idx < SM_DEC + N_HEADS)
        return jnp.where(is_f, logf, 0.0), jnp.where(is_b, beta, jnp.where(is_d, dec, 0.0)), is_f

    zc = scol_ref[...] + bcol_ref[...]
    lane = lax.broadcasted_iota(jnp.int32, zc.shape, 1)
    logf_c, rest_c, is_f_c = funcs(zc, lane, acol_ref[...])
    cum_c = _mm_mask(lower, logf_c) + ccol_scr[...]
    ccol_scr[...] = cum_c[tm - 1:tm, :]
    ocol_ref[...] = rest_c
    for h in range(N_HEADS):
        orep_ref[:, h * LANES:(h + 1) * LANES] = jnp.broadcast_to(
            cum_c[:, SM_FOX + h:SM_FOX + h + 1] * LOG2E, (tm, LANES))

    zr = srow_ref[...] + brow_ref[...]
    sub = lax.broadcasted_iota(jnp.int32, zr.shape, 0)
    logf_r, _, _ = funcs(zr, sub, arow_ref[...])
    cum_r = _mm_f32(logf_r, upper) + crow_scr[...]
    crow_scr[...] = cum_r[:, tm - 1:tm]
    orow_ref[...] = cum_r * LOG2E


def _small_prep(scol, srow, bcol, brow, acol, arow, batch, seq, tm):
    n = batch * seq
    nt = seq // tm
    return pl.pallas_call(
        _small_prep_kernel,
        grid=(batch, nt),
        in_specs=[
            pl.BlockSpec((tm, LANES), lambda b, j: (b * nt + j, 0)),
            pl.BlockSpec((BF16_ROWS, tm), lambda b, j: (0, b * nt + j)),
            pl.BlockSpec((1, LANES), lambda b, j: (0, 0)),
            pl.BlockSpec((BF16_ROWS, 1), lambda b, j: (0, 0)),
            pl.BlockSpec((1, LANES), lambda b, j: (0, 0)),
            pl.BlockSpec((BF16_ROWS, 1), lambda b, j: (0, 0)),
        ],
        out_specs=[
            pl.BlockSpec((tm, LANES), lambda b, j: (b * nt + j, 0)),
            pl.BlockSpec((BF16_ROWS, tm), lambda b, j: (0, b * nt + j)),
            pl.BlockSpec((tm, N_HEADS * LANES), lambda b, j: (b * nt + j, 0)),
        ],
        out_shape=[jax.ShapeDtypeStruct((n, LANES), F32), jax.ShapeDtypeStruct((BF16_ROWS, n), F32),
                   jax.ShapeDtypeStruct((n, N_HEADS * LANES), F32)],
        scratch_shapes=[pltpu.VMEM((1, LANES), F32), pltpu.VMEM((BF16_ROWS, 1), F32)],
        compiler_params=_cparams(("parallel", "arbitrary")),
        name="small_prep",
    )(scol, srow, bcol, brow, acol, arow)


def _rwkv_prep_kernel(u_ref, up_ref, mu_ref, w0_ref, w2_ref, a0_ref, a2_ref, g2_ref, kk_ref, ka_ref, rk_ref,
                      scan_ref, pc_ref, post_ref, *, tiles_per_seq):
    tm = u_ref.shape[0]
    u = u_ref[...].astype(F32)
    prev = up_ref[...].astype(F32)[BF16_ROWS - 1:BF16_ROWS, :]
    prev = jnp.where(pl.program_id(0) % tiles_per_seq == 0, 0.0, prev)
    rows = lax.broadcasted_iota(jnp.int32, (tm, 1), 0)
    u_prev = jnp.where(rows == 0, prev, pltpu.roll(u, 1, 0))
    xm = u + (u_prev - u) * mu_ref[...]
    r = xm[:, 0:W_MIX]
    k = xm[:, W_MIX:2 * W_MIX]
    v = xm[:, 2 * W_MIX:3 * W_MIX]
    x_lora = xm[:, 3 * W_MIX:3 * W_MIX + LANES]
    xg = xm[:, 3 * W_MIX + LANES:]
    logw = -_softplus(-(w0_ref[...] + _mm(jnp.tanh(x_lora), w2_ref[...]))) - 0.5
    log_decay = -jnp.exp(logw)
    a = _sigmoid(a0_ref[...] + _mm(x_lora, a2_ref[...]))
    g = _mm(_sigmoid(xg), g2_ref[...])
    same = _block_masks()[0].astype(F32)
    kk_raw = k * kk_ref[...]
    kk = kk_raw * lax.rsqrt(_mm_f32(kk_raw * kk_raw, same) + L2_EPS)
    k2 = k * (1.0 + (a - 1.0) * ka_ref[...])
    bonus = _mm_f32(r * k2 * rk_ref[...], same) * v
    ti = lax.broadcasted_iota(jnp.int32, (tm, tm), 0)
    tj = lax.broadcasted_iota(jnp.int32, (tm, tm), 1)
    in_chunk = jnp.logical_and(ti // CHUNK == tj // CHUNK, ti >= tj).astype(BF16)
    ci = lax.broadcasted_iota(jnp.int32, (tm // CHUNK, tm), 0)
    cj = lax.broadcasted_iota(jnp.int32, (tm // CHUNK, tm), 1)
    cum = _mm_mask(in_chunk, log_decay)
    cum_end = _mm_mask((ci == cj // CHUNK).astype(BF16), log_decay)
    inv = jnp.exp(-cum)
    scan_ref[0] = (-kk * jnp.exp(cum - log_decay)).astype(BF16)
    scan_ref[1] = (kk * a * inv).astype(BF16)
    scan_ref[2] = (k2 * inv).astype(BF16)
    scan_ref[3] = (r * jnp.exp(cum)).astype(BF16)
    scan_ref[4] = v.astype(BF16)
    pc_ref[...] = jnp.exp(cum_end)
    post_ref[0] = g
    post_ref[1] = bonus


def _rwkv_prep(u, mu, w0, w2p, a0, a2p, g2, k_k, k_a, r_k, seq, tm):
    n = u.shape[0]
    ublk = U_RWKV // D_MODEL
    row = lambda c: pl.BlockSpec((1, c), lambda i: (0, 0))
    mat = lambda r: pl.BlockSpec((r, W_MIX), lambda i: (0, 0))
    return pl.pallas_call(
        functools.partial(_rwkv_prep_kernel, tiles_per_seq=seq // tm),
        grid=(n // tm,),
        in_specs=[
            pl.BlockSpec((tm, D_MODEL), lambda i: (i, ublk)),
            pl.BlockSpec((BF16_ROWS, D_MODEL), lambda i: (jnp.maximum(i * (tm // BF16_ROWS) - 1, 0), ublk)),
            row(D_MODEL), row(W_MIX), mat(LANES), row(W_MIX), mat(LANES), mat(LANES), row(W_MIX), row(W_MIX), row(W_MIX),
        ],
        out_specs=[
            pl.BlockSpec((5, tm, W_MIX), lambda i: (0, i, 0)),
            pl.BlockSpec((tm // CHUNK, W_MIX), lambda i: (i, 0)),
            pl.BlockSpec((2, tm, W_MIX), lambda i: (0, i, 0)),
        ],
        out_shape=[jax.ShapeDtypeStruct((5, n, W_MIX), BF16), jax.ShapeDtypeStruct((n // CHUNK, W_MIX), F32),
                   jax.ShapeDtypeStruct((2, n, W_MIX), F32)],
        compiler_params=_cparams(("parallel",)),
        name="rwkv_prep",
    )(u, u, mu, w0, w2p, a0, a2p, g2, k_k, k_a, r_k)


def _rwkv_chunk_kernel(x_ref, pc_ref, o_ref, s_scr, *, nchunk, nbatch):
    @pl.when(pl.program_id(0) == 0)
    def _():
        s_scr[...] = jnp.zeros_like(s_scr)

    same, strict, incl, diag = _block_masks()
    eye = diag.astype(F32)

    def chunk(ci, carry):
        sl = pl.ds(pl.multiple_of(ci * CHUNK, CHUNK), CHUNK)
        for b in range(nbatch):
            a_s, b_s, k_s, r_s, v_s = (_stack_heads(x_ref[i, b, sl, :], same) for i in range(5))
            pc = pc_ref[b, pl.ds(ci, 1), :]
            m_ab = jnp.where(strict, _mm_nt(a_s, b_s), 0.0)
            m_ak = jnp.where(strict, _mm_nt(a_s, k_s), 0.0)
            n_rb = jnp.where(incl, _mm_nt(r_s, b_s), 0.0)
            n_rk = jnp.where(incl, _mm_nt(r_s, k_s), 0.0)
            t_inv = _unit_lower_inverse(m_ab, eye)
            a2 = _mm(t_inv, a_s)
            u0 = _mm(t_inv, _mm(m_ak, v_s))
            r2 = r_s.astype(F32) + _mm(n_rb, a2)
            o0 = _mm(n_rb, u0) + _mm(n_rk, v_s)
            b_end = b_s.astype(F32) * pc
            k_end = k_s.astype(F32) * pc
            s = s_scr[b]
            o = _mm_nt(r2, s) + o0
            s_scr[b] = _mm(s, eye * pc + _mm_tn(a2, b_end)) + _mm_tn(u0, b_end) + _mm_tn(v_s, k_end)
            mean = jnp.sum(o, axis=-1, keepdims=True) * (1.0 / HEAD_DIM)
            cen = jnp.where(same, o - mean, 0.0)
            var = jnp.sum(cen * cen, axis=-1, keepdims=True) * (1.0 / HEAD_DIM)
            o_ref[b, sl, :] = _unstack_heads(cen * lax.rsqrt(var + RWKV_GN_EPS))
        return carry

    lax.fori_loop(0, nchunk, chunk, 0)


def _rwkv_chunk(xs, pc, batch, seq, cblk):
    xs = xs.reshape(5, batch, seq, W_MIX)
    pc = pc.reshape(batch, seq // CHUNK, W_MIX)
    out = pl.pallas_call(
        functools.partial(_rwkv_chunk_kernel, nchunk=cblk // CHUNK, nbatch=batch),
        grid=(seq // cblk,),
        in_specs=[
            pl.BlockSpec((5, batch, cblk, W_MIX), lambda j: (0, 0, j, 0)),
            pl.BlockSpec((batch, cblk // CHUNK, W_MIX), lambda j: (0, j, 0)),
        ],
        out_specs=pl.BlockSpec((batch, cblk, W_MIX), lambda j: (0, j, 0)),
        out_shape=jax.ShapeDtypeStruct((batch, seq, W_MIX), F32),
        scratch_shapes=[pltpu.VMEM((batch, W_MIX, W_MIX), F32)],
        compiler_params=_cparams(("arbitrary",)),
        name="rwkv_chunk",
    )(xs, pc)
    return out.reshape(batch * seq, W_MIX)


def _gdn_prep_kernel(u_ref, up_ref, cw_ref, o_ref, ext_scr, *, tiles_per_seq):
    tm = u_ref.shape[0]
    c3 = 3 * W_MIX
    prev = up_ref[...].astype(F32)[:, :c3]
    ext_scr[0:BF16_ROWS, :] = jnp.where(pl.program_id(0) % tiles_per_seq == 0, 0.0, prev)
    ext_scr[BF16_ROWS:, :] = u_ref[...].astype(F32)[:, :c3]
    y = jnp.zeros((tm, c3), F32)
    for j in range(GDN_CONV):
        y = y + ext_scr[pl.ds(BF16_ROWS - (GDN_CONV - 1) + j, tm), :] * cw_ref[j:j + 1, :]
    y = _silu(y)
    same = _block_masks()[0].astype(F32)
    q = y[:, 0:W_MIX]
    k = y[:, W_MIX:2 * W_MIX]
    o_ref[0] = (q * lax.rsqrt(_mm_f32(q * q, same) + L2_EPS) * (HEAD_DIM ** -0.5)).astype(BF16)
    o_ref[1] = (k * lax.rsqrt(_mm_f32(k * k, same) + L2_EPS)).astype(BF16)
    o_ref[2] = y[:, 2 * W_MIX:].astype(BF16)


def _gdn_prep(u, conv_w, seq, tm):
    n = u.shape[0]
    ublk = U_GDN // D_MODEL
    return pl.pallas_call(
        functools.partial(_gdn_prep_kernel, tiles_per_seq=seq // tm),
        grid=(n // tm,),
        in_specs=[
            pl.BlockSpec((tm, D_MODEL), lambda i: (i, ublk)),
            pl.BlockSpec((BF16_ROWS, D_MODEL), lambda i: (jnp.maximum(i * (tm // BF16_ROWS) - 1, 0), ublk)),
            pl.BlockSpec((GDN_CONV, 3 * W_MIX), lambda i: (0, 0)),
        ],
        out_specs=pl.BlockSpec((3, tm, W_MIX), lambda i: (0, i, 0)),
        out_shape=jax.ShapeDtypeStruct((3, n, W_MIX), BF16),
        scratch_shapes=[pltpu.VMEM((tm + BF16_ROWS, 3 * W_MIX), F32)],
        compiler_params=_cparams(("parallel",)),
        name="gdn_prep",
    )(u, u, conv_w)


def _gdn_chunk_kernel(x_ref, col_ref, nw_ref, o_ref, s_scr, *, nchunk, nbatch):
    @pl.when(pl.program_id(0) == 0)
    def _():
        s_scr[...] = jnp.zeros_like(s_scr)

    same, strict, incl, diag = _block_masks()
    eye = diag.astype(F32)
    lower = incl.astype(BF16)
    ones = same.astype(BF16)

    def head_col(col, lane0):
        return jnp.concatenate([col[:, lane0 + h:lane0 + h + 1] for h in range(N_HEADS)], axis=0)

    def chunk(ci, carry):
        sl = pl.ds(pl.multiple_of(ci * CHUNK, CHUNK), CHUNK)
        for b in range(nbatch):
            q_s, k_s, v_s = (_stack_heads(x_ref[i, b, sl, :], same) for i in range(3))
            col = col_ref[b, sl, :]
            beta = head_col(col, SM_BETA)
            g = head_col(col, SM_DEC)
            g_wide = jnp.broadcast_to(g, (W_MIX, LANES))
            gam = _mm_mask(lower, g_wide)[:, 0:1]
            gam_end = _mm_mask(ones, g_wide)[:, 0:1]
            gdiff = _mm_mask(lower, jnp.where(strict, g, 0.0))
            decay = jnp.exp(jnp.where(incl, gdiff, -jnp.inf))
            a_mat = jnp.where(strict, beta * decay * _mm_nt(k_s, k_s), 0.0)
            t_inv = _unit_lower_inverse(-a_mat, eye)
            e_gam = jnp.exp(gam)
            u0 = _mm(t_inv, beta * v_s.astype(F32))
            wm = _mm(t_inv, (beta * e_gam) * k_s.astype(F32))
            qk = _mm_nt(q_s, k_s) * decay
            q2 = e_gam * q_s.astype(F32) - _mm(qk, wm)
            o0 = _mm(qk, u0)
            k_end = k_s.astype(F32) * jnp.exp(gam_end - gam)
            s = s_scr[b]
            o = _mm(q2, s) + o0
            s_scr[b] = _mm(eye * jnp.exp(gam_end) - _mm_tn(k_end, wm), s) + _mm_tn(k_end, u0)
            ms = jnp.sum(o * o, axis=-1, keepdims=True) * (1.0 / HEAD_DIM)
            o_ref[b, sl, :] = _unstack_heads(o * lax.rsqrt(ms + NORM_EPS)) * nw_ref[...]
        return carry

    lax.fori_loop(0, nchunk, chunk, 0)


def _gdn_chunk(xs, col, norm_w, batch, seq, cblk):
    xs = xs.reshape(3, batch, seq, W_MIX)
    col = col.reshape(batch, seq, LANES)
    out = pl.pallas_call(
        functools.partial(_gdn_chunk_kernel, nchunk=cblk // CHUNK, nbatch=batch),
        grid=(seq // cblk,),
        in_specs=[
            pl.BlockSpec((3, batch, cblk, W_MIX), lambda j: (0, 0, j, 0)),
            pl.BlockSpec((batch, cblk, LANES), lambda j: (0, j, 0)),
            pl.BlockSpec((1, W_MIX), lambda j: (0, 0)),
        ],
        out_specs=pl.BlockSpec((batch, cblk, W_MIX), lambda j: (0, j, 0)),
        out_shape=jax.ShapeDtypeStruct((batch, seq, W_MIX), F32),
        scratch_shapes=[pltpu.VMEM((batch, W_MIX, W_MIX), F32)],
        compiler_params=_cparams(("arbitrary",)),
        name="gdn_chunk",
    )(xs, col, norm_w)
    return out.reshape(batch * seq, W_MIX)


def _causal_pairs(nq):
    pairs = [(i, j) for i in range(nq) for j in range(i + 1)]
    return jnp.asarray(np.array([p[0] for p in pairs], np.int32)), jnp.asarray(np.array([p[1] for p in pairs], np.int32))


def _softmax_updates(scores, vt_ones, m_scr, acc_scr):
    probs = []
    for i, s in enumerate(scores):
        m_old = m_scr[i]
        m_new = jnp.maximum(m_old, jnp.max(s, axis=0, keepdims=True))
        m_scr[i] = m_new
        probs.append((jnp.exp2(m_old - m_new), jnp.exp2((s - m_new).astype(BF16))))
    for i, (alpha, p) in enumerate(probs):
        acc_scr[i] = alpha * acc_scr[i] + jnp.dot(vt_ones[i], p, preferred_element_type=F32)


def _pair_lanes(h):
    p = h // 2
    return slice(p * LANES, (p + 1) * LANES), h % 2 == 0


def _values_and_ones(vt, tk):
    row = lax.broadcasted_iota(jnp.int32, (LANES, tk), 0)
    out = []
    for h in range(N_HEADS):
        slab, low = _pair_lanes(h)
        mine = row < HEAD_DIM if low else row >= HEAD_DIM
        out.append(jnp.where(mine, vt[slab, :], jnp.ones((), vt.dtype)))
    return out


def _normalized(acc, low):
    if low:
        return acc[0:HEAD_DIM] / acc[HEAD_DIM:HEAD_DIM + 1]
    return acc[HEAD_DIM:] / acc[0:1]


def _key_after_query(tq):
    return lax.broadcasted_iota(jnp.int32, (tq, tq), 0) > lax.broadcasted_iota(jnp.int32, (tq, tq), 1)


def _fox_kernel(qi_ref, kj_ref, q_ref, k_ref, vt_ref, crow_ref, crep_ref, o_ref, qm_scr, m_scr, acc_scr):
    t = pl.program_id(1)
    qi = qi_ref[t]
    kj = kj_ref[t]
    tq = q_ref.shape[0]

    @pl.when(kj == 0)
    def _():
        m_scr[...] = jnp.full_like(m_scr, -jnp.inf)
        acc_scr[...] = jnp.zeros_like(acc_scr)
        q = q_ref[...]
        lane = lax.broadcasted_iota(jnp.int32, (tq, LANES), 1)
        for h in range(N_HEADS):
            slab, low = _pair_lanes(h)
            mine = lane < HEAD_DIM if low else lane >= HEAD_DIM
            qm_scr[h] = jnp.where(mine, q[:, slab], jnp.zeros((), BF16))

    def step(diagonal):
        k = k_ref[...]
        vts = _values_and_ones(vt_ref[...], tq)
        cq = crow_ref[...]
        if diagonal:
            masked = _key_after_query(tq)
        scores = []
        for h in range(N_HEADS):
            slab, _ = _pair_lanes(h)
            s = lax.dot_general(k[:, slab], qm_scr[h], (((1,), (1,)), ((), ())), preferred_element_type=F32)
            ck = crep_ref[:, h * LANES:(h + 1) * LANES]
            s = (s - jnp.concatenate([ck] * (tq // LANES), axis=1)) + cq[SM_FOX + h:SM_FOX + h + 1, :]
            if diagonal:
                s = jnp.where(masked, -jnp.inf, s)
            scores.append(s)
        _softmax_updates(scores, vts, m_scr, acc_scr)

    @pl.when(kj < qi)
    def _():
        step(False)

    @pl.when(kj == qi)
    def _():
        step(True)
        for p in range(N_HEADS // 2):
            pair = jnp.concatenate([_normalized(acc_scr[2 * p], True), _normalized(acc_scr[2 * p + 1], False)], axis=0)
            o_ref[:, p * LANES:(p + 1) * LANES] = pair.T


def _attn_specs(nq, tq, ucol, vt_rows):
    cb = ucol // W_MIX
    q_spec = pl.BlockSpec((tq, W_MIX), lambda b, t, qi, kj: (b * nq + qi[t], cb))
    k_spec = pl.BlockSpec((tq, W_MIX), lambda b, t, qi, kj: (b * nq + kj[t], cb + 1))
    vt_spec = pl.BlockSpec((W_MIX, tq), lambda b, t, qi, kj: (vt_rows // W_MIX, b * nq + kj[t]))
    return q_spec, k_spec, vt_spec


def _fox_attention(u, vt, crow, crep, batch, seq, tq):
    nq = seq // tq
    qi, kj = _causal_pairs(nq)
    q_spec, k_spec, vt_spec = _attn_specs(nq, tq, U_FOX, W_MIX)
    stat = lambda: pltpu.VMEM((N_HEADS, 1, tq), F32)
    return pl.pallas_call(
        _fox_kernel,
        grid_spec=pltpu.PrefetchScalarGridSpec(
            num_scalar_prefetch=2,
            grid=(batch, qi.shape[0]),
            in_specs=[
                q_spec, k_spec, vt_spec,
                pl.BlockSpec((BF16_ROWS, tq), lambda b, t, qi, kj: (0, b * nq + qi[t])),
                pl.BlockSpec((tq, N_HEADS * LANES), lambda b, t, qi, kj: (b * nq + kj[t], 0)),
            ],
            out_specs=pl.BlockSpec((tq, W_MIX), lambda b, t, qi, kj: (b * nq + qi[t], 0)),
            scratch_shapes=[pltpu.VMEM((N_HEADS, tq, LANES), BF16), stat(),
                            pltpu.VMEM((N_HEADS, LANES, tq), F32)],
        ),
        out_shape=jax.ShapeDtypeStruct((batch * seq, W_MIX), F32),
        compiler_params=_cparams(("parallel", "arbitrary")),
        name="fox_attention",
    )(qi, kj, u, u, vt, crow, crep)


def _diff_kernel(qi_ref, kj_ref, q_ref, k_ref, vt_ref, cq_ref, sq_ref, ck_ref, sk_ref, rot_ref, lam_ref, ln_ref, o_ref,
                 qm_scr, m_scr, acc_scr, *, lam_init):
    t = pl.program_id(1)
    qi = qi_ref[t]
    kj = kj_ref[t]
    tq = q_ref.shape[0]
    lane = lax.broadcasted_iota(jnp.int32, (tq, LANES), 1)

    def rope(x, cos, sin):
        return x.astype(F32) * cos + jnp.dot(x, rot_ref[...], preferred_element_type=F32) * sin

    @pl.when(kj == 0)
    def _():
        m_scr[...] = jnp.full_like(m_scr, -jnp.inf)
        acc_scr[...] = jnp.zeros_like(acc_scr)
        q = rope(q_ref[...], cq_ref[...], sq_ref[...])
        for h in range(N_HEADS):
            slab, low = _pair_lanes(h)
            base = 0 if low else HEAD_DIM
            for c in range(2):
                lo = base + c * DIFF_DH
                sel = jnp.logical_and(lane >= lo, lane < lo + DIFF_DH)
                qm_scr[2 * h + c] = jnp.where(sel, q[:, slab], 0.0).astype(BF16)

    def step(diagonal):
        k = rope(k_ref[...], ck_ref[...], sk_ref[...]).astype(BF16)
        vts = _values_and_ones(vt_ref[...], tq)
        if diagonal:
            masked = _key_after_query(tq)
        scores = []
        for h in range(N_HEADS):
            slab, _ = _pair_lanes(h)
            for c in range(2):
                i = 2 * h + c
                s = lax.dot_general(k[:, slab], qm_scr[i], (((1,), (1,)), ((), ())), preferred_element_type=F32)
                if diagonal:
                    s = jnp.where(masked, -jnp.inf, s)
                scores.append(s)
        _softmax_updates(scores, [vts[i // 2] for i in range(2 * N_HEADS)], m_scr, acc_scr)

    @pl.when(kj < qi)
    def _():
        step(False)

    @pl.when(kj == qi)
    def _():
        step(True)
        lp = lam_ref[...]
        lam = (jnp.exp(jnp.sum(lp[0:1] * lp[1:2], axis=-1, keepdims=True))
               - jnp.exp(jnp.sum(lp[2:3] * lp[3:4], axis=-1, keepdims=True)) + lam_init)
        head = lambda h: (_normalized(acc_scr[2 * h], h % 2 == 0) - lam * _normalized(acc_scr[2 * h + 1], h % 2 == 0))
        is_lo = lane < HEAD_DIM
        for p in range(N_HEADS // 2):
            o = jnp.concatenate([head(2 * p), head(2 * p + 1)], axis=0).T
            sq = o * o
            ms_lo = jnp.sum(jnp.where(is_lo, sq, 0.0), axis=-1, keepdims=True)
            ms_hi = jnp.sum(jnp.where(is_lo, 0.0, sq), axis=-1, keepdims=True)
            ms = jnp.where(is_lo, ms_lo, ms_hi) * (1.0 / HEAD_DIM)
            o_ref[:, p * LANES:(p + 1) * LANES] = o * lax.rsqrt(ms + DIFF_LN_EPS) * ln_ref[...] * (1.0 - lam_init)


def _diff_attention(u, vt, cos, sin, rot, lam_p, subln, batch, seq, tq, lam_init):
    nq = seq // tq
    qi, kj = _causal_pairs(nq)
    q_spec, k_spec, vt_spec = _attn_specs(nq, tq, U_DIFF, 0)
    tab_q = pl.BlockSpec((tq, W_MIX), lambda b, t, qi, kj: (qi[t], 0))
    tab_k = pl.BlockSpec((tq, W_MIX), lambda b, t, qi, kj: (kj[t], 0))
    const = lambda r, c: pl.BlockSpec((r, c), lambda b, t, qi, kj: (0, 0))
    stat = lambda: pltpu.VMEM((2 * N_HEADS, 1, tq), F32)
    return pl.pallas_call(
        functools.partial(_diff_kernel, lam_init=lam_init),
        grid_spec=pltpu.PrefetchScalarGridSpec(
            num_scalar_prefetch=2,
            grid=(batch, qi.shape[0]),
            in_specs=[q_spec, k_spec, vt_spec, tab_q, tab_q, tab_k, tab_k,
                      const(W_MIX, W_MIX), const(4, DIFF_DH), const(1, LANES)],
            out_specs=pl.BlockSpec((tq, W_MIX), lambda b, t, qi, kj: (b * nq + qi[t], 0)),
            scratch_shapes=[pltpu.VMEM((2 * N_HEADS, tq, LANES), BF16), stat(),
                            pltpu.VMEM((2 * N_HEADS, LANES, tq), F32)],
        ),
        out_shape=jax.ShapeDtypeStruct((batch * seq, W_MIX), F32),
        compiler_params=_cparams(("parallel", "arbitrary")),
        name="diff_attention",
    )(qi, kj, u, u, vt, cos, sin, cos, sin, rot, lam_p, subln)


def _merge_kernel(x_ref, gate_ref, oa_ref, post_ref, yb_ref, yc_ref, od_ref, gd_ref, lnw_ref, lnb_ref,
                  wbo_ref, wout_ref, o_ref):
    y_a = (oa_ref[...] * lnw_ref[...] + lnb_ref[...] + post_ref[1]) * post_ref[0]
    y_d = od_ref[...] * _silu(gd_ref[...].astype(F32))
    acc = jnp.zeros(x_ref.shape, F32)
    for b, y in enumerate((y_a, yb_ref[...], yc_ref[...], y_d)):
        gate = _sigmoid(gate_ref[:, b * D_MODEL:(b + 1) * D_MODEL].astype(F32))
        acc = acc + gate * _mm(y, wbo_ref[b])
    o_ref[...] = x_ref[...] + _mm(acc, wout_ref[...])


def _merge(x, u, o_a, post, y_b, y_c, o_d, ln_w, ln_b, w_bo, w_out, tm):
    n = x.shape[0]
    tok = lambda c: pl.BlockSpec((tm, c), lambda i: (i, 0))
    return pl.pallas_call(
        _merge_kernel,
        grid=(n // tm,),
        in_specs=[
            tok(D_MODEL),
            pl.BlockSpec((tm, 4 * D_MODEL), lambda i: (i, U_GATE // (4 * D_MODEL))),
            tok(W_MIX),
            pl.BlockSpec((2, tm, W_MIX), lambda i: (0, i, 0)),
            tok(W_MIX), tok(W_MIX), tok(W_MIX),
            pl.BlockSpec((tm, W_MIX), lambda i: (i, (U_GDN + 3 * W_MIX) // W_MIX)),
            pl.BlockSpec((1, W_MIX), lambda i: (0, 0)),
            pl.BlockSpec((1, W_MIX), lambda i: (0, 0)),
            pl.BlockSpec((4, W_MIX, D_MODEL), lambda i: (0, 0, 0)),
            pl.BlockSpec((D_MODEL, D_MODEL), lambda i: (0, 0)),
        ],
        out_specs=tok(D_MODEL),
        out_shape=jax.ShapeDtypeStruct((n, D_MODEL), F32),
        compiler_params=_cparams(("parallel",)),
        name="merge",
    )(x, u, o_a, post, y_b, y_c, o_d, u, ln_w, ln_b, w_bo, w_out)


def _ffn_kernel(x_ref, g_ref, wg_ref, wu_ref, wd_ref, o_ref, h_scr, acc_scr):
    f = pl.program_id(1)

    @pl.when(f == 0)
    def _():
        h_scr[...] = _rms(x_ref[...], g_ref[...], NORM_EPS).astype(BF16)
        acc_scr[...] = jnp.zeros_like(acc_scr)

    h = h_scr[...]
    act = _silu(jnp.dot(h, wg_ref[...], preferred_element_type=F32)) * jnp.dot(h, wu_ref[...], preferred_element_type=F32)
    acc_scr[...] += _mm(act, wd_ref[...])

    @pl.when(f == pl.num_programs(1) - 1)
    def _():
        o_ref[...] = x_ref[...] + acc_scr[...]


def _ffn(x, g, wg, wu, wd, tm, tf):
    n = x.shape[0]
    ff = wg.shape[1]
    return pl.pallas_call(
        _ffn_kernel,
        grid=(n // tm, ff // tf),
        in_specs=[
            pl.BlockSpec((tm, D_MODEL), lambda i, f: (i, 0)),
            pl.BlockSpec((1, D_MODEL), lambda i, f: (0, 0)),
            pl.BlockSpec((D_MODEL, tf), lambda i, f: (0, f)),
            pl.BlockSpec((D_MODEL, tf), lambda i, f: (0, f)),
            pl.BlockSpec((tf, D_MODEL), lambda i, f: (f, 0)),
        ],
        out_specs=pl.BlockSpec((tm, D_MODEL), lambda i, f: (i, 0)),
        out_shape=jax.ShapeDtypeStruct((n, D_MODEL), F32),
        scratch_shapes=[pltpu.VMEM((tm, D_MODEL), BF16), pltpu.VMEM((tm, D_MODEL), F32)],
        compiler_params=_cparams(("parallel", "arbitrary")),
        name="ffn",
    )(x, g, wg, wu, wd)


def _moe_kernel(x_ref, g_ref, router_ref, wg_ref, wu_ref, wd_ref, o_ref, h_scr, c_scr, acc_scr):
    e = pl.program_id(1)
    f = pl.program_id(2)

    @pl.when(jnp.logical_and(e == 0, f == 0))
    def _():
        h = _rms(x_ref[...], g_ref[...], NORM_EPS)
        h_scr[...] = h.astype(BF16)
        acc_scr[...] = jnp.zeros_like(acc_scr)
        logits = _mm_f32(h, router_ref[...])
        lane = lax.broadcasted_iota(jnp.int32, logits.shape, 1).astype(F32)
        lg = jnp.where(lane < N_EXPERTS, logits, -jnp.inf)
        m1 = jnp.max(lg, axis=-1, keepdims=True)
        i1 = jnp.min(jnp.where(lg == m1, lane, float(LANES)), axis=-1, keepdims=True)
        lg2 = jnp.where(lane == i1, -jnp.inf, lg)
        m2 = jnp.max(lg2, axis=-1, keepdims=True)
        i2 = jnp.min(jnp.where(lg2 == m2, lane, float(LANES)), axis=-1, keepdims=True)
        e2 = jnp.exp(m2 - m1)
        c_scr[...] = jnp.where(lane == i1, 1.0 / (1.0 + e2), 0.0) + jnp.where(lane == i2, e2 / (1.0 + e2), 0.0)

    c = c_scr[...]
    lane = lax.broadcasted_iota(jnp.int32, c.shape, 1)
    ce = jnp.sum(jnp.where(lane == e, c, 0.0), axis=-1, keepdims=True)
    h = h_scr[...]
    act = _silu(jnp.dot(h, wg_ref[0], preferred_element_type=F32)) * jnp.dot(h, wu_ref[0], preferred_element_type=F32)
    acc_scr[...] += _mm(act * ce, wd_ref[0])

    @pl.when(jnp.logical_and(e == pl.num_programs(1) - 1, f == pl.num_programs(2) - 1))
    def _():
        o_ref[...] = x_ref[...] + acc_scr[...]


def _moe(x, g, router, wg, wu, wd, tm, tf):
    n = x.shape[0]
    ff = wg.shape[2]
    return pl.pallas_call(
        _moe_kernel,
        grid=(n // tm, N_EXPERTS, ff // tf),
        in_specs=[
            pl.BlockSpec((tm, D_MODEL), lambda i, e, f: (i, 0)),
            pl.BlockSpec((1, D_MODEL), lambda i, e, f: (0, 0)),
            pl.BlockSpec((D_MODEL, LANES), lambda i, e, f: (0, 0)),
            pl.BlockSpec((1, D_MODEL, tf), lambda i, e, f: (e, 0, f)),
            pl.BlockSpec((1, D_MODEL, tf), lambda i, e, f: (e, 0, f)),
            pl.BlockSpec((1, tf, D_MODEL), lambda i, e, f: (e, f, 0)),
        ],
        out_specs=pl.BlockSpec((tm, D_MODEL), lambda i, e, f: (i, 0)),
        out_shape=jax.ShapeDtypeStruct((n, D_MODEL), F32),
        scratch_shapes=[pltpu.VMEM((tm, D_MODEL), BF16), pltpu.VMEM((tm, LANES), F32), pltpu.VMEM((tm, D_MODEL), F32)],
        compiler_params=_cparams(("parallel", "arbitrary", "arbitrary"), vmem_mb=56),
        name="moe",
    )(x, g, router, wg, wu, wd)


def _ple_kernel(x_ref, p_ref, g_ref, wgate_ref, wproj_ref, fin_ref, o_ref, *, final):
    x = x_ref[...]
    h = _rms(x, g_ref[...], NORM_EPS)
    y = x + _sigmoid(_mm(h, wgate_ref[...])) * _mm(p_ref[...], wproj_ref[...])
    o_ref[...] = _rms(y, fin_ref[...], NORM_EPS) if final else y


def _ple(x, p, g, wgate, wproj, fin, tm, final):
    n = x.shape[0]
    return pl.pallas_call(
        functools.partial(_ple_kernel, final=final),
        grid=(n // tm,),
        in_specs=[
            pl.BlockSpec((tm, D_MODEL), lambda i: (i, 0)),
            pl.BlockSpec((tm, P_DIM), lambda i: (i, 0)),
            pl.BlockSpec((1, D_MODEL), lambda i: (0, 0)),
            pl.BlockSpec((D_MODEL, D_MODEL), lambda i: (0, 0)),
            pl.BlockSpec((P_DIM, D_MODEL), lambda i: (0, 0)),
            pl.BlockSpec((1, D_MODEL), lambda i: (0, 0)),
        ],
        out_specs=pl.BlockSpec((tm, D_MODEL), lambda i: (i, 0)),
        out_shape=jax.ShapeDtypeStruct((n, D_MODEL), F32),
        compiler_params=_cparams(("parallel",)),
        name="ple",
    )(x, p, g, wgate, wproj, fin)


def _tiles(n, seq):
    tm = min(512, seq)
    tm_in = 1024 if n % 1024 == 0 else tm
    return dict(tm=tm, tm_in=tm_in, tn_in=1536, tq=min(512, seq), cblk=min(512, seq))


def _rope_tables(seq):
    half = ROPE_DIMS // 2
    inv = ROPE_THETA ** (-jnp.arange(half, dtype=F32) * 2.0 / ROPE_DIMS)
    ang = jnp.arange(seq, dtype=F32)[:, None] * inv[None, :]
    pad = jnp.zeros((seq, DIFF_DH - ROPE_DIMS), F32)
    cos = jnp.concatenate([jnp.cos(ang), jnp.cos(ang), pad + 1.0], axis=-1)
    sin = jnp.concatenate([jnp.sin(ang), jnp.sin(ang), pad], axis=-1)
    d = jnp.arange(W_MIX)
    dd = d % DIFF_DH
    src = jnp.where(dd < half, d + half, d - half)
    sign = jnp.where(dd < half, -1.0, jnp.where(dd < ROPE_DIMS, 1.0, 0.0))
    rot = jnp.zeros((W_MIX, W_MIX), F32).at[src, d].set(sign)
    reps = W_MIX // DIFF_DH
    return jnp.tile(cos, (1, reps)), jnp.tile(sin, (1, reps)), rot.astype(BF16)


def _split_w_in(w):
    a0 = 0
    b0 = a0 + 4 * W_MIX
    c0 = b0 + 3 * W_MIX
    d0 = c0 + 3 * W_MIX + N_HEADS
    g0 = d0 + 4 * W_MIX + 2 * N_HEADS
    d_small = d0 + 3 * W_MIX
    diff_q = w[:, b0:b0 + W_MIX] * (DIFF_DH ** -0.5 * LOG2E)
    fox_q = w[:, c0:c0 + W_MIX] * (HEAD_DIM ** -0.5 * LOG2E)
    main = jnp.concatenate([
        w[:, g0:], w[:, a0:b0], w[:, d0:d_small], w[:, d_small + 2 * N_HEADS:g0],
        diff_q, w[:, b0 + W_MIX:c0], fox_q, w[:, c0 + W_MIX:c0 + 3 * W_MIX],
    ], axis=1).astype(BF16)
    small = jnp.concatenate([
        w[:, c0 + 3 * W_MIX:d0], w[:, d_small:d_small + 2 * N_HEADS],
        jnp.zeros((D_MODEL, LANES - 3 * N_HEADS), w.dtype),
    ], axis=1).astype(BF16)
    v_t = jnp.concatenate([w[:, b0 + 2 * W_MIX:c0], w[:, c0 + 2 * W_MIX:c0 + 3 * W_MIX]], axis=1).T.astype(BF16)
    return main, small, small[:, :BF16_ROWS].T, v_t


def _small_params(fbias, a_log, dt_bias):
    zeros = jnp.zeros((N_HEADS,), F32)
    bias = jnp.concatenate([fbias, zeros, dt_bias, jnp.zeros((LANES - 3 * N_HEADS,), F32)])
    neg_a = jnp.concatenate([zeros, zeros, -jnp.exp(a_log), jnp.zeros((LANES - 3 * N_HEADS,), F32)])
    return (bias.reshape(1, LANES), bias[:BF16_ROWS].reshape(BF16_ROWS, 1),
            neg_a.reshape(1, LANES), neg_a[:BF16_ROWS].reshape(BF16_ROWS, 1))


def _pad_rows(w, top, total):
    return jnp.concatenate([jnp.zeros((top, w.shape[1]), w.dtype), w,
                            jnp.zeros((total - top - w.shape[0], w.shape[1]), w.dtype)], axis=0)


def kernel(x, p, norm_mix, norm_ffn, norm_ple, w_in, w_bo, w_out, rwkv_mu, rwkv_w0, rwkv_w2, rwkv_a0, rwkv_a2, rwkv_g2, rwkv_kk, rwkv_ka, rwkv_rk, rwkv_ln_w, rwkv_ln_b, diff_lam, diff_subln, fox_fbias, gdn_conv, gdn_a_log, gdn_dt_bias, gdn_norm, ffn_w_gate, ffn_w_up, ffn_w_down, moe_router, moe_w_gate, moe_w_up, moe_w_down, ple_proj, ple_gate, final_norm):
    batch, seq, _ = x.shape
    depth = w_in.shape[0]
    n = batch * seq
    t = _tiles(n, seq)
    tm, tq, cblk = t["tm"], t["tq"], t["cblk"]
    row = lambda v: v.reshape(1, -1).astype(F32)
    cos, sin, rot = _rope_tables(seq)
    xf = x.reshape(n, D_MODEL)
    pf = p.reshape(depth, n, P_DIM)

    for i in range(depth):
        w_main, w_small, w_small_t, w_vt = _split_w_in(w_in[i])
        u, scol, srow, vt = _inproj(xf, row(norm_mix[i]), w_main, w_small, w_small_t, w_vt, t["tm_in"], t["tn_in"])
        bcol, brow, acol, arow = _small_params(fox_fbias[i], gdn_a_log[i], gdn_dt_bias[i])
        hcol, hrow, hrep = _small_prep(scol, srow, bcol, brow, acol, arow, batch, seq, tm)

        scan_in, pc, post = _rwkv_prep(
            u, row(rwkv_mu[i]), row(rwkv_w0[i]), _pad_rows(rwkv_w2[i], 0, LANES), row(rwkv_a0[i]),
            _pad_rows(rwkv_a2[i], LANES // 2, LANES), rwkv_g2[i], row(rwkv_kk[i]), row(rwkv_ka[i]), row(rwkv_rk[i]),
            seq, tm)
        o_a = _rwkv_chunk(scan_in, pc, batch, seq, cblk)

        o_d = _gdn_chunk(_gdn_prep(u, gdn_conv[i].T, seq, tm), hcol, jnp.tile(row(gdn_norm[i]), (1, N_HEADS)),
                         batch, seq, cblk)

        lam_init = 0.8 - 0.6 * math.exp(-0.3 * i)
        y_b = _diff_attention(u, vt, cos, sin, rot, diff_lam[i].astype(F32), jnp.tile(row(diff_subln[i]), (1, 2)),
                              batch, seq, tq, lam_init)
        y_c = _fox_attention(u, vt, hrow, hrep, batch, seq, tq)

        xf = _merge(xf, u, o_a, post, y_b, y_c, o_d, row(rwkv_ln_w[i]), row(rwkv_ln_b[i]),
                    w_bo[i].astype(BF16), w_out[i].astype(BF16), tm)

        j = i // 2
        if i % 2 == 0:
            xf = _ffn(xf, row(norm_ffn[i]), ffn_w_gate[j].astype(BF16), ffn_w_up[j].astype(BF16),
                      ffn_w_down[j].astype(BF16), tm, ffn_w_gate.shape[2] // 2)
        else:
            router = jnp.concatenate([moe_router[j], jnp.zeros((D_MODEL, LANES - N_EXPERTS), F32)], axis=1)
            xf = _moe(xf, row(norm_ffn[i]), router, moe_w_gate[j].astype(BF16), moe_w_up[j].astype(BF16),
                      moe_w_down[j].astype(BF16), tm, moe_w_gate.shape[3] // 2)
        xf = _ple(xf, pf[i], row(norm_ple[i]), ple_gate[i].astype(BF16), ple_proj[i].astype(BF16),
                  row(final_norm), tm, i == depth - 1)
    return xf.reshape(batch, seq, D_MODEL)
```

```python
import functools
import math

import jax
import jax.numpy as jnp
import numpy as np
from jax import lax
from jax.experimental import pallas as pl
from jax.experimental.pallas import tpu as pltpu

F32 = jnp.float32
BF16 = jnp.bfloat16
HIGHEST = lax.Precision.HIGHEST

D_MODEL = 1024
P_DIM = 256
W_MIX = 256
HEAD_DIM = 64
N_HEADS = 4
DIFF_DH = 32
ROPE_THETA = 500000.0
ROPE_DIMS = 8
RWKV_GN_EPS = 64e-5
DIFF_LN_EPS = 1e-5
GDN_CONV = 4
CHUNK = 64
N_EXPERTS = 8
NORM_EPS = 1e-6
L2_EPS = 1e-6
LOG2E = math.log2(math.e)
LANES = 128
BF16_ROWS = 16
assert CHUNK == HEAD_DIM

U_GATE = 0
U_RWKV = 4096
U_GDN = 5120
U_DIFF = 6144
U_FOX = 6912
U_COLS = 7680
SM_FOX, SM_BETA, SM_DEC = 0, 4, 8


def _cparams(semantics, vmem_mb=48):
    return pltpu.CompilerParams(dimension_semantics=semantics, vmem_limit_bytes=vmem_mb * 1024 * 1024)


def _mm(a, b):
    return jnp.dot(a.astype(BF16), b.astype(BF16), preferred_element_type=F32)


def _mm_nt(a, b):
    return lax.dot_general(a.astype(BF16), b.astype(BF16), (((1,), (1,)), ((), ())), preferred_element_type=F32)


def _mm_tn(a, b):
    return lax.dot_general(a.astype(BF16), b.astype(BF16), (((0,), (0,)), ((), ())), preferred_element_type=F32)


def _mm_f32(a, b):
    return jnp.dot(a, b, preferred_element_type=F32, precision=HIGHEST)


def _mm_mask(mask, x):
    hi = x.astype(BF16)
    r1 = x - hi.astype(F32)
    mid = r1.astype(BF16)
    lo = (r1 - mid.astype(F32)).astype(BF16)
    dot = lambda t: jnp.dot(mask, t, preferred_element_type=F32)
    return dot(hi) + dot(mid) + dot(lo)


def _rms(x, g, eps):
    return x * lax.rsqrt(jnp.mean(x * x, axis=-1, keepdims=True) + eps) * g


def _sigmoid(x):
    return 1.0 / (1.0 + jnp.exp(-x))


def _silu(x):
    return x * _sigmoid(x)


def _softplus(x):
    return jnp.maximum(x, 0.0) + jnp.log(1.0 + jnp.exp(-jnp.abs(x)))


def _tri_masks(c):
    ii = lax.broadcasted_iota(jnp.int32, (c, c), 0)
    jj = lax.broadcasted_iota(jnp.int32, (c, c), 1)
    return ii > jj, ii >= jj, ii == jj


def _block_masks():
    ii = lax.broadcasted_iota(jnp.int32, (W_MIX, W_MIX), 0)
    jj = lax.broadcasted_iota(jnp.int32, (W_MIX, W_MIX), 1)
    same = (ii // HEAD_DIM) == (jj // HEAD_DIM)
    return same, jnp.logical_and(same, ii > jj), jnp.logical_and(same, ii >= jj), ii == jj


def _unit_lower_inverse(n, eye):
    r = eye + n
    p = n
    for _ in range(int(math.log2(CHUNK)) - 1):
        p = _mm(p, p)
        r = r + _mm(r, p)
    return r


def _stack_heads(x, same):
    return jnp.where(same, jnp.concatenate([x, x, x, x], axis=0), jnp.zeros((), x.dtype))


def _unstack_heads(x):
    return x[0:CHUNK] + x[CHUNK:2 * CHUNK] + x[2 * CHUNK:3 * CHUNK] + x[3 * CHUNK:4 * CHUNK]


def _inproj_kernel(x_ref, g_ref, w_ref, ws_ref, wst_ref, wvt_ref, u_ref, scol_ref, srow_ref, vt_ref, h_scr):
    @pl.when(pl.program_id(1) == 0)
    def _():
        hb = _rms(x_ref[...], g_ref[...], NORM_EPS).astype(BF16)
        h_scr[...] = hb
        nt = (((1,), (1,)), ((), ()))
        scol_ref[...] = jnp.dot(hb, ws_ref[...], preferred_element_type=F32)
        srow_ref[...] = lax.dot_general(wst_ref[...], hb, nt, preferred_element_type=F32)
        vt_ref[...] = lax.dot_general(wvt_ref[...], hb, nt, preferred_element_type=F32).astype(BF16)

    u_ref[...] = jnp.dot(h_scr[...], w_ref[...], preferred_element_type=F32).astype(BF16)


def _inproj(x, g, w, ws, wst, wvt, tm, tn):
    n = x.shape[0]
    return pl.pallas_call(
        _inproj_kernel,
        grid=(n // tm, U_COLS // tn),
        in_specs=[
            pl.BlockSpec((tm, D_MODEL), lambda i, j: (i, 0)),
            pl.BlockSpec((1, D_MODEL), lambda i, j: (0, 0)),
            pl.BlockSpec((D_MODEL, tn), lambda i, j: (0, j)),
            pl.BlockSpec((D_MODEL, LANES), lambda i, j: (0, 0)),
            pl.BlockSpec((BF16_ROWS, D_MODEL), lambda i, j: (0, 0)),
            pl.BlockSpec((2 * W_MIX, D_MODEL), lambda i, j: (0, 0)),
        ],
        out_specs=[
            pl.BlockSpec((tm, tn), lambda i, j: (i, j)),
            pl.BlockSpec((tm, LANES), lambda i, j: (i, 0)),
            pl.BlockSpec((BF16_ROWS, tm), lambda i, j: (0, i)),
            pl.BlockSpec((2 * W_MIX, tm), lambda i, j: (0, i)),
        ],
        out_shape=[
            jax.ShapeDtypeStruct((n, U_COLS), BF16),
            jax.ShapeDtypeStruct((n, LANES), F32),
            jax.ShapeDtypeStruct((BF16_ROWS, n), F32),
            jax.ShapeDtypeStruct((2 * W_MIX, n), BF16),
        ],
        scratch_shapes=[pltpu.VMEM((tm, D_MODEL), BF16)],
        compiler_params=_cparams(("parallel", "arbitrary")),
        name="inproj",
    )(x, g, w, ws, wst, wvt)


def _small_prep_kernel(scol_ref, srow_ref, bcol_ref, brow_ref, acol_ref, arow_ref, ocol_ref, orow_ref, orep_ref,
                       ccol_scr, crow_scr):
    @pl.when(pl.program_id(1) == 0)
    def _():
        ccol_scr[...] = jnp.zeros_like(ccol_scr)
        crow_scr[...] = jnp.zeros_like(crow_scr)

    tm = scol_ref.shape[0]
    strict, incl, _ = _tri_masks(tm)
    lower = incl.astype(BF16)
    upper = jnp.logical_not(strict).astype(F32)

    def funcs(z, idx, neg_a):
        logf = jnp.minimum(z, 0.0) - jnp.log(1.0 + jnp.exp(-jnp.abs(z)))
        beta = _sigmoid(z)
        dec = neg_a * _softplus(z)
        is_f = idx < SM_BETA
        is_b = jnp.logical_and(idx >= SM_BETA, idx < SM_DEC)
        is_d = jnp.logical_and(idx >= SM_DEC, idx < SM_DEC + N_HEADS)
        return jnp.where(is_f, logf, 0.0), jnp.where(is_b, beta, jnp.where(is_d, dec, 0.0)), is_f

    zc = scol_ref[...] + bcol_ref[...]
    lane = lax.broadcasted_iota(jnp.int32, zc.shape, 1)
    logf_c, rest_c, is_f_c = funcs(zc, lane, acol_ref[...])
    cum_c = _mm_mask(lower, logf_c) + ccol_scr[...]
    ccol_scr[...] = cum_c[tm - 1:tm, :]
    ocol_ref[...] = rest_c
    for h in range(N_HEADS):
        orep_ref[:, h * LANES:(h + 1) * LANES] = jnp.broadcast_to(
            cum_c[:, SM_FOX + h:SM_FOX + h + 1] * LOG2E, (tm, LANES))

    zr = srow_ref[...] + brow_ref[...]
    sub = lax.broadcasted_iota(jnp.int32, zr.shape, 0)
    logf_r, _, _ = funcs(zr, sub, arow_ref[...])
    cum_r = _mm_f32(logf_r, upper) + crow_scr[...]
    crow_scr[...] = cum_r[:, tm - 1:tm]
    orow_ref[...] = cum_r * LOG2E


def _small_prep(scol, srow, bcol, brow, acol, arow, batch, seq, tm):
    n = batch * seq
    nt = seq // tm
    return pl.pallas_call(
        _small_prep_kernel,
        grid=(batch, nt),
        in_specs=[
            pl.BlockSpec((tm, LANES), lambda b, j: (b * nt + j, 0)),
            pl.BlockSpec((BF16_ROWS, tm), lambda b, j: (0, b * nt + j)),
            pl.BlockSpec((1, LANES), lambda b, j: (0, 0)),
            pl.BlockSpec((BF16_ROWS, 1), lambda b, j: (0, 0)),
            pl.BlockSpec((1, LANES), lambda b, j: (0, 0)),
            pl.BlockSpec((BF16_ROWS, 1), lambda b, j: (0, 0)),
        ],
        out_specs=[
            pl.BlockSpec((tm, LANES), lambda b, j: (b * nt + j, 0)),
            pl.BlockSpec((BF16_ROWS, tm), lambda b, j: (0, b * nt + j)),
            pl.BlockSpec((tm, N_HEADS * LANES), lambda b, j: (b * nt + j, 0)),
        ],
        out_shape=[jax.ShapeDtypeStruct((n, LANES), F32), jax.ShapeDtypeStruct((BF16_ROWS, n), F32),
                   jax.ShapeDtypeStruct((n, N_HEADS * LANES), F32)],
        scratch_shapes=[pltpu.VMEM((1, LANES), F32), pltpu.VMEM((BF16_ROWS, 1), F32)],
        compiler_params=_cparams(("parallel", "arbitrary")),
        name="small_prep",
    )(scol, srow, bcol, brow, acol, arow)


def _rwkv_prep_kernel(u_ref, up_ref, mu_ref, w0_ref, w2_ref, a0_ref, a2_ref, g2_ref, kk_ref, ka_ref, rk_ref,
                      scan_ref, pc_ref, post_ref, *, tiles_per_seq):
    tm = u_ref.shape[0]
    u = u_ref[...].astype(F32)
    prev = up_ref[...].astype(F32)[BF16_ROWS - 1:BF16_ROWS, :]
    prev = jnp.where(pl.program_id(0) % tiles_per_seq == 0, 0.0, prev)
    rows = lax.broadcasted_iota(jnp.int32, (tm, 1), 0)
    u_prev = jnp.where(rows == 0, prev, pltpu.roll(u, 1, 0))
    xm = u + (u_prev - u) * mu_ref[...]
    r = xm[:, 0:W_MIX]
    k = xm[:, W_MIX:2 * W_MIX]
    v = xm[:, 2 * W_MIX:3 * W_MIX]
    x_lora = xm[:, 3 * W_MIX:3 * W_MIX + LANES]
    xg = xm[:, 3 * W_MIX + LANES:]
    logw = -_softplus(-(w0_ref[...] + _mm(jnp.tanh(x_lora), w2_ref[...]))) - 0.5
    log_decay = -jnp.exp(logw)
    a = _sigmoid(a0_ref[...] + _mm(x_lora, a2_ref[...]))
    g = _mm(_sigmoid(xg), g2_ref[...])
    same = _block_masks()[0].astype(F32)
    kk_raw = k * kk_ref[...]
    kk = kk_raw * lax.rsqrt(_mm_f32(kk_raw * kk_raw, same) + L2_EPS)
    k2 = k * (1.0 + (a - 1.0) * ka_ref[...])
    bonus = _mm_f32(r * k2 * rk_ref[...], same) * v
    ti = lax.broadcasted_iota(jnp.int32, (tm, tm), 0)
    tj = lax.broadcasted_iota(jnp.int32, (tm, tm), 1)
    in_chunk = jnp.logical_and(ti // CHUNK == tj // CHUNK, ti >= tj).astype(BF16)
    ci = lax.broadcasted_iota(jnp.int32, (tm // CHUNK, tm), 0)
    cj = lax.broadcasted_iota(jnp.int32, (tm // CHUNK, tm), 1)
    cum = _mm_mask(in_chunk, log_decay)
    cum_end = _mm_mask((ci == cj // CHUNK).astype(BF16), log_decay)
    inv = jnp.exp(-cum)
    scan_ref[0] = (-kk * jnp.exp(cum - log_decay)).astype(BF16)
    scan_ref[1] = (kk * a * inv).astype(BF16)
    scan_ref[2] = (k2 * inv).astype(BF16)
    scan_ref[3] = (r * jnp.exp(cum)).astype(BF16)
    scan_ref[4] = v.astype(BF16)
    pc_ref[...] = jnp.exp(cum_end)
    post_ref[0] = g
    post_ref[1] = bonus


def _rwkv_prep(u, mu, w0, w2p, a0, a2p, g2, k_k, k_a, r_k, seq, tm):
    n = u.shape[0]
    ublk = U_RWKV // D_MODEL
    row = lambda c: pl.BlockSpec((1, c), lambda i: (0, 0))
    mat = lambda r: pl.BlockSpec((r, W_MIX), lambda i: (0, 0))
    return pl.pallas_call(
        functools.partial(_rwkv_prep_kernel, tiles_per_seq=seq // tm),
        grid=(n // tm,),
        in_specs=[
            pl.BlockSpec((tm, D_MODEL), lambda i: (i, ublk)),
            pl.BlockSpec((BF16_ROWS, D_MODEL), lambda i: (jnp.maximum(i * (tm // BF16_ROWS) - 1, 0), ublk)),
            row(D_MODEL), row(W_MIX), mat(LANES), row(W_MIX), mat(LANES), mat(LANES), row(W_MIX), row(W_MIX), row(W_MIX),
        ],
        out_specs=[
            pl.BlockSpec((5, tm, W_MIX), lambda i: (0, i, 0)),
            pl.BlockSpec((tm // CHUNK, W_MIX), lambda i: (i, 0)),
            pl.BlockSpec((2, tm, W_MIX), lambda i: (0, i, 0)),
        ],
        out_shape=[jax.ShapeDtypeStruct((5, n, W_MIX), BF16), jax.ShapeDtypeStruct((n // CHUNK, W_MIX), F32),
                   jax.ShapeDtypeStruct((2, n, W_MIX), F32)],
        compiler_params=_cparams(("parallel",)),
        name="rwkv_prep",
    )(u, u, mu, w0, w2p, a0, a2p, g2, k_k, k_a, r_k)


def _rwkv_chunk_kernel(x_ref, pc_ref, o_ref, s_scr, *, nchunk, nbatch):
    @pl.when(pl.program_id(0) == 0)
    def _():
        s_scr[...] = jnp.zeros_like(s_scr)

    same, strict, incl, diag = _block_masks()
    eye = diag.astype(F32)

    def chunk(ci, carry):
        sl = pl.ds(pl.multiple_of(ci * CHUNK, CHUNK), CHUNK)
        for b in range(nbatch):
            a_s, b_s, k_s, r_s, v_s = (_stack_heads(x_ref[i, b, sl, :], same) for i in range(5))
            pc = pc_ref[b, pl.ds(ci, 1), :]
            m_ab = jnp.where(strict, _mm_nt(a_s, b_s), 0.0)
            m_ak = jnp.where(strict, _mm_nt(a_s, k_s), 0.0)
            n_rb = jnp.where(incl, _mm_nt(r_s, b_s), 0.0)
            n_rk = jnp.where(incl, _mm_nt(r_s, k_s), 0.0)
            t_inv = _unit_lower_inverse(m_ab, eye)
            a2 = _mm(t_inv, a_s)
            u0 = _mm(t_inv, _mm(m_ak, v_s))
            r2 = r_s.astype(F32) + _mm(n_rb, a2)
            o0 = _mm(n_rb, u0) + _mm(n_rk, v_s)
            b_end = b_s.astype(F32) * pc
            k_end = k_s.astype(F32) * pc
            s = s_scr[b]
            o = _mm_nt(r2, s) + o0
            s_scr[b] = _mm(s, eye * pc + _mm_tn(a2, b_end)) + _mm_tn(u0, b_end) + _mm_tn(v_s, k_end)
            mean = jnp.sum(o, axis=-1, keepdims=True) * (1.0 / HEAD_DIM)
            cen = jnp.where(same, o - mean, 0.0)
            var = jnp.sum(cen * cen, axis=-1, keepdims=True) * (1.0 / HEAD_DIM)
            o_ref[b, sl, :] = _unstack_heads(cen * lax.rsqrt(var + RWKV_GN_EPS))
        return carry

    lax.fori_loop(0, nchunk, chunk, 0)


def _rwkv_chunk(xs, pc, batch, seq, cblk):
    xs = xs.reshape(5, batch, seq, W_MIX)
    pc = pc.reshape(batch, seq // CHUNK, W_MIX)
    out = pl.pallas_call(
        functools.partial(_rwkv_chunk_kernel, nchunk=cblk // CHUNK, nbatch=batch),
        grid=(seq // cblk,),
        in_specs=[
            pl.BlockSpec((5, batch, cblk, W_MIX), lambda j: (0, 0, j, 0)),
            pl.BlockSpec((batch, cblk // CHUNK, W_MIX), lambda j: (0, j, 0)),
        ],
        out_specs=pl.BlockSpec((batch, cblk, W_MIX), lambda j: (0, j, 0)),
        out_shape=jax.ShapeDtypeStruct((batch, seq, W_MIX), F32),
        scratch_shapes=[pltpu.VMEM((batch, W_MIX, W_MIX), F32)],
        compiler_params=_cparams(("arbitrary",)),
        name="rwkv_chunk",
    )(xs, pc)
    return out.reshape(batch * seq, W_MIX)


def _gdn_prep_kernel(u_ref, up_ref, cw_ref, o_ref, ext_scr, *, tiles_per_seq):
    tm = u_ref.shape[0]
    c3 = 3 * W_MIX
    prev = up_ref[...].astype(F32)[:, :c3]
    ext_scr[0:BF16_ROWS, :] = jnp.where(pl.program_id(0) % tiles_per_seq == 0, 0.0, prev)
    ext_scr[BF16_ROWS:, :] = u_ref[...].astype(F32)[:, :c3]
    y = jnp.zeros((tm, c3), F32)
    for j in range(GDN_CONV):
        y = y + ext_scr[pl.ds(BF16_ROWS - (GDN_CONV - 1) + j, tm), :] * cw_ref[j:j + 1, :]
    y = _silu(y)
    same = _block_masks()[0].astype(F32)
    q = y[:, 0:W_MIX]
    k = y[:, W_MIX:2 * W_MIX]
    o_ref[0] = (q * lax.rsqrt(_mm_f32(q * q, same) + L2_EPS) * (HEAD_DIM ** -0.5)).astype(BF16)
    o_ref[1] = (k * lax.rsqrt(_mm_f32(k * k, same) + L2_EPS)).astype(BF16)
    o_ref[2] = y[:, 2 * W_MIX:].astype(BF16)


def _gdn_prep(u, conv_w, seq, tm):
    n = u.shape[0]
    ublk = U_GDN // D_MODEL
    return pl.pallas_call(
        functools.partial(_gdn_prep_kernel, tiles_per_seq=seq // tm),
        grid=(n // tm,),
        in_specs=[
            pl.BlockSpec((tm, D_MODEL), lambda i: (i, ublk)),
            pl.BlockSpec((BF16_ROWS, D_MODEL), lambda i: (jnp.maximum(i * (tm // BF16_ROWS) - 1, 0), ublk)),
            pl.BlockSpec((GDN_CONV, 3 * W_MIX), lambda i: (0, 0)),
        ],
        out_specs=pl.BlockSpec((3, tm, W_MIX), lambda i: (0, i, 0)),
        out_shape=jax.ShapeDtypeStruct((3, n, W_MIX), BF16),
        scratch_shapes=[pltpu.VMEM((tm + BF16_ROWS, 3 * W_MIX), F32)],
        compiler_params=_cparams(("parallel",)),
        name="gdn_prep",
    )(u, u, conv_w)


def _gdn_chunk_kernel(x_ref, col_ref, nw_ref, o_ref, s_scr, *, nchunk, nbatch):
    @pl.when(pl.program_id(0) == 0)
    def _():
        s_scr[...] = jnp.zeros_like(s_scr)

    same, strict, incl, diag = _block_masks()
    eye = diag.astype(F32)
    lower = incl.astype(BF16)
    ones = same.astype(BF16)

    def head_col(col, lane0):
        return jnp.concatenate([col[:, lane0 + h:lane0 + h + 1] for h in range(N_HEADS)], axis=0)

    def chunk(ci, carry):
        sl = pl.ds(pl.multiple_of(ci * CHUNK, CHUNK), CHUNK)
        for b in range(nbatch):
            q_s, k_s, v_s = (_stack_heads(x_ref[i, b, sl, :], same) for i in range(3))
            col = col_ref[b, sl, :]
            beta = head_col(col, SM_BETA)
            g = head_col(col, SM_DEC)
            g_wide = jnp.broadcast_to(g, (W_MIX, LANES))
            gam = _mm_mask(lower, g_wide)[:, 0:1]
            gam_end = _mm_mask(ones, g_wide)[:, 0:1]
            gdiff = _mm_mask(lower, jnp.where(strict, g, 0.0))
            decay = jnp.exp(jnp.where(incl, gdiff, -jnp.inf))
            a_mat = jnp.where(strict, beta * decay * _mm_nt(k_s, k_s), 0.0)
            t_inv = _unit_lower_inverse(-a_mat, eye)
            e_gam = jnp.exp(gam)
            u0 = _mm(t_inv, beta * v_s.astype(F32))
            wm = _mm(t_inv, (beta * e_gam) * k_s.astype(F32))
            qk = _mm_nt(q_s, k_s) * decay
            q2 = e_gam * q_s.astype(F32) - _mm(qk, wm)
            o0 = _mm(qk, u0)
            k_end = k_s.astype(F32) * jnp.exp(gam_end - gam)
            s = s_scr[b]
            o = _mm(q2, s) + o0
            s_scr[b] = _mm(eye * jnp.exp(gam_end) - _mm_tn(k_end, wm), s) + _mm_tn(k_end, u0)
            ms = jnp.sum(o * o, axis=-1, keepdims=True) * (1.0 / HEAD_DIM)
            o_ref[b, sl, :] = _unstack_heads(o * lax.rsqrt(ms + NORM_EPS)) * nw_ref[...]
        return carry

    lax.fori_loop(0, nchunk, chunk, 0)


def _gdn_chunk(xs, col, norm_w, batch, seq, cblk):
    xs = xs.reshape(3, batch, seq, W_MIX)
    col = col.reshape(batch, seq, LANES)
    out = pl.pallas_call(
        functools.partial(_gdn_chunk_kernel, nchunk=cblk // CHUNK, nbatch=batch),
        grid=(seq // cblk,),
        in_specs=[
            pl.BlockSpec((3, batch, cblk, W_MIX), lambda j: (0, 0, j, 0)),
            pl.BlockSpec((batch, cblk, LANES), lambda j: (0, j, 0)),
            pl.BlockSpec((1, W_MIX), lambda j: (0, 0)),
        ],
        out_specs=pl.BlockSpec((batch, cblk, W_MIX), lambda j: (0, j, 0)),
        out_shape=jax.ShapeDtypeStruct((batch, seq, W_MIX), F32),
        scratch_shapes=[pltpu.VMEM((batch, W_MIX, W_MIX), F32)],
        compiler_params=_cparams(("arbitrary",)),
        name="gdn_chunk",
    )(xs, col, norm_w)
    return out.reshape(batch * seq, W_MIX)


def _causal_pairs(nq):
    pairs = [(i, j) for i in range(nq) for j in range(i + 1)]
    return jnp.asarray(np.array([p[0] for p in pairs], np.int32)), jnp.asarray(np.array([p[1] for p in pairs], np.int32))


def _softmax_updates(scores, vt_ones, m_scr, acc_scr):
    probs = []
    for i, s in enumerate(scores):
        m_old = m_scr[i]
        m_new = jnp.maximum(m_old, jnp.max(s, axis=0, keepdims=True))
        m_scr[i] = m_new
        probs.append((jnp.exp2(m_old - m_new), jnp.exp2((s - m_new).astype(BF16))))
    for i, (alpha, p) in enumerate(probs):
        acc_scr[i] = alpha * acc_scr[i] + jnp.dot(vt_ones[i], p, preferred_element_type=F32)


def _pair_lanes(h):
    p = h // 2
    return slice(p * LANES, (p + 1) * LANES), h % 2 == 0


def _values_and_ones(vt, tk):
    row = lax.broadcasted_iota(jnp.int32, (LANES, tk), 0)
    out = []
    for h in range(N_HEADS):
        slab, low = _pair_lanes(h)
        mine = row < HEAD_DIM if low else row >= HEAD_DIM
        out.append(jnp.where(mine, vt[slab, :], jnp.ones((), vt.dtype)))
    return out


def _normalized(acc, low):
    if low:
        return acc[0:HEAD_DIM] / acc[HEAD_DIM:HEAD_DIM + 1]
    return acc[HEAD_DIM:] / acc[0:1]


def _key_after_query(tq):
    return lax.broadcasted_iota(jnp.int32, (tq, tq), 0) > lax.broadcasted_iota(jnp.int32, (tq, tq), 1)


def _fox_kernel(qi_ref, kj_ref, q_ref, k_ref, vt_ref, crow_ref, crep_ref, o_ref, qm_scr, m_scr, acc_scr):
    t = pl.program_id(1)
    qi = qi_ref[t]
    kj = kj_ref[t]
    tq = q_ref.shape[0]

    @pl.when(kj == 0)
    def _():
        m_scr[...] = jnp.full_like(m_scr, -jnp.inf)
        acc_scr[...] = jnp.zeros_like(acc_scr)
        q = q_ref[...]
        lane = lax.broadcasted_iota(jnp.int32, (tq, LANES), 1)
        for h in range(N_HEADS):
            slab, low = _pair_lanes(h)
            mine = lane < HEAD_DIM if low else lane >= HEAD_DIM
            qm_scr[h] = jnp.where(mine, q[:, slab], jnp.zeros((), BF16))

    def step(diagonal):
        k = k_ref[...]
        vts = _values_and_ones(vt_ref[...], tq)
        cq = crow_ref[...]
        if diagonal:
            masked = _key_after_query(tq)
        scores = []
        for h in range(N_HEADS):
            slab, _ = _pair_lanes(h)
            s = lax.dot_general(k[:, slab], qm_scr[h], (((1,), (1,)), ((), ())), preferred_element_type=F32)
            ck = crep_ref[:, h * LANES:(h + 1) * LANES]
            s = (s - jnp.concatenate([ck] * (tq // LANES), axis=1)) + cq[SM_FOX + h:SM_FOX + h + 1, :]
            if diagonal:
                s = jnp.where(masked, -jnp.inf, s)
            scores.append(s)
        _softmax_updates(scores, vts, m_scr, acc_scr)

    @pl.when(kj < qi)
    def _():
        step(False)

    @pl.when(kj == qi)
    def _():
        step(True)
        for p in range(N_HEADS // 2):
            pair = jnp.concatenate([_normalized(acc_scr[2 * p], True), _normalized(acc_scr[2 * p + 1], False)], axis=0)
            o_ref[:, p * LANES:(p + 1) * LANES] = pair.T


def _attn_specs(nq, tq, ucol, vt_rows):
    cb = ucol // W_MIX
    q_spec = pl.BlockSpec((tq, W_MIX), lambda b, t, qi, kj: (b * nq + qi[t], cb))
    k_spec = pl.BlockSpec((tq, W_MIX), lambda b, t, qi, kj: (b * nq + kj[t], cb + 1))
    vt_spec = pl.BlockSpec((W_MIX, tq), lambda b, t, qi, kj: (vt_rows // W_MIX, b * nq + kj[t]))
    return q_spec, k_spec, vt_spec


def _fox_attention(u, vt, crow, crep, batch, seq, tq):
    nq = seq // tq
    qi, kj = _causal_pairs(nq)
    q_spec, k_spec, vt_spec = _attn_specs(nq, tq, U_FOX, W_MIX)
    stat = lambda: pltpu.VMEM((N_HEADS, 1, tq), F32)
    return pl.pallas_call(
        _fox_kernel,
        grid_spec=pltpu.PrefetchScalarGridSpec(
            num_scalar_prefetch=2,
            grid=(batch, qi.shape[0]),
            in_specs=[
                q_spec, k_spec, vt_spec,
                pl.BlockSpec((BF16_ROWS, tq), lambda b, t, qi, kj: (0, b * nq + qi[t])),
                pl.BlockSpec((tq, N_HEADS * LANES), lambda b, t, qi, kj: (b * nq + kj[t], 0)),
            ],
            out_specs=pl.BlockSpec((tq, W_MIX), lambda b, t, qi, kj: (b * nq + qi[t], 0)),
            scratch_shapes=[pltpu.VMEM((N_HEADS, tq, LANES), BF16), stat(),
                            pltpu.VMEM((N_HEADS, LANES, tq), F32)],
        ),
        out_shape=jax.ShapeDtypeStruct((batch * seq, W_MIX), F32),
        compiler_params=_cparams(("parallel", "arbitrary")),
        name="fox_attention",
    )(qi, kj, u, u, vt, crow, crep)


def _diff_kernel(qi_ref, kj_ref, q_ref, k_ref, vt_ref, cq_ref, sq_ref, ck_ref, sk_ref, rot_ref, lam_ref, ln_ref, o_ref,
                 qm_scr, m_scr, acc_scr, *, lam_init):
    t = pl.program_id(1)
    qi = qi_ref[t]
    kj = kj_ref[t]
    tq = q_ref.shape[0]
    lane = lax.broadcasted_iota(jnp.int32, (tq, LANES), 1)

    def rope(x, cos, sin):
        return x.astype(F32) * cos + jnp.dot(x, rot_ref[...], preferred_element_type=F32) * sin

    @pl.when(kj == 0)
    def _():
        m_scr[...] = jnp.full_like(m_scr, -jnp.inf)
        acc_scr[...] = jnp.zeros_like(acc_scr)
        q = rope(q_ref[...], cq_ref[...], sq_ref[...])
        for h in range(N_HEADS):
            slab, low = _pair_lanes(h)
            base = 0 if low else HEAD_DIM
            for c in range(2):
                lo = base + c * DIFF_DH
                sel = jnp.logical_and(lane >= lo, lane < lo + DIFF_DH)
                qm_scr[2 * h + c] = jnp.where(sel, q[:, slab], 0.0).astype(BF16)

    def step(diagonal):
        k = rope(k_ref[...], ck_ref[...], sk_ref[...]).astype(BF16)
        vts = _values_and_ones(vt_ref[...], tq)
        if diagonal:
            masked = _key_after_query(tq)
        scores = []
        for h in range(N_HEADS):
            slab, _ = _pair_lanes(h)
            for c in range(2):
                i = 2 * h + c
                s = lax.dot_general(k[:, slab], qm_scr[i], (((1,), (1,)), ((), ())), preferred_element_type=F32)
                if diagonal:
                    s = jnp.where(masked, -jnp.inf, s)
                scores.append(s)
        _softmax_updates(scores, [vts[i // 2] for i in range(2 * N_HEADS)], m_scr, acc_scr)

    @pl.when(kj < qi)
    def _():
        step(False)

    @pl.when(kj == qi)
    def _():
        step(True)
        lp = lam_ref[...]
        lam = (jnp.exp(jnp.sum(lp[0:1] * lp[1:2], axis=-1, keepdims=True))
               - jnp.exp(jnp.sum(lp[2:3] * lp[3:4], axis=-1, keepdims=True)) + lam_init)
        head = lambda h: (_normalized(acc_scr[2 * h], h % 2 == 0) - lam * _normalized(acc_scr[2 * h + 1], h % 2 == 0))
        is_lo = lane < HEAD_DIM
        for p in range(N_HEADS // 2):
            o = jnp.concatenate([head(2 * p), head(2 * p + 1)], axis=0).T
            sq = o * o
            ms_lo = jnp.sum(jnp.where(is_lo, sq, 0.0), axis=-1, keepdims=True)
            ms_hi = jnp.sum(jnp.where(is_lo, 0.0, sq), axis=-1, keepdims=True)
            ms = jnp.where(is_lo, ms_lo, ms_hi) * (1.0 / HEAD_DIM)
            o_ref[:, p * LANES:(p + 1) * LANES] = o * lax.rsqrt(ms + DIFF_LN_EPS) * ln_ref[...] * (1.0 - lam_init)


def _diff_attention(u, vt, cos, sin, rot, lam_p, subln, batch, seq, tq, lam_init):
    nq = seq // tq
    qi, kj = _causal_pairs(nq)
    q_spec, k_spec, vt_spec = _attn_specs(nq, tq, U_DIFF, 0)
    tab_q = pl.BlockSpec((tq, W_MIX), lambda b, t, qi, kj: (qi[t], 0))
    tab_k = pl.BlockSpec((tq, W_MIX), lambda b, t, qi, kj: (kj[t], 0))
    const = lambda r, c: pl.BlockSpec((r, c), lambda b, t, qi, kj: (0, 0))
    stat = lambda: pltpu.VMEM((2 * N_HEADS, 1, tq), F32)
    return pl.pallas_call(
        functools.partial(_diff_kernel, lam_init=lam_init),
        grid_spec=pltpu.PrefetchScalarGridSpec(
            num_scalar_prefetch=2,
            grid=(batch, qi.shape[0]),
            in_specs=[q_spec, k_spec, vt_spec, tab_q, tab_q, tab_k, tab_k,
                      const(W_MIX, W_MIX), const(4, DIFF_DH), const(1, LANES)],
            out_specs=pl.BlockSpec((tq, W_MIX), lambda b, t, qi, kj: (b * nq + qi[t], 0)),
            scratch_shapes=[pltpu.VMEM((2 * N_HEADS, tq, LANES), BF16), stat(),
                            pltpu.VMEM((2 * N_HEADS, LANES, tq), F32)],
        ),
        out_shape=jax.ShapeDtypeStruct((batch * seq, W_MIX), F32),
        compiler_params=_cparams(("parallel", "arbitrary")),
        name="diff_attention",
    )(qi, kj, u, u, vt, cos, sin, cos, sin, rot, lam_p, subln)


def _merge_kernel(x_ref, gate_ref, oa_ref, post_ref, yb_ref, yc_ref, od_ref, gd_ref, lnw_ref, lnb_ref,
                  wbo_ref, wout_ref, o_ref):
    y_a = (oa_ref[...] * lnw_ref[...] + lnb_ref[...] + post_ref[1]) * post_ref[0]
    y_d = od_ref[...] * _silu(gd_ref[...].astype(F32))
    acc = jnp.zeros(x_ref.shape, F32)
    for b, y in enumerate((y_a, yb_ref[...], yc_ref[...], y_d)):
        gate = _sigmoid(gate_ref[:, b * D_MODEL:(b + 1) * D_MODEL].astype(F32))
        acc = acc + gate * _mm(y, wbo_ref[b])
    o_ref[...] = x_ref[...] + _mm(acc, wout_ref[...])


def _merge(x, u, o_a, post, y_b, y_c, o_d, ln_w, ln_b, w_bo, w_out, tm):
    n = x.shape[0]
    tok = lambda c: pl.BlockSpec((tm, c), lambda i: (i, 0))
    return pl.pallas_call(
        _merge_kernel,
        grid=(n // tm,),
        in_specs=[
            tok(D_MODEL),
            pl.BlockSpec((tm, 4 * D_MODEL), lambda i: (i, U_GATE // (4 * D_MODEL))),
            tok(W_MIX),
            pl.BlockSpec((2, tm, W_MIX), lambda i: (0, i, 0)),
            tok(W_MIX), tok(W_MIX), tok(W_MIX),
            pl.BlockSpec((tm, W_MIX), lambda i: (i, (U_GDN + 3 * W_MIX) // W_MIX)),
            pl.BlockSpec((1, W_MIX), lambda i: (0, 0)),
            pl.BlockSpec((1, W_MIX), lambda i: (0, 0)),
            pl.BlockSpec((4, W_MIX, D_MODEL), lambda i: (0, 0, 0)),
            pl.BlockSpec((D_MODEL, D_MODEL), lambda i: (0, 0)),
        ],
        out_specs=tok(D_MODEL),
        out_shape=jax.ShapeDtypeStruct((n, D_MODEL), F32),
        compiler_params=_cparams(("parallel",)),
        name="merge",
    )(x, u, o_a, post, y_b, y_c, o_d, u, ln_w, ln_b, w_bo, w_out)


def _ffn_kernel(x_ref, g_ref, wg_ref, wu_ref, wd_ref, o_ref, h_scr, acc_scr):
    f = pl.program_id(1)

    @pl.when(f == 0)
    def _():
        h_scr[...] = _rms(x_ref[...], g_ref[...], NORM_EPS).astype(BF16)
        acc_scr[...] = jnp.zeros_like(acc_scr)

    h = h_scr[...]
    act = _silu(jnp.dot(h, wg_ref[...], preferred_element_type=F32)) * jnp.dot(h, wu_ref[...], preferred_element_type=F32)
    acc_scr[...] += _mm(act, wd_ref[...])

    @pl.when(f == pl.num_programs(1) - 1)
    def _():
        o_ref[...] = x_ref[...] + acc_scr[...]


def _ffn(x, g, wg, wu, wd, tm, tf):
    n = x.shape[0]
    ff = wg.shape[1]
    return pl.pallas_call(
        _ffn_kernel,
        grid=(n // tm, ff // tf),
        in_specs=[
            pl.BlockSpec((tm, D_MODEL), lambda i, f: (i, 0)),
            pl.BlockSpec((1, D_MODEL), lambda i, f: (0, 0)),
            pl.BlockSpec((D_MODEL, tf), lambda i, f: (0, f)),
            pl.BlockSpec((D_MODEL, tf), lambda i, f: (0, f)),
            pl.BlockSpec((tf, D_MODEL), lambda i, f: (f, 0)),
        ],
        out_specs=pl.BlockSpec((tm, D_MODEL), lambda i, f: (i, 0)),
        out_shape=jax.ShapeDtypeStruct((n, D_MODEL), F32),
        scratch_shapes=[pltpu.VMEM((tm, D_MODEL), BF16), pltpu.VMEM((tm, D_MODEL), F32)],
        compiler_params=_cparams(("parallel", "arbitrary")),
        name="ffn",
    )(x, g, wg, wu, wd)


def _router_kernel(x_ref, g_ref, router_ref, h_ref, c_ref, rc_ref, rr_ref, cnt_ref):
    t = x_ref.shape[0]
    h = _rms(x_ref[...], g_ref[...], NORM_EPS)
    h_ref[...] = h.astype(BF16)
    logits = _mm_f32(h, router_ref[...])
    lane = lax.broadcasted_iota(jnp.int32, logits.shape, 1).astype(F32)
    lg = jnp.where(lane < N_EXPERTS, logits, -jnp.inf)
    m1 = jnp.max(lg, axis=-1, keepdims=True)
    i1 = jnp.min(jnp.where(lg == m1, lane, float(LANES)), axis=-1, keepdims=True)
    lg2 = jnp.where(lane == i1, -jnp.inf, lg)
    m2 = jnp.max(lg2, axis=-1, keepdims=True)
    i2 = jnp.min(jnp.where(lg2 == m2, lane, float(LANES)), axis=-1, keepdims=True)
    e2 = jnp.exp(m2 - m1)
    c_ref[...] = jnp.where(lane == i1, 1.0 / (1.0 + e2), 0.0) + jnp.where(lane == i2, e2 / (1.0 + e2), 0.0)
    sel = jnp.logical_or(lane == i1, lane == i2)
    sel_f = jnp.where(sel, 1.0, 0.0)
    earlier = _tri_masks(t)[0].astype(BF16)
    rank = jnp.dot(earlier, sel_f.astype(BF16), preferred_element_type=F32)
    rc = jnp.where(sel, rank, -1.0)
    rc_ref[...] = rc
    rr_ref[...] = rc.T[0:BF16_ROWS, :]
    cnt_ref[...] = jnp.broadcast_to(jnp.sum(sel_f, axis=0, keepdims=True), cnt_ref.shape).astype(jnp.int32)


def _router(x, g, router, tm):
    n = x.shape[0]
    return pl.pallas_call(
        _router_kernel,
        grid=(n // tm,),
        in_specs=[
            pl.BlockSpec((tm, D_MODEL), lambda i: (i, 0)),
            pl.BlockSpec((1, D_MODEL), lambda i: (0, 0)),
            pl.BlockSpec((D_MODEL, LANES), lambda i: (0, 0)),
        ],
        out_specs=[
            pl.BlockSpec((tm, D_MODEL), lambda i: (i, 0)),
            pl.BlockSpec((tm, LANES), lambda i: (i, 0)),
            pl.BlockSpec((tm, LANES), lambda i: (i, 0)),
            pl.BlockSpec((BF16_ROWS, tm), lambda i: (0, i)),
            pl.BlockSpec((8, LANES), lambda i: (i, 0)),
        ],
        out_shape=[
            jax.ShapeDtypeStruct((n, D_MODEL), BF16),
            jax.ShapeDtypeStruct((n, LANES), F32),
            jax.ShapeDtypeStruct((n, LANES), F32),
            jax.ShapeDtypeStruct((BF16_ROWS, n), F32),
            jax.ShapeDtypeStruct((n // tm * 8, LANES), jnp.int32),
        ],
        compiler_params=_cparams(("parallel",)),
        name="router",
    )(x, g, router)


def _moe_kernel(cnt_ref, x_ref, h_ref, c_ref, rc_ref, rr_ref, wg_ref, wu_ref, wd_ref, o_ref, xg_scr, yg_scr, acc_scr,
                *, rows):
    i = pl.program_id(0)
    e = pl.program_id(1)
    f = pl.program_id(2)
    last_f = f == pl.num_programs(2) - 1
    t = h_ref.shape[0]
    nsub = xg_scr.shape[0] // rows
    cnt = cnt_ref[i * N_EXPERTS + e]
    blocks = [(sb, slice(sb * rows, (sb + 1) * rows)) for sb in range(nsub)]

    @pl.when(jnp.logical_and(e == 0, f == 0))
    def _():
        acc_scr[...] = jnp.zeros_like(acc_scr)

    @pl.when(f == 0)
    def _():
        rr = rr_ref[pl.ds(e, 1), :]
        for sb, rs in blocks:
            @pl.when(cnt > sb * rows)
            def _():
                slot = (lax.broadcasted_iota(jnp.int32, (rows, t), 0) + sb * rows).astype(F32)
                gather = jnp.where(rr == slot, 1.0, 0.0).astype(BF16)
                xg_scr[rs, :] = jnp.dot(gather, h_ref[...], preferred_element_type=F32).astype(BF16)
                yg_scr[rs, :] = jnp.zeros((rows, D_MODEL), F32)

    for sb, rs in blocks:
        @pl.when(cnt > sb * rows)
        def _():
            xb = xg_scr[rs, :]
            act = (_silu(jnp.dot(xb, wg_ref[0], preferred_element_type=F32))
                   * jnp.dot(xb, wu_ref[0], preferred_element_type=F32))
            yg_scr[rs, :] += _mm(act, wd_ref[0])

    @pl.when(last_f)
    def _():
        lane = lax.broadcasted_iota(jnp.int32, (t, LANES), 1)
        mine = lane == e
        rc = jnp.sum(jnp.where(mine, rc_ref[...], 0.0), axis=-1, keepdims=True)
        ce = jnp.sum(jnp.where(mine, c_ref[...], 0.0), axis=-1, keepdims=True)
        for sb, rs in blocks:
            @pl.when(cnt > sb * rows)
            def _():
                slot = (lax.broadcasted_iota(jnp.int32, (t, rows), 1) + sb * rows).astype(F32)
                scatter = jnp.where(rc == slot, 1.0, 0.0).astype(BF16)
                acc_scr[...] += ce * jnp.dot(scatter, yg_scr[rs, :].astype(BF16), preferred_element_type=F32)

    @pl.when(jnp.logical_and(e == pl.num_programs(1) - 1, last_f))
    def _():
        o_ref[...] = x_ref[...] + acc_scr[...]


def _moe(x, g, router, wg, wu, wd, tm, tf, rows):
    n = x.shape[0]
    ff = wg.shape[2]
    h, c, rc, rr, cnt = _router(x, g, router, tm)
    cnt = cnt.reshape(n // tm, 8, LANES)[:, 0, :N_EXPERTS].reshape(-1)
    nsub = -(-tm // rows)
    tok = lambda cols: pl.BlockSpec((tm, cols), lambda i, e, f, cnt: (i, 0))
    return pl.pallas_call(
        functools.partial(_moe_kernel, rows=rows),
        grid_spec=pltpu.PrefetchScalarGridSpec(
            num_scalar_prefetch=1,
            grid=(n // tm, N_EXPERTS, ff // tf),
            in_specs=[
                tok(D_MODEL), tok(D_MODEL), tok(LANES), tok(LANES),
                pl.BlockSpec((BF16_ROWS, tm), lambda i, e, f, cnt: (0, i)),
                pl.BlockSpec((1, D_MODEL, tf), lambda i, e, f, cnt: (e, 0, f)),
                pl.BlockSpec((1, D_MODEL, tf), lambda i, e, f, cnt: (e, 0, f)),
                pl.BlockSpec((1, tf, D_MODEL), lambda i, e, f, cnt: (e, f, 0)),
            ],
            out_specs=tok(D_MODEL),
            scratch_shapes=[pltpu.VMEM((nsub * rows, D_MODEL), BF16), pltpu.VMEM((nsub * rows, D_MODEL), F32),
                            pltpu.VMEM((tm, D_MODEL), F32)],
        ),
        out_shape=jax.ShapeDtypeStruct((n, D_MODEL), F32),
        compiler_params=_cparams(("parallel", "arbitrary", "arbitrary"), vmem_mb=56),
        name="moe",
    )(cnt, x, h, c, rc, rr, wg, wu, wd)


def _ple_kernel(x_ref, p_ref, g_ref, wgate_ref, wproj_ref, fin_ref, o_ref, *, final):
    x = x_ref[...]
    h = _rms(x, g_ref[...], NORM_EPS)
    y = x + _sigmoid(_mm(h, wgate_ref[...])) * _mm(p_ref[...], wproj_ref[...])
    o_ref[...] = _rms(y, fin_ref[...], NORM_EPS) if final else y


def _ple(x, p, g, wgate, wproj, fin, tm, final):
    n = x.shape[0]
    return pl.pallas_call(
        functools.partial(_ple_kernel, final=final),
        grid=(n // tm,),
        in_specs=[
            pl.BlockSpec((tm, D_MODEL), lambda i: (i, 0)),
            pl.BlockSpec((tm, P_DIM), lambda i: (i, 0)),
            pl.BlockSpec((1, D_MODEL), lambda i: (0, 0)),
            pl.BlockSpec((D_MODEL, D_MODEL), lambda i: (0, 0)),
            pl.BlockSpec((P_DIM, D_MODEL), lambda i: (0, 0)),
            pl.BlockSpec((1, D_MODEL), lambda i: (0, 0)),
        ],
        out_specs=pl.BlockSpec((tm, D_MODEL), lambda i: (i, 0)),
        out_shape=jax.ShapeDtypeStruct((n, D_MODEL), F32),
        compiler_params=_cparams(("parallel",)),
        name="ple",
    )(x, p, g, wgate, wproj, fin)


def _tiles(n, seq):
    tm = min(512, seq)
    tm_in = 1024 if n % 1024 == 0 else tm
    moe_rows = tm_in // 4 + tm_in // 32
    return dict(tm=tm, tm_in=tm_in, tn_in=1536, tq=min(512, seq), cblk=min(512, seq), moe_rows=moe_rows)


def _rope_tables(seq):
    half = ROPE_DIMS // 2
    inv = ROPE_THETA ** (-jnp.arange(half, dtype=F32) * 2.0 / ROPE_DIMS)
    ang = jnp.arange(seq, dtype=F32)[:, None] * inv[None, :]
    pad = jnp.zeros((seq, DIFF_DH - ROPE_DIMS), F32)
    cos = jnp.concatenate([jnp.cos(ang), jnp.cos(ang), pad + 1.0], axis=-1)
    sin = jnp.concatenate([jnp.sin(ang), jnp.sin(ang), pad], axis=-1)
    d = jnp.arange(W_MIX)
    dd = d % DIFF_DH
    src = jnp.where(dd < half, d + half, d - half)
    sign = jnp.where(dd < half, -1.0, jnp.where(dd < ROPE_DIMS, 1.0, 0.0))
    rot = jnp.zeros((W_MIX, W_MIX), F32).at[src, d].set(sign)
    reps = W_MIX // DIFF_DH
    return jnp.tile(cos, (1, reps)), jnp.tile(sin, (1, reps)), rot.astype(BF16)


def _split_w_in(w):
    a0 = 0
    b0 = a0 + 4 * W_MIX
    c0 = b0 + 3 * W_MIX
    d0 = c0 + 3 * W_MIX + N_HEADS
    g0 = d0 + 4 * W_MIX + 2 * N_HEADS
    d_small = d0 + 3 * W_MIX
    diff_q = w[:, b0:b0 + W_MIX] * (DIFF_DH ** -0.5 * LOG2E)
    fox_q = w[:, c0:c0 + W_MIX] * (HEAD_DIM ** -0.5 * LOG2E)
    main = jnp.concatenate([
        w[:, g0:], w[:, a0:b0], w[:, d0:d_small], w[:, d_small + 2 * N_HEADS:g0],
        diff_q, w[:, b0 + W_MIX:c0], fox_q, w[:, c0 + W_MIX:c0 + 3 * W_MIX],
    ], axis=1).astype(BF16)
    small = jnp.concatenate([
        w[:, c0 + 3 * W_MIX:d0], w[:, d_small:d_small + 2 * N_HEADS],
        jnp.zeros((D_MODEL, LANES - 3 * N_HEADS), w.dtype),
    ], axis=1).astype(BF16)
    v_t = jnp.concatenate([w[:, b0 + 2 * W_MIX:c0], w[:, c0 + 2 * W_MIX:c0 + 3 * W_MIX]], axis=1).T.astype(BF16)
    return main, small, small[:, :BF16_ROWS].T, v_t


def _small_params(fbias, a_log, dt_bias):
    zeros = jnp.zeros((N_HEADS,), F32)
    bias = jnp.concatenate([fbias, zeros, dt_bias, jnp.zeros((LANES - 3 * N_HEADS,), F32)])
    neg_a = jnp.concatenate([zeros, zeros, -jnp.exp(a_log), jnp.zeros((LANES - 3 * N_HEADS,), F32)])
    return (bias.reshape(1, LANES), bias[:BF16_ROWS].reshape(BF16_ROWS, 1),
            neg_a.reshape(1, LANES), neg_a[:BF16_ROWS].reshape(BF16_ROWS, 1))


def _pad_rows(w, top, total):
    return jnp.concatenate([jnp.zeros((top, w.shape[1]), w.dtype), w,
                            jnp.zeros((total - top - w.shape[0], w.shape[1]), w.dtype)], axis=0)


def kernel(x, p, norm_mix, norm_ffn, norm_ple, w_in, w_bo, w_out, rwkv_mu, rwkv_w0, rwkv_w2, rwkv_a0, rwkv_a2, rwkv_g2, rwkv_kk, rwkv_ka, rwkv_rk, rwkv_ln_w, rwkv_ln_b, diff_lam, diff_subln, fox_fbias, gdn_conv, gdn_a_log, gdn_dt_bias, gdn_norm, ffn_w_gate, ffn_w_up, ffn_w_down, moe_router, moe_w_gate, moe_w_up, moe_w_down, ple_proj, ple_gate, final_norm):
    batch, seq, _ = x.shape
    depth = w_in.shape[0]
    n = batch * seq
    t = _tiles(n, seq)
    tm, tq, cblk = t["tm"], t["tq"], t["cblk"]
    row = lambda v: v.reshape(1, -1).astype(F32)
    cos, sin, rot = _rope_tables(seq)
    xf = x.reshape(n, D_MODEL)
    pf = p.reshape(depth, n, P_DIM)

    for i in range(depth):
        w_main, w_small, w_small_t, w_vt = _split_w_in(w_in[i])
        u, scol, srow, vt = _inproj(xf, row(norm_mix[i]), w_main, w_small, w_small_t, w_vt, t["tm_in"], t["tn_in"])
        bcol, brow, acol, arow = _small_params(fox_fbias[i], gdn_a_log[i], gdn_dt_bias[i])
        hcol, hrow, hrep = _small_prep(scol, srow, bcol, brow, acol, arow, batch, seq, tm)

        scan_in, pc, post = _rwkv_prep(
            u, row(rwkv_mu[i]), row(rwkv_w0[i]), _pad_rows(rwkv_w2[i], 0, LANES), row(rwkv_a0[i]),
            _pad_rows(rwkv_a2[i], LANES // 2, LANES), rwkv_g2[i], row(rwkv_kk[i]), row(rwkv_ka[i]), row(rwkv_rk[i]),
            seq, tm)
        o_a = _rwkv_chunk(scan_in, pc, batch, seq, cblk)

        o_d = _gdn_chunk(_gdn_prep(u, gdn_conv[i].T, seq, tm), hcol, jnp.tile(row(gdn_norm[i]), (1, N_HEADS)),
                         batch, seq, cblk)

        lam_init = 0.8 - 0.6 * math.exp(-0.3 * i)
        y_b = _diff_attention(u, vt, cos, sin, rot, diff_lam[i].astype(F32), jnp.tile(row(diff_subln[i]), (1, 2)),
                              batch, seq, tq, lam_init)
        y_c = _fox_attention(u, vt, hrow, hrep, batch, seq, tq)

        xf = _merge(xf, u, o_a, post, y_b, y_c, o_d, row(rwkv_ln_w[i]), row(rwkv_ln_b[i]),
                    w_bo[i].astype(BF16), w_out[i].astype(BF16), tm)

        j = i // 2
        if i % 2 == 0:
            xf = _ffn(xf, row(norm_ffn[i]), ffn_w_gate[j].astype(BF16), ffn_w_up[j].astype(BF16),
                      ffn_w_down[j].astype(BF16), tm, ffn_w_gate.shape[2] // 2)
        else:
            router = jnp.concatenate([moe_router[j], jnp.zeros((D_MODEL, LANES - N_EXPERTS), F32)], axis=1)
            xf = _moe(xf, row(norm_ffn[i]), router, moe_w_gate[j].astype(BF16), moe_w_up[j].astype(BF16),
                      moe_w_down[j].astype(BF16), t["tm_in"], moe_w_gate.shape[3] // 4, t["moe_rows"])
        xf = _ple(xf, pf[i], row(norm_ple[i]), ple_gate[i].astype(BF16), ple_proj[i].astype(BF16),
                  row(final_norm), tm, i == depth - 1)
    return xf.reshape(batch, seq, D_MODEL)
```

```python
import functools
import math

import jax
import jax.numpy as jnp
import numpy as np
from jax import lax
from jax.experimental import pallas as pl
from jax.experimental.pallas import tpu as pltpu

F32 = jnp.float32
BF16 = jnp.bfloat16
HIGHEST = lax.Precision.HIGHEST

D_MODEL = 1024
P_DIM = 256
W_MIX = 256
HEAD_DIM = 64
N_HEADS = 4
DIFF_DH = 32
ROPE_THETA = 500000.0
ROPE_DIMS = 8
RWKV_GN_EPS = 64e-5
DIFF_LN_EPS = 1e-5
GDN_CONV = 4
CHUNK = 64
CHUNK_UNROLL = 2
N_EXPERTS = 8
NORM_EPS = 1e-6
L2_EPS = 1e-6
LOG2E = math.log2(math.e)
LANES = 128
BF16_ROWS = 16
assert CHUNK == HEAD_DIM

U_GATE = 0
U_RWKV = 4096
U_GDN = 5120
U_DIFF = 6144
U_FOX = 6912
U_COLS = 7680
SM_FOX, SM_BETA, SM_DEC = 0, 4, 8


def _cparams(semantics, vmem_mb=48):
    return pltpu.CompilerParams(dimension_semantics=semantics, vmem_limit_bytes=vmem_mb * 1024 * 1024)


def _mm(a, b):
    return jnp.dot(a.astype(BF16), b.astype(BF16), preferred_element_type=F32)


def _mm_nt(a, b):
    return lax.dot_general(a.astype(BF16), b.astype(BF16), (((1,), (1,)), ((), ())), preferred_element_type=F32)


def _mm_tn(a, b):
    return lax.dot_general(a.astype(BF16), b.astype(BF16), (((0,), (0,)), ((), ())), preferred_element_type=F32)


def _mm_f32(a, b):
    return jnp.dot(a, b, preferred_element_type=F32, precision=HIGHEST)


def _mm_mask(mask, x):
    hi = x.astype(BF16)
    r1 = x - hi.astype(F32)
    mid = r1.astype(BF16)
    lo = (r1 - mid.astype(F32)).astype(BF16)
    dot = lambda t: jnp.dot(mask, t, preferred_element_type=F32)
    return dot(hi) + dot(mid) + dot(lo)


def _rms(x, g, eps):
    return x * lax.rsqrt(jnp.mean(x * x, axis=-1, keepdims=True) + eps) * g


def _sigmoid(x):
    return 1.0 / (1.0 + jnp.exp(-x))


def _silu(x):
    return x * _sigmoid(x)


def _softplus(x):
    return jnp.maximum(x, 0.0) + jnp.log(1.0 + jnp.exp(-jnp.abs(x)))


def _tri_masks(c):
    ii = lax.broadcasted_iota(jnp.int32, (c, c), 0)
    jj = lax.broadcasted_iota(jnp.int32, (c, c), 1)
    return ii > jj, ii >= jj, ii == jj


def _block_masks():
    ii = lax.broadcasted_iota(jnp.int32, (W_MIX, W_MIX), 0)
    jj = lax.broadcasted_iota(jnp.int32, (W_MIX, W_MIX), 1)
    same = (ii // HEAD_DIM) == (jj // HEAD_DIM)
    return same, jnp.logical_and(same, ii > jj), jnp.logical_and(same, ii >= jj), ii == jj


def _unit_lower_inverses(ns, eye):
    rs = [eye + n for n in ns]
    ps = list(ns)
    for _ in range(int(math.log2(CHUNK)) - 1):
        ps = [_mm(p, p) for p in ps]
        rs = [r + _mm(r, p) for r, p in zip(rs, ps)]
    return rs


def _stack_heads(x, same):
    return jnp.where(same, jnp.concatenate([x, x, x, x], axis=0), jnp.zeros((), x.dtype))


def _unstack_heads(x):
    return x[0:CHUNK] + x[CHUNK:2 * CHUNK] + x[2 * CHUNK:3 * CHUNK] + x[3 * CHUNK:4 * CHUNK]


def _inproj_kernel(x_ref, g_ref, w_ref, ws_ref, wst_ref, wvt_ref, u_ref, scol_ref, srow_ref, vt_ref, h_scr):
    @pl.when(pl.program_id(1) == 0)
    def _():
        hb = _rms(x_ref[...], g_ref[...], NORM_EPS).astype(BF16)
        h_scr[...] = hb
        nt = (((1,), (1,)), ((), ()))
        scol_ref[...] = jnp.dot(hb, ws_ref[...], preferred_element_type=F32)
        srow_ref[...] = lax.dot_general(wst_ref[...], hb, nt, preferred_element_type=F32)
        vt_ref[...] = lax.dot_general(wvt_ref[...], hb, nt, preferred_element_type=F32).astype(BF16)

    u_ref[...] = jnp.dot(h_scr[...], w_ref[...], preferred_element_type=F32).astype(BF16)


def _inproj(x, g, w, ws, wst, wvt, tm, tn):
    n = x.shape[0]
    return pl.pallas_call(
        _inproj_kernel,
        grid=(n // tm, U_COLS // tn),
        in_specs=[
            pl.BlockSpec((tm, D_MODEL), lambda i, j: (i, 0)),
            pl.BlockSpec((1, D_MODEL), lambda i, j: (0, 0)),
            pl.BlockSpec((D_MODEL, tn), lambda i, j: (0, j)),
            pl.BlockSpec((D_MODEL, LANES), lambda i, j: (0, 0)),
            pl.BlockSpec((BF16_ROWS, D_MODEL), lambda i, j: (0, 0)),
            pl.BlockSpec((2 * W_MIX, D_MODEL), lambda i, j: (0, 0)),
        ],
        out_specs=[
            pl.BlockSpec((tm, tn), lambda i, j: (i, j)),
            pl.BlockSpec((tm, LANES), lambda i, j: (i, 0)),
            pl.BlockSpec((BF16_ROWS, tm), lambda i, j: (0, i)),
            pl.BlockSpec((2 * W_MIX, tm), lambda i, j: (0, i)),
        ],
        out_shape=[
            jax.ShapeDtypeStruct((n, U_COLS), BF16),
            jax.ShapeDtypeStruct((n, LANES), F32),
            jax.ShapeDtypeStruct((BF16_ROWS, n), F32),
            jax.ShapeDtypeStruct((2 * W_MIX, n), BF16),
        ],
        scratch_shapes=[pltpu.VMEM((tm, D_MODEL), BF16)],
        compiler_params=_cparams(("parallel", "arbitrary")),
        name="inproj",
    )(x, g, w, ws, wst, wvt)


def _small_prep_kernel(scol_ref, srow_ref, bcol_ref, brow_ref, acol_ref, arow_ref, ocol_ref, orow_ref, orep_ref,
                       ccol_scr, crow_scr):
    @pl.when(pl.program_id(1) == 0)
    def _():
        ccol_scr[...] = jnp.zeros_like(ccol_scr)
        crow_scr[...] = jnp.zeros_like(crow_scr)

    tm = scol_ref.shape[0]
    strict, incl, _ = _tri_masks(tm)
    lower = incl.astype(BF16)
    upper = jnp.logical_not(strict).astype(F32)

    def funcs(z, idx, neg_a):
        logf = jnp.minimum(z, 0.0) - jnp.log(1.0 + jnp.exp(-jnp.abs(z)))
        beta = _sigmoid(z)
        dec = neg_a * _softplus(z)
        is_f = idx < SM_BETA
        is_b = jnp.logical_and(idx >= SM_BETA, idx < SM_DEC)
        is_d = jnp.logical_and(idx >= SM_DEC, idx < SM_DEC + N_HEADS)
        return jnp.where(is_f, logf, 0.0), jnp.where(is_b, beta, jnp.where(is_d, dec, 0.0)), is_f

    zc = scol_ref[...] + bcol_ref[...]
    lane = lax.broadcasted_iota(jnp.int32, zc.shape, 1)
    logf_c, rest_c, is_f_c = funcs(zc, lane, acol_ref[...])
    cum_c = _mm_mask(lower, logf_c) + ccol_scr[...]
    ccol_scr[...] = cum_c[tm - 1:tm, :]
    ocol_ref[...] = rest_c
    for h in range(N_HEADS):
        orep_ref[:, h * LANES:(h + 1) * LANES] = jnp.broadcast_to(
            cum_c[:, SM_FOX + h:SM_FOX + h + 1] * LOG2E, (tm, LANES))

    zr = srow_ref[...] + brow_ref[...]
    sub = lax.broadcasted_iota(jnp.int32, zr.shape, 0)
    logf_r, _, _ = funcs(zr, sub, arow_ref[...])
    cum_r = _mm_f32(logf_r, upper) + crow_scr[...]
    crow_scr[...] = cum_r[:, tm - 1:tm]
    orow_ref[...] = cum_r * LOG2E


def _small_prep(scol, srow, bcol, brow, acol, arow, batch, seq, tm):
    n = batch * seq
    nt = seq // tm
    return pl.pallas_call(
        _small_prep_kernel,
        grid=(batch, nt),
        in_specs=[
            pl.BlockSpec((tm, LANES), lambda b, j: (b * nt + j, 0)),
            pl.BlockSpec((BF16_ROWS, tm), lambda b, j: (0, b * nt + j)),
            pl.BlockSpec((1, LANES), lambda b, j: (0, 0)),
            pl.BlockSpec((BF16_ROWS, 1), lambda b, j: (0, 0)),
            pl.BlockSpec((1, LANES), lambda b, j: (0, 0)),
            pl.BlockSpec((BF16_ROWS, 1), lambda b, j: (0, 0)),
        ],
        out_specs=[
            pl.BlockSpec((tm, LANES), lambda b, j: (b * nt + j, 0)),
            pl.BlockSpec((BF16_ROWS, tm), lambda b, j: (0, b * nt + j)),
            pl.BlockSpec((tm, N_HEADS * LANES), lambda b, j: (b * nt + j, 0)),
        ],
        out_shape=[jax.ShapeDtypeStruct((n, LANES), F32), jax.ShapeDtypeStruct((BF16_ROWS, n), F32),
                   jax.ShapeDtypeStruct((n, N_HEADS * LANES), F32)],
        scratch_shapes=[pltpu.VMEM((1, LANES), F32), pltpu.VMEM((BF16_ROWS, 1), F32)],
        compiler_params=_cparams(("parallel", "arbitrary")),
        name="small_prep",
    )(scol, srow, bcol, brow, acol, arow)


def _rwkv_prep_kernel(u_ref, up_ref, mu_ref, w0_ref, w2_ref, a0_ref, a2_ref, g2_ref, kk_ref, ka_ref, rk_ref,
                      scan_ref, pc_ref, post_ref, *, tiles_per_seq):
    tm = u_ref.shape[0]
    u = u_ref[...].astype(F32)
    prev = up_ref[...].astype(F32)[BF16_ROWS - 1:BF16_ROWS, :]
    prev = jnp.where(pl.program_id(0) % tiles_per_seq == 0, 0.0, prev)
    rows = lax.broadcasted_iota(jnp.int32, (tm, 1), 0)
    u_prev = jnp.where(rows == 0, prev, pltpu.roll(u, 1, 0))
    xm = u + (u_prev - u) * mu_ref[...]
    r = xm[:, 0:W_MIX]
    k = xm[:, W_MIX:2 * W_MIX]
    v = xm[:, 2 * W_MIX:3 * W_MIX]
    x_lora = xm[:, 3 * W_MIX:3 * W_MIX + LANES]
    xg = xm[:, 3 * W_MIX + LANES:]
    logw = -_softplus(-(w0_ref[...] + _mm(jnp.tanh(x_lora), w2_ref[...]))) - 0.5
    log_decay = -jnp.exp(logw)
    a = _sigmoid(a0_ref[...] + _mm(x_lora, a2_ref[...]))
    g = _mm(_sigmoid(xg), g2_ref[...])
    same = _block_masks()[0].astype(F32)
    kk_raw = k * kk_ref[...]
    kk = kk_raw * lax.rsqrt(_mm_f32(kk_raw * kk_raw, same) + L2_EPS)
    k2 = k * (1.0 + (a - 1.0) * ka_ref[...])
    bonus = _mm_f32(r * k2 * rk_ref[...], same) * v
    ti = lax.broadcasted_iota(jnp.int32, (tm, tm), 0)
    tj = lax.broadcasted_iota(jnp.int32, (tm, tm), 1)
    in_chunk = jnp.logical_and(ti // CHUNK == tj // CHUNK, ti >= tj).astype(BF16)
    ci = lax.broadcasted_iota(jnp.int32, (tm // CHUNK, tm), 0)
    cj = lax.broadcasted_iota(jnp.int32, (tm // CHUNK, tm), 1)
    cum = _mm_mask(in_chunk, log_decay)
    cum_end = _mm_mask((ci == cj // CHUNK).astype(BF16), log_decay)
    inv = jnp.exp(-cum)
    scan_ref[0] = (-kk * jnp.exp(cum - log_decay)).astype(BF16)
    scan_ref[1] = (kk * a * inv).astype(BF16)
    scan_ref[2] = (k2 * inv).astype(BF16)
    scan_ref[3] = (r * jnp.exp(cum)).astype(BF16)
    scan_ref[4] = v.astype(BF16)
    pc_ref[...] = jnp.exp(cum_end)
    post_ref[0] = g
    post_ref[1] = bonus


def _rwkv_prep(u, mu, w0, w2p, a0, a2p, g2, k_k, k_a, r_k, seq, tm):
    n = u.shape[0]
    ublk = U_RWKV // D_MODEL
    row = lambda c: pl.BlockSpec((1, c), lambda i: (0, 0))
    mat = lambda r: pl.BlockSpec((r, W_MIX), lambda i: (0, 0))
    return pl.pallas_call(
        functools.partial(_rwkv_prep_kernel, tiles_per_seq=seq // tm),
        grid=(n // tm,),
        in_specs=[
            pl.BlockSpec((tm, D_MODEL), lambda i: (i, ublk)),
            pl.BlockSpec((BF16_ROWS, D_MODEL), lambda i: (jnp.maximum(i * (tm // BF16_ROWS) - 1, 0), ublk)),
            row(D_MODEL), row(W_MIX), mat(LANES), row(W_MIX), mat(LANES), mat(LANES), row(W_MIX), row(W_MIX), row(W_MIX),
        ],
        out_specs=[
            pl.BlockSpec((5, tm, W_MIX), lambda i: (0, i, 0)),
            pl.BlockSpec((tm // CHUNK, W_MIX), lambda i: (i, 0)),
            pl.BlockSpec((2, tm, W_MIX), lambda i: (0, i, 0)),
        ],
        out_shape=[jax.ShapeDtypeStruct((5, n, W_MIX), BF16), jax.ShapeDtypeStruct((n // CHUNK, W_MIX), F32),
                   jax.ShapeDtypeStruct((2, n, W_MIX), F32)],
        compiler_params=_cparams(("parallel",)),
        name="rwkv_prep",
    )(u, u, mu, w0, w2p, a0, a2p, g2, k_k, k_a, r_k)


def _rwkv_chunk_kernel(x_ref, pc_ref, o_ref, s_scr, *, nchunk, nbatch):
    @pl.when(pl.program_id(0) == 0)
    def _():
        s_scr[...] = jnp.zeros_like(s_scr)

    same, strict, incl, diag = _block_masks()
    eye = diag.astype(F32)

    def chunks(i, carry):
        items = [(b, i * CHUNK_UNROLL + j) for j in range(CHUNK_UNROLL) for b in range(nbatch)]
        each = lambda f, *lists: [f(*args) for args in zip(*lists)]
        sls = [pl.ds(pl.multiple_of(ci * CHUNK, CHUNK), CHUNK) for _, ci in items]
        a_s, b_s, k_s, r_s, v_s = ([_stack_heads(x_ref[i, b, sl, :], same) for (b, _), sl in zip(items, sls)]
                                   for i in range(5))
        pc = [pc_ref[b, pl.ds(ci, 1), :] for b, ci in items]
        m_ab = each(lambda a, b: jnp.where(strict, _mm_nt(a, b), 0.0), a_s, b_s)
        m_ak = each(lambda a, k: jnp.where(strict, _mm_nt(a, k), 0.0), a_s, k_s)
        n_rb = each(lambda r, b: jnp.where(incl, _mm_nt(r, b), 0.0), r_s, b_s)
        n_rk = each(lambda r, k: jnp.where(incl, _mm_nt(r, k), 0.0), r_s, k_s)
        t_inv = _unit_lower_inverses(m_ab, eye)
        a2 = each(_mm, t_inv, a_s)
        u0 = each(_mm, t_inv, each(_mm, m_ak, v_s))
        r2 = each(lambda r, n, a: r.astype(F32) + _mm(n, a), r_s, n_rb, a2)
        o0 = each(lambda n1, u, n2, v: _mm(n1, u) + _mm(n2, v), n_rb, u0, n_rk, v_s)
        b_end = each(lambda b, p: b.astype(F32) * p, b_s, pc)
        k_end = each(lambda k, p: k.astype(F32) * p, k_s, pc)
        g_mat = each(lambda p, a, b: eye * p + _mm_tn(a, b), pc, a2, b_end)
        s0 = each(lambda u, b, v, k: _mm_tn(u, b) + _mm_tn(v, k), u0, b_end, v_s, k_end)
        for n, ((b, _), sl) in enumerate(zip(items, sls)):
            s = s_scr[b]
            o = _mm_nt(r2[n], s) + o0[n]
            s_scr[b] = _mm(s, g_mat[n]) + s0[n]
            mean = jnp.sum(o, axis=-1, keepdims=True) * (1.0 / HEAD_DIM)
            cen = jnp.where(same, o - mean, 0.0)
            var = jnp.sum(cen * cen, axis=-1, keepdims=True) * (1.0 / HEAD_DIM)
            o_ref[b, sl, :] = _unstack_heads(cen * lax.rsqrt(var + RWKV_GN_EPS))
        return carry

    lax.fori_loop(0, nchunk // CHUNK_UNROLL, chunks, 0)


def _rwkv_chunk(xs, pc, batch, seq, cblk):
    xs = xs.reshape(5, batch, seq, W_MIX)
    pc = pc.reshape(batch, seq // CHUNK, W_MIX)
    out = pl.pallas_call(
        functools.partial(_rwkv_chunk_kernel, nchunk=cblk // CHUNK, nbatch=batch),
        grid=(seq // cblk,),
        in_specs=[
            pl.BlockSpec((5, batch, cblk, W_MIX), lambda j: (0, 0, j, 0)),
            pl.BlockSpec((batch, cblk // CHUNK, W_MIX), lambda j: (0, j, 0)),
        ],
        out_specs=pl.BlockSpec((batch, cblk, W_MIX), lambda j: (0, j, 0)),
        out_shape=jax.ShapeDtypeStruct((batch, seq, W_MIX), F32),
        scratch_shapes=[pltpu.VMEM((batch, W_MIX, W_MIX), F32)],
        compiler_params=_cparams(("arbitrary",)),
        name="rwkv_chunk",
    )(xs, pc)
    return out.reshape(batch * seq, W_MIX)


def _gdn_prep_kernel(u_ref, up_ref, cw_ref, o_ref, ext_scr, *, tiles_per_seq):
    tm = u_ref.shape[0]
    c3 = 3 * W_MIX
    prev = up_ref[...].astype(F32)[:, :c3]
    ext_scr[0:BF16_ROWS, :] = jnp.where(pl.program_id(0) % tiles_per_seq == 0, 0.0, prev)
    ext_scr[BF16_ROWS:, :] = u_ref[...].astype(F32)[:, :c3]
    y = jnp.zeros((tm, c3), F32)
    for j in range(GDN_CONV):
        y = y + ext_scr[pl.ds(BF16_ROWS - (GDN_CONV - 1) + j, tm), :] * cw_ref[j:j + 1, :]
    y = _silu(y)
    same = _block_masks()[0].astype(F32)
    q = y[:, 0:W_MIX]
    k = y[:, W_MIX:2 * W_MIX]
    o_ref[0] = (q * lax.rsqrt(_mm_f32(q * q, same) + L2_EPS) * (HEAD_DIM ** -0.5)).astype(BF16)
    o_ref[1] = (k * lax.rsqrt(_mm_f32(k * k, same) + L2_EPS)).astype(BF16)
    o_ref[2] = y[:, 2 * W_MIX:].astype(BF16)


def _gdn_prep(u, conv_w, seq, tm):
    n = u.shape[0]
    ublk = U_GDN // D_MODEL
    return pl.pallas_call(
        functools.partial(_gdn_prep_kernel, tiles_per_seq=seq // tm),
        grid=(n // tm,),
        in_specs=[
            pl.BlockSpec((tm, D_MODEL), lambda i: (i, ublk)),
            pl.BlockSpec((BF16_ROWS, D_MODEL), lambda i: (jnp.maximum(i * (tm // BF16_ROWS) - 1, 0), ublk)),
            pl.BlockSpec((GDN_CONV, 3 * W_MIX), lambda i: (0, 0)),
        ],
        out_specs=pl.BlockSpec((3, tm, W_MIX), lambda i: (0, i, 0)),
        out_shape=jax.ShapeDtypeStruct((3, n, W_MIX), BF16),
        scratch_shapes=[pltpu.VMEM((tm + BF16_ROWS, 3 * W_MIX), F32)],
        compiler_params=_cparams(("parallel",)),
        name="gdn_prep",
    )(u, u, conv_w)


def _gdn_chunk_kernel(x_ref, col_ref, nw_ref, o_ref, s_scr, *, nchunk, nbatch):
    @pl.when(pl.program_id(0) == 0)
    def _():
        s_scr[...] = jnp.zeros_like(s_scr)

    same, strict, incl, diag = _block_masks()
    eye = diag.astype(F32)
    lower = incl.astype(BF16)
    ones = same.astype(BF16)

    def head_col(col, lane0):
        return jnp.concatenate([col[:, lane0 + h:lane0 + h + 1] for h in range(N_HEADS)], axis=0)

    def chunks(i, carry):
        items = [(b, i * CHUNK_UNROLL + j) for j in range(CHUNK_UNROLL) for b in range(nbatch)]
        each = lambda f, *lists: [f(*args) for args in zip(*lists)]
        sls = [pl.ds(pl.multiple_of(ci * CHUNK, CHUNK), CHUNK) for _, ci in items]
        q_s, k_s, v_s = ([_stack_heads(x_ref[i, b, sl, :], same) for (b, _), sl in zip(items, sls)] for i in range(3))
        cols = [col_ref[b, sl, :] for (b, _), sl in zip(items, sls)]
        beta = [head_col(c, SM_BETA) for c in cols]
        g = [head_col(c, SM_DEC) for c in cols]
        g_wide = [jnp.broadcast_to(x, (W_MIX, LANES)) for x in g]
        gam = [_mm_mask(lower, x)[:, 0:1] for x in g_wide]
        gam_end = [_mm_mask(ones, x)[:, 0:1] for x in g_wide]
        gdiff = [_mm_mask(lower, jnp.where(strict, x, 0.0)) for x in g]
        decay = [jnp.exp(jnp.where(incl, x, -jnp.inf)) for x in gdiff]
        a_mat = each(lambda bt, d, k: jnp.where(strict, bt * d * _mm_nt(k, k), 0.0), beta, decay, k_s)
        t_inv = _unit_lower_inverses([-a for a in a_mat], eye)
        e_gam = [jnp.exp(x) for x in gam]
        u0 = each(lambda t, bt, v: _mm(t, bt * v.astype(F32)), t_inv, beta, v_s)
        wm = each(lambda t, bt, e, k: _mm(t, (bt * e) * k.astype(F32)), t_inv, beta, e_gam, k_s)
        qk = each(lambda q, k, d: _mm_nt(q, k) * d, q_s, k_s, decay)
        q2 = each(lambda e, q, a, w: e * q.astype(F32) - _mm(a, w), e_gam, q_s, qk, wm)
        o0 = each(_mm, qk, u0)
        k_end = each(lambda k, ge, ga: k.astype(F32) * jnp.exp(ge - ga), k_s, gam_end, gam)
        g_mat = each(lambda ge, k, w: eye * jnp.exp(ge) - _mm_tn(k, w), gam_end, k_end, wm)
        s0 = each(_mm_tn, k_end, u0)
        for n, ((b, _), sl) in enumerate(zip(items, sls)):
            s = s_scr[b]
            o = _mm(q2[n], s) + o0[n]
            s_scr[b] = _mm(g_mat[n], s) + s0[n]
            ms = jnp.sum(o * o, axis=-1, keepdims=True) * (1.0 / HEAD_DIM)
            o_ref[b, sl, :] = _unstack_heads(o * lax.rsqrt(ms + NORM_EPS)) * nw_ref[...]
        return carry

    lax.fori_loop(0, nchunk // CHUNK_UNROLL, chunks, 0)


def _gdn_chunk(xs, col, norm_w, batch, seq, cblk):
    xs = xs.reshape(3, batch, seq, W_MIX)
    col = col.reshape(batch, seq, LANES)
    out = pl.pallas_call(
        functools.partial(_gdn_chunk_kernel, nchunk=cblk // CHUNK, nbatch=batch),
        grid=(seq // cblk,),
        in_specs=[
            pl.BlockSpec((3, batch, cblk, W_MIX), lambda j: (0, 0, j, 0)),
            pl.BlockSpec((batch, cblk, LANES), lambda j: (0, j, 0)),
            pl.BlockSpec((1, W_MIX), lambda j: (0, 0)),
        ],
        out_specs=pl.BlockSpec((batch, cblk, W_MIX), lambda j: (0, j, 0)),
        out_shape=jax.ShapeDtypeStruct((batch, seq, W_MIX), F32),
        scratch_shapes=[pltpu.VMEM((batch, W_MIX, W_MIX), F32)],
        compiler_params=_cparams(("arbitrary",)),
        name="gdn_chunk",
    )(xs, col, norm_w)
    return out.reshape(batch * seq, W_MIX)


def _causal_pairs(nq):
    pairs = [(i, j) for i in range(nq) for j in range(i + 1)]
    return jnp.asarray(np.array([p[0] for p in pairs], np.int32)), jnp.asarray(np.array([p[1] for p in pairs], np.int32))


def _softmax_updates(scores, vt_ones, m_scr, acc_scr):
    probs = []
    for i, s in enumerate(scores):
        m_old = m_scr[i]
        m_new = jnp.maximum(m_old, jnp.max(s, axis=0, keepdims=True))
        m_scr[i] = m_new
        probs.append((jnp.exp2(m_old - m_new), jnp.exp2((s - m_new).astype(BF16))))
    for i, (alpha, p) in enumerate(probs):
        acc_scr[i] = alpha * acc_scr[i] + jnp.dot(vt_ones[i], p, preferred_element_type=F32)


def _pair_lanes(h):
    p = h // 2
    return slice(p * LANES, (p + 1) * LANES), h % 2 == 0


def _values_and_ones(vt, tk):
    row = lax.broadcasted_iota(jnp.int32, (LANES, tk), 0)
    out = []
    for h in range(N_HEADS):
        slab, low = _pair_lanes(h)
        mine = row < HEAD_DIM if low else row >= HEAD_DIM
        out.append(jnp.where(mine, vt[slab, :], jnp.ones((), vt.dtype)))
    return out


def _normalized(acc, low):
    if low:
        return acc[0:HEAD_DIM] / acc[HEAD_DIM:HEAD_DIM + 1]
    return acc[HEAD_DIM:] / acc[0:1]


def _key_after_query(tq):
    return lax.broadcasted_iota(jnp.int32, (tq, tq), 0) > lax.broadcasted_iota(jnp.int32, (tq, tq), 1)


def _fox_kernel(qi_ref, kj_ref, q_ref, k_ref, vt_ref, crow_ref, crep_ref, o_ref, qm_scr, m_scr, acc_scr):
    t = pl.program_id(1)
    qi = qi_ref[t]
    kj = kj_ref[t]
    tq = q_ref.shape[0]

    @pl.when(kj == 0)
    def _():
        m_scr[...] = jnp.full_like(m_scr, -jnp.inf)
        acc_scr[...] = jnp.zeros_like(acc_scr)
        q = q_ref[...]
        lane = lax.broadcasted_iota(jnp.int32, (tq, LANES), 1)
        for h in range(N_HEADS):
            slab, low = _pair_lanes(h)
            mine = lane < HEAD_DIM if low else lane >= HEAD_DIM
            qm_scr[h] = jnp.where(mine, q[:, slab], jnp.zeros((), BF16))

    def step(diagonal):
        k = k_ref[...]
        vts = _values_and_ones(vt_ref[...], tq)
        cq = crow_ref[...]
        if diagonal:
            masked = _key_after_query(tq)
        scores = []
        for h in range(N_HEADS):
            slab, _ = _pair_lanes(h)
            s = lax.dot_general(k[:, slab], qm_scr[h], (((1,), (1,)), ((), ())), preferred_element_type=F32)
            ck = crep_ref[:, h * LANES:(h + 1) * LANES]
            s = (s - jnp.concatenate([ck] * (tq // LANES), axis=1)) + cq[SM_FOX + h:SM_FOX + h + 1, :]
            if diagonal:
                s = jnp.where(masked, -jnp.inf, s)
            scores.append(s)
        _softmax_updates(scores, vts, m_scr, acc_scr)

    @pl.when(kj < qi)
    def _():
        step(False)

    @pl.when(kj == qi)
    def _():
        step(True)
        for p in range(N_HEADS // 2):
            pair = jnp.concatenate([_normalized(acc_scr[2 * p], True), _normalized(acc_scr[2 * p + 1], False)], axis=0)
            o_ref[:, p * LANES:(p + 1) * LANES] = pair.T


def _attn_specs(nq, tq, ucol, vt_rows):
    cb = ucol // W_MIX
    q_spec = pl.BlockSpec((tq, W_MIX), lambda b, t, qi, kj: (b * nq + qi[t], cb))
    k_spec = pl.BlockSpec((tq, W_MIX), lambda b, t, qi, kj: (b * nq + kj[t], cb + 1))
    vt_spec = pl.BlockSpec((W_MIX, tq), lambda b, t, qi, kj: (vt_rows // W_MIX, b * nq + kj[t]))
    return q_spec, k_spec, vt_spec


def _fox_attention(u, vt, crow, crep, batch, seq, tq):
    nq = seq // tq
    qi, kj = _causal_pairs(nq)
    q_spec, k_spec, vt_spec = _attn_specs(nq, tq, U_FOX, W_MIX)
    stat = lambda: pltpu.VMEM((N_HEADS, 1, tq), F32)
    return pl.pallas_call(
        _fox_kernel,
        grid_spec=pltpu.PrefetchScalarGridSpec(
            num_scalar_prefetch=2,
            grid=(batch, qi.shape[0]),
            in_specs=[
                q_spec, k_spec, vt_spec,
                pl.BlockSpec((BF16_ROWS, tq), lambda b, t, qi, kj: (0, b * nq + qi[t])),
                pl.BlockSpec((tq, N_HEADS * LANES), lambda b, t, qi, kj: (b * nq + kj[t], 0)),
            ],
            out_specs=pl.BlockSpec((tq, W_MIX), lambda b, t, qi, kj: (b * nq + qi[t], 0)),
            scratch_shapes=[pltpu.VMEM((N_HEADS, tq, LANES), BF16), stat(),
                            pltpu.VMEM((N_HEADS, LANES, tq), F32)],
        ),
        out_shape=jax.ShapeDtypeStruct((batch * seq, W_MIX), F32),
        compiler_params=_cparams(("parallel", "arbitrary")),
        name="fox_attention",
    )(qi, kj, u, u, vt, crow, crep)


def _diff_kernel(qi_ref, kj_ref, q_ref, k_ref, vt_ref, cq_ref, sq_ref, ck_ref, sk_ref, rot_ref, lam_ref, ln_ref, o_ref,
                 qm_scr, m_scr, acc_scr, *, lam_init):
    t = pl.program_id(1)
    qi = qi_ref[t]
    kj = kj_ref[t]
    tq = q_ref.shape[0]
    lane = lax.broadcasted_iota(jnp.int32, (tq, LANES), 1)

    def rope(x, cos, sin):
        return x.astype(F32) * cos + jnp.dot(x, rot_ref[...], preferred_element_type=F32) * sin

    @pl.when(kj == 0)
    def _():
        m_scr[...] = jnp.full_like(m_scr, -jnp.inf)
        acc_scr[...] = jnp.zeros_like(acc_scr)
        q = rope(q_ref[...], cq_ref[...], sq_ref[...])
        for h in range(N_HEADS):
            slab, low = _pair_lanes(h)
            base = 0 if low else HEAD_DIM
            for c in range(2):
                lo = base + c * DIFF_DH
                sel = jnp.logical_and(lane >= lo, lane < lo + DIFF_DH)
                qm_scr[2 * h + c] = jnp.where(sel, q[:, slab], 0.0).astype(BF16)

    def step(diagonal):
        k = rope(k_ref[...], ck_ref[...], sk_ref[...]).astype(BF16)
        vts = _values_and_ones(vt_ref[...], tq)
        if diagonal:
            masked = _key_after_query(tq)
        scores = []
        for h in range(N_HEADS):
            slab, _ = _pair_lanes(h)
            for c in range(2):
                i = 2 * h + c
                s = lax.dot_general(k[:, slab], qm_scr[i], (((1,), (1,)), ((), ())), preferred_element_type=F32)
                if diagonal:
                    s = jnp.where(masked, -jnp.inf, s)
                scores.append(s)
        _softmax_updates(scores, [vts[i // 2] for i in range(2 * N_HEADS)], m_scr, acc_scr)

    @pl.when(kj < qi)
    def _():
        step(False)

    @pl.when(kj == qi)
    def _():
        step(True)
        lp = lam_ref[...]
        lam = (jnp.exp(jnp.sum(lp[0:1] * lp[1:2], axis=-1, keepdims=True))
               - jnp.exp(jnp.sum(lp[2:3] * lp[3:4], axis=-1, keepdims=True)) + lam_init)
        head = lambda h: (_normalized(acc_scr[2 * h], h % 2 == 0) - lam * _normalized(acc_scr[2 * h + 1], h % 2 == 0))
        is_lo = lane < HEAD_DIM
        for p in range(N_HEADS // 2):
            o = jnp.concatenate([head(2 * p), head(2 * p + 1)], axis=0).T
            sq = o * o
            ms_lo = jnp.sum(jnp.where(is_lo, sq, 0.0), axis=-1, keepdims=True)
            ms_hi = jnp.sum(jnp.where(is_lo, 0.0, sq), axis=-1, keepdims=True)
            ms = jnp.where(is_lo, ms_lo, ms_hi) * (1.0 / HEAD_DIM)
            o_ref[:, p * LANES:(p + 1) * LANES] = o * lax.rsqrt(ms + DIFF_LN_EPS) * ln_ref[...] * (1.0 - lam_init)


def _diff_attention(u, vt, cos, sin, rot, lam_p, subln, batch, seq, tq, lam_init):
    nq = seq // tq
    qi, kj = _causal_pairs(nq)
    q_spec, k_spec, vt_spec = _attn_specs(nq, tq, U_DIFF, 0)
    tab_q = pl.BlockSpec((tq, W_MIX), lambda b, t, qi, kj: (qi[t], 0))
    tab_k = pl.BlockSpec((tq, W_MIX), lambda b, t, qi, kj: (kj[t], 0))
    const = lambda r, c: pl.BlockSpec((r, c), lambda b, t, qi, kj: (0, 0))
    stat = lambda: pltpu.VMEM((2 * N_HEADS, 1, tq), F32)
    return pl.pallas_call(
        functools.partial(_diff_kernel, lam_init=lam_init),
        grid_spec=pltpu.PrefetchScalarGridSpec(
            num_scalar_prefetch=2,
            grid=(batch, qi.shape[0]),
            in_specs=[q_spec, k_spec, vt_spec, tab_q, tab_q, tab_k, tab_k,
                      const(W_MIX, W_MIX), const(4, DIFF_DH), const(1, LANES)],
            out_specs=pl.BlockSpec((tq, W_MIX), lambda b, t, qi, kj: (b * nq + qi[t], 0)),
            scratch_shapes=[pltpu.VMEM((2 * N_HEADS, tq, LANES), BF16), stat(),
                            pltpu.VMEM((2 * N_HEADS, LANES, tq), F32)],
        ),
        out_shape=jax.ShapeDtypeStruct((batch * seq, W_MIX), F32),
        compiler_params=_cparams(("parallel", "arbitrary")),
        name="diff_attention",
    )(qi, kj, u, u, vt, cos, sin, cos, sin, rot, lam_p, subln)


def _merge_kernel(x_ref, gate_ref, oa_ref, post_ref, yb_ref, yc_ref, od_ref, gd_ref, lnw_ref, lnb_ref,
                  wbo_ref, wout_ref, o_ref):
    y_a = (oa_ref[...] * lnw_ref[...] + lnb_ref[...] + post_ref[1]) * post_ref[0]
    y_d = od_ref[...] * _silu(gd_ref[...].astype(F32))
    acc = jnp.zeros(x_ref.shape, F32)
    for b, y in enumerate((y_a, yb_ref[...], yc_ref[...], y_d)):
        gate = _sigmoid(gate_ref[:, b * D_MODEL:(b + 1) * D_MODEL].astype(F32))
        acc = acc + gate * _mm(y, wbo_ref[b])
    o_ref[...] = x_ref[...] + _mm(acc, wout_ref[...])


def _merge(x, u, o_a, post, y_b, y_c, o_d, ln_w, ln_b, w_bo, w_out, tm):
    n = x.shape[0]
    tok = lambda c: pl.BlockSpec((tm, c), lambda i: (i, 0))
    return pl.pallas_call(
        _merge_kernel,
        grid=(n // tm,),
        in_specs=[
            tok(D_MODEL),
            pl.BlockSpec((tm, 4 * D_MODEL), lambda i: (i, U_GATE // (4 * D_MODEL))),
            tok(W_MIX),
            pl.BlockSpec((2, tm, W_MIX), lambda i: (0, i, 0)),
            tok(W_MIX), tok(W_MIX), tok(W_MIX),
            pl.BlockSpec((tm, W_MIX), lambda i: (i, (U_GDN + 3 * W_MIX) // W_MIX)),
            pl.BlockSpec((1, W_MIX), lambda i: (0, 0)),
            pl.BlockSpec((1, W_MIX), lambda i: (0, 0)),
            pl.BlockSpec((4, W_MIX, D_MODEL), lambda i: (0, 0, 0)),
            pl.BlockSpec((D_MODEL, D_MODEL), lambda i: (0, 0)),
        ],
        out_specs=tok(D_MODEL),
        out_shape=jax.ShapeDtypeStruct((n, D_MODEL), F32),
        compiler_params=_cparams(("parallel",)),
        name="merge",
    )(x, u, o_a, post, y_b, y_c, o_d, u, ln_w, ln_b, w_bo, w_out)


def _ffn_kernel(x_ref, g_ref, wg_ref, wu_ref, wd_ref, o_ref, h_scr, acc_scr):
    f = pl.program_id(1)

    @pl.when(f == 0)
    def _():
        h_scr[...] = _rms(x_ref[...], g_ref[...], NORM_EPS).astype(BF16)
        acc_scr[...] = jnp.zeros_like(acc_scr)

    h = h_scr[...]
    act = _silu(jnp.dot(h, wg_ref[...], preferred_element_type=F32)) * jnp.dot(h, wu_ref[...], preferred_element_type=F32)
    acc_scr[...] += _mm(act, wd_ref[...])

    @pl.when(f == pl.num_programs(1) - 1)
    def _():
        o_ref[...] = x_ref[...] + acc_scr[...]


def _ffn(x, g, wg, wu, wd, tm, tf):
    n = x.shape[0]
    ff = wg.shape[1]
    return pl.pallas_call(
        _ffn_kernel,
        grid=(n // tm, ff // tf),
        in_specs=[
            pl.BlockSpec((tm, D_MODEL), lambda i, f: (i, 0)),
            pl.BlockSpec((1, D_MODEL), lambda i, f: (0, 0)),
            pl.BlockSpec((D_MODEL, tf), lambda i, f: (0, f)),
            pl.BlockSpec((D_MODEL, tf), lambda i, f: (0, f)),
            pl.BlockSpec((tf, D_MODEL), lambda i, f: (f, 0)),
        ],
        out_specs=pl.BlockSpec((tm, D_MODEL), lambda i, f: (i, 0)),
        out_shape=jax.ShapeDtypeStruct((n, D_MODEL), F32),
        scratch_shapes=[pltpu.VMEM((tm, D_MODEL), BF16), pltpu.VMEM((tm, D_MODEL), F32)],
        compiler_params=_cparams(("parallel", "arbitrary")),
        name="ffn",
    )(x, g, wg, wu, wd)


def _router_kernel(x_ref, g_ref, router_ref, h_ref, c_ref, rc_ref, rr_ref, cnt_ref):
    t = x_ref.shape[0]
    h = _rms(x_ref[...], g_ref[...], NORM_EPS)
    h_ref[...] = h.astype(BF16)
    logits = _mm_f32(h, router_ref[...])
    lane = lax.broadcasted_iota(jnp.int32, logits.shape, 1).astype(F32)
    lg = jnp.where(lane < N_EXPERTS, logits, -jnp.inf)
    m1 = jnp.max(lg, axis=-1, keepdims=True)
    i1 = jnp.min(jnp.where(lg == m1, lane, float(LANES)), axis=-1, keepdims=True)
    lg2 = jnp.where(lane == i1, -jnp.inf, lg)
    m2 = jnp.max(lg2, axis=-1, keepdims=True)
    i2 = jnp.min(jnp.where(lg2 == m2, lane, float(LANES)), axis=-1, keepdims=True)
    e2 = jnp.exp(m2 - m1)
    c_ref[...] = jnp.where(lane == i1, 1.0 / (1.0 + e2), 0.0) + jnp.where(lane == i2, e2 / (1.0 + e2), 0.0)
    sel = jnp.logical_or(lane == i1, lane == i2)
    sel_f = jnp.where(sel, 1.0, 0.0)
    earlier = _tri_masks(t)[0].astype(BF16)
    rank = jnp.dot(earlier, sel_f.astype(BF16), preferred_element_type=F32)
    rc = jnp.where(sel, rank, -1.0)
    rc_ref[...] = rc
    rr_ref[...] = rc.T[0:BF16_ROWS, :]
    cnt_ref[...] = jnp.broadcast_to(jnp.sum(sel_f, axis=0, keepdims=True), cnt_ref.shape).astype(jnp.int32)


def _router(x, g, router, tm):
    n = x.shape[0]
    return pl.pallas_call(
        _router_kernel,
        grid=(n // tm,),
        in_specs=[
            pl.BlockSpec((tm, D_MODEL), lambda i: (i, 0)),
            pl.BlockSpec((1, D_MODEL), lambda i: (0, 0)),
            pl.BlockSpec((D_MODEL, LANES), lambda i: (0, 0)),
        ],
        out_specs=[
            pl.BlockSpec((tm, D_MODEL), lambda i: (i, 0)),
            pl.BlockSpec((tm, LANES), lambda i: (i, 0)),
            pl.BlockSpec((tm, LANES), lambda i: (i, 0)),
            pl.BlockSpec((BF16_ROWS, tm), lambda i: (0, i)),
            pl.BlockSpec((8, LANES), lambda i: (i, 0)),
        ],
        out_shape=[
            jax.ShapeDtypeStruct((n, D_MODEL), BF16),
            jax.ShapeDtypeStruct((n, LANES), F32),
            jax.ShapeDtypeStruct((n, LANES), F32),
            jax.ShapeDtypeStruct((BF16_ROWS, n), F32),
            jax.ShapeDtypeStruct((n // tm * 8, LANES), jnp.int32),
        ],
        compiler_params=_cparams(("parallel",)),
        name="router",
    )(x, g, router)


def _moe_kernel(cnt_ref, x_ref, h_ref, c_ref, rc_ref, rr_ref, wg_ref, wu_ref, wd_ref, o_ref, xg_scr, yg_scr, *, rows):
    i = pl.program_id(0)
    e = pl.program_id(1)
    f = pl.program_id(2)
    last_f = f == pl.num_programs(2) - 1
    t = h_ref.shape[0]
    nsub = xg_scr.shape[0] // rows
    cnt = cnt_ref[i * N_EXPERTS + e]
    blocks = [(sb, slice(sb * rows, (sb + 1) * rows)) for sb in range(nsub)]

    @pl.when(jnp.logical_and(e == 0, f == 0))
    def _():
        o_ref[...] = x_ref[...]

    @pl.when(f == 0)
    def _():
        rr = rr_ref[pl.ds(e, 1), :]
        for sb, rs in blocks:
            @pl.when(cnt > sb * rows)
            def _():
                slot = (lax.broadcasted_iota(jnp.int32, (rows, t), 0) + sb * rows).astype(F32)
                gather = jnp.where(rr == slot, 1.0, 0.0).astype(BF16)
                xg_scr[rs, :] = jnp.dot(gather, h_ref[...], preferred_element_type=F32).astype(BF16)
                yg_scr[rs, :] = jnp.zeros((rows, D_MODEL), F32)

    for sb, rs in blocks:
        @pl.when(cnt > sb * rows)
        def _():
            xb = xg_scr[rs, :]
            act = (_silu(jnp.dot(xb, wg_ref[0], preferred_element_type=F32))
                   * jnp.dot(xb, wu_ref[0], preferred_element_type=F32))
            yg_scr[rs, :] += _mm(act, wd_ref[0])

    @pl.when(last_f)
    def _():
        lane = lax.broadcasted_iota(jnp.int32, (t, LANES), 1)
        mine = lane == e
        rc = jnp.sum(jnp.where(mine, rc_ref[...], 0.0), axis=-1, keepdims=True)
        ce = jnp.sum(jnp.where(mine, c_ref[...], 0.0), axis=-1, keepdims=True)
        for sb, rs in blocks:
            @pl.when(cnt > sb * rows)
            def _():
                slot = (lax.broadcasted_iota(jnp.int32, (t, rows), 1) + sb * rows).astype(F32)
                scatter = jnp.where(rc == slot, 1.0, 0.0).astype(BF16)
                o_ref[...] += ce * jnp.dot(scatter, yg_scr[rs, :].astype(BF16), preferred_element_type=F32)


def _moe(x, g, router, wg, wu, wd, tm, tf, rows):
    n = x.shape[0]
    ff = wg.shape[2]
    h, c, rc, rr, cnt = _router(x, g, router, tm)
    cnt = cnt.reshape(n // tm, 8, LANES)[:, 0, :N_EXPERTS].reshape(-1)
    nsub = -(-tm // rows)
    tok = lambda cols, **kw: pl.BlockSpec((tm, cols), lambda i, e, f, cnt: (i, 0), **kw)
    once = dict(pipeline_mode=pl.Buffered(1))
    return pl.pallas_call(
        functools.partial(_moe_kernel, rows=rows),
        grid_spec=pltpu.PrefetchScalarGridSpec(
            num_scalar_prefetch=1,
            grid=(n // tm, N_EXPERTS, ff // tf),
            in_specs=[
                tok(D_MODEL, **once), tok(D_MODEL, **once), tok(LANES), tok(LANES),
                pl.BlockSpec((BF16_ROWS, tm), lambda i, e, f, cnt: (0, i)),
                pl.BlockSpec((1, D_MODEL, tf), lambda i, e, f, cnt: (e, 0, f)),
                pl.BlockSpec((1, D_MODEL, tf), lambda i, e, f, cnt: (e, 0, f)),
                pl.BlockSpec((1, tf, D_MODEL), lambda i, e, f, cnt: (e, f, 0)),
            ],
            out_specs=tok(D_MODEL),
            scratch_shapes=[pltpu.VMEM((nsub * rows, D_MODEL), BF16), pltpu.VMEM((nsub * rows, D_MODEL), F32)],
        ),
        out_shape=jax.ShapeDtypeStruct((n, D_MODEL), F32),
        compiler_params=_cparams(("parallel", "arbitrary", "arbitrary"), vmem_mb=56),
        name="moe",
    )(cnt, x, h, c, rc, rr, wg, wu, wd)


def _ple_kernel(x_ref, p_ref, g_ref, wgate_ref, wproj_ref, fin_ref, o_ref, *, final):
    x = x_ref[...]
    h = _rms(x, g_ref[...], NORM_EPS)
    y = x + _sigmoid(_mm(h, wgate_ref[...])) * _mm(p_ref[...], wproj_ref[...])
    o_ref[...] = _rms(y, fin_ref[...], NORM_EPS) if final else y


def _ple(x, p, g, wgate, wproj, fin, tm, final):
    n = x.shape[0]
    return pl.pallas_call(
        functools.partial(_ple_kernel, final=final),
        grid=(n // tm,),
        in_specs=[
            pl.BlockSpec((tm, D_MODEL), lambda i: (i, 0)),
            pl.BlockSpec((tm, P_DIM), lambda i: (i, 0)),
            pl.BlockSpec((1, D_MODEL), lambda i: (0, 0)),
            pl.BlockSpec((D_MODEL, D_MODEL), lambda i: (0, 0)),
            pl.BlockSpec((P_DIM, D_MODEL), lambda i: (0, 0)),
            pl.BlockSpec((1, D_MODEL), lambda i: (0, 0)),
        ],
        out_specs=pl.BlockSpec((tm, D_MODEL), lambda i: (i, 0)),
        out_shape=jax.ShapeDtypeStruct((n, D_MODEL), F32),
        compiler_params=_cparams(("parallel",)),
        name="ple",
    )(x, p, g, wgate, wproj, fin)


def _tiles(n, seq):
    tm = min(512, seq)
    tm_in = 1024 if n % 1024 == 0 else tm
    moe_rows = tm_in // 4 + tm_in // 32
    return dict(tm=tm, tm_in=tm_in, tn_in=1536, tq=min(512, seq), cblk=min(512, seq), moe_rows=moe_rows)


def _rope_tables(seq):
    half = ROPE_DIMS // 2
    inv = ROPE_THETA ** (-jnp.arange(half, dtype=F32) * 2.0 / ROPE_DIMS)
    ang = jnp.arange(seq, dtype=F32)[:, None] * inv[None, :]
    pad = jnp.zeros((seq, DIFF_DH - ROPE_DIMS), F32)
    cos = jnp.concatenate([jnp.cos(ang), jnp.cos(ang), pad + 1.0], axis=-1)
    sin = jnp.concatenate([jnp.sin(ang), jnp.sin(ang), pad], axis=-1)
    d = jnp.arange(W_MIX)
    dd = d % DIFF_DH
    src = jnp.where(dd < half, d + half, d - half)
    sign = jnp.where(dd < half, -1.0, jnp.where(dd < ROPE_DIMS, 1.0, 0.0))
    rot = jnp.zeros((W_MIX, W_MIX), F32).at[src, d].set(sign)
    reps = W_MIX // DIFF_DH
    return jnp.tile(cos, (1, reps)), jnp.tile(sin, (1, reps)), rot.astype(BF16)


def _split_w_in(w):
    a0 = 0
    b0 = a0 + 4 * W_MIX
    c0 = b0 + 3 * W_MIX
    d0 = c0 + 3 * W_MIX + N_HEADS
    g0 = d0 + 4 * W_MIX + 2 * N_HEADS
    d_small = d0 + 3 * W_MIX
    diff_q = w[:, b0:b0 + W_MIX] * (DIFF_DH ** -0.5 * LOG2E)
    fox_q = w[:, c0:c0 + W_MIX] * (HEAD_DIM ** -0.5 * LOG2E)
    main = jnp.concatenate([
        w[:, g0:], w[:, a0:b0], w[:, d0:d_small], w[:, d_small + 2 * N_HEADS:g0],
        diff_q, w[:, b0 + W_MIX:c0], fox_q, w[:, c0 + W_MIX:c0 + 3 * W_MIX],
    ], axis=1).astype(BF16)
    small = jnp.concatenate([
        w[:, c0 + 3 * W_MIX:d0], w[:, d_small:d_small + 2 * N_HEADS],
        jnp.zeros((D_MODEL, LANES - 3 * N_HEADS), w.dtype),
    ], axis=1).astype(BF16)
    v_t = jnp.concatenate([w[:, b0 + 2 * W_MIX:c0], w[:, c0 + 2 * W_MIX:c0 + 3 * W_MIX]], axis=1).T.astype(BF16)
    return main, small, small[:, :BF16_ROWS].T, v_t


def _small_params(fbias, a_log, dt_bias):
    zeros = jnp.zeros((N_HEADS,), F32)
    bias = jnp.concatenate([fbias, zeros, dt_bias, jnp.zeros((LANES - 3 * N_HEADS,), F32)])
    neg_a = jnp.concatenate([zeros, zeros, -jnp.exp(a_log), jnp.zeros((LANES - 3 * N_HEADS,), F32)])
    return (bias.reshape(1, LANES), bias[:BF16_ROWS].reshape(BF16_ROWS, 1),
            neg_a.reshape(1, LANES), neg_a[:BF16_ROWS].reshape(BF16_ROWS, 1))


def _pad_rows(w, top, total):
    return jnp.concatenate([jnp.zeros((top, w.shape[1]), w.dtype), w,
                            jnp.zeros((total - top - w.shape[0], w.shape[1]), w.dtype)], axis=0)


def kernel(x, p, norm_mix, norm_ffn, norm_ple, w_in, w_bo, w_out, rwkv_mu, rwkv_w0, rwkv_w2, rwkv_a0, rwkv_a2, rwkv_g2, rwkv_kk, rwkv_ka, rwkv_rk, rwkv_ln_w, rwkv_ln_b, diff_lam, diff_subln, fox_fbias, gdn_conv, gdn_a_log, gdn_dt_bias, gdn_norm, ffn_w_gate, ffn_w_up, ffn_w_down, moe_router, moe_w_gate, moe_w_up, moe_w_down, ple_proj, ple_gate, final_norm):
    batch, seq, _ = x.shape
    depth = w_in.shape[0]
    n = batch * seq
    t = _tiles(n, seq)
    tm, tq, cblk = t["tm"], t["tq"], t["cblk"]
    row = lambda v: v.reshape(1, -1).astype(F32)
    cos, sin, rot = _rope_tables(seq)
    xf = x.reshape(n, D_MODEL)
    pf = p.reshape(depth, n, P_DIM)

    for i in range(depth):
        w_main, w_small, w_small_t, w_vt = _split_w_in(w_in[i])
        u, scol, srow, vt = _inproj(xf, row(norm_mix[i]), w_main, w_small, w_small_t, w_vt, t["tm_in"], t["tn_in"])
        bcol, brow, acol, arow = _small_params(fox_fbias[i], gdn_a_log[i], gdn_dt_bias[i])
        hcol, hrow, hrep = _small_prep(scol, srow, bcol, brow, acol, arow, batch, seq, tm)

        scan_in, pc, post = _rwkv_prep(
            u, row(rwkv_mu[i]), row(rwkv_w0[i]), _pad_rows(rwkv_w2[i], 0, LANES), row(rwkv_a0[i]),
            _pad_rows(rwkv_a2[i], LANES // 2, LANES), rwkv_g2[i], row(rwkv_kk[i]), row(rwkv_ka[i]), row(rwkv_rk[i]),
            seq, tm)
        o_a = _rwkv_chunk(scan_in, pc, batch, seq, cblk)

        o_d = _gdn_chunk(_gdn_prep(u, gdn_conv[i].T, seq, tm), hcol, jnp.tile(row(gdn_norm[i]), (1, N_HEADS)),
                         batch, seq, cblk)

        lam_init = 0.8 - 0.6 * math.exp(-0.3 * i)
        y_b = _diff_attention(u, vt, cos, sin, rot, diff_lam[i].astype(F32), jnp.tile(row(diff_subln[i]), (1, 2)),
                              batch, seq, tq, lam_init)
        y_c = _fox_attention(u, vt, hrow, hrep, batch, seq, tq)

        xf = _merge(xf, u, o_a, post, y_b, y_c, o_d, row(rwkv_ln_w[i]), row(rwkv_ln_b[i]),
                    w_bo[i].astype(BF16), w_out[i].astype(BF16), tm)

        j = i // 2
        if i % 2 == 0:
            xf = _ffn(xf, row(norm_ffn[i]), ffn_w_gate[j].astype(BF16), ffn_w_up[j].astype(BF16),
                      ffn_w_down[j].astype(BF16), tm, ffn_w_gate.shape[2] // 2)
        else:
            router = jnp.concatenate([moe_router[j], jnp.zeros((D_MODEL, LANES - N_EXPERTS), F32)], axis=1)
            xf = _moe(xf, row(norm_ffn[i]), router, moe_w_gate[j].astype(BF16), moe_w_up[j].astype(BF16),
                      moe_w_down[j].astype(BF16), t["tm_in"], moe_w_gate.shape[3] // 2, t["moe_rows"])
        xf = _ple(xf, pf[i], row(norm_ple[i]), ple_gate[i].astype(BF16), ple_proj[i].astype(BF16),
                  row(final_norm), tm, i == depth - 1)
    return xf.reshape(batch, seq, D_MODEL)
```

```python
import functools
import math

import jax
import jax.numpy as jnp
import numpy as np
from jax import lax
from jax.experimental import pallas as pl
from jax.experimental.pallas import tpu as pltpu

F32 = jnp.float32
BF16 = jnp.bfloat16
HIGHEST = lax.Precision.HIGHEST

D_MODEL = 1024
P_DIM = 256
W_MIX = 256
HEAD_DIM = 64
N_HEADS = 4
DIFF_DH = 32
ROPE_THETA = 500000.0
ROPE_DIMS = 8
RWKV_GN_EPS = 64e-5
DIFF_LN_EPS = 1e-5
GDN_CONV = 4
CHUNK = 64
CHUNK_UNROLL = 2
N_EXPERTS = 8
NORM_EPS = 1e-6
L2_EPS = 1e-6
LOG2E = math.log2(math.e)
LANES = 128
BF16_ROWS = 16
assert CHUNK == HEAD_DIM

U_GATE = 0
U_RWKV = 4096
U_GDN = 5120
U_DIFF = 6144
U_FOX = 6656
U_COLS = 7168
SM_FOX, SM_BETA, SM_DEC = 0, 4, 8
FOX_CK, FOX_CQ = 0, 16


def _cparams(semantics, vmem_mb=48):
    return pltpu.CompilerParams(dimension_semantics=semantics, vmem_limit_bytes=vmem_mb * 1024 * 1024)


def _mm(a, b):
    return jnp.dot(a.astype(BF16), b.astype(BF16), preferred_element_type=F32)


def _mm_nt(a, b):
    return lax.dot_general(a.astype(BF16), b.astype(BF16), (((1,), (1,)), ((), ())), preferred_element_type=F32)


def _mm_tn(a, b):
    return lax.dot_general(a.astype(BF16), b.astype(BF16), (((0,), (0,)), ((), ())), preferred_element_type=F32)


def _mm_f32(a, b):
    return jnp.dot(a, b, preferred_element_type=F32, precision=HIGHEST)


def _mm_mask(mask, x):
    hi = x.astype(BF16)
    r1 = x - hi.astype(F32)
    mid = r1.astype(BF16)
    lo = (r1 - mid.astype(F32)).astype(BF16)
    dot = lambda t: jnp.dot(mask, t, preferred_element_type=F32)
    return dot(hi) + dot(mid) + dot(lo)


def _rms(x, g, eps):
    return x * lax.rsqrt(jnp.mean(x * x, axis=-1, keepdims=True) + eps) * g


def _sigmoid(x):
    return 1.0 / (1.0 + jnp.exp(-x))


def _silu(x):
    return x * _sigmoid(x)


def _softplus(x):
    return jnp.maximum(x, 0.0) + jnp.log(1.0 + jnp.exp(-jnp.abs(x)))


def _tri_masks(c):
    ii = lax.broadcasted_iota(jnp.int32, (c, c), 0)
    jj = lax.broadcasted_iota(jnp.int32, (c, c), 1)
    return ii > jj, ii >= jj, ii == jj


def _block_masks():
    ii = lax.broadcasted_iota(jnp.int32, (W_MIX, W_MIX), 0)
    jj = lax.broadcasted_iota(jnp.int32, (W_MIX, W_MIX), 1)
    same = (ii // HEAD_DIM) == (jj // HEAD_DIM)
    return same, jnp.logical_and(same, ii > jj), jnp.logical_and(same, ii >= jj), ii == jj


def _unit_lower_inverses(ns, eye):
    rs = [eye + n for n in ns]
    ps = list(ns)
    for _ in range(int(math.log2(CHUNK)) - 1):
        ps = [_mm(p, p) for p in ps]
        rs = [r + _mm(r, p) for r, p in zip(rs, ps)]
    return rs


def _stack_heads(x, same):
    return jnp.where(same, jnp.concatenate([x, x, x, x], axis=0), jnp.zeros((), x.dtype))


def _unstack_heads(x):
    return x[0:CHUNK] + x[CHUNK:2 * CHUNK] + x[2 * CHUNK:3 * CHUNK] + x[3 * CHUNK:4 * CHUNK]


def _inproj_kernel(x_ref, g_ref, w_ref, ws_ref, wvt_ref, u_ref, scol_ref, vt_ref, h_scr):
    @pl.when(pl.program_id(1) == 0)
    def _():
        hb = _rms(x_ref[...], g_ref[...], NORM_EPS).astype(BF16)
        h_scr[...] = hb
        scol_ref[...] = jnp.dot(hb, ws_ref[...], preferred_element_type=F32)
        vt_ref[...] = lax.dot_general(wvt_ref[...], hb, (((1,), (1,)), ((), ())),
                                      preferred_element_type=F32).astype(BF16)

    u_ref[...] = jnp.dot(h_scr[...], w_ref[...], preferred_element_type=F32).astype(BF16)


def _inproj(x, g, w, ws, wvt, tm, tn):
    n = x.shape[0]
    return pl.pallas_call(
        _inproj_kernel,
        grid=(n // tm, U_COLS // tn),
        in_specs=[
            pl.BlockSpec((tm, D_MODEL), lambda i, j: (i, 0)),
            pl.BlockSpec((1, D_MODEL), lambda i, j: (0, 0)),
            pl.BlockSpec((D_MODEL, tn), lambda i, j: (0, j)),
            pl.BlockSpec((D_MODEL, LANES), lambda i, j: (0, 0)),
            pl.BlockSpec((2 * W_MIX, D_MODEL), lambda i, j: (0, 0)),
        ],
        out_specs=[
            pl.BlockSpec((tm, tn), lambda i, j: (i, j)),
            pl.BlockSpec((tm, LANES), lambda i, j: (i, 0)),
            pl.BlockSpec((2 * W_MIX, tm), lambda i, j: (0, i)),
        ],
        out_shape=[
            jax.ShapeDtypeStruct((n, U_COLS), BF16),
            jax.ShapeDtypeStruct((n, LANES), F32),
            jax.ShapeDtypeStruct((2 * W_MIX, n), BF16),
        ],
        scratch_shapes=[pltpu.VMEM((tm, D_MODEL), BF16)],
        compiler_params=_cparams(("parallel", "arbitrary"), vmem_mb=56),
        name="inproj",
    )(x, g, w, ws, wvt)


def _lane_placement(base):
    r = lax.broadcasted_iota(jnp.int32, (3 * LANES, LANES), 0)
    m = lax.broadcasted_iota(jnp.int32, (3 * LANES, LANES), 1)
    head, part = r % LANES, r // LANES
    return jnp.logical_and(head < N_HEADS, m == base + 3 * head + part).astype(BF16)


def _small_prep_kernel(scol_ref, bias_ref, nega_ref, ocol_ref, okb_ref, oqb_ref, carry_scr):
    @pl.when(pl.program_id(1) == 0)
    def _():
        carry_scr[...] = jnp.zeros_like(carry_scr)

    tm = scol_ref.shape[0]
    lower = _tri_masks(tm)[1].astype(BF16)
    z = scol_ref[...] + bias_ref[...]
    lane = lax.broadcasted_iota(jnp.int32, z.shape, 1)
    is_f = lane < SM_BETA
    is_b = jnp.logical_and(lane >= SM_BETA, lane < SM_DEC)
    is_d = jnp.logical_and(lane >= SM_DEC, lane < SM_DEC + N_HEADS)
    logf = jnp.where(is_f, jnp.minimum(z, 0.0) - jnp.log(1.0 + jnp.exp(-jnp.abs(z))), 0.0)
    ocol_ref[...] = jnp.where(is_b, _sigmoid(z), jnp.where(is_d, nega_ref[...] * _softplus(z), 0.0))
    cum = _mm_mask(lower, logf) + carry_scr[...]
    carry_scr[...] = cum[tm - 1:tm, :]
    c2 = cum * LOG2E
    hi = c2.astype(BF16)
    r1 = c2 - hi.astype(F32)
    mid = r1.astype(BF16)
    lo = (r1 - mid.astype(F32)).astype(BF16)
    parts = jnp.concatenate([hi, mid, lo], axis=1)
    ones_lanes = jnp.logical_and(lane >= FOX_CQ, lane < FOX_CQ + 3 * N_HEADS)
    okb_ref[...] = (jnp.dot(parts, _lane_placement(FOX_CK), preferred_element_type=F32)
                    + jnp.where(ones_lanes, 1.0, 0.0)).astype(BF16)
    oqb_ref[...] = jnp.dot(parts, _lane_placement(FOX_CQ), preferred_element_type=F32).astype(BF16)


def _small_prep(scol, bias, neg_a, batch, seq, tm):
    n = batch * seq
    nt = seq // tm
    tok = lambda: pl.BlockSpec((tm, LANES), lambda b, j: (b * nt + j, 0))
    const = lambda: pl.BlockSpec((1, LANES), lambda b, j: (0, 0))
    return pl.pallas_call(
        _small_prep_kernel,
        grid=(batch, nt),
        in_specs=[tok(), const(), const()],
        out_specs=[tok(), tok(), tok()],
        out_shape=[jax.ShapeDtypeStruct((n, LANES), F32), jax.ShapeDtypeStruct((n, LANES), BF16),
                   jax.ShapeDtypeStruct((n, LANES), BF16)],
        scratch_shapes=[pltpu.VMEM((1, LANES), F32)],
        compiler_params=_cparams(("parallel", "arbitrary")),
        name="small_prep",
    )(scol, bias, neg_a)


def _rwkv_prep_kernel(u_ref, up_ref, mu_ref, w0_ref, w2_ref, a0_ref, a2_ref, g2_ref, kk_ref, ka_ref, rk_ref,
                      scan_ref, pc_ref, post_ref, *, tiles_per_seq):
    tm = u_ref.shape[0]
    u = u_ref[...].astype(F32)
    prev = up_ref[...].astype(F32)[BF16_ROWS - 1:BF16_ROWS, :]
    prev = jnp.where(pl.program_id(0) % tiles_per_seq == 0, 0.0, prev)
    rows = lax.broadcasted_iota(jnp.int32, (tm, 1), 0)
    u_prev = jnp.where(rows == 0, prev, pltpu.roll(u, 1, 0))
    xm = u + (u_prev - u) * mu_ref[...]
    r = xm[:, 0:W_MIX]
    k = xm[:, W_MIX:2 * W_MIX]
    v = xm[:, 2 * W_MIX:3 * W_MIX]
    x_lora = xm[:, 3 * W_MIX:3 * W_MIX + LANES]
    xg = xm[:, 3 * W_MIX + LANES:]
    logw = -_softplus(-(w0_ref[...] + _mm(jnp.tanh(x_lora), w2_ref[...]))) - 0.5
    log_decay = -jnp.exp(logw)
    a = _sigmoid(a0_ref[...] + _mm(x_lora, a2_ref[...]))
    g = _mm(_sigmoid(xg), g2_ref[...])
    same = _block_masks()[0].astype(F32)
    kk_raw = k * kk_ref[...]
    kk = kk_raw * lax.rsqrt(_mm_f32(kk_raw * kk_raw, same) + L2_EPS)
    k2 = k * (1.0 + (a - 1.0) * ka_ref[...])
    bonus = _mm_f32(r * k2 * rk_ref[...], same) * v
    ti = lax.broadcasted_iota(jnp.int32, (tm, tm), 0)
    tj = lax.broadcasted_iota(jnp.int32, (tm, tm), 1)
    in_chunk = jnp.logical_and(ti // CHUNK == tj // CHUNK, ti >= tj).astype(BF16)
    ci = lax.broadcasted_iota(jnp.int32, (tm // CHUNK, tm), 0)
    cj = lax.broadcasted_iota(jnp.int32, (tm // CHUNK, tm), 1)
    cum = _mm_mask(in_chunk, log_decay)
    cum_end = _mm_mask((ci == cj // CHUNK).astype(BF16), log_decay)
    inv = jnp.exp(-cum)
    scan_ref[0] = (-kk * jnp.exp(cum - log_decay)).astype(BF16)
    scan_ref[1] = (kk * a * inv).astype(BF16)
    scan_ref[2] = (k2 * inv).astype(BF16)
    scan_ref[3] = (r * jnp.exp(cum)).astype(BF16)
    scan_ref[4] = v.astype(BF16)
    pc_ref[...] = jnp.exp(cum_end)
    post_ref[0] = g
    post_ref[1] = bonus


def _rwkv_prep(u, mu, w0, w2p, a0, a2p, g2, k_k, k_a, r_k, seq, tm):
    n = u.shape[0]
    ublk = U_RWKV // D_MODEL
    row = lambda c: pl.BlockSpec((1, c), lambda i: (0, 0))
    mat = lambda r: pl.BlockSpec((r, W_MIX), lambda i: (0, 0))
    return pl.pallas_call(
        functools.partial(_rwkv_prep_kernel, tiles_per_seq=seq // tm),
        grid=(n // tm,),
        in_specs=[
            pl.BlockSpec((tm, D_MODEL), lambda i: (i, ublk)),
            pl.BlockSpec((BF16_ROWS, D_MODEL), lambda i: (jnp.maximum(i * (tm // BF16_ROWS) - 1, 0), ublk)),
            row(D_MODEL), row(W_MIX), mat(LANES), row(W_MIX), mat(LANES), mat(LANES), row(W_MIX), row(W_MIX), row(W_MIX),
        ],
        out_specs=[
            pl.BlockSpec((5, tm, W_MIX), lambda i: (0, i, 0)),
            pl.BlockSpec((tm // CHUNK, W_MIX), lambda i: (i, 0)),
            pl.BlockSpec((2, tm, W_MIX), lambda i: (0, i, 0)),
        ],
        out_shape=[jax.ShapeDtypeStruct((5, n, W_MIX), BF16), jax.ShapeDtypeStruct((n // CHUNK, W_MIX), F32),
                   jax.ShapeDtypeStruct((2, n, W_MIX), F32)],
        compiler_params=_cparams(("parallel",)),
        name="rwkv_prep",
    )(u, u, mu, w0, w2p, a0, a2p, g2, k_k, k_a, r_k)


def _rwkv_chunk_kernel(x_ref, pc_ref, o_ref, s_scr, *, nchunk, nbatch):
    @pl.when(pl.program_id(0) == 0)
    def _():
        s_scr[...] = jnp.zeros_like(s_scr)

    same, strict, incl, diag = _block_masks()
    eye = diag.astype(F32)

    def chunks(i, carry):
        items = [(b, i * CHUNK_UNROLL + j) for j in range(CHUNK_UNROLL) for b in range(nbatch)]
        each = lambda f, *lists: [f(*args) for args in zip(*lists)]
        sls = [pl.ds(pl.multiple_of(ci * CHUNK, CHUNK), CHUNK) for _, ci in items]
        a_s, b_s, k_s, r_s, v_s = ([_stack_heads(x_ref[i, b, sl, :], same) for (b, _), sl in zip(items, sls)]
                                   for i in range(5))
        pc = [pc_ref[b, pl.ds(ci, 1), :] for b, ci in items]
        m_ab = each(lambda a, b: jnp.where(strict, _mm_nt(a, b), 0.0), a_s, b_s)
        m_ak = each(lambda a, k: jnp.where(strict, _mm_nt(a, k), 0.0), a_s, k_s)
        n_rb = each(lambda r, b: jnp.where(incl, _mm_nt(r, b), 0.0), r_s, b_s)
        n_rk = each(lambda r, k: jnp.where(incl, _mm_nt(r, k), 0.0), r_s, k_s)
        t_inv = _unit_lower_inverses(m_ab, eye)
        a2 = each(_mm, t_inv, a_s)
        u0 = each(_mm, t_inv, each(_mm, m_ak, v_s))
        r2 = each(lambda r, n, a: r.astype(F32) + _mm(n, a), r_s, n_rb, a2)
        o0 = each(lambda n1, u, n2, v: _mm(n1, u) + _mm(n2, v), n_rb, u0, n_rk, v_s)
        b_end = each(lambda b, p: b.astype(F32) * p, b_s, pc)
        k_end = each(lambda k, p: k.astype(F32) * p, k_s, pc)
        g_mat = each(lambda p, a, b: eye * p + _mm_tn(a, b), pc, a2, b_end)
        s0 = each(lambda u, b, v, k: _mm_tn(u, b) + _mm_tn(v, k), u0, b_end, v_s, k_end)
        for n, ((b, _), sl) in enumerate(zip(items, sls)):
            s = s_scr[b]
            o = _mm_nt(r2[n], s) + o0[n]
            s_scr[b] = _mm(s, g_mat[n]) + s0[n]
            mean = jnp.sum(o, axis=-1, keepdims=True) * (1.0 / HEAD_DIM)
            cen = jnp.where(same, o - mean, 0.0)
            var = jnp.sum(cen * cen, axis=-1, keepdims=True) * (1.0 / HEAD_DIM)
            o_ref[b, sl, :] = _unstack_heads(cen * lax.rsqrt(var + RWKV_GN_EPS))
        return carry

    lax.fori_loop(0, nchunk // CHUNK_UNROLL, chunks, 0)


def _rwkv_chunk(xs, pc, batch, seq, cblk):
    xs = xs.reshape(5, batch, seq, W_MIX)
    pc = pc.reshape(batch, seq // CHUNK, W_MIX)
    out = pl.pallas_call(
        functools.partial(_rwkv_chunk_kernel, nchunk=cblk // CHUNK, nbatch=batch),
        grid=(seq // cblk,),
        in_specs=[
            pl.BlockSpec((5, batch, cblk, W_MIX), lambda j: (0, 0, j, 0)),
            pl.BlockSpec((batch, cblk // CHUNK, W_MIX), lambda j: (0, j, 0)),
        ],
        out_specs=pl.BlockSpec((batch, cblk, W_MIX), lambda j: (0, j, 0)),
        out_shape=jax.ShapeDtypeStruct((batch, seq, W_MIX), F32),
        scratch_shapes=[pltpu.VMEM((batch, W_MIX, W_MIX), F32)],
        compiler_params=_cparams(("arbitrary",)),
        name="rwkv_chunk",
    )(xs, pc)
    return out.reshape(batch * seq, W_MIX)


def _gdn_prep_kernel(u_ref, up_ref, cw_ref, o_ref, ext_scr, *, tiles_per_seq):
    tm = u_ref.shape[0]
    c3 = 3 * W_MIX
    prev = up_ref[...].astype(F32)[:, :c3]
    ext_scr[0:BF16_ROWS, :] = jnp.where(pl.program_id(0) % tiles_per_seq == 0, 0.0, prev)
    ext_scr[BF16_ROWS:, :] = u_ref[...].astype(F32)[:, :c3]
    y = jnp.zeros((tm, c3), F32)
    for j in range(GDN_CONV):
        y = y + ext_scr[pl.ds(BF16_ROWS - (GDN_CONV - 1) + j, tm), :] * cw_ref[j:j + 1, :]
    y = _silu(y)
    same = _block_masks()[0].astype(F32)
    q = y[:, 0:W_MIX]
    k = y[:, W_MIX:2 * W_MIX]
    o_ref[0] = (q * lax.rsqrt(_mm_f32(q * q, same) + L2_EPS) * (HEAD_DIM ** -0.5)).astype(BF16)
    o_ref[1] = (k * lax.rsqrt(_mm_f32(k * k, same) + L2_EPS)).astype(BF16)
    o_ref[2] = y[:, 2 * W_MIX:].astype(BF16)


def _gdn_prep(u, conv_w, seq, tm):
    n = u.shape[0]
    ublk = U_GDN // D_MODEL
    return pl.pallas_call(
        functools.partial(_gdn_prep_kernel, tiles_per_seq=seq // tm),
        grid=(n // tm,),
        in_specs=[
            pl.BlockSpec((tm, D_MODEL), lambda i: (i, ublk)),
            pl.BlockSpec((BF16_ROWS, D_MODEL), lambda i: (jnp.maximum(i * (tm // BF16_ROWS) - 1, 0), ublk)),
            pl.BlockSpec((GDN_CONV, 3 * W_MIX), lambda i: (0, 0)),
        ],
        out_specs=pl.BlockSpec((3, tm, W_MIX), lambda i: (0, i, 0)),
        out_shape=jax.ShapeDtypeStruct((3, n, W_MIX), BF16),
        scratch_shapes=[pltpu.VMEM((tm + BF16_ROWS, 3 * W_MIX), F32)],
        compiler_params=_cparams(("parallel",)),
        name="gdn_prep",
    )(u, u, conv_w)


def _gdn_chunk_kernel(x_ref, col_ref, nw_ref, o_ref, s_scr, *, nchunk, nbatch):
    @pl.when(pl.program_id(0) == 0)
    def _():
        s_scr[...] = jnp.zeros_like(s_scr)

    same, strict, incl, diag = _block_masks()
    eye = diag.astype(F32)
    lower = incl.astype(BF16)
    ones = same.astype(BF16)

    def head_col(col, lane0):
        return jnp.concatenate([col[:, lane0 + h:lane0 + h + 1] for h in range(N_HEADS)], axis=0)

    def chunks(i, carry):
        items = [(b, i * CHUNK_UNROLL + j) for j in range(CHUNK_UNROLL) for b in range(nbatch)]
        each = lambda f, *lists: [f(*args) for args in zip(*lists)]
        sls = [pl.ds(pl.multiple_of(ci * CHUNK, CHUNK), CHUNK) for _, ci in items]
        q_s, k_s, v_s = ([_stack_heads(x_ref[i, b, sl, :], same) for (b, _), sl in zip(items, sls)] for i in range(3))
        cols = [col_ref[b, sl, :] for (b, _), sl in zip(items, sls)]
        beta = [head_col(c, SM_BETA) for c in cols]
        g = [head_col(c, SM_DEC) for c in cols]
        g_wide = [jnp.broadcast_to(x, (W_MIX, LANES)) for x in g]
        gam = [_mm_mask(lower, x)[:, 0:1] for x in g_wide]
        gam_end = [_mm_mask(ones, x)[:, 0:1] for x in g_wide]
        gdiff = [_mm_mask(lower, jnp.where(strict, x, 0.0)) for x in g]
        decay = [jnp.exp(jnp.where(incl, x, -jnp.inf)) for x in gdiff]
        a_mat = each(lambda bt, d, k: jnp.where(strict, bt * d * _mm_nt(k, k), 0.0), beta, decay, k_s)
        t_inv = _unit_lower_inverses([-a for a in a_mat], eye)
        e_gam = [jnp.exp(x) for x in gam]
        u0 = each(lambda t, bt, v: _mm(t, bt * v.astype(F32)), t_inv, beta, v_s)
        wm = each(lambda t, bt, e, k: _mm(t, (bt * e) * k.astype(F32)), t_inv, beta, e_gam, k_s)
        qk = each(lambda q, k, d: _mm_nt(q, k) * d, q_s, k_s, decay)
        q2 = each(lambda e, q, a, w: e * q.astype(F32) - _mm(a, w), e_gam, q_s, qk, wm)
        o0 = each(_mm, qk, u0)
        k_end = each(lambda k, ge, ga: k.astype(F32) * jnp.exp(ge - ga), k_s, gam_end, gam)
        g_mat = each(lambda ge, k, w: eye * jnp.exp(ge) - _mm_tn(k, w), gam_end, k_end, wm)
        s0 = each(_mm_tn, k_end, u0)
        for n, ((b, _), sl) in enumerate(zip(items, sls)):
            s = s_scr[b]
            o = _mm(q2[n], s) + o0[n]
            s_scr[b] = _mm(g_mat[n], s) + s0[n]
            ms = jnp.sum(o * o, axis=-1, keepdims=True) * (1.0 / HEAD_DIM)
            o_ref[b, sl, :] = _unstack_heads(o * lax.rsqrt(ms + NORM_EPS)) * nw_ref[...]
        return carry

    lax.fori_loop(0, nchunk // CHUNK_UNROLL, chunks, 0)


def _gdn_chunk(xs, col, norm_w, batch, seq, cblk):
    xs = xs.reshape(3, batch, seq, W_MIX)
    col = col.reshape(batch, seq, LANES)
    out = pl.pallas_call(
        functools.partial(_gdn_chunk_kernel, nchunk=cblk // CHUNK, nbatch=batch),
        grid=(seq // cblk,),
        in_specs=[
            pl.BlockSpec((3, batch, cblk, W_MIX), lambda j: (0, 0, j, 0)),
            pl.BlockSpec((batch, cblk, LANES), lambda j: (0, j, 0)),
            pl.BlockSpec((1, W_MIX), lambda j: (0, 0)),
        ],
        out_specs=pl.BlockSpec((batch, cblk, W_MIX), lambda j: (0, j, 0)),
        out_shape=jax.ShapeDtypeStruct((batch, seq, W_MIX), F32),
        scratch_shapes=[pltpu.VMEM((batch, W_MIX, W_MIX), F32)],
        compiler_params=_cparams(("arbitrary",)),
        name="gdn_chunk",
    )(xs, col, norm_w)
    return out.reshape(batch * seq, W_MIX)


def _causal_pairs(nq):
    pairs = [(i, j) for i in range(nq) for j in range(i + 1)]
    return jnp.asarray(np.array([p[0] for p in pairs], np.int32)), jnp.asarray(np.array([p[1] for p in pairs], np.int32))


def _softmax_updates(scores, vt_ones, m_scr, acc_scr):
    probs = []
    for i, s in enumerate(scores):
        m_old = m_scr[i]
        m_new = jnp.maximum(m_old, jnp.max(s, axis=0, keepdims=True))
        m_scr[i] = m_new
        probs.append((jnp.exp2(m_old - m_new), jnp.exp2((s - m_new).astype(BF16))))
    for i, (alpha, p) in enumerate(probs):
        acc_scr[i] = alpha * acc_scr[i] + jnp.dot(vt_ones[i], p, preferred_element_type=F32)


def _pair_lanes(h):
    p = h // 2
    return slice(p * LANES, (p + 1) * LANES), h % 2 == 0


def _values_and_ones(vt, tk):
    row = lax.broadcasted_iota(jnp.int32, (LANES, tk), 0)
    out = []
    for h in range(N_HEADS):
        slab, low = _pair_lanes(h)
        mine = row < HEAD_DIM if low else row >= HEAD_DIM
        out.append(jnp.where(mine, vt[slab, :], jnp.ones((), vt.dtype)))
    return out


def _normalized(acc, low):
    if low:
        return acc[0:HEAD_DIM] / acc[HEAD_DIM:HEAD_DIM + 1]
    return acc[HEAD_DIM:] / acc[0:1]


def _key_after_query(tq):
    return lax.broadcasted_iota(jnp.int32, (tq, tq), 0) > lax.broadcasted_iota(jnp.int32, (tq, tq), 1)


def _fox_kernel(qi_ref, kj_ref, q_ref, k_ref, vt_ref, qb_ref, kb_ref, o_ref, qm_scr, m_scr, acc_scr):
    t = pl.program_id(1)
    qi = qi_ref[t]
    kj = kj_ref[t]
    tq = q_ref.shape[0]

    @pl.when(kj == 0)
    def _():
        m_scr[...] = jnp.full_like(m_scr, -jnp.inf)
        acc_scr[...] = jnp.zeros_like(acc_scr)
        q = q_ref[...]
        qb = qb_ref[...]
        lane = lax.broadcasted_iota(jnp.int32, (tq, LANES), 1)
        zero = jnp.zeros((), BF16)
        for h in range(N_HEADS):
            slab, low = _pair_lanes(h)
            mine = lane < HEAD_DIM if low else lane >= HEAD_DIM
            ck_lanes = jnp.logical_and(lane >= FOX_CK + 3 * h, lane < FOX_CK + 3 * h + 3)
            cq_lanes = jnp.logical_and(lane >= FOX_CQ + 3 * h, lane < FOX_CQ + 3 * h + 3)
            bias = jnp.where(ck_lanes, -jnp.ones((), BF16), jnp.where(cq_lanes, qb, zero))
            qm_scr[h] = jnp.concatenate([jnp.where(mine, q[:, slab], zero), bias], axis=1)

    def step(diagonal):
        k = k_ref[...]
        kb = kb_ref[...]
        vts = _values_and_ones(vt_ref[...], tq)
        if diagonal:
            masked = _key_after_query(tq)
        keys = [jnp.concatenate([k[:, p * LANES:(p + 1) * LANES], kb], axis=1) for p in range(N_HEADS // 2)]
        scores = []
        for h in range(N_HEADS):
            s = lax.dot_general(keys[h // 2], qm_scr[h], (((1,), (1,)), ((), ())), preferred_element_type=F32)
            if diagonal:
                s = jnp.where(masked, -jnp.inf, s)
            scores.append(s)
        _softmax_updates(scores, vts, m_scr, acc_scr)

    @pl.when(kj < qi)
    def _():
        step(False)

    @pl.when(kj == qi)
    def _():
        step(True)
        for p in range(N_HEADS // 2):
            pair = jnp.concatenate([_normalized(acc_scr[2 * p], True), _normalized(acc_scr[2 * p + 1], False)], axis=0)
            o_ref[:, p * LANES:(p + 1) * LANES] = pair.T


def _attn_specs(nq, tq, ucol, vt_rows):
    cb = ucol // W_MIX
    q_spec = pl.BlockSpec((tq, W_MIX), lambda b, t, qi, kj: (b * nq + qi[t], cb))
    k_spec = pl.BlockSpec((tq, W_MIX), lambda b, t, qi, kj: (b * nq + kj[t], cb + 1))
    vt_spec = pl.BlockSpec((W_MIX, tq), lambda b, t, qi, kj: (vt_rows // W_MIX, b * nq + kj[t]))
    return q_spec, k_spec, vt_spec


def _fox_attention(u, vt, qb, kb, batch, seq, tq):
    nq = seq // tq
    qi, kj = _causal_pairs(nq)
    q_spec, k_spec, vt_spec = _attn_specs(nq, tq, U_FOX, W_MIX)
    stat = lambda: pltpu.VMEM((N_HEADS, 1, tq), F32)
    return pl.pallas_call(
        _fox_kernel,
        grid_spec=pltpu.PrefetchScalarGridSpec(
            num_scalar_prefetch=2,
            grid=(batch, qi.shape[0]),
            in_specs=[
                q_spec, k_spec, vt_spec,
                pl.BlockSpec((tq, LANES), lambda b, t, qi, kj: (b * nq + qi[t], 0)),
                pl.BlockSpec((tq, LANES), lambda b, t, qi, kj: (b * nq + kj[t], 0)),
            ],
            out_specs=pl.BlockSpec((tq, W_MIX), lambda b, t, qi, kj: (b * nq + qi[t], 0)),
            scratch_shapes=[pltpu.VMEM((N_HEADS, tq, 2 * LANES), BF16), stat(),
                            pltpu.VMEM((N_HEADS, LANES, tq), F32)],
        ),
        out_shape=jax.ShapeDtypeStruct((batch * seq, W_MIX), F32),
        compiler_params=_cparams(("parallel", "arbitrary")),
        name="fox_attention",
    )(qi, kj, u, u, vt, qb, kb)


def _diff_kernel(qi_ref, kj_ref, q_ref, k_ref, vt_ref, cq_ref, sq_ref, ck_ref, sk_ref, rot_ref, lam_ref, ln_ref, o_ref,
                 qm_scr, m_scr, acc_scr, *, lam_init):
    t = pl.program_id(1)
    qi = qi_ref[t]
    kj = kj_ref[t]
    tq = q_ref.shape[0]
    lane = lax.broadcasted_iota(jnp.int32, (tq, LANES), 1)

    def rope(x, cos, sin):
        return x.astype(F32) * cos + jnp.dot(x, rot_ref[...], preferred_element_type=F32) * sin

    @pl.when(kj == 0)
    def _():
        m_scr[...] = jnp.full_like(m_scr, -jnp.inf)
        acc_scr[...] = jnp.zeros_like(acc_scr)
        q = rope(q_ref[...], cq_ref[...], sq_ref[...])
        for h in range(N_HEADS):
            slab, low = _pair_lanes(h)
            base = 0 if low else HEAD_DIM
            for c in range(2):
                lo = base + c * DIFF_DH
                sel = jnp.logical_and(lane >= lo, lane < lo + DIFF_DH)
                qm_scr[2 * h + c] = jnp.where(sel, q[:, slab], 0.0).astype(BF16)

    def step(diagonal):
        k = rope(k_ref[...], ck_ref[...], sk_ref[...]).astype(BF16)
        vts = _values_and_ones(vt_ref[...], tq)
        if diagonal:
            masked = _key_after_query(tq)
        scores = []
        for h in range(N_HEADS):
            slab, _ = _pair_lanes(h)
            for c in range(2):
                i = 2 * h + c
                s = lax.dot_general(k[:, slab], qm_scr[i], (((1,), (1,)), ((), ())), preferred_element_type=F32)
                if diagonal:
                    s = jnp.where(masked, -jnp.inf, s)
                scores.append(s)
        _softmax_updates(scores, [vts[i // 2] for i in range(2 * N_HEADS)], m_scr, acc_scr)

    @pl.when(kj < qi)
    def _():
        step(False)

    @pl.when(kj == qi)
    def _():
        step(True)
        lp = lam_ref[...]
        lam = (jnp.exp(jnp.sum(lp[0:1] * lp[1:2], axis=-1, keepdims=True))
               - jnp.exp(jnp.sum(lp[2:3] * lp[3:4], axis=-1, keepdims=True)) + lam_init)
        head = lambda h: (_normalized(acc_scr[2 * h], h % 2 == 0) - lam * _normalized(acc_scr[2 * h + 1], h % 2 == 0))
        is_lo = lane < HEAD_DIM
        for p in range(N_HEADS // 2):
            o = jnp.concatenate([head(2 * p), head(2 * p + 1)], axis=0).T
            sq = o * o
            ms_lo = jnp.sum(jnp.where(is_lo, sq, 0.0), axis=-1, keepdims=True)
            ms_hi = jnp.sum(jnp.where(is_lo, 0.0, sq), axis=-1, keepdims=True)
            ms = jnp.where(is_lo, ms_lo, ms_hi) * (1.0 / HEAD_DIM)
            o_ref[:, p * LANES:(p + 1) * LANES] = o * lax.rsqrt(ms + DIFF_LN_EPS) * ln_ref[...] * (1.0 - lam_init)


def _diff_attention(u, vt, cos, sin, rot, lam_p, subln, batch, seq, tq, lam_init):
    nq = seq // tq
    qi, kj = _causal_pairs(nq)
    q_spec, k_spec, vt_spec = _attn_specs(nq, tq, U_DIFF, 0)
    tab_q = pl.BlockSpec((tq, W_MIX), lambda b, t, qi, kj: (qi[t], 0))
    tab_k = pl.BlockSpec((tq, W_MIX), lambda b, t, qi, kj: (kj[t], 0))
    const = lambda r, c: pl.BlockSpec((r, c), lambda b, t, qi, kj: (0, 0))
    stat = lambda: pltpu.VMEM((2 * N_HEADS, 1, tq), F32)
    return pl.pallas_call(
        functools.partial(_diff_kernel, lam_init=lam_init),
        grid_spec=pltpu.PrefetchScalarGridSpec(
            num_scalar_prefetch=2,
            grid=(batch, qi.shape[0]),
            in_specs=[q_spec, k_spec, vt_spec, tab_q, tab_q, tab_k, tab_k,
                      const(W_MIX, W_MIX), const(4, DIFF_DH), const(1, LANES)],
            out_specs=pl.BlockSpec((tq, W_MIX), lambda b, t, qi, kj: (b * nq + qi[t], 0)),
            scratch_shapes=[pltpu.VMEM((2 * N_HEADS, tq, LANES), BF16), stat(),
                            pltpu.VMEM((2 * N_HEADS, LANES, tq), F32)],
        ),
        out_shape=jax.ShapeDtypeStruct((batch * seq, W_MIX), F32),
        compiler_params=_cparams(("parallel", "arbitrary")),
        name="diff_attention",
    )(qi, kj, u, u, vt, cos, sin, cos, sin, rot, lam_p, subln)


def _merge_kernel(x_ref, gate_ref, oa_ref, post_ref, yb_ref, yc_ref, od_ref, gd_ref, lnw_ref, lnb_ref,
                  wbo_ref, wout_ref, o_ref):
    y_a = (oa_ref[...] * lnw_ref[...] + lnb_ref[...] + post_ref[1]) * post_ref[0]
    y_d = od_ref[...] * _silu(gd_ref[...].astype(F32))
    acc = jnp.zeros(x_ref.shape, F32)
    for b, y in enumerate((y_a, yb_ref[...], yc_ref[...], y_d)):
        gate = _sigmoid(gate_ref[:, b * D_MODEL:(b + 1) * D_MODEL].astype(F32))
        acc = acc + gate * _mm(y, wbo_ref[b])
    o_ref[...] = x_ref[...] + _mm(acc, wout_ref[...])


def _merge(x, u, o_a, post, y_b, y_c, o_d, ln_w, ln_b, w_bo, w_out, tm):
    n = x.shape[0]
    tok = lambda c: pl.BlockSpec((tm, c), lambda i: (i, 0))
    return pl.pallas_call(
        _merge_kernel,
        grid=(n // tm,),
        in_specs=[
            tok(D_MODEL),
            pl.BlockSpec((tm, 4 * D_MODEL), lambda i: (i, U_GATE // (4 * D_MODEL))),
            tok(W_MIX),
            pl.BlockSpec((2, tm, W_MIX), lambda i: (0, i, 0)),
            tok(W_MIX), tok(W_MIX), tok(W_MIX),
            pl.BlockSpec((tm, W_MIX), lambda i: (i, (U_GDN + 3 * W_MIX) // W_MIX)),
            pl.BlockSpec((1, W_MIX), lambda i: (0, 0)),
            pl.BlockSpec((1, W_MIX), lambda i: (0, 0)),
            pl.BlockSpec((4, W_MIX, D_MODEL), lambda i: (0, 0, 0)),
            pl.BlockSpec((D_MODEL, D_MODEL), lambda i: (0, 0)),
        ],
        out_specs=tok(D_MODEL),
        out_shape=jax.ShapeDtypeStruct((n, D_MODEL), F32),
        compiler_params=_cparams(("parallel",)),
        name="merge",
    )(x, u, o_a, post, y_b, y_c, o_d, u, ln_w, ln_b, w_bo, w_out)


def _ffn_kernel(x_ref, g_ref, wg_ref, wu_ref, wd_ref, o_ref, h_scr):
    @pl.when(pl.program_id(1) == 0)
    def _():
        x = x_ref[...]
        h_scr[...] = _rms(x, g_ref[...], NORM_EPS).astype(BF16)
        o_ref[...] = x

    h = h_scr[...]
    act = _silu(jnp.dot(h, wg_ref[...], preferred_element_type=F32)) * jnp.dot(h, wu_ref[...], preferred_element_type=F32)
    o_ref[...] += _mm(act, wd_ref[...])


def _ffn(x, g, wg, wu, wd, tm, tf):
    n = x.shape[0]
    ff = wg.shape[1]
    return pl.pallas_call(
        _ffn_kernel,
        grid=(n // tm, ff // tf),
        in_specs=[
            pl.BlockSpec((tm, D_MODEL), lambda i, f: (i, 0), pipeline_mode=pl.Buffered(1)),
            pl.BlockSpec((1, D_MODEL), lambda i, f: (0, 0)),
            pl.BlockSpec((D_MODEL, tf), lambda i, f: (0, f)),
            pl.BlockSpec((D_MODEL, tf), lambda i, f: (0, f)),
            pl.BlockSpec((tf, D_MODEL), lambda i, f: (f, 0)),
        ],
        out_specs=pl.BlockSpec((tm, D_MODEL), lambda i, f: (i, 0)),
        out_shape=jax.ShapeDtypeStruct((n, D_MODEL), F32),
        scratch_shapes=[pltpu.VMEM((tm, D_MODEL), BF16)],
        compiler_params=_cparams(("parallel", "arbitrary"), vmem_mb=56),
        name="ffn",
    )(x, g, wg, wu, wd)


def _router_kernel(x_ref, g_ref, router_ref, h_ref, c_ref, rc_ref, rr_ref, cnt_ref):
    t = x_ref.shape[0]
    h = _rms(x_ref[...], g_ref[...], NORM_EPS)
    h_ref[...] = h.astype(BF16)
    logits = _mm_f32(h, router_ref[...])
    lane = lax.broadcasted_iota(jnp.int32, logits.shape, 1).astype(F32)
    lg = jnp.where(lane < N_EXPERTS, logits, -jnp.inf)
    m1 = jnp.max(lg, axis=-1, keepdims=True)
    i1 = jnp.min(jnp.where(lg == m1, lane, float(LANES)), axis=-1, keepdims=True)
    lg2 = jnp.where(lane == i1, -jnp.inf, lg)
    m2 = jnp.max(lg2, axis=-1, keepdims=True)
    i2 = jnp.min(jnp.where(lg2 == m2, lane, float(LANES)), axis=-1, keepdims=True)
    e2 = jnp.exp(m2 - m1)
    c_ref[...] = jnp.where(lane == i1, 1.0 / (1.0 + e2), 0.0) + jnp.where(lane == i2, e2 / (1.0 + e2), 0.0)
    sel = jnp.logical_or(lane == i1, lane == i2)
    sel_f = jnp.where(sel, 1.0, 0.0)
    earlier = _tri_masks(t)[0].astype(BF16)
    rank = jnp.dot(earlier, sel_f.astype(BF16), preferred_element_type=F32)
    rc = jnp.where(sel, rank, -1.0)
    rc_ref[...] = rc
    rr_ref[...] = rc.T[0:BF16_ROWS, :]
    cnt_ref[...] = jnp.broadcast_to(jnp.sum(sel_f, axis=0, keepdims=True), cnt_ref.shape).astype(jnp.int32)


def _router(x, g, router, tm):
    n = x.shape[0]
    return pl.pallas_call(
        _router_kernel,
        grid=(n // tm,),
        in_specs=[
            pl.BlockSpec((tm, D_MODEL), lambda i: (i, 0)),
            pl.BlockSpec((1, D_MODEL), lambda i: (0, 0)),
            pl.BlockSpec((D_MODEL, LANES), lambda i: (0, 0)),
        ],
        out_specs=[
            pl.BlockSpec((tm, D_MODEL), lambda i: (i, 0)),
            pl.BlockSpec((tm, LANES), lambda i: (i, 0)),
            pl.BlockSpec((tm, LANES), lambda i: (i, 0)),
            pl.BlockSpec((BF16_ROWS, tm), lambda i: (0, i)),
            pl.BlockSpec((8, LANES), lambda i: (i, 0)),
        ],
        out_shape=[
            jax.ShapeDtypeStruct((n, D_MODEL), BF16),
            jax.ShapeDtypeStruct((n, LANES), F32),
            jax.ShapeDtypeStruct((n, LANES), F32),
            jax.ShapeDtypeStruct((BF16_ROWS, n), F32),
            jax.ShapeDtypeStruct((n // tm * 8, LANES), jnp.int32),
        ],
        compiler_params=_cparams(("parallel",)),
        name="router",
    )(x, g, router)


def _moe_kernel(cnt_ref, x_ref, h_ref, c_ref, rc_ref, rr_ref, wg_ref, wu_ref, wd_ref, o_ref, xg_scr, yg_scr, *, rows):
    i = pl.program_id(0)
    e = pl.program_id(1)
    f = pl.program_id(2)
    last_f = f == pl.num_programs(2) - 1
    t = h_ref.shape[0]
    nsub = xg_scr.shape[0] // rows
    cnt = cnt_ref[i * N_EXPERTS + e]
    blocks = [(sb, slice(sb * rows, (sb + 1) * rows)) for sb in range(nsub)]

    @pl.when(jnp.logical_and(e == 0, f == 0))
    def _():
        o_ref[...] = x_ref[...]

    @pl.when(f == 0)
    def _():
        rr = rr_ref[pl.ds(e, 1), :]
        for sb, rs in blocks:
            @pl.when(cnt > sb * rows)
            def _():
                slot = (lax.broadcasted_iota(jnp.int32, (rows, t), 0) + sb * rows).astype(F32)
                gather = jnp.where(rr == slot, 1.0, 0.0).astype(BF16)
                xg_scr[rs, :] = jnp.dot(gather, h_ref[...], preferred_element_type=F32).astype(BF16)
                yg_scr[rs, :] = jnp.zeros((rows, D_MODEL), F32)

    for sb, rs in blocks:
        @pl.when(cnt > sb * rows)
        def _():
            xb = xg_scr[rs, :]
            act = (_silu(jnp.dot(xb, wg_ref[0], preferred_element_type=F32))
                   * jnp.dot(xb, wu_ref[0], preferred_element_type=F32))
            yg_scr[rs, :] += _mm(act, wd_ref[0])

    @pl.when(last_f)
    def _():
        lane = lax.broadcasted_iota(jnp.int32, (t, LANES), 1)
        mine = lane == e
        rc = jnp.sum(jnp.where(mine, rc_ref[...], 0.0), axis=-1, keepdims=True)
        ce = jnp.sum(jnp.where(mine, c_ref[...], 0.0), axis=-1, keepdims=True)
        for sb, rs in blocks:
            @pl.when(cnt > sb * rows)
            def _():
                slot = (lax.broadcasted_iota(jnp.int32, (t, rows), 1) + sb * rows).astype(F32)
                scatter = jnp.where(rc == slot, 1.0, 0.0).astype(BF16)
                o_ref[...] += ce * jnp.dot(scatter, yg_scr[rs, :].astype(BF16), preferred_element_type=F32)


def _moe(x, g, router, wg, wu, wd, tm, tf, rows):
    n = x.shape[0]
    ff = wg.shape[2]
    h, c, rc, rr, cnt = _router(x, g, router, tm)
    cnt = cnt.reshape(n // tm, 8, LANES)[:, 0, :N_EXPERTS].reshape(-1)
    nsub = -(-tm // rows)
    tok = lambda cols, **kw: pl.BlockSpec((tm, cols), lambda i, e, f, cnt: (i, 0), **kw)
    once = dict(pipeline_mode=pl.Buffered(1))
    return pl.pallas_call(
        functools.partial(_moe_kernel, rows=rows),
        grid_spec=pltpu.PrefetchScalarGridSpec(
            num_scalar_prefetch=1,
            grid=(n // tm, N_EXPERTS, ff // tf),
            in_specs=[
                tok(D_MODEL, **once), tok(D_MODEL, **once), tok(LANES), tok(LANES),
                pl.BlockSpec((BF16_ROWS, tm), lambda i, e, f, cnt: (0, i)),
                pl.BlockSpec((1, D_MODEL, tf), lambda i, e, f, cnt: (e, 0, f)),
                pl.BlockSpec((1, D_MODEL, tf), lambda i, e, f, cnt: (e, 0, f)),
                pl.BlockSpec((1, tf, D_MODEL), lambda i, e, f, cnt: (e, f, 0)),
            ],
            out_specs=tok(D_MODEL),
            scratch_shapes=[pltpu.VMEM((nsub * rows, D_MODEL), BF16), pltpu.VMEM((nsub * rows, D_MODEL), F32)],
        ),
        out_shape=jax.ShapeDtypeStruct((n, D_MODEL), F32),
        compiler_params=_cparams(("parallel", "arbitrary", "arbitrary"), vmem_mb=56),
        name="moe",
    )(cnt, x, h, c, rc, rr, wg, wu, wd)


def _ple_kernel(x_ref, p_ref, g_ref, wgate_ref, wproj_ref, fin_ref, o_ref, *, final):
    x = x_ref[...]
    h = _rms(x, g_ref[...], NORM_EPS)
    y = x + _sigmoid(_mm(h, wgate_ref[...])) * _mm(p_ref[...], wproj_ref[...])
    o_ref[...] = _rms(y, fin_ref[...], NORM_EPS) if final else y


def _ple(x, p, g, wgate, wproj, fin, tm, final):
    n = x.shape[0]
    return pl.pallas_call(
        functools.partial(_ple_kernel, final=final),
        grid=(n // tm,),
        in_specs=[
            pl.BlockSpec((tm, D_MODEL), lambda i: (i, 0)),
            pl.BlockSpec((tm, P_DIM), lambda i: (i, 0)),
            pl.BlockSpec((1, D_MODEL), lambda i: (0, 0)),
            pl.BlockSpec((D_MODEL, D_MODEL), lambda i: (0, 0)),
            pl.BlockSpec((P_DIM, D_MODEL), lambda i: (0, 0)),
            pl.BlockSpec((1, D_MODEL), lambda i: (0, 0)),
        ],
        out_specs=pl.BlockSpec((tm, D_MODEL), lambda i: (i, 0)),
        out_shape=jax.ShapeDtypeStruct((n, D_MODEL), F32),
        compiler_params=_cparams(("parallel",)),
        name="ple",
    )(x, p, g, wgate, wproj, fin)


def _tiles(n, seq):
    tm = min(512, seq)
    tm_big = 1024 if n % 1024 == 0 else tm
    tm_in = 2048 if n % 2048 == 0 else tm_big
    moe_rows = tm_big // 4 + tm_big // 32
    return dict(tm=tm, tm_big=tm_big, tm_in=tm_in, tn_in=1024, tq=min(512, seq), cblk=min(512, seq), moe_rows=moe_rows)


def _rope_tables(seq):
    half = ROPE_DIMS // 2
    inv = ROPE_THETA ** (-jnp.arange(half, dtype=F32) * 2.0 / ROPE_DIMS)
    ang = jnp.arange(seq, dtype=F32)[:, None] * inv[None, :]
    pad = jnp.zeros((seq, DIFF_DH - ROPE_DIMS), F32)
    cos = jnp.concatenate([jnp.cos(ang), jnp.cos(ang), pad + 1.0], axis=-1)
    sin = jnp.concatenate([jnp.sin(ang), jnp.sin(ang), pad], axis=-1)
    d = jnp.arange(W_MIX)
    dd = d % DIFF_DH
    src = jnp.where(dd < half, d + half, d - half)
    sign = jnp.where(dd < half, -1.0, jnp.where(dd < ROPE_DIMS, 1.0, 0.0))
    rot = jnp.zeros((W_MIX, W_MIX), F32).at[src, d].set(sign)
    reps = W_MIX // DIFF_DH
    return jnp.tile(cos, (1, reps)), jnp.tile(sin, (1, reps)), rot.astype(BF16)


def _split_w_in(w):
    a0 = 0
    b0 = a0 + 4 * W_MIX
    c0 = b0 + 3 * W_MIX
    d0 = c0 + 3 * W_MIX + N_HEADS
    g0 = d0 + 4 * W_MIX + 2 * N_HEADS
    d_small = d0 + 3 * W_MIX
    diff_q = w[:, b0:b0 + W_MIX] * (DIFF_DH ** -0.5 * LOG2E)
    fox_q = w[:, c0:c0 + W_MIX] * (HEAD_DIM ** -0.5 * LOG2E)
    main = jnp.concatenate([
        w[:, g0:], w[:, a0:b0], w[:, d0:d_small], w[:, d_small + 2 * N_HEADS:g0],
        diff_q, w[:, b0 + W_MIX:b0 + 2 * W_MIX], fox_q, w[:, c0 + W_MIX:c0 + 2 * W_MIX],
    ], axis=1).astype(BF16)
    small = jnp.concatenate([
        w[:, c0 + 3 * W_MIX:d0], w[:, d_small:d_small + 2 * N_HEADS],
        jnp.zeros((D_MODEL, LANES - 3 * N_HEADS), w.dtype),
    ], axis=1).astype(BF16)
    v_t = jnp.concatenate([w[:, b0 + 2 * W_MIX:c0], w[:, c0 + 2 * W_MIX:c0 + 3 * W_MIX]], axis=1).T.astype(BF16)
    return main, small, v_t


def _small_params(fbias, a_log, dt_bias):
    zeros = jnp.zeros((N_HEADS,), F32)
    bias = jnp.concatenate([fbias, zeros, dt_bias, jnp.zeros((LANES - 3 * N_HEADS,), F32)])
    neg_a = jnp.concatenate([zeros, zeros, -jnp.exp(a_log), jnp.zeros((LANES - 3 * N_HEADS,), F32)])
    return bias.reshape(1, LANES), neg_a.reshape(1, LANES)


def _pad_rows(w, top, total):
    return jnp.concatenate([jnp.zeros((top, w.shape[1]), w.dtype), w,
                            jnp.zeros((total - top - w.shape[0], w.shape[1]), w.dtype)], axis=0)


def kernel(x, p, norm_mix, norm_ffn, norm_ple, w_in, w_bo, w_out, rwkv_mu, rwkv_w0, rwkv_w2, rwkv_a0, rwkv_a2, rwkv_g2, rwkv_kk, rwkv_ka, rwkv_rk, rwkv_ln_w, rwkv_ln_b, diff_lam, diff_subln, fox_fbias, gdn_conv, gdn_a_log, gdn_dt_bias, gdn_norm, ffn_w_gate, ffn_w_up, ffn_w_down, moe_router, moe_w_gate, moe_w_up, moe_w_down, ple_proj, ple_gate, final_norm):
    batch, seq, _ = x.shape
    depth = w_in.shape[0]
    n = batch * seq
    t = _tiles(n, seq)
    tm, tq, cblk = t["tm"], t["tq"], t["cblk"]
    row = lambda v: v.reshape(1, -1).astype(F32)
    cos, sin, rot = _rope_tables(seq)
    xf = x.reshape(n, D_MODEL)
    pf = p.reshape(depth, n, P_DIM)

    for i in range(depth):
        w_main, w_small, w_vt = _split_w_in(w_in[i])
        u, scol, vt = _inproj(xf, row(norm_mix[i]), w_main, w_small, w_vt, t["tm_in"], t["tn_in"])
        bias, neg_a = _small_params(fox_fbias[i], gdn_a_log[i], gdn_dt_bias[i])
        hcol, fox_kb, fox_qb = _small_prep(scol, bias, neg_a, batch, seq, tm)

        scan_in, pc, post = _rwkv_prep(
            u, row(rwkv_mu[i]), row(rwkv_w0[i]), _pad_rows(rwkv_w2[i], 0, LANES), row(rwkv_a0[i]),
            _pad_rows(rwkv_a2[i], LANES // 2, LANES), rwkv_g2[i], row(rwkv_kk[i]), row(rwkv_ka[i]), row(rwkv_rk[i]),
            seq, tm)
        o_a = _rwkv_chunk(scan_in, pc, batch, seq, cblk)

        o_d = _gdn_chunk(_gdn_prep(u, gdn_conv[i].T, seq, tm), hcol, jnp.tile(row(gdn_norm[i]), (1, N_HEADS)),
                         batch, seq, cblk)

        lam_init = 0.8 - 0.6 * math.exp(-0.3 * i)
        y_b = _diff_attention(u, vt, cos, sin, rot, diff_lam[i].astype(F32), jnp.tile(row(diff_subln[i]), (1, 2)),
                              batch, seq, tq, lam_init)
        y_c = _fox_attention(u, vt, fox_qb, fox_kb, batch, seq, tq)

        xf = _merge(xf, u, o_a, post, y_b, y_c, o_d, row(rwkv_ln_w[i]), row(rwkv_ln_b[i]),
                    w_bo[i].astype(BF16), w_out[i].astype(BF16), tm)

        j = i // 2
        if i % 2 == 0:
            xf = _ffn(xf, row(norm_ffn[i]), ffn_w_gate[j].astype(BF16), ffn_w_up[j].astype(BF16),
                      ffn_w_down[j].astype(BF16), t["tm_big"], ffn_w_gate.shape[2] // 2)
        else:
            router = jnp.concatenate([moe_router[j], jnp.zeros((D_MODEL, LANES - N_EXPERTS), F32)], axis=1)
            xf = _moe(xf, row(norm_ffn[i]), router, moe_w_gate[j].astype(BF16), moe_w_up[j].astype(BF16),
                      moe_w_down[j].astype(BF16), t["tm_big"], moe_w_gate.shape[3] // 2, t["moe_rows"])
        xf = _ple(xf, pf[i], row(norm_ple[i]), ple_gate[i].astype(BF16), ple_proj[i].astype(BF16),
                  row(final_norm), tm, i == depth - 1)
    return xf.reshape(batch, seq, D_MODEL)
```

```python
import functools
import math

import jax
import jax.numpy as jnp
import numpy as np
from jax import lax
from jax.experimental import pallas as pl
from jax.experimental.pallas import tpu as pltpu

F32 = jnp.float32
BF16 = jnp.bfloat16
HIGHEST = lax.Precision.HIGHEST

D_MODEL = 1024
P_DIM = 256
W_MIX = 256
HEAD_DIM = 64
N_HEADS = 4
DIFF_DH = 32
ROPE_THETA = 500000.0
ROPE_DIMS = 8
RWKV_GN_EPS = 64e-5
DIFF_LN_EPS = 1e-5
GDN_CONV = 4
CHUNK = 64
CHUNK_UNROLL = 4
N_EXPERTS = 8
NORM_EPS = 1e-6
L2_EPS = 1e-6
LOG2E = math.log2(math.e)
LANES = 128
BF16_ROWS = 16
assert CHUNK == HEAD_DIM

U_GATE = 0
U_RWKV = 4096
U_GDN = 5120
U_DIFF = 6144
U_FOX = 6656
U_COLS = 7168
SM_FOX, SM_BETA, SM_DEC = 0, 4, 8
FOX_CK, FOX_CQ = 0, 16


def _cparams(semantics, vmem_mb=48):
    return pltpu.CompilerParams(dimension_semantics=semantics, vmem_limit_bytes=vmem_mb * 1024 * 1024)


def _mm(a, b):
    return jnp.dot(a.astype(BF16), b.astype(BF16), preferred_element_type=F32)


def _mm_nt(a, b):
    return lax.dot_general(a.astype(BF16), b.astype(BF16), (((1,), (1,)), ((), ())), preferred_element_type=F32)


def _mm_tn(a, b):
    return lax.dot_general(a.astype(BF16), b.astype(BF16), (((0,), (0,)), ((), ())), preferred_element_type=F32)


def _mm_f32(a, b):
    return jnp.dot(a, b, preferred_element_type=F32, precision=HIGHEST)


def _mm_mask(mask, x):
    hi = x.astype(BF16)
    r1 = x - hi.astype(F32)
    mid = r1.astype(BF16)
    lo = (r1 - mid.astype(F32)).astype(BF16)
    dot = lambda t: jnp.dot(mask, t, preferred_element_type=F32)
    return dot(hi) + dot(mid) + dot(lo)


def _rms(x, g, eps):
    return x * lax.rsqrt(jnp.mean(x * x, axis=-1, keepdims=True) + eps) * g


def _sigmoid(x):
    return 1.0 / (1.0 + jnp.exp(-x))


def _silu(x):
    return x * _sigmoid(x)


def _softplus(x):
    return jnp.maximum(x, 0.0) + jnp.log(1.0 + jnp.exp(-jnp.abs(x)))


def _tri_masks(c):
    ii = lax.broadcasted_iota(jnp.int32, (c, c), 0)
    jj = lax.broadcasted_iota(jnp.int32, (c, c), 1)
    return ii > jj, ii >= jj, ii == jj


def _block_masks():
    ii = lax.broadcasted_iota(jnp.int32, (W_MIX, W_MIX), 0)
    jj = lax.broadcasted_iota(jnp.int32, (W_MIX, W_MIX), 1)
    return (ii // HEAD_DIM) == (jj // HEAD_DIM), ii == jj


def _wide_masks():
    ii = lax.broadcasted_iota(jnp.int32, (CHUNK, W_MIX), 0)
    jj = lax.broadcasted_iota(jnp.int32, (CHUNK, W_MIX), 1) % HEAD_DIM
    return ii > jj, ii >= jj, ii == jj


def _stack_heads(x, same):
    return jnp.where(same, jnp.concatenate([x, x, x, x], axis=0), jnp.zeros((), x.dtype))


def _unit_lower_inverses(ns, eye_w, same):
    rs = [eye_w + n for n in ns]
    ps = list(ns)
    for _ in range(int(math.log2(CHUNK)) - 1):
        ps = [_mm(p, _stack_heads(p.astype(BF16), same)) for p in ps]
        rs = [r + _mm(r, _stack_heads(p.astype(BF16), same)) for r, p in zip(rs, ps)]
    return rs


def _unstack_heads(x):
    return x[0:CHUNK] + x[CHUNK:2 * CHUNK] + x[2 * CHUNK:3 * CHUNK] + x[3 * CHUNK:4 * CHUNK]


def _inproj_kernel(x_ref, g_ref, w_ref, ws_ref, wvt_ref, u_ref, scol_ref, vt_ref, h_scr):
    @pl.when(pl.program_id(1) == 0)
    def _():
        hb = _rms(x_ref[...], g_ref[...], NORM_EPS).astype(BF16)
        h_scr[...] = hb
        scol_ref[...] = jnp.dot(hb, ws_ref[...], preferred_element_type=F32)
        vt_ref[...] = lax.dot_general(wvt_ref[...], hb, (((1,), (1,)), ((), ())),
                                      preferred_element_type=F32).astype(BF16)

    u_ref[...] = jnp.dot(h_scr[...], w_ref[...], preferred_element_type=F32).astype(BF16)


def _inproj(x, g, w, ws, wvt, tm, tn):
    n = x.shape[0]
    return pl.pallas_call(
        _inproj_kernel,
        grid=(n // tm, U_COLS // tn),
        in_specs=[
            pl.BlockSpec((tm, D_MODEL), lambda i, j: (i, 0)),
            pl.BlockSpec((1, D_MODEL), lambda i, j: (0, 0)),
            pl.BlockSpec((D_MODEL, tn), lambda i, j: (0, j)),
            pl.BlockSpec((D_MODEL, LANES), lambda i, j: (0, 0)),
            pl.BlockSpec((2 * W_MIX, D_MODEL), lambda i, j: (0, 0)),
        ],
        out_specs=[
            pl.BlockSpec((tm, tn), lambda i, j: (i, j)),
            pl.BlockSpec((tm, LANES), lambda i, j: (i, 0)),
            pl.BlockSpec((2 * W_MIX, tm), lambda i, j: (0, i)),
        ],
        out_shape=[
            jax.ShapeDtypeStruct((n, U_COLS), BF16),
            jax.ShapeDtypeStruct((n, LANES), F32),
            jax.ShapeDtypeStruct((2 * W_MIX, n), BF16),
        ],
        scratch_shapes=[pltpu.VMEM((tm, D_MODEL), BF16)],
        compiler_params=_cparams(("parallel", "arbitrary"), vmem_mb=56),
        name="inproj",
    )(x, g, w, ws, wvt)


def _lane_placement(base):
    r = lax.broadcasted_iota(jnp.int32, (3 * LANES, LANES), 0)
    m = lax.broadcasted_iota(jnp.int32, (3 * LANES, LANES), 1)
    head, part = r % LANES, r // LANES
    return jnp.logical_and(head < N_HEADS, m == base + 3 * head + part).astype(BF16)


def _split3(x):
    hi = x.astype(BF16)
    r1 = x - hi.astype(F32)
    mid = r1.astype(BF16)
    lo = (r1 - mid.astype(F32)).astype(BF16)
    return jnp.concatenate([hi, mid, lo], axis=1)


def _head_expansion():
    r = lax.broadcasted_iota(jnp.int32, (3 * LANES, 2 * W_MIX), 0) % LANES
    m = lax.broadcasted_iota(jnp.int32, (3 * LANES, 2 * W_MIX), 1)
    src = jnp.where(m < W_MIX, SM_BETA + m // HEAD_DIM, SM_DEC + (m - W_MIX) // HEAD_DIM)
    return (r == src).astype(BF16)


def _small_prep_kernel(scol_ref, bias_ref, nega_ref, ogb_ref, okb_ref, oqb_ref, carry_scr):
    @pl.when(pl.program_id(1) == 0)
    def _():
        carry_scr[...] = jnp.zeros_like(carry_scr)

    tm = scol_ref.shape[0]
    lower = _tri_masks(tm)[1].astype(BF16)
    z = scol_ref[...] + bias_ref[...]
    lane = lax.broadcasted_iota(jnp.int32, z.shape, 1)
    is_f = lane < SM_BETA
    is_b = jnp.logical_and(lane >= SM_BETA, lane < SM_DEC)
    is_d = jnp.logical_and(lane >= SM_DEC, lane < SM_DEC + N_HEADS)
    logf = jnp.where(is_f, jnp.minimum(z, 0.0) - jnp.log(1.0 + jnp.exp(-jnp.abs(z))), 0.0)
    gdn = jnp.where(is_b, _sigmoid(z), jnp.where(is_d, nega_ref[...] * _softplus(z), 0.0))
    ogb_ref[...] = jnp.dot(_split3(gdn), _head_expansion(), preferred_element_type=F32)
    cum = _mm_mask(lower, logf) + carry_scr[...]
    carry_scr[...] = cum[tm - 1:tm, :]
    parts = _split3(cum * LOG2E)
    ones_lanes = jnp.logical_and(lane >= FOX_CQ, lane < FOX_CQ + 3 * N_HEADS)
    okb_ref[...] = (jnp.dot(parts, _lane_placement(FOX_CK), preferred_element_type=F32)
                    + jnp.where(ones_lanes, 1.0, 0.0)).astype(BF16)
    oqb_ref[...] = jnp.dot(parts, _lane_placement(FOX_CQ), preferred_element_type=F32).astype(BF16)


def _small_prep(scol, bias, neg_a, batch, seq, tm):
    n = batch * seq
    nt = seq // tm
    tok = lambda: pl.BlockSpec((tm, LANES), lambda b, j: (b * nt + j, 0))
    const = lambda: pl.BlockSpec((1, LANES), lambda b, j: (0, 0))
    return pl.pallas_call(
        _small_prep_kernel,
        grid=(batch, nt),
        in_specs=[tok(), const(), const()],
        out_specs=[pl.BlockSpec((tm, 2 * W_MIX), lambda b, j: (b * nt + j, 0)), tok(), tok()],
        out_shape=[jax.ShapeDtypeStruct((n, 2 * W_MIX), F32), jax.ShapeDtypeStruct((n, LANES), BF16),
                   jax.ShapeDtypeStruct((n, LANES), BF16)],
        scratch_shapes=[pltpu.VMEM((1, LANES), F32)],
        compiler_params=_cparams(("parallel", "arbitrary")),
        name="small_prep",
    )(scol, bias, neg_a)


def _rwkv_prep_kernel(u_ref, up_ref, mu_ref, w0_ref, w2_ref, a0_ref, a2_ref, g2_ref, kk_ref, ka_ref, rk_ref,
                      scan_ref, pc_ref, post_ref, *, tiles_per_seq):
    tm = u_ref.shape[0]
    u = u_ref[...].astype(F32)
    prev = up_ref[...].astype(F32)[BF16_ROWS - 1:BF16_ROWS, :]
    prev = jnp.where(pl.program_id(0) % tiles_per_seq == 0, 0.0, prev)
    rows = lax.broadcasted_iota(jnp.int32, (tm, 1), 0)
    u_prev = jnp.where(rows == 0, prev, pltpu.roll(u, 1, 0))
    xm = u + (u_prev - u) * mu_ref[...]
    r = xm[:, 0:W_MIX]
    k = xm[:, W_MIX:2 * W_MIX]
    v = xm[:, 2 * W_MIX:3 * W_MIX]
    x_lora = xm[:, 3 * W_MIX:3 * W_MIX + LANES]
    xg = xm[:, 3 * W_MIX + LANES:]
    logw = -_softplus(-(w0_ref[...] + _mm(jnp.tanh(x_lora), w2_ref[...]))) - 0.5
    log_decay = -jnp.exp(logw)
    a = _sigmoid(a0_ref[...] + _mm(x_lora, a2_ref[...]))
    g = _mm(_sigmoid(xg), g2_ref[...])
    same = _block_masks()[0].astype(F32)
    kk_raw = k * kk_ref[...]
    kk = kk_raw * lax.rsqrt(_mm_f32(kk_raw * kk_raw, same) + L2_EPS)
    k2 = k * (1.0 + (a - 1.0) * ka_ref[...])
    bonus = _mm_f32(r * k2 * rk_ref[...], same) * v
    ti = lax.broadcasted_iota(jnp.int32, (tm, tm), 0)
    tj = lax.broadcasted_iota(jnp.int32, (tm, tm), 1)
    in_chunk = jnp.logical_and(ti // CHUNK == tj // CHUNK, ti >= tj).astype(BF16)
    ci = lax.broadcasted_iota(jnp.int32, (tm // CHUNK, tm), 0)
    cj = lax.broadcasted_iota(jnp.int32, (tm // CHUNK, tm), 1)
    cum = _mm_mask(in_chunk, log_decay)
    cum_end = _mm_mask((ci == cj // CHUNK).astype(BF16), log_decay)
    inv = jnp.exp(-cum)
    scan_ref[0] = (-kk * jnp.exp(cum - log_decay)).astype(BF16)
    scan_ref[1] = (kk * a * inv).astype(BF16)
    scan_ref[2] = (k2 * inv).astype(BF16)
    scan_ref[3] = (r * jnp.exp(cum)).astype(BF16)
    scan_ref[4] = v.astype(BF16)
    pc_ref[...] = jnp.exp(cum_end)
    post_ref[0] = g
    post_ref[1] = bonus


def _rwkv_prep(u, mu, w0, w2p, a0, a2p, g2, k_k, k_a, r_k, seq, tm):
    n = u.shape[0]
    ublk = U_RWKV // D_MODEL
    row = lambda c: pl.BlockSpec((1, c), lambda i: (0, 0))
    mat = lambda r: pl.BlockSpec((r, W_MIX), lambda i: (0, 0))
    return pl.pallas_call(
        functools.partial(_rwkv_prep_kernel, tiles_per_seq=seq // tm),
        grid=(n // tm,),
        in_specs=[
            pl.BlockSpec((tm, D_MODEL), lambda i: (i, ublk)),
            pl.BlockSpec((BF16_ROWS, D_MODEL), lambda i: (jnp.maximum(i * (tm // BF16_ROWS) - 1, 0), ublk)),
            row(D_MODEL), row(W_MIX), mat(LANES), row(W_MIX), mat(LANES), mat(LANES), row(W_MIX), row(W_MIX), row(W_MIX),
        ],
        out_specs=[
            pl.BlockSpec((5, tm, W_MIX), lambda i: (0, i, 0)),
            pl.BlockSpec((tm // CHUNK, W_MIX), lambda i: (i, 0)),
            pl.BlockSpec((2, tm, W_MIX), lambda i: (0, i, 0)),
        ],
        out_shape=[jax.ShapeDtypeStruct((5, n, W_MIX), BF16), jax.ShapeDtypeStruct((n // CHUNK, W_MIX), F32),
                   jax.ShapeDtypeStruct((2, n, W_MIX), F32)],
        compiler_params=_cparams(("parallel",)),
        name="rwkv_prep",
    )(u, u, mu, w0, w2p, a0, a2p, g2, k_k, k_a, r_k)


def _rwkv_chunk_kernel(x_ref, pc_ref, o_ref, s_scr, *, nchunk, nbatch):
    @pl.when(pl.program_id(0) == 0)
    def _():
        s_scr[...] = jnp.zeros_like(s_scr)

    same, diag = _block_masks()
    eye = diag.astype(F32)
    strict_w, incl_w, diag_w = _wide_masks()
    eye_w = diag_w.astype(F32)
    stack = lambda x: _stack_heads(x.astype(BF16), same)

    def chunks(i, carry):
        items = [(b, i * CHUNK_UNROLL + j) for j in range(CHUNK_UNROLL) for b in range(nbatch)]
        each = lambda f, *lists: [f(*args) for args in zip(*lists)]
        sls = [pl.ds(pl.multiple_of(ci * CHUNK, CHUNK), CHUNK) for _, ci in items]
        a, bb, k, r, v = ([x_ref[i, b, sl, :] for (b, _), sl in zip(items, sls)] for i in range(5))
        pc = [pc_ref[b, pl.ds(ci, 1), :] for b, ci in items]
        a_s, b_s, k_s, v_s = (each(stack, x) for x in (a, bb, k, v))
        m_ab = each(lambda x, y: jnp.where(strict_w, _mm_nt(x, y), 0.0), a, b_s)
        m_ak = each(lambda x, y: jnp.where(strict_w, _mm_nt(x, y), 0.0), a, k_s)
        n_rb = each(lambda x, y: jnp.where(incl_w, _mm_nt(x, y), 0.0), r, b_s)
        n_rk = each(lambda x, y: jnp.where(incl_w, _mm_nt(x, y), 0.0), r, k_s)
        t_inv = _unit_lower_inverses(m_ab, eye_w, same)
        a2 = each(_mm, t_inv, a_s)
        u0 = each(lambda t, m, x: _mm(t, stack(_mm(m, x))), t_inv, m_ak, v_s)
        r2 = each(lambda x, n, y: x.astype(F32) + _mm(n, stack(y)), r, n_rb, a2)
        o0 = each(lambda n1, u, n2, x: _mm(n1, stack(u)) + _mm(n2, x), n_rb, u0, n_rk, v_s)
        b_end = each(lambda x, p: x.astype(F32) * p, bb, pc)
        k_end = each(lambda x, p: x.astype(F32) * p, k, pc)
        g_mat = each(lambda p, x, y: eye * p + jnp.where(same, _mm_tn(x, y), 0.0), pc, a2, b_end)
        s0 = each(lambda u, x, y, z: jnp.where(same, _mm_tn(u, x) + _mm_tn(y, z), 0.0), u0, b_end, v, k_end)
        for n, ((b, _), sl) in enumerate(zip(items, sls)):
            s = s_scr[b]
            o = _stack_heads(_mm_nt(r2[n], s) + o0[n], same)
            s_scr[b] = _mm(s, g_mat[n]) + s0[n]
            mean = jnp.sum(o, axis=-1, keepdims=True) * (1.0 / HEAD_DIM)
            cen = jnp.where(same, o - mean, 0.0)
            var = jnp.sum(cen * cen, axis=-1, keepdims=True) * (1.0 / HEAD_DIM)
            o_ref[b, sl, :] = _unstack_heads(cen * lax.rsqrt(var + RWKV_GN_EPS))
        return carry

    lax.fori_loop(0, nchunk // CHUNK_UNROLL, chunks, 0)


def _rwkv_chunk(xs, pc, batch, seq, cblk):
    xs = xs.reshape(5, batch, seq, W_MIX)
    pc = pc.reshape(batch, seq // CHUNK, W_MIX)
    out = pl.pallas_call(
        functools.partial(_rwkv_chunk_kernel, nchunk=cblk // CHUNK, nbatch=batch),
        grid=(seq // cblk,),
        in_specs=[
            pl.BlockSpec((5, batch, cblk, W_MIX), lambda j: (0, 0, j, 0)),
            pl.BlockSpec((batch, cblk // CHUNK, W_MIX), lambda j: (0, j, 0)),
        ],
        out_specs=pl.BlockSpec((batch, cblk, W_MIX), lambda j: (0, j, 0)),
        out_shape=jax.ShapeDtypeStruct((batch, seq, W_MIX), F32),
        scratch_shapes=[pltpu.VMEM((batch, W_MIX, W_MIX), F32)],
        compiler_params=_cparams(("arbitrary",)),
        name="rwkv_chunk",
    )(xs, pc)
    return out.reshape(batch * seq, W_MIX)


def _gdn_prep_kernel(u_ref, up_ref, cw_ref, o_ref, ext_scr, *, tiles_per_seq):
    tm = u_ref.shape[0]
    c3 = 3 * W_MIX
    prev = up_ref[...].astype(F32)[:, :c3]
    ext_scr[0:BF16_ROWS, :] = jnp.where(pl.program_id(0) % tiles_per_seq == 0, 0.0, prev)
    ext_scr[BF16_ROWS:, :] = u_ref[...].astype(F32)[:, :c3]
    y = jnp.zeros((tm, c3), F32)
    for j in range(GDN_CONV):
        y = y + ext_scr[pl.ds(BF16_ROWS - (GDN_CONV - 1) + j, tm), :] * cw_ref[j:j + 1, :]
    y = _silu(y)
    same = _block_masks()[0].astype(F32)
    q = y[:, 0:W_MIX]
    k = y[:, W_MIX:2 * W_MIX]
    o_ref[0] = (q * lax.rsqrt(_mm_f32(q * q, same) + L2_EPS) * (HEAD_DIM ** -0.5)).astype(BF16)
    o_ref[1] = (k * lax.rsqrt(_mm_f32(k * k, same) + L2_EPS)).astype(BF16)
    o_ref[2] = y[:, 2 * W_MIX:].astype(BF16)


def _gdn_prep(u, conv_w, seq, tm):
    n = u.shape[0]
    ublk = U_GDN // D_MODEL
    return pl.pallas_call(
        functools.partial(_gdn_prep_kernel, tiles_per_seq=seq // tm),
        grid=(n // tm,),
        in_specs=[
            pl.BlockSpec((tm, D_MODEL), lambda i: (i, ublk)),
            pl.BlockSpec((BF16_ROWS, D_MODEL), lambda i: (jnp.maximum(i * (tm // BF16_ROWS) - 1, 0), ublk)),
            pl.BlockSpec((GDN_CONV, 3 * W_MIX), lambda i: (0, 0)),
        ],
        out_specs=pl.BlockSpec((3, tm, W_MIX), lambda i: (0, i, 0)),
        out_shape=jax.ShapeDtypeStruct((3, n, W_MIX), BF16),
        scratch_shapes=[pltpu.VMEM((tm + BF16_ROWS, 3 * W_MIX), F32)],
        compiler_params=_cparams(("parallel",)),
        name="gdn_prep",
    )(u, u, conv_w)


def _gdn_chunk_kernel(x_ref, gb_ref, nw_ref, o_ref, s_scr, *, nchunk, nbatch):
    @pl.when(pl.program_id(0) == 0)
    def _():
        s_scr[...] = jnp.zeros_like(s_scr)

    same, diag = _block_masks()
    eye = diag.astype(F32)
    strict_w, incl_w, diag_w = _wide_masks()
    eye_w = diag_w.astype(F32)
    lower = _tri_masks(CHUNK)[1].astype(BF16)
    stack = lambda x: _stack_heads(x.astype(BF16), same)

    def chunks(i, carry):
        items = [(b, i * CHUNK_UNROLL + j) for j in range(CHUNK_UNROLL) for b in range(nbatch)]
        each = lambda f, *lists: [f(*args) for args in zip(*lists)]
        sls = [pl.ds(pl.multiple_of(ci * CHUNK, CHUNK), CHUNK) for _, ci in items]
        q, k, v = ([x_ref[i, b, sl, :] for (b, _), sl in zip(items, sls)] for i in range(3))
        beta = [gb_ref[b, sl, 0:W_MIX] for (b, _), sl in zip(items, sls)]
        g = [gb_ref[b, sl, W_MIX:2 * W_MIX] for (b, _), sl in zip(items, sls)]
        k_s = each(stack, k)
        gam = [_mm_mask(lower, x) for x in g]
        gam_end = [x[CHUNK - 1:CHUNK, :] for x in gam]
        gdiff = [_mm_mask(lower, jnp.where(strict_w, x, 0.0)) for x in g]
        decay = [jnp.exp(jnp.where(incl_w, x, -jnp.inf)) for x in gdiff]
        a_mat = each(lambda bt, d, x, y: jnp.where(strict_w, bt * d * _mm_nt(x, y), 0.0), beta, decay, k, k_s)
        t_inv = _unit_lower_inverses([-a for a in a_mat], eye_w, same)
        e_gam = [jnp.exp(x) for x in gam]
        u0 = each(lambda t, bt, x: _mm(t, stack(bt * x.astype(F32))), t_inv, beta, v)
        wm = each(lambda t, bt, e, x: _mm(t, stack((bt * e) * x.astype(F32))), t_inv, beta, e_gam, k)
        qk = each(lambda x, y, d: _mm_nt(x, y) * d, q, k_s, decay)
        q2 = each(lambda e, x, a, w: e * x.astype(F32) - _mm(a, stack(w)), e_gam, q, qk, wm)
        o0 = each(lambda a, u: _mm(a, stack(u)), qk, u0)
        k_end = each(lambda x, ge, ga: x.astype(F32) * jnp.exp(ge - ga), k, gam_end, gam)
        g_mat = each(lambda ge, x, w: eye * jnp.exp(ge) - jnp.where(same, _mm_tn(x, w), 0.0), gam_end, k_end, wm)
        s0 = each(lambda x, u: jnp.where(same, _mm_tn(x, u), 0.0), k_end, u0)
        for n, ((b, _), sl) in enumerate(zip(items, sls)):
            s = s_scr[b]
            o = _stack_heads(_mm(q2[n], s) + o0[n], same)
            s_scr[b] = _mm(g_mat[n], s) + s0[n]
            ms = jnp.sum(o * o, axis=-1, keepdims=True) * (1.0 / HEAD_DIM)
            o_ref[b, sl, :] = _unstack_heads(o * lax.rsqrt(ms + NORM_EPS)) * nw_ref[...]
        return carry

    lax.fori_loop(0, nchunk // CHUNK_UNROLL, chunks, 0)


def _gdn_chunk(xs, gb, norm_w, batch, seq, cblk):
    xs = xs.reshape(3, batch, seq, W_MIX)
    gb = gb.reshape(batch, seq, 2 * W_MIX)
    out = pl.pallas_call(
        functools.partial(_gdn_chunk_kernel, nchunk=cblk // CHUNK, nbatch=batch),
        grid=(seq // cblk,),
        in_specs=[
            pl.BlockSpec((3, batch, cblk, W_MIX), lambda j: (0, 0, j, 0)),
            pl.BlockSpec((batch, cblk, 2 * W_MIX), lambda j: (0, j, 0)),
            pl.BlockSpec((1, W_MIX), lambda j: (0, 0)),
        ],
        out_specs=pl.BlockSpec((batch, cblk, W_MIX), lambda j: (0, j, 0)),
        out_shape=jax.ShapeDtypeStruct((batch, seq, W_MIX), F32),
        scratch_shapes=[pltpu.VMEM((batch, W_MIX, W_MIX), F32)],
        compiler_params=_cparams(("arbitrary",)),
        name="gdn_chunk",
    )(xs, gb, norm_w)
    return out.reshape(batch * seq, W_MIX)


def _causal_pairs(nq):
    pairs = [(i, j) for i in range(nq) for j in range(i + 1)]
    return jnp.asarray(np.array([p[0] for p in pairs], np.int32)), jnp.asarray(np.array([p[1] for p in pairs], np.int32))


def _softmax_updates(scores, vt_ones, m_scr, acc_scr):
    probs = []
    for i, s in enumerate(scores):
        m_old = m_scr[i]
        m_new = jnp.maximum(m_old, jnp.max(s, axis=0, keepdims=True))
        m_scr[i] = m_new
        probs.append((jnp.exp2(m_old - m_new), jnp.exp2((s - m_new).astype(BF16))))
    for i, (alpha, p) in enumerate(probs):
        acc_scr[i] = alpha * acc_scr[i] + jnp.dot(vt_ones[i], p, preferred_element_type=F32)


def _pair_lanes(h):
    p = h // 2
    return slice(p * LANES, (p + 1) * LANES), h % 2 == 0


def _values_and_ones(vt, tk):
    row = lax.broadcasted_iota(jnp.int32, (LANES, tk), 0)
    out = []
    for h in range(N_HEADS):
        slab, low = _pair_lanes(h)
        mine = row < HEAD_DIM if low else row >= HEAD_DIM
        out.append(jnp.where(mine, vt[slab, :], jnp.ones((), vt.dtype)))
    return out


def _normalized(acc, low):
    if low:
        return acc[0:HEAD_DIM] / acc[HEAD_DIM:HEAD_DIM + 1]
    return acc[HEAD_DIM:] / acc[0:1]


def _key_after_query(tq):
    return lax.broadcasted_iota(jnp.int32, (tq, tq), 0) > lax.broadcasted_iota(jnp.int32, (tq, tq), 1)


def _fox_kernel(qi_ref, kj_ref, q_ref, k_ref, vt_ref, qb_ref, kb_ref, o_ref, qm_scr, m_scr, acc_scr):
    t = pl.program_id(1)
    qi = qi_ref[t]
    kj = kj_ref[t]
    tq = q_ref.shape[0]

    @pl.when(kj == 0)
    def _():
        m_scr[...] = jnp.full_like(m_scr, -jnp.inf)
        acc_scr[...] = jnp.zeros_like(acc_scr)
        q = q_ref[...]
        qb = qb_ref[...]
        lane = lax.broadcasted_iota(jnp.int32, (tq, LANES), 1)
        zero = jnp.zeros((), BF16)
        for h in range(N_HEADS):
            slab, low = _pair_lanes(h)
            mine = lane < HEAD_DIM if low else lane >= HEAD_DIM
            ck_lanes = jnp.logical_and(lane >= FOX_CK + 3 * h, lane < FOX_CK + 3 * h + 3)
            cq_lanes = jnp.logical_and(lane >= FOX_CQ + 3 * h, lane < FOX_CQ + 3 * h + 3)
            bias = jnp.where(ck_lanes, -jnp.ones((), BF16), jnp.where(cq_lanes, qb, zero))
            qm_scr[h] = jnp.concatenate([jnp.where(mine, q[:, slab], zero), bias], axis=1)

    def step(diagonal):
        k = k_ref[...]
        kb = kb_ref[...]
        vts = _values_and_ones(vt_ref[...], tq)
        if diagonal:
            masked = _key_after_query(tq)
        keys = [jnp.concatenate([k[:, p * LANES:(p + 1) * LANES], kb], axis=1) for p in range(N_HEADS // 2)]
        scores = []
        for h in range(N_HEADS):
            s = lax.dot_general(keys[h // 2], qm_scr[h], (((1,), (1,)), ((), ())), preferred_element_type=F32)
            if diagonal:
                s = jnp.where(masked, -jnp.inf, s)
            scores.append(s)
        _softmax_updates(scores, vts, m_scr, acc_scr)

    @pl.when(kj < qi)
    def _():
        step(False)

    @pl.when(kj == qi)
    def _():
        step(True)
        for p in range(N_HEADS // 2):
            pair = jnp.concatenate([_normalized(acc_scr[2 * p], True), _normalized(acc_scr[2 * p + 1], False)], axis=0)
            o_ref[:, p * LANES:(p + 1) * LANES] = pair.T


def _attn_specs(nq, tq, ucol, vt_rows):
    cb = ucol // W_MIX
    q_spec = pl.BlockSpec((tq, W_MIX), lambda b, t, qi, kj: (b * nq + qi[t], cb))
    k_spec = pl.BlockSpec((tq, W_MIX), lambda b, t, qi, kj: (b * nq + kj[t], cb + 1))
    vt_spec = pl.BlockSpec((W_MIX, tq), lambda b, t, qi, kj: (vt_rows // W_MIX, b * nq + kj[t]))
    return q_spec, k_spec, vt_spec


def _fox_attention(u, vt, qb, kb, batch, seq, tq):
    nq = seq // tq
    qi, kj = _causal_pairs(nq)
    q_spec, k_spec, vt_spec = _attn_specs(nq, tq, U_FOX, W_MIX)
    stat = lambda: pltpu.VMEM((N_HEADS, 1, tq), F32)
    return pl.pallas_call(
        _fox_kernel,
        grid_spec=pltpu.PrefetchScalarGridSpec(
            num_scalar_prefetch=2,
            grid=(batch, qi.shape[0]),
            in_specs=[
                q_spec, k_spec, vt_spec,
                pl.BlockSpec((tq, LANES), lambda b, t, qi, kj: (b * nq + qi[t], 0)),
                pl.BlockSpec((tq, LANES), lambda b, t, qi, kj: (b * nq + kj[t], 0)),
            ],
            out_specs=pl.BlockSpec((tq, W_MIX), lambda b, t, qi, kj: (b * nq + qi[t], 0)),
            scratch_shapes=[pltpu.VMEM((N_HEADS, tq, 2 * LANES), BF16), stat(),
                            pltpu.VMEM((N_HEADS, LANES, tq), F32)],
        ),
        out_shape=jax.ShapeDtypeStruct((batch * seq, W_MIX), F32),
        compiler_params=_cparams(("parallel", "arbitrary")),
        name="fox_attention",
    )(qi, kj, u, u, vt, qb, kb)


def _diff_kernel(qi_ref, kj_ref, q_ref, k_ref, vt_ref, cq_ref, sq_ref, ck_ref, sk_ref, rot_ref, lam_ref, ln_ref, o_ref,
                 qm_scr, m_scr, acc_scr, *, lam_init):
    t = pl.program_id(1)
    qi = qi_ref[t]
    kj = kj_ref[t]
    tq = q_ref.shape[0]
    lane = lax.broadcasted_iota(jnp.int32, (tq, LANES), 1)

    def rope(x, cos, sin):
        return x.astype(F32) * cos + jnp.dot(x, rot_ref[...], preferred_element_type=F32) * sin

    @pl.when(kj == 0)
    def _():
        m_scr[...] = jnp.full_like(m_scr, -jnp.inf)
        acc_scr[...] = jnp.zeros_like(acc_scr)
        q = rope(q_ref[...], cq_ref[...], sq_ref[...])
        for h in range(N_HEADS):
            slab, low = _pair_lanes(h)
            base = 0 if low else HEAD_DIM
            for c in range(2):
                lo = base + c * DIFF_DH
                sel = jnp.logical_and(lane >= lo, lane < lo + DIFF_DH)
                qm_scr[2 * h + c] = jnp.where(sel, q[:, slab], 0.0).astype(BF16)

    def step(diagonal):
        k = rope(k_ref[...], ck_ref[...], sk_ref[...]).astype(BF16)
        vts = _values_and_ones(vt_ref[...], tq)
        if diagonal:
            masked = _key_after_query(tq)
        scores = []
        for h in range(N_HEADS):
            slab, _ = _pair_lanes(h)
            for c in range(2):
                i = 2 * h + c
                s = lax.dot_general(k[:, slab], qm_scr[i], (((1,), (1,)), ((), ())), preferred_element_type=F32)
                if diagonal:
                    s = jnp.where(masked, -jnp.inf, s)
                scores.append(s)
        _softmax_updates(scores, [vts[i // 2] for i in range(2 * N_HEADS)], m_scr, acc_scr)

    @pl.when(kj < qi)
    def _():
        step(False)

    @pl.when(kj == qi)
    def _():
        step(True)
        lp = lam_ref[...]
        lam = (jnp.exp(jnp.sum(lp[0:1] * lp[1:2], axis=-1, keepdims=True))
               - jnp.exp(jnp.sum(lp[2:3] * lp[3:4], axis=-1, keepdims=True)) + lam_init)
        head = lambda h: (_normalized(acc_scr[2 * h], h % 2 == 0) - lam * _normalized(acc_scr[2 * h + 1], h % 2 == 0))
        is_lo = lane < HEAD_DIM
        for p in range(N_HEADS // 2):
            o = jnp.concatenate([head(2 * p), head(2 * p + 1)], axis=0).T
            sq = o * o
            ms_lo = jnp.sum(jnp.where(is_lo, sq, 0.0), axis=-1, keepdims=True)
            ms_hi = jnp.sum(jnp.where(is_lo, 0.0, sq), axis=-1, keepdims=True)
            ms = jnp.where(is_lo, ms_lo, ms_hi) * (1.0 / HEAD_DIM)
            o_ref[:, p * LANES:(p + 1) * LANES] = o * lax.rsqrt(ms + DIFF_LN_EPS) * ln_ref[...] * (1.0 - lam_init)


def _diff_attention(u, vt, cos, sin, rot, lam_p, subln, batch, seq, tq, lam_init):
    nq = seq // tq
    qi, kj = _causal_pairs(nq)
    q_spec, k_spec, vt_spec = _attn_specs(nq, tq, U_DIFF, 0)
    tab_q = pl.BlockSpec((tq, W_MIX), lambda b, t, qi, kj: (qi[t], 0))
    tab_k = pl.BlockSpec((tq, W_MIX), lambda b, t, qi, kj: (kj[t], 0))
    const = lambda r, c: pl.BlockSpec((r, c), lambda b, t, qi, kj: (0, 0))
    stat = lambda: pltpu.VMEM((2 * N_HEADS, 1, tq), F32)
    return pl.pallas_call(
        functools.partial(_diff_kernel, lam_init=lam_init),
        grid_spec=pltpu.PrefetchScalarGridSpec(
            num_scalar_prefetch=2,
            grid=(batch, qi.shape[0]),
            in_specs=[q_spec, k_spec, vt_spec, tab_q, tab_q, tab_k, tab_k,
                      const(W_MIX, W_MIX), const(4, DIFF_DH), const(1, LANES)],
            out_specs=pl.BlockSpec((tq, W_MIX), lambda b, t, qi, kj: (b * nq + qi[t], 0)),
            scratch_shapes=[pltpu.VMEM((2 * N_HEADS, tq, LANES), BF16), stat(),
                            pltpu.VMEM((2 * N_HEADS, LANES, tq), F32)],
        ),
        out_shape=jax.ShapeDtypeStruct((batch * seq, W_MIX), F32),
        compiler_params=_cparams(("parallel", "arbitrary")),
        name="diff_attention",
    )(qi, kj, u, u, vt, cos, sin, cos, sin, rot, lam_p, subln)


def _merge_kernel(x_ref, gate_ref, oa_ref, post_ref, yb_ref, yc_ref, od_ref, gd_ref, lnw_ref, lnb_ref,
                  wbo_ref, wout_ref, o_ref):
    y_a = (oa_ref[...] * lnw_ref[...] + lnb_ref[...] + post_ref[1]) * post_ref[0]
    y_d = od_ref[...] * _silu(gd_ref[...].astype(F32))
    acc = jnp.zeros(x_ref.shape, F32)
    for b, y in enumerate((y_a, yb_ref[...], yc_ref[...], y_d)):
        gate = _sigmoid(gate_ref[:, b * D_MODEL:(b + 1) * D_MODEL].astype(F32))
        acc = acc + gate * _mm(y, wbo_ref[b])
    o_ref[...] = x_ref[...] + _mm(acc, wout_ref[...])


def _merge(x, u, o_a, post, y_b, y_c, o_d, ln_w, ln_b, w_bo, w_out, tm):
    n = x.shape[0]
    tok = lambda c: pl.BlockSpec((tm, c), lambda i: (i, 0))
    return pl.pallas_call(
        _merge_kernel,
        grid=(n // tm,),
        in_specs=[
            tok(D_MODEL),
            pl.BlockSpec((tm, 4 * D_MODEL), lambda i: (i, U_GATE // (4 * D_MODEL))),
            tok(W_MIX),
            pl.BlockSpec((2, tm, W_MIX), lambda i: (0, i, 0)),
            tok(W_MIX), tok(W_MIX), tok(W_MIX),
            pl.BlockSpec((tm, W_MIX), lambda i: (i, (U_GDN + 3 * W_MIX) // W_MIX)),
            pl.BlockSpec((1, W_MIX), lambda i: (0, 0)),
            pl.BlockSpec((1, W_MIX), lambda i: (0, 0)),
            pl.BlockSpec((4, W_MIX, D_MODEL), lambda i: (0, 0, 0)),
            pl.BlockSpec((D_MODEL, D_MODEL), lambda i: (0, 0)),
        ],
        out_specs=tok(D_MODEL),
        out_shape=jax.ShapeDtypeStruct((n, D_MODEL), F32),
        compiler_params=_cparams(("parallel",)),
        name="merge",
    )(x, u, o_a, post, y_b, y_c, o_d, u, ln_w, ln_b, w_bo, w_out)


def _ffn_kernel(x_ref, g_ref, wg_ref, wu_ref, wd_ref, o_ref, h_scr, acc_scr):
    f = pl.program_id(1)

    @pl.when(f == 0)
    def _():
        h_scr[...] = _rms(x_ref[...], g_ref[...], NORM_EPS).astype(BF16)
        acc_scr[...] = jnp.zeros_like(acc_scr)

    h = h_scr[...]
    act = _silu(jnp.dot(h, wg_ref[...], preferred_element_type=F32)) * jnp.dot(h, wu_ref[...], preferred_element_type=F32)
    acc_scr[...] += _mm(act, wd_ref[...])

    @pl.when(f == pl.num_programs(1) - 1)
    def _():
        o_ref[...] = x_ref[...] + acc_scr[...]


def _ffn(x, g, wg, wu, wd, tm, tf):
    n = x.shape[0]
    ff = wg.shape[1]
    return pl.pallas_call(
        _ffn_kernel,
        grid=(n // tm, ff // tf),
        in_specs=[
            pl.BlockSpec((tm, D_MODEL), lambda i, f: (i, 0)),
            pl.BlockSpec((1, D_MODEL), lambda i, f: (0, 0)),
            pl.BlockSpec((D_MODEL, tf), lambda i, f: (0, f)),
            pl.BlockSpec((D_MODEL, tf), lambda i, f: (0, f)),
            pl.BlockSpec((tf, D_MODEL), lambda i, f: (f, 0)),
        ],
        out_specs=pl.BlockSpec((tm, D_MODEL), lambda i, f: (i, 0)),
        out_shape=jax.ShapeDtypeStruct((n, D_MODEL), F32),
        scratch_shapes=[pltpu.VMEM((tm, D_MODEL), BF16), pltpu.VMEM((tm, D_MODEL), F32)],
        compiler_params=_cparams(("parallel", "arbitrary")),
        name="ffn",
    )(x, g, wg, wu, wd)


def _router_kernel(x_ref, g_ref, router_ref, h_ref, c_ref, rc_ref, rr_ref, cnt_ref):
    t = x_ref.shape[0]
    h = _rms(x_ref[...], g_ref[...], NORM_EPS)
    h_ref[...] = h.astype(BF16)
    logits = _mm_f32(h, router_ref[...])
    lane = lax.broadcasted_iota(jnp.int32, logits.shape, 1).astype(F32)
    lg = jnp.where(lane < N_EXPERTS, logits, -jnp.inf)
    m1 = jnp.max(lg, axis=-1, keepdims=True)
    i1 = jnp.min(jnp.where(lg == m1, lane, float(LANES)), axis=-1, keepdims=True)
    lg2 = jnp.where(lane == i1, -jnp.inf, lg)
    m2 = jnp.max(lg2, axis=-1, keepdims=True)
    i2 = jnp.min(jnp.where(lg2 == m2, lane, float(LANES)), axis=-1, keepdims=True)
    e2 = jnp.exp(m2 - m1)
    c_ref[...] = jnp.where(lane == i1, 1.0 / (1.0 + e2), 0.0) + jnp.where(lane == i2, e2 / (1.0 + e2), 0.0)
    sel = jnp.logical_or(lane == i1, lane == i2)
    sel_f = jnp.where(sel, 1.0, 0.0)
    earlier = _tri_masks(t)[0].astype(BF16)
    rank = jnp.dot(earlier, sel_f.astype(BF16), preferred_element_type=F32)
    rc = jnp.where(sel, rank, -1.0)
    rc_ref[...] = rc
    rr_ref[...] = rc.T[0:BF16_ROWS, :]
    cnt_ref[...] = jnp.broadcast_to(jnp.sum(sel_f, axis=0, keepdims=True), cnt_ref.shape).astype(jnp.int32)


def _router(x, g, router, tm):
    n = x.shape[0]
    return pl.pallas_call(
        _router_kernel,
        grid=(n // tm,),
        in_specs=[
            pl.BlockSpec((tm, D_MODEL), lambda i: (i, 0)),
            pl.BlockSpec((1, D_MODEL), lambda i: (0, 0)),
            pl.BlockSpec((D_MODEL, LANES), lambda i: (0, 0)),
        ],
        out_specs=[
            pl.BlockSpec((tm, D_MODEL), lambda i: (i, 0)),
            pl.BlockSpec((tm, LANES), lambda i: (i, 0)),
            pl.BlockSpec((tm, LANES), lambda i: (i, 0)),
            pl.BlockSpec((BF16_ROWS, tm), lambda i: (0, i)),
            pl.BlockSpec((8, LANES), lambda i: (i, 0)),
        ],
        out_shape=[
            jax.ShapeDtypeStruct((n, D_MODEL), BF16),
            jax.ShapeDtypeStruct((n, LANES), F32),
            jax.ShapeDtypeStruct((n, LANES), F32),
            jax.ShapeDtypeStruct((BF16_ROWS, n), F32),
            jax.ShapeDtypeStruct((n // tm * 8, LANES), jnp.int32),
        ],
        compiler_params=_cparams(("parallel",)),
        name="router",
    )(x, g, router)


def _moe_kernel(cnt_ref, x_ref, h_ref, c_ref, rc_ref, rr_ref, wg_ref, wu_ref, wd_ref, o_ref, xg_scr, yg_scr, *, rows):
    i = pl.program_id(0)
    e = pl.program_id(1)
    f = pl.program_id(2)
    last_f = f == pl.num_programs(2) - 1
    t = h_ref.shape[0]
    nsub = xg_scr.shape[0] // rows
    cnt = cnt_ref[i * N_EXPERTS + e]
    blocks = [(sb, slice(sb * rows, (sb + 1) * rows)) for sb in range(nsub)]

    @pl.when(jnp.logical_and(e == 0, f == 0))
    def _():
        o_ref[...] = x_ref[...]

    @pl.when(f == 0)
    def _():
        rr = rr_ref[pl.ds(e, 1), :]
        for sb, rs in blocks:
            @pl.when(cnt > sb * rows)
            def _():
                slot = (lax.broadcasted_iota(jnp.int32, (rows, t), 0) + sb * rows).astype(F32)
                gather = jnp.where(rr == slot, 1.0, 0.0).astype(BF16)
                xg_scr[rs, :] = jnp.dot(gather, h_ref[...], preferred_element_type=F32).astype(BF16)
                yg_scr[rs, :] = jnp.zeros((rows, D_MODEL), F32)

    for sb, rs in blocks:
        @pl.when(cnt > sb * rows)
        def _():
            xb = xg_scr[rs, :]
            act = (_silu(jnp.dot(xb, wg_ref[0], preferred_element_type=F32))
                   * jnp.dot(xb, wu_ref[0], preferred_element_type=F32))
            yg_scr[rs, :] += _mm(act, wd_ref[0])

    @pl.when(last_f)
    def _():
        lane = lax.broadcasted_iota(jnp.int32, (t, LANES), 1)
        mine = lane == e
        rc = jnp.sum(jnp.where(mine, rc_ref[...], 0.0), axis=-1, keepdims=True)
        ce = jnp.sum(jnp.where(mine, c_ref[...], 0.0), axis=-1, keepdims=True)
        for sb, rs in blocks:
            @pl.when(cnt > sb * rows)
            def _():
                slot = (lax.broadcasted_iota(jnp.int32, (t, rows), 1) + sb * rows).astype(F32)
                scatter = jnp.where(rc == slot, 1.0, 0.0).astype(BF16)
                o_ref[...] += ce * jnp.dot(scatter, yg_scr[rs, :].astype(BF16), preferred_element_type=F32)


def _moe(x, g, router, wg, wu, wd, tm, tf, rows):
    n = x.shape[0]
    ff = wg.shape[2]
    h, c, rc, rr, cnt = _router(x, g, router, tm)
    cnt = cnt.reshape(n // tm, 8, LANES)[:, 0, :N_EXPERTS].reshape(-1)
    nsub = -(-tm // rows)
    tok = lambda cols, **kw: pl.BlockSpec((tm, cols), lambda i, e, f, cnt: (i, 0), **kw)
    once = dict(pipeline_mode=pl.Buffered(1))
    return pl.pallas_call(
        functools.partial(_moe_kernel, rows=rows),
        grid_spec=pltpu.PrefetchScalarGridSpec(
            num_scalar_prefetch=1,
            grid=(n // tm, N_EXPERTS, ff // tf),
            in_specs=[
                tok(D_MODEL, **once), tok(D_MODEL, **once), tok(LANES), tok(LANES),
                pl.BlockSpec((BF16_ROWS, tm), lambda i, e, f, cnt: (0, i)),
                pl.BlockSpec((1, D_MODEL, tf), lambda i, e, f, cnt: (e, 0, f)),
                pl.BlockSpec((1, D_MODEL, tf), lambda i, e, f, cnt: (e, 0, f)),
                pl.BlockSpec((1, tf, D_MODEL), lambda i, e, f, cnt: (e, f, 0)),
            ],
            out_specs=tok(D_MODEL),
            scratch_shapes=[pltpu.VMEM((nsub * rows, D_MODEL), BF16), pltpu.VMEM((nsub * rows, D_MODEL), F32)],
        ),
        out_shape=jax.ShapeDtypeStruct((n, D_MODEL), F32),
        compiler_params=_cparams(("parallel", "arbitrary", "arbitrary"), vmem_mb=56),
        name="moe",
    )(cnt, x, h, c, rc, rr, wg, wu, wd)


def _ple_kernel(x_ref, p_ref, g_ref, wgate_ref, wproj_ref, fin_ref, o_ref, *, final):
    x = x_ref[...]
    h = _rms(x, g_ref[...], NORM_EPS)
    y = x + _sigmoid(_mm(h, wgate_ref[...])) * _mm(p_ref[...], wproj_ref[...])
    o_ref[...] = _rms(y, fin_ref[...], NORM_EPS) if final else y


def _ple(x, p, g, wgate, wproj, fin, tm, final):
    n = x.shape[0]
    return pl.pallas_call(
        functools.partial(_ple_kernel, final=final),
        grid=(n // tm,),
        in_specs=[
            pl.BlockSpec((tm, D_MODEL), lambda i: (i, 0)),
            pl.BlockSpec((tm, P_DIM), lambda i: (i, 0)),
            pl.BlockSpec((1, D_MODEL), lambda i: (0, 0)),
            pl.BlockSpec((D_MODEL, D_MODEL), lambda i: (0, 0)),
            pl.BlockSpec((P_DIM, D_MODEL), lambda i: (0, 0)),
            pl.BlockSpec((1, D_MODEL), lambda i: (0, 0)),
        ],
        out_specs=pl.BlockSpec((tm, D_MODEL), lambda i: (i, 0)),
        out_shape=jax.ShapeDtypeStruct((n, D_MODEL), F32),
        compiler_params=_cparams(("parallel",)),
        name="ple",
    )(x, p, g, wgate, wproj, fin)


def _tiles(n, seq):
    tm = min(512, seq)
    tm_big = 1024 if n % 1024 == 0 else tm
    tm_in = 2048 if n % 2048 == 0 else tm_big
    moe_rows = tm_big // 4 + tm_big // 32
    return dict(tm=tm, tm_big=tm_big, tm_in=tm_in, tn_in=1024, tq=min(512, seq), cblk=min(512, seq), moe_rows=moe_rows)


def _rope_tables(seq):
    half = ROPE_DIMS // 2
    inv = ROPE_THETA ** (-jnp.arange(half, dtype=F32) * 2.0 / ROPE_DIMS)
    ang = jnp.arange(seq, dtype=F32)[:, None] * inv[None, :]
    pad = jnp.zeros((seq, DIFF_DH - ROPE_DIMS), F32)
    cos = jnp.concatenate([jnp.cos(ang), jnp.cos(ang), pad + 1.0], axis=-1)
    sin = jnp.concatenate([jnp.sin(ang), jnp.sin(ang), pad], axis=-1)
    d = jnp.arange(W_MIX)
    dd = d % DIFF_DH
    src = jnp.where(dd < half, d + half, d - half)
    sign = jnp.where(dd < half, -1.0, jnp.where(dd < ROPE_DIMS, 1.0, 0.0))
    rot = jnp.zeros((W_MIX, W_MIX), F32).at[src, d].set(sign)
    reps = W_MIX // DIFF_DH
    return jnp.tile(cos, (1, reps)), jnp.tile(sin, (1, reps)), rot.astype(BF16)


def _split_w_in(w):
    a0 = 0
    b0 = a0 + 4 * W_MIX
    c0 = b0 + 3 * W_MIX
    d0 = c0 + 3 * W_MIX + N_HEADS
    g0 = d0 + 4 * W_MIX + 2 * N_HEADS
    d_small = d0 + 3 * W_MIX
    diff_q = w[:, b0:b0 + W_MIX] * (DIFF_DH ** -0.5 * LOG2E)
    fox_q = w[:, c0:c0 + W_MIX] * (HEAD_DIM ** -0.5 * LOG2E)
    main = jnp.concatenate([
        w[:, g0:], w[:, a0:b0], w[:, d0:d_small], w[:, d_small + 2 * N_HEADS:g0],
        diff_q, w[:, b0 + W_MIX:b0 + 2 * W_MIX], fox_q, w[:, c0 + W_MIX:c0 + 2 * W_MIX],
    ], axis=1).astype(BF16)
    small = jnp.concatenate([
        w[:, c0 + 3 * W_MIX:d0], w[:, d_small:d_small + 2 * N_HEADS],
        jnp.zeros((D_MODEL, LANES - 3 * N_HEADS), w.dtype),
    ], axis=1).astype(BF16)
    v_t = jnp.concatenate([w[:, b0 + 2 * W_MIX:c0], w[:, c0 + 2 * W_MIX:c0 + 3 * W_MIX]], axis=1).T.astype(BF16)
    return main, small, v_t


def _small_params(fbias, a_log, dt_bias):
    zeros = jnp.zeros((N_HEADS,), F32)
    bias = jnp.concatenate([fbias, zeros, dt_bias, jnp.zeros((LANES - 3 * N_HEADS,), F32)])
    neg_a = jnp.concatenate([zeros, zeros, -jnp.exp(a_log), jnp.zeros((LANES - 3 * N_HEADS,), F32)])
    return bias.reshape(1, LANES), neg_a.reshape(1, LANES)


def _pad_rows(w, top, total):
    return jnp.concatenate([jnp.zeros((top, w.shape[1]), w.dtype), w,
                            jnp.zeros((total - top - w.shape[0], w.shape[1]), w.dtype)], axis=0)


def kernel(x, p, norm_mix, norm_ffn, norm_ple, w_in, w_bo, w_out, rwkv_mu, rwkv_w0, rwkv_w2, rwkv_a0, rwkv_a2, rwkv_g2, rwkv_kk, rwkv_ka, rwkv_rk, rwkv_ln_w, rwkv_ln_b, diff_lam, diff_subln, fox_fbias, gdn_conv, gdn_a_log, gdn_dt_bias, gdn_norm, ffn_w_gate, ffn_w_up, ffn_w_down, moe_router, moe_w_gate, moe_w_up, moe_w_down, ple_proj, ple_gate, final_norm):
    batch, seq, _ = x.shape
    depth = w_in.shape[0]
    n = batch * seq
    t = _tiles(n, seq)
    tm, tq, cblk = t["tm"], t["tq"], t["cblk"]
    row = lambda v: v.reshape(1, -1).astype(F32)
    cos, sin, rot = _rope_tables(seq)
    xf = x.reshape(n, D_MODEL)
    pf = p.reshape(depth, n, P_DIM)

    for i in range(depth):
        w_main, w_small, w_vt = _split_w_in(w_in[i])
        u, scol, vt = _inproj(xf, row(norm_mix[i]), w_main, w_small, w_vt, t["tm_in"], t["tn_in"])
        bias, neg_a = _small_params(fox_fbias[i], gdn_a_log[i], gdn_dt_bias[i])
        hcol, fox_kb, fox_qb = _small_prep(scol, bias, neg_a, batch, seq, tm)

        scan_in, pc, post = _rwkv_prep(
            u, row(rwkv_mu[i]), row(rwkv_w0[i]), _pad_rows(rwkv_w2[i], 0, LANES), row(rwkv_a0[i]),
            _pad_rows(rwkv_a2[i], LANES // 2, LANES), rwkv_g2[i], row(rwkv_kk[i]), row(rwkv_ka[i]), row(rwkv_rk[i]),
            seq, tm)
        o_a = _rwkv_chunk(scan_in, pc, batch, seq, cblk)

        o_d = _gdn_chunk(_gdn_prep(u, gdn_conv[i].T, seq, tm), hcol, jnp.tile(row(gdn_norm[i]), (1, N_HEADS)),
                         batch, seq, cblk)

        lam_init = 0.8 - 0.6 * math.exp(-0.3 * i)
        y_b = _diff_attention(u, vt, cos, sin, rot, diff_lam[i].astype(F32), jnp.tile(row(diff_subln[i]), (1, 2)),
                              batch, seq, tq, lam_init)
        y_c = _fox_attention(u, vt, fox_qb, fox_kb, batch, seq, tq)

        xf = _merge(xf, u, o_a, post, y_b, y_c, o_d, row(rwkv_ln_w[i]), row(rwkv_ln_b[i]),
                    w_bo[i].astype(BF16), w_out[i].astype(BF16), tm)

        j = i // 2
        if i % 2 == 0:
            xf = _ffn(xf, row(norm_ffn[i]), ffn_w_gate[j].astype(BF16), ffn_w_up[j].astype(BF16),
                      ffn_w_down[j].astype(BF16), tm, ffn_w_gate.shape[2] // 2)
        else:
            router = jnp.concatenate([moe_router[j], jnp.zeros((D_MODEL, LANES - N_EXPERTS), F32)], axis=1)
            xf = _moe(xf, row(norm_ffn[i]), router, moe_w_gate[j].astype(BF16), moe_w_up[j].astype(BF16),
                      moe_w_down[j].astype(BF16), t["tm_big"], moe_w_gate.shape[3] // 2, t["moe_rows"])
        xf = _ple(xf, pf[i], row(norm_ple[i]), ple_gate[i].astype(BF16), ple_proj[i].astype(BF16),
                  row(final_norm), tm, i == depth - 1)
    return xf.reshape(batch, seq, D_MODEL)
```

```python
import functools
import math

import jax
import jax.numpy as jnp
import numpy as np
from jax import lax
from jax.experimental import pallas as pl
from jax.experimental.pallas import tpu as pltpu

F32 = jnp.float32
BF16 = jnp.bfloat16
HIGHEST = lax.Precision.HIGHEST

D_MODEL = 1024
P_DIM = 256
W_MIX = 256
HEAD_DIM = 64
N_HEADS = 4
DIFF_DH = 32
ROPE_THETA = 500000.0
ROPE_DIMS = 8
RWKV_GN_EPS = 64e-5
DIFF_LN_EPS = 1e-5
GDN_CONV = 4
CHUNK = 64
CHUNK_UNROLL = 4
N_EXPERTS = 8
NORM_EPS = 1e-6
L2_EPS = 1e-6
LOG2E = math.log2(math.e)
LANES = 128
BF16_ROWS = 16
assert CHUNK == HEAD_DIM

U_GATE = 0
U_RWKV = 4096
U_GDN = 5120
U_DIFF = 6144
U_FOX = 6656
U_COLS = 7168
SM_FOX, SM_BETA, SM_DEC = 0, 4, 8
FOX_CK, FOX_CQ = 0, 16


def _cparams(semantics, vmem_mb=48):
    return pltpu.CompilerParams(dimension_semantics=semantics, vmem_limit_bytes=vmem_mb * 1024 * 1024)


def _mm(a, b):
    return jnp.dot(a.astype(BF16), b.astype(BF16), preferred_element_type=F32)


def _mm_nt(a, b):
    return lax.dot_general(a.astype(BF16), b.astype(BF16), (((1,), (1,)), ((), ())), preferred_element_type=F32)


def _mm_tn(a, b):
    return lax.dot_general(a.astype(BF16), b.astype(BF16), (((0,), (0,)), ((), ())), preferred_element_type=F32)


def _mm_f32(a, b):
    return jnp.dot(a, b, preferred_element_type=F32, precision=HIGHEST)


def _mm_mask(mask, x):
    hi = x.astype(BF16)
    r1 = x - hi.astype(F32)
    mid = r1.astype(BF16)
    lo = (r1 - mid.astype(F32)).astype(BF16)
    dot = lambda t: jnp.dot(mask, t, preferred_element_type=F32)
    return dot(hi) + dot(mid) + dot(lo)


def _rms(x, g, eps):
    return x * lax.rsqrt(jnp.mean(x * x, axis=-1, keepdims=True) + eps) * g


def _sigmoid(x):
    return 1.0 / (1.0 + jnp.exp(-x))


def _silu(x):
    return x * _sigmoid(x)


def _softplus(x):
    return jnp.maximum(x, 0.0) + jnp.log(1.0 + jnp.exp(-jnp.abs(x)))


def _tri_masks(c):
    ii = lax.broadcasted_iota(jnp.int32, (c, c), 0)
    jj = lax.broadcasted_iota(jnp.int32, (c, c), 1)
    return ii > jj, ii >= jj, ii == jj


def _block_masks():
    ii = lax.broadcasted_iota(jnp.int32, (W_MIX, W_MIX), 0)
    jj = lax.broadcasted_iota(jnp.int32, (W_MIX, W_MIX), 1)
    return (ii // HEAD_DIM) == (jj // HEAD_DIM), ii == jj


def _wide_masks():
    ii = lax.broadcasted_iota(jnp.int32, (CHUNK, W_MIX), 0)
    jj = lax.broadcasted_iota(jnp.int32, (CHUNK, W_MIX), 1) % HEAD_DIM
    return ii > jj, ii >= jj, ii == jj


def _stack_heads(x, same):
    return jnp.where(same, jnp.concatenate([x, x, x, x], axis=0), jnp.zeros((), x.dtype))


def _unit_lower_inverses(ns, eye_w, same):
    rs = [eye_w + n for n in ns]
    ps = list(ns)
    for _ in range(int(math.log2(CHUNK)) - 1):
        ps = [_mm(p, _stack_heads(p.astype(BF16), same)) for p in ps]
        rs = [r + _mm(r, _stack_heads(p.astype(BF16), same)) for r, p in zip(rs, ps)]
    return rs


def _unstack_heads(x):
    return x[0:CHUNK] + x[CHUNK:2 * CHUNK] + x[2 * CHUNK:3 * CHUNK] + x[3 * CHUNK:4 * CHUNK]


def _inproj_kernel(x_ref, g_ref, w_ref, ws_ref, wvt_ref, u_ref, scol_ref, vt_ref, h_scr):
    @pl.when(pl.program_id(1) == 0)
    def _():
        hb = _rms(x_ref[...], g_ref[...], NORM_EPS).astype(BF16)
        h_scr[...] = hb
        scol_ref[...] = jnp.dot(hb, ws_ref[...], preferred_element_type=F32)
        vt_ref[...] = lax.dot_general(wvt_ref[...], hb, (((1,), (1,)), ((), ())),
                                      preferred_element_type=F32).astype(BF16)

    u_ref[...] = jnp.dot(h_scr[...], w_ref[...], preferred_element_type=F32).astype(BF16)


def _inproj(x, g, w, ws, wvt, tm, tn):
    n = x.shape[0]
    return pl.pallas_call(
        _inproj_kernel,
        grid=(n // tm, U_COLS // tn),
        in_specs=[
            pl.BlockSpec((tm, D_MODEL), lambda i, j: (i, 0)),
            pl.BlockSpec((1, D_MODEL), lambda i, j: (0, 0)),
            pl.BlockSpec((D_MODEL, tn), lambda i, j: (0, j)),
            pl.BlockSpec((D_MODEL, LANES), lambda i, j: (0, 0)),
            pl.BlockSpec((2 * W_MIX, D_MODEL), lambda i, j: (0, 0)),
        ],
        out_specs=[
            pl.BlockSpec((tm, tn), lambda i, j: (i, j)),
            pl.BlockSpec((tm, LANES), lambda i, j: (i, 0)),
            pl.BlockSpec((2 * W_MIX, tm), lambda i, j: (0, i)),
        ],
        out_shape=[
            jax.ShapeDtypeStruct((n, U_COLS), BF16),
            jax.ShapeDtypeStruct((n, LANES), F32),
            jax.ShapeDtypeStruct((2 * W_MIX, n), BF16),
        ],
        scratch_shapes=[pltpu.VMEM((tm, D_MODEL), BF16)],
        compiler_params=_cparams(("parallel", "arbitrary"), vmem_mb=56),
        name="inproj",
    )(x, g, w, ws, wvt)


def _lane_placement(base):
    r = lax.broadcasted_iota(jnp.int32, (3 * LANES, LANES), 0)
    m = lax.broadcasted_iota(jnp.int32, (3 * LANES, LANES), 1)
    head, part = r % LANES, r // LANES
    return jnp.logical_and(head < N_HEADS, m == base + 3 * head + part).astype(BF16)


def _split3(x):
    hi = x.astype(BF16)
    r1 = x - hi.astype(F32)
    mid = r1.astype(BF16)
    lo = (r1 - mid.astype(F32)).astype(BF16)
    return jnp.concatenate([hi, mid, lo], axis=1)


def _head_expansion():
    r = lax.broadcasted_iota(jnp.int32, (3 * LANES, 2 * W_MIX), 0) % LANES
    m = lax.broadcasted_iota(jnp.int32, (3 * LANES, 2 * W_MIX), 1)
    src = jnp.where(m < W_MIX, SM_BETA + m // HEAD_DIM, SM_DEC + (m - W_MIX) // HEAD_DIM)
    return (r == src).astype(BF16)


def _small_prep_kernel(scol_ref, bias_ref, nega_ref, ogb_ref, okb_ref, oqb_ref, carry_scr):
    @pl.when(pl.program_id(1) == 0)
    def _():
        carry_scr[...] = jnp.zeros_like(carry_scr)

    tm = scol_ref.shape[0]
    lower = _tri_masks(tm)[1].astype(BF16)
    z = scol_ref[...] + bias_ref[...]
    lane = lax.broadcasted_iota(jnp.int32, z.shape, 1)
    is_f = lane < SM_BETA
    is_b = jnp.logical_and(lane >= SM_BETA, lane < SM_DEC)
    is_d = jnp.logical_and(lane >= SM_DEC, lane < SM_DEC + N_HEADS)
    logf = jnp.where(is_f, jnp.minimum(z, 0.0) - jnp.log(1.0 + jnp.exp(-jnp.abs(z))), 0.0)
    gdn = jnp.where(is_b, _sigmoid(z), jnp.where(is_d, nega_ref[...] * _softplus(z), 0.0))
    ogb_ref[...] = jnp.dot(_split3(gdn), _head_expansion(), preferred_element_type=F32)
    cum = _mm_mask(lower, logf) + carry_scr[...]
    carry_scr[...] = cum[tm - 1:tm, :]
    parts = _split3(cum * LOG2E)
    ones_lanes = jnp.logical_and(lane >= FOX_CQ, lane < FOX_CQ + 3 * N_HEADS)
    okb_ref[...] = (jnp.dot(parts, _lane_placement(FOX_CK), preferred_element_type=F32)
                    + jnp.where(ones_lanes, 1.0, 0.0)).astype(BF16)
    oqb_ref[...] = jnp.dot(parts, _lane_placement(FOX_CQ), preferred_element_type=F32).astype(BF16)


def _small_prep(scol, bias, neg_a, batch, seq, tm):
    n = batch * seq
    nt = seq // tm
    tok = lambda: pl.BlockSpec((tm, LANES), lambda b, j: (b * nt + j, 0))
    const = lambda: pl.BlockSpec((1, LANES), lambda b, j: (0, 0))
    return pl.pallas_call(
        _small_prep_kernel,
        grid=(batch, nt),
        in_specs=[tok(), const(), const()],
        out_specs=[pl.BlockSpec((tm, 2 * W_MIX), lambda b, j: (b * nt + j, 0)), tok(), tok()],
        out_shape=[jax.ShapeDtypeStruct((n, 2 * W_MIX), F32), jax.ShapeDtypeStruct((n, LANES), BF16),
                   jax.ShapeDtypeStruct((n, LANES), BF16)],
        scratch_shapes=[pltpu.VMEM((1, LANES), F32)],
        compiler_params=_cparams(("parallel", "arbitrary")),
        name="small_prep",
    )(scol, bias, neg_a)


def _rwkv_prep_kernel(u_ref, up_ref, mu_ref, w0_ref, w2_ref, a0_ref, a2_ref, g2_ref, kk_ref, ka_ref, rk_ref,
                      scan_ref, pc_ref, post_ref, *, tiles_per_seq):
    tm = u_ref.shape[0]
    u = u_ref[...].astype(F32)
    prev = up_ref[...].astype(F32)[BF16_ROWS - 1:BF16_ROWS, :]
    prev = jnp.where(pl.program_id(0) % tiles_per_seq == 0, 0.0, prev)
    rows = lax.broadcasted_iota(jnp.int32, (tm, 1), 0)
    u_prev = jnp.where(rows == 0, prev, pltpu.roll(u, 1, 0))
    xm = u + (u_prev - u) * mu_ref[...]
    r = xm[:, 0:W_MIX]
    k = xm[:, W_MIX:2 * W_MIX]
    v = xm[:, 2 * W_MIX:3 * W_MIX]
    x_lora = xm[:, 3 * W_MIX:3 * W_MIX + LANES]
    xg = xm[:, 3 * W_MIX + LANES:]
    logw = -_softplus(-(w0_ref[...] + _mm(jnp.tanh(x_lora), w2_ref[...]))) - 0.5
    log_decay = -jnp.exp(logw)
    a = _sigmoid(a0_ref[...] + _mm(x_lora, a2_ref[...]))
    g = _mm(_sigmoid(xg), g2_ref[...])
    same = _block_masks()[0].astype(F32)
    kk_raw = k * kk_ref[...]
    kk = kk_raw * lax.rsqrt(_mm_f32(kk_raw * kk_raw, same) + L2_EPS)
    k2 = k * (1.0 + (a - 1.0) * ka_ref[...])
    bonus = _mm_f32(r * k2 * rk_ref[...], same) * v
    ti = lax.broadcasted_iota(jnp.int32, (tm, tm), 0)
    tj = lax.broadcasted_iota(jnp.int32, (tm, tm), 1)
    in_chunk = jnp.logical_and(ti // CHUNK == tj // CHUNK, ti >= tj).astype(BF16)
    ci = lax.broadcasted_iota(jnp.int32, (tm // CHUNK, tm), 0)
    cj = lax.broadcasted_iota(jnp.int32, (tm // CHUNK, tm), 1)
    cum = _mm_mask(in_chunk, log_decay)
    cum_end = _mm_mask((ci == cj // CHUNK).astype(BF16), log_decay)
    inv = jnp.exp(-cum)
    scan_ref[0] = (-kk * jnp.exp(cum - log_decay)).astype(BF16)
    scan_ref[1] = (kk * a * inv).astype(BF16)
    scan_ref[2] = (k2 * inv).astype(BF16)
    scan_ref[3] = (r * jnp.exp(cum)).astype(BF16)
    scan_ref[4] = v.astype(BF16)
    pc_ref[...] = jnp.exp(cum_end)
    post_ref[0] = g
    post_ref[1] = bonus


def _rwkv_prep(u, mu, w0, w2p, a0, a2p, g2, k_k, k_a, r_k, seq, tm):
    n = u.shape[0]
    ublk = U_RWKV // D_MODEL
    row = lambda c: pl.BlockSpec((1, c), lambda i: (0, 0))
    mat = lambda r: pl.BlockSpec((r, W_MIX), lambda i: (0, 0))
    return pl.pallas_call(
        functools.partial(_rwkv_prep_kernel, tiles_per_seq=seq // tm),
        grid=(n // tm,),
        in_specs=[
            pl.BlockSpec((tm, D_MODEL), lambda i: (i, ublk)),
            pl.BlockSpec((BF16_ROWS, D_MODEL), lambda i: (jnp.maximum(i * (tm // BF16_ROWS) - 1, 0), ublk)),
            row(D_MODEL), row(W_MIX), mat(LANES), row(W_MIX), mat(LANES), mat(LANES), row(W_MIX), row(W_MIX), row(W_MIX),
        ],
        out_specs=[
            pl.BlockSpec((5, tm, W_MIX), lambda i: (0, i, 0)),
            pl.BlockSpec((tm // CHUNK, W_MIX), lambda i: (i, 0)),
            pl.BlockSpec((2, tm, W_MIX), lambda i: (0, i, 0)),
        ],
        out_shape=[jax.ShapeDtypeStruct((5, n, W_MIX), BF16), jax.ShapeDtypeStruct((n // CHUNK, W_MIX), F32),
                   jax.ShapeDtypeStruct((2, n, W_MIX), F32)],
        compiler_params=_cparams(("parallel",)),
        name="rwkv_prep",
    )(u, u, mu, w0, w2p, a0, a2p, g2, k_k, k_a, r_k)


def _rwkv_chunk_kernel(x_ref, pc_ref, o_ref, s_scr, *, nchunk, nbatch):
    @pl.when(pl.program_id(0) == 0)
    def _():
        s_scr[...] = jnp.zeros_like(s_scr)

    same, diag = _block_masks()
    eye = diag.astype(F32)
    strict_w, incl_w, diag_w = _wide_masks()
    eye_w = diag_w.astype(F32)
    stack = lambda x: _stack_heads(x.astype(BF16), same)

    def chunks(i, carry):
        items = [(b, i * CHUNK_UNROLL + j) for j in range(CHUNK_UNROLL) for b in range(nbatch)]
        each = lambda f, *lists: [f(*args) for args in zip(*lists)]
        sls = [pl.ds(pl.multiple_of(ci * CHUNK, CHUNK), CHUNK) for _, ci in items]
        a, bb, k, r, v = ([x_ref[i, b, sl, :] for (b, _), sl in zip(items, sls)] for i in range(5))
        pc = [pc_ref[b, pl.ds(ci, 1), :] for b, ci in items]
        a_s, b_s, k_s, v_s = (each(stack, x) for x in (a, bb, k, v))
        m_ab = each(lambda x, y: jnp.where(strict_w, _mm_nt(x, y), 0.0), a, b_s)
        m_ak = each(lambda x, y: jnp.where(strict_w, _mm_nt(x, y), 0.0), a, k_s)
        n_rb = each(lambda x, y: jnp.where(incl_w, _mm_nt(x, y), 0.0), r, b_s)
        n_rk = each(lambda x, y: jnp.where(incl_w, _mm_nt(x, y), 0.0), r, k_s)
        t_inv = _unit_lower_inverses(m_ab, eye_w, same)
        a2 = each(_mm, t_inv, a_s)
        u0 = each(lambda t, m, x: _mm(t, stack(_mm(m, x))), t_inv, m_ak, v_s)
        r2 = each(lambda x, n, y: x.astype(F32) + _mm(n, stack(y)), r, n_rb, a2)
        o0 = each(lambda n1, u, n2, x: _mm(n1, stack(u)) + _mm(n2, x), n_rb, u0, n_rk, v_s)
        b_end = each(lambda x, p: x.astype(F32) * p, bb, pc)
        k_end = each(lambda x, p: x.astype(F32) * p, k, pc)
        g_mat = each(lambda p, x, y: eye * p + jnp.where(same, _mm_tn(x, y), 0.0), pc, a2, b_end)
        s0 = each(lambda u, x, y, z: jnp.where(same, _mm_tn(u, x) + _mm_tn(y, z), 0.0), u0, b_end, v, k_end)
        for n, ((b, _), sl) in enumerate(zip(items, sls)):
            s = s_scr[b]
            o = _stack_heads(_mm_nt(r2[n], s) + o0[n], same)
            s_scr[b] = _mm(s, g_mat[n]) + s0[n]
            mean = jnp.sum(o, axis=-1, keepdims=True) * (1.0 / HEAD_DIM)
            cen = jnp.where(same, o - mean, 0.0)
            var = jnp.sum(cen * cen, axis=-1, keepdims=True) * (1.0 / HEAD_DIM)
            o_ref[b, sl, :] = _unstack_heads(cen * lax.rsqrt(var + RWKV_GN_EPS))
        return carry

    lax.fori_loop(0, nchunk // CHUNK_UNROLL, chunks, 0)


def _rwkv_chunk(xs, pc, batch, seq, cblk):
    xs = xs.reshape(5, batch, seq, W_MIX)
    pc = pc.reshape(batch, seq // CHUNK, W_MIX)
    out = pl.pallas_call(
        functools.partial(_rwkv_chunk_kernel, nchunk=cblk // CHUNK, nbatch=batch),
        grid=(seq // cblk,),
        in_specs=[
            pl.BlockSpec((5, batch, cblk, W_MIX), lambda j: (0, 0, j, 0)),
            pl.BlockSpec((batch, cblk // CHUNK, W_MIX), lambda j: (0, j, 0)),
        ],
        out_specs=pl.BlockSpec((batch, cblk, W_MIX), lambda j: (0, j, 0)),
        out_shape=jax.ShapeDtypeStruct((batch, seq, W_MIX), F32),
        scratch_shapes=[pltpu.VMEM((batch, W_MIX, W_MIX), F32)],
        compiler_params=_cparams(("arbitrary",)),
        name="rwkv_chunk",
    )(xs, pc)
    return out.reshape(batch * seq, W_MIX)


def _gdn_prep_kernel(u_ref, up_ref, cw_ref, o_ref, ext_scr, *, tiles_per_seq):
    tm = u_ref.shape[0]
    c3 = 3 * W_MIX
    prev = up_ref[...].astype(F32)[:, :c3]
    ext_scr[0:BF16_ROWS, :] = jnp.where(pl.program_id(0) % tiles_per_seq == 0, 0.0, prev)
    ext_scr[BF16_ROWS:, :] = u_ref[...].astype(F32)[:, :c3]
    y = jnp.zeros((tm, c3), F32)
    for j in range(GDN_CONV):
        y = y + ext_scr[pl.ds(BF16_ROWS - (GDN_CONV - 1) + j, tm), :] * cw_ref[j:j + 1, :]
    y = _silu(y)
    same = _block_masks()[0].astype(F32)
    q = y[:, 0:W_MIX]
    k = y[:, W_MIX:2 * W_MIX]
    o_ref[0] = (q * lax.rsqrt(_mm_f32(q * q, same) + L2_EPS) * (HEAD_DIM ** -0.5)).astype(BF16)
    o_ref[1] = (k * lax.rsqrt(_mm_f32(k * k, same) + L2_EPS)).astype(BF16)
    o_ref[2] = y[:, 2 * W_MIX:].astype(BF16)


def _gdn_prep(u, conv_w, seq, tm):
    n = u.shape[0]
    ublk = U_GDN // D_MODEL
    return pl.pallas_call(
        functools.partial(_gdn_prep_kernel, tiles_per_seq=seq // tm),
        grid=(n // tm,),
        in_specs=[
            pl.BlockSpec((tm, D_MODEL), lambda i: (i, ublk)),
            pl.BlockSpec((BF16_ROWS, D_MODEL), lambda i: (jnp.maximum(i * (tm // BF16_ROWS) - 1, 0), ublk)),
            pl.BlockSpec((GDN_CONV, 3 * W_MIX), lambda i: (0, 0)),
        ],
        out_specs=pl.BlockSpec((3, tm, W_MIX), lambda i: (0, i, 0)),
        out_shape=jax.ShapeDtypeStruct((3, n, W_MIX), BF16),
        scratch_shapes=[pltpu.VMEM((tm + BF16_ROWS, 3 * W_MIX), F32)],
        compiler_params=_cparams(("parallel",)),
        name="gdn_prep",
    )(u, u, conv_w)


def _gdn_chunk_kernel(x_ref, gb_ref, nw_ref, o_ref, s_scr, *, nchunk, nbatch):
    @pl.when(pl.program_id(0) == 0)
    def _():
        s_scr[...] = jnp.zeros_like(s_scr)

    same, diag = _block_masks()
    eye = diag.astype(F32)
    strict_w, incl_w, diag_w = _wide_masks()
    eye_w = diag_w.astype(F32)
    lower = _tri_masks(CHUNK)[1].astype(BF16)
    stack = lambda x: _stack_heads(x.astype(BF16), same)

    def chunks(i, carry):
        items = [(b, i * CHUNK_UNROLL + j) for j in range(CHUNK_UNROLL) for b in range(nbatch)]
        each = lambda f, *lists: [f(*args) for args in zip(*lists)]
        sls = [pl.ds(pl.multiple_of(ci * CHUNK, CHUNK), CHUNK) for _, ci in items]
        q, k, v = ([x_ref[i, b, sl, :] for (b, _), sl in zip(items, sls)] for i in range(3))
        beta = [gb_ref[b, sl, 0:W_MIX] for (b, _), sl in zip(items, sls)]
        g = [gb_ref[b, sl, W_MIX:2 * W_MIX] for (b, _), sl in zip(items, sls)]
        k_s = each(stack, k)
        gam = [_mm_mask(lower, x) for x in g]
        gam_end = [x[CHUNK - 1:CHUNK, :] for x in gam]
        gdiff = [_mm_mask(lower, jnp.where(strict_w, x, 0.0)) for x in g]
        decay = [jnp.exp(jnp.where(incl_w, x, -jnp.inf)) for x in gdiff]
        a_mat = each(lambda bt, d, x, y: jnp.where(strict_w, bt * d * _mm_nt(x, y), 0.0), beta, decay, k, k_s)
        t_inv = _unit_lower_inverses([-a for a in a_mat], eye_w, same)
        e_gam = [jnp.exp(x) for x in gam]
        u0 = each(lambda t, bt, x: _mm(t, stack(bt * x.astype(F32))), t_inv, beta, v)
        wm = each(lambda t, bt, e, x: _mm(t, stack((bt * e) * x.astype(F32))), t_inv, beta, e_gam, k)
        qk = each(lambda x, y, d: _mm_nt(x, y) * d, q, k_s, decay)
        q2 = each(lambda e, x, a, w: e * x.astype(F32) - _mm(a, stack(w)), e_gam, q, qk, wm)
        o0 = each(lambda a, u: _mm(a, stack(u)), qk, u0)
        k_end = each(lambda x, ge, ga: x.astype(F32) * jnp.exp(ge - ga), k, gam_end, gam)
        g_mat = each(lambda ge, x, w: eye * jnp.exp(ge) - jnp.where(same, _mm_tn(x, w), 0.0), gam_end, k_end, wm)
        s0 = each(lambda x, u: jnp.where(same, _mm_tn(x, u), 0.0), k_end, u0)
        for n, ((b, _), sl) in enumerate(zip(items, sls)):
            s = s_scr[b]
            o = _stack_heads(_mm(q2[n], s) + o0[n], same)
            s_scr[b] = _mm(g_mat[n], s) + s0[n]
            ms = jnp.sum(o * o, axis=-1, keepdims=True) * (1.0 / HEAD_DIM)
            o_ref[b, sl, :] = _unstack_heads(o * lax.rsqrt(ms + NORM_EPS)) * nw_ref[...]
        return carry

    lax.fori_loop(0, nchunk // CHUNK_UNROLL, chunks, 0)


def _gdn_chunk(xs, gb, norm_w, batch, seq, cblk):
    xs = xs.reshape(3, batch, seq, W_MIX)
    gb = gb.reshape(batch, seq, 2 * W_MIX)
    out = pl.pallas_call(
        functools.partial(_gdn_chunk_kernel, nchunk=cblk // CHUNK, nbatch=batch),
        grid=(seq // cblk,),
        in_specs=[
            pl.BlockSpec((3, batch, cblk, W_MIX), lambda j: (0, 0, j, 0)),
            pl.BlockSpec((batch, cblk, 2 * W_MIX), lambda j: (0, j, 0)),
            pl.BlockSpec((1, W_MIX), lambda j: (0, 0)),
        ],
        out_specs=pl.BlockSpec((batch, cblk, W_MIX), lambda j: (0, j, 0)),
        out_shape=jax.ShapeDtypeStruct((batch, seq, W_MIX), F32),
        scratch_shapes=[pltpu.VMEM((batch, W_MIX, W_MIX), F32)],
        compiler_params=_cparams(("arbitrary",)),
        name="gdn_chunk",
    )(xs, gb, norm_w)
    return out.reshape(batch * seq, W_MIX)


def _causal_pairs(nq):
    pairs = [(i, j) for i in range(nq) for j in range(i + 1)]
    return jnp.asarray(np.array([p[0] for p in pairs], np.int32)), jnp.asarray(np.array([p[1] for p in pairs], np.int32))


def _softmax_updates(scores, vt_ones, m_scr, acc_scr):
    probs = []
    for i, s in enumerate(scores):
        m_old = m_scr[i]
        m_new = jnp.maximum(m_old, jnp.max(s, axis=0, keepdims=True))
        m_scr[i] = m_new
        probs.append((jnp.exp2(m_old - m_new), jnp.exp2((s - m_new).astype(BF16))))
    for i, (alpha, p) in enumerate(probs):
        acc_scr[i] = alpha * acc_scr[i] + jnp.dot(vt_ones[i], p, preferred_element_type=F32)


def _pair_lanes(h):
    p = h // 2
    return slice(p * LANES, (p + 1) * LANES), h % 2 == 0


def _values_and_ones(vt, tk):
    row = lax.broadcasted_iota(jnp.int32, (LANES, tk), 0)
    out = []
    for h in range(N_HEADS):
        slab, low = _pair_lanes(h)
        mine = row < HEAD_DIM if low else row >= HEAD_DIM
        out.append(jnp.where(mine, vt[slab, :], jnp.ones((), vt.dtype)))
    return out


def _normalized(acc, low):
    if low:
        return acc[0:HEAD_DIM] / acc[HEAD_DIM:HEAD_DIM + 1]
    return acc[HEAD_DIM:] / acc[0:1]


def _key_after_query(tq):
    return lax.broadcasted_iota(jnp.int32, (tq, tq), 0) > lax.broadcasted_iota(jnp.int32, (tq, tq), 1)


def _fox_kernel(qi_ref, kj_ref, q_ref, k_ref, vt_ref, qb_ref, kb_ref, o_ref, qm_scr, m_scr, acc_scr):
    t = pl.program_id(1)
    qi = qi_ref[t]
    kj = kj_ref[t]
    tq = q_ref.shape[0]

    @pl.when(kj == 0)
    def _():
        m_scr[...] = jnp.full_like(m_scr, -jnp.inf)
        acc_scr[...] = jnp.zeros_like(acc_scr)
        q = q_ref[...]
        qb = qb_ref[...]
        lane = lax.broadcasted_iota(jnp.int32, (tq, LANES), 1)
        zero = jnp.zeros((), BF16)
        for h in range(N_HEADS):
            slab, low = _pair_lanes(h)
            mine = lane < HEAD_DIM if low else lane >= HEAD_DIM
            ck_lanes = jnp.logical_and(lane >= FOX_CK + 3 * h, lane < FOX_CK + 3 * h + 3)
            cq_lanes = jnp.logical_and(lane >= FOX_CQ + 3 * h, lane < FOX_CQ + 3 * h + 3)
            bias = jnp.where(ck_lanes, -jnp.ones((), BF16), jnp.where(cq_lanes, qb, zero))
            qm_scr[h] = jnp.concatenate([jnp.where(mine, q[:, slab], zero), bias], axis=1)

    def step(diagonal):
        k = k_ref[...]
        kb = kb_ref[...]
        vts = _values_and_ones(vt_ref[...], tq)
        if diagonal:
            masked = _key_after_query(tq)
        keys = [jnp.concatenate([k[:, p * LANES:(p + 1) * LANES], kb], axis=1) for p in range(N_HEADS // 2)]
        scores = []
        for h in range(N_HEADS):
            s = lax.dot_general(keys[h // 2], qm_scr[h], (((1,), (1,)), ((), ())), preferred_element_type=F32)
            if diagonal:
                s = jnp.where(masked, -jnp.inf, s)
            scores.append(s)
        _softmax_updates(scores, vts, m_scr, acc_scr)

    @pl.when(kj < qi)
    def _():
        step(False)

    @pl.when(kj == qi)
    def _():
        step(True)
        for p in range(N_HEADS // 2):
            pair = jnp.concatenate([_normalized(acc_scr[2 * p], True), _normalized(acc_scr[2 * p + 1], False)], axis=0)
            o_ref[:, p * LANES:(p + 1) * LANES] = pair.T


def _attn_specs(nq, tq, ucol, vt_rows):
    cb = ucol // W_MIX
    q_spec = pl.BlockSpec((tq, W_MIX), lambda b, t, qi, kj: (b * nq + qi[t], cb))
    k_spec = pl.BlockSpec((tq, W_MIX), lambda b, t, qi, kj: (b * nq + kj[t], cb + 1))
    vt_spec = pl.BlockSpec((W_MIX, tq), lambda b, t, qi, kj: (vt_rows // W_MIX, b * nq + kj[t]))
    return q_spec, k_spec, vt_spec


def _fox_attention(u, vt, qb, kb, batch, seq, tq):
    nq = seq // tq
    qi, kj = _causal_pairs(nq)
    q_spec, k_spec, vt_spec = _attn_specs(nq, tq, U_FOX, W_MIX)
    stat = lambda: pltpu.VMEM((N_HEADS, 1, tq), F32)
    return pl.pallas_call(
        _fox_kernel,
        grid_spec=pltpu.PrefetchScalarGridSpec(
            num_scalar_prefetch=2,
            grid=(batch, qi.shape[0]),
            in_specs=[
                q_spec, k_spec, vt_spec,
                pl.BlockSpec((tq, LANES), lambda b, t, qi, kj: (b * nq + qi[t], 0)),
                pl.BlockSpec((tq, LANES), lambda b, t, qi, kj: (b * nq + kj[t], 0)),
            ],
            out_specs=pl.BlockSpec((tq, W_MIX), lambda b, t, qi, kj: (b * nq + qi[t], 0)),
            scratch_shapes=[pltpu.VMEM((N_HEADS, tq, 2 * LANES), BF16), stat(),
                            pltpu.VMEM((N_HEADS, LANES, tq), F32)],
        ),
        out_shape=jax.ShapeDtypeStruct((batch * seq, W_MIX), F32),
        compiler_params=_cparams(("parallel", "arbitrary")),
        name="fox_attention",
    )(qi, kj, u, u, vt, qb, kb)


def _diff_kernel(qi_ref, kj_ref, q_ref, k_ref, vt_ref, cq_ref, sq_ref, ck_ref, sk_ref, rot_ref, lam_ref, ln_ref, o_ref,
                 qm_scr, m_scr, acc_scr, *, lam_init):
    t = pl.program_id(1)
    qi = qi_ref[t]
    kj = kj_ref[t]
    tq = q_ref.shape[0]
    lane = lax.broadcasted_iota(jnp.int32, (tq, LANES), 1)

    def rope(x, cos, sin):
        return x.astype(F32) * cos + jnp.dot(x, rot_ref[...], preferred_element_type=F32) * sin

    @pl.when(kj == 0)
    def _():
        m_scr[...] = jnp.full_like(m_scr, -jnp.inf)
        acc_scr[...] = jnp.zeros_like(acc_scr)
        q = rope(q_ref[...], cq_ref[...], sq_ref[...])
        for h in range(N_HEADS):
            slab, low = _pair_lanes(h)
            base = 0 if low else HEAD_DIM
            for c in range(2):
                lo = base + c * DIFF_DH
                sel = jnp.logical_and(lane >= lo, lane < lo + DIFF_DH)
                qm_scr[2 * h + c] = jnp.where(sel, q[:, slab], 0.0).astype(BF16)

    def step(diagonal):
        k = rope(k_ref[...], ck_ref[...], sk_ref[...]).astype(BF16)
        vts = _values_and_ones(vt_ref[...], tq)
        if diagonal:
            masked = _key_after_query(tq)
        scores = []
        for h in range(N_HEADS):
            slab, _ = _pair_lanes(h)
            for c in range(2):
                i = 2 * h + c
                s = lax.dot_general(k[:, slab], qm_scr[i], (((1,), (1,)), ((), ())), preferred_element_type=F32)
                if diagonal:
                    s = jnp.where(masked, -jnp.inf, s)
                scores.append(s)
        _softmax_updates(scores, [vts[i // 2] for i in range(2 * N_HEADS)], m_scr, acc_scr)

    @pl.when(kj < qi)
    def _():
        step(False)

    @pl.when(kj == qi)
    def _():
        step(True)
        lp = lam_ref[...]
        lam = (jnp.exp(jnp.sum(lp[0:1] * lp[1:2], axis=-1, keepdims=True))
               - jnp.exp(jnp.sum(lp[2:3] * lp[3:4], axis=-1, keepdims=True)) + lam_init)
        head = lambda h: (_normalized(acc_scr[2 * h], h % 2 == 0) - lam * _normalized(acc_scr[2 * h + 1], h % 2 == 0))
        is_lo = lane < HEAD_DIM
        for p in range(N_HEADS // 2):
            o = jnp.concatenate([head(2 * p), head(2 * p + 1)], axis=0).T
            sq = o * o
            ms_lo = jnp.sum(jnp.where(is_lo, sq, 0.0), axis=-1, keepdims=True)
            ms_hi = jnp.sum(jnp.where(is_lo, 0.0, sq), axis=-1, keepdims=True)
            ms = jnp.where(is_lo, ms_lo, ms_hi) * (1.0 / HEAD_DIM)
            o_ref[:, p * LANES:(p + 1) * LANES] = o * lax.rsqrt(ms + DIFF_LN_EPS) * ln_ref[...] * (1.0 - lam_init)


def _diff_attention(u, vt, cos, sin, rot, lam_p, subln, batch, seq, tq, lam_init):
    nq = seq // tq
    qi, kj = _causal_pairs(nq)
    q_spec, k_spec, vt_spec = _attn_specs(nq, tq, U_DIFF, 0)
    tab_q = pl.BlockSpec((tq, W_MIX), lambda b, t, qi, kj: (qi[t], 0))
    tab_k = pl.BlockSpec((tq, W_MIX), lambda b, t, qi, kj: (kj[t], 0))
    const = lambda r, c: pl.BlockSpec((r, c), lambda b, t, qi, kj: (0, 0))
    stat = lambda: pltpu.VMEM((2 * N_HEADS, 1, tq), F32)
    return pl.pallas_call(
        functools.partial(_diff_kernel, lam_init=lam_init),
        grid_spec=pltpu.PrefetchScalarGridSpec(
            num_scalar_prefetch=2,
            grid=(batch, qi.shape[0]),
            in_specs=[q_spec, k_spec, vt_spec, tab_q, tab_q, tab_k, tab_k,
                      const(W_MIX, W_MIX), const(4, DIFF_DH), const(1, LANES)],
            out_specs=pl.BlockSpec((tq, W_MIX), lambda b, t, qi, kj: (b * nq + qi[t], 0)),
            scratch_shapes=[pltpu.VMEM((2 * N_HEADS, tq, LANES), BF16), stat(),
                            pltpu.VMEM((2 * N_HEADS, LANES, tq), F32)],
        ),
        out_shape=jax.ShapeDtypeStruct((batch * seq, W_MIX), F32),
        compiler_params=_cparams(("parallel", "arbitrary")),
        name="diff_attention",
    )(qi, kj, u, u, vt, cos, sin, cos, sin, rot, lam_p, subln)


def _merge_kernel(x_ref, gate_ref, oa_ref, post_ref, yb_ref, yc_ref, od_ref, gd_ref, lnw_ref, lnb_ref,
                  wbo_ref, wout_ref, o_ref):
    y_a = (oa_ref[...] * lnw_ref[...] + lnb_ref[...] + post_ref[1]) * post_ref[0]
    y_d = od_ref[...] * _silu(gd_ref[...].astype(F32))
    acc = jnp.zeros(x_ref.shape, F32)
    for b, y in enumerate((y_a, yb_ref[...], yc_ref[...], y_d)):
        gate = _sigmoid(gate_ref[:, b * D_MODEL:(b + 1) * D_MODEL].astype(F32))
        acc = acc + gate * _mm(y, wbo_ref[b])
    o_ref[...] = x_ref[...] + _mm(acc, wout_ref[...])


def _merge(x, u, o_a, post, y_b, y_c, o_d, ln_w, ln_b, w_bo, w_out, tm):
    n = x.shape[0]
    tok = lambda c: pl.BlockSpec((tm, c), lambda i: (i, 0))
    return pl.pallas_call(
        _merge_kernel,
        grid=(n // tm,),
        in_specs=[
            tok(D_MODEL),
            pl.BlockSpec((tm, 4 * D_MODEL), lambda i: (i, U_GATE // (4 * D_MODEL))),
            tok(W_MIX),
            pl.BlockSpec((2, tm, W_MIX), lambda i: (0, i, 0)),
            tok(W_MIX), tok(W_MIX), tok(W_MIX),
            pl.BlockSpec((tm, W_MIX), lambda i: (i, (U_GDN + 3 * W_MIX) // W_MIX)),
            pl.BlockSpec((1, W_MIX), lambda i: (0, 0)),
            pl.BlockSpec((1, W_MIX), lambda i: (0, 0)),
            pl.BlockSpec((4, W_MIX, D_MODEL), lambda i: (0, 0, 0)),
            pl.BlockSpec((D_MODEL, D_MODEL), lambda i: (0, 0)),
        ],
        out_specs=tok(D_MODEL),
        out_shape=jax.ShapeDtypeStruct((n, D_MODEL), F32),
        compiler_params=_cparams(("parallel",)),
        name="merge",
    )(x, u, o_a, post, y_b, y_c, o_d, u, ln_w, ln_b, w_bo, w_out)


def _ffn_kernel(x_ref, g_ref, wg_ref, wu_ref, wd_ref, o_ref, h_scr, acc_scr):
    f = pl.program_id(1)

    @pl.when(f == 0)
    def _():
        h_scr[...] = _rms(x_ref[...], g_ref[...], NORM_EPS).astype(BF16)
        acc_scr[...] = jnp.zeros_like(acc_scr)

    h = h_scr[...]
    act = _silu(jnp.dot(h, wg_ref[...], preferred_element_type=F32)) * jnp.dot(h, wu_ref[...], preferred_element_type=F32)
    acc_scr[...] += _mm(act, wd_ref[...])

    @pl.when(f == pl.num_programs(1) - 1)
    def _():
        o_ref[...] = acc_scr[...]


def _ffn(x, g, wg, wu, wd, tm, tf):
    n = x.shape[0]
    ff = wg.shape[1]
    return pl.pallas_call(
        _ffn_kernel,
        grid=(n // tm, ff // tf),
        in_specs=[
            pl.BlockSpec((tm, D_MODEL), lambda i, f: (i, 0)),
            pl.BlockSpec((1, D_MODEL), lambda i, f: (0, 0)),
            pl.BlockSpec((D_MODEL, tf), lambda i, f: (0, f)),
            pl.BlockSpec((D_MODEL, tf), lambda i, f: (0, f)),
            pl.BlockSpec((tf, D_MODEL), lambda i, f: (f, 0)),
        ],
        out_specs=pl.BlockSpec((tm, D_MODEL), lambda i, f: (i, 0)),
        out_shape=jax.ShapeDtypeStruct((n, D_MODEL), F32),
        scratch_shapes=[pltpu.VMEM((tm, D_MODEL), BF16), pltpu.VMEM((tm, D_MODEL), F32)],
        compiler_params=_cparams(("parallel", "arbitrary")),
        name="ffn",
    )(x, g, wg, wu, wd)


def _router_kernel(x_ref, g_ref, router_ref, h_ref, c_ref, rc_ref, rr_ref, cnt_ref):
    t = x_ref.shape[0]
    h = _rms(x_ref[...], g_ref[...], NORM_EPS)
    h_ref[...] = h.astype(BF16)
    logits = _mm_f32(h, router_ref[...])
    lane = lax.broadcasted_iota(jnp.int32, logits.shape, 1).astype(F32)
    lg = jnp.where(lane < N_EXPERTS, logits, -jnp.inf)
    m1 = jnp.max(lg, axis=-1, keepdims=True)
    i1 = jnp.min(jnp.where(lg == m1, lane, float(LANES)), axis=-1, keepdims=True)
    lg2 = jnp.where(lane == i1, -jnp.inf, lg)
    m2 = jnp.max(lg2, axis=-1, keepdims=True)
    i2 = jnp.min(jnp.where(lg2 == m2, lane, float(LANES)), axis=-1, keepdims=True)
    e2 = jnp.exp(m2 - m1)
    c_ref[...] = jnp.where(lane == i1, 1.0 / (1.0 + e2), 0.0) + jnp.where(lane == i2, e2 / (1.0 + e2), 0.0)
    sel = jnp.logical_or(lane == i1, lane == i2)
    sel_f = jnp.where(sel, 1.0, 0.0)
    earlier = _tri_masks(t)[0].astype(BF16)
    rank = jnp.dot(earlier, sel_f.astype(BF16), preferred_element_type=F32)
    rc = jnp.where(sel, rank, -1.0)
    rc_ref[...] = rc
    rr_ref[...] = rc.T[0:BF16_ROWS, :]
    cnt_ref[...] = jnp.broadcast_to(jnp.sum(sel_f, axis=0, keepdims=True), cnt_ref.shape).astype(jnp.int32)


def _router(x, g, router, tm):
    n = x.shape[0]
    return pl.pallas_call(
        _router_kernel,
        grid=(n // tm,),
        in_specs=[
            pl.BlockSpec((tm, D_MODEL), lambda i: (i, 0)),
            pl.BlockSpec((1, D_MODEL), lambda i: (0, 0)),
            pl.BlockSpec((D_MODEL, LANES), lambda i: (0, 0)),
        ],
        out_specs=[
            pl.BlockSpec((tm, D_MODEL), lambda i: (i, 0)),
            pl.BlockSpec((tm, LANES), lambda i: (i, 0)),
            pl.BlockSpec((tm, LANES), lambda i: (i, 0)),
            pl.BlockSpec((BF16_ROWS, tm), lambda i: (0, i)),
            pl.BlockSpec((8, LANES), lambda i: (i, 0)),
        ],
        out_shape=[
            jax.ShapeDtypeStruct((n, D_MODEL), BF16),
            jax.ShapeDtypeStruct((n, LANES), F32),
            jax.ShapeDtypeStruct((n, LANES), F32),
            jax.ShapeDtypeStruct((BF16_ROWS, n), F32),
            jax.ShapeDtypeStruct((n // tm * 8, LANES), jnp.int32),
        ],
        compiler_params=_cparams(("parallel",)),
        name="router",
    )(x, g, router)


def _moe_kernel(cnt_ref, h_ref, c_ref, rc_ref, rr_ref, wg_ref, wu_ref, wd_ref, o_ref, xg_scr, yg_scr, *, rows, parts):
    i = pl.program_id(0)
    e = pl.program_id(1)
    f = pl.program_id(2)
    last_f = f == pl.num_programs(2) - 1
    t = h_ref.shape[0] // parts
    nsub = xg_scr.shape[1] // rows
    cnts = [cnt_ref[(i * parts + p) * N_EXPERTS + e] for p in range(parts)]
    toks = [slice(p * t, (p + 1) * t) for p in range(parts)]
    blocks = [(sb, slice(sb * rows, (sb + 1) * rows)) for sb in range(nsub)]

    @pl.when(jnp.logical_and(e == 0, f == 0))
    def _():
        o_ref[...] = jnp.zeros_like(o_ref)

    @pl.when(f == 0)
    def _():
        for p in range(parts):
            rr = rr_ref[pl.ds(e, 1), toks[p]]
            for sb, rs in blocks:
                @pl.when(cnts[p] > sb * rows)
                def _():
                    slot = (lax.broadcasted_iota(jnp.int32, (rows, t), 0) + sb * rows).astype(F32)
                    gather = jnp.where(rr == slot, 1.0, 0.0).astype(BF16)
                    xg_scr[p, rs, :] = jnp.dot(gather, h_ref[toks[p], :], preferred_element_type=F32).astype(BF16)
                    yg_scr[p, rs, :] = jnp.zeros((rows, D_MODEL), F32)

    for p in range(parts):
        for sb, rs in blocks:
            @pl.when(cnts[p] > sb * rows)
            def _():
                xb = xg_scr[p, rs, :]
                act = (_silu(jnp.dot(xb, wg_ref[0], preferred_element_type=F32))
                       * jnp.dot(xb, wu_ref[0], preferred_element_type=F32))
                yg_scr[p, rs, :] += _mm(act, wd_ref[0])

    @pl.when(last_f)
    def _():
        mine = lax.broadcasted_iota(jnp.int32, (t, LANES), 1) == e
        for p in range(parts):
            rc = jnp.sum(jnp.where(mine, rc_ref[toks[p], :], 0.0), axis=-1, keepdims=True)
            ce = jnp.sum(jnp.where(mine, c_ref[toks[p], :], 0.0), axis=-1, keepdims=True)
            for sb, rs in blocks:
                @pl.when(cnts[p] > sb * rows)
                def _():
                    slot = (lax.broadcasted_iota(jnp.int32, (t, rows), 1) + sb * rows).astype(F32)
                    scatter = jnp.where(rc == slot, 1.0, 0.0).astype(BF16)
                    o_ref[toks[p], :] += ce * jnp.dot(scatter, yg_scr[p, rs, :].astype(BF16),
                                                      preferred_element_type=F32)


def _moe(x, g, router, wg, wu, wd, tm, parts, tf, rows):
    n = x.shape[0]
    ff = wg.shape[2]
    h, c, rc, rr, cnt = _router(x, g, router, tm)
    cnt = cnt.reshape(n // tm, 8, LANES)[:, 0, :N_EXPERTS].reshape(-1)
    nsub = -(-tm // rows)
    tt = tm * parts
    tok = lambda cols, **kw: pl.BlockSpec((tt, cols), lambda i, e, f, cnt: (i, 0), **kw)
    once = dict(pipeline_mode=pl.Buffered(1))
    return pl.pallas_call(
        functools.partial(_moe_kernel, rows=rows, parts=parts),
        grid_spec=pltpu.PrefetchScalarGridSpec(
            num_scalar_prefetch=1,
            grid=(n // tt, N_EXPERTS, ff // tf),
            in_specs=[
                tok(D_MODEL, **once), tok(LANES), tok(LANES),
                pl.BlockSpec((BF16_ROWS, tt), lambda i, e, f, cnt: (0, i)),
                pl.BlockSpec((1, D_MODEL, tf), lambda i, e, f, cnt: (e, 0, f)),
                pl.BlockSpec((1, D_MODEL, tf), lambda i, e, f, cnt: (e, 0, f)),
                pl.BlockSpec((1, tf, D_MODEL), lambda i, e, f, cnt: (e, f, 0)),
            ],
            out_specs=tok(D_MODEL),
            scratch_shapes=[pltpu.VMEM((parts, nsub * rows, D_MODEL), BF16),
                            pltpu.VMEM((parts, nsub * rows, D_MODEL), F32)],
        ),
        out_shape=jax.ShapeDtypeStruct((n, D_MODEL), F32),
        compiler_params=_cparams(("parallel", "arbitrary", "arbitrary"), vmem_mb=56),
        name="moe",
    )(cnt, h, c, rc, rr, wg, wu, wd)


def _ple_kernel(x_ref, d_ref, p_ref, g_ref, wgate_ref, wproj_ref, fin_ref, o_ref, *, final):
    x = x_ref[...] + d_ref[...]
    h = _rms(x, g_ref[...], NORM_EPS)
    y = x + _sigmoid(_mm(h, wgate_ref[...])) * _mm(p_ref[...], wproj_ref[...])
    o_ref[...] = _rms(y, fin_ref[...], NORM_EPS) if final else y


def _ple(x, delta, p, g, wgate, wproj, fin, tm, final):
    n = x.shape[0]
    tok = lambda c: pl.BlockSpec((tm, c), lambda i: (i, 0))
    const = lambda r, c: pl.BlockSpec((r, c), lambda i: (0, 0))
    return pl.pallas_call(
        functools.partial(_ple_kernel, final=final),
        grid=(n // tm,),
        in_specs=[tok(D_MODEL), tok(D_MODEL), tok(P_DIM), const(1, D_MODEL), const(D_MODEL, D_MODEL),
                  const(P_DIM, D_MODEL), const(1, D_MODEL)],
        out_specs=tok(D_MODEL),
        out_shape=jax.ShapeDtypeStruct((n, D_MODEL), F32),
        compiler_params=_cparams(("parallel",)),
        name="ple",
    )(x, delta, p, g, wgate, wproj, fin)


def _tiles(n, seq):
    tm = min(512, seq)
    tm_big = 1024 if n % 1024 == 0 else tm
    tm_in = 2048 if n % 2048 == 0 else tm_big
    moe_rows = tm_big // 4 + tm_big // 32
    moe_parts = 2 if n % (2 * tm_big) == 0 else 1
    return dict(tm=tm, tm_big=tm_big, tm_in=tm_in, tn_in=1024, tq=min(512, seq), cblk=min(512, seq), moe_rows=moe_rows,
                moe_parts=moe_parts)


def _rope_tables(seq):
    half = ROPE_DIMS // 2
    inv = ROPE_THETA ** (-jnp.arange(half, dtype=F32) * 2.0 / ROPE_DIMS)
    ang = jnp.arange(seq, dtype=F32)[:, None] * inv[None, :]
    pad = jnp.zeros((seq, DIFF_DH - ROPE_DIMS), F32)
    cos = jnp.concatenate([jnp.cos(ang), jnp.cos(ang), pad + 1.0], axis=-1)
    sin = jnp.concatenate([jnp.sin(ang), jnp.sin(ang), pad], axis=-1)
    d = jnp.arange(W_MIX)
    dd = d % DIFF_DH
    src = jnp.where(dd < half, d + half, d - half)
    sign = jnp.where(dd < half, -1.0, jnp.where(dd < ROPE_DIMS, 1.0, 0.0))
    rot = jnp.zeros((W_MIX, W_MIX), F32).at[src, d].set(sign)
    reps = W_MIX // DIFF_DH
    return jnp.tile(cos, (1, reps)), jnp.tile(sin, (1, reps)), rot.astype(BF16)


def _split_w_in(w):
    a0 = 0
    b0 = a0 + 4 * W_MIX
    c0 = b0 + 3 * W_MIX
    d0 = c0 + 3 * W_MIX + N_HEADS
    g0 = d0 + 4 * W_MIX + 2 * N_HEADS
    d_small = d0 + 3 * W_MIX
    diff_q = w[:, b0:b0 + W_MIX] * (DIFF_DH ** -0.5 * LOG2E)
    fox_q = w[:, c0:c0 + W_MIX] * (HEAD_DIM ** -0.5 * LOG2E)
    main = jnp.concatenate([
        w[:, g0:], w[:, a0:b0], w[:, d0:d_small], w[:, d_small + 2 * N_HEADS:g0],
        diff_q, w[:, b0 + W_MIX:b0 + 2 * W_MIX], fox_q, w[:, c0 + W_MIX:c0 + 2 * W_MIX],
    ], axis=1).astype(BF16)
    small = jnp.concatenate([
        w[:, c0 + 3 * W_MIX:d0], w[:, d_small:d_small + 2 * N_HEADS],
        jnp.zeros((D_MODEL, LANES - 3 * N_HEADS), w.dtype),
    ], axis=1).astype(BF16)
    v_t = jnp.concatenate([w[:, b0 + 2 * W_MIX:c0], w[:, c0 + 2 * W_MIX:c0 + 3 * W_MIX]], axis=1).T.astype(BF16)
    return main, small, v_t


def _small_params(fbias, a_log, dt_bias):
    zeros = jnp.zeros((N_HEADS,), F32)
    bias = jnp.concatenate([fbias, zeros, dt_bias, jnp.zeros((LANES - 3 * N_HEADS,), F32)])
    neg_a = jnp.concatenate([zeros, zeros, -jnp.exp(a_log), jnp.zeros((LANES - 3 * N_HEADS,), F32)])
    return bias.reshape(1, LANES), neg_a.reshape(1, LANES)


def _pad_rows(w, top, total):
    return jnp.concatenate([jnp.zeros((top, w.shape[1]), w.dtype), w,
                            jnp.zeros((total - top - w.shape[0], w.shape[1]), w.dtype)], axis=0)


def kernel(x, p, norm_mix, norm_ffn, norm_ple, w_in, w_bo, w_out, rwkv_mu, rwkv_w0, rwkv_w2, rwkv_a0, rwkv_a2, rwkv_g2, rwkv_kk, rwkv_ka, rwkv_rk, rwkv_ln_w, rwkv_ln_b, diff_lam, diff_subln, fox_fbias, gdn_conv, gdn_a_log, gdn_dt_bias, gdn_norm, ffn_w_gate, ffn_w_up, ffn_w_down, moe_router, moe_w_gate, moe_w_up, moe_w_down, ple_proj, ple_gate, final_norm):
    batch, seq, _ = x.shape
    depth = w_in.shape[0]
    n = batch * seq
    t = _tiles(n, seq)
    tm, tq, cblk = t["tm"], t["tq"], t["cblk"]
    row = lambda v: v.reshape(1, -1).astype(F32)
    cos, sin, rot = _rope_tables(seq)
    xf = x.reshape(n, D_MODEL)
    pf = p.reshape(depth, n, P_DIM)

    for i in range(depth):
        w_main, w_small, w_vt = _split_w_in(w_in[i])
        u, scol, vt = _inproj(xf, row(norm_mix[i]), w_main, w_small, w_vt, t["tm_in"], t["tn_in"])
        bias, neg_a = _small_params(fox_fbias[i], gdn_a_log[i], gdn_dt_bias[i])
        hcol, fox_kb, fox_qb = _small_prep(scol, bias, neg_a, batch, seq, tm)

        scan_in, pc, post = _rwkv_prep(
            u, row(rwkv_mu[i]), row(rwkv_w0[i]), _pad_rows(rwkv_w2[i], 0, LANES), row(rwkv_a0[i]),
            _pad_rows(rwkv_a2[i], LANES // 2, LANES), rwkv_g2[i], row(rwkv_kk[i]), row(rwkv_ka[i]), row(rwkv_rk[i]),
            seq, tm)
        o_a = _rwkv_chunk(scan_in, pc, batch, seq, cblk)

        o_d = _gdn_chunk(_gdn_prep(u, gdn_conv[i].T, seq, tm), hcol, jnp.tile(row(gdn_norm[i]), (1, N_HEADS)),
                         batch, seq, cblk)

        lam_init = 0.8 - 0.6 * math.exp(-0.3 * i)
        y_b = _diff_attention(u, vt, cos, sin, rot, diff_lam[i].astype(F32), jnp.tile(row(diff_subln[i]), (1, 2)),
                              batch, seq, tq, lam_init)
        y_c = _fox_attention(u, vt, fox_qb, fox_kb, batch, seq, tq)

        xf = _merge(xf, u, o_a, post, y_b, y_c, o_d, row(rwkv_ln_w[i]), row(rwkv_ln_b[i]),
                    w_bo[i].astype(BF16), w_out[i].astype(BF16), tm)

        j = i // 2
        if i % 2 == 0:
            delta = _ffn(xf, row(norm_ffn[i]), ffn_w_gate[j].astype(BF16), ffn_w_up[j].astype(BF16),
                         ffn_w_down[j].astype(BF16), tm, ffn_w_gate.shape[2] // 2)
        else:
            router = jnp.concatenate([moe_router[j], jnp.zeros((D_MODEL, LANES - N_EXPERTS), F32)], axis=1)
            delta = _moe(xf, row(norm_ffn[i]), router, moe_w_gate[j].astype(BF16), moe_w_up[j].astype(BF16),
                         moe_w_down[j].astype(BF16), t["tm_big"], t["moe_parts"], moe_w_gate.shape[3] // 7,
                         t["moe_rows"])
        xf = _ple(xf, delta, pf[i], row(norm_ple[i]), ple_gate[i].astype(BF16), ple_proj[i].astype(BF16),
                  row(final_norm), tm, i == depth - 1)
    return xf.reshape(batch, seq, D_MODEL)
```

```python
import functools
import math

import jax
import jax.numpy as jnp
import numpy as np
from jax import lax
from jax.experimental import pallas as pl
from jax.experimental.pallas import tpu as pltpu

F32 = jnp.float32
BF16 = jnp.bfloat16
HIGHEST = lax.Precision.HIGHEST

D_MODEL = 1024
P_DIM = 256
W_MIX = 256
HEAD_DIM = 64
N_HEADS = 4
DIFF_DH = 32
ROPE_THETA = 500000.0
ROPE_DIMS = 8
RWKV_GN_EPS = 64e-5
DIFF_LN_EPS = 1e-5
GDN_CONV = 4
CHUNK = 64
CHUNK_UNROLL = 4
N_EXPERTS = 8
NORM_EPS = 1e-6
L2_EPS = 1e-6
LOG2E = math.log2(math.e)
LANES = 128
BF16_ROWS = 16
assert CHUNK == HEAD_DIM

U_GATE = 0
U_RWKV = 4096
U_GDN = 5120
U_DIFF = 6144
U_FOX = 6656
U_COLS = 7168
SM_FOX, SM_BETA, SM_DEC = 0, 4, 8
FOX_CK, FOX_CQ = 0, 16


def _cparams(semantics, vmem_mb=48):
    return pltpu.CompilerParams(dimension_semantics=semantics, vmem_limit_bytes=vmem_mb * 1024 * 1024)


def _mm(a, b):
    return jnp.dot(a.astype(BF16), b.astype(BF16), preferred_element_type=F32)


def _mm_nt(a, b):
    return lax.dot_general(a.astype(BF16), b.astype(BF16), (((1,), (1,)), ((), ())), preferred_element_type=F32)


def _mm_tn(a, b):
    return lax.dot_general(a.astype(BF16), b.astype(BF16), (((0,), (0,)), ((), ())), preferred_element_type=F32)


def _mm_f32(a, b):
    return jnp.dot(a, b, preferred_element_type=F32, precision=HIGHEST)


def _mm_mask(mask, x):
    hi = x.astype(BF16)
    r1 = x - hi.astype(F32)
    mid = r1.astype(BF16)
    lo = (r1 - mid.astype(F32)).astype(BF16)
    dot = lambda t: jnp.dot(mask, t, preferred_element_type=F32)
    return dot(hi) + dot(mid) + dot(lo)


def _rms(x, g, eps):
    return x * lax.rsqrt(jnp.mean(x * x, axis=-1, keepdims=True) + eps) * g


def _sigmoid(x):
    return 1.0 / (1.0 + jnp.exp(-x))


def _silu(x):
    return x * _sigmoid(x)


def _softplus(x):
    return jnp.maximum(x, 0.0) + jnp.log(1.0 + jnp.exp(-jnp.abs(x)))


def _tri_masks(c):
    ii = lax.broadcasted_iota(jnp.int32, (c, c), 0)
    jj = lax.broadcasted_iota(jnp.int32, (c, c), 1)
    return ii > jj, ii >= jj, ii == jj


def _block_masks():
    ii = lax.broadcasted_iota(jnp.int32, (W_MIX, W_MIX), 0)
    jj = lax.broadcasted_iota(jnp.int32, (W_MIX, W_MIX), 1)
    return (ii // HEAD_DIM) == (jj // HEAD_DIM), ii == jj


def _wide_masks():
    ii = lax.broadcasted_iota(jnp.int32, (CHUNK, W_MIX), 0)
    jj = lax.broadcasted_iota(jnp.int32, (CHUNK, W_MIX), 1) % HEAD_DIM
    return ii > jj, ii >= jj, ii == jj


def _stack_heads(x, same):
    return jnp.where(same, jnp.concatenate([x, x, x, x], axis=0), jnp.zeros((), x.dtype))


def _unit_lower_inverses(ns, eye_w, same):
    rs = [eye_w + n for n in ns]
    ps = list(ns)
    for _ in range(int(math.log2(CHUNK)) - 1):
        ps = [_mm(p, _stack_heads(p.astype(BF16), same)) for p in ps]
        rs = [r + _mm(r, _stack_heads(p.astype(BF16), same)) for r, p in zip(rs, ps)]
    return rs


def _unstack_heads(x):
    return x[0:CHUNK] + x[CHUNK:2 * CHUNK] + x[2 * CHUNK:3 * CHUNK] + x[3 * CHUNK:4 * CHUNK]


def _inproj_kernel(x_ref, g_ref, w_ref, ws_ref, wvt_ref, u_ref, scol_ref, vt_ref, h_scr):
    @pl.when(pl.program_id(1) == 0)
    def _():
        hb = _rms(x_ref[...], g_ref[...], NORM_EPS).astype(BF16)
        h_scr[...] = hb
        scol_ref[...] = jnp.dot(hb, ws_ref[...], preferred_element_type=F32)
        vt_ref[...] = lax.dot_general(wvt_ref[...], hb, (((1,), (1,)), ((), ())),
                                      preferred_element_type=F32).astype(BF16)

    u_ref[...] = jnp.dot(h_scr[...], w_ref[...], preferred_element_type=F32).astype(BF16)


def _inproj(x, g, w, ws, wvt, tm, tn):
    n = x.shape[0]
    return pl.pallas_call(
        _inproj_kernel,
        grid=(n // tm, U_COLS // tn),
        in_specs=[
            pl.BlockSpec((tm, D_MODEL), lambda i, j: (i, 0)),
            pl.BlockSpec((1, D_MODEL), lambda i, j: (0, 0)),
            pl.BlockSpec((D_MODEL, tn), lambda i, j: (0, j)),
            pl.BlockSpec((D_MODEL, LANES), lambda i, j: (0, 0)),
            pl.BlockSpec((2 * W_MIX, D_MODEL), lambda i, j: (0, 0)),
        ],
        out_specs=[
            pl.BlockSpec((tm, tn), lambda i, j: (i, j)),
            pl.BlockSpec((tm, LANES), lambda i, j: (i, 0)),
            pl.BlockSpec((2 * W_MIX, tm), lambda i, j: (0, i)),
        ],
        out_shape=[
            jax.ShapeDtypeStruct((n, U_COLS), BF16),
            jax.ShapeDtypeStruct((n, LANES), F32),
            jax.ShapeDtypeStruct((2 * W_MIX, n), BF16),
        ],
        scratch_shapes=[pltpu.VMEM((tm, D_MODEL), BF16)],
        compiler_params=_cparams(("parallel", "arbitrary"), vmem_mb=56),
        name="inproj",
    )(x, g, w, ws, wvt)


def _lane_placement(base):
    r = lax.broadcasted_iota(jnp.int32, (3 * LANES, LANES), 0)
    m = lax.broadcasted_iota(jnp.int32, (3 * LANES, LANES), 1)
    head, part = r % LANES, r // LANES
    return jnp.logical_and(head < N_HEADS, m == base + 3 * head + part).astype(BF16)


def _split3(x):
    hi = x.astype(BF16)
    r1 = x - hi.astype(F32)
    mid = r1.astype(BF16)
    lo = (r1 - mid.astype(F32)).astype(BF16)
    return jnp.concatenate([hi, mid, lo], axis=1)


def _head_expansion():
    r = lax.broadcasted_iota(jnp.int32, (3 * LANES, 2 * W_MIX), 0) % LANES
    m = lax.broadcasted_iota(jnp.int32, (3 * LANES, 2 * W_MIX), 1)
    src = jnp.where(m < W_MIX, SM_BETA + m // HEAD_DIM, SM_DEC + (m - W_MIX) // HEAD_DIM)
    return (r == src).astype(BF16)


def _small_prep_kernel(scol_ref, bias_ref, nega_ref, ogb_ref, okb_ref, oqb_ref, carry_scr):
    @pl.when(pl.program_id(1) == 0)
    def _():
        carry_scr[...] = jnp.zeros_like(carry_scr)

    tm = scol_ref.shape[0]
    lower = _tri_masks(tm)[1].astype(BF16)
    z = scol_ref[...] + bias_ref[...]
    lane = lax.broadcasted_iota(jnp.int32, z.shape, 1)
    is_f = lane < SM_BETA
    is_b = jnp.logical_and(lane >= SM_BETA, lane < SM_DEC)
    is_d = jnp.logical_and(lane >= SM_DEC, lane < SM_DEC + N_HEADS)
    logf = jnp.where(is_f, jnp.minimum(z, 0.0) - jnp.log(1.0 + jnp.exp(-jnp.abs(z))), 0.0)
    gdn = jnp.where(is_b, _sigmoid(z), jnp.where(is_d, nega_ref[...] * _softplus(z), 0.0))
    ogb_ref[...] = jnp.dot(_split3(gdn), _head_expansion(), preferred_element_type=F32)
    cum = _mm_mask(lower, logf) + carry_scr[...]
    carry_scr[...] = cum[tm - 1:tm, :]
    parts = _split3(cum * LOG2E)
    ones_lanes = jnp.logical_and(lane >= FOX_CQ, lane < FOX_CQ + 3 * N_HEADS)
    okb_ref[...] = (jnp.dot(parts, _lane_placement(FOX_CK), preferred_element_type=F32)
                    + jnp.where(ones_lanes, 1.0, 0.0)).astype(BF16)
    oqb_ref[...] = jnp.dot(parts, _lane_placement(FOX_CQ), preferred_element_type=F32).astype(BF16)


def _small_prep(scol, bias, neg_a, batch, seq, tm):
    n = batch * seq
    nt = seq // tm
    tok = lambda: pl.BlockSpec((tm, LANES), lambda b, j: (b * nt + j, 0))
    const = lambda: pl.BlockSpec((1, LANES), lambda b, j: (0, 0))
    return pl.pallas_call(
        _small_prep_kernel,
        grid=(batch, nt),
        in_specs=[tok(), const(), const()],
        out_specs=[pl.BlockSpec((tm, 2 * W_MIX), lambda b, j: (b * nt + j, 0)), tok(), tok()],
        out_shape=[jax.ShapeDtypeStruct((n, 2 * W_MIX), F32), jax.ShapeDtypeStruct((n, LANES), BF16),
                   jax.ShapeDtypeStruct((n, LANES), BF16)],
        scratch_shapes=[pltpu.VMEM((1, LANES), F32)],
        compiler_params=_cparams(("parallel", "arbitrary")),
        name="small_prep",
    )(scol, bias, neg_a)


def _rwkv_prep_kernel(u_ref, up_ref, mu_ref, w0_ref, w2_ref, a0_ref, a2_ref, g2_ref, kk_ref, ka_ref, rk_ref,
                      scan_ref, pc_ref, post_ref, *, tiles_per_seq):
    tm = u_ref.shape[0]
    u = u_ref[...].astype(F32)
    prev = up_ref[...].astype(F32)[BF16_ROWS - 1:BF16_ROWS, :]
    prev = jnp.where(pl.program_id(0) % tiles_per_seq == 0, 0.0, prev)
    rows = lax.broadcasted_iota(jnp.int32, (tm, 1), 0)
    u_prev = jnp.where(rows == 0, prev, pltpu.roll(u, 1, 0))
    xm = u + (u_prev - u) * mu_ref[...]
    r = xm[:, 0:W_MIX]
    k = xm[:, W_MIX:2 * W_MIX]
    v = xm[:, 2 * W_MIX:3 * W_MIX]
    x_lora = xm[:, 3 * W_MIX:3 * W_MIX + LANES]
    xg = xm[:, 3 * W_MIX + LANES:]
    logw = -_softplus(-(w0_ref[...] + _mm(jnp.tanh(x_lora), w2_ref[...]))) - 0.5
    log_decay = -jnp.exp(logw)
    a = _sigmoid(a0_ref[...] + _mm(x_lora, a2_ref[...]))
    g = _mm(_sigmoid(xg), g2_ref[...])
    same = _block_masks()[0].astype(F32)
    kk_raw = k * kk_ref[...]
    kk = kk_raw * lax.rsqrt(_mm_f32(kk_raw * kk_raw, same) + L2_EPS)
    k2 = k * (1.0 + (a - 1.0) * ka_ref[...])
    bonus = _mm_f32(r * k2 * rk_ref[...], same) * v
    ti = lax.broadcasted_iota(jnp.int32, (tm, tm), 0)
    tj = lax.broadcasted_iota(jnp.int32, (tm, tm), 1)
    in_chunk = jnp.logical_and(ti // CHUNK == tj // CHUNK, ti >= tj).astype(BF16)
    ci = lax.broadcasted_iota(jnp.int32, (tm // CHUNK, tm), 0)
    cj = lax.broadcasted_iota(jnp.int32, (tm // CHUNK, tm), 1)
    cum = _mm_mask(in_chunk, log_decay)
    cum_end = _mm_mask((ci == cj // CHUNK).astype(BF16), log_decay)
    inv = jnp.exp(-cum)
    scan_ref[0] = (-kk * jnp.exp(cum - log_decay)).astype(BF16)
    scan_ref[1] = (kk * a * inv).astype(BF16)
    scan_ref[2] = (k2 * inv).astype(BF16)
    scan_ref[3] = (r * jnp.exp(cum)).astype(BF16)
    scan_ref[4] = v.astype(BF16)
    pc_ref[...] = jnp.exp(cum_end)
    post_ref[0] = g
    post_ref[1] = bonus


def _rwkv_prep(u, mu, w0, w2p, a0, a2p, g2, k_k, k_a, r_k, seq, tm):
    n = u.shape[0]
    ublk = U_RWKV // D_MODEL
    row = lambda c: pl.BlockSpec((1, c), lambda i: (0, 0))
    mat = lambda r: pl.BlockSpec((r, W_MIX), lambda i: (0, 0))
    return pl.pallas_call(
        functools.partial(_rwkv_prep_kernel, tiles_per_seq=seq // tm),
        grid=(n // tm,),
        in_specs=[
            pl.BlockSpec((tm, D_MODEL), lambda i: (i, ublk)),
            pl.BlockSpec((BF16_ROWS, D_MODEL), lambda i: (jnp.maximum(i * (tm // BF16_ROWS) - 1, 0), ublk)),
            row(D_MODEL), row(W_MIX), mat(LANES), row(W_MIX), mat(LANES), mat(LANES), row(W_MIX), row(W_MIX), row(W_MIX),
        ],
        out_specs=[
            pl.BlockSpec((5, tm, W_MIX), lambda i: (0, i, 0)),
            pl.BlockSpec((tm // CHUNK, W_MIX), lambda i: (i, 0)),
            pl.BlockSpec((2, tm, W_MIX), lambda i: (0, i, 0)),
        ],
        out_shape=[jax.ShapeDtypeStruct((5, n, W_MIX), BF16), jax.ShapeDtypeStruct((n // CHUNK, W_MIX), F32),
                   jax.ShapeDtypeStruct((2, n, W_MIX), F32)],
        compiler_params=_cparams(("parallel",)),
        name="rwkv_prep",
    )(u, u, mu, w0, w2p, a0, a2p, g2, k_k, k_a, r_k)


def _rwkv_chunk_kernel(x_ref, pc_ref, o_ref, s_scr, *, nchunk, nbatch):
    @pl.when(pl.program_id(0) == 0)
    def _():
        s_scr[...] = jnp.zeros_like(s_scr)

    same, diag = _block_masks()
    eye = diag.astype(F32)
    strict_w, incl_w, diag_w = _wide_masks()
    eye_w = diag_w.astype(F32)
    stack = lambda x: _stack_heads(x.astype(BF16), same)

    def chunks(i, carry):
        items = [(b, i * CHUNK_UNROLL + j) for j in range(CHUNK_UNROLL) for b in range(nbatch)]
        each = lambda f, *lists: [f(*args) for args in zip(*lists)]
        sls = [pl.ds(pl.multiple_of(ci * CHUNK, CHUNK), CHUNK) for _, ci in items]
        a, bb, k, r, v = ([x_ref[i, b, sl, :] for (b, _), sl in zip(items, sls)] for i in range(5))
        pc = [pc_ref[b, pl.ds(ci, 1), :] for b, ci in items]
        a_s, b_s, k_s, v_s = (each(stack, x) for x in (a, bb, k, v))
        m_ab = each(lambda x, y: jnp.where(strict_w, _mm_nt(x, y), 0.0), a, b_s)
        m_ak = each(lambda x, y: jnp.where(strict_w, _mm_nt(x, y), 0.0), a, k_s)
        n_rb = each(lambda x, y: jnp.where(incl_w, _mm_nt(x, y), 0.0), r, b_s)
        n_rk = each(lambda x, y: jnp.where(incl_w, _mm_nt(x, y), 0.0), r, k_s)
        t_inv = _unit_lower_inverses(m_ab, eye_w, same)
        a2 = each(_mm, t_inv, a_s)
        u0 = each(lambda t, m, x: _mm(t, stack(_mm(m, x))), t_inv, m_ak, v_s)
        r2 = each(lambda x, n, y: x.astype(F32) + _mm(n, stack(y)), r, n_rb, a2)
        o0 = each(lambda n1, u, n2, x: _mm(n1, stack(u)) + _mm(n2, x), n_rb, u0, n_rk, v_s)
        b_end = each(lambda x, p: x.astype(F32) * p, bb, pc)
        k_end = each(lambda x, p: x.astype(F32) * p, k, pc)
        g_mat = each(lambda p, x, y: eye * p + jnp.where(same, _mm_tn(x, y), 0.0), pc, a2, b_end)
        s0 = each(lambda u, x, y, z: jnp.where(same, _mm_tn(u, x) + _mm_tn(y, z), 0.0), u0, b_end, v, k_end)
        for n, ((b, _), sl) in enumerate(zip(items, sls)):
            s = s_scr[b]
            o = _stack_heads(_mm_nt(r2[n], s) + o0[n], same)
            s_scr[b] = _mm(s, g_mat[n]) + s0[n]
            mean = jnp.sum(o, axis=-1, keepdims=True) * (1.0 / HEAD_DIM)
            cen = jnp.where(same, o - mean, 0.0)
            var = jnp.sum(cen * cen, axis=-1, keepdims=True) * (1.0 / HEAD_DIM)
            o_ref[b, sl, :] = _unstack_heads(cen * lax.rsqrt(var + RWKV_GN_EPS))
        return carry

    lax.fori_loop(0, nchunk // CHUNK_UNROLL, chunks, 0)


def _rwkv_chunk(xs, pc, batch, seq, cblk):
    xs = xs.reshape(5, batch, seq, W_MIX)
    pc = pc.reshape(batch, seq // CHUNK, W_MIX)
    out = pl.pallas_call(
        functools.partial(_rwkv_chunk_kernel, nchunk=cblk // CHUNK, nbatch=batch),
        grid=(seq // cblk,),
        in_specs=[
            pl.BlockSpec((5, batch, cblk, W_MIX), lambda j: (0, 0, j, 0)),
            pl.BlockSpec((batch, cblk // CHUNK, W_MIX), lambda j: (0, j, 0)),
        ],
        out_specs=pl.BlockSpec((batch, cblk, W_MIX), lambda j: (0, j, 0)),
        out_shape=jax.ShapeDtypeStruct((batch, seq, W_MIX), F32),
        scratch_shapes=[pltpu.VMEM((batch, W_MIX, W_MIX), F32)],
        compiler_params=_cparams(("arbitrary",)),
        name="rwkv_chunk",
    )(xs, pc)
    return out.reshape(batch * seq, W_MIX)


def _gdn_prep_kernel(u_ref, up_ref, cw_ref, o_ref, ext_scr, *, tiles_per_seq):
    tm = u_ref.shape[0]
    c3 = 3 * W_MIX
    prev = up_ref[...].astype(F32)[:, :c3]
    ext_scr[0:BF16_ROWS, :] = jnp.where(pl.program_id(0) % tiles_per_seq == 0, 0.0, prev)
    ext_scr[BF16_ROWS:, :] = u_ref[...].astype(F32)[:, :c3]
    y = jnp.zeros((tm, c3), F32)
    for j in range(GDN_CONV):
        y = y + ext_scr[pl.ds(BF16_ROWS - (GDN_CONV - 1) + j, tm), :] * cw_ref[j:j + 1, :]
    y = _silu(y)
    same = _block_masks()[0].astype(F32)
    q = y[:, 0:W_MIX]
    k = y[:, W_MIX:2 * W_MIX]
    o_ref[0] = (q * lax.rsqrt(_mm_f32(q * q, same) + L2_EPS) * (HEAD_DIM ** -0.5)).astype(BF16)
    o_ref[1] = (k * lax.rsqrt(_mm_f32(k * k, same) + L2_EPS)).astype(BF16)
    o_ref[2] = y[:, 2 * W_MIX:].astype(BF16)


def _gdn_prep(u, conv_w, seq, tm):
    n = u.shape[0]
    ublk = U_GDN // D_MODEL
    return pl.pallas_call(
        functools.partial(_gdn_prep_kernel, tiles_per_seq=seq // tm),
        grid=(n // tm,),
        in_specs=[
            pl.BlockSpec((tm, D_MODEL), lambda i: (i, ublk)),
            pl.BlockSpec((BF16_ROWS, D_MODEL), lambda i: (jnp.maximum(i * (tm // BF16_ROWS) - 1, 0), ublk)),
            pl.BlockSpec((GDN_CONV, 3 * W_MIX), lambda i: (0, 0)),
        ],
        out_specs=pl.BlockSpec((3, tm, W_MIX), lambda i: (0, i, 0)),
        out_shape=jax.ShapeDtypeStruct((3, n, W_MIX), BF16),
        scratch_shapes=[pltpu.VMEM((tm + BF16_ROWS, 3 * W_MIX), F32)],
        compiler_params=_cparams(("parallel",)),
        name="gdn_prep",
    )(u, u, conv_w)


def _gdn_chunk_kernel(x_ref, gb_ref, nw_ref, o_ref, s_scr, *, nchunk, nbatch):
    @pl.when(pl.program_id(0) == 0)
    def _():
        s_scr[...] = jnp.zeros_like(s_scr)

    same, diag = _block_masks()
    eye = diag.astype(F32)
    strict_w, incl_w, diag_w = _wide_masks()
    eye_w = diag_w.astype(F32)
    lower = _tri_masks(CHUNK)[1].astype(BF16)
    stack = lambda x: _stack_heads(x.astype(BF16), same)

    def chunks(i, carry):
        items = [(b, i * CHUNK_UNROLL + j) for j in range(CHUNK_UNROLL) for b in range(nbatch)]
        each = lambda f, *lists: [f(*args) for args in zip(*lists)]
        sls = [pl.ds(pl.multiple_of(ci * CHUNK, CHUNK), CHUNK) for _, ci in items]
        q, k, v = ([x_ref[i, b, sl, :] for (b, _), sl in zip(items, sls)] for i in range(3))
        beta = [gb_ref[b, sl, 0:W_MIX] for (b, _), sl in zip(items, sls)]
        g = [gb_ref[b, sl, W_MIX:2 * W_MIX] for (b, _), sl in zip(items, sls)]
        k_s = each(stack, k)
        gam = [_mm_mask(lower, x) for x in g]
        gam_end = [x[CHUNK - 1:CHUNK, :] for x in gam]
        gdiff = [_mm_mask(lower, jnp.where(strict_w, x, 0.0)) for x in g]
        decay = [jnp.exp(jnp.where(incl_w, x, -jnp.inf)) for x in gdiff]
        a_mat = each(lambda bt, d, x, y: jnp.where(strict_w, bt * d * _mm_nt(x, y), 0.0), beta, decay, k, k_s)
        t_inv = _unit_lower_inverses([-a for a in a_mat], eye_w, same)
        e_gam = [jnp.exp(x) for x in gam]
        u0 = each(lambda t, bt, x: _mm(t, stack(bt * x.astype(F32))), t_inv, beta, v)
        wm = each(lambda t, bt, e, x: _mm(t, stack((bt * e) * x.astype(F32))), t_inv, beta, e_gam, k)
        qk = each(lambda x, y, d: _mm_nt(x, y) * d, q, k_s, decay)
        q2 = each(lambda e, x, a, w: e * x.astype(F32) - _mm(a, stack(w)), e_gam, q, qk, wm)
        o0 = each(lambda a, u: _mm(a, stack(u)), qk, u0)
        k_end = each(lambda x, ge, ga: x.astype(F32) * jnp.exp(ge - ga), k, gam_end, gam)
        g_mat = each(lambda ge, x, w: eye * jnp.exp(ge) - jnp.where(same, _mm_tn(x, w), 0.0), gam_end, k_end, wm)
        s0 = each(lambda x, u: jnp.where(same, _mm_tn(x, u), 0.0), k_end, u0)
        for n, ((b, _), sl) in enumerate(zip(items, sls)):
            s = s_scr[b]
            o = _stack_heads(_mm(q2[n], s) + o0[n], same)
            s_scr[b] = _mm(g_mat[n], s) + s0[n]
            ms = jnp.sum(o * o, axis=-1, keepdims=True) * (1.0 / HEAD_DIM)
            o_ref[b, sl, :] = _unstack_heads(o * lax.rsqrt(ms + NORM_EPS)) * nw_ref[...]
        return carry

    lax.fori_loop(0, nchunk // CHUNK_UNROLL, chunks, 0)


def _gdn_chunk(xs, gb, norm_w, batch, seq, cblk):
    xs = xs.reshape(3, batch, seq, W_MIX)
    gb = gb.reshape(batch, seq, 2 * W_MIX)
    out = pl.pallas_call(
        functools.partial(_gdn_chunk_kernel, nchunk=cblk // CHUNK, nbatch=batch),
        grid=(seq // cblk,),
        in_specs=[
            pl.BlockSpec((3, batch, cblk, W_MIX), lambda j: (0, 0, j, 0)),
            pl.BlockSpec((batch, cblk, 2 * W_MIX), lambda j: (0, j, 0)),
            pl.BlockSpec((1, W_MIX), lambda j: (0, 0)),
        ],
        out_specs=pl.BlockSpec((batch, cblk, W_MIX), lambda j: (0, j, 0)),
        out_shape=jax.ShapeDtypeStruct((batch, seq, W_MIX), F32),
        scratch_shapes=[pltpu.VMEM((batch, W_MIX, W_MIX), F32)],
        compiler_params=_cparams(("arbitrary",)),
        name="gdn_chunk",
    )(xs, gb, norm_w)
    return out.reshape(batch * seq, W_MIX)


def _causal_pairs(nq):
    pairs = [(i, j) for i in range(nq) for j in range(i + 1)]
    return jnp.asarray(np.array([p[0] for p in pairs], np.int32)), jnp.asarray(np.array([p[1] for p in pairs], np.int32))


def _softmax_updates(scores, vt_ones, m_scr, acc_scr):
    probs = []
    for i, s in enumerate(scores):
        m_old = m_scr[i]
        m_new = jnp.maximum(m_old, jnp.max(s, axis=0, keepdims=True))
        m_scr[i] = m_new
        probs.append((jnp.exp2(m_old - m_new), jnp.exp2((s - m_new).astype(BF16))))
    for i, (alpha, p) in enumerate(probs):
        acc_scr[i] = alpha * acc_scr[i] + jnp.dot(vt_ones[i], p, preferred_element_type=F32)


def _pair_lanes(h):
    p = h // 2
    return slice(p * LANES, (p + 1) * LANES), h % 2 == 0


def _values_and_ones(vt, tk):
    row = lax.broadcasted_iota(jnp.int32, (LANES, tk), 0)
    out = []
    for h in range(N_HEADS):
        slab, low = _pair_lanes(h)
        mine = row < HEAD_DIM if low else row >= HEAD_DIM
        out.append(jnp.where(mine, vt[slab, :], jnp.ones((), vt.dtype)))
    return out


def _normalized(acc, low):
    if low:
        return acc[0:HEAD_DIM] / acc[HEAD_DIM:HEAD_DIM + 1]
    return acc[HEAD_DIM:] / acc[0:1]


def _key_after_query(tq):
    return lax.broadcasted_iota(jnp.int32, (tq, tq), 0) > lax.broadcasted_iota(jnp.int32, (tq, tq), 1)


def _fox_kernel(qi_ref, kj_ref, q_ref, k_ref, vt_ref, qb_ref, kb_ref, o_ref, qm_scr, m_scr, acc_scr):
    t = pl.program_id(1)
    qi = qi_ref[t]
    kj = kj_ref[t]
    tq = q_ref.shape[0]

    @pl.when(kj == 0)
    def _():
        m_scr[...] = jnp.full_like(m_scr, -jnp.inf)
        acc_scr[...] = jnp.zeros_like(acc_scr)
        q = q_ref[...]
        qb = qb_ref[...]
        lane = lax.broadcasted_iota(jnp.int32, (tq, LANES), 1)
        zero = jnp.zeros((), BF16)
        for h in range(N_HEADS):
            slab, low = _pair_lanes(h)
            mine = lane < HEAD_DIM if low else lane >= HEAD_DIM
            ck_lanes = jnp.logical_and(lane >= FOX_CK + 3 * h, lane < FOX_CK + 3 * h + 3)
            cq_lanes = jnp.logical_and(lane >= FOX_CQ + 3 * h, lane < FOX_CQ + 3 * h + 3)
            bias = jnp.where(ck_lanes, -jnp.ones((), BF16), jnp.where(cq_lanes, qb, zero))
            qm_scr[h] = jnp.concatenate([jnp.where(mine, q[:, slab], zero), bias], axis=1)

    def step(diagonal):
        k = k_ref[...]
        kb = kb_ref[...]
        vts = _values_and_ones(vt_ref[...], tq)
        if diagonal:
            masked = _key_after_query(tq)
        keys = [jnp.concatenate([k[:, p * LANES:(p + 1) * LANES], kb], axis=1) for p in range(N_HEADS // 2)]
        scores = []
        for h in range(N_HEADS):
            s = lax.dot_general(keys[h // 2], qm_scr[h], (((1,), (1,)), ((), ())), preferred_element_type=F32)
            if diagonal:
                s = jnp.where(masked, -jnp.inf, s)
            scores.append(s)
        _softmax_updates(scores, vts, m_scr, acc_scr)

    @pl.when(kj < qi)
    def _():
        step(False)

    @pl.when(kj == qi)
    def _():
        step(True)
        for p in range(N_HEADS // 2):
            pair = jnp.concatenate([_normalized(acc_scr[2 * p], True), _normalized(acc_scr[2 * p + 1], False)], axis=0)
            o_ref[:, p * LANES:(p + 1) * LANES] = pair.T


def _attn_specs(nq, tq, ucol, vt_rows):
    cb = ucol // W_MIX
    q_spec = pl.BlockSpec((tq, W_MIX), lambda b, t, qi, kj: (b * nq + qi[t], cb))
    k_spec = pl.BlockSpec((tq, W_MIX), lambda b, t, qi, kj: (b * nq + kj[t], cb + 1))
    vt_spec = pl.BlockSpec((W_MIX, tq), lambda b, t, qi, kj: (vt_rows // W_MIX, b * nq + kj[t]))
    return q_spec, k_spec, vt_spec


def _fox_attention(u, vt, qb, kb, batch, seq, tq):
    nq = seq // tq
    qi, kj = _causal_pairs(nq)
    q_spec, k_spec, vt_spec = _attn_specs(nq, tq, U_FOX, W_MIX)
    stat = lambda: pltpu.VMEM((N_HEADS, 1, tq), F32)
    return pl.pallas_call(
        _fox_kernel,
        grid_spec=pltpu.PrefetchScalarGridSpec(
            num_scalar_prefetch=2,
            grid=(batch, qi.shape[0]),
            in_specs=[
                q_spec, k_spec, vt_spec,
                pl.BlockSpec((tq, LANES), lambda b, t, qi, kj: (b * nq + qi[t], 0)),
                pl.BlockSpec((tq, LANES), lambda b, t, qi, kj: (b * nq + kj[t], 0)),
            ],
            out_specs=pl.BlockSpec((tq, W_MIX), lambda b, t, qi, kj: (b * nq + qi[t], 0)),
            scratch_shapes=[pltpu.VMEM((N_HEADS, tq, 2 * LANES), BF16), stat(),
                            pltpu.VMEM((N_HEADS, LANES, tq), F32)],
        ),
        out_shape=jax.ShapeDtypeStruct((batch * seq, W_MIX), F32),
        compiler_params=_cparams(("parallel", "arbitrary")),
        name="fox_attention",
    )(qi, kj, u, u, vt, qb, kb)


def _diff_kernel(qi_ref, kj_ref, q_ref, k_ref, vt_ref, cq_ref, sq_ref, ck_ref, sk_ref, rot_ref, lam_ref, ln_ref, o_ref,
                 qm_scr, m_scr, acc_scr, *, lam_init):
    t = pl.program_id(1)
    qi = qi_ref[t]
    kj = kj_ref[t]
    tq = q_ref.shape[0]
    lane = lax.broadcasted_iota(jnp.int32, (tq, LANES), 1)

    def rope(x, cos, sin):
        return x.astype(F32) * cos + jnp.dot(x, rot_ref[...], preferred_element_type=F32) * sin

    @pl.when(kj == 0)
    def _():
        m_scr[...] = jnp.full_like(m_scr, -jnp.inf)
        acc_scr[...] = jnp.zeros_like(acc_scr)
        q = rope(q_ref[...], cq_ref[...], sq_ref[...])
        for h in range(N_HEADS):
            slab, low = _pair_lanes(h)
            base = 0 if low else HEAD_DIM
            for c in range(2):
                lo = base + c * DIFF_DH
                sel = jnp.logical_and(lane >= lo, lane < lo + DIFF_DH)
                qm_scr[2 * h + c] = jnp.where(sel, q[:, slab], 0.0).astype(BF16)

    def step(diagonal):
        k = rope(k_ref[...], ck_ref[...], sk_ref[...]).astype(BF16)
        vts = _values_and_ones(vt_ref[...], tq)
        if diagonal:
            masked = _key_after_query(tq)
        scores = []
        for h in range(N_HEADS):
            slab, _ = _pair_lanes(h)
            for c in range(2):
                i = 2 * h + c
                s = lax.dot_general(k[:, slab], qm_scr[i], (((1,), (1,)), ((), ())), preferred_element_type=F32)
                if diagonal:
                    s = jnp.where(masked, -jnp.inf, s)
                scores.append(s)
        _softmax_updates(scores, [vts[i // 2] for i in range(2 * N_HEADS)], m_scr, acc_scr)

    @pl.when(kj < qi)
    def _():
        step(False)

    @pl.when(kj == qi)
    def _():
        step(True)
        lp = lam_ref[...]
        lam = (jnp.exp(jnp.sum(lp[0:1] * lp[1:2], axis=-1, keepdims=True))
               - jnp.exp(jnp.sum(lp[2:3] * lp[3:4], axis=-1, keepdims=True)) + lam_init)
        head = lambda h: (_normalized(acc_scr[2 * h], h % 2 == 0) - lam * _normalized(acc_scr[2 * h + 1], h % 2 == 0))
        is_lo = lane < HEAD_DIM
        for p in range(N_HEADS // 2):
            o = jnp.concatenate([head(2 * p), head(2 * p + 1)], axis=0).T
            sq = o * o
            ms_lo = jnp.sum(jnp.where(is_lo, sq, 0.0), axis=-1, keepdims=True)
            ms_hi = jnp.sum(jnp.where(is_lo, 0.0, sq), axis=-1, keepdims=True)
            ms = jnp.where(is_lo, ms_lo, ms_hi) * (1.0 / HEAD_DIM)
            o_ref[:, p * LANES:(p + 1) * LANES] = o * lax.rsqrt(ms + DIFF_LN_EPS) * ln_ref[...] * (1.0 - lam_init)


def _diff_attention(u, vt, cos, sin, rot, lam_p, subln, batch, seq, tq, lam_init):
    nq = seq // tq
    qi, kj = _causal_pairs(nq)
    q_spec, k_spec, vt_spec = _attn_specs(nq, tq, U_DIFF, 0)
    tab_q = pl.BlockSpec((tq, W_MIX), lambda b, t, qi, kj: (qi[t], 0))
    tab_k = pl.BlockSpec((tq, W_MIX), lambda b, t, qi, kj: (kj[t], 0))
    const = lambda r, c: pl.BlockSpec((r, c), lambda b, t, qi, kj: (0, 0))
    stat = lambda: pltpu.VMEM((2 * N_HEADS, 1, tq), F32)
    return pl.pallas_call(
        functools.partial(_diff_kernel, lam_init=lam_init),
        grid_spec=pltpu.PrefetchScalarGridSpec(
            num_scalar_prefetch=2,
            grid=(batch, qi.shape[0]),
            in_specs=[q_spec, k_spec, vt_spec, tab_q, tab_q, tab_k, tab_k,
                      const(W_MIX, W_MIX), const(4, DIFF_DH), const(1, LANES)],
            out_specs=pl.BlockSpec((tq, W_MIX), lambda b, t, qi, kj: (b * nq + qi[t], 0)),
            scratch_shapes=[pltpu.VMEM((2 * N_HEADS, tq, LANES), BF16), stat(),
                            pltpu.VMEM((2 * N_HEADS, LANES, tq), F32)],
        ),
        out_shape=jax.ShapeDtypeStruct((batch * seq, W_MIX), F32),
        compiler_params=_cparams(("parallel", "arbitrary")),
        name="diff_attention",
    )(qi, kj, u, u, vt, cos, sin, cos, sin, rot, lam_p, subln)


def _merge_kernel(x_ref, gate_ref, oa_ref, post_ref, yb_ref, yc_ref, od_ref, gd_ref, lnw_ref, lnb_ref,
                  wbo_ref, wout_ref, o_ref):
    y_a = (oa_ref[...] * lnw_ref[...] + lnb_ref[...] + post_ref[1]) * post_ref[0]
    y_d = od_ref[...] * _silu(gd_ref[...].astype(F32))
    acc = jnp.zeros(x_ref.shape, F32)
    for b, y in enumerate((y_a, yb_ref[...], yc_ref[...], y_d)):
        gate = _sigmoid(gate_ref[:, b * D_MODEL:(b + 1) * D_MODEL].astype(F32))
        acc = acc + gate * _mm(y, wbo_ref[b])
    o_ref[...] = x_ref[...] + _mm(acc, wout_ref[...])


def _merge(x, u, o_a, post, y_b, y_c, o_d, ln_w, ln_b, w_bo, w_out, tm):
    n = x.shape[0]
    tok = lambda c: pl.BlockSpec((tm, c), lambda i: (i, 0))
    return pl.pallas_call(
        _merge_kernel,
        grid=(n // tm,),
        in_specs=[
            tok(D_MODEL),
            pl.BlockSpec((tm, 4 * D_MODEL), lambda i: (i, U_GATE // (4 * D_MODEL))),
            tok(W_MIX),
            pl.BlockSpec((2, tm, W_MIX), lambda i: (0, i, 0)),
            tok(W_MIX), tok(W_MIX), tok(W_MIX),
            pl.BlockSpec((tm, W_MIX), lambda i: (i, (U_GDN + 3 * W_MIX) // W_MIX)),
            pl.BlockSpec((1, W_MIX), lambda i: (0, 0)),
            pl.BlockSpec((1, W_MIX), lambda i: (0, 0)),
            pl.BlockSpec((4, W_MIX, D_MODEL), lambda i: (0, 0, 0)),
            pl.BlockSpec((D_MODEL, D_MODEL), lambda i: (0, 0)),
        ],
        out_specs=tok(D_MODEL),
        out_shape=jax.ShapeDtypeStruct((n, D_MODEL), F32),
        compiler_params=_cparams(("parallel",)),
        name="merge",
    )(x, u, o_a, post, y_b, y_c, o_d, u, ln_w, ln_b, w_bo, w_out)


def _ffn_kernel(x_ref, g_ref, wg_ref, wu_ref, wd_ref, o_ref, h_scr, acc_scr):
    f = pl.program_id(1)

    @pl.when(f == 0)
    def _():
        h_scr[...] = _rms(x_ref[...], g_ref[...], NORM_EPS).astype(BF16)
        acc_scr[...] = jnp.zeros_like(acc_scr)

    h = h_scr[...]
    act = _silu(jnp.dot(h, wg_ref[...], preferred_element_type=F32)) * jnp.dot(h, wu_ref[...], preferred_element_type=F32)
    acc_scr[...] += _mm(act, wd_ref[...])

    @pl.when(f == pl.num_programs(1) - 1)
    def _():
        o_ref[...] = acc_scr[...]


def _ffn(x, g, wg, wu, wd, tm, tf):
    n = x.shape[0]
    ff = wg.shape[1]
    return pl.pallas_call(
        _ffn_kernel,
        grid=(n // tm, ff // tf),
        in_specs=[
            pl.BlockSpec((tm, D_MODEL), lambda i, f: (i, 0)),
            pl.BlockSpec((1, D_MODEL), lambda i, f: (0, 0)),
            pl.BlockSpec((D_MODEL, tf), lambda i, f: (0, f)),
            pl.BlockSpec((D_MODEL, tf), lambda i, f: (0, f)),
            pl.BlockSpec((tf, D_MODEL), lambda i, f: (f, 0)),
        ],
        out_specs=pl.BlockSpec((tm, D_MODEL), lambda i, f: (i, 0)),
        out_shape=jax.ShapeDtypeStruct((n, D_MODEL), F32),
        scratch_shapes=[pltpu.VMEM((tm, D_MODEL), BF16), pltpu.VMEM((tm, D_MODEL), F32)],
        compiler_params=_cparams(("parallel", "arbitrary")),
        name="ffn",
    )(x, g, wg, wu, wd)


def _router_kernel(x_ref, g_ref, router_ref, h_ref, c_ref, rc_ref, rr_ref, cnt_ref):
    t = x_ref.shape[0]
    h = _rms(x_ref[...], g_ref[...], NORM_EPS)
    h_ref[...] = h.astype(BF16)
    logits = _mm_f32(h, router_ref[...])
    lane = lax.broadcasted_iota(jnp.int32, logits.shape, 1).astype(F32)
    lg = jnp.where(lane < N_EXPERTS, logits, -jnp.inf)
    m1 = jnp.max(lg, axis=-1, keepdims=True)
    i1 = jnp.min(jnp.where(lg == m1, lane, float(LANES)), axis=-1, keepdims=True)
    lg2 = jnp.where(lane == i1, -jnp.inf, lg)
    m2 = jnp.max(lg2, axis=-1, keepdims=True)
    i2 = jnp.min(jnp.where(lg2 == m2, lane, float(LANES)), axis=-1, keepdims=True)
    e2 = jnp.exp(m2 - m1)
    c_ref[...] = jnp.where(lane == i1, 1.0 / (1.0 + e2), 0.0) + jnp.where(lane == i2, e2 / (1.0 + e2), 0.0)
    sel = jnp.logical_or(lane == i1, lane == i2)
    sel_f = jnp.where(sel, 1.0, 0.0)
    earlier = _tri_masks(t)[0].astype(BF16)
    rank = jnp.dot(earlier, sel_f.astype(BF16), preferred_element_type=F32)
    rc = jnp.where(sel, rank, -1.0)
    rc_ref[...] = rc
    rr_ref[...] = rc.T[0:BF16_ROWS, :]
    cnt_ref[...] = jnp.broadcast_to(jnp.sum(sel_f, axis=0, keepdims=True), cnt_ref.shape).astype(jnp.int32)


def _router(x, g, router, tm):
    n = x.shape[0]
    return pl.pallas_call(
        _router_kernel,
        grid=(n // tm,),
        in_specs=[
            pl.BlockSpec((tm, D_MODEL), lambda i: (i, 0)),
            pl.BlockSpec((1, D_MODEL), lambda i: (0, 0)),
            pl.BlockSpec((D_MODEL, LANES), lambda i: (0, 0)),
        ],
        out_specs=[
            pl.BlockSpec((tm, D_MODEL), lambda i: (i, 0)),
            pl.BlockSpec((tm, LANES), lambda i: (i, 0)),
            pl.BlockSpec((tm, LANES), lambda i: (i, 0)),
            pl.BlockSpec((BF16_ROWS, tm), lambda i: (0, i)),
            pl.BlockSpec((8, LANES), lambda i: (i, 0)),
        ],
        out_shape=[
            jax.ShapeDtypeStruct((n, D_MODEL), BF16),
            jax.ShapeDtypeStruct((n, LANES), F32),
            jax.ShapeDtypeStruct((n, LANES), F32),
            jax.ShapeDtypeStruct((BF16_ROWS, n), F32),
            jax.ShapeDtypeStruct((n // tm * 8, LANES), jnp.int32),
        ],
        compiler_params=_cparams(("parallel",)),
        name="router",
    )(x, g, router)


def _moe_kernel(cnt_ref, h_ref, c_ref, rc_ref, rr_ref, wg_ref, wu_ref, wd_ref, o_ref, xg_scr, yg_scr, *, rows, parts):
    i = pl.program_id(0)
    e = pl.program_id(1)
    f = pl.program_id(2)
    last_f = f == pl.num_programs(2) - 1
    t = h_ref.shape[0] // parts
    nsub = xg_scr.shape[1] // rows
    cnts = [cnt_ref[(i * parts + p) * N_EXPERTS + e] for p in range(parts)]
    toks = [slice(p * t, (p + 1) * t) for p in range(parts)]
    blocks = [(sb, slice(sb * rows, (sb + 1) * rows)) for sb in range(nsub)]

    @pl.when(jnp.logical_and(e == 0, f == 0))
    def _():
        o_ref[...] = jnp.zeros_like(o_ref)

    @pl.when(f == 0)
    def _():
        for p in range(parts):
            rr = rr_ref[pl.ds(e, 1), toks[p]]
            for sb, rs in blocks:
                @pl.when(cnts[p] > sb * rows)
                def _():
                    slot = (lax.broadcasted_iota(jnp.int32, (rows, t), 0) + sb * rows).astype(F32)
                    gather = jnp.where(rr == slot, 1.0, 0.0).astype(BF16)
                    xg_scr[p, rs, :] = jnp.dot(gather, h_ref[toks[p], :], preferred_element_type=F32).astype(BF16)
                    yg_scr[p, rs, :] = jnp.zeros((rows, D_MODEL), F32)

    for p in range(parts):
        for sb, rs in blocks:
            @pl.when(cnts[p] > sb * rows)
            def _():
                xb = xg_scr[p, rs, :]
                act = (_silu(jnp.dot(xb, wg_ref[0, 0], preferred_element_type=F32))
                       * jnp.dot(xb, wu_ref[0, 0], preferred_element_type=F32))
                yg_scr[p, rs, :] += _mm(act, wd_ref[0, 0])

    @pl.when(last_f)
    def _():
        mine = lax.broadcasted_iota(jnp.int32, (t, LANES), 1) == e
        for p in range(parts):
            rc = jnp.sum(jnp.where(mine, rc_ref[toks[p], :], 0.0), axis=-1, keepdims=True)
            ce = jnp.sum(jnp.where(mine, c_ref[toks[p], :], 0.0), axis=-1, keepdims=True)
            for sb, rs in blocks:
                @pl.when(cnts[p] > sb * rows)
                def _():
                    slot = (lax.broadcasted_iota(jnp.int32, (t, rows), 1) + sb * rows).astype(F32)
                    scatter = jnp.where(rc == slot, 1.0, 0.0).astype(BF16)
                    o_ref[toks[p], :] += ce * jnp.dot(scatter, yg_scr[p, rs, :].astype(BF16),
                                                      preferred_element_type=F32)


def _moe(x, g, router, wg, wu, wd, tm, parts, tf, rows):
    n = x.shape[0]
    ff = wg.shape[2]
    nf = ff // tf
    wg, wu = (w.reshape(N_EXPERTS, D_MODEL, nf, tf).transpose(0, 2, 1, 3) for w in (wg, wu))
    wd = wd.reshape(N_EXPERTS, nf, tf, D_MODEL)
    h, c, rc, rr, cnt = _router(x, g, router, tm)
    cnt = cnt.reshape(n // tm, 8, LANES)[:, 0, :N_EXPERTS].reshape(-1)
    nsub = -(-tm // rows)
    tt = tm * parts
    tok = lambda cols, **kw: pl.BlockSpec((tt, cols), lambda i, e, f, cnt: (i, 0), **kw)
    once = dict(pipeline_mode=pl.Buffered(1))
    return pl.pallas_call(
        functools.partial(_moe_kernel, rows=rows, parts=parts),
        grid_spec=pltpu.PrefetchScalarGridSpec(
            num_scalar_prefetch=1,
            grid=(n // tt, N_EXPERTS, nf),
            in_specs=[
                tok(D_MODEL, **once), tok(LANES), tok(LANES),
                pl.BlockSpec((BF16_ROWS, tt), lambda i, e, f, cnt: (0, i)),
                pl.BlockSpec((1, 1, D_MODEL, tf), lambda i, e, f, cnt: (e, f, 0, 0)),
                pl.BlockSpec((1, 1, D_MODEL, tf), lambda i, e, f, cnt: (e, f, 0, 0)),
                pl.BlockSpec((1, 1, tf, D_MODEL), lambda i, e, f, cnt: (e, f, 0, 0)),
            ],
            out_specs=tok(D_MODEL),
            scratch_shapes=[pltpu.VMEM((parts, nsub * rows, D_MODEL), BF16),
                            pltpu.VMEM((parts, nsub * rows, D_MODEL), F32)],
        ),
        out_shape=jax.ShapeDtypeStruct((n, D_MODEL), F32),
        compiler_params=_cparams(("parallel", "arbitrary", "arbitrary"), vmem_mb=56),
        name="moe",
    )(cnt, h, c, rc, rr, wg, wu, wd)


def _ple_kernel(x_ref, d_ref, p_ref, g_ref, wgate_ref, wproj_ref, fin_ref, o_ref, *, final):
    x = x_ref[...] + d_ref[...]
    h = _rms(x, g_ref[...], NORM_EPS)
    y = x + _sigmoid(_mm(h, wgate_ref[...])) * _mm(p_ref[...], wproj_ref[...])
    o_ref[...] = _rms(y, fin_ref[...], NORM_EPS) if final else y


def _ple(x, delta, p, g, wgate, wproj, fin, tm, final):
    n = x.shape[0]
    tok = lambda c: pl.BlockSpec((tm, c), lambda i: (i, 0))
    const = lambda r, c: pl.BlockSpec((r, c), lambda i: (0, 0))
    return pl.pallas_call(
        functools.partial(_ple_kernel, final=final),
        grid=(n // tm,),
        in_specs=[tok(D_MODEL), tok(D_MODEL), tok(P_DIM), const(1, D_MODEL), const(D_MODEL, D_MODEL),
                  const(P_DIM, D_MODEL), const(1, D_MODEL)],
        out_specs=tok(D_MODEL),
        out_shape=jax.ShapeDtypeStruct((n, D_MODEL), F32),
        compiler_params=_cparams(("parallel",)),
        name="ple",
    )(x, delta, p, g, wgate, wproj, fin)


def _tiles(n, seq):
    tm = min(512, seq)
    tm_big = 1024 if n % 1024 == 0 else tm
    tm_in = 2048 if n % 2048 == 0 else tm_big
    moe_rows = tm_big // 4 + tm_big // 32
    moe_parts = 1
    return dict(tm=tm, tm_big=tm_big, tm_in=tm_in, tn_in=1024, tq=min(512, seq), cblk=min(512, seq), moe_rows=moe_rows,
                moe_parts=moe_parts)


def _rope_tables(seq):
    half = ROPE_DIMS // 2
    inv = ROPE_THETA ** (-jnp.arange(half, dtype=F32) * 2.0 / ROPE_DIMS)
    ang = jnp.arange(seq, dtype=F32)[:, None] * inv[None, :]
    pad = jnp.zeros((seq, DIFF_DH - ROPE_DIMS), F32)
    cos = jnp.concatenate([jnp.cos(ang), jnp.cos(ang), pad + 1.0], axis=-1)
    sin = jnp.concatenate([jnp.sin(ang), jnp.sin(ang), pad], axis=-1)
    d = jnp.arange(W_MIX)
    dd = d % DIFF_DH
    src = jnp.where(dd < half, d + half, d - half)
    sign = jnp.where(dd < half, -1.0, jnp.where(dd < ROPE_DIMS, 1.0, 0.0))
    rot = jnp.zeros((W_MIX, W_MIX), F32).at[src, d].set(sign)
    reps = W_MIX // DIFF_DH
    return jnp.tile(cos, (1, reps)), jnp.tile(sin, (1, reps)), rot.astype(BF16)


def _split_w_in(w):
    a0 = 0
    b0 = a0 + 4 * W_MIX
    c0 = b0 + 3 * W_MIX
    d0 = c0 + 3 * W_MIX + N_HEADS
    g0 = d0 + 4 * W_MIX + 2 * N_HEADS
    d_small = d0 + 3 * W_MIX
    diff_q = w[:, b0:b0 + W_MIX] * (DIFF_DH ** -0.5 * LOG2E)
    fox_q = w[:, c0:c0 + W_MIX] * (HEAD_DIM ** -0.5 * LOG2E)
    main = jnp.concatenate([
        w[:, g0:], w[:, a0:b0], w[:, d0:d_small], w[:, d_small + 2 * N_HEADS:g0],
        diff_q, w[:, b0 + W_MIX:b0 + 2 * W_MIX], fox_q, w[:, c0 + W_MIX:c0 + 2 * W_MIX],
    ], axis=1).astype(BF16)
    small = jnp.concatenate([
        w[:, c0 + 3 * W_MIX:d0], w[:, d_small:d_small + 2 * N_HEADS],
        jnp.zeros((D_MODEL, LANES - 3 * N_HEADS), w.dtype),
    ], axis=1).astype(BF16)
    v_t = jnp.concatenate([w[:, b0 + 2 * W_MIX:c0], w[:, c0 + 2 * W_MIX:c0 + 3 * W_MIX]], axis=1).T.astype(BF16)
    return main, small, v_t


def _small_params(fbias, a_log, dt_bias):
    zeros = jnp.zeros((N_HEADS,), F32)
    bias = jnp.concatenate([fbias, zeros, dt_bias, jnp.zeros((LANES - 3 * N_HEADS,), F32)])
    neg_a = jnp.concatenate([zeros, zeros, -jnp.exp(a_log), jnp.zeros((LANES - 3 * N_HEADS,), F32)])
    return bias.reshape(1, LANES), neg_a.reshape(1, LANES)


def _pad_rows(w, top, total):
    return jnp.concatenate([jnp.zeros((top, w.shape[1]), w.dtype), w,
                            jnp.zeros((total - top - w.shape[0], w.shape[1]), w.dtype)], axis=0)


def kernel(x, p, norm_mix, norm_ffn, norm_ple, w_in, w_bo, w_out, rwkv_mu, rwkv_w0, rwkv_w2, rwkv_a0, rwkv_a2, rwkv_g2, rwkv_kk, rwkv_ka, rwkv_rk, rwkv_ln_w, rwkv_ln_b, diff_lam, diff_subln, fox_fbias, gdn_conv, gdn_a_log, gdn_dt_bias, gdn_norm, ffn_w_gate, ffn_w_up, ffn_w_down, moe_router, moe_w_gate, moe_w_up, moe_w_down, ple_proj, ple_gate, final_norm):
    batch, seq, _ = x.shape
    depth = w_in.shape[0]
    n = batch * seq
    t = _tiles(n, seq)
    tm, tq, cblk = t["tm"], t["tq"], t["cblk"]
    row = lambda v: v.reshape(1, -1).astype(F32)
    cos, sin, rot = _rope_tables(seq)
    xf = x.reshape(n, D_MODEL)
    pf = p.reshape(depth, n, P_DIM)

    for i in range(depth):
        w_main, w_small, w_vt = _split_w_in(w_in[i])
        u, scol, vt = _inproj(xf, row(norm_mix[i]), w_main, w_small, w_vt, t["tm_in"], t["tn_in"])
        bias, neg_a = _small_params(fox_fbias[i], gdn_a_log[i], gdn_dt_bias[i])
        hcol, fox_kb, fox_qb = _small_prep(scol, bias, neg_a, batch, seq, tm)

        scan_in, pc, post = _rwkv_prep(
            u, row(rwkv_mu[i]), row(rwkv_w0[i]), _pad_rows(rwkv_w2[i], 0, LANES), row(rwkv_a0[i]),
            _pad_rows(rwkv_a2[i], LANES // 2, LANES), rwkv_g2[i], row(rwkv_kk[i]), row(rwkv_ka[i]), row(rwkv_rk[i]),
            seq, tm)
        o_a = _rwkv_chunk(scan_in, pc, batch, seq, cblk)

        o_d = _gdn_chunk(_gdn_prep(u, gdn_conv[i].T, seq, tm), hcol, jnp.tile(row(gdn_norm[i]), (1, N_HEADS)),
                         batch, seq, cblk)

        lam_init = 0.8 - 0.6 * math.exp(-0.3 * i)
        y_b = _diff_attention(u, vt, cos, sin, rot, diff_lam[i].astype(F32), jnp.tile(row(diff_subln[i]), (1, 2)),
                              batch, seq, tq, lam_init)
        y_c = _fox_attention(u, vt, fox_qb, fox_kb, batch, seq, tq)

        xf = _merge(xf, u, o_a, post, y_b, y_c, o_d, row(rwkv_ln_w[i]), row(rwkv_ln_b[i]),
                    w_bo[i].astype(BF16), w_out[i].astype(BF16), tm)

        j = i // 2
        if i % 2 == 0:
            delta = _ffn(xf, row(norm_ffn[i]), ffn_w_gate[j].astype(BF16), ffn_w_up[j].astype(BF16),
                         ffn_w_down[j].astype(BF16), tm, ffn_w_gate.shape[2] // 2)
        else:
            router = jnp.concatenate([moe_router[j], jnp.zeros((D_MODEL, LANES - N_EXPERTS), F32)], axis=1)
            delta = _moe(xf, row(norm_ffn[i]), router, moe_w_gate[j].astype(BF16), moe_w_up[j].astype(BF16),
                         moe_w_down[j].astype(BF16), t["tm_big"], t["moe_parts"], moe_w_gate.shape[3] // 2,
                         t["moe_rows"])
        xf = _ple(xf, delta, pf[i], row(norm_ple[i]), ple_gate[i].astype(BF16), ple_proj[i].astype(BF16),
                  row(final_norm), tm, i == depth - 1)
    return xf.reshape(batch, seq, D_MODEL)
```

```python
import functools
import math

import jax
import jax.numpy as jnp
import numpy as np
from jax import lax
from jax.experimental import pallas as pl
from jax.experimental.pallas import tpu as pltpu

F32 = jnp.float32
BF16 = jnp.bfloat16
HIGHEST = lax.Precision.HIGHEST

D_MODEL = 1024
P_DIM = 256
W_MIX = 256
HEAD_DIM = 64
N_HEADS = 4
DIFF_DH = 32
ROPE_THETA = 500000.0
ROPE_DIMS = 8
RWKV_GN_EPS = 64e-5
DIFF_LN_EPS = 1e-5
GDN_CONV = 4
CHUNK = 64
CHUNK_UNROLL = 4
N_EXPERTS = 8
NORM_EPS = 1e-6
L2_EPS = 1e-6
LOG2E = math.log2(math.e)
LANES = 128
BF16_ROWS = 16
assert CHUNK == HEAD_DIM

U_GATE = 0
U_RWKV = 4096
U_GDN = 5120
U_DIFF = 6144
U_FOX = 6656
U_COLS = 7168
SM_FOX, SM_BETA, SM_DEC = 0, 4, 8
FOX_CK, FOX_CQ = 0, 16


def _cparams(semantics, vmem_mb=48):
    return pltpu.CompilerParams(dimension_semantics=semantics, vmem_limit_bytes=vmem_mb * 1024 * 1024)


def _mm(a, b):
    return jnp.dot(a.astype(BF16), b.astype(BF16), preferred_element_type=F32)


def _mm_nt(a, b):
    return lax.dot_general(a.astype(BF16), b.astype(BF16), (((1,), (1,)), ((), ())), preferred_element_type=F32)


def _mm_tn(a, b):
    return lax.dot_general(a.astype(BF16), b.astype(BF16), (((0,), (0,)), ((), ())), preferred_element_type=F32)


def _mm_f32(a, b):
    return jnp.dot(a, b, preferred_element_type=F32, precision=HIGHEST)


def _mm_mask(mask, x):
    hi = x.astype(BF16)
    r1 = x - hi.astype(F32)
    mid = r1.astype(BF16)
    lo = (r1 - mid.astype(F32)).astype(BF16)
    dot = lambda t: jnp.dot(mask, t, preferred_element_type=F32)
    return dot(hi) + dot(mid) + dot(lo)


def _rms(x, g, eps):
    return x * lax.rsqrt(jnp.mean(x * x, axis=-1, keepdims=True) + eps) * g


def _sigmoid(x):
    return 1.0 / (1.0 + jnp.exp(-x))


def _silu(x):
    return x * _sigmoid(x)


def _softplus(x):
    return jnp.maximum(x, 0.0) + jnp.log(1.0 + jnp.exp(-jnp.abs(x)))


def _tri_masks(c):
    ii = lax.broadcasted_iota(jnp.int32, (c, c), 0)
    jj = lax.broadcasted_iota(jnp.int32, (c, c), 1)
    return ii > jj, ii >= jj, ii == jj


def _block_masks():
    ii = lax.broadcasted_iota(jnp.int32, (W_MIX, W_MIX), 0)
    jj = lax.broadcasted_iota(jnp.int32, (W_MIX, W_MIX), 1)
    return (ii // HEAD_DIM) == (jj // HEAD_DIM), ii == jj


def _wide_masks():
    ii = lax.broadcasted_iota(jnp.int32, (CHUNK, W_MIX), 0)
    jj = lax.broadcasted_iota(jnp.int32, (CHUNK, W_MIX), 1) % HEAD_DIM
    return ii > jj, ii >= jj, ii == jj


def _stack_heads(x, same):
    return jnp.where(same, jnp.concatenate([x, x, x, x], axis=0), jnp.zeros((), x.dtype))


def _unit_lower_inverses(ns, eye_w, same):
    rs = [eye_w + n for n in ns]
    ps = list(ns)
    for _ in range(int(math.log2(CHUNK)) - 1):
        ps = [_mm(p, _stack_heads(p.astype(BF16), same)) for p in ps]
        rs = [r + _mm(r, _stack_heads(p.astype(BF16), same)) for r, p in zip(rs, ps)]
    return rs


def _unstack_heads(x):
    return x[0:CHUNK] + x[CHUNK:2 * CHUNK] + x[2 * CHUNK:3 * CHUNK] + x[3 * CHUNK:4 * CHUNK]


def _inproj_kernel(x_ref, g_ref, w_ref, ws_ref, wvt_ref, u_ref, scol_ref, vt_ref, h_scr):
    @pl.when(pl.program_id(1) == 0)
    def _():
        hb = _rms(x_ref[...], g_ref[...], NORM_EPS).astype(BF16)
        h_scr[...] = hb
        scol_ref[...] = jnp.dot(hb, ws_ref[...], preferred_element_type=F32)
        vt_ref[...] = lax.dot_general(wvt_ref[...], hb, (((1,), (1,)), ((), ())),
                                      preferred_element_type=F32).astype(BF16)

    u_ref[...] = jnp.dot(h_scr[...], w_ref[...], preferred_element_type=F32).astype(BF16)


def _inproj(x, g, w, ws, wvt, tm, tn):
    n = x.shape[0]
    return pl.pallas_call(
        _inproj_kernel,
        grid=(n // tm, U_COLS // tn),
        in_specs=[
            pl.BlockSpec((tm, D_MODEL), lambda i, j: (i, 0)),
            pl.BlockSpec((1, D_MODEL), lambda i, j: (0, 0)),
            pl.BlockSpec((D_MODEL, tn), lambda i, j: (0, j)),
            pl.BlockSpec((D_MODEL, LANES), lambda i, j: (0, 0)),
            pl.BlockSpec((2 * W_MIX, D_MODEL), lambda i, j: (0, 0)),
        ],
        out_specs=[
            pl.BlockSpec((tm, tn), lambda i, j: (i, j)),
            pl.BlockSpec((tm, LANES), lambda i, j: (i, 0)),
            pl.BlockSpec((2 * W_MIX, tm), lambda i, j: (0, i)),
        ],
        out_shape=[
            jax.ShapeDtypeStruct((n, U_COLS), BF16),
            jax.ShapeDtypeStruct((n, LANES), F32),
            jax.ShapeDtypeStruct((2 * W_MIX, n), BF16),
        ],
        scratch_shapes=[pltpu.VMEM((tm, D_MODEL), BF16)],
        compiler_params=_cparams(("parallel", "arbitrary"), vmem_mb=56),
        name="inproj",
    )(x, g, w, ws, wvt)


def _lane_placement(base):
    r = lax.broadcasted_iota(jnp.int32, (3 * LANES, LANES), 0)
    m = lax.broadcasted_iota(jnp.int32, (3 * LANES, LANES), 1)
    head, part = r % LANES, r // LANES
    return jnp.logical_and(head < N_HEADS, m == base + 3 * head + part).astype(BF16)


def _split3(x):
    hi = x.astype(BF16)
    r1 = x - hi.astype(F32)
    mid = r1.astype(BF16)
    lo = (r1 - mid.astype(F32)).astype(BF16)
    return jnp.concatenate([hi, mid, lo], axis=1)


def _head_expansion():
    r = lax.broadcasted_iota(jnp.int32, (3 * LANES, 2 * W_MIX), 0) % LANES
    m = lax.broadcasted_iota(jnp.int32, (3 * LANES, 2 * W_MIX), 1)
    src = jnp.where(m < W_MIX, SM_BETA + m // HEAD_DIM, SM_DEC + (m - W_MIX) // HEAD_DIM)
    return (r == src).astype(BF16)


def _small_prep_kernel(scol_ref, bias_ref, nega_ref, ogb_ref, okb_ref, oqb_ref, carry_scr):
    @pl.when(pl.program_id(1) == 0)
    def _():
        carry_scr[...] = jnp.zeros_like(carry_scr)

    tm = scol_ref.shape[0]
    lower = _tri_masks(tm)[1].astype(BF16)
    z = scol_ref[...] + bias_ref[...]
    lane = lax.broadcasted_iota(jnp.int32, z.shape, 1)
    is_f = lane < SM_BETA
    is_b = jnp.logical_and(lane >= SM_BETA, lane < SM_DEC)
    is_d = jnp.logical_and(lane >= SM_DEC, lane < SM_DEC + N_HEADS)
    logf = jnp.where(is_f, jnp.minimum(z, 0.0) - jnp.log(1.0 + jnp.exp(-jnp.abs(z))), 0.0)
    gdn = jnp.where(is_b, _sigmoid(z), jnp.where(is_d, nega_ref[...] * _softplus(z), 0.0))
    ogb_ref[...] = jnp.dot(_split3(gdn), _head_expansion(), preferred_element_type=F32)
    cum = _mm_mask(lower, logf) + carry_scr[...]
    carry_scr[...] = cum[tm - 1:tm, :]
    parts = _split3(cum * LOG2E)
    ones_lanes = jnp.logical_and(lane >= FOX_CQ, lane < FOX_CQ + 3 * N_HEADS)
    okb_ref[...] = (jnp.dot(parts, _lane_placement(FOX_CK), preferred_element_type=F32)
                    + jnp.where(ones_lanes, 1.0, 0.0)).astype(BF16)
    oqb_ref[...] = jnp.dot(parts, _lane_placement(FOX_CQ), preferred_element_type=F32).astype(BF16)


def _small_prep(scol, bias, neg_a, batch, seq, tm):
    n = batch * seq
    nt = seq // tm
    tok = lambda: pl.BlockSpec((tm, LANES), lambda b, j: (b * nt + j, 0))
    const = lambda: pl.BlockSpec((1, LANES), lambda b, j: (0, 0))
    return pl.pallas_call(
        _small_prep_kernel,
        grid=(batch, nt),
        in_specs=[tok(), const(), const()],
        out_specs=[pl.BlockSpec((tm, 2 * W_MIX), lambda b, j: (b * nt + j, 0)), tok(), tok()],
        out_shape=[jax.ShapeDtypeStruct((n, 2 * W_MIX), F32), jax.ShapeDtypeStruct((n, LANES), BF16),
                   jax.ShapeDtypeStruct((n, LANES), BF16)],
        scratch_shapes=[pltpu.VMEM((1, LANES), F32)],
        compiler_params=_cparams(("parallel", "arbitrary")),
        name="small_prep",
    )(scol, bias, neg_a)


def _rwkv_prep_kernel(u_ref, up_ref, mu_ref, w0_ref, w2_ref, a0_ref, a2_ref, g2_ref, kk_ref, ka_ref, rk_ref,
                      scan_ref, pc_ref, post_ref, *, tiles_per_seq):
    tm = u_ref.shape[0]
    u = u_ref[...].astype(F32)
    prev = up_ref[...].astype(F32)[BF16_ROWS - 1:BF16_ROWS, :]
    prev = jnp.where(pl.program_id(0) % tiles_per_seq == 0, 0.0, prev)
    rows = lax.broadcasted_iota(jnp.int32, (tm, 1), 0)
    u_prev = jnp.where(rows == 0, prev, pltpu.roll(u, 1, 0))
    xm = u + (u_prev - u) * mu_ref[...]
    r = xm[:, 0:W_MIX]
    k = xm[:, W_MIX:2 * W_MIX]
    v = xm[:, 2 * W_MIX:3 * W_MIX]
    x_lora = xm[:, 3 * W_MIX:3 * W_MIX + LANES]
    xg = xm[:, 3 * W_MIX + LANES:]
    logw = -_softplus(-(w0_ref[...] + _mm(jnp.tanh(x_lora), w2_ref[...]))) - 0.5
    log_decay = -jnp.exp(logw)
    a = _sigmoid(a0_ref[...] + _mm(x_lora, a2_ref[...]))
    g = _mm(_sigmoid(xg), g2_ref[...])
    same = _block_masks()[0].astype(F32)
    kk_raw = k * kk_ref[...]
    kk = kk_raw * lax.rsqrt(_mm_f32(kk_raw * kk_raw, same) + L2_EPS)
    k2 = k * (1.0 + (a - 1.0) * ka_ref[...])
    bonus = _mm_f32(r * k2 * rk_ref[...], same) * v
    ti = lax.broadcasted_iota(jnp.int32, (tm, tm), 0)
    tj = lax.broadcasted_iota(jnp.int32, (tm, tm), 1)
    in_chunk = jnp.logical_and(ti // CHUNK == tj // CHUNK, ti >= tj).astype(BF16)
    ci = lax.broadcasted_iota(jnp.int32, (tm // CHUNK, tm), 0)
    cj = lax.broadcasted_iota(jnp.int32, (tm // CHUNK, tm), 1)
    cum = _mm_mask(in_chunk, log_decay)
    cum_end = _mm_mask((ci == cj // CHUNK).astype(BF16), log_decay)
    inv = jnp.exp(-cum)
    scan_ref[0] = (-kk * jnp.exp(cum - log_decay)).astype(BF16)
    scan_ref[1] = (kk * a * inv).astype(BF16)
    scan_ref[2] = (k2 * inv).astype(BF16)
    scan_ref[3] = (r * jnp.exp(cum)).astype(BF16)
    scan_ref[4] = v.astype(BF16)
    pc_ref[...] = jnp.exp(cum_end)
    post_ref[0] = g
    post_ref[1] = bonus


def _rwkv_prep(u, mu, w0, w2p, a0, a2p, g2, k_k, k_a, r_k, seq, tm):
    n = u.shape[0]
    ublk = U_RWKV // D_MODEL
    row = lambda c: pl.BlockSpec((1, c), lambda i: (0, 0))
    mat = lambda r: pl.BlockSpec((r, W_MIX), lambda i: (0, 0))
    return pl.pallas_call(
        functools.partial(_rwkv_prep_kernel, tiles_per_seq=seq // tm),
        grid=(n // tm,),
        in_specs=[
            pl.BlockSpec((tm, D_MODEL), lambda i: (i, ublk)),
            pl.BlockSpec((BF16_ROWS, D_MODEL), lambda i: (jnp.maximum(i * (tm // BF16_ROWS) - 1, 0), ublk)),
            row(D_MODEL), row(W_MIX), mat(LANES), row(W_MIX), mat(LANES), mat(LANES), row(W_MIX), row(W_MIX), row(W_MIX),
        ],
        out_specs=[
            pl.BlockSpec((5, tm, W_MIX), lambda i: (0, i, 0)),
            pl.BlockSpec((tm // CHUNK, W_MIX), lambda i: (i, 0)),
            pl.BlockSpec((2, tm, W_MIX), lambda i: (0, i, 0)),
        ],
        out_shape=[jax.ShapeDtypeStruct((5, n, W_MIX), BF16), jax.ShapeDtypeStruct((n // CHUNK, W_MIX), F32),
                   jax.ShapeDtypeStruct((2, n, W_MIX), F32)],
        compiler_params=_cparams(("parallel",)),
        name="rwkv_prep",
    )(u, u, mu, w0, w2p, a0, a2p, g2, k_k, k_a, r_k)


def _rwkv_chunk_kernel(x_ref, pc_ref, o_ref, s_scr, *, nchunk, nbatch):
    @pl.when(pl.program_id(0) == 0)
    def _():
        s_scr[...] = jnp.zeros_like(s_scr)

    same, diag = _block_masks()
    eye = diag.astype(F32)
    strict_w, incl_w, diag_w = _wide_masks()
    eye_w = diag_w.astype(F32)
    stack = lambda x: _stack_heads(x.astype(BF16), same)

    def chunks(i, carry):
        items = [(b, i * CHUNK_UNROLL + j) for j in range(CHUNK_UNROLL) for b in range(nbatch)]
        each = lambda f, *lists: [f(*args) for args in zip(*lists)]
        sls = [pl.ds(pl.multiple_of(ci * CHUNK, CHUNK), CHUNK) for _, ci in items]
        a, bb, k, r, v = ([x_ref[i, b, sl, :] for (b, _), sl in zip(items, sls)] for i in range(5))
        pc = [pc_ref[b, pl.ds(ci, 1), :] for b, ci in items]
        a_s, b_s, k_s, v_s = (each(stack, x) for x in (a, bb, k, v))
        m_ab = each(lambda x, y: jnp.where(strict_w, _mm_nt(x, y), 0.0), a, b_s)
        m_ak = each(lambda x, y: jnp.where(strict_w, _mm_nt(x, y), 0.0), a, k_s)
        n_rb = each(lambda x, y: jnp.where(incl_w, _mm_nt(x, y), 0.0), r, b_s)
        n_rk = each(lambda x, y: jnp.where(incl_w, _mm_nt(x, y), 0.0), r, k_s)
        t_inv = _unit_lower_inverses(m_ab, eye_w, same)
        a2 = each(_mm, t_inv, a_s)
        u0 = each(lambda t, m, x: _mm(t, stack(_mm(m, x))), t_inv, m_ak, v_s)
        r2 = each(lambda x, n, y: x.astype(F32) + _mm(n, stack(y)), r, n_rb, a2)
        o0 = each(lambda n1, u, n2, x: _mm(n1, stack(u)) + _mm(n2, x), n_rb, u0, n_rk, v_s)
        b_end = each(lambda x, p: x.astype(F32) * p, bb, pc)
        k_end = each(lambda x, p: x.astype(F32) * p, k, pc)
        g_mat = each(lambda p, x, y: eye * p + jnp.where(same, _mm_tn(x, y), 0.0), pc, a2, b_end)
        s0 = each(lambda u, x, y, z: jnp.where(same, _mm_tn(u, x) + _mm_tn(y, z), 0.0), u0, b_end, v, k_end)
        for n, ((b, _), sl) in enumerate(zip(items, sls)):
            s = s_scr[b]
            o = _stack_heads(_mm_nt(r2[n], s) + o0[n], same)
            s_scr[b] = _mm(s, g_mat[n]) + s0[n]
            mean = jnp.sum(o, axis=-1, keepdims=True) * (1.0 / HEAD_DIM)
            cen = jnp.where(same, o - mean, 0.0)
            var = jnp.sum(cen * cen, axis=-1, keepdims=True) * (1.0 / HEAD_DIM)
            o_ref[b, sl, :] = _unstack_heads(cen * lax.rsqrt(var + RWKV_GN_EPS))
        return carry

    lax.fori_loop(0, nchunk // CHUNK_UNROLL, chunks, 0)


def _rwkv_chunk(xs, pc, batch, seq, cblk):
    xs = xs.reshape(5, batch, seq, W_MIX)
    pc = pc.reshape(batch, seq // CHUNK, W_MIX)
    out = pl.pallas_call(
        functools.partial(_rwkv_chunk_kernel, nchunk=cblk // CHUNK, nbatch=batch),
        grid=(seq // cblk,),
        in_specs=[
            pl.BlockSpec((5, batch, cblk, W_MIX), lambda j: (0, 0, j, 0)),
            pl.BlockSpec((batch, cblk // CHUNK, W_MIX), lambda j: (0, j, 0)),
        ],
        out_specs=pl.BlockSpec((batch, cblk, W_MIX), lambda j: (0, j, 0)),
        out_shape=jax.ShapeDtypeStruct((batch, seq, W_MIX), F32),
        scratch_shapes=[pltpu.VMEM((batch, W_MIX, W_MIX), F32)],
        compiler_params=_cparams(("arbitrary",)),
        name="rwkv_chunk",
    )(xs, pc)
    return out.reshape(batch * seq, W_MIX)


def _gdn_prep_kernel(u_ref, up_ref, cw_ref, o_ref, ext_scr, *, tiles_per_seq):
    tm = u_ref.shape[0]
    c3 = 3 * W_MIX
    prev = up_ref[...].astype(F32)[:, :c3]
    ext_scr[0:BF16_ROWS, :] = jnp.where(pl.program_id(0) % tiles_per_seq == 0, 0.0, prev)
    ext_scr[BF16_ROWS:, :] = u_ref[...].astype(F32)[:, :c3]
    y = jnp.zeros((tm, c3), F32)
    for j in range(GDN_CONV):
        y = y + ext_scr[pl.ds(BF16_ROWS - (GDN_CONV - 1) + j, tm), :] * cw_ref[j:j + 1, :]
    y = _silu(y)
    same = _block_masks()[0].astype(F32)
    q = y[:, 0:W_MIX]
    k = y[:, W_MIX:2 * W_MIX]
    o_ref[0] = (q * lax.rsqrt(_mm_f32(q * q, same) + L2_EPS) * (HEAD_DIM ** -0.5)).astype(BF16)
    o_ref[1] = (k * lax.rsqrt(_mm_f32(k * k, same) + L2_EPS)).astype(BF16)
    o_ref[2] = y[:, 2 * W_MIX:].astype(BF16)


def _gdn_prep(u, conv_w, seq, tm):
    n = u.shape[0]
    ublk = U_GDN // D_MODEL
    return pl.pallas_call(
        functools.partial(_gdn_prep_kernel, tiles_per_seq=seq // tm),
        grid=(n // tm,),
        in_specs=[
            pl.BlockSpec((tm, D_MODEL), lambda i: (i, ublk)),
            pl.BlockSpec((BF16_ROWS, D_MODEL), lambda i: (jnp.maximum(i * (tm // BF16_ROWS) - 1, 0), ublk)),
            pl.BlockSpec((GDN_CONV, 3 * W_MIX), lambda i: (0, 0)),
        ],
        out_specs=pl.BlockSpec((3, tm, W_MIX), lambda i: (0, i, 0)),
        out_shape=jax.ShapeDtypeStruct((3, n, W_MIX), BF16),
        scratch_shapes=[pltpu.VMEM((tm + BF16_ROWS, 3 * W_MIX), F32)],
        compiler_params=_cparams(("parallel",)),
        name="gdn_prep",
    )(u, u, conv_w)


def _gdn_chunk_kernel(x_ref, gb_ref, nw_ref, o_ref, s_scr, *, nchunk, nbatch):
    @pl.when(pl.program_id(0) == 0)
    def _():
        s_scr[...] = jnp.zeros_like(s_scr)

    same, diag = _block_masks()
    eye = diag.astype(F32)
    strict_w, incl_w, diag_w = _wide_masks()
    eye_w = diag_w.astype(F32)
    lower = _tri_masks(CHUNK)[1].astype(BF16)
    stack = lambda x: _stack_heads(x.astype(BF16), same)

    def chunks(i, carry):
        items = [(b, i * CHUNK_UNROLL + j) for j in range(CHUNK_UNROLL) for b in range(nbatch)]
        each = lambda f, *lists: [f(*args) for args in zip(*lists)]
        sls = [pl.ds(pl.multiple_of(ci * CHUNK, CHUNK), CHUNK) for _, ci in items]
        q, k, v = ([x_ref[i, b, sl, :] for (b, _), sl in zip(items, sls)] for i in range(3))
        beta = [gb_ref[b, sl, 0:W_MIX] for (b, _), sl in zip(items, sls)]
        g = [gb_ref[b, sl, W_MIX:2 * W_MIX] for (b, _), sl in zip(items, sls)]
        k_s = each(stack, k)
        gam = [_mm_mask(lower, x) for x in g]
        gam_end = [x[CHUNK - 1:CHUNK, :] for x in gam]
        gdiff = [_mm_mask(lower, jnp.where(strict_w, x, 0.0)) for x in g]
        decay = [jnp.exp(jnp.where(incl_w, x, -jnp.inf)) for x in gdiff]
        a_mat = each(lambda bt, d, x, y: jnp.where(strict_w, bt * d * _mm_nt(x, y), 0.0), beta, decay, k, k_s)
        t_inv = _unit_lower_inverses([-a for a in a_mat], eye_w, same)
        e_gam = [jnp.exp(x) for x in gam]
        u0 = each(lambda t, bt, x: _mm(t, stack(bt * x.astype(F32))), t_inv, beta, v)
        wm = each(lambda t, bt, e, x: _mm(t, stack((bt * e) * x.astype(F32))), t_inv, beta, e_gam, k)
        qk = each(lambda x, y, d: _mm_nt(x, y) * d, q, k_s, decay)
        q2 = each(lambda e, x, a, w: e * x.astype(F32) - _mm(a, stack(w)), e_gam, q, qk, wm)
        o0 = each(lambda a, u: _mm(a, stack(u)), qk, u0)
        k_end = each(lambda x, ge, ga: x.astype(F32) * jnp.exp(ge - ga), k, gam_end, gam)
        g_mat = each(lambda ge, x, w: eye * jnp.exp(ge) - jnp.where(same, _mm_tn(x, w), 0.0), gam_end, k_end, wm)
        s0 = each(lambda x, u: jnp.where(same, _mm_tn(x, u), 0.0), k_end, u0)
        for n, ((b, _), sl) in enumerate(zip(items, sls)):
            s = s_scr[b]
            o = _stack_heads(_mm(q2[n], s) + o0[n], same)
            s_scr[b] = _mm(g_mat[n], s) + s0[n]
            ms = jnp.sum(o * o, axis=-1, keepdims=True) * (1.0 / HEAD_DIM)
            o_ref[b, sl, :] = _unstack_heads(o * lax.rsqrt(ms + NORM_EPS)) * nw_ref[...]
        return carry

    lax.fori_loop(0, nchunk // CHUNK_UNROLL, chunks, 0)


def _gdn_chunk(xs, gb, norm_w, batch, seq, cblk):
    xs = xs.reshape(3, batch, seq, W_MIX)
    gb = gb.reshape(batch, seq, 2 * W_MIX)
    out = pl.pallas_call(
        functools.partial(_gdn_chunk_kernel, nchunk=cblk // CHUNK, nbatch=batch),
        grid=(seq // cblk,),
        in_specs=[
            pl.BlockSpec((3, batch, cblk, W_MIX), lambda j: (0, 0, j, 0)),
            pl.BlockSpec((batch, cblk, 2 * W_MIX), lambda j: (0, j, 0)),
            pl.BlockSpec((1, W_MIX), lambda j: (0, 0)),
        ],
        out_specs=pl.BlockSpec((batch, cblk, W_MIX), lambda j: (0, j, 0)),
        out_shape=jax.ShapeDtypeStruct((batch, seq, W_MIX), F32),
        scratch_shapes=[pltpu.VMEM((batch, W_MIX, W_MIX), F32)],
        compiler_params=_cparams(("arbitrary",)),
        name="gdn_chunk",
    )(xs, gb, norm_w)
    return out.reshape(batch * seq, W_MIX)


def _causal_pairs(nq):
    pairs = [(i, j) for i in range(nq) for j in range(i + 1)]
    return jnp.asarray(np.array([p[0] for p in pairs], np.int32)), jnp.asarray(np.array([p[1] for p in pairs], np.int32))


def _softmax_updates(scores, vt_ones, m_scr, acc_scr):
    probs = []
    for i, s in enumerate(scores):
        m_old = m_scr[i]
        m_new = jnp.maximum(m_old, jnp.max(s, axis=0, keepdims=True))
        m_scr[i] = m_new
        probs.append((jnp.exp2(m_old - m_new), jnp.exp2((s - m_new).astype(BF16))))
    for i, (alpha, p) in enumerate(probs):
        acc_scr[i] = alpha * acc_scr[i] + jnp.dot(vt_ones[i], p, preferred_element_type=F32)


def _pair_lanes(h):
    p = h // 2
    return slice(p * LANES, (p + 1) * LANES), h % 2 == 0


def _values_and_ones(vt, tk):
    row = lax.broadcasted_iota(jnp.int32, (LANES, tk), 0)
    out = []
    for h in range(N_HEADS):
        slab, low = _pair_lanes(h)
        mine = row < HEAD_DIM if low else row >= HEAD_DIM
        out.append(jnp.where(mine, vt[slab, :], jnp.ones((), vt.dtype)))
    return out


def _normalized(acc, low):
    if low:
        return acc[0:HEAD_DIM] / acc[HEAD_DIM:HEAD_DIM + 1]
    return acc[HEAD_DIM:] / acc[0:1]


def _key_after_query(tq):
    return lax.broadcasted_iota(jnp.int32, (tq, tq), 0) > lax.broadcasted_iota(jnp.int32, (tq, tq), 1)


def _fox_kernel(qi_ref, kj_ref, q_ref, k_ref, vt_ref, qb_ref, kb_ref, o_ref, qm_scr, m_scr, acc_scr):
    t = pl.program_id(1)
    qi = qi_ref[t]
    kj = kj_ref[t]
    tq = q_ref.shape[0]

    @pl.when(kj == 0)
    def _():
        m_scr[...] = jnp.full_like(m_scr, -jnp.inf)
        acc_scr[...] = jnp.zeros_like(acc_scr)
        q = q_ref[...]
        qb = qb_ref[...]
        lane = lax.broadcasted_iota(jnp.int32, (tq, LANES), 1)
        zero = jnp.zeros((), BF16)
        for h in range(N_HEADS):
            slab, low = _pair_lanes(h)
            mine = lane < HEAD_DIM if low else lane >= HEAD_DIM
            ck_lanes = jnp.logical_and(lane >= FOX_CK + 3 * h, lane < FOX_CK + 3 * h + 3)
            cq_lanes = jnp.logical_and(lane >= FOX_CQ + 3 * h, lane < FOX_CQ + 3 * h + 3)
            bias = jnp.where(ck_lanes, -jnp.ones((), BF16), jnp.where(cq_lanes, qb, zero))
            qm_scr[h] = jnp.concatenate([jnp.where(mine, q[:, slab], zero), bias], axis=1)

    def step(diagonal):
        k = k_ref[...]
        kb = kb_ref[...]
        vts = _values_and_ones(vt_ref[...], tq)
        if diagonal:
            masked = _key_after_query(tq)
        keys = [jnp.concatenate([k[:, p * LANES:(p + 1) * LANES], kb], axis=1) for p in range(N_HEADS // 2)]
        scores = []
        for h in range(N_HEADS):
            s = lax.dot_general(keys[h // 2], qm_scr[h], (((1,), (1,)), ((), ())), preferred_element_type=F32)
            if diagonal:
                s = jnp.where(masked, -jnp.inf, s)
            scores.append(s)
        _softmax_updates(scores, vts, m_scr, acc_scr)

    @pl.when(kj < qi)
    def _():
        step(False)

    @pl.when(kj == qi)
    def _():
        step(True)
        for p in range(N_HEADS // 2):
            pair = jnp.concatenate([_normalized(acc_scr[2 * p], True), _normalized(acc_scr[2 * p + 1], False)], axis=0)
            o_ref[:, p * LANES:(p + 1) * LANES] = pair.T


def _attn_specs(nq, tq, ucol, vt_rows):
    cb = ucol // W_MIX
    q_spec = pl.BlockSpec((tq, W_MIX), lambda b, t, qi, kj: (b * nq + qi[t], cb))
    k_spec = pl.BlockSpec((tq, W_MIX), lambda b, t, qi, kj: (b * nq + kj[t], cb + 1))
    vt_spec = pl.BlockSpec((W_MIX, tq), lambda b, t, qi, kj: (vt_rows // W_MIX, b * nq + kj[t]))
    return q_spec, k_spec, vt_spec


def _fox_attention(u, vt, qb, kb, batch, seq, tq):
    nq = seq // tq
    qi, kj = _causal_pairs(nq)
    q_spec, k_spec, vt_spec = _attn_specs(nq, tq, U_FOX, W_MIX)
    stat = lambda: pltpu.VMEM((N_HEADS, 1, tq), F32)
    return pl.pallas_call(
        _fox_kernel,
        grid_spec=pltpu.PrefetchScalarGridSpec(
            num_scalar_prefetch=2,
            grid=(batch, qi.shape[0]),
            in_specs=[
                q_spec, k_spec, vt_spec,
                pl.BlockSpec((tq, LANES), lambda b, t, qi, kj: (b * nq + qi[t], 0)),
                pl.BlockSpec((tq, LANES), lambda b, t, qi, kj: (b * nq + kj[t], 0)),
            ],
            out_specs=pl.BlockSpec((tq, W_MIX), lambda b, t, qi, kj: (b * nq + qi[t], 0)),
            scratch_shapes=[pltpu.VMEM((N_HEADS, tq, 2 * LANES), BF16), stat(),
                            pltpu.VMEM((N_HEADS, LANES, tq), F32)],
        ),
        out_shape=jax.ShapeDtypeStruct((batch * seq, W_MIX), F32),
        compiler_params=_cparams(("parallel", "arbitrary")),
        name="fox_attention",
    )(qi, kj, u, u, vt, qb, kb)


def _diff_kernel(qi_ref, kj_ref, q_ref, k_ref, vt_ref, cq_ref, sq_ref, ck_ref, sk_ref, rot_ref, lam_ref, ln_ref, o_ref,
                 qm_scr, m_scr, acc_scr, *, lam_init):
    t = pl.program_id(1)
    qi = qi_ref[t]
    kj = kj_ref[t]
    tq = q_ref.shape[0]
    lane = lax.broadcasted_iota(jnp.int32, (tq, LANES), 1)

    def rope(x, cos, sin):
        return x.astype(F32) * cos + jnp.dot(x, rot_ref[...], preferred_element_type=F32) * sin

    @pl.when(kj == 0)
    def _():
        m_scr[...] = jnp.full_like(m_scr, -jnp.inf)
        acc_scr[...] = jnp.zeros_like(acc_scr)
        q = rope(q_ref[...], cq_ref[...], sq_ref[...])
        for h in range(N_HEADS):
            slab, low = _pair_lanes(h)
            base = 0 if low else HEAD_DIM
            for c in range(2):
                lo = base + c * DIFF_DH
                sel = jnp.logical_and(lane >= lo, lane < lo + DIFF_DH)
                qm_scr[2 * h + c] = jnp.where(sel, q[:, slab], 0.0).astype(BF16)

    def step(diagonal):
        k = rope(k_ref[...], ck_ref[...], sk_ref[...]).astype(BF16)
        vts = _values_and_ones(vt_ref[...], tq)
        if diagonal:
            masked = _key_after_query(tq)
        scores = []
        for h in range(N_HEADS):
            slab, _ = _pair_lanes(h)
            for c in range(2):
                i = 2 * h + c
                s = lax.dot_general(k[:, slab], qm_scr[i], (((1,), (1,)), ((), ())), preferred_element_type=F32)
                if diagonal:
                    s = jnp.where(masked, -jnp.inf, s)
                scores.append(s)
        _softmax_updates(scores, [vts[i // 2] for i in range(2 * N_HEADS)], m_scr, acc_scr)

    @pl.when(kj < qi)
    def _():
        step(False)

    @pl.when(kj == qi)
    def _():
        step(True)
        lp = lam_ref[...]
        lam = (jnp.exp(jnp.sum(lp[0:1] * lp[1:2], axis=-1, keepdims=True))
               - jnp.exp(jnp.sum(lp[2:3] * lp[3:4], axis=-1, keepdims=True)) + lam_init)
        head = lambda h: (_normalized(acc_scr[2 * h], h % 2 == 0) - lam * _normalized(acc_scr[2 * h + 1], h % 2 == 0))
        is_lo = lane < HEAD_DIM
        for p in range(N_HEADS // 2):
            o = jnp.concatenate([head(2 * p), head(2 * p + 1)], axis=0).T
            sq = o * o
            ms_lo = jnp.sum(jnp.where(is_lo, sq, 0.0), axis=-1, keepdims=True)
            ms_hi = jnp.sum(jnp.where(is_lo, 0.0, sq), axis=-1, keepdims=True)
            ms = jnp.where(is_lo, ms_lo, ms_hi) * (1.0 / HEAD_DIM)
            o_ref[:, p * LANES:(p + 1) * LANES] = o * lax.rsqrt(ms + DIFF_LN_EPS) * ln_ref[...] * (1.0 - lam_init)


def _diff_attention(u, vt, cos, sin, rot, lam_p, subln, batch, seq, tq, lam_init):
    nq = seq // tq
    qi, kj = _causal_pairs(nq)
    q_spec, k_spec, vt_spec = _attn_specs(nq, tq, U_DIFF, 0)
    tab_q = pl.BlockSpec((tq, W_MIX), lambda b, t, qi, kj: (qi[t], 0))
    tab_k = pl.BlockSpec((tq, W_MIX), lambda b, t, qi, kj: (kj[t], 0))
    const = lambda r, c: pl.BlockSpec((r, c), lambda b, t, qi, kj: (0, 0))
    stat = lambda: pltpu.VMEM((2 * N_HEADS, 1, tq), F32)
    return pl.pallas_call(
        functools.partial(_diff_kernel, lam_init=lam_init),
        grid_spec=pltpu.PrefetchScalarGridSpec(
            num_scalar_prefetch=2,
            grid=(batch, qi.shape[0]),
            in_specs=[q_spec, k_spec, vt_spec, tab_q, tab_q, tab_k, tab_k,
                      const(W_MIX, W_MIX), const(4, DIFF_DH), const(1, LANES)],
            out_specs=pl.BlockSpec((tq, W_MIX), lambda b, t, qi, kj: (b * nq + qi[t], 0)),
            scratch_shapes=[pltpu.VMEM((2 * N_HEADS, tq, LANES), BF16), stat(),
                            pltpu.VMEM((2 * N_HEADS, LANES, tq), F32)],
        ),
        out_shape=jax.ShapeDtypeStruct((batch * seq, W_MIX), F32),
        compiler_params=_cparams(("parallel", "arbitrary")),
        name="diff_attention",
    )(qi, kj, u, u, vt, cos, sin, cos, sin, rot, lam_p, subln)


def _merge_kernel(x_ref, gate_ref, oa_ref, post_ref, yb_ref, yc_ref, od_ref, gd_ref, lnw_ref, lnb_ref,
                  wbo_ref, wout_ref, o_ref):
    y_a = (oa_ref[...] * lnw_ref[...] + lnb_ref[...] + post_ref[1]) * post_ref[0]
    y_d = od_ref[...] * _silu(gd_ref[...].astype(F32))
    acc = jnp.zeros(x_ref.shape, F32)
    for b, y in enumerate((y_a, yb_ref[...], yc_ref[...], y_d)):
        gate = _sigmoid(gate_ref[:, b * D_MODEL:(b + 1) * D_MODEL].astype(F32))
        acc = acc + gate * _mm(y, wbo_ref[b])
    o_ref[...] = x_ref[...] + _mm(acc, wout_ref[...])


def _merge(x, u, o_a, post, y_b, y_c, o_d, ln_w, ln_b, w_bo, w_out, tm):
    n = x.shape[0]
    tok = lambda c: pl.BlockSpec((tm, c), lambda i: (i, 0))
    return pl.pallas_call(
        _merge_kernel,
        grid=(n // tm,),
        in_specs=[
            tok(D_MODEL),
            pl.BlockSpec((tm, 4 * D_MODEL), lambda i: (i, U_GATE // (4 * D_MODEL))),
            tok(W_MIX),
            pl.BlockSpec((2, tm, W_MIX), lambda i: (0, i, 0)),
            tok(W_MIX), tok(W_MIX), tok(W_MIX),
            pl.BlockSpec((tm, W_MIX), lambda i: (i, (U_GDN + 3 * W_MIX) // W_MIX)),
            pl.BlockSpec((1, W_MIX), lambda i: (0, 0)),
            pl.BlockSpec((1, W_MIX), lambda i: (0, 0)),
            pl.BlockSpec((4, W_MIX, D_MODEL), lambda i: (0, 0, 0)),
            pl.BlockSpec((D_MODEL, D_MODEL), lambda i: (0, 0)),
        ],
        out_specs=tok(D_MODEL),
        out_shape=jax.ShapeDtypeStruct((n, D_MODEL), F32),
        compiler_params=_cparams(("parallel",)),
        name="merge",
    )(x, u, o_a, post, y_b, y_c, o_d, u, ln_w, ln_b, w_bo, w_out)


def _ffn_kernel(x_ref, g_ref, wg_ref, wu_ref, wd_ref, o_ref, h_scr, acc_scr):
    f = pl.program_id(1)

    @pl.when(f == 0)
    def _():
        h_scr[...] = _rms(x_ref[...], g_ref[...], NORM_EPS).astype(BF16)
        acc_scr[...] = jnp.zeros_like(acc_scr)

    h = h_scr[...]
    act = _silu(jnp.dot(h, wg_ref[...], preferred_element_type=F32)) * jnp.dot(h, wu_ref[...], preferred_element_type=F32)
    acc_scr[...] += _mm(act, wd_ref[...])

    @pl.when(f == pl.num_programs(1) - 1)
    def _():
        o_ref[...] = acc_scr[...]


def _ffn(x, g, wg, wu, wd, tm, tf):
    n = x.shape[0]
    ff = wg.shape[1]
    return pl.pallas_call(
        _ffn_kernel,
        grid=(n // tm, ff // tf),
        in_specs=[
            pl.BlockSpec((tm, D_MODEL), lambda i, f: (i, 0)),
            pl.BlockSpec((1, D_MODEL), lambda i, f: (0, 0)),
            pl.BlockSpec((D_MODEL, tf), lambda i, f: (0, f)),
            pl.BlockSpec((D_MODEL, tf), lambda i, f: (0, f)),
            pl.BlockSpec((tf, D_MODEL), lambda i, f: (f, 0)),
        ],
        out_specs=pl.BlockSpec((tm, D_MODEL), lambda i, f: (i, 0)),
        out_shape=jax.ShapeDtypeStruct((n, D_MODEL), F32),
        scratch_shapes=[pltpu.VMEM((tm, D_MODEL), BF16), pltpu.VMEM((tm, D_MODEL), F32)],
        compiler_params=_cparams(("parallel", "arbitrary")),
        name="ffn",
    )(x, g, wg, wu, wd)


def _router_kernel(x_ref, g_ref, router_ref, h_ref, c_ref, rc_ref, rr_ref, cnt_ref):
    t = x_ref.shape[0]
    h = _rms(x_ref[...], g_ref[...], NORM_EPS)
    h_ref[...] = h.astype(BF16)
    logits = _mm_f32(h, router_ref[...])
    lane = lax.broadcasted_iota(jnp.int32, logits.shape, 1).astype(F32)
    lg = jnp.where(lane < N_EXPERTS, logits, -jnp.inf)
    m1 = jnp.max(lg, axis=-1, keepdims=True)
    i1 = jnp.min(jnp.where(lg == m1, lane, float(LANES)), axis=-1, keepdims=True)
    lg2 = jnp.where(lane == i1, -jnp.inf, lg)
    m2 = jnp.max(lg2, axis=-1, keepdims=True)
    i2 = jnp.min(jnp.where(lg2 == m2, lane, float(LANES)), axis=-1, keepdims=True)
    e2 = jnp.exp(m2 - m1)
    c_ref[...] = jnp.where(lane == i1, 1.0 / (1.0 + e2), 0.0) + jnp.where(lane == i2, e2 / (1.0 + e2), 0.0)
    sel = jnp.logical_or(lane == i1, lane == i2)
    sel_f = jnp.where(sel, 1.0, 0.0)
    earlier = _tri_masks(t)[0].astype(BF16)
    rank = jnp.dot(earlier, sel_f.astype(BF16), preferred_element_type=F32)
    rc = jnp.where(sel, rank, -1.0)
    rc_ref[...] = rc
    rr_ref[...] = rc.T[0:BF16_ROWS, :]
    cnt_ref[...] = jnp.broadcast_to(jnp.sum(sel_f, axis=0, keepdims=True), cnt_ref.shape).astype(jnp.int32)


def _router(x, g, router, tm):
    n = x.shape[0]
    return pl.pallas_call(
        _router_kernel,
        grid=(n // tm,),
        in_specs=[
            pl.BlockSpec((tm, D_MODEL), lambda i: (i, 0)),
            pl.BlockSpec((1, D_MODEL), lambda i: (0, 0)),
            pl.BlockSpec((D_MODEL, LANES), lambda i: (0, 0)),
        ],
        out_specs=[
            pl.BlockSpec((tm, D_MODEL), lambda i: (i, 0)),
            pl.BlockSpec((tm, LANES), lambda i: (i, 0)),
            pl.BlockSpec((tm, LANES), lambda i: (i, 0)),
            pl.BlockSpec((BF16_ROWS, tm), lambda i: (0, i)),
            pl.BlockSpec((8, LANES), lambda i: (i, 0)),
        ],
        out_shape=[
            jax.ShapeDtypeStruct((n, D_MODEL), BF16),
            jax.ShapeDtypeStruct((n, LANES), F32),
            jax.ShapeDtypeStruct((n, LANES), F32),
            jax.ShapeDtypeStruct((BF16_ROWS, n), F32),
            jax.ShapeDtypeStruct((n // tm * 8, LANES), jnp.int32),
        ],
        compiler_params=_cparams(("parallel",)),
        name="router",
    )(x, g, router)


def _moe_kernel(cnt_ref, h_ref, c_ref, rc_ref, rr_ref, wg_ref, wu_ref, wd_ref, o_ref, xg_scr, yg_scr, *, rows, parts):
    i = pl.program_id(0)
    e = pl.program_id(1)
    f = pl.program_id(2)
    last_f = f == pl.num_programs(2) - 1
    t = h_ref.shape[0] // parts
    nsub = xg_scr.shape[1] // rows
    cnts = [cnt_ref[(i * parts + p) * N_EXPERTS + e] for p in range(parts)]
    toks = [slice(p * t, (p + 1) * t) for p in range(parts)]
    blocks = [(sb, slice(sb * rows, (sb + 1) * rows)) for sb in range(nsub)]

    @pl.when(jnp.logical_and(e == 0, f == 0))
    def _():
        o_ref[...] = jnp.zeros_like(o_ref)

    @pl.when(f == 0)
    def _():
        for p in range(parts):
            rr = rr_ref[pl.ds(e, 1), toks[p]]
            for sb, rs in blocks:
                @pl.when(cnts[p] > sb * rows)
                def _():
                    slot = (lax.broadcasted_iota(jnp.int32, (rows, t), 0) + sb * rows).astype(F32)
                    gather = jnp.where(rr == slot, 1.0, 0.0).astype(BF16)
                    xg_scr[p, rs, :] = jnp.dot(gather, h_ref[toks[p], :], preferred_element_type=F32).astype(BF16)
                    yg_scr[p, rs, :] = jnp.zeros((rows, D_MODEL), F32)

    for p in range(parts):
        for sb, rs in blocks:
            @pl.when(cnts[p] > sb * rows)
            def _():
                xb = xg_scr[p, rs, :]
                act = (_silu(jnp.dot(xb, wg_ref[0], preferred_element_type=F32))
                       * jnp.dot(xb, wu_ref[0], preferred_element_type=F32))
                yg_scr[p, rs, :] += _mm(act, wd_ref[0])

    @pl.when(last_f)
    def _():
        mine = lax.broadcasted_iota(jnp.int32, (t, LANES), 1) == e
        for p in range(parts):
            rc = jnp.sum(jnp.where(mine, rc_ref[toks[p], :], 0.0), axis=-1, keepdims=True)
            ce = jnp.sum(jnp.where(mine, c_ref[toks[p], :], 0.0), axis=-1, keepdims=True)
            for sb, rs in blocks:
                @pl.when(cnts[p] > sb * rows)
                def _():
                    slot = (lax.broadcasted_iota(jnp.int32, (t, rows), 1) + sb * rows).astype(F32)
                    scatter = jnp.where(rc == slot, 1.0, 0.0).astype(BF16)
                    o_ref[toks[p], :] += ce * jnp.dot(scatter, yg_scr[p, rs, :].astype(BF16),
                                                      preferred_element_type=F32)


def _moe(x, g, router, wg, wu, wd, tm, parts, tf, rows):
    n = x.shape[0]
    ff = wg.shape[2]
    h, c, rc, rr, cnt = _router(x, g, router, tm)
    cnt = cnt.reshape(n // tm, 8, LANES)[:, 0, :N_EXPERTS].reshape(-1)
    nsub = -(-tm // rows)
    tt = tm * parts
    tok = lambda cols, **kw: pl.BlockSpec((tt, cols), lambda i, e, f, cnt: (i, 0), **kw)
    once = dict(pipeline_mode=pl.Buffered(1))
    return pl.pallas_call(
        functools.partial(_moe_kernel, rows=rows, parts=parts),
        grid_spec=pltpu.PrefetchScalarGridSpec(
            num_scalar_prefetch=1,
            grid=(n // tt, N_EXPERTS, ff // tf),
            in_specs=[
                tok(D_MODEL, **once), tok(LANES), tok(LANES),
                pl.BlockSpec((BF16_ROWS, tt), lambda i, e, f, cnt: (0, i)),
                pl.BlockSpec((1, D_MODEL, tf), lambda i, e, f, cnt: (e, 0, f)),
                pl.BlockSpec((1, D_MODEL, tf), lambda i, e, f, cnt: (e, 0, f)),
                pl.BlockSpec((1, tf, D_MODEL), lambda i, e, f, cnt: (e, f, 0)),
            ],
            out_specs=tok(D_MODEL),
            scratch_shapes=[pltpu.VMEM((parts, nsub * rows, D_MODEL), BF16),
                            pltpu.VMEM((parts, nsub * rows, D_MODEL), F32)],
        ),
        out_shape=jax.ShapeDtypeStruct((n, D_MODEL), F32),
        compiler_params=_cparams(("parallel", "arbitrary", "arbitrary"), vmem_mb=58),
        name="moe",
    )(cnt, h, c, rc, rr, wg, wu, wd)


def _ple_kernel(x_ref, d_ref, p_ref, g_ref, wgate_ref, wproj_ref, fin_ref, o_ref, *, final):
    x = x_ref[...] + d_ref[...]
    h = _rms(x, g_ref[...], NORM_EPS)
    y = x + _sigmoid(_mm(h, wgate_ref[...])) * _mm(p_ref[...], wproj_ref[...])
    o_ref[...] = _rms(y, fin_ref[...], NORM_EPS) if final else y


def _ple(x, delta, p, g, wgate, wproj, fin, tm, final):
    n = x.shape[0]
    tok = lambda c: pl.BlockSpec((tm, c), lambda i: (i, 0))
    const = lambda r, c: pl.BlockSpec((r, c), lambda i: (0, 0))
    return pl.pallas_call(
        functools.partial(_ple_kernel, final=final),
        grid=(n // tm,),
        in_specs=[tok(D_MODEL), tok(D_MODEL), tok(P_DIM), const(1, D_MODEL), const(D_MODEL, D_MODEL),
                  const(P_DIM, D_MODEL), const(1, D_MODEL)],
        out_specs=tok(D_MODEL),
        out_shape=jax.ShapeDtypeStruct((n, D_MODEL), F32),
        compiler_params=_cparams(("parallel",)),
        name="ple",
    )(x, delta, p, g, wgate, wproj, fin)


def _tiles(n, seq):
    tm = min(512, seq)
    tm_big = 1024 if n % 1024 == 0 else tm
    tm_in = 2048 if n % 2048 == 0 else tm_big
    moe_rows = tm_big // 4 + tm_big // 32
    moe_parts = 2 if n % (2 * tm_big) == 0 else 1
    return dict(tm=tm, tm_big=tm_big, tm_in=tm_in, tn_in=1024, tq=min(512, seq), cblk=min(512, seq), moe_rows=moe_rows,
                moe_parts=moe_parts)


def _rope_tables(seq):
    half = ROPE_DIMS // 2
    inv = ROPE_THETA ** (-jnp.arange(half, dtype=F32) * 2.0 / ROPE_DIMS)
    ang = jnp.arange(seq, dtype=F32)[:, None] * inv[None, :]
    pad = jnp.zeros((seq, DIFF_DH - ROPE_DIMS), F32)
    cos = jnp.concatenate([jnp.cos(ang), jnp.cos(ang), pad + 1.0], axis=-1)
    sin = jnp.concatenate([jnp.sin(ang), jnp.sin(ang), pad], axis=-1)
    d = jnp.arange(W_MIX)
    dd = d % DIFF_DH
    src = jnp.where(dd < half, d + half, d - half)
    sign = jnp.where(dd < half, -1.0, jnp.where(dd < ROPE_DIMS, 1.0, 0.0))
    rot = jnp.zeros((W_MIX, W_MIX), F32).at[src, d].set(sign)
    reps = W_MIX // DIFF_DH
    return jnp.tile(cos, (1, reps)), jnp.tile(sin, (1, reps)), rot.astype(BF16)


def _split_w_in(w):
    a0 = 0
    b0 = a0 + 4 * W_MIX
    c0 = b0 + 3 * W_MIX
    d0 = c0 + 3 * W_MIX + N_HEADS
    g0 = d0 + 4 * W_MIX + 2 * N_HEADS
    d_small = d0 + 3 * W_MIX
    diff_q = w[:, b0:b0 + W_MIX] * (DIFF_DH ** -0.5 * LOG2E)
    fox_q = w[:, c0:c0 + W_MIX] * (HEAD_DIM ** -0.5 * LOG2E)
    main = jnp.concatenate([
        w[:, g0:], w[:, a0:b0], w[:, d0:d_small], w[:, d_small + 2 * N_HEADS:g0],
        diff_q, w[:, b0 + W_MIX:b0 + 2 * W_MIX], fox_q, w[:, c0 + W_MIX:c0 + 2 * W_MIX],
    ], axis=1).astype(BF16)
    small = jnp.concatenate([
        w[:, c0 + 3 * W_MIX:d0], w[:, d_small:d_small + 2 * N_HEADS],
        jnp.zeros((D_MODEL, LANES - 3 * N_HEADS), w.dtype),
    ], axis=1).astype(BF16)
    v_t = jnp.concatenate([w[:, b0 + 2 * W_MIX:c0], w[:, c0 + 2 * W_MIX:c0 + 3 * W_MIX]], axis=1).T.astype(BF16)
    return main, small, v_t


def _small_params(fbias, a_log, dt_bias):
    zeros = jnp.zeros((N_HEADS,), F32)
    bias = jnp.concatenate([fbias, zeros, dt_bias, jnp.zeros((LANES - 3 * N_HEADS,), F32)])
    neg_a = jnp.concatenate([zeros, zeros, -jnp.exp(a_log), jnp.zeros((LANES - 3 * N_HEADS,), F32)])
    return bias.reshape(1, LANES), neg_a.reshape(1, LANES)


def _pad_rows(w, top, total):
    return jnp.concatenate([jnp.zeros((top, w.shape[1]), w.dtype), w,
                            jnp.zeros((total - top - w.shape[0], w.shape[1]), w.dtype)], axis=0)


def kernel(x, p, norm_mix, norm_ffn, norm_ple, w_in, w_bo, w_out, rwkv_mu, rwkv_w0, rwkv_w2, rwkv_a0, rwkv_a2, rwkv_g2, rwkv_kk, rwkv_ka, rwkv_rk, rwkv_ln_w, rwkv_ln_b, diff_lam, diff_subln, fox_fbias, gdn_conv, gdn_a_log, gdn_dt_bias, gdn_norm, ffn_w_gate, ffn_w_up, ffn_w_down, moe_router, moe_w_gate, moe_w_up, moe_w_down, ple_proj, ple_gate, final_norm):
    batch, seq, _ = x.shape
    depth = w_in.shape[0]
    n = batch * seq
    t = _tiles(n, seq)
    tm, tq, cblk = t["tm"], t["tq"], t["cblk"]
    row = lambda v: v.reshape(1, -1).astype(F32)
    cos, sin, rot = _rope_tables(seq)
    xf = x.reshape(n, D_MODEL)
    pf = p.reshape(depth, n, P_DIM)

    for i in range(depth):
        w_main, w_small, w_vt = _split_w_in(w_in[i])
        u, scol, vt = _inproj(xf, row(norm_mix[i]), w_main, w_small, w_vt, t["tm_in"], t["tn_in"])
        bias, neg_a = _small_params(fox_fbias[i], gdn_a_log[i], gdn_dt_bias[i])
        hcol, fox_kb, fox_qb = _small_prep(scol, bias, neg_a, batch, seq, tm)

        scan_in, pc, post = _rwkv_prep(
            u, row(rwkv_mu[i]), row(rwkv_w0[i]), _pad_rows(rwkv_w2[i], 0, LANES), row(rwkv_a0[i]),
            _pad_rows(rwkv_a2[i], LANES // 2, LANES), rwkv_g2[i], row(rwkv_kk[i]), row(rwkv_ka[i]), row(rwkv_rk[i]),
            seq, tm)
        o_a = _rwkv_chunk(scan_in, pc, batch, seq, cblk)

        o_d = _gdn_chunk(_gdn_prep(u, gdn_conv[i].T, seq, tm), hcol, jnp.tile(row(gdn_norm[i]), (1, N_HEADS)),
                         batch, seq, cblk)

        lam_init = 0.8 - 0.6 * math.exp(-0.3 * i)
        y_b = _diff_attention(u, vt, cos, sin, rot, diff_lam[i].astype(F32), jnp.tile(row(diff_subln[i]), (1, 2)),
                              batch, seq, tq, lam_init)
        y_c = _fox_attention(u, vt, fox_qb, fox_kb, batch, seq, tq)

        xf = _merge(xf, u, o_a, post, y_b, y_c, o_d, row(rwkv_ln_w[i]), row(rwkv_ln_b[i]),
                    w_bo[i].astype(BF16), w_out[i].astype(BF16), tm)

        j = i // 2
        if i % 2 == 0:
            delta = _ffn(xf, row(norm_ffn[i]), ffn_w_gate[j].astype(BF16), ffn_w_up[j].astype(BF16),
                         ffn_w_down[j].astype(BF16), tm, ffn_w_gate.shape[2] // 2)
        else:
            router = jnp.concatenate([moe_router[j], jnp.zeros((D_MODEL, LANES - N_EXPERTS), F32)], axis=1)
            delta = _moe(xf, row(norm_ffn[i]), router, moe_w_gate[j].astype(BF16), moe_w_up[j].astype(BF16),
                         moe_w_down[j].astype(BF16), t["tm_big"], t["moe_parts"], moe_w_gate.shape[3] // 4,
                         t["moe_rows"])
        xf = _ple(xf, delta, pf[i], row(norm_ple[i]), ple_gate[i].astype(BF16), ple_proj[i].astype(BF16),
                  row(final_norm), tm, i == depth - 1)
    return xf.reshape(batch, seq, D_MODEL)
```

```python
import functools
import math

import jax
import jax.numpy as jnp
import numpy as np
from jax import lax
from jax.experimental import pallas as pl
from jax.experimental.pallas import tpu as pltpu

F32 = jnp.float32
BF16 = jnp.bfloat16
HIGHEST = lax.Precision.HIGHEST

D_MODEL = 1024
P_DIM = 256
W_MIX = 256
HEAD_DIM = 64
N_HEADS = 4
DIFF_DH = 32
ROPE_THETA = 500000.0
ROPE_DIMS = 8
RWKV_GN_EPS = 64e-5
DIFF_LN_EPS = 1e-5
GDN_CONV = 4
CHUNK = 64
CHUNK_UNROLL = 4
N_EXPERTS = 8
NORM_EPS = 1e-6
L2_EPS = 1e-6
LOG2E = math.log2(math.e)
LANES = 128
BF16_ROWS = 16
assert CHUNK == HEAD_DIM

U_GATE = 0
U_RWKV = 4096
U_GDN = 5120
U_DIFF = 6144
U_FOX = 6656
U_COLS = 7168
SM_FOX, SM_BETA, SM_DEC = 0, 4, 8
FOX_CK, FOX_CQ = 0, 16


def _cparams(semantics, vmem_mb=48):
    return pltpu.CompilerParams(dimension_semantics=semantics, vmem_limit_bytes=vmem_mb * 1024 * 1024)


def _mm(a, b):
    return jnp.dot(a.astype(BF16), b.astype(BF16), preferred_element_type=F32)


def _mm_nt(a, b):
    return lax.dot_general(a.astype(BF16), b.astype(BF16), (((1,), (1,)), ((), ())), preferred_element_type=F32)


def _mm_tn(a, b):
    return lax.dot_general(a.astype(BF16), b.astype(BF16), (((0,), (0,)), ((), ())), preferred_element_type=F32)


def _mm_f32(a, b):
    return jnp.dot(a, b, preferred_element_type=F32, precision=HIGHEST)


def _mm_mask(mask, x):
    hi = x.astype(BF16)
    r1 = x - hi.astype(F32)
    mid = r1.astype(BF16)
    lo = (r1 - mid.astype(F32)).astype(BF16)
    dot = lambda t: jnp.dot(mask, t, preferred_element_type=F32)
    return dot(hi) + dot(mid) + dot(lo)


def _rms(x, g, eps):
    return x * lax.rsqrt(jnp.mean(x * x, axis=-1, keepdims=True) + eps) * g


def _sigmoid(x):
    return 1.0 / (1.0 + jnp.exp(-x))


def _silu(x):
    return x * _sigmoid(x)


def _softplus(x):
    return jnp.maximum(x, 0.0) + jnp.log(1.0 + jnp.exp(-jnp.abs(x)))


def _tri_masks(c):
    ii = lax.broadcasted_iota(jnp.int32, (c, c), 0)
    jj = lax.broadcasted_iota(jnp.int32, (c, c), 1)
    return ii > jj, ii >= jj, ii == jj


def _block_masks():
    ii = lax.broadcasted_iota(jnp.int32, (W_MIX, W_MIX), 0)
    jj = lax.broadcasted_iota(jnp.int32, (W_MIX, W_MIX), 1)
    return (ii // HEAD_DIM) == (jj // HEAD_DIM), ii == jj


def _wide_masks():
    ii = lax.broadcasted_iota(jnp.int32, (CHUNK, W_MIX), 0)
    jj = lax.broadcasted_iota(jnp.int32, (CHUNK, W_MIX), 1) % HEAD_DIM
    return ii > jj, ii >= jj, ii == jj


def _stack_heads(x, same):
    return jnp.where(same, jnp.concatenate([x, x, x, x], axis=0), jnp.zeros((), x.dtype))


def _unit_lower_inverses(ns, eye_w, same):
    rs = [eye_w + n for n in ns]
    ps = list(ns)
    for _ in range(int(math.log2(CHUNK)) - 1):
        ps = [_mm(p, _stack_heads(p.astype(BF16), same)) for p in ps]
        rs = [r + _mm(r, _stack_heads(p.astype(BF16), same)) for r, p in zip(rs, ps)]
    return rs


def _unstack_heads(x):
    return x[0:CHUNK] + x[CHUNK:2 * CHUNK] + x[2 * CHUNK:3 * CHUNK] + x[3 * CHUNK:4 * CHUNK]


def _inproj_kernel(x_ref, g_ref, w_ref, ws_ref, wvt_ref, u_ref, scol_ref, vt_ref, h_scr):
    @pl.when(pl.program_id(1) == 0)
    def _():
        hb = _rms(x_ref[...], g_ref[...], NORM_EPS).astype(BF16)
        h_scr[...] = hb
        scol_ref[...] = jnp.dot(hb, ws_ref[...], preferred_element_type=F32)
        vt_ref[...] = lax.dot_general(wvt_ref[...], hb, (((1,), (1,)), ((), ())),
                                      preferred_element_type=F32).astype(BF16)

    u_ref[...] = jnp.dot(h_scr[...], w_ref[...], preferred_element_type=F32).astype(BF16)


def _inproj(x, g, w, ws, wvt, tm, tn):
    n = x.shape[0]
    return pl.pallas_call(
        _inproj_kernel,
        grid=(n // tm, U_COLS // tn),
        in_specs=[
            pl.BlockSpec((tm, D_MODEL), lambda i, j: (i, 0)),
            pl.BlockSpec((1, D_MODEL), lambda i, j: (0, 0)),
            pl.BlockSpec((D_MODEL, tn), lambda i, j: (0, j)),
            pl.BlockSpec((D_MODEL, LANES), lambda i, j: (0, 0)),
            pl.BlockSpec((2 * W_MIX, D_MODEL), lambda i, j: (0, 0)),
        ],
        out_specs=[
            pl.BlockSpec((tm, tn), lambda i, j: (i, j)),
            pl.BlockSpec((tm, LANES), lambda i, j: (i, 0)),
            pl.BlockSpec((2 * W_MIX, tm), lambda i, j: (0, i)),
        ],
        out_shape=[
            jax.ShapeDtypeStruct((n, U_COLS), BF16),
            jax.ShapeDtypeStruct((n, LANES), F32),
            jax.ShapeDtypeStruct((2 * W_MIX, n), BF16),
        ],
        scratch_shapes=[pltpu.VMEM((tm, D_MODEL), BF16)],
        compiler_params=_cparams(("parallel", "arbitrary"), vmem_mb=56),
        name="inproj",
    )(x, g, w, ws, wvt)


def _lane_placement(base):
    r = lax.broadcasted_iota(jnp.int32, (3 * LANES, LANES), 0)
    m = lax.broadcasted_iota(jnp.int32, (3 * LANES, LANES), 1)
    head, part = r % LANES, r // LANES
    return jnp.logical_and(head < N_HEADS, m == base + 3 * head + part).astype(BF16)


def _split3(x):
    hi = x.astype(BF16)
    r1 = x - hi.astype(F32)
    mid = r1.astype(BF16)
    lo = (r1 - mid.astype(F32)).astype(BF16)
    return jnp.concatenate([hi, mid, lo], axis=1)


def _head_expansion():
    r = lax.broadcasted_iota(jnp.int32, (3 * LANES, 2 * W_MIX), 0) % LANES
    m = lax.broadcasted_iota(jnp.int32, (3 * LANES, 2 * W_MIX), 1)
    src = jnp.where(m < W_MIX, SM_BETA + m // HEAD_DIM, SM_DEC + (m - W_MIX) // HEAD_DIM)
    return (r == src).astype(BF16)


def _small_prep_kernel(scol_ref, bias_ref, nega_ref, ogb_ref, okb_ref, oqb_ref, carry_scr):
    @pl.when(pl.program_id(1) == 0)
    def _():
        carry_scr[...] = jnp.zeros_like(carry_scr)

    tm = scol_ref.shape[0]
    lower = _tri_masks(tm)[1].astype(BF16)
    z = scol_ref[...] + bias_ref[...]
    lane = lax.broadcasted_iota(jnp.int32, z.shape, 1)
    is_f = lane < SM_BETA
    is_b = jnp.logical_and(lane >= SM_BETA, lane < SM_DEC)
    is_d = jnp.logical_and(lane >= SM_DEC, lane < SM_DEC + N_HEADS)
    logf = jnp.where(is_f, jnp.minimum(z, 0.0) - jnp.log(1.0 + jnp.exp(-jnp.abs(z))), 0.0)
    gdn = jnp.where(is_b, _sigmoid(z), jnp.where(is_d, nega_ref[...] * _softplus(z), 0.0))
    ogb_ref[...] = jnp.dot(_split3(gdn), _head_expansion(), preferred_element_type=F32)
    cum = _mm_mask(lower, logf) + carry_scr[...]
    carry_scr[...] = cum[tm - 1:tm, :]
    parts = _split3(cum * LOG2E)
    ones_lanes = jnp.logical_and(lane >= FOX_CQ, lane < FOX_CQ + 3 * N_HEADS)
    okb_ref[...] = (jnp.dot(parts, _lane_placement(FOX_CK), preferred_element_type=F32)
                    + jnp.where(ones_lanes, 1.0, 0.0)).astype(BF16)
    oqb_ref[...] = jnp.dot(parts, _lane_placement(FOX_CQ), preferred_element_type=F32).astype(BF16)


def _small_prep(scol, bias, neg_a, batch, seq, tm):
    n = batch * seq
    nt = seq // tm
    tok = lambda: pl.BlockSpec((tm, LANES), lambda b, j: (b * nt + j, 0))
    const = lambda: pl.BlockSpec((1, LANES), lambda b, j: (0, 0))
    return pl.pallas_call(
        _small_prep_kernel,
        grid=(batch, nt),
        in_specs=[tok(), const(), const()],
        out_specs=[pl.BlockSpec((tm, 2 * W_MIX), lambda b, j: (b * nt + j, 0)), tok(), tok()],
        out_shape=[jax.ShapeDtypeStruct((n, 2 * W_MIX), F32), jax.ShapeDtypeStruct((n, LANES), BF16),
                   jax.ShapeDtypeStruct((n, LANES), BF16)],
        scratch_shapes=[pltpu.VMEM((1, LANES), F32)],
        compiler_params=_cparams(("parallel", "arbitrary")),
        name="small_prep",
    )(scol, bias, neg_a)


def _rwkv_prep_kernel(u_ref, up_ref, mu_ref, w0_ref, w2_ref, a0_ref, a2_ref, g2_ref, kk_ref, ka_ref, rk_ref,
                      scan_ref, pc_ref, post_ref, *, tiles_per_seq):
    tm = u_ref.shape[0]
    u = u_ref[...].astype(F32)
    prev = up_ref[...].astype(F32)[BF16_ROWS - 1:BF16_ROWS, :]
    prev = jnp.where(pl.program_id(0) % tiles_per_seq == 0, 0.0, prev)
    rows = lax.broadcasted_iota(jnp.int32, (tm, 1), 0)
    u_prev = jnp.where(rows == 0, prev, pltpu.roll(u, 1, 0))
    xm = u + (u_prev - u) * mu_ref[...]
    r = xm[:, 0:W_MIX]
    k = xm[:, W_MIX:2 * W_MIX]
    v = xm[:, 2 * W_MIX:3 * W_MIX]
    x_lora = xm[:, 3 * W_MIX:3 * W_MIX + LANES]
    xg = xm[:, 3 * W_MIX + LANES:]
    logw = -_softplus(-(w0_ref[...] + _mm(jnp.tanh(x_lora), w2_ref[...]))) - 0.5
    log_decay = -jnp.exp(logw)
    a = _sigmoid(a0_ref[...] + _mm(x_lora, a2_ref[...]))
    g = _mm(_sigmoid(xg), g2_ref[...])
    same = _block_masks()[0].astype(F32)
    kk_raw = k * kk_ref[...]
    kk = kk_raw * lax.rsqrt(_mm_f32(kk_raw * kk_raw, same) + L2_EPS)
    k2 = k * (1.0 + (a - 1.0) * ka_ref[...])
    bonus = _mm_f32(r * k2 * rk_ref[...], same) * v
    ti = lax.broadcasted_iota(jnp.int32, (tm, tm), 0)
    tj = lax.broadcasted_iota(jnp.int32, (tm, tm), 1)
    in_chunk = jnp.logical_and(ti // CHUNK == tj // CHUNK, ti >= tj).astype(BF16)
    ci = lax.broadcasted_iota(jnp.int32, (tm // CHUNK, tm), 0)
    cj = lax.broadcasted_iota(jnp.int32, (tm // CHUNK, tm), 1)
    cum = _mm_mask(in_chunk, log_decay)
    cum_end = _mm_mask((ci == cj // CHUNK).astype(BF16), log_decay)
    inv = jnp.exp(-cum)
    scan_ref[0] = (-kk * jnp.exp(cum - log_decay)).astype(BF16)
    scan_ref[1] = (kk * a * inv).astype(BF16)
    scan_ref[2] = (k2 * inv).astype(BF16)
    scan_ref[3] = (r * jnp.exp(cum)).astype(BF16)
    scan_ref[4] = v.astype(BF16)
    pc_ref[...] = jnp.exp(cum_end)
    post_ref[0] = g
    post_ref[1] = bonus


def _rwkv_prep(u, mu, w0, w2p, a0, a2p, g2, k_k, k_a, r_k, seq, tm):
    n = u.shape[0]
    ublk = U_RWKV // D_MODEL
    row = lambda c: pl.BlockSpec((1, c), lambda i: (0, 0))
    mat = lambda r: pl.BlockSpec((r, W_MIX), lambda i: (0, 0))
    return pl.pallas_call(
        functools.partial(_rwkv_prep_kernel, tiles_per_seq=seq // tm),
        grid=(n // tm,),
        in_specs=[
            pl.BlockSpec((tm, D_MODEL), lambda i: (i, ublk)),
            pl.BlockSpec((BF16_ROWS, D_MODEL), lambda i: (jnp.maximum(i * (tm // BF16_ROWS) - 1, 0), ublk)),
            row(D_MODEL), row(W_MIX), mat(LANES), row(W_MIX), mat(LANES), mat(LANES), row(W_MIX), row(W_MIX), row(W_MIX),
        ],
        out_specs=[
            pl.BlockSpec((5, tm, W_MIX), lambda i: (0, i, 0)),
            pl.BlockSpec((tm // CHUNK, W_MIX), lambda i: (i, 0)),
            pl.BlockSpec((2, tm, W_MIX), lambda i: (0, i, 0)),
        ],
        out_shape=[jax.ShapeDtypeStruct((5, n, W_MIX), BF16), jax.ShapeDtypeStruct((n // CHUNK, W_MIX), F32),
                   jax.ShapeDtypeStruct((2, n, W_MIX), F32)],
        compiler_params=_cparams(("parallel",)),
        name="rwkv_prep",
    )(u, u, mu, w0, w2p, a0, a2p, g2, k_k, k_a, r_k)


def _rwkv_chunk_kernel(x_ref, pc_ref, o_ref, s_scr, *, nchunk, nbatch):
    @pl.when(pl.program_id(0) == 0)
    def _():
        s_scr[...] = jnp.zeros_like(s_scr)

    same, diag = _block_masks()
    eye = diag.astype(F32)
    strict_w, incl_w, diag_w = _wide_masks()
    eye_w = diag_w.astype(F32)
    stack = lambda x: _stack_heads(x.astype(BF16), same)

    def chunks(i, carry):
        items = [(b, i * CHUNK_UNROLL + j) for j in range(CHUNK_UNROLL) for b in range(nbatch)]
        each = lambda f, *lists: [f(*args) for args in zip(*lists)]
        sls = [pl.ds(pl.multiple_of(ci * CHUNK, CHUNK), CHUNK) for _, ci in items]
        a, bb, k, r, v = ([x_ref[i, b, sl, :] for (b, _), sl in zip(items, sls)] for i in range(5))
        pc = [pc_ref[b, pl.ds(ci, 1), :] for b, ci in items]
        a_s, b_s, k_s, v_s = (each(stack, x) for x in (a, bb, k, v))
        m_ab = each(lambda x, y: jnp.where(strict_w, _mm_nt(x, y), 0.0), a, b_s)
        m_ak = each(lambda x, y: jnp.where(strict_w, _mm_nt(x, y), 0.0), a, k_s)
        n_rb = each(lambda x, y: jnp.where(incl_w, _mm_nt(x, y), 0.0), r, b_s)
        n_rk = each(lambda x, y: jnp.where(incl_w, _mm_nt(x, y), 0.0), r, k_s)
        t_inv = _unit_lower_inverses(m_ab, eye_w, same)
        a2 = each(_mm, t_inv, a_s)
        u0 = each(lambda t, m, x: _mm(t, stack(_mm(m, x))), t_inv, m_ak, v_s)
        r2 = each(lambda x, n, y: x.astype(F32) + _mm(n, stack(y)), r, n_rb, a2)
        o0 = each(lambda n1, u, n2, x: _mm(n1, stack(u)) + _mm(n2, x), n_rb, u0, n_rk, v_s)
        b_end = each(lambda x, p: x.astype(F32) * p, bb, pc)
        k_end = each(lambda x, p: x.astype(F32) * p, k, pc)
        g_mat = each(lambda p, x, y: eye * p + jnp.where(same, _mm_tn(x, y), 0.0), pc, a2, b_end)
        s0 = each(lambda u, x, y, z: jnp.where(same, _mm_tn(u, x) + _mm_tn(y, z), 0.0), u0, b_end, v, k_end)
        for n, ((b, _), sl) in enumerate(zip(items, sls)):
            s = s_scr[b]
            o = _stack_heads(_mm_nt(r2[n], s) + o0[n], same)
            s_scr[b] = _mm(s, g_mat[n]) + s0[n]
            mean = jnp.sum(o, axis=-1, keepdims=True) * (1.0 / HEAD_DIM)
            cen = jnp.where(same, o - mean, 0.0)
            var = jnp.sum(cen * cen, axis=-1, keepdims=True) * (1.0 / HEAD_DIM)
            o_ref[b, sl, :] = _unstack_heads(cen * lax.rsqrt(var + RWKV_GN_EPS))
        return carry

    lax.fori_loop(0, nchunk // CHUNK_UNROLL, chunks, 0)


def _rwkv_chunk(xs, pc, batch, seq, cblk):
    xs = xs.reshape(5, batch, seq, W_MIX)
    pc = pc.reshape(batch, seq // CHUNK, W_MIX)
    out = pl.pallas_call(
        functools.partial(_rwkv_chunk_kernel, nchunk=cblk // CHUNK, nbatch=batch),
        grid=(seq // cblk,),
        in_specs=[
            pl.BlockSpec((5, batch, cblk, W_MIX), lambda j: (0, 0, j, 0)),
            pl.BlockSpec((batch, cblk // CHUNK, W_MIX), lambda j: (0, j, 0)),
        ],
        out_specs=pl.BlockSpec((batch, cblk, W_MIX), lambda j: (0, j, 0)),
        out_shape=jax.ShapeDtypeStruct((batch, seq, W_MIX), F32),
        scratch_shapes=[pltpu.VMEM((batch, W_MIX, W_MIX), F32)],
        compiler_params=_cparams(("arbitrary",)),
        name="rwkv_chunk",
    )(xs, pc)
    return out.reshape(batch * seq, W_MIX)


def _gdn_prep_kernel(u_ref, up_ref, cw_ref, o_ref, ext_scr, *, tiles_per_seq):
    tm = u_ref.shape[0]
    c3 = 3 * W_MIX
    prev = up_ref[...].astype(F32)[:, :c3]
    ext_scr[0:BF16_ROWS, :] = jnp.where(pl.program_id(0) % tiles_per_seq == 0, 0.0, prev)
    ext_scr[BF16_ROWS:, :] = u_ref[...].astype(F32)[:, :c3]
    y = jnp.zeros((tm, c3), F32)
    for j in range(GDN_CONV):
        y = y + ext_scr[pl.ds(BF16_ROWS - (GDN_CONV - 1) + j, tm), :] * cw_ref[j:j + 1, :]
    y = _silu(y)
    same = _block_masks()[0].astype(F32)
    q = y[:, 0:W_MIX]
    k = y[:, W_MIX:2 * W_MIX]
    o_ref[0] = (q * lax.rsqrt(_mm_f32(q * q, same) + L2_EPS) * (HEAD_DIM ** -0.5)).astype(BF16)
    o_ref[1] = (k * lax.rsqrt(_mm_f32(k * k, same) + L2_EPS)).astype(BF16)
    o_ref[2] = y[:, 2 * W_MIX:].astype(BF16)


def _gdn_prep(u, conv_w, seq, tm):
    n = u.shape[0]
    ublk = U_GDN // D_MODEL
    return pl.pallas_call(
        functools.partial(_gdn_prep_kernel, tiles_per_seq=seq // tm),
        grid=(n // tm,),
        in_specs=[
            pl.BlockSpec((tm, D_MODEL), lambda i: (i, ublk)),
            pl.BlockSpec((BF16_ROWS, D_MODEL), lambda i: (jnp.maximum(i * (tm // BF16_ROWS) - 1, 0), ublk)),
            pl.BlockSpec((GDN_CONV, 3 * W_MIX), lambda i: (0, 0)),
        ],
        out_specs=pl.BlockSpec((3, tm, W_MIX), lambda i: (0, i, 0)),
        out_shape=jax.ShapeDtypeStruct((3, n, W_MIX), BF16),
        scratch_shapes=[pltpu.VMEM((tm + BF16_ROWS, 3 * W_MIX), F32)],
        compiler_params=_cparams(("parallel",)),
        name="gdn_prep",
    )(u, u, conv_w)


def _gdn_chunk_kernel(x_ref, gb_ref, nw_ref, o_ref, s_scr, *, nchunk, nbatch):
    @pl.when(pl.program_id(0) == 0)
    def _():
        s_scr[...] = jnp.zeros_like(s_scr)

    same, diag = _block_masks()
    eye = diag.astype(F32)
    strict_w, incl_w, diag_w = _wide_masks()
    eye_w = diag_w.astype(F32)
    lower = _tri_masks(CHUNK)[1].astype(BF16)
    stack = lambda x: _stack_heads(x.astype(BF16), same)

    def chunks(i, carry):
        items = [(b, i * CHUNK_UNROLL + j) for j in range(CHUNK_UNROLL) for b in range(nbatch)]
        each = lambda f, *lists: [f(*args) for args in zip(*lists)]
        sls = [pl.ds(pl.multiple_of(ci * CHUNK, CHUNK), CHUNK) for _, ci in items]
        q, k, v = ([x_ref[i, b, sl, :] for (b, _), sl in zip(items, sls)] for i in range(3))
        beta = [gb_ref[b, sl, 0:W_MIX] for (b, _), sl in zip(items, sls)]
        g = [gb_ref[b, sl, W_MIX:2 * W_MIX] for (b, _), sl in zip(items, sls)]
        k_s = each(stack, k)
        gam = [_mm_mask(lower, x) for x in g]
        gam_end = [x[CHUNK - 1:CHUNK, :] for x in gam]
        gdiff = [_mm_mask(lower, jnp.where(strict_w, x, 0.0)) for x in g]
        decay = [jnp.exp(jnp.where(incl_w, x, -jnp.inf)) for x in gdiff]
        a_mat = each(lambda bt, d, x, y: jnp.where(strict_w, bt * d * _mm_nt(x, y), 0.0), beta, decay, k, k_s)
        t_inv = _unit_lower_inverses([-a for a in a_mat], eye_w, same)
        e_gam = [jnp.exp(x) for x in gam]
        u0 = each(lambda t, bt, x: _mm(t, stack(bt * x.astype(F32))), t_inv, beta, v)
        wm = each(lambda t, bt, e, x: _mm(t, stack((bt * e) * x.astype(F32))), t_inv, beta, e_gam, k)
        qk = each(lambda x, y, d: _mm_nt(x, y) * d, q, k_s, decay)
        q2 = each(lambda e, x, a, w: e * x.astype(F32) - _mm(a, stack(w)), e_gam, q, qk, wm)
        o0 = each(lambda a, u: _mm(a, stack(u)), qk, u0)
        k_end = each(lambda x, ge, ga: x.astype(F32) * jnp.exp(ge - ga), k, gam_end, gam)
        g_mat = each(lambda ge, x, w: eye * jnp.exp(ge) - jnp.where(same, _mm_tn(x, w), 0.0), gam_end, k_end, wm)
        s0 = each(lambda x, u: jnp.where(same, _mm_tn(x, u), 0.0), k_end, u0)
        for n, ((b, _), sl) in enumerate(zip(items, sls)):
            s = s_scr[b]
            o = _stack_heads(_mm(q2[n], s) + o0[n], same)
            s_scr[b] = _mm(g_mat[n], s) + s0[n]
            ms = jnp.sum(o * o, axis=-1, keepdims=True) * (1.0 / HEAD_DIM)
            o_ref[b, sl, :] = _unstack_heads(o * lax.rsqrt(ms + NORM_EPS)) * nw_ref[...]
        return carry

    lax.fori_loop(0, nchunk // CHUNK_UNROLL, chunks, 0)


def _gdn_chunk(xs, gb, norm_w, batch, seq, cblk):
    xs = xs.reshape(3, batch, seq, W_MIX)
    gb = gb.reshape(batch, seq, 2 * W_MIX)
    out = pl.pallas_call(
        functools.partial(_gdn_chunk_kernel, nchunk=cblk // CHUNK, nbatch=batch),
        grid=(seq // cblk,),
        in_specs=[
            pl.BlockSpec((3, batch, cblk, W_MIX), lambda j: (0, 0, j, 0)),
            pl.BlockSpec((batch, cblk, 2 * W_MIX), lambda j: (0, j, 0)),
            pl.BlockSpec((1, W_MIX), lambda j: (0, 0)),
        ],
        out_specs=pl.BlockSpec((batch, cblk, W_MIX), lambda j: (0, j, 0)),
        out_shape=jax.ShapeDtypeStruct((batch, seq, W_MIX), F32),
        scratch_shapes=[pltpu.VMEM((batch, W_MIX, W_MIX), F32)],
        compiler_params=_cparams(("arbitrary",)),
        name="gdn_chunk",
    )(xs, gb, norm_w)
    return out.reshape(batch * seq, W_MIX)


def _causal_pairs(nq):
    pairs = [(i, j) for i in range(nq) for j in range(i + 1)]
    return jnp.asarray(np.array([p[0] for p in pairs], np.int32)), jnp.asarray(np.array([p[1] for p in pairs], np.int32))


def _softmax_updates(scores, vt_ones, m_scr, acc_scr):
    probs = []
    for i, s in enumerate(scores):
        m_old = m_scr[i]
        m_new = jnp.maximum(m_old, jnp.max(s, axis=0, keepdims=True))
        m_scr[i] = m_new
        probs.append((jnp.exp2(m_old - m_new), jnp.exp2((s - m_new).astype(BF16))))
    for i, (alpha, p) in enumerate(probs):
        acc_scr[i] = alpha * acc_scr[i] + jnp.dot(vt_ones[i], p, preferred_element_type=F32)


def _pair_lanes(h):
    p = h // 2
    return slice(p * LANES, (p + 1) * LANES), h % 2 == 0


ACC_ROWS = HEAD_DIM + BF16_ROWS


def _values_and_ones(vt, tk):
    ones = jnp.ones((BF16_ROWS, tk), vt.dtype)
    return [jnp.concatenate([vt[h * HEAD_DIM:(h + 1) * HEAD_DIM, :], ones], axis=0) for h in range(N_HEADS)]


def _normalized(acc):
    return acc[0:HEAD_DIM] / acc[HEAD_DIM:HEAD_DIM + 1]


def _key_after_query(tq):
    return lax.broadcasted_iota(jnp.int32, (tq, tq), 0) > lax.broadcasted_iota(jnp.int32, (tq, tq), 1)


def _fox_kernel(qi_ref, kj_ref, q_ref, k_ref, vt_ref, qb_ref, kb_ref, o_ref, qm_scr, m_scr, acc_scr):
    t = pl.program_id(1)
    qi = qi_ref[t]
    kj = kj_ref[t]
    tq = q_ref.shape[0]

    @pl.when(kj == 0)
    def _():
        m_scr[...] = jnp.full_like(m_scr, -jnp.inf)
        acc_scr[...] = jnp.zeros_like(acc_scr)
        q = q_ref[...]
        qb = qb_ref[...]
        lane = lax.broadcasted_iota(jnp.int32, (tq, LANES), 1)
        zero = jnp.zeros((), BF16)
        for h in range(N_HEADS):
            slab, low = _pair_lanes(h)
            mine = lane < HEAD_DIM if low else lane >= HEAD_DIM
            ck_lanes = jnp.logical_and(lane >= FOX_CK + 3 * h, lane < FOX_CK + 3 * h + 3)
            cq_lanes = jnp.logical_and(lane >= FOX_CQ + 3 * h, lane < FOX_CQ + 3 * h + 3)
            bias = jnp.where(ck_lanes, -jnp.ones((), BF16), jnp.where(cq_lanes, qb, zero))
            qm_scr[h] = jnp.concatenate([jnp.where(mine, q[:, slab], zero), bias], axis=1)

    def step(diagonal):
        k = k_ref[...]
        kb = kb_ref[...]
        vts = _values_and_ones(vt_ref[...], tq)
        if diagonal:
            masked = _key_after_query(tq)
        keys = [jnp.concatenate([k[:, p * LANES:(p + 1) * LANES], kb], axis=1) for p in range(N_HEADS // 2)]
        scores = []
        for h in range(N_HEADS):
            s = lax.dot_general(keys[h // 2], qm_scr[h], (((1,), (1,)), ((), ())), preferred_element_type=F32)
            if diagonal:
                s = jnp.where(masked, -jnp.inf, s)
            scores.append(s)
        _softmax_updates(scores, vts, m_scr, acc_scr)

    @pl.when(kj < qi)
    def _():
        step(False)

    @pl.when(kj == qi)
    def _():
        step(True)
        for p in range(N_HEADS // 2):
            pair = jnp.concatenate([_normalized(acc_scr[2 * p]), _normalized(acc_scr[2 * p + 1])], axis=0)
            o_ref[:, p * LANES:(p + 1) * LANES] = pair.T


def _attn_specs(nq, tq, ucol, vt_rows):
    cb = ucol // W_MIX
    q_spec = pl.BlockSpec((tq, W_MIX), lambda b, t, qi, kj: (b * nq + qi[t], cb))
    k_spec = pl.BlockSpec((tq, W_MIX), lambda b, t, qi, kj: (b * nq + kj[t], cb + 1))
    vt_spec = pl.BlockSpec((W_MIX, tq), lambda b, t, qi, kj: (vt_rows // W_MIX, b * nq + kj[t]))
    return q_spec, k_spec, vt_spec


def _fox_attention(u, vt, qb, kb, batch, seq, tq):
    nq = seq // tq
    qi, kj = _causal_pairs(nq)
    q_spec, k_spec, vt_spec = _attn_specs(nq, tq, U_FOX, W_MIX)
    stat = lambda: pltpu.VMEM((N_HEADS, 1, tq), F32)
    return pl.pallas_call(
        _fox_kernel,
        grid_spec=pltpu.PrefetchScalarGridSpec(
            num_scalar_prefetch=2,
            grid=(batch, qi.shape[0]),
            in_specs=[
                q_spec, k_spec, vt_spec,
                pl.BlockSpec((tq, LANES), lambda b, t, qi, kj: (b * nq + qi[t], 0)),
                pl.BlockSpec((tq, LANES), lambda b, t, qi, kj: (b * nq + kj[t], 0)),
            ],
            out_specs=pl.BlockSpec((tq, W_MIX), lambda b, t, qi, kj: (b * nq + qi[t], 0)),
            scratch_shapes=[pltpu.VMEM((N_HEADS, tq, 2 * LANES), BF16), stat(),
                            pltpu.VMEM((N_HEADS, ACC_ROWS, tq), F32)],
        ),
        out_shape=jax.ShapeDtypeStruct((batch * seq, W_MIX), F32),
        compiler_params=_cparams(("parallel", "arbitrary")),
        name="fox_attention",
    )(qi, kj, u, u, vt, qb, kb)


def _diff_kernel(qi_ref, kj_ref, q_ref, k_ref, vt_ref, cq_ref, sq_ref, ck_ref, sk_ref, rot_ref, lam_ref, ln_ref, o_ref,
                 qm_scr, m_scr, acc_scr, *, lam_init):
    t = pl.program_id(1)
    qi = qi_ref[t]
    kj = kj_ref[t]
    tq = q_ref.shape[0]
    lane = lax.broadcasted_iota(jnp.int32, (tq, LANES), 1)

    def rope(x, cos, sin):
        return x.astype(F32) * cos + jnp.dot(x, rot_ref[...], preferred_element_type=F32) * sin

    @pl.when(kj == 0)
    def _():
        m_scr[...] = jnp.full_like(m_scr, -jnp.inf)
        acc_scr[...] = jnp.zeros_like(acc_scr)
        q = rope(q_ref[...], cq_ref[...], sq_ref[...])
        for h in range(N_HEADS):
            slab, low = _pair_lanes(h)
            base = 0 if low else HEAD_DIM
            for c in range(2):
                lo = base + c * DIFF_DH
                sel = jnp.logical_and(lane >= lo, lane < lo + DIFF_DH)
                qm_scr[2 * h + c] = jnp.where(sel, q[:, slab], 0.0).astype(BF16)

    def step(diagonal):
        k = rope(k_ref[...], ck_ref[...], sk_ref[...]).astype(BF16)
        vts = _values_and_ones(vt_ref[...], tq)
        if diagonal:
            masked = _key_after_query(tq)
        scores = []
        for h in range(N_HEADS):
            slab, _ = _pair_lanes(h)
            for c in range(2):
                i = 2 * h + c
                s = lax.dot_general(k[:, slab], qm_scr[i], (((1,), (1,)), ((), ())), preferred_element_type=F32)
                if diagonal:
                    s = jnp.where(masked, -jnp.inf, s)
                scores.append(s)
        _softmax_updates(scores, [vts[i // 2] for i in range(2 * N_HEADS)], m_scr, acc_scr)

    @pl.when(kj < qi)
    def _():
        step(False)

    @pl.when(kj == qi)
    def _():
        step(True)
        lp = lam_ref[...]
        lam = (jnp.exp(jnp.sum(lp[0:1] * lp[1:2], axis=-1, keepdims=True))
               - jnp.exp(jnp.sum(lp[2:3] * lp[3:4], axis=-1, keepdims=True)) + lam_init)
        head = lambda h: _normalized(acc_scr[2 * h]) - lam * _normalized(acc_scr[2 * h + 1])
        is_lo = lane < HEAD_DIM
        for p in range(N_HEADS // 2):
            o = jnp.concatenate([head(2 * p), head(2 * p + 1)], axis=0).T
            sq = o * o
            ms_lo = jnp.sum(jnp.where(is_lo, sq, 0.0), axis=-1, keepdims=True)
            ms_hi = jnp.sum(jnp.where(is_lo, 0.0, sq), axis=-1, keepdims=True)
            ms = jnp.where(is_lo, ms_lo, ms_hi) * (1.0 / HEAD_DIM)
            o_ref[:, p * LANES:(p + 1) * LANES] = o * lax.rsqrt(ms + DIFF_LN_EPS) * ln_ref[...] * (1.0 - lam_init)


def _diff_attention(u, vt, cos, sin, rot, lam_p, subln, batch, seq, tq, lam_init):
    nq = seq // tq
    qi, kj = _causal_pairs(nq)
    q_spec, k_spec, vt_spec = _attn_specs(nq, tq, U_DIFF, 0)
    tab_q = pl.BlockSpec((tq, W_MIX), lambda b, t, qi, kj: (qi[t], 0))
    tab_k = pl.BlockSpec((tq, W_MIX), lambda b, t, qi, kj: (kj[t], 0))
    const = lambda r, c: pl.BlockSpec((r, c), lambda b, t, qi, kj: (0, 0))
    stat = lambda: pltpu.VMEM((2 * N_HEADS, 1, tq), F32)
    return pl.pallas_call(
        functools.partial(_diff_kernel, lam_init=lam_init),
        grid_spec=pltpu.PrefetchScalarGridSpec(
            num_scalar_prefetch=2,
            grid=(batch, qi.shape[0]),
            in_specs=[q_spec, k_spec, vt_spec, tab_q, tab_q, tab_k, tab_k,
                      const(W_MIX, W_MIX), const(4, DIFF_DH), const(1, LANES)],
            out_specs=pl.BlockSpec((tq, W_MIX), lambda b, t, qi, kj: (b * nq + qi[t], 0)),
            scratch_shapes=[pltpu.VMEM((2 * N_HEADS, tq, LANES), BF16), stat(),
                            pltpu.VMEM((2 * N_HEADS, ACC_ROWS, tq), F32)],
        ),
        out_shape=jax.ShapeDtypeStruct((batch * seq, W_MIX), F32),
        compiler_params=_cparams(("parallel", "arbitrary")),
        name="diff_attention",
    )(qi, kj, u, u, vt, cos, sin, cos, sin, rot, lam_p, subln)


def _merge_kernel(x_ref, gate_ref, oa_ref, post_ref, yb_ref, yc_ref, od_ref, gd_ref, lnw_ref, lnb_ref,
                  wbo_ref, wout_ref, o_ref):
    y_a = (oa_ref[...] * lnw_ref[...] + lnb_ref[...] + post_ref[1]) * post_ref[0]
    y_d = od_ref[...] * _silu(gd_ref[...].astype(F32))
    acc = jnp.zeros(x_ref.shape, F32)
    for b, y in enumerate((y_a, yb_ref[...], yc_ref[...], y_d)):
        gate = _sigmoid(gate_ref[:, b * D_MODEL:(b + 1) * D_MODEL].astype(F32))
        acc = acc + gate * _mm(y, wbo_ref[b])
    o_ref[...] = x_ref[...] + _mm(acc, wout_ref[...])


def _merge(x, u, o_a, post, y_b, y_c, o_d, ln_w, ln_b, w_bo, w_out, tm):
    n = x.shape[0]
    tok = lambda c: pl.BlockSpec((tm, c), lambda i: (i, 0))
    return pl.pallas_call(
        _merge_kernel,
        grid=(n // tm,),
        in_specs=[
            tok(D_MODEL),
            pl.BlockSpec((tm, 4 * D_MODEL), lambda i: (i, U_GATE // (4 * D_MODEL))),
            tok(W_MIX),
            pl.BlockSpec((2, tm, W_MIX), lambda i: (0, i, 0)),
            tok(W_MIX), tok(W_MIX), tok(W_MIX),
            pl.BlockSpec((tm, W_MIX), lambda i: (i, (U_GDN + 3 * W_MIX) // W_MIX)),
            pl.BlockSpec((1, W_MIX), lambda i: (0, 0)),
            pl.BlockSpec((1, W_MIX), lambda i: (0, 0)),
            pl.BlockSpec((4, W_MIX, D_MODEL), lambda i: (0, 0, 0)),
            pl.BlockSpec((D_MODEL, D_MODEL), lambda i: (0, 0)),
        ],
        out_specs=tok(D_MODEL),
        out_shape=jax.ShapeDtypeStruct((n, D_MODEL), F32),
        compiler_params=_cparams(("parallel",)),
        name="merge",
    )(x, u, o_a, post, y_b, y_c, o_d, u, ln_w, ln_b, w_bo, w_out)


def _ffn_kernel(x_ref, g_ref, wg_ref, wu_ref, wd_ref, o_ref, h_scr, acc_scr):
    f = pl.program_id(1)

    @pl.when(f == 0)
    def _():
        h_scr[...] = _rms(x_ref[...], g_ref[...], NORM_EPS).astype(BF16)
        acc_scr[...] = jnp.zeros_like(acc_scr)

    h = h_scr[...]
    act = _silu(jnp.dot(h, wg_ref[...], preferred_element_type=F32)) * jnp.dot(h, wu_ref[...], preferred_element_type=F32)
    acc_scr[...] += _mm(act, wd_ref[...])

    @pl.when(f == pl.num_programs(1) - 1)
    def _():
        o_ref[...] = acc_scr[...]


def _ffn(x, g, wg, wu, wd, tm, tf):
    n = x.shape[0]
    ff = wg.shape[1]
    return pl.pallas_call(
        _ffn_kernel,
        grid=(n // tm, ff // tf),
        in_specs=[
            pl.BlockSpec((tm, D_MODEL), lambda i, f: (i, 0)),
            pl.BlockSpec((1, D_MODEL), lambda i, f: (0, 0)),
            pl.BlockSpec((D_MODEL, tf), lambda i, f: (0, f)),
            pl.BlockSpec((D_MODEL, tf), lambda i, f: (0, f)),
            pl.BlockSpec((tf, D_MODEL), lambda i, f: (f, 0)),
        ],
        out_specs=pl.BlockSpec((tm, D_MODEL), lambda i, f: (i, 0)),
        out_shape=jax.ShapeDtypeStruct((n, D_MODEL), F32),
        scratch_shapes=[pltpu.VMEM((tm, D_MODEL), BF16), pltpu.VMEM((tm, D_MODEL), F32)],
        compiler_params=_cparams(("parallel", "arbitrary")),
        name="ffn",
    )(x, g, wg, wu, wd)


def _router_kernel(x_ref, g_ref, router_ref, h_ref, c_ref, rc_ref, rr_ref, cnt_ref):
    t = x_ref.shape[0]
    h = _rms(x_ref[...], g_ref[...], NORM_EPS)
    h_ref[...] = h.astype(BF16)
    logits = _mm_f32(h, router_ref[...])
    lane = lax.broadcasted_iota(jnp.int32, logits.shape, 1).astype(F32)
    lg = jnp.where(lane < N_EXPERTS, logits, -jnp.inf)
    m1 = jnp.max(lg, axis=-1, keepdims=True)
    i1 = jnp.min(jnp.where(lg == m1, lane, float(LANES)), axis=-1, keepdims=True)
    lg2 = jnp.where(lane == i1, -jnp.inf, lg)
    m2 = jnp.max(lg2, axis=-1, keepdims=True)
    i2 = jnp.min(jnp.where(lg2 == m2, lane, float(LANES)), axis=-1, keepdims=True)
    e2 = jnp.exp(m2 - m1)
    c_ref[...] = jnp.where(lane == i1, 1.0 / (1.0 + e2), 0.0) + jnp.where(lane == i2, e2 / (1.0 + e2), 0.0)
    sel = jnp.logical_or(lane == i1, lane == i2)
    sel_f = jnp.where(sel, 1.0, 0.0)
    earlier = _tri_masks(t)[0].astype(BF16)
    rank = jnp.dot(earlier, sel_f.astype(BF16), preferred_element_type=F32)
    rc = jnp.where(sel, rank, -1.0)
    rc_ref[...] = rc
    rr_ref[...] = rc.T[0:BF16_ROWS, :]
    cnt_ref[...] = jnp.broadcast_to(jnp.sum(sel_f, axis=0, keepdims=True), cnt_ref.shape).astype(jnp.int32)


def _router(x, g, router, tm):
    n = x.shape[0]
    return pl.pallas_call(
        _router_kernel,
        grid=(n // tm,),
        in_specs=[
            pl.BlockSpec((tm, D_MODEL), lambda i: (i, 0)),
            pl.BlockSpec((1, D_MODEL), lambda i: (0, 0)),
            pl.BlockSpec((D_MODEL, LANES), lambda i: (0, 0)),
        ],
        out_specs=[
            pl.BlockSpec((tm, D_MODEL), lambda i: (i, 0)),
            pl.BlockSpec((tm, LANES), lambda i: (i, 0)),
            pl.BlockSpec((tm, LANES), lambda i: (i, 0)),
            pl.BlockSpec((BF16_ROWS, tm), lambda i: (0, i)),
            pl.BlockSpec((8, LANES), lambda i: (i, 0)),
        ],
        out_shape=[
            jax.ShapeDtypeStruct((n, D_MODEL), BF16),
            jax.ShapeDtypeStruct((n, LANES), F32),
            jax.ShapeDtypeStruct((n, LANES), F32),
            jax.ShapeDtypeStruct((BF16_ROWS, n), F32),
            jax.ShapeDtypeStruct((n // tm * 8, LANES), jnp.int32),
        ],
        compiler_params=_cparams(("parallel",)),
        name="router",
    )(x, g, router)


def _moe_kernel(cnt_ref, h_ref, c_ref, rc_ref, rr_ref, wg_ref, wu_ref, wd_ref, o_ref, xg_scr, yg_scr, *, rows, parts):
    i = pl.program_id(0)
    e = pl.program_id(1)
    f = pl.program_id(2)
    last_f = f == pl.num_programs(2) - 1
    t = h_ref.shape[0] // parts
    nsub = xg_scr.shape[0]
    cnts = [cnt_ref[(i * parts + p) * N_EXPERTS + e] for p in range(parts)]
    most = functools.reduce(jnp.maximum, cnts)
    toks = [slice(p * t, (p + 1) * t) for p in range(parts)]
    part_rows = [slice(p * rows, (p + 1) * rows) for p in range(parts)]

    @pl.when(jnp.logical_and(e == 0, f == 0))
    def _():
        o_ref[...] = jnp.zeros_like(o_ref)

    @pl.when(f == 0)
    def _():
        for sb in range(nsub):
            @pl.when(most > sb * rows)
            def _():
                yg_scr[sb] = jnp.zeros(yg_scr.shape[1:], F32)
                for p in range(parts):
                    @pl.when(cnts[p] > sb * rows)
                    def _():
                        rr = rr_ref[pl.ds(e, 1), toks[p]]
                        slot = (lax.broadcasted_iota(jnp.int32, (rows, t), 0) + sb * rows).astype(F32)
                        gather = jnp.where(rr == slot, 1.0, 0.0).astype(BF16)
                        xg_scr[sb, part_rows[p], :] = jnp.dot(gather, h_ref[toks[p], :],
                                                              preferred_element_type=F32).astype(BF16)

                    @pl.when(cnts[p] <= sb * rows)
                    def _():
                        xg_scr[sb, part_rows[p], :] = jnp.zeros((rows, D_MODEL), BF16)

    for sb in range(nsub):
        @pl.when(most > sb * rows)
        def _():
            xb = xg_scr[sb]
            act = (_silu(jnp.dot(xb, wg_ref[0], preferred_element_type=F32))
                   * jnp.dot(xb, wu_ref[0], preferred_element_type=F32))
            yg_scr[sb] += _mm(act, wd_ref[0])

    @pl.when(last_f)
    def _():
        mine = lax.broadcasted_iota(jnp.int32, (t, LANES), 1) == e
        for p in range(parts):
            rc = jnp.sum(jnp.where(mine, rc_ref[toks[p], :], 0.0), axis=-1, keepdims=True)
            ce = jnp.sum(jnp.where(mine, c_ref[toks[p], :], 0.0), axis=-1, keepdims=True)
            for sb in range(nsub):
                @pl.when(cnts[p] > sb * rows)
                def _():
                    slot = (lax.broadcasted_iota(jnp.int32, (t, rows), 1) + sb * rows).astype(F32)
                    scatter = jnp.where(rc == slot, 1.0, 0.0).astype(BF16)
                    o_ref[toks[p], :] += ce * jnp.dot(scatter, yg_scr[sb, part_rows[p], :].astype(BF16),
                                                      preferred_element_type=F32)


def _moe(x, g, router, wg, wu, wd, tm, parts, tf, rows):
    n = x.shape[0]
    ff = wg.shape[2]
    h, c, rc, rr, cnt = _router(x, g, router, tm)
    cnt = cnt.reshape(n // tm, 8, LANES)[:, 0, :N_EXPERTS].reshape(-1)
    nsub = -(-tm // rows)
    tt = tm * parts
    tok = lambda cols, **kw: pl.BlockSpec((tt, cols), lambda i, e, f, cnt: (i, 0), **kw)
    once = dict(pipeline_mode=pl.Buffered(1))
    return pl.pallas_call(
        functools.partial(_moe_kernel, rows=rows, parts=parts),
        grid_spec=pltpu.PrefetchScalarGridSpec(
            num_scalar_prefetch=1,
            grid=(n // tt, N_EXPERTS, ff // tf),
            in_specs=[
                tok(D_MODEL, **once), tok(LANES), tok(LANES),
                pl.BlockSpec((BF16_ROWS, tt), lambda i, e, f, cnt: (0, i)),
                pl.BlockSpec((1, D_MODEL, tf), lambda i, e, f, cnt: (e, 0, f)),
                pl.BlockSpec((1, D_MODEL, tf), lambda i, e, f, cnt: (e, 0, f)),
                pl.BlockSpec((1, tf, D_MODEL), lambda i, e, f, cnt: (e, f, 0)),
            ],
            out_specs=tok(D_MODEL),
            scratch_shapes=[pltpu.VMEM((nsub, parts * rows, D_MODEL), BF16),
                            pltpu.VMEM((nsub, parts * rows, D_MODEL), F32)],
        ),
        out_shape=jax.ShapeDtypeStruct((n, D_MODEL), F32),
        compiler_params=_cparams(("parallel", "arbitrary", "arbitrary"), vmem_mb=58),
        name="moe",
    )(cnt, h, c, rc, rr, wg, wu, wd)


def _ple_kernel(x_ref, d_ref, p_ref, g_ref, wgate_ref, wproj_ref, fin_ref, o_ref, *, final):
    x = x_ref[...] + d_ref[...]
    h = _rms(x, g_ref[...], NORM_EPS)
    y = x + _sigmoid(_mm(h, wgate_ref[...])) * _mm(p_ref[...], wproj_ref[...])
    o_ref[...] = _rms(y, fin_ref[...], NORM_EPS) if final else y


def _ple(x, delta, p, g, wgate, wproj, fin, tm, final):
    n = x.shape[0]
    tok = lambda c: pl.BlockSpec((tm, c), lambda i: (i, 0))
    const = lambda r, c: pl.BlockSpec((r, c), lambda i: (0, 0))
    return pl.pallas_call(
        functools.partial(_ple_kernel, final=final),
        grid=(n // tm,),
        in_specs=[tok(D_MODEL), tok(D_MODEL), tok(P_DIM), const(1, D_MODEL), const(D_MODEL, D_MODEL),
                  const(P_DIM, D_MODEL), const(1, D_MODEL)],
        out_specs=tok(D_MODEL),
        out_shape=jax.ShapeDtypeStruct((n, D_MODEL), F32),
        compiler_params=_cparams(("parallel",)),
        name="ple",
    )(x, delta, p, g, wgate, wproj, fin)


def _tiles(n, seq):
    tm = min(512, seq)
    tm_big = 1024 if n % 1024 == 0 else tm
    tm_in = 2048 if n % 2048 == 0 else tm_big
    moe_rows = tm_big // 4 + tm_big // 32
    moe_parts = 2 if n % (2 * tm_big) == 0 else 1
    return dict(tm=tm, tm_big=tm_big, tm_in=tm_in, tn_in=1024, tq=min(512, seq), cblk=min(512, seq), moe_rows=moe_rows,
                moe_parts=moe_parts)


def _rope_tables(seq):
    half = ROPE_DIMS // 2
    inv = ROPE_THETA ** (-jnp.arange(half, dtype=F32) * 2.0 / ROPE_DIMS)
    ang = jnp.arange(seq, dtype=F32)[:, None] * inv[None, :]
    pad = jnp.zeros((seq, DIFF_DH - ROPE_DIMS), F32)
    cos = jnp.concatenate([jnp.cos(ang), jnp.cos(ang), pad + 1.0], axis=-1)
    sin = jnp.concatenate([jnp.sin(ang), jnp.sin(ang), pad], axis=-1)
    d = jnp.arange(W_MIX)
    dd = d % DIFF_DH
    src = jnp.where(dd < half, d + half, d - half)
    sign = jnp.where(dd < half, -1.0, jnp.where(dd < ROPE_DIMS, 1.0, 0.0))
    rot = jnp.zeros((W_MIX, W_MIX), F32).at[src, d].set(sign)
    reps = W_MIX // DIFF_DH
    return jnp.tile(cos, (1, reps)), jnp.tile(sin, (1, reps)), rot.astype(BF16)


def _split_w_in(w):
    a0 = 0
    b0 = a0 + 4 * W_MIX
    c0 = b0 + 3 * W_MIX
    d0 = c0 + 3 * W_MIX + N_HEADS
    g0 = d0 + 4 * W_MIX + 2 * N_HEADS
    d_small = d0 + 3 * W_MIX
    diff_q = w[:, b0:b0 + W_MIX] * (DIFF_DH ** -0.5 * LOG2E)
    fox_q = w[:, c0:c0 + W_MIX] * (HEAD_DIM ** -0.5 * LOG2E)
    main = jnp.concatenate([
        w[:, g0:], w[:, a0:b0], w[:, d0:d_small], w[:, d_small + 2 * N_HEADS:g0],
        diff_q, w[:, b0 + W_MIX:b0 + 2 * W_MIX], fox_q, w[:, c0 + W_MIX:c0 + 2 * W_MIX],
    ], axis=1).astype(BF16)
    small = jnp.concatenate([
        w[:, c0 + 3 * W_MIX:d0], w[:, d_small:d_small + 2 * N_HEADS],
        jnp.zeros((D_MODEL, LANES - 3 * N_HEADS), w.dtype),
    ], axis=1).astype(BF16)
    v_t = jnp.concatenate([w[:, b0 + 2 * W_MIX:c0], w[:, c0 + 2 * W_MIX:c0 + 3 * W_MIX]], axis=1).T.astype(BF16)
    return main, small, v_t


def _small_params(fbias, a_log, dt_bias):
    zeros = jnp.zeros((N_HEADS,), F32)
    bias = jnp.concatenate([fbias, zeros, dt_bias, jnp.zeros((LANES - 3 * N_HEADS,), F32)])
    neg_a = jnp.concatenate([zeros, zeros, -jnp.exp(a_log), jnp.zeros((LANES - 3 * N_HEADS,), F32)])
    return bias.reshape(1, LANES), neg_a.reshape(1, LANES)


def _pad_rows(w, top, total):
    return jnp.concatenate([jnp.zeros((top, w.shape[1]), w.dtype), w,
                            jnp.zeros((total - top - w.shape[0], w.shape[1]), w.dtype)], axis=0)


def kernel(x, p, norm_mix, norm_ffn, norm_ple, w_in, w_bo, w_out, rwkv_mu, rwkv_w0, rwkv_w2, rwkv_a0, rwkv_a2, rwkv_g2, rwkv_kk, rwkv_ka, rwkv_rk, rwkv_ln_w, rwkv_ln_b, diff_lam, diff_subln, fox_fbias, gdn_conv, gdn_a_log, gdn_dt_bias, gdn_norm, ffn_w_gate, ffn_w_up, ffn_w_down, moe_router, moe_w_gate, moe_w_up, moe_w_down, ple_proj, ple_gate, final_norm):
    batch, seq, _ = x.shape
    depth = w_in.shape[0]
    n = batch * seq
    t = _tiles(n, seq)
    tm, tq, cblk = t["tm"], t["tq"], t["cblk"]
    row = lambda v: v.reshape(1, -1).astype(F32)
    cos, sin, rot = _rope_tables(seq)
    xf = x.reshape(n, D_MODEL)
    pf = p.reshape(depth, n, P_DIM)

    for i in range(depth):
        w_main, w_small, w_vt = _split_w_in(w_in[i])
        u, scol, vt = _inproj(xf, row(norm_mix[i]), w_main, w_small, w_vt, t["tm_in"], t["tn_in"])
        bias, neg_a = _small_params(fox_fbias[i], gdn_a_log[i], gdn_dt_bias[i])
        hcol, fox_kb, fox_qb = _small_prep(scol, bias, neg_a, batch, seq, tm)

        scan_in, pc, post = _rwkv_prep(
            u, row(rwkv_mu[i]), row(rwkv_w0[i]), _pad_rows(rwkv_w2[i], 0, LANES), row(rwkv_a0[i]),
            _pad_rows(rwkv_a2[i], LANES // 2, LANES), rwkv_g2[i], row(rwkv_kk[i]), row(rwkv_ka[i]), row(rwkv_rk[i]),
            seq, tm)
        o_a = _rwkv_chunk(scan_in, pc, batch, seq, cblk)

        o_d = _gdn_chunk(_gdn_prep(u, gdn_conv[i].T, seq, tm), hcol, jnp.tile(row(gdn_norm[i]), (1, N_HEADS)),
                         batch, seq, cblk)

        lam_init = 0.8 - 0.6 * math.exp(-0.3 * i)
        y_b = _diff_attention(u, vt, cos, sin, rot, diff_lam[i].astype(F32), jnp.tile(row(diff_subln[i]), (1, 2)),
                              batch, seq, tq, lam_init)
        y_c = _fox_attention(u, vt, fox_qb, fox_kb, batch, seq, tq)

        xf = _merge(xf, u, o_a, post, y_b, y_c, o_d, row(rwkv_ln_w[i]), row(rwkv_ln_b[i]),
                    w_bo[i].astype(BF16), w_out[i].astype(BF16), tm)

        j = i // 2
        if i % 2 == 0:
            delta = _ffn(xf, row(norm_ffn[i]), ffn_w_gate[j].astype(BF16), ffn_w_up[j].astype(BF16),
                         ffn_w_down[j].astype(BF16), tm, ffn_w_gate.shape[2] // 2)
        else:
            router = jnp.concatenate([moe_router[j], jnp.zeros((D_MODEL, LANES - N_EXPERTS), F32)], axis=1)
            delta = _moe(xf, row(norm_ffn[i]), router, moe_w_gate[j].astype(BF16), moe_w_up[j].astype(BF16),
                         moe_w_down[j].astype(BF16), t["tm_big"], t["moe_parts"], moe_w_gate.shape[3] // 4,
                         t["moe_rows"])
        xf = _ple(xf, delta, pf[i], row(norm_ple[i]), ple_gate[i].astype(BF16), ple_proj[i].astype(BF16),
                  row(final_norm), tm, i == depth - 1)
    return xf.reshape(batch, seq, D_MODEL)
```

```python
import functools
import math

import jax
import jax.numpy as jnp
import numpy as np
from jax import lax
from jax.experimental import pallas as pl
from jax.experimental.pallas import tpu as pltpu

F32 = jnp.float32
BF16 = jnp.bfloat16
HIGHEST = lax.Precision.HIGHEST

D_MODEL = 1024
P_DIM = 256
W_MIX = 256
HEAD_DIM = 64
N_HEADS = 4
DIFF_DH = 32
ROPE_THETA = 500000.0
ROPE_DIMS = 8
RWKV_GN_EPS = 64e-5
DIFF_LN_EPS = 1e-5
GDN_CONV = 4
CHUNK = 64
CHUNK_UNROLL = 4
N_EXPERTS = 8
NORM_EPS = 1e-6
L2_EPS = 1e-6
LOG2E = math.log2(math.e)
LANES = 128
BF16_ROWS = 16
assert CHUNK == HEAD_DIM

U_GATE = 0
U_RWKV = 4096
U_GDN = 5120
U_DIFF = 6144
U_FOX = 6656
U_COLS = 7168
SM_FOX, SM_BETA, SM_DEC = 0, 4, 8
FOX_CK, FOX_CQ = 0, 16


def _cparams(semantics, vmem_mb=48):
    return pltpu.CompilerParams(dimension_semantics=semantics, vmem_limit_bytes=vmem_mb * 1024 * 1024)


def _mm(a, b):
    return jnp.dot(a.astype(BF16), b.astype(BF16), preferred_element_type=F32)


def _mm_nt(a, b):
    return lax.dot_general(a.astype(BF16), b.astype(BF16), (((1,), (1,)), ((), ())), preferred_element_type=F32)


def _mm_tn(a, b):
    return lax.dot_general(a.astype(BF16), b.astype(BF16), (((0,), (0,)), ((), ())), preferred_element_type=F32)


def _mm_f32(a, b):
    return jnp.dot(a, b, preferred_element_type=F32, precision=HIGHEST)


def _mm_mask(mask, x):
    hi = x.astype(BF16)
    r1 = x - hi.astype(F32)
    mid = r1.astype(BF16)
    lo = (r1 - mid.astype(F32)).astype(BF16)
    dot = lambda t: jnp.dot(mask, t, preferred_element_type=F32)
    return dot(hi) + dot(mid) + dot(lo)


def _rms(x, g, eps):
    return x * lax.rsqrt(jnp.mean(x * x, axis=-1, keepdims=True) + eps) * g


def _sigmoid(x):
    return 0.5 * jnp.tanh(0.5 * x) + 0.5


def _silu(x):
    return x * _sigmoid(x)


def _softplus(x):
    return jnp.maximum(x, 0.0) + jnp.log(1.0 + jnp.exp(-jnp.abs(x)))


def _tri_masks(c):
    ii = lax.broadcasted_iota(jnp.int32, (c, c), 0)
    jj = lax.broadcasted_iota(jnp.int32, (c, c), 1)
    return ii > jj, ii >= jj, ii == jj


def _block_masks():
    ii = lax.broadcasted_iota(jnp.int32, (W_MIX, W_MIX), 0)
    jj = lax.broadcasted_iota(jnp.int32, (W_MIX, W_MIX), 1)
    return (ii // HEAD_DIM) == (jj // HEAD_DIM), ii == jj


def _wide_masks():
    ii = lax.broadcasted_iota(jnp.int32, (CHUNK, W_MIX), 0)
    jj = lax.broadcasted_iota(jnp.int32, (CHUNK, W_MIX), 1) % HEAD_DIM
    return ii > jj, ii >= jj, ii == jj


def _stack_heads(x, same):
    return jnp.where(same, jnp.concatenate([x, x, x, x], axis=0), jnp.zeros((), x.dtype))


def _unit_lower_inverses(ns, eye_w, same):
    rs = [eye_w + n for n in ns]
    ps = list(ns)
    for _ in range(int(math.log2(CHUNK)) - 1):
        ps = [_mm(p, _stack_heads(p.astype(BF16), same)) for p in ps]
        rs = [r + _mm(r, _stack_heads(p.astype(BF16), same)) for r, p in zip(rs, ps)]
    return rs


def _unstack_heads(x):
    return x[0:CHUNK] + x[CHUNK:2 * CHUNK] + x[2 * CHUNK:3 * CHUNK] + x[3 * CHUNK:4 * CHUNK]


def _inproj_kernel(x_ref, g_ref, w_ref, ws_ref, wvt_ref, u_ref, scol_ref, vt_ref, h_scr):
    @pl.when(pl.program_id(1) == 0)
    def _():
        hb = _rms(x_ref[...], g_ref[...], NORM_EPS).astype(BF16)
        h_scr[...] = hb
        scol_ref[...] = jnp.dot(hb, ws_ref[...], preferred_element_type=F32)
        vt_ref[...] = lax.dot_general(wvt_ref[...], hb, (((1,), (1,)), ((), ())),
                                      preferred_element_type=F32).astype(BF16)

    u_ref[...] = jnp.dot(h_scr[...], w_ref[...], preferred_element_type=F32).astype(BF16)


def _inproj(x, g, w, ws, wvt, tm, tn):
    n = x.shape[0]
    return pl.pallas_call(
        _inproj_kernel,
        grid=(n // tm, U_COLS // tn),
        in_specs=[
            pl.BlockSpec((tm, D_MODEL), lambda i, j: (i, 0)),
            pl.BlockSpec((1, D_MODEL), lambda i, j: (0, 0)),
            pl.BlockSpec((D_MODEL, tn), lambda i, j: (0, j)),
            pl.BlockSpec((D_MODEL, LANES), lambda i, j: (0, 0)),
            pl.BlockSpec((2 * W_MIX, D_MODEL), lambda i, j: (0, 0)),
        ],
        out_specs=[
            pl.BlockSpec((tm, tn), lambda i, j: (i, j)),
            pl.BlockSpec((tm, LANES), lambda i, j: (i, 0)),
            pl.BlockSpec((2 * W_MIX, tm), lambda i, j: (0, i)),
        ],
        out_shape=[
            jax.ShapeDtypeStruct((n, U_COLS), BF16),
            jax.ShapeDtypeStruct((n, LANES), F32),
            jax.ShapeDtypeStruct((2 * W_MIX, n), BF16),
        ],
        scratch_shapes=[pltpu.VMEM((tm, D_MODEL), BF16)],
        compiler_params=_cparams(("parallel", "arbitrary"), vmem_mb=56),
        name="inproj",
    )(x, g, w, ws, wvt)


def _lane_placement(base):
    r = lax.broadcasted_iota(jnp.int32, (3 * LANES, LANES), 0)
    m = lax.broadcasted_iota(jnp.int32, (3 * LANES, LANES), 1)
    head, part = r % LANES, r // LANES
    return jnp.logical_and(head < N_HEADS, m == base + 3 * head + part).astype(BF16)


def _split3(x):
    hi = x.astype(BF16)
    r1 = x - hi.astype(F32)
    mid = r1.astype(BF16)
    lo = (r1 - mid.astype(F32)).astype(BF16)
    return jnp.concatenate([hi, mid, lo], axis=1)


def _head_expansion():
    r = lax.broadcasted_iota(jnp.int32, (3 * LANES, 2 * W_MIX), 0) % LANES
    m = lax.broadcasted_iota(jnp.int32, (3 * LANES, 2 * W_MIX), 1)
    src = jnp.where(m < W_MIX, SM_BETA + m // HEAD_DIM, SM_DEC + (m - W_MIX) // HEAD_DIM)
    return (r == src).astype(BF16)


def _small_prep_kernel(scol_ref, bias_ref, nega_ref, ogb_ref, okb_ref, oqb_ref, carry_scr):
    @pl.when(pl.program_id(1) == 0)
    def _():
        carry_scr[...] = jnp.zeros_like(carry_scr)

    tm = scol_ref.shape[0]
    lower = _tri_masks(tm)[1].astype(BF16)
    z = scol_ref[...] + bias_ref[...]
    lane = lax.broadcasted_iota(jnp.int32, z.shape, 1)
    is_f = lane < SM_BETA
    is_b = jnp.logical_and(lane >= SM_BETA, lane < SM_DEC)
    is_d = jnp.logical_and(lane >= SM_DEC, lane < SM_DEC + N_HEADS)
    logf = jnp.where(is_f, jnp.minimum(z, 0.0) - jnp.log(1.0 + jnp.exp(-jnp.abs(z))), 0.0)
    gdn = jnp.where(is_b, _sigmoid(z), jnp.where(is_d, nega_ref[...] * _softplus(z), 0.0))
    ogb_ref[...] = jnp.dot(_split3(gdn), _head_expansion(), preferred_element_type=F32)
    cum = _mm_mask(lower, logf) + carry_scr[...]
    carry_scr[...] = cum[tm - 1:tm, :]
    parts = _split3(cum * LOG2E)
    ones_lanes = jnp.logical_and(lane >= FOX_CQ, lane < FOX_CQ + 3 * N_HEADS)
    okb_ref[...] = (jnp.dot(parts, _lane_placement(FOX_CK), preferred_element_type=F32)
                    + jnp.where(ones_lanes, 1.0, 0.0)).astype(BF16)
    oqb_ref[...] = jnp.dot(parts, _lane_placement(FOX_CQ), preferred_element_type=F32).astype(BF16)


def _small_prep(scol, bias, neg_a, batch, seq, tm):
    n = batch * seq
    nt = seq // tm
    tok = lambda: pl.BlockSpec((tm, LANES), lambda b, j: (b * nt + j, 0))
    const = lambda: pl.BlockSpec((1, LANES), lambda b, j: (0, 0))
    return pl.pallas_call(
        _small_prep_kernel,
        grid=(batch, nt),
        in_specs=[tok(), const(), const()],
        out_specs=[pl.BlockSpec((tm, 2 * W_MIX), lambda b, j: (b * nt + j, 0)), tok(), tok()],
        out_shape=[jax.ShapeDtypeStruct((n, 2 * W_MIX), F32), jax.ShapeDtypeStruct((n, LANES), BF16),
                   jax.ShapeDtypeStruct((n, LANES), BF16)],
        scratch_shapes=[pltpu.VMEM((1, LANES), F32)],
        compiler_params=_cparams(("parallel", "arbitrary")),
        name="small_prep",
    )(scol, bias, neg_a)


def _rwkv_prep_kernel(u_ref, up_ref, mu_ref, w0_ref, w2_ref, a0_ref, a2_ref, g2_ref, kk_ref, ka_ref, rk_ref,
                      scan_ref, pc_ref, post_ref, *, tiles_per_seq):
    tm = u_ref.shape[0]
    u = u_ref[...].astype(F32)
    prev = up_ref[...].astype(F32)[BF16_ROWS - 1:BF16_ROWS, :]
    prev = jnp.where(pl.program_id(0) % tiles_per_seq == 0, 0.0, prev)
    rows = lax.broadcasted_iota(jnp.int32, (tm, 1), 0)
    u_prev = jnp.where(rows == 0, prev, pltpu.roll(u, 1, 0))
    xm = u + (u_prev - u) * mu_ref[...]
    r = xm[:, 0:W_MIX]
    k = xm[:, W_MIX:2 * W_MIX]
    v = xm[:, 2 * W_MIX:3 * W_MIX]
    x_lora = xm[:, 3 * W_MIX:3 * W_MIX + LANES]
    xg = xm[:, 3 * W_MIX + LANES:]
    logw = -_softplus(-(w0_ref[...] + _mm(jnp.tanh(x_lora), w2_ref[...]))) - 0.5
    log_decay = -jnp.exp(logw)
    a = _sigmoid(a0_ref[...] + _mm(x_lora, a2_ref[...]))
    g = _mm(_sigmoid(xg), g2_ref[...])
    same = _block_masks()[0].astype(F32)
    kk_raw = k * kk_ref[...]
    kk = kk_raw * lax.rsqrt(_mm_f32(kk_raw * kk_raw, same) + L2_EPS)
    k2 = k * (1.0 + (a - 1.0) * ka_ref[...])
    bonus = _mm_f32(r * k2 * rk_ref[...], same) * v
    ti = lax.broadcasted_iota(jnp.int32, (tm, tm), 0)
    tj = lax.broadcasted_iota(jnp.int32, (tm, tm), 1)
    in_chunk = jnp.logical_and(ti // CHUNK == tj // CHUNK, ti >= tj).astype(BF16)
    ci = lax.broadcasted_iota(jnp.int32, (tm // CHUNK, tm), 0)
    cj = lax.broadcasted_iota(jnp.int32, (tm // CHUNK, tm), 1)
    cum = _mm_mask(in_chunk, log_decay)
    cum_end = _mm_mask((ci == cj // CHUNK).astype(BF16), log_decay)
    inv = jnp.exp(-cum)
    scan_ref[0] = (-kk * jnp.exp(cum - log_decay)).astype(BF16)
    scan_ref[1] = (kk * a * inv).astype(BF16)
    scan_ref[2] = (k2 * inv).astype(BF16)
    scan_ref[3] = (r * jnp.exp(cum)).astype(BF16)
    scan_ref[4] = v.astype(BF16)
    pc_ref[...] = jnp.exp(cum_end)
    post_ref[0] = g
    post_ref[1] = bonus


def _rwkv_prep(u, mu, w0, w2p, a0, a2p, g2, k_k, k_a, r_k, seq, tm):
    n = u.shape[0]
    ublk = U_RWKV // D_MODEL
    row = lambda c: pl.BlockSpec((1, c), lambda i: (0, 0))
    mat = lambda r: pl.BlockSpec((r, W_MIX), lambda i: (0, 0))
    return pl.pallas_call(
        functools.partial(_rwkv_prep_kernel, tiles_per_seq=seq // tm),
        grid=(n // tm,),
        in_specs=[
            pl.BlockSpec((tm, D_MODEL), lambda i: (i, ublk)),
            pl.BlockSpec((BF16_ROWS, D_MODEL), lambda i: (jnp.maximum(i * (tm // BF16_ROWS) - 1, 0), ublk)),
            row(D_MODEL), row(W_MIX), mat(LANES), row(W_MIX), mat(LANES), mat(LANES), row(W_MIX), row(W_MIX), row(W_MIX),
        ],
        out_specs=[
            pl.BlockSpec((5, tm, W_MIX), lambda i: (0, i, 0)),
            pl.BlockSpec((tm // CHUNK, W_MIX), lambda i: (i, 0)),
            pl.BlockSpec((2, tm, W_MIX), lambda i: (0, i, 0)),
        ],
        out_shape=[jax.ShapeDtypeStruct((5, n, W_MIX), BF16), jax.ShapeDtypeStruct((n // CHUNK, W_MIX), F32),
                   jax.ShapeDtypeStruct((2, n, W_MIX), F32)],
        compiler_params=_cparams(("parallel",)),
        name="rwkv_prep",
    )(u, u, mu, w0, w2p, a0, a2p, g2, k_k, k_a, r_k)


def _rwkv_chunk_kernel(x_ref, pc_ref, o_ref, s_scr, *, nchunk, nbatch):
    @pl.when(pl.program_id(0) == 0)
    def _():
        s_scr[...] = jnp.zeros_like(s_scr)

    same, diag = _block_masks()
    eye = diag.astype(F32)
    strict_w, incl_w, diag_w = _wide_masks()
    eye_w = diag_w.astype(F32)
    stack = lambda x: _stack_heads(x.astype(BF16), same)

    def chunks(i, carry):
        items = [(b, i * CHUNK_UNROLL + j) for j in range(CHUNK_UNROLL) for b in range(nbatch)]
        each = lambda f, *lists: [f(*args) for args in zip(*lists)]
        sls = [pl.ds(pl.multiple_of(ci * CHUNK, CHUNK), CHUNK) for _, ci in items]
        a, bb, k, r, v = ([x_ref[i, b, sl, :] for (b, _), sl in zip(items, sls)] for i in range(5))
        pc = [pc_ref[b, pl.ds(ci, 1), :] for b, ci in items]
        a_s, b_s, k_s, v_s = (each(stack, x) for x in (a, bb, k, v))
        m_ab = each(lambda x, y: jnp.where(strict_w, _mm_nt(x, y), 0.0), a, b_s)
        m_ak = each(lambda x, y: jnp.where(strict_w, _mm_nt(x, y), 0.0), a, k_s)
        n_rb = each(lambda x, y: jnp.where(incl_w, _mm_nt(x, y), 0.0), r, b_s)
        n_rk = each(lambda x, y: jnp.where(incl_w, _mm_nt(x, y), 0.0), r, k_s)
        t_inv = _unit_lower_inverses(m_ab, eye_w, same)
        a2 = each(_mm, t_inv, a_s)
        u0 = each(lambda t, m, x: _mm(t, stack(_mm(m, x))), t_inv, m_ak, v_s)
        r2 = each(lambda x, n, y: x.astype(F32) + _mm(n, stack(y)), r, n_rb, a2)
        o0 = each(lambda n1, u, n2, x: _mm(n1, stack(u)) + _mm(n2, x), n_rb, u0, n_rk, v_s)
        b_end = each(lambda x, p: x.astype(F32) * p, bb, pc)
        k_end = each(lambda x, p: x.astype(F32) * p, k, pc)
        g_mat = each(lambda p, x, y: eye * p + jnp.where(same, _mm_tn(x, y), 0.0), pc, a2, b_end)
        s0 = each(lambda u, x, y, z: jnp.where(same, _mm_tn(u, x) + _mm_tn(y, z), 0.0), u0, b_end, v, k_end)
        for n, ((b, _), sl) in enumerate(zip(items, sls)):
            s = s_scr[b]
            o = _stack_heads(_mm_nt(r2[n], s) + o0[n], same)
            s_scr[b] = _mm(s, g_mat[n]) + s0[n]
            mean = jnp.sum(o, axis=-1, keepdims=True) * (1.0 / HEAD_DIM)
            cen = jnp.where(same, o - mean, 0.0)
            var = jnp.sum(cen * cen, axis=-1, keepdims=True) * (1.0 / HEAD_DIM)
            o_ref[b, sl, :] = _unstack_heads(cen * lax.rsqrt(var + RWKV_GN_EPS))
        return carry

    lax.fori_loop(0, nchunk // CHUNK_UNROLL, chunks, 0)


def _rwkv_chunk(xs, pc, batch, seq, cblk):
    xs = xs.reshape(5, batch, seq, W_MIX)
    pc = pc.reshape(batch, seq // CHUNK, W_MIX)
    out = pl.pallas_call(
        functools.partial(_rwkv_chunk_kernel, nchunk=cblk // CHUNK, nbatch=batch),
        grid=(seq // cblk,),
        in_specs=[
            pl.BlockSpec((5, batch, cblk, W_MIX), lambda j: (0, 0, j, 0)),
            pl.BlockSpec((batch, cblk // CHUNK, W_MIX), lambda j: (0, j, 0)),
        ],
        out_specs=pl.BlockSpec((batch, cblk, W_MIX), lambda j: (0, j, 0)),
        out_shape=jax.ShapeDtypeStruct((batch, seq, W_MIX), F32),
        scratch_shapes=[pltpu.VMEM((batch, W_MIX, W_MIX), F32)],
        compiler_params=_cparams(("arbitrary",)),
        name="rwkv_chunk",
    )(xs, pc)
    return out.reshape(batch * seq, W_MIX)


def _gdn_prep_kernel(u_ref, up_ref, cw_ref, o_ref, ext_scr, *, tiles_per_seq):
    tm = u_ref.shape[0]
    c3 = 3 * W_MIX
    prev = up_ref[...].astype(F32)[:, :c3]
    ext_scr[0:BF16_ROWS, :] = jnp.where(pl.program_id(0) % tiles_per_seq == 0, 0.0, prev)
    ext_scr[BF16_ROWS:, :] = u_ref[...].astype(F32)[:, :c3]
    y = jnp.zeros((tm, c3), F32)
    for j in range(GDN_CONV):
        y = y + ext_scr[pl.ds(BF16_ROWS - (GDN_CONV - 1) + j, tm), :] * cw_ref[j:j + 1, :]
    y = _silu(y)
    same = _block_masks()[0].astype(F32)
    q = y[:, 0:W_MIX]
    k = y[:, W_MIX:2 * W_MIX]
    o_ref[0] = (q * lax.rsqrt(_mm_f32(q * q, same) + L2_EPS) * (HEAD_DIM ** -0.5)).astype(BF16)
    o_ref[1] = (k * lax.rsqrt(_mm_f32(k * k, same) + L2_EPS)).astype(BF16)
    o_ref[2] = y[:, 2 * W_MIX:].astype(BF16)


def _gdn_prep(u, conv_w, seq, tm):
    n = u.shape[0]
    ublk = U_GDN // D_MODEL
    return pl.pallas_call(
        functools.partial(_gdn_prep_kernel, tiles_per_seq=seq // tm),
        grid=(n // tm,),
        in_specs=[
            pl.BlockSpec((tm, D_MODEL), lambda i: (i, ublk)),
            pl.BlockSpec((BF16_ROWS, D_MODEL), lambda i: (jnp.maximum(i * (tm // BF16_ROWS) - 1, 0), ublk)),
            pl.BlockSpec((GDN_CONV, 3 * W_MIX), lambda i: (0, 0)),
        ],
        out_specs=pl.BlockSpec((3, tm, W_MIX), lambda i: (0, i, 0)),
        out_shape=jax.ShapeDtypeStruct((3, n, W_MIX), BF16),
        scratch_shapes=[pltpu.VMEM((tm + BF16_ROWS, 3 * W_MIX), F32)],
        compiler_params=_cparams(("parallel",)),
        name="gdn_prep",
    )(u, u, conv_w)


def _gdn_chunk_kernel(x_ref, gb_ref, nw_ref, o_ref, s_scr, *, nchunk, nbatch):
    @pl.when(pl.program_id(0) == 0)
    def _():
        s_scr[...] = jnp.zeros_like(s_scr)

    same, diag = _block_masks()
    eye = diag.astype(F32)
    strict_w, incl_w, diag_w = _wide_masks()
    eye_w = diag_w.astype(F32)
    lower = _tri_masks(CHUNK)[1].astype(BF16)
    stack = lambda x: _stack_heads(x.astype(BF16), same)

    def chunks(i, carry):
        items = [(b, i * CHUNK_UNROLL + j) for j in range(CHUNK_UNROLL) for b in range(nbatch)]
        each = lambda f, *lists: [f(*args) for args in zip(*lists)]
        sls = [pl.ds(pl.multiple_of(ci * CHUNK, CHUNK), CHUNK) for _, ci in items]
        q, k, v = ([x_ref[i, b, sl, :] for (b, _), sl in zip(items, sls)] for i in range(3))
        beta = [gb_ref[b, sl, 0:W_MIX] for (b, _), sl in zip(items, sls)]
        g = [gb_ref[b, sl, W_MIX:2 * W_MIX] for (b, _), sl in zip(items, sls)]
        k_s = each(stack, k)
        gam = [_mm_mask(lower, x) for x in g]
        gam_end = [x[CHUNK - 1:CHUNK, :] for x in gam]
        gdiff = [_mm_mask(lower, jnp.where(strict_w, x, 0.0)) for x in g]
        decay = [jnp.exp(jnp.where(incl_w, x, -jnp.inf)) for x in gdiff]
        a_mat = each(lambda bt, d, x, y: jnp.where(strict_w, bt * d * _mm_nt(x, y), 0.0), beta, decay, k, k_s)
        t_inv = _unit_lower_inverses([-a for a in a_mat], eye_w, same)
        e_gam = [jnp.exp(x) for x in gam]
        u0 = each(lambda t, bt, x: _mm(t, stack(bt * x.astype(F32))), t_inv, beta, v)
        wm = each(lambda t, bt, e, x: _mm(t, stack((bt * e) * x.astype(F32))), t_inv, beta, e_gam, k)
        qk = each(lambda x, y, d: _mm_nt(x, y) * d, q, k_s, decay)
        q2 = each(lambda e, x, a, w: e * x.astype(F32) - _mm(a, stack(w)), e_gam, q, qk, wm)
        o0 = each(lambda a, u: _mm(a, stack(u)), qk, u0)
        k_end = each(lambda x, ge, ga: x.astype(F32) * jnp.exp(ge - ga), k, gam_end, gam)
        g_mat = each(lambda ge, x, w: eye * jnp.exp(ge) - jnp.where(same, _mm_tn(x, w), 0.0), gam_end, k_end, wm)
        s0 = each(lambda x, u: jnp.where(same, _mm_tn(x, u), 0.0), k_end, u0)
        for n, ((b, _), sl) in enumerate(zip(items, sls)):
            s = s_scr[b]
            o = _stack_heads(_mm(q2[n], s) + o0[n], same)
            s_scr[b] = _mm(g_mat[n], s) + s0[n]
            ms = jnp.sum(o * o, axis=-1, keepdims=True) * (1.0 / HEAD_DIM)
            o_ref[b, sl, :] = _unstack_heads(o * lax.rsqrt(ms + NORM_EPS)) * nw_ref[...]
        return carry

    lax.fori_loop(0, nchunk // CHUNK_UNROLL, chunks, 0)


def _gdn_chunk(xs, gb, norm_w, batch, seq, cblk):
    xs = xs.reshape(3, batch, seq, W_MIX)
    gb = gb.reshape(batch, seq, 2 * W_MIX)
    out = pl.pallas_call(
        functools.partial(_gdn_chunk_kernel, nchunk=cblk // CHUNK, nbatch=batch),
        grid=(seq // cblk,),
        in_specs=[
            pl.BlockSpec((3, batch, cblk, W_MIX), lambda j: (0, 0, j, 0)),
            pl.BlockSpec((batch, cblk, 2 * W_MIX), lambda j: (0, j, 0)),
            pl.BlockSpec((1, W_MIX), lambda j: (0, 0)),
        ],
        out_specs=pl.BlockSpec((batch, cblk, W_MIX), lambda j: (0, j, 0)),
        out_shape=jax.ShapeDtypeStruct((batch, seq, W_MIX), F32),
        scratch_shapes=[pltpu.VMEM((batch, W_MIX, W_MIX), F32)],
        compiler_params=_cparams(("arbitrary",)),
        name="gdn_chunk",
    )(xs, gb, norm_w)
    return out.reshape(batch * seq, W_MIX)


def _causal_pairs(nq):
    pairs = [(i, j) for i in range(nq) for j in range(i + 1)]
    return jnp.asarray(np.array([p[0] for p in pairs], np.int32)), jnp.asarray(np.array([p[1] for p in pairs], np.int32))


def _softmax_updates(scores, vt_ones, m_scr, acc_scr):
    probs = []
    for i, s in enumerate(scores):
        m_old = m_scr[i]
        m_new = jnp.maximum(m_old, jnp.max(s, axis=0, keepdims=True))
        m_scr[i] = m_new
        probs.append((jnp.exp2(m_old - m_new), jnp.exp2((s - m_new).astype(BF16))))
    for i, (alpha, p) in enumerate(probs):
        acc_scr[i] = alpha * acc_scr[i] + jnp.dot(vt_ones[i], p, preferred_element_type=F32)


def _pair_lanes(h):
    p = h // 2
    return slice(p * LANES, (p + 1) * LANES), h % 2 == 0


ACC_ROWS = HEAD_DIM + BF16_ROWS


def _values_and_ones(vt, tk):
    ones = jnp.ones((BF16_ROWS, tk), vt.dtype)
    return [jnp.concatenate([vt[h * HEAD_DIM:(h + 1) * HEAD_DIM, :], ones], axis=0) for h in range(N_HEADS)]


def _normalized(acc):
    return acc[0:HEAD_DIM] / acc[HEAD_DIM:HEAD_DIM + 1]


def _key_after_query(tq):
    return lax.broadcasted_iota(jnp.int32, (tq, tq), 0) > lax.broadcasted_iota(jnp.int32, (tq, tq), 1)


def _fox_kernel(qi_ref, kj_ref, q_ref, k_ref, vt_ref, qb_ref, kb_ref, o_ref, qm_scr, m_scr, acc_scr):
    t = pl.program_id(1)
    qi = qi_ref[t]
    kj = kj_ref[t]
    tq = q_ref.shape[0]

    @pl.when(kj == 0)
    def _():
        m_scr[...] = jnp.full_like(m_scr, -jnp.inf)
        acc_scr[...] = jnp.zeros_like(acc_scr)
        q = q_ref[...]
        qb = qb_ref[...]
        lane = lax.broadcasted_iota(jnp.int32, (tq, LANES), 1)
        zero = jnp.zeros((), BF16)
        for h in range(N_HEADS):
            slab, low = _pair_lanes(h)
            mine = lane < HEAD_DIM if low else lane >= HEAD_DIM
            ck_lanes = jnp.logical_and(lane >= FOX_CK + 3 * h, lane < FOX_CK + 3 * h + 3)
            cq_lanes = jnp.logical_and(lane >= FOX_CQ + 3 * h, lane < FOX_CQ + 3 * h + 3)
            bias = jnp.where(ck_lanes, -jnp.ones((), BF16), jnp.where(cq_lanes, qb, zero))
            qm_scr[h] = jnp.concatenate([jnp.where(mine, q[:, slab], zero), bias], axis=1)

    def step(diagonal):
        k = k_ref[...]
        kb = kb_ref[...]
        vts = _values_and_ones(vt_ref[...], tq)
        if diagonal:
            masked = _key_after_query(tq)
        keys = [jnp.concatenate([k[:, p * LANES:(p + 1) * LANES], kb], axis=1) for p in range(N_HEADS // 2)]
        scores = []
        for h in range(N_HEADS):
            s = lax.dot_general(keys[h // 2], qm_scr[h], (((1,), (1,)), ((), ())), preferred_element_type=F32)
            if diagonal:
                s = jnp.where(masked, -jnp.inf, s)
            scores.append(s)
        _softmax_updates(scores, vts, m_scr, acc_scr)

    @pl.when(kj < qi)
    def _():
        step(False)

    @pl.when(kj == qi)
    def _():
        step(True)
        for p in range(N_HEADS // 2):
            pair = jnp.concatenate([_normalized(acc_scr[2 * p]), _normalized(acc_scr[2 * p + 1])], axis=0)
            o_ref[:, p * LANES:(p + 1) * LANES] = pair.T


def _attn_specs(nq, tq, ucol, vt_rows):
    cb = ucol // W_MIX
    q_spec = pl.BlockSpec((tq, W_MIX), lambda b, t, qi, kj: (b * nq + qi[t], cb))
    k_spec = pl.BlockSpec((tq, W_MIX), lambda b, t, qi, kj: (b * nq + kj[t], cb + 1))
    vt_spec = pl.BlockSpec((W_MIX, tq), lambda b, t, qi, kj: (vt_rows // W_MIX, b * nq + kj[t]))
    return q_spec, k_spec, vt_spec


def _fox_attention(u, vt, qb, kb, batch, seq, tq):
    nq = seq // tq
    qi, kj = _causal_pairs(nq)
    q_spec, k_spec, vt_spec = _attn_specs(nq, tq, U_FOX, W_MIX)
    stat = lambda: pltpu.VMEM((N_HEADS, 1, tq), F32)
    return pl.pallas_call(
        _fox_kernel,
        grid_spec=pltpu.PrefetchScalarGridSpec(
            num_scalar_prefetch=2,
            grid=(batch, qi.shape[0]),
            in_specs=[
                q_spec, k_spec, vt_spec,
                pl.BlockSpec((tq, LANES), lambda b, t, qi, kj: (b * nq + qi[t], 0)),
                pl.BlockSpec((tq, LANES), lambda b, t, qi, kj: (b * nq + kj[t], 0)),
            ],
            out_specs=pl.BlockSpec((tq, W_MIX), lambda b, t, qi, kj: (b * nq + qi[t], 0)),
            scratch_shapes=[pltpu.VMEM((N_HEADS, tq, 2 * LANES), BF16), stat(),
                            pltpu.VMEM((N_HEADS, ACC_ROWS, tq), F32)],
        ),
        out_shape=jax.ShapeDtypeStruct((batch * seq, W_MIX), F32),
        compiler_params=_cparams(("parallel", "arbitrary")),
        name="fox_attention",
    )(qi, kj, u, u, vt, qb, kb)


def _diff_kernel(qi_ref, kj_ref, q_ref, k_ref, vt_ref, cq_ref, sq_ref, ck_ref, sk_ref, rot_ref, lam_ref, ln_ref, o_ref,
                 qm_scr, m_scr, acc_scr, *, lam_init):
    t = pl.program_id(1)
    qi = qi_ref[t]
    kj = kj_ref[t]
    tq = q_ref.shape[0]
    lane = lax.broadcasted_iota(jnp.int32, (tq, LANES), 1)

    def rope(x, cos, sin):
        return x.astype(F32) * cos + jnp.dot(x, rot_ref[...], preferred_element_type=F32) * sin

    @pl.when(kj == 0)
    def _():
        m_scr[...] = jnp.full_like(m_scr, -jnp.inf)
        acc_scr[...] = jnp.zeros_like(acc_scr)
        q = rope(q_ref[...], cq_ref[...], sq_ref[...])
        for h in range(N_HEADS):
            slab, low = _pair_lanes(h)
            base = 0 if low else HEAD_DIM
            for c in range(2):
                lo = base + c * DIFF_DH
                sel = jnp.logical_and(lane >= lo, lane < lo + DIFF_DH)
                qm_scr[2 * h + c] = jnp.where(sel, q[:, slab], 0.0).astype(BF16)

    def step(diagonal):
        k = rope(k_ref[...], ck_ref[...], sk_ref[...]).astype(BF16)
        vts = _values_and_ones(vt_ref[...], tq)
        if diagonal:
            masked = _key_after_query(tq)
        scores = []
        for h in range(N_HEADS):
            slab, _ = _pair_lanes(h)
            for c in range(2):
                i = 2 * h + c
                s = lax.dot_general(k[:, slab], qm_scr[i], (((1,), (1,)), ((), ())), preferred_element_type=F32)
                if diagonal:
                    s = jnp.where(masked, -jnp.inf, s)
                scores.append(s)
        _softmax_updates(scores, [vts[i // 2] for i in range(2 * N_HEADS)], m_scr, acc_scr)

    @pl.when(kj < qi)
    def _():
        step(False)

    @pl.when(kj == qi)
    def _():
        step(True)
        lp = lam_ref[...]
        lam = (jnp.exp(jnp.sum(lp[0:1] * lp[1:2], axis=-1, keepdims=True))
               - jnp.exp(jnp.sum(lp[2:3] * lp[3:4], axis=-1, keepdims=True)) + lam_init)
        head = lambda h: _normalized(acc_scr[2 * h]) - lam * _normalized(acc_scr[2 * h + 1])
        is_lo = lane < HEAD_DIM
        for p in range(N_HEADS // 2):
            o = jnp.concatenate([head(2 * p), head(2 * p + 1)], axis=0).T
            sq = o * o
            ms_lo = jnp.sum(jnp.where(is_lo, sq, 0.0), axis=-1, keepdims=True)
            ms_hi = jnp.sum(jnp.where(is_lo, 0.0, sq), axis=-1, keepdims=True)
            ms = jnp.where(is_lo, ms_lo, ms_hi) * (1.0 / HEAD_DIM)
            o_ref[:, p * LANES:(p + 1) * LANES] = o * lax.rsqrt(ms + DIFF_LN_EPS) * ln_ref[...] * (1.0 - lam_init)


def _diff_attention(u, vt, cos, sin, rot, lam_p, subln, batch, seq, tq, lam_init):
    nq = seq // tq
    qi, kj = _causal_pairs(nq)
    q_spec, k_spec, vt_spec = _attn_specs(nq, tq, U_DIFF, 0)
    tab_q = pl.BlockSpec((tq, W_MIX), lambda b, t, qi, kj: (qi[t], 0))
    tab_k = pl.BlockSpec((tq, W_MIX), lambda b, t, qi, kj: (kj[t], 0))
    const = lambda r, c: pl.BlockSpec((r, c), lambda b, t, qi, kj: (0, 0))
    stat = lambda: pltpu.VMEM((2 * N_HEADS, 1, tq), F32)
    return pl.pallas_call(
        functools.partial(_diff_kernel, lam_init=lam_init),
        grid_spec=pltpu.PrefetchScalarGridSpec(
            num_scalar_prefetch=2,
            grid=(batch, qi.shape[0]),
            in_specs=[q_spec, k_spec, vt_spec, tab_q, tab_q, tab_k, tab_k,
                      const(W_MIX, W_MIX), const(4, DIFF_DH), const(1, LANES)],
            out_specs=pl.BlockSpec((tq, W_MIX), lambda b, t, qi, kj: (b * nq + qi[t], 0)),
            scratch_shapes=[pltpu.VMEM((2 * N_HEADS, tq, LANES), BF16), stat(),
                            pltpu.VMEM((2 * N_HEADS, ACC_ROWS, tq), F32)],
        ),
        out_shape=jax.ShapeDtypeStruct((batch * seq, W_MIX), F32),
        compiler_params=_cparams(("parallel", "arbitrary")),
        name="diff_attention",
    )(qi, kj, u, u, vt, cos, sin, cos, sin, rot, lam_p, subln)


def _merge_kernel(x_ref, gate_ref, oa_ref, post_ref, yb_ref, yc_ref, od_ref, gd_ref, lnw_ref, lnb_ref,
                  wbo_ref, wout_ref, o_ref):
    y_a = (oa_ref[...] * lnw_ref[...] + lnb_ref[...] + post_ref[1]) * post_ref[0]
    y_d = od_ref[...] * _silu(gd_ref[...].astype(F32))
    acc = jnp.zeros(x_ref.shape, F32)
    for b, y in enumerate((y_a, yb_ref[...], yc_ref[...], y_d)):
        gate = _sigmoid(gate_ref[:, b * D_MODEL:(b + 1) * D_MODEL].astype(F32))
        acc = acc + gate * _mm(y, wbo_ref[b])
    o_ref[...] = x_ref[...] + _mm(acc, wout_ref[...])


def _merge(x, u, o_a, post, y_b, y_c, o_d, ln_w, ln_b, w_bo, w_out, tm):
    n = x.shape[0]
    tok = lambda c: pl.BlockSpec((tm, c), lambda i: (i, 0))
    return pl.pallas_call(
        _merge_kernel,
        grid=(n // tm,),
        in_specs=[
            tok(D_MODEL),
            pl.BlockSpec((tm, 4 * D_MODEL), lambda i: (i, U_GATE // (4 * D_MODEL))),
            tok(W_MIX),
            pl.BlockSpec((2, tm, W_MIX), lambda i: (0, i, 0)),
            tok(W_MIX), tok(W_MIX), tok(W_MIX),
            pl.BlockSpec((tm, W_MIX), lambda i: (i, (U_GDN + 3 * W_MIX) // W_MIX)),
            pl.BlockSpec((1, W_MIX), lambda i: (0, 0)),
            pl.BlockSpec((1, W_MIX), lambda i: (0, 0)),
            pl.BlockSpec((4, W_MIX, D_MODEL), lambda i: (0, 0, 0)),
            pl.BlockSpec((D_MODEL, D_MODEL), lambda i: (0, 0)),
        ],
        out_specs=tok(D_MODEL),
        out_shape=jax.ShapeDtypeStruct((n, D_MODEL), F32),
        compiler_params=_cparams(("parallel",)),
        name="merge",
    )(x, u, o_a, post, y_b, y_c, o_d, u, ln_w, ln_b, w_bo, w_out)


def _ffn_kernel(x_ref, g_ref, wg_ref, wu_ref, wd_ref, o_ref, h_scr, acc_scr):
    f = pl.program_id(1)

    @pl.when(f == 0)
    def _():
        h_scr[...] = _rms(x_ref[...], g_ref[...], NORM_EPS).astype(BF16)
        acc_scr[...] = jnp.zeros_like(acc_scr)

    h = h_scr[...]
    act = _silu(jnp.dot(h, wg_ref[...], preferred_element_type=F32)) * jnp.dot(h, wu_ref[...], preferred_element_type=F32)
    acc_scr[...] += _mm(act, wd_ref[...])

    @pl.when(f == pl.num_programs(1) - 1)
    def _():
        o_ref[...] = x_ref[...] + acc_scr[...]


def _ffn(x, g, wg, wu, wd, tm, tf):
    n = x.shape[0]
    ff = wg.shape[1]
    return pl.pallas_call(
        _ffn_kernel,
        grid=(n // tm, ff // tf),
        in_specs=[
            pl.BlockSpec((tm, D_MODEL), lambda i, f: (i, 0)),
            pl.BlockSpec((1, D_MODEL), lambda i, f: (0, 0)),
            pl.BlockSpec((D_MODEL, tf), lambda i, f: (0, f)),
            pl.BlockSpec((D_MODEL, tf), lambda i, f: (0, f)),
            pl.BlockSpec((tf, D_MODEL), lambda i, f: (f, 0)),
        ],
        out_specs=pl.BlockSpec((tm, D_MODEL), lambda i, f: (i, 0)),
        out_shape=jax.ShapeDtypeStruct((n, D_MODEL), F32),
        scratch_shapes=[pltpu.VMEM((tm, D_MODEL), BF16), pltpu.VMEM((tm, D_MODEL), F32)],
        compiler_params=_cparams(("parallel", "arbitrary")),
        name="ffn",
    )(x, g, wg, wu, wd)


def _router_kernel(x_ref, g_ref, router_ref, h_ref, c_ref, rc_ref, rr_ref, cnt_ref):
    t = x_ref.shape[0]
    h = _rms(x_ref[...], g_ref[...], NORM_EPS)
    h_ref[...] = h.astype(BF16)
    logits = _mm_f32(h, router_ref[...])
    lane = lax.broadcasted_iota(jnp.int32, logits.shape, 1).astype(F32)
    lg = jnp.where(lane < N_EXPERTS, logits, -jnp.inf)
    m1 = jnp.max(lg, axis=-1, keepdims=True)
    i1 = jnp.min(jnp.where(lg == m1, lane, float(LANES)), axis=-1, keepdims=True)
    lg2 = jnp.where(lane == i1, -jnp.inf, lg)
    m2 = jnp.max(lg2, axis=-1, keepdims=True)
    i2 = jnp.min(jnp.where(lg2 == m2, lane, float(LANES)), axis=-1, keepdims=True)
    e2 = jnp.exp(m2 - m1)
    c_ref[...] = jnp.where(lane == i1, 1.0 / (1.0 + e2), 0.0) + jnp.where(lane == i2, e2 / (1.0 + e2), 0.0)
    sel = jnp.logical_or(lane == i1, lane == i2)
    sel_f = jnp.where(sel, 1.0, 0.0)
    earlier = _tri_masks(t)[0].astype(BF16)
    rank = jnp.dot(earlier, sel_f.astype(BF16), preferred_element_type=F32)
    rc = jnp.where(sel, rank, -1.0)
    rc_ref[...] = rc
    rr_ref[...] = rc.T[0:BF16_ROWS, :]
    cnt_ref[...] = jnp.broadcast_to(jnp.sum(sel_f, axis=0, keepdims=True), cnt_ref.shape).astype(jnp.int32)


def _router(x, g, router, tm):
    n = x.shape[0]
    return pl.pallas_call(
        _router_kernel,
        grid=(n // tm,),
        in_specs=[
            pl.BlockSpec((tm, D_MODEL), lambda i: (i, 0)),
            pl.BlockSpec((1, D_MODEL), lambda i: (0, 0)),
            pl.BlockSpec((D_MODEL, LANES), lambda i: (0, 0)),
        ],
        out_specs=[
            pl.BlockSpec((tm, D_MODEL), lambda i: (i, 0)),
            pl.BlockSpec((tm, LANES), lambda i: (i, 0)),
            pl.BlockSpec((tm, LANES), lambda i: (i, 0)),
            pl.BlockSpec((BF16_ROWS, tm), lambda i: (0, i)),
            pl.BlockSpec((8, LANES), lambda i: (i, 0)),
        ],
        out_shape=[
            jax.ShapeDtypeStruct((n, D_MODEL), BF16),
            jax.ShapeDtypeStruct((n, LANES), F32),
            jax.ShapeDtypeStruct((n, LANES), F32),
            jax.ShapeDtypeStruct((BF16_ROWS, n), F32),
            jax.ShapeDtypeStruct((n // tm * 8, LANES), jnp.int32),
        ],
        compiler_params=_cparams(("parallel",)),
        name="router",
    )(x, g, router)


def _moe_kernel(cnt_ref, x_ref, h_ref, c_ref, rc_ref, rr_ref, wg_ref, wu_ref, wd_ref, o_ref, xg_scr, yg_scr, *, rows):
    i = pl.program_id(0)
    e = pl.program_id(1)
    f = pl.program_id(2)
    last_f = f == pl.num_programs(2) - 1
    t = h_ref.shape[0]
    nsub = xg_scr.shape[0] // rows
    cnt = cnt_ref[i * N_EXPERTS + e]
    blocks = [(sb, slice(sb * rows, (sb + 1) * rows)) for sb in range(nsub)]

    @pl.when(jnp.logical_and(e == 0, f == 0))
    def _():
        o_ref[...] = x_ref[...]

    @pl.when(f == 0)
    def _():
        rr = rr_ref[pl.ds(e, 1), :]
        for sb, rs in blocks:
            @pl.when(cnt > sb * rows)
            def _():
                slot = (lax.broadcasted_iota(jnp.int32, (rows, t), 0) + sb * rows).astype(F32)
                gather = jnp.where(rr == slot, 1.0, 0.0).astype(BF16)
                xg_scr[rs, :] = jnp.dot(gather, h_ref[...], preferred_element_type=F32).astype(BF16)
                yg_scr[rs, :] = jnp.zeros((rows, D_MODEL), F32)

    for sb, rs in blocks:
        @pl.when(cnt > sb * rows)
        def _():
            xb = xg_scr[rs, :]
            act = (_silu(jnp.dot(xb, wg_ref[0], preferred_element_type=F32))
                   * jnp.dot(xb, wu_ref[0], preferred_element_type=F32))
            yg_scr[rs, :] += _mm(act, wd_ref[0])

    @pl.when(last_f)
    def _():
        lane = lax.broadcasted_iota(jnp.int32, (t, LANES), 1)
        mine = lane == e
        rc = jnp.sum(jnp.where(mine, rc_ref[...], 0.0), axis=-1, keepdims=True)
        ce = jnp.sum(jnp.where(mine, c_ref[...], 0.0), axis=-1, keepdims=True)
        for sb, rs in blocks:
            @pl.when(cnt > sb * rows)
            def _():
                slot = (lax.broadcasted_iota(jnp.int32, (t, rows), 1) + sb * rows).astype(F32)
                scatter = jnp.where(rc == slot, 1.0, 0.0).astype(BF16)
                o_ref[...] += ce * jnp.dot(scatter, yg_scr[rs, :].astype(BF16), preferred_element_type=F32)


def _moe(x, g, router, wg, wu, wd, tm, tf, rows):
    n = x.shape[0]
    ff = wg.shape[2]
    h, c, rc, rr, cnt = _router(x, g, router, tm)
    cnt = cnt.reshape(n // tm, 8, LANES)[:, 0, :N_EXPERTS].reshape(-1)
    nsub = -(-tm // rows)
    tok = lambda cols, **kw: pl.BlockSpec((tm, cols), lambda i, e, f, cnt: (i, 0), **kw)
    once = dict(pipeline_mode=pl.Buffered(1))
    return pl.pallas_call(
        functools.partial(_moe_kernel, rows=rows),
        grid_spec=pltpu.PrefetchScalarGridSpec(
            num_scalar_prefetch=1,
            grid=(n // tm, N_EXPERTS, ff // tf),
            in_specs=[
                tok(D_MODEL, **once), tok(D_MODEL, **once), tok(LANES), tok(LANES),
                pl.BlockSpec((BF16_ROWS, tm), lambda i, e, f, cnt: (0, i)),
                pl.BlockSpec((1, D_MODEL, tf), lambda i, e, f, cnt: (e, 0, f)),
                pl.BlockSpec((1, D_MODEL, tf), lambda i, e, f, cnt: (e, 0, f)),
                pl.BlockSpec((1, tf, D_MODEL), lambda i, e, f, cnt: (e, f, 0)),
            ],
            out_specs=tok(D_MODEL),
            scratch_shapes=[pltpu.VMEM((nsub * rows, D_MODEL), BF16), pltpu.VMEM((nsub * rows, D_MODEL), F32)],
        ),
        out_shape=jax.ShapeDtypeStruct((n, D_MODEL), F32),
        compiler_params=_cparams(("parallel", "arbitrary", "arbitrary"), vmem_mb=56),
        name="moe",
    )(cnt, x, h, c, rc, rr, wg, wu, wd)


def _ple_kernel(x_ref, p_ref, g_ref, wgate_ref, wproj_ref, fin_ref, o_ref, *, final):
    x = x_ref[...]
    h = _rms(x, g_ref[...], NORM_EPS)
    y = x + _sigmoid(_mm(h, wgate_ref[...])) * _mm(p_ref[...], wproj_ref[...])
    o_ref[...] = _rms(y, fin_ref[...], NORM_EPS) if final else y


def _ple(x, p, g, wgate, wproj, fin, tm, final):
    n = x.shape[0]
    return pl.pallas_call(
        functools.partial(_ple_kernel, final=final),
        grid=(n // tm,),
        in_specs=[
            pl.BlockSpec((tm, D_MODEL), lambda i: (i, 0)),
            pl.BlockSpec((tm, P_DIM), lambda i: (i, 0)),
            pl.BlockSpec((1, D_MODEL), lambda i: (0, 0)),
            pl.BlockSpec((D_MODEL, D_MODEL), lambda i: (0, 0)),
            pl.BlockSpec((P_DIM, D_MODEL), lambda i: (0, 0)),
            pl.BlockSpec((1, D_MODEL), lambda i: (0, 0)),
        ],
        out_specs=pl.BlockSpec((tm, D_MODEL), lambda i: (i, 0)),
        out_shape=jax.ShapeDtypeStruct((n, D_MODEL), F32),
        compiler_params=_cparams(("parallel",)),
        name="ple",
    )(x, p, g, wgate, wproj, fin)


def _tiles(n, seq):
    tm = min(512, seq)
    tm_big = 1024 if n % 1024 == 0 else tm
    tm_in = 2048 if n % 2048 == 0 else tm_big
    moe_rows = tm_big // 4 + tm_big // 32
    return dict(tm=tm, tm_big=tm_big, tm_in=tm_in, tn_in=1024, tq=min(512, seq), cblk=min(512, seq), moe_rows=moe_rows)


def _rope_tables(seq):
    half = ROPE_DIMS // 2
    inv = ROPE_THETA ** (-jnp.arange(half, dtype=F32) * 2.0 / ROPE_DIMS)
    ang = jnp.arange(seq, dtype=F32)[:, None] * inv[None, :]
    pad = jnp.zeros((seq, DIFF_DH - ROPE_DIMS), F32)
    cos = jnp.concatenate([jnp.cos(ang), jnp.cos(ang), pad + 1.0], axis=-1)
    sin = jnp.concatenate([jnp.sin(ang), jnp.sin(ang), pad], axis=-1)
    d = jnp.arange(W_MIX)
    dd = d % DIFF_DH
    src = jnp.where(dd < half, d + half, d - half)
    sign = jnp.where(dd < half, -1.0, jnp.where(dd < ROPE_DIMS, 1.0, 0.0))
    rot = jnp.zeros((W_MIX, W_MIX), F32).at[src, d].set(sign)
    reps = W_MIX // DIFF_DH
    return jnp.tile(cos, (1, reps)), jnp.tile(sin, (1, reps)), rot.astype(BF16)


def _split_w_in(w):
    a0 = 0
    b0 = a0 + 4 * W_MIX
    c0 = b0 + 3 * W_MIX
    d0 = c0 + 3 * W_MIX + N_HEADS
    g0 = d0 + 4 * W_MIX + 2 * N_HEADS
    d_small = d0 + 3 * W_MIX
    diff_q = w[:, b0:b0 + W_MIX] * (DIFF_DH ** -0.5 * LOG2E)
    fox_q = w[:, c0:c0 + W_MIX] * (HEAD_DIM ** -0.5 * LOG2E)
    main = jnp.concatenate([
        w[:, g0:], w[:, a0:b0], w[:, d0:d_small], w[:, d_small + 2 * N_HEADS:g0],
        diff_q, w[:, b0 + W_MIX:b0 + 2 * W_MIX], fox_q, w[:, c0 + W_MIX:c0 + 2 * W_MIX],
    ], axis=1).astype(BF16)
    small = jnp.concatenate([
        w[:, c0 + 3 * W_MIX:d0], w[:, d_small:d_small + 2 * N_HEADS],
        jnp.zeros((D_MODEL, LANES - 3 * N_HEADS), w.dtype),
    ], axis=1).astype(BF16)
    v_t = jnp.concatenate([w[:, b0 + 2 * W_MIX:c0], w[:, c0 + 2 * W_MIX:c0 + 3 * W_MIX]], axis=1).T.astype(BF16)
    return main, small, v_t


def _small_params(fbias, a_log, dt_bias):
    zeros = jnp.zeros((N_HEADS,), F32)
    bias = jnp.concatenate([fbias, zeros, dt_bias, jnp.zeros((LANES - 3 * N_HEADS,), F32)])
    neg_a = jnp.concatenate([zeros, zeros, -jnp.exp(a_log), jnp.zeros((LANES - 3 * N_HEADS,), F32)])
    return bias.reshape(1, LANES), neg_a.reshape(1, LANES)


def _pad_rows(w, top, total):
    return jnp.concatenate([jnp.zeros((top, w.shape[1]), w.dtype), w,
                            jnp.zeros((total - top - w.shape[0], w.shape[1]), w.dtype)], axis=0)


def kernel(x, p, norm_mix, norm_ffn, norm_ple, w_in, w_bo, w_out, rwkv_mu, rwkv_w0, rwkv_w2, rwkv_a0, rwkv_a2, rwkv_g2, rwkv_kk, rwkv_ka, rwkv_rk, rwkv_ln_w, rwkv_ln_b, diff_lam, diff_subln, fox_fbias, gdn_conv, gdn_a_log, gdn_dt_bias, gdn_norm, ffn_w_gate, ffn_w_up, ffn_w_down, moe_router, moe_w_gate, moe_w_up, moe_w_down, ple_proj, ple_gate, final_norm):
    batch, seq, _ = x.shape
    depth = w_in.shape[0]
    n = batch * seq
    t = _tiles(n, seq)
    tm, tq, cblk = t["tm"], t["tq"], t["cblk"]
    row = lambda v: v.reshape(1, -1).astype(F32)
    cos, sin, rot = _rope_tables(seq)
    xf = x.reshape(n, D_MODEL)
    pf = p.reshape(depth, n, P_DIM)

    for i in range(depth):
        w_main, w_small, w_vt = _split_w_in(w_in[i])
        u, scol, vt = _inproj(xf, row(norm_mix[i]), w_main, w_small, w_vt, t["tm_in"], t["tn_in"])
        bias, neg_a = _small_params(fox_fbias[i], gdn_a_log[i], gdn_dt_bias[i])
        hcol, fox_kb, fox_qb = _small_prep(scol, bias, neg_a, batch, seq, tm)

        scan_in, pc, post = _rwkv_prep(
            u, row(rwkv_mu[i]), row(rwkv_w0[i]), _pad_rows(rwkv_w2[i], 0, LANES), row(rwkv_a0[i]),
            _pad_rows(rwkv_a2[i], LANES // 2, LANES), rwkv_g2[i], row(rwkv_kk[i]), row(rwkv_ka[i]), row(rwkv_rk[i]),
            seq, tm)
        o_a = _rwkv_chunk(scan_in, pc, batch, seq, cblk)

        o_d = _gdn_chunk(_gdn_prep(u, gdn_conv[i].T, seq, tm), hcol, jnp.tile(row(gdn_norm[i]), (1, N_HEADS)),
                         batch, seq, cblk)

        lam_init = 0.8 - 0.6 * math.exp(-0.3 * i)
        y_b = _diff_attention(u, vt, cos, sin, rot, diff_lam[i].astype(F32), jnp.tile(row(diff_subln[i]), (1, 2)),
                              batch, seq, tq, lam_init)
        y_c = _fox_attention(u, vt, fox_qb, fox_kb, batch, seq, tq)

        xf = _merge(xf, u, o_a, post, y_b, y_c, o_d, row(rwkv_ln_w[i]), row(rwkv_ln_b[i]),
                    w_bo[i].astype(BF16), w_out[i].astype(BF16), tm)

        j = i // 2
        if i % 2 == 0:
            xf = _ffn(xf, row(norm_ffn[i]), ffn_w_gate[j].astype(BF16), ffn_w_up[j].astype(BF16),
                      ffn_w_down[j].astype(BF16), tm, ffn_w_gate.shape[2] // 2)
        else:
            router = jnp.concatenate([moe_router[j], jnp.zeros((D_MODEL, LANES - N_EXPERTS), F32)], axis=1)
            xf = _moe(xf, row(norm_ffn[i]), router, moe_w_gate[j].astype(BF16), moe_w_up[j].astype(BF16),
                      moe_w_down[j].astype(BF16), t["tm_big"], moe_w_gate.shape[3] // 2, t["moe_rows"])
        xf = _ple(xf, pf[i], row(norm_ple[i]), ple_gate[i].astype(BF16), ple_proj[i].astype(BF16),
                  row(final_norm), tm, i == depth - 1)
    return xf.reshape(batch, seq, D_MODEL)
```

```python
import functools
import math

import jax
import jax.numpy as jnp
import numpy as np
from jax import lax
from jax.experimental import pallas as pl
from jax.experimental.pallas import tpu as pltpu

F32 = jnp.float32
BF16 = jnp.bfloat16

D_MODEL = 1024
P_DIM = 256
W_MIX = 256
HEAD_DIM = 64
N_HEADS = 4
DIFF_DH = 32
ROPE_THETA = 500000.0
ROPE_DIMS = 8
RWKV_GN_EPS = 64e-5
DIFF_LN_EPS = 1e-5
GDN_CONV = 4
CHUNK = 64
CHUNK_UNROLL = 4
N_EXPERTS = 8
NORM_EPS = 1e-6
L2_EPS = 1e-6
LOG2E = math.log2(math.e)
LANES = 128
BF16_ROWS = 16
assert CHUNK == HEAD_DIM

U_GATE = 0
U_RWKV = 4096
U_GDN = 5120
U_DIFF = 6144
U_FOX = 6656
U_COLS = 7168
SM_FOX, SM_BETA, SM_DEC = 0, 4, 8
FOX_CK, FOX_CQ = 0, 16


def _cparams(semantics, vmem_mb=48):
    return pltpu.CompilerParams(dimension_semantics=semantics, vmem_limit_bytes=vmem_mb * 1024 * 1024)


def _mm(a, b):
    return jnp.dot(a.astype(BF16), b.astype(BF16), preferred_element_type=F32)


def _mm_nt(a, b):
    return lax.dot_general(a.astype(BF16), b.astype(BF16), (((1,), (1,)), ((), ())), preferred_element_type=F32)


def _mm_tn(a, b):
    return lax.dot_general(a.astype(BF16), b.astype(BF16), (((0,), (0,)), ((), ())), preferred_element_type=F32)


def _mm_split(a, b):
    ah = a.astype(BF16)
    al = (a - ah.astype(F32)).astype(BF16)
    bh = b.astype(BF16)
    bl = (b - bh.astype(F32)).astype(BF16)
    dot = lambda x, y: jnp.dot(x, y, preferred_element_type=F32)
    return dot(ah, bh) + (dot(ah, bl) + dot(al, bh))


def _mm_mask(mask, x):
    hi = x.astype(BF16)
    r1 = x - hi.astype(F32)
    mid = r1.astype(BF16)
    lo = (r1 - mid.astype(F32)).astype(BF16)
    dot = lambda t: jnp.dot(mask, t, preferred_element_type=F32)
    return dot(hi) + dot(mid) + dot(lo)


def _head_sums(x, same):
    hi = x.astype(BF16)
    lo = (x - hi.astype(F32)).astype(BF16)
    return jnp.dot(hi, same, preferred_element_type=F32) + jnp.dot(lo, same, preferred_element_type=F32)


def _rms(x, g, eps):
    return x * lax.rsqrt(jnp.mean(x * x, axis=-1, keepdims=True) + eps) * g


def _sigmoid(x):
    return 0.5 * jnp.tanh(0.5 * x) + 0.5


def _silu(x):
    return x * _sigmoid(x)


def _softplus(x):
    return jnp.maximum(x, 0.0) + jnp.log(1.0 + jnp.exp(-jnp.abs(x)))


def _tri_masks(c):
    ii = lax.broadcasted_iota(jnp.int32, (c, c), 0)
    jj = lax.broadcasted_iota(jnp.int32, (c, c), 1)
    return ii > jj, ii >= jj, ii == jj


def _block_masks():
    ii = lax.broadcasted_iota(jnp.int32, (W_MIX, W_MIX), 0)
    jj = lax.broadcasted_iota(jnp.int32, (W_MIX, W_MIX), 1)
    return (ii // HEAD_DIM) == (jj // HEAD_DIM), ii == jj


def _wide_masks():
    ii = lax.broadcasted_iota(jnp.int32, (CHUNK, W_MIX), 0)
    jj = lax.broadcasted_iota(jnp.int32, (CHUNK, W_MIX), 1) % HEAD_DIM
    return ii > jj, ii >= jj, ii == jj


def _stack_heads(x, same):
    return jnp.where(same, jnp.concatenate([x, x, x, x], axis=0), jnp.zeros((), x.dtype))


def _unit_lower_inverses(ns, eye_w, same):
    rs = [eye_w + n for n in ns]
    ps = list(ns)
    for _ in range(int(math.log2(CHUNK)) - 1):
        ps = [_mm(p, _stack_heads(p.astype(BF16), same)) for p in ps]
        rs = [r + _mm(r, _stack_heads(p.astype(BF16), same)) for r, p in zip(rs, ps)]
    return rs


def _unstack_heads(x):
    return x[0:CHUNK] + x[CHUNK:2 * CHUNK] + x[2 * CHUNK:3 * CHUNK] + x[3 * CHUNK:4 * CHUNK]


def _inproj_kernel(x_ref, g_ref, w_ref, ws_ref, wvt_ref, u_ref, scol_ref, vt_ref, h_scr):
    @pl.when(pl.program_id(1) == 0)
    def _():
        hb = _rms(x_ref[...], g_ref[...], NORM_EPS).astype(BF16)
        h_scr[...] = hb
        scol_ref[...] = jnp.dot(hb, ws_ref[...], preferred_element_type=F32)
        vt_ref[...] = lax.dot_general(wvt_ref[...], hb, (((1,), (1,)), ((), ())),
                                      preferred_element_type=F32).astype(BF16)

    u_ref[...] = jnp.dot(h_scr[...], w_ref[...], preferred_element_type=F32).astype(BF16)


def _inproj(x, g, w, ws, wvt, tm, tn):
    n = x.shape[0]
    return pl.pallas_call(
        _inproj_kernel,
        grid=(n // tm, U_COLS // tn),
        in_specs=[
            pl.BlockSpec((tm, D_MODEL), lambda i, j: (i, 0)),
            pl.BlockSpec((1, D_MODEL), lambda i, j: (0, 0)),
            pl.BlockSpec((D_MODEL, tn), lambda i, j: (0, j)),
            pl.BlockSpec((D_MODEL, LANES), lambda i, j: (0, 0)),
            pl.BlockSpec((2 * W_MIX, D_MODEL), lambda i, j: (0, 0)),
        ],
        out_specs=[
            pl.BlockSpec((tm, tn), lambda i, j: (i, j)),
            pl.BlockSpec((tm, LANES), lambda i, j: (i, 0)),
            pl.BlockSpec((2 * W_MIX, tm), lambda i, j: (0, i)),
        ],
        out_shape=[
            jax.ShapeDtypeStruct((n, U_COLS), BF16),
            jax.ShapeDtypeStruct((n, LANES), F32),
            jax.ShapeDtypeStruct((2 * W_MIX, n), BF16),
        ],
        scratch_shapes=[pltpu.VMEM((tm, D_MODEL), BF16)],
        compiler_params=_cparams(("parallel", "arbitrary"), vmem_mb=56),
        name="inproj",
    )(x, g, w, ws, wvt)


def _lane_placement(base):
    r = lax.broadcasted_iota(jnp.int32, (3 * LANES, LANES), 0)
    m = lax.broadcasted_iota(jnp.int32, (3 * LANES, LANES), 1)
    head, part = r % LANES, r // LANES
    return jnp.logical_and(head < N_HEADS, m == base + 3 * head + part).astype(BF16)


def _split3(x):
    hi = x.astype(BF16)
    r1 = x - hi.astype(F32)
    mid = r1.astype(BF16)
    lo = (r1 - mid.astype(F32)).astype(BF16)
    return jnp.concatenate([hi, mid, lo], axis=1)


def _head_expansion():
    r = lax.broadcasted_iota(jnp.int32, (3 * LANES, 2 * W_MIX), 0) % LANES
    m = lax.broadcasted_iota(jnp.int32, (3 * LANES, 2 * W_MIX), 1)
    src = jnp.where(m < W_MIX, SM_BETA + m // HEAD_DIM, SM_DEC + (m - W_MIX) // HEAD_DIM)
    return (r == src).astype(BF16)


def _small_prep_kernel(scol_ref, bias_ref, nega_ref, ogb_ref, okb_ref, oqb_ref, carry_scr):
    @pl.when(pl.program_id(1) == 0)
    def _():
        carry_scr[...] = jnp.zeros_like(carry_scr)

    tm = scol_ref.shape[0]
    lower = _tri_masks(tm)[1].astype(BF16)
    z = scol_ref[...] + bias_ref[...]
    lane = lax.broadcasted_iota(jnp.int32, z.shape, 1)
    is_f = lane < SM_BETA
    is_b = jnp.logical_and(lane >= SM_BETA, lane < SM_DEC)
    is_d = jnp.logical_and(lane >= SM_DEC, lane < SM_DEC + N_HEADS)
    logf = jnp.where(is_f, jnp.minimum(z, 0.0) - jnp.log(1.0 + jnp.exp(-jnp.abs(z))), 0.0)
    gdn = jnp.where(is_b, _sigmoid(z), jnp.where(is_d, nega_ref[...] * _softplus(z), 0.0))
    ogb_ref[...] = jnp.dot(_split3(gdn), _head_expansion(), preferred_element_type=F32)
    cum = _mm_mask(lower, logf) + carry_scr[...]
    carry_scr[...] = cum[tm - 1:tm, :]
    parts = _split3(cum * LOG2E)
    ones_lanes = jnp.logical_and(lane >= FOX_CQ, lane < FOX_CQ + 3 * N_HEADS)
    okb_ref[...] = (jnp.dot(parts, _lane_placement(FOX_CK), preferred_element_type=F32)
                    + jnp.where(ones_lanes, 1.0, 0.0)).astype(BF16)
    oqb_ref[...] = jnp.dot(parts, _lane_placement(FOX_CQ), preferred_element_type=F32).astype(BF16)


def _small_prep(scol, bias, neg_a, batch, seq, tm):
    n = batch * seq
    nt = seq // tm
    tok = lambda: pl.BlockSpec((tm, LANES), lambda b, j: (b * nt + j, 0))
    const = lambda: pl.BlockSpec((1, LANES), lambda b, j: (0, 0))
    return pl.pallas_call(
        _small_prep_kernel,
        grid=(batch, nt),
        in_specs=[tok(), const(), const()],
        out_specs=[pl.BlockSpec((tm, 2 * W_MIX), lambda b, j: (b * nt + j, 0)), tok(), tok()],
        out_shape=[jax.ShapeDtypeStruct((n, 2 * W_MIX), F32), jax.ShapeDtypeStruct((n, LANES), BF16),
                   jax.ShapeDtypeStruct((n, LANES), BF16)],
        scratch_shapes=[pltpu.VMEM((1, LANES), F32)],
        compiler_params=_cparams(("parallel", "arbitrary")),
        name="small_prep",
    )(scol, bias, neg_a)


def _rwkv_prep_kernel(u_ref, up_ref, mu_ref, w0_ref, w2_ref, a0_ref, a2_ref, g2_ref, kk_ref, ka_ref, rk_ref,
                      scan_ref, pc_ref, post_ref, *, tiles_per_seq):
    tm = u_ref.shape[0]
    u = u_ref[...].astype(F32)
    prev = up_ref[...].astype(F32)[BF16_ROWS - 1:BF16_ROWS, :]
    prev = jnp.where(pl.program_id(0) % tiles_per_seq == 0, 0.0, prev)
    rows = lax.broadcasted_iota(jnp.int32, (tm, 1), 0)
    u_prev = jnp.where(rows == 0, prev, pltpu.roll(u, 1, 0))
    xm = u + (u_prev - u) * mu_ref[...]
    r = xm[:, 0:W_MIX]
    k = xm[:, W_MIX:2 * W_MIX]
    v = xm[:, 2 * W_MIX:3 * W_MIX]
    x_lora = xm[:, 3 * W_MIX:3 * W_MIX + LANES]
    xg = xm[:, 3 * W_MIX + LANES:]
    logw = -_softplus(-(w0_ref[...] + _mm(jnp.tanh(x_lora), w2_ref[...]))) - 0.5
    log_decay = -jnp.exp(logw)
    a = _sigmoid(a0_ref[...] + _mm(x_lora, a2_ref[...]))
    g = _mm(_sigmoid(xg), g2_ref[...])
    same = _block_masks()[0].astype(BF16)
    kk_raw = k * kk_ref[...]
    kk = kk_raw * lax.rsqrt(_head_sums(kk_raw * kk_raw, same) + L2_EPS)
    k2 = k * (1.0 + (a - 1.0) * ka_ref[...])
    bonus = _head_sums(r * k2 * rk_ref[...], same) * v
    ti = lax.broadcasted_iota(jnp.int32, (tm, tm), 0)
    tj = lax.broadcasted_iota(jnp.int32, (tm, tm), 1)
    in_chunk = jnp.logical_and(ti // CHUNK == tj // CHUNK, ti >= tj).astype(BF16)
    ci = lax.broadcasted_iota(jnp.int32, (tm // CHUNK, tm), 0)
    cj = lax.broadcasted_iota(jnp.int32, (tm // CHUNK, tm), 1)
    cum = _mm_mask(in_chunk, log_decay)
    cum_end = _mm_mask((ci == cj // CHUNK).astype(BF16), log_decay)
    inv = jnp.exp(-cum)
    scan_ref[0] = (-kk * jnp.exp(cum - log_decay)).astype(BF16)
    scan_ref[1] = (kk * a * inv).astype(BF16)
    scan_ref[2] = (k2 * inv).astype(BF16)
    scan_ref[3] = (r * jnp.exp(cum)).astype(BF16)
    scan_ref[4] = v.astype(BF16)
    pc_ref[...] = jnp.exp(cum_end)
    post_ref[0] = g
    post_ref[1] = bonus


def _rwkv_prep(u, mu, w0, w2p, a0, a2p, g2, k_k, k_a, r_k, seq, tm):
    n = u.shape[0]
    ublk = U_RWKV // D_MODEL
    row = lambda c: pl.BlockSpec((1, c), lambda i: (0, 0))
    mat = lambda r: pl.BlockSpec((r, W_MIX), lambda i: (0, 0))
    return pl.pallas_call(
        functools.partial(_rwkv_prep_kernel, tiles_per_seq=seq // tm),
        grid=(n // tm,),
        in_specs=[
            pl.BlockSpec((tm, D_MODEL), lambda i: (i, ublk)),
            pl.BlockSpec((BF16_ROWS, D_MODEL), lambda i: (jnp.maximum(i * (tm // BF16_ROWS) - 1, 0), ublk)),
            row(D_MODEL), row(W_MIX), mat(LANES), row(W_MIX), mat(LANES), mat(LANES), row(W_MIX), row(W_MIX), row(W_MIX),
        ],
        out_specs=[
            pl.BlockSpec((5, tm, W_MIX), lambda i: (0, i, 0)),
            pl.BlockSpec((tm // CHUNK, W_MIX), lambda i: (i, 0)),
            pl.BlockSpec((2, tm, W_MIX), lambda i: (0, i, 0)),
        ],
        out_shape=[jax.ShapeDtypeStruct((5, n, W_MIX), BF16), jax.ShapeDtypeStruct((n // CHUNK, W_MIX), F32),
                   jax.ShapeDtypeStruct((2, n, W_MIX), F32)],
        compiler_params=_cparams(("parallel",)),
        name="rwkv_prep",
    )(u, u, mu, w0, w2p, a0, a2p, g2, k_k, k_a, r_k)


def _rwkv_chunk_kernel(x_ref, pc_ref, o_ref, s_scr, *, nchunk, nbatch):
    @pl.when(pl.program_id(0) == 0)
    def _():
        s_scr[...] = jnp.zeros_like(s_scr)

    same, diag = _block_masks()
    eye = diag.astype(F32)
    strict_w, incl_w, diag_w = _wide_masks()
    eye_w = diag_w.astype(F32)
    stack = lambda x: _stack_heads(x.astype(BF16), same)

    def chunks(i, carry):
        items = [(b, i * CHUNK_UNROLL + j) for j in range(CHUNK_UNROLL) for b in range(nbatch)]
        each = lambda f, *lists: [f(*args) for args in zip(*lists)]
        sls = [pl.ds(pl.multiple_of(ci * CHUNK, CHUNK), CHUNK) for _, ci in items]
        a, bb, k, r, v = ([x_ref[i, b, sl, :] for (b, _), sl in zip(items, sls)] for i in range(5))
        pc = [pc_ref[b, pl.ds(ci, 1), :] for b, ci in items]
        a_s, b_s, k_s, v_s = (each(stack, x) for x in (a, bb, k, v))
        m_ab = each(lambda x, y: jnp.where(strict_w, _mm_nt(x, y), 0.0), a, b_s)
        m_ak = each(lambda x, y: jnp.where(strict_w, _mm_nt(x, y), 0.0), a, k_s)
        n_rb = each(lambda x, y: jnp.where(incl_w, _mm_nt(x, y), 0.0), r, b_s)
        n_rk = each(lambda x, y: jnp.where(incl_w, _mm_nt(x, y), 0.0), r, k_s)
        t_inv = _unit_lower_inverses(m_ab, eye_w, same)
        a2 = each(_mm, t_inv, a_s)
        u0 = each(lambda t, m, x: _mm(t, stack(_mm(m, x))), t_inv, m_ak, v_s)
        r2 = each(lambda x, n, y: x.astype(F32) + _mm(n, stack(y)), r, n_rb, a2)
        o0 = each(lambda n1, u, n2, x: _mm(n1, stack(u)) + _mm(n2, x), n_rb, u0, n_rk, v_s)
        b_end = each(lambda x, p: x.astype(F32) * p, bb, pc)
        k_end = each(lambda x, p: x.astype(F32) * p, k, pc)
        g_mat = each(lambda p, x, y: eye * p + jnp.where(same, _mm_tn(x, y), 0.0), pc, a2, b_end)
        s0 = each(lambda u, x, y, z: jnp.where(same, _mm_tn(u, x) + _mm_tn(y, z), 0.0), u0, b_end, v, k_end)
        for n, ((b, _), sl) in enumerate(zip(items, sls)):
            s = s_scr[b]
            o = _stack_heads(_mm_nt(r2[n], s) + o0[n], same)
            s_scr[b] = _mm(s, g_mat[n]) + s0[n]
            mean = jnp.sum(o, axis=-1, keepdims=True) * (1.0 / HEAD_DIM)
            cen = jnp.where(same, o - mean, 0.0)
            var = jnp.sum(cen * cen, axis=-1, keepdims=True) * (1.0 / HEAD_DIM)
            o_ref[b, sl, :] = _unstack_heads(cen * lax.rsqrt(var + RWKV_GN_EPS))
        return carry

    lax.fori_loop(0, nchunk // CHUNK_UNROLL, chunks, 0)


def _rwkv_chunk(xs, pc, batch, seq, cblk):
    xs = xs.reshape(5, batch, seq, W_MIX)
    pc = pc.reshape(batch, seq // CHUNK, W_MIX)
    out = pl.pallas_call(
        functools.partial(_rwkv_chunk_kernel, nchunk=cblk // CHUNK, nbatch=batch),
        grid=(seq // cblk,),
        in_specs=[
            pl.BlockSpec((5, batch, cblk, W_MIX), lambda j: (0, 0, j, 0)),
            pl.BlockSpec((batch, cblk // CHUNK, W_MIX), lambda j: (0, j, 0)),
        ],
        out_specs=pl.BlockSpec((batch, cblk, W_MIX), lambda j: (0, j, 0)),
        out_shape=jax.ShapeDtypeStruct((batch, seq, W_MIX), F32),
        scratch_shapes=[pltpu.VMEM((batch, W_MIX, W_MIX), F32)],
        compiler_params=_cparams(("arbitrary",)),
        name="rwkv_chunk",
    )(xs, pc)
    return out.reshape(batch * seq, W_MIX)


def _gdn_prep_kernel(u_ref, up_ref, cw_ref, o_ref, ext_scr, *, tiles_per_seq):
    tm = u_ref.shape[0]
    c3 = 3 * W_MIX
    prev = up_ref[...].astype(F32)[:, :c3]
    ext_scr[0:BF16_ROWS, :] = jnp.where(pl.program_id(0) % tiles_per_seq == 0, 0.0, prev)
    ext_scr[BF16_ROWS:, :] = u_ref[...].astype(F32)[:, :c3]
    y = jnp.zeros((tm, c3), F32)
    for j in range(GDN_CONV):
        y = y + ext_scr[pl.ds(BF16_ROWS - (GDN_CONV - 1) + j, tm), :] * cw_ref[j:j + 1, :]
    y = _silu(y)
    same = _block_masks()[0].astype(BF16)
    q = y[:, 0:W_MIX]
    k = y[:, W_MIX:2 * W_MIX]
    o_ref[0] = (q * lax.rsqrt(_head_sums(q * q, same) + L2_EPS) * (HEAD_DIM ** -0.5)).astype(BF16)
    o_ref[1] = (k * lax.rsqrt(_head_sums(k * k, same) + L2_EPS)).astype(BF16)
    o_ref[2] = y[:, 2 * W_MIX:].astype(BF16)


def _gdn_prep(u, conv_w, seq, tm):
    n = u.shape[0]
    ublk = U_GDN // D_MODEL
    return pl.pallas_call(
        functools.partial(_gdn_prep_kernel, tiles_per_seq=seq // tm),
        grid=(n // tm,),
        in_specs=[
            pl.BlockSpec((tm, D_MODEL), lambda i: (i, ublk)),
            pl.BlockSpec((BF16_ROWS, D_MODEL), lambda i: (jnp.maximum(i * (tm // BF16_ROWS) - 1, 0), ublk)),
            pl.BlockSpec((GDN_CONV, 3 * W_MIX), lambda i: (0, 0)),
        ],
        out_specs=pl.BlockSpec((3, tm, W_MIX), lambda i: (0, i, 0)),
        out_shape=jax.ShapeDtypeStruct((3, n, W_MIX), BF16),
        scratch_shapes=[pltpu.VMEM((tm + BF16_ROWS, 3 * W_MIX), F32)],
        compiler_params=_cparams(("parallel",)),
        name="gdn_prep",
    )(u, u, conv_w)


def _gdn_chunk_kernel(x_ref, gb_ref, nw_ref, o_ref, s_scr, *, nchunk, nbatch):
    @pl.when(pl.program_id(0) == 0)
    def _():
        s_scr[...] = jnp.zeros_like(s_scr)

    same, diag = _block_masks()
    eye = diag.astype(F32)
    strict_w, incl_w, diag_w = _wide_masks()
    eye_w = diag_w.astype(F32)
    lower = _tri_masks(CHUNK)[1].astype(BF16)
    stack = lambda x: _stack_heads(x.astype(BF16), same)

    def chunks(i, carry):
        items = [(b, i * CHUNK_UNROLL + j) for j in range(CHUNK_UNROLL) for b in range(nbatch)]
        each = lambda f, *lists: [f(*args) for args in zip(*lists)]
        sls = [pl.ds(pl.multiple_of(ci * CHUNK, CHUNK), CHUNK) for _, ci in items]
        q, k, v = ([x_ref[i, b, sl, :] for (b, _), sl in zip(items, sls)] for i in range(3))
        beta = [gb_ref[b, sl, 0:W_MIX] for (b, _), sl in zip(items, sls)]
        g = [gb_ref[b, sl, W_MIX:2 * W_MIX] for (b, _), sl in zip(items, sls)]
        k_s = each(stack, k)
        gam = [_mm_mask(lower, x) for x in g]
        gam_end = [x[CHUNK - 1:CHUNK, :] for x in gam]
        gdiff = [_mm_mask(lower, jnp.where(strict_w, x, 0.0)) for x in g]
        decay = [jnp.exp(jnp.where(incl_w, x, -jnp.inf)) for x in gdiff]
        a_mat = each(lambda bt, d, x, y: jnp.where(strict_w, bt * d * _mm_nt(x, y), 0.0), beta, decay, k, k_s)
        t_inv = _unit_lower_inverses([-a for a in a_mat], eye_w, same)
        e_gam = [jnp.exp(x) for x in gam]
        u0 = each(lambda t, bt, x: _mm(t, stack(bt * x.astype(F32))), t_inv, beta, v)
        wm = each(lambda t, bt, e, x: _mm(t, stack((bt * e) * x.astype(F32))), t_inv, beta, e_gam, k)
        qk = each(lambda x, y, d: _mm_nt(x, y) * d, q, k_s, decay)
        q2 = each(lambda e, x, a, w: e * x.astype(F32) - _mm(a, stack(w)), e_gam, q, qk, wm)
        o0 = each(lambda a, u: _mm(a, stack(u)), qk, u0)
        k_end = each(lambda x, ge, ga: x.astype(F32) * jnp.exp(ge - ga), k, gam_end, gam)
        g_mat = each(lambda ge, x, w: eye * jnp.exp(ge) - jnp.where(same, _mm_tn(x, w), 0.0), gam_end, k_end, wm)
        s0 = each(lambda x, u: jnp.where(same, _mm_tn(x, u), 0.0), k_end, u0)
        for n, ((b, _), sl) in enumerate(zip(items, sls)):
            s = s_scr[b]
            o = _stack_heads(_mm(q2[n], s) + o0[n], same)
            s_scr[b] = _mm(g_mat[n], s) + s0[n]
            ms = jnp.sum(o * o, axis=-1, keepdims=True) * (1.0 / HEAD_DIM)
            o_ref[b, sl, :] = _unstack_heads(o * lax.rsqrt(ms + NORM_EPS)) * nw_ref[...]
        return carry

    lax.fori_loop(0, nchunk // CHUNK_UNROLL, chunks, 0)


def _gdn_chunk(xs, gb, norm_w, batch, seq, cblk):
    xs = xs.reshape(3, batch, seq, W_MIX)
    gb = gb.reshape(batch, seq, 2 * W_MIX)
    out = pl.pallas_call(
        functools.partial(_gdn_chunk_kernel, nchunk=cblk // CHUNK, nbatch=batch),
        grid=(seq // cblk,),
        in_specs=[
            pl.BlockSpec((3, batch, cblk, W_MIX), lambda j: (0, 0, j, 0)),
            pl.BlockSpec((batch, cblk, 2 * W_MIX), lambda j: (0, j, 0)),
            pl.BlockSpec((1, W_MIX), lambda j: (0, 0)),
        ],
        out_specs=pl.BlockSpec((batch, cblk, W_MIX), lambda j: (0, j, 0)),
        out_shape=jax.ShapeDtypeStruct((batch, seq, W_MIX), F32),
        scratch_shapes=[pltpu.VMEM((batch, W_MIX, W_MIX), F32)],
        compiler_params=_cparams(("arbitrary",)),
        name="gdn_chunk",
    )(xs, gb, norm_w)
    return out.reshape(batch * seq, W_MIX)


def _causal_pairs(nq):
    pairs = [(i, j) for i in range(nq) for j in range(i + 1)]
    return jnp.asarray(np.array([p[0] for p in pairs], np.int32)), jnp.asarray(np.array([p[1] for p in pairs], np.int32))


def _softmax_updates(scores, vt_ones, m_scr, acc_scr):
    probs = []
    for i, s in enumerate(scores):
        m_old = m_scr[i]
        m_new = jnp.maximum(m_old, jnp.max(s, axis=0, keepdims=True))
        m_scr[i] = m_new
        probs.append((jnp.exp2(m_old - m_new), jnp.exp2((s - m_new).astype(BF16))))
    for i, (alpha, p) in enumerate(probs):
        acc_scr[i] = alpha * acc_scr[i] + jnp.dot(vt_ones[i], p, preferred_element_type=F32)


def _pair_lanes(h):
    p = h // 2
    return slice(p * LANES, (p + 1) * LANES), h % 2 == 0


ACC_ROWS = HEAD_DIM + BF16_ROWS


def _values_and_ones(vt, tk):
    ones = jnp.ones((BF16_ROWS, tk), vt.dtype)
    return [jnp.concatenate([vt[h * HEAD_DIM:(h + 1) * HEAD_DIM, :], ones], axis=0) for h in range(N_HEADS)]


def _normalized(acc):
    return acc[0:HEAD_DIM] / acc[HEAD_DIM:HEAD_DIM + 1]


def _key_after_query(tq):
    return lax.broadcasted_iota(jnp.int32, (tq, tq), 0) > lax.broadcasted_iota(jnp.int32, (tq, tq), 1)


def _fox_kernel(qi_ref, kj_ref, q_ref, k_ref, vt_ref, qb_ref, kb_ref, o_ref, qm_scr, m_scr, acc_scr):
    t = pl.program_id(1)
    qi = qi_ref[t]
    kj = kj_ref[t]
    tq = q_ref.shape[0]

    @pl.when(kj == 0)
    def _():
        m_scr[...] = jnp.full_like(m_scr, -jnp.inf)
        acc_scr[...] = jnp.zeros_like(acc_scr)
        q = q_ref[...]
        qb = qb_ref[...]
        lane = lax.broadcasted_iota(jnp.int32, (tq, LANES), 1)
        zero = jnp.zeros((), BF16)
        for h in range(N_HEADS):
            slab, low = _pair_lanes(h)
            mine = lane < HEAD_DIM if low else lane >= HEAD_DIM
            ck_lanes = jnp.logical_and(lane >= FOX_CK + 3 * h, lane < FOX_CK + 3 * h + 3)
            cq_lanes = jnp.logical_and(lane >= FOX_CQ + 3 * h, lane < FOX_CQ + 3 * h + 3)
            bias = jnp.where(ck_lanes, -jnp.ones((), BF16), jnp.where(cq_lanes, qb, zero))
            qm_scr[h] = jnp.concatenate([jnp.where(mine, q[:, slab], zero), bias], axis=1)

    def step(diagonal):
        k = k_ref[...]
        kb = kb_ref[...]
        vts = _values_and_ones(vt_ref[...], tq)
        if diagonal:
            masked = _key_after_query(tq)
        keys = [jnp.concatenate([k[:, p * LANES:(p + 1) * LANES], kb], axis=1) for p in range(N_HEADS // 2)]
        scores = []
        for h in range(N_HEADS):
            s = lax.dot_general(keys[h // 2], qm_scr[h], (((1,), (1,)), ((), ())), preferred_element_type=F32)
            if diagonal:
                s = jnp.where(masked, -jnp.inf, s)
            scores.append(s)
        _softmax_updates(scores, vts, m_scr, acc_scr)

    @pl.when(kj < qi)
    def _():
        step(False)

    @pl.when(kj == qi)
    def _():
        step(True)
        for p in range(N_HEADS // 2):
            pair = jnp.concatenate([_normalized(acc_scr[2 * p]), _normalized(acc_scr[2 * p + 1])], axis=0)
            o_ref[:, p * LANES:(p + 1) * LANES] = pair.T


def _attn_specs(nq, tq, ucol, vt_rows):
    cb = ucol // W_MIX
    q_spec = pl.BlockSpec((tq, W_MIX), lambda b, t, qi, kj: (b * nq + qi[t], cb))
    k_spec = pl.BlockSpec((tq, W_MIX), lambda b, t, qi, kj: (b * nq + kj[t], cb + 1))
    vt_spec = pl.BlockSpec((W_MIX, tq), lambda b, t, qi, kj: (vt_rows // W_MIX, b * nq + kj[t]))
    return q_spec, k_spec, vt_spec


def _fox_attention(u, vt, qb, kb, batch, seq, tq):
    nq = seq // tq
    qi, kj = _causal_pairs(nq)
    q_spec, k_spec, vt_spec = _attn_specs(nq, tq, U_FOX, W_MIX)
    stat = lambda: pltpu.VMEM((N_HEADS, 1, tq), F32)
    return pl.pallas_call(
        _fox_kernel,
        grid_spec=pltpu.PrefetchScalarGridSpec(
            num_scalar_prefetch=2,
            grid=(batch, qi.shape[0]),
            in_specs=[
                q_spec, k_spec, vt_spec,
                pl.BlockSpec((tq, LANES), lambda b, t, qi, kj: (b * nq + qi[t], 0)),
                pl.BlockSpec((tq, LANES), lambda b, t, qi, kj: (b * nq + kj[t], 0)),
            ],
            out_specs=pl.BlockSpec((tq, W_MIX), lambda b, t, qi, kj: (b * nq + qi[t], 0)),
            scratch_shapes=[pltpu.VMEM((N_HEADS, tq, 2 * LANES), BF16), stat(),
                            pltpu.VMEM((N_HEADS, ACC_ROWS, tq), F32)],
        ),
        out_shape=jax.ShapeDtypeStruct((batch * seq, W_MIX), F32),
        compiler_params=_cparams(("parallel", "arbitrary")),
        name="fox_attention",
    )(qi, kj, u, u, vt, qb, kb)


def _diff_kernel(qi_ref, kj_ref, q_ref, k_ref, vt_ref, cq_ref, sq_ref, ck_ref, sk_ref, rot_ref, lam_ref, ln_ref, o_ref,
                 qm_scr, m_scr, acc_scr, *, lam_init):
    t = pl.program_id(1)
    qi = qi_ref[t]
    kj = kj_ref[t]
    tq = q_ref.shape[0]
    lane = lax.broadcasted_iota(jnp.int32, (tq, LANES), 1)

    def rope(x, cos, sin):
        return x.astype(F32) * cos + jnp.dot(x, rot_ref[...], preferred_element_type=F32) * sin

    @pl.when(kj == 0)
    def _():
        m_scr[...] = jnp.full_like(m_scr, -jnp.inf)
        acc_scr[...] = jnp.zeros_like(acc_scr)
        q = rope(q_ref[...], cq_ref[...], sq_ref[...])
        for h in range(N_HEADS):
            slab, low = _pair_lanes(h)
            base = 0 if low else HEAD_DIM
            for c in range(2):
                lo = base + c * DIFF_DH
                sel = jnp.logical_and(lane >= lo, lane < lo + DIFF_DH)
                qm_scr[2 * h + c] = jnp.where(sel, q[:, slab], 0.0).astype(BF16)

    def step(diagonal):
        k = rope(k_ref[...], ck_ref[...], sk_ref[...]).astype(BF16)
        vts = _values_and_ones(vt_ref[...], tq)
        if diagonal:
            masked = _key_after_query(tq)
        scores = []
        for h in range(N_HEADS):
            slab, _ = _pair_lanes(h)
            for c in range(2):
                i = 2 * h + c
                s = lax.dot_general(k[:, slab], qm_scr[i], (((1,), (1,)), ((), ())), preferred_element_type=F32)
                if diagonal:
                    s = jnp.where(masked, -jnp.inf, s)
                scores.append(s)
        _softmax_updates(scores, [vts[i // 2] for i in range(2 * N_HEADS)], m_scr, acc_scr)

    @pl.when(kj < qi)
    def _():
        step(False)

    @pl.when(kj == qi)
    def _():
        step(True)
        lp = lam_ref[...]
        lam = (jnp.exp(jnp.sum(lp[0:1] * lp[1:2], axis=-1, keepdims=True))
               - jnp.exp(jnp.sum(lp[2:3] * lp[3:4], axis=-1, keepdims=True)) + lam_init)
        head = lambda h: _normalized(acc_scr[2 * h]) - lam * _normalized(acc_scr[2 * h + 1])
        is_lo = lane < HEAD_DIM
        for p in range(N_HEADS // 2):
            o = jnp.concatenate([head(2 * p), head(2 * p + 1)], axis=0).T
            sq = o * o
            ms_lo = jnp.sum(jnp.where(is_lo, sq, 0.0), axis=-1, keepdims=True)
            ms_hi = jnp.sum(jnp.where(is_lo, 0.0, sq), axis=-1, keepdims=True)
            ms = jnp.where(is_lo, ms_lo, ms_hi) * (1.0 / HEAD_DIM)
            o_ref[:, p * LANES:(p + 1) * LANES] = o * lax.rsqrt(ms + DIFF_LN_EPS) * ln_ref[...] * (1.0 - lam_init)


def _diff_attention(u, vt, cos, sin, rot, lam_p, subln, batch, seq, tq, lam_init):
    nq = seq // tq
    qi, kj = _causal_pairs(nq)
    q_spec, k_spec, vt_spec = _attn_specs(nq, tq, U_DIFF, 0)
    tab_q = pl.BlockSpec((tq, W_MIX), lambda b, t, qi, kj: (qi[t], 0))
    tab_k = pl.BlockSpec((tq, W_MIX), lambda b, t, qi, kj: (kj[t], 0))
    const = lambda r, c: pl.BlockSpec((r, c), lambda b, t, qi, kj: (0, 0))
    stat = lambda: pltpu.VMEM((2 * N_HEADS, 1, tq), F32)
    return pl.pallas_call(
        functools.partial(_diff_kernel, lam_init=lam_init),
        grid_spec=pltpu.PrefetchScalarGridSpec(
            num_scalar_prefetch=2,
            grid=(batch, qi.shape[0]),
            in_specs=[q_spec, k_spec, vt_spec, tab_q, tab_q, tab_k, tab_k,
                      const(W_MIX, W_MIX), const(4, DIFF_DH), const(1, LANES)],
            out_specs=pl.BlockSpec((tq, W_MIX), lambda b, t, qi, kj: (b * nq + qi[t], 0)),
            scratch_shapes=[pltpu.VMEM((2 * N_HEADS, tq, LANES), BF16), stat(),
                            pltpu.VMEM((2 * N_HEADS, ACC_ROWS, tq), F32)],
        ),
        out_shape=jax.ShapeDtypeStruct((batch * seq, W_MIX), F32),
        compiler_params=_cparams(("parallel", "arbitrary")),
        name="diff_attention",
    )(qi, kj, u, u, vt, cos, sin, cos, sin, rot, lam_p, subln)


def _merge_kernel(x_ref, gate_ref, oa_ref, post_ref, yb_ref, yc_ref, od_ref, gd_ref, lnw_ref, lnb_ref,
                  wbo_ref, wout_ref, o_ref):
    y_a = (oa_ref[...] * lnw_ref[...] + lnb_ref[...] + post_ref[1]) * post_ref[0]
    y_d = od_ref[...] * _silu(gd_ref[...].astype(F32))
    acc = jnp.zeros(x_ref.shape, F32)
    for b, y in enumerate((y_a, yb_ref[...], yc_ref[...], y_d)):
        gate = _sigmoid(gate_ref[:, b * D_MODEL:(b + 1) * D_MODEL].astype(F32))
        acc = acc + gate * _mm(y, wbo_ref[b])
    o_ref[...] = x_ref[...] + _mm(acc, wout_ref[...])


def _merge(x, u, o_a, post, y_b, y_c, o_d, ln_w, ln_b, w_bo, w_out, tm):
    n = x.shape[0]
    tok = lambda c: pl.BlockSpec((tm, c), lambda i: (i, 0))
    return pl.pallas_call(
        _merge_kernel,
        grid=(n // tm,),
        in_specs=[
            tok(D_MODEL),
            pl.BlockSpec((tm, 4 * D_MODEL), lambda i: (i, U_GATE // (4 * D_MODEL))),
            tok(W_MIX),
            pl.BlockSpec((2, tm, W_MIX), lambda i: (0, i, 0)),
            tok(W_MIX), tok(W_MIX), tok(W_MIX),
            pl.BlockSpec((tm, W_MIX), lambda i: (i, (U_GDN + 3 * W_MIX) // W_MIX)),
            pl.BlockSpec((1, W_MIX), lambda i: (0, 0)),
            pl.BlockSpec((1, W_MIX), lambda i: (0, 0)),
            pl.BlockSpec((4, W_MIX, D_MODEL), lambda i: (0, 0, 0)),
            pl.BlockSpec((D_MODEL, D_MODEL), lambda i: (0, 0)),
        ],
        out_specs=tok(D_MODEL),
        out_shape=jax.ShapeDtypeStruct((n, D_MODEL), F32),
        compiler_params=_cparams(("parallel",)),
        name="merge",
    )(x, u, o_a, post, y_b, y_c, o_d, u, ln_w, ln_b, w_bo, w_out)


def _ffn_kernel(x_ref, g_ref, wg_ref, wu_ref, wd_ref, o_ref, h_scr, acc_scr):
    f = pl.program_id(1)

    @pl.when(f == 0)
    def _():
        h_scr[...] = _rms(x_ref[...], g_ref[...], NORM_EPS).astype(BF16)
        acc_scr[...] = jnp.zeros_like(acc_scr)

    h = h_scr[...]
    act = _silu(jnp.dot(h, wg_ref[...], preferred_element_type=F32)) * jnp.dot(h, wu_ref[...], preferred_element_type=F32)
    acc_scr[...] += _mm(act, wd_ref[...])

    @pl.when(f == pl.num_programs(1) - 1)
    def _():
        o_ref[...] = x_ref[...] + acc_scr[...]


def _ffn(x, g, wg, wu, wd, tm, tf):
    n = x.shape[0]
    ff = wg.shape[1]
    return pl.pallas_call(
        _ffn_kernel,
        grid=(n // tm, ff // tf),
        in_specs=[
            pl.BlockSpec((tm, D_MODEL), lambda i, f: (i, 0)),
            pl.BlockSpec((1, D_MODEL), lambda i, f: (0, 0)),
            pl.BlockSpec((D_MODEL, tf), lambda i, f: (0, f)),
            pl.BlockSpec((D_MODEL, tf), lambda i, f: (0, f)),
            pl.BlockSpec((tf, D_MODEL), lambda i, f: (f, 0)),
        ],
        out_specs=pl.BlockSpec((tm, D_MODEL), lambda i, f: (i, 0)),
        out_shape=jax.ShapeDtypeStruct((n, D_MODEL), F32),
        scratch_shapes=[pltpu.VMEM((tm, D_MODEL), BF16), pltpu.VMEM((tm, D_MODEL), F32)],
        compiler_params=_cparams(("parallel", "arbitrary")),
        name="ffn",
    )(x, g, wg, wu, wd)


def _router_kernel(x_ref, g_ref, router_ref, h_ref, c_ref, rc_ref, rr_ref, cnt_ref):
    t = x_ref.shape[0]
    h = _rms(x_ref[...], g_ref[...], NORM_EPS)
    h_ref[...] = h.astype(BF16)
    logits = _mm_split(h, router_ref[...])
    lane = lax.broadcasted_iota(jnp.int32, logits.shape, 1).astype(F32)
    lg = jnp.where(lane < N_EXPERTS, logits, -jnp.inf)
    m1 = jnp.max(lg, axis=-1, keepdims=True)
    i1 = jnp.min(jnp.where(lg == m1, lane, float(LANES)), axis=-1, keepdims=True)
    lg2 = jnp.where(lane == i1, -jnp.inf, lg)
    m2 = jnp.max(lg2, axis=-1, keepdims=True)
    i2 = jnp.min(jnp.where(lg2 == m2, lane, float(LANES)), axis=-1, keepdims=True)
    e2 = jnp.exp(m2 - m1)
    c_ref[...] = jnp.where(lane == i1, 1.0 / (1.0 + e2), 0.0) + jnp.where(lane == i2, e2 / (1.0 + e2), 0.0)
    sel = jnp.logical_or(lane == i1, lane == i2)
    sel_f = jnp.where(sel, 1.0, 0.0)
    earlier = _tri_masks(t)[0].astype(BF16)
    rank = jnp.dot(earlier, sel_f.astype(BF16), preferred_element_type=F32)
    rc = jnp.where(sel, rank, -1.0)
    rc_ref[...] = rc
    rr_ref[...] = rc.T[0:BF16_ROWS, :]
    cnt_ref[...] = jnp.broadcast_to(jnp.sum(sel_f, axis=0, keepdims=True), cnt_ref.shape).astype(jnp.int32)


def _router(x, g, router, tm):
    n = x.shape[0]
    return pl.pallas_call(
        _router_kernel,
        grid=(n // tm,),
        in_specs=[
            pl.BlockSpec((tm, D_MODEL), lambda i: (i, 0)),
            pl.BlockSpec((1, D_MODEL), lambda i: (0, 0)),
            pl.BlockSpec((D_MODEL, LANES), lambda i: (0, 0)),
        ],
        out_specs=[
            pl.BlockSpec((tm, D_MODEL), lambda i: (i, 0)),
            pl.BlockSpec((tm, LANES), lambda i: (i, 0)),
            pl.BlockSpec((tm, LANES), lambda i: (i, 0)),
            pl.BlockSpec((BF16_ROWS, tm), lambda i: (0, i)),
            pl.BlockSpec((8, LANES), lambda i: (i, 0)),
        ],
        out_shape=[
            jax.ShapeDtypeStruct((n, D_MODEL), BF16),
            jax.ShapeDtypeStruct((n, LANES), F32),
            jax.ShapeDtypeStruct((n, LANES), F32),
            jax.ShapeDtypeStruct((BF16_ROWS, n), F32),
            jax.ShapeDtypeStruct((n // tm * 8, LANES), jnp.int32),
        ],
        compiler_params=_cparams(("parallel",)),
        name="router",
    )(x, g, router)


def _moe_kernel(cnt_ref, x_ref, h_ref, c_ref, rc_ref, rr_ref, wg_ref, wu_ref, wd_ref, o_ref, xg_scr, yg_scr, *, rows):
    i = pl.program_id(0)
    e = pl.program_id(1)
    f = pl.program_id(2)
    last_f = f == pl.num_programs(2) - 1
    t = h_ref.shape[0]
    nsub = xg_scr.shape[0] // rows
    cnt = cnt_ref[i * N_EXPERTS + e]
    blocks = [(sb, slice(sb * rows, (sb + 1) * rows)) for sb in range(nsub)]

    @pl.when(jnp.logical_and(e == 0, f == 0))
    def _():
        o_ref[...] = x_ref[...]

    @pl.when(f == 0)
    def _():
        rr = rr_ref[pl.ds(e, 1), :]
        for sb, rs in blocks:
            @pl.when(cnt > sb * rows)
            def _():
                slot = (lax.broadcasted_iota(jnp.int32, (rows, t), 0) + sb * rows).astype(F32)
                gather = jnp.where(rr == slot, 1.0, 0.0).astype(BF16)
                xg_scr[rs, :] = jnp.dot(gather, h_ref[...], preferred_element_type=F32).astype(BF16)
                yg_scr[rs, :] = jnp.zeros((rows, D_MODEL), F32)

    for sb, rs in blocks:
        @pl.when(cnt > sb * rows)
        def _():
            xb = xg_scr[rs, :]
            act = (_silu(jnp.dot(xb, wg_ref[0], preferred_element_type=F32))
                   * jnp.dot(xb, wu_ref[0], preferred_element_type=F32))
            yg_scr[rs, :] += _mm(act, wd_ref[0])

    @pl.when(last_f)
    def _():
        lane = lax.broadcasted_iota(jnp.int32, (t, LANES), 1)
        mine = lane == e
        rc = jnp.sum(jnp.where(mine, rc_ref[...], 0.0), axis=-1, keepdims=True)
        ce = jnp.sum(jnp.where(mine, c_ref[...], 0.0), axis=-1, keepdims=True)
        for sb, rs in blocks:
            @pl.when(cnt > sb * rows)
            def _():
                slot = (lax.broadcasted_iota(jnp.int32, (t, rows), 1) + sb * rows).astype(F32)
                scatter = jnp.where(rc == slot, 1.0, 0.0).astype(BF16)
                o_ref[...] += ce * jnp.dot(scatter, yg_scr[rs, :].astype(BF16), preferred_element_type=F32)


def _moe(x, g, router, wg, wu, wd, tm, tf, rows):
    n = x.shape[0]
    ff = wg.shape[2]
    h, c, rc, rr, cnt = _router(x, g, router, tm)
    cnt = cnt.reshape(n // tm, 8, LANES)[:, 0, :N_EXPERTS].reshape(-1)
    nsub = -(-tm // rows)
    tok = lambda cols, **kw: pl.BlockSpec((tm, cols), lambda i, e, f, cnt: (i, 0), **kw)
    once = dict(pipeline_mode=pl.Buffered(1))
    return pl.pallas_call(
        functools.partial(_moe_kernel, rows=rows),
        grid_spec=pltpu.PrefetchScalarGridSpec(
            num_scalar_prefetch=1,
            grid=(n // tm, N_EXPERTS, ff // tf),
            in_specs=[
                tok(D_MODEL, **once), tok(D_MODEL, **once), tok(LANES), tok(LANES),
                pl.BlockSpec((BF16_ROWS, tm), lambda i, e, f, cnt: (0, i)),
                pl.BlockSpec((1, D_MODEL, tf), lambda i, e, f, cnt: (e, 0, f)),
                pl.BlockSpec((1, D_MODEL, tf), lambda i, e, f, cnt: (e, 0, f)),
                pl.BlockSpec((1, tf, D_MODEL), lambda i, e, f, cnt: (e, f, 0)),
            ],
            out_specs=tok(D_MODEL),
            scratch_shapes=[pltpu.VMEM((nsub * rows, D_MODEL), BF16), pltpu.VMEM((nsub * rows, D_MODEL), F32)],
        ),
        out_shape=jax.ShapeDtypeStruct((n, D_MODEL), F32),
        compiler_params=_cparams(("parallel", "arbitrary", "arbitrary"), vmem_mb=56),
        name="moe",
    )(cnt, x, h, c, rc, rr, wg, wu, wd)


def _ple_kernel(x_ref, p_ref, g_ref, wgate_ref, wproj_ref, fin_ref, o_ref, *, final):
    x = x_ref[...]
    h = _rms(x, g_ref[...], NORM_EPS)
    y = x + _sigmoid(_mm(h, wgate_ref[...])) * _mm(p_ref[...], wproj_ref[...])
    o_ref[...] = _rms(y, fin_ref[...], NORM_EPS) if final else y


def _ple(x, p, g, wgate, wproj, fin, tm, final):
    n = x.shape[0]
    return pl.pallas_call(
        functools.partial(_ple_kernel, final=final),
        grid=(n // tm,),
        in_specs=[
            pl.BlockSpec((tm, D_MODEL), lambda i: (i, 0)),
            pl.BlockSpec((tm, P_DIM), lambda i: (i, 0)),
            pl.BlockSpec((1, D_MODEL), lambda i: (0, 0)),
            pl.BlockSpec((D_MODEL, D_MODEL), lambda i: (0, 0)),
            pl.BlockSpec((P_DIM, D_MODEL), lambda i: (0, 0)),
            pl.BlockSpec((1, D_MODEL), lambda i: (0, 0)),
        ],
        out_specs=pl.BlockSpec((tm, D_MODEL), lambda i: (i, 0)),
        out_shape=jax.ShapeDtypeStruct((n, D_MODEL), F32),
        compiler_params=_cparams(("parallel",)),
        name="ple",
    )(x, p, g, wgate, wproj, fin)


def _tiles(n, seq):
    tm = min(512, seq)
    tm_big = 1024 if n % 1024 == 0 else tm
    tm_in = 2048 if n % 2048 == 0 else tm_big
    moe_rows = tm_big // 4 + tm_big // 32
    return dict(tm=tm, tm_big=tm_big, tm_in=tm_in, tn_in=1024, tq=min(512, seq), cblk=min(512, seq), moe_rows=moe_rows)


def _rope_tables(seq):
    half = ROPE_DIMS // 2
    inv = ROPE_THETA ** (-jnp.arange(half, dtype=F32) * 2.0 / ROPE_DIMS)
    ang = jnp.arange(seq, dtype=F32)[:, None] * inv[None, :]
    pad = jnp.zeros((seq, DIFF_DH - ROPE_DIMS), F32)
    cos = jnp.concatenate([jnp.cos(ang), jnp.cos(ang), pad + 1.0], axis=-1)
    sin = jnp.concatenate([jnp.sin(ang), jnp.sin(ang), pad], axis=-1)
    d = jnp.arange(W_MIX)
    dd = d % DIFF_DH
    src = jnp.where(dd < half, d + half, d - half)
    sign = jnp.where(dd < half, -1.0, jnp.where(dd < ROPE_DIMS, 1.0, 0.0))
    rot = jnp.zeros((W_MIX, W_MIX), F32).at[src, d].set(sign)
    reps = W_MIX // DIFF_DH
    return jnp.tile(cos, (1, reps)), jnp.tile(sin, (1, reps)), rot.astype(BF16)


def _split_w_in(w):
    a0 = 0
    b0 = a0 + 4 * W_MIX
    c0 = b0 + 3 * W_MIX
    d0 = c0 + 3 * W_MIX + N_HEADS
    g0 = d0 + 4 * W_MIX + 2 * N_HEADS
    d_small = d0 + 3 * W_MIX
    diff_q = w[:, b0:b0 + W_MIX] * (DIFF_DH ** -0.5 * LOG2E)
    fox_q = w[:, c0:c0 + W_MIX] * (HEAD_DIM ** -0.5 * LOG2E)
    main = jnp.concatenate([
        w[:, g0:], w[:, a0:b0], w[:, d0:d_small], w[:, d_small + 2 * N_HEADS:g0],
        diff_q, w[:, b0 + W_MIX:b0 + 2 * W_MIX], fox_q, w[:, c0 + W_MIX:c0 + 2 * W_MIX],
    ], axis=1).astype(BF16)
    small = jnp.concatenate([
        w[:, c0 + 3 * W_MIX:d0], w[:, d_small:d_small + 2 * N_HEADS],
        jnp.zeros((D_MODEL, LANES - 3 * N_HEADS), w.dtype),
    ], axis=1).astype(BF16)
    v_t = jnp.concatenate([w[:, b0 + 2 * W_MIX:c0], w[:, c0 + 2 * W_MIX:c0 + 3 * W_MIX]], axis=1).T.astype(BF16)
    return main, small, v_t


def _small_params(fbias, a_log, dt_bias):
    zeros = jnp.zeros((N_HEADS,), F32)
    bias = jnp.concatenate([fbias, zeros, dt_bias, jnp.zeros((LANES - 3 * N_HEADS,), F32)])
    neg_a = jnp.concatenate([zeros, zeros, -jnp.exp(a_log), jnp.zeros((LANES - 3 * N_HEADS,), F32)])
    return bias.reshape(1, LANES), neg_a.reshape(1, LANES)


def _pad_rows(w, top, total):
    return jnp.concatenate([jnp.zeros((top, w.shape[1]), w.dtype), w,
                            jnp.zeros((total - top - w.shape[0], w.shape[1]), w.dtype)], axis=0)


def kernel(x, p, norm_mix, norm_ffn, norm_ple, w_in, w_bo, w_out, rwkv_mu, rwkv_w0, rwkv_w2, rwkv_a0, rwkv_a2, rwkv_g2, rwkv_kk, rwkv_ka, rwkv_rk, rwkv_ln_w, rwkv_ln_b, diff_lam, diff_subln, fox_fbias, gdn_conv, gdn_a_log, gdn_dt_bias, gdn_norm, ffn_w_gate, ffn_w_up, ffn_w_down, moe_router, moe_w_gate, moe_w_up, moe_w_down, ple_proj, ple_gate, final_norm):
    batch, seq, _ = x.shape
    depth = w_in.shape[0]
    n = batch * seq
    t = _tiles(n, seq)
    tm, tq, cblk = t["tm"], t["tq"], t["cblk"]
    row = lambda v: v.reshape(1, -1).astype(F32)
    cos, sin, rot = _rope_tables(seq)
    xf = x.reshape(n, D_MODEL)
    pf = p.reshape(depth, n, P_DIM)

    for i in range(depth):
        w_main, w_small, w_vt = _split_w_in(w_in[i])
        u, scol, vt = _inproj(xf, row(norm_mix[i]), w_main, w_small, w_vt, t["tm_in"], t["tn_in"])
        bias, neg_a = _small_params(fox_fbias[i], gdn_a_log[i], gdn_dt_bias[i])
        hcol, fox_kb, fox_qb = _small_prep(scol, bias, neg_a, batch, seq, tm)

        scan_in, pc, post = _rwkv_prep(
            u, row(rwkv_mu[i]), row(rwkv_w0[i]), _pad_rows(rwkv_w2[i], 0, LANES), row(rwkv_a0[i]),
            _pad_rows(rwkv_a2[i], LANES // 2, LANES), rwkv_g2[i], row(rwkv_kk[i]), row(rwkv_ka[i]), row(rwkv_rk[i]),
            seq, tm)
        o_a = _rwkv_chunk(scan_in, pc, batch, seq, cblk)

        o_d = _gdn_chunk(_gdn_prep(u, gdn_conv[i].T, seq, tm), hcol, jnp.tile(row(gdn_norm[i]), (1, N_HEADS)),
                         batch, seq, cblk)

        lam_init = 0.8 - 0.6 * math.exp(-0.3 * i)
        y_b = _diff_attention(u, vt, cos, sin, rot, diff_lam[i].astype(F32), jnp.tile(row(diff_subln[i]), (1, 2)),
                              batch, seq, tq, lam_init)
        y_c = _fox_attention(u, vt, fox_qb, fox_kb, batch, seq, tq)

        xf = _merge(xf, u, o_a, post, y_b, y_c, o_d, row(rwkv_ln_w[i]), row(rwkv_ln_b[i]),
                    w_bo[i].astype(BF16), w_out[i].astype(BF16), tm)

        j = i // 2
        if i % 2 == 0:
            xf = _ffn(xf, row(norm_ffn[i]), ffn_w_gate[j].astype(BF16), ffn_w_up[j].astype(BF16),
                      ffn_w_down[j].astype(BF16), tm, ffn_w_gate.shape[2] // 2)
        else:
            router = jnp.concatenate([moe_router[j], jnp.zeros((D_MODEL, LANES - N_EXPERTS), F32)], axis=1)
            xf = _moe(xf, row(norm_ffn[i]), router, moe_w_gate[j].astype(BF16), moe_w_up[j].astype(BF16),
                      moe_w_down[j].astype(BF16), t["tm_big"], moe_w_gate.shape[3] // 2, t["moe_rows"])
        xf = _ple(xf, pf[i], row(norm_ple[i]), ple_gate[i].astype(BF16), ple_proj[i].astype(BF16),
                  row(final_norm), tm, i == depth - 1)
    return xf.reshape(batch, seq, D_MODEL)
```

```python
import functools
import math

import jax
import jax.numpy as jnp
import numpy as np
from jax import lax
from jax.experimental import pallas as pl
from jax.experimental.pallas import tpu as pltpu

F32 = jnp.float32
BF16 = jnp.bfloat16

D_MODEL = 1024
P_DIM = 256
W_MIX = 256
HEAD_DIM = 64
N_HEADS = 4
DIFF_DH = 32
ROPE_THETA = 500000.0
ROPE_DIMS = 8
RWKV_GN_EPS = 64e-5
DIFF_LN_EPS = 1e-5
GDN_CONV = 4
CHUNK = 64
CHUNK_UNROLL = 4
N_EXPERTS = 8
NORM_EPS = 1e-6
L2_EPS = 1e-6
LOG2E = math.log2(math.e)
LANES = 128
BF16_ROWS = 16
assert CHUNK == HEAD_DIM

U_GATE = 0
U_RWKV = 4096
U_GDN = 5120
U_DIFF = 6144
U_FOX = 6656
U_COLS = 7168
SM_FOX, SM_BETA, SM_DEC = 0, 4, 8
FOX_CK, FOX_CQ = 0, 16


def _cparams(semantics, vmem_mb=48):
    return pltpu.CompilerParams(dimension_semantics=semantics, vmem_limit_bytes=vmem_mb * 1024 * 1024)


def _mm(a, b):
    return jnp.dot(a.astype(BF16), b.astype(BF16), preferred_element_type=F32)


def _mm_nt(a, b):
    return lax.dot_general(a.astype(BF16), b.astype(BF16), (((1,), (1,)), ((), ())), preferred_element_type=F32)


def _mm_tn(a, b):
    return lax.dot_general(a.astype(BF16), b.astype(BF16), (((0,), (0,)), ((), ())), preferred_element_type=F32)


def _mm_split(a, b):
    ah = a.astype(BF16)
    al = (a - ah.astype(F32)).astype(BF16)
    bh = b.astype(BF16)
    bl = (b - bh.astype(F32)).astype(BF16)
    dot = lambda x, y: jnp.dot(x, y, preferred_element_type=F32)
    return dot(ah, bh) + (dot(ah, bl) + dot(al, bh))


def _mm_mask(mask, x):
    hi = x.astype(BF16)
    r1 = x - hi.astype(F32)
    mid = r1.astype(BF16)
    lo = (r1 - mid.astype(F32)).astype(BF16)
    dot = lambda t: jnp.dot(mask, t, preferred_element_type=F32)
    return dot(hi) + dot(mid) + dot(lo)


def _head_sums(x, same):
    hi = x.astype(BF16)
    lo = (x - hi.astype(F32)).astype(BF16)
    return jnp.dot(hi, same, preferred_element_type=F32) + jnp.dot(lo, same, preferred_element_type=F32)


def _rms(x, g, eps):
    return x * lax.rsqrt(jnp.mean(x * x, axis=-1, keepdims=True) + eps) * g


def _sigmoid(x):
    return 0.5 * jnp.tanh(0.5 * x) + 0.5


def _silu(x):
    return x * _sigmoid(x)


def _softplus(x):
    return jnp.maximum(x, 0.0) + jnp.log(1.0 + jnp.exp(-jnp.abs(x)))


def _tri_masks(c):
    ii = lax.broadcasted_iota(jnp.int32, (c, c), 0)
    jj = lax.broadcasted_iota(jnp.int32, (c, c), 1)
    return ii > jj, ii >= jj, ii == jj


def _block_masks():
    ii = lax.broadcasted_iota(jnp.int32, (W_MIX, W_MIX), 0)
    jj = lax.broadcasted_iota(jnp.int32, (W_MIX, W_MIX), 1)
    return (ii // HEAD_DIM) == (jj // HEAD_DIM), ii == jj


def _wide_masks():
    ii = lax.broadcasted_iota(jnp.int32, (CHUNK, W_MIX), 0)
    jj = lax.broadcasted_iota(jnp.int32, (CHUNK, W_MIX), 1) % HEAD_DIM
    return ii > jj, ii >= jj, ii == jj


def _stack_heads(x, same):
    return jnp.where(same, jnp.concatenate([x, x, x, x], axis=0), jnp.zeros((), x.dtype))


def _unit_lower_inverses(ns, eye_w, same):
    rs = [eye_w + n for n in ns]
    ps = list(ns)
    for _ in range(int(math.log2(CHUNK)) - 1):
        ps = [_mm(p, _stack_heads(p.astype(BF16), same)) for p in ps]
        rs = [r + _mm(r, _stack_heads(p.astype(BF16), same)) for r, p in zip(rs, ps)]
    return rs


def _unstack_heads(x):
    return x[0:CHUNK] + x[CHUNK:2 * CHUNK] + x[2 * CHUNK:3 * CHUNK] + x[3 * CHUNK:4 * CHUNK]


def _inproj_kernel(x_ref, g_ref, w_ref, ws_ref, wvt_ref, u_ref, scol_ref, vt_ref, h_scr):
    @pl.when(pl.program_id(1) == 0)
    def _():
        hb = _rms(x_ref[...], g_ref[...], NORM_EPS).astype(BF16)
        h_scr[...] = hb
        scol_ref[...] = jnp.dot(hb, ws_ref[...], preferred_element_type=F32)
        vt_ref[...] = lax.dot_general(wvt_ref[...], hb, (((1,), (1,)), ((), ())),
                                      preferred_element_type=F32).astype(BF16)

    u_ref[...] = jnp.dot(h_scr[...], w_ref[...], preferred_element_type=F32).astype(BF16)


def _inproj(x, g, w, ws, wvt, tm, tn):
    n = x.shape[0]
    return pl.pallas_call(
        _inproj_kernel,
        grid=(n // tm, U_COLS // tn),
        in_specs=[
            pl.BlockSpec((tm, D_MODEL), lambda i, j: (i, 0)),
            pl.BlockSpec((1, D_MODEL), lambda i, j: (0, 0)),
            pl.BlockSpec((D_MODEL, tn), lambda i, j: (0, j)),
            pl.BlockSpec((D_MODEL, LANES), lambda i, j: (0, 0)),
            pl.BlockSpec((2 * W_MIX, D_MODEL), lambda i, j: (0, 0)),
        ],
        out_specs=[
            pl.BlockSpec((tm, tn), lambda i, j: (i, j)),
            pl.BlockSpec((tm, LANES), lambda i, j: (i, 0)),
            pl.BlockSpec((2 * W_MIX, tm), lambda i, j: (0, i)),
        ],
        out_shape=[
            jax.ShapeDtypeStruct((n, U_COLS), BF16),
            jax.ShapeDtypeStruct((n, LANES), F32),
            jax.ShapeDtypeStruct((2 * W_MIX, n), BF16),
        ],
        scratch_shapes=[pltpu.VMEM((tm, D_MODEL), BF16)],
        compiler_params=_cparams(("parallel", "arbitrary"), vmem_mb=56),
        name="inproj",
    )(x, g, w, ws, wvt)


def _lane_placement(base):
    r = lax.broadcasted_iota(jnp.int32, (3 * LANES, LANES), 0)
    m = lax.broadcasted_iota(jnp.int32, (3 * LANES, LANES), 1)
    head, part = r % LANES, r // LANES
    return jnp.logical_and(head < N_HEADS, m == base + 3 * head + part).astype(BF16)


def _split3(x):
    hi = x.astype(BF16)
    r1 = x - hi.astype(F32)
    mid = r1.astype(BF16)
    lo = (r1 - mid.astype(F32)).astype(BF16)
    return jnp.concatenate([hi, mid, lo], axis=1)


def _head_expansion():
    r = lax.broadcasted_iota(jnp.int32, (3 * LANES, 2 * W_MIX), 0) % LANES
    m = lax.broadcasted_iota(jnp.int32, (3 * LANES, 2 * W_MIX), 1)
    src = jnp.where(m < W_MIX, SM_BETA + m // HEAD_DIM, SM_DEC + (m - W_MIX) // HEAD_DIM)
    return (r == src).astype(BF16)


def _small_prep_kernel(scol_ref, bias_ref, nega_ref, ogb_ref, okb_ref, oqb_ref, carry_scr):
    @pl.when(pl.program_id(1) == 0)
    def _():
        carry_scr[...] = jnp.zeros_like(carry_scr)

    tm = scol_ref.shape[0]
    lower = _tri_masks(tm)[1].astype(BF16)
    z = scol_ref[...] + bias_ref[...]
    lane = lax.broadcasted_iota(jnp.int32, z.shape, 1)
    is_f = lane < SM_BETA
    is_b = jnp.logical_and(lane >= SM_BETA, lane < SM_DEC)
    is_d = jnp.logical_and(lane >= SM_DEC, lane < SM_DEC + N_HEADS)
    logf = jnp.where(is_f, jnp.minimum(z, 0.0) - jnp.log(1.0 + jnp.exp(-jnp.abs(z))), 0.0)
    gdn = jnp.where(is_b, _sigmoid(z), jnp.where(is_d, nega_ref[...] * _softplus(z), 0.0))
    ogb_ref[...] = jnp.dot(_split3(gdn), _head_expansion(), preferred_element_type=F32)
    cum = _mm_mask(lower, logf) + carry_scr[...]
    carry_scr[...] = cum[tm - 1:tm, :]
    parts = _split3(cum * LOG2E)
    ones_lanes = jnp.logical_and(lane >= FOX_CQ, lane < FOX_CQ + 3 * N_HEADS)
    okb_ref[...] = (jnp.dot(parts, _lane_placement(FOX_CK), preferred_element_type=F32)
                    + jnp.where(ones_lanes, 1.0, 0.0)).astype(BF16)
    oqb_ref[...] = jnp.dot(parts, _lane_placement(FOX_CQ), preferred_element_type=F32).astype(BF16)


def _small_prep(scol, bias, neg_a, batch, seq, tm):
    n = batch * seq
    nt = seq // tm
    tok = lambda: pl.BlockSpec((tm, LANES), lambda b, j: (b * nt + j, 0))
    const = lambda: pl.BlockSpec((1, LANES), lambda b, j: (0, 0))
    return pl.pallas_call(
        _small_prep_kernel,
        grid=(batch, nt),
        in_specs=[tok(), const(), const()],
        out_specs=[pl.BlockSpec((tm, 2 * W_MIX), lambda b, j: (b * nt + j, 0)), tok(), tok()],
        out_shape=[jax.ShapeDtypeStruct((n, 2 * W_MIX), F32), jax.ShapeDtypeStruct((n, LANES), BF16),
                   jax.ShapeDtypeStruct((n, LANES), BF16)],
        scratch_shapes=[pltpu.VMEM((1, LANES), F32)],
        compiler_params=_cparams(("parallel", "arbitrary")),
        name="small_prep",
    )(scol, bias, neg_a)


def _rwkv_prep_kernel(u_ref, up_ref, mu_ref, w0_ref, w2_ref, a0_ref, a2_ref, g2_ref, kk_ref, ka_ref, rk_ref,
                      scan_ref, pc_ref, post_ref, *, tiles_per_seq):
    tm = u_ref.shape[0]
    u = u_ref[...].astype(F32)
    prev = up_ref[...].astype(F32)[BF16_ROWS - 1:BF16_ROWS, :]
    prev = jnp.where(pl.program_id(0) % tiles_per_seq == 0, 0.0, prev)
    rows = lax.broadcasted_iota(jnp.int32, (tm, 1), 0)
    u_prev = jnp.where(rows == 0, prev, pltpu.roll(u, 1, 0))
    xm = u + (u_prev - u) * mu_ref[...]
    r = xm[:, 0:W_MIX]
    k = xm[:, W_MIX:2 * W_MIX]
    v = xm[:, 2 * W_MIX:3 * W_MIX]
    x_lora = xm[:, 3 * W_MIX:3 * W_MIX + LANES]
    xg = xm[:, 3 * W_MIX + LANES:]
    logw = -_softplus(-(w0_ref[...] + _mm(jnp.tanh(x_lora), w2_ref[...]))) - 0.5
    log_decay = -jnp.exp(logw)
    a = _sigmoid(a0_ref[...] + _mm(x_lora, a2_ref[...]))
    g = _mm(_sigmoid(xg), g2_ref[...])
    same = _block_masks()[0].astype(BF16)
    kk_raw = k * kk_ref[...]
    kk = kk_raw * lax.rsqrt(_head_sums(kk_raw * kk_raw, same) + L2_EPS)
    k2 = k * (1.0 + (a - 1.0) * ka_ref[...])
    bonus = _head_sums(r * k2 * rk_ref[...], same) * v
    ti = lax.broadcasted_iota(jnp.int32, (tm, tm), 0)
    tj = lax.broadcasted_iota(jnp.int32, (tm, tm), 1)
    in_chunk = jnp.logical_and(ti // CHUNK == tj // CHUNK, ti >= tj).astype(BF16)
    ci = lax.broadcasted_iota(jnp.int32, (tm // CHUNK, tm), 0)
    cj = lax.broadcasted_iota(jnp.int32, (tm // CHUNK, tm), 1)
    cum = _mm_mask(in_chunk, log_decay)
    cum_end = _mm_mask((ci == cj // CHUNK).astype(BF16), log_decay)
    inv = jnp.exp(-cum)
    scan_ref[0] = (-kk * jnp.exp(cum - log_decay)).astype(BF16)
    scan_ref[1] = (kk * a * inv).astype(BF16)
    scan_ref[2] = (k2 * inv).astype(BF16)
    scan_ref[3] = (r * jnp.exp(cum)).astype(BF16)
    scan_ref[4] = v.astype(BF16)
    pc_ref[...] = jnp.exp(cum_end)
    post_ref[0] = g.astype(BF16)
    post_ref[1] = bonus.astype(BF16)


def _rwkv_prep(u, mu, w0, w2p, a0, a2p, g2, k_k, k_a, r_k, seq, tm):
    n = u.shape[0]
    ublk = U_RWKV // D_MODEL
    row = lambda c: pl.BlockSpec((1, c), lambda i: (0, 0))
    mat = lambda r: pl.BlockSpec((r, W_MIX), lambda i: (0, 0))
    return pl.pallas_call(
        functools.partial(_rwkv_prep_kernel, tiles_per_seq=seq // tm),
        grid=(n // tm,),
        in_specs=[
            pl.BlockSpec((tm, D_MODEL), lambda i: (i, ublk)),
            pl.BlockSpec((BF16_ROWS, D_MODEL), lambda i: (jnp.maximum(i * (tm // BF16_ROWS) - 1, 0), ublk)),
            row(D_MODEL), row(W_MIX), mat(LANES), row(W_MIX), mat(LANES), mat(LANES), row(W_MIX), row(W_MIX), row(W_MIX),
        ],
        out_specs=[
            pl.BlockSpec((5, tm, W_MIX), lambda i: (0, i, 0)),
            pl.BlockSpec((tm // CHUNK, W_MIX), lambda i: (i, 0)),
            pl.BlockSpec((2, tm, W_MIX), lambda i: (0, i, 0)),
        ],
        out_shape=[jax.ShapeDtypeStruct((5, n, W_MIX), BF16), jax.ShapeDtypeStruct((n // CHUNK, W_MIX), F32),
                   jax.ShapeDtypeStruct((2, n, W_MIX), BF16)],
        compiler_params=_cparams(("parallel",)),
        name="rwkv_prep",
    )(u, u, mu, w0, w2p, a0, a2p, g2, k_k, k_a, r_k)


def _rwkv_chunk_kernel(x_ref, pc_ref, o_ref, s_scr, *, nchunk, nbatch):
    @pl.when(pl.program_id(0) == 0)
    def _():
        s_scr[...] = jnp.zeros_like(s_scr)

    same, diag = _block_masks()
    eye = diag.astype(F32)
    strict_w, incl_w, diag_w = _wide_masks()
    eye_w = diag_w.astype(F32)
    stack = lambda x: _stack_heads(x.astype(BF16), same)

    def chunks(i, carry):
        items = [(b, i * CHUNK_UNROLL + j) for j in range(CHUNK_UNROLL) for b in range(nbatch)]
        each = lambda f, *lists: [f(*args) for args in zip(*lists)]
        sls = [pl.ds(pl.multiple_of(ci * CHUNK, CHUNK), CHUNK) for _, ci in items]
        a, bb, k, r, v = ([x_ref[i, b, sl, :] for (b, _), sl in zip(items, sls)] for i in range(5))
        pc = [pc_ref[b, pl.ds(ci, 1), :] for b, ci in items]
        a_s, b_s, k_s, v_s = (each(stack, x) for x in (a, bb, k, v))
        m_ab = each(lambda x, y: jnp.where(strict_w, _mm_nt(x, y), 0.0), a, b_s)
        m_ak = each(lambda x, y: jnp.where(strict_w, _mm_nt(x, y), 0.0), a, k_s)
        n_rb = each(lambda x, y: jnp.where(incl_w, _mm_nt(x, y), 0.0), r, b_s)
        n_rk = each(lambda x, y: jnp.where(incl_w, _mm_nt(x, y), 0.0), r, k_s)
        t_inv = _unit_lower_inverses(m_ab, eye_w, same)
        a2 = each(_mm, t_inv, a_s)
        u0 = each(lambda t, m, x: _mm(t, stack(_mm(m, x))), t_inv, m_ak, v_s)
        r2 = each(lambda x, n, y: x.astype(F32) + _mm(n, stack(y)), r, n_rb, a2)
        o0 = each(lambda n1, u, n2, x: _mm(n1, stack(u)) + _mm(n2, x), n_rb, u0, n_rk, v_s)
        b_end = each(lambda x, p: x.astype(F32) * p, bb, pc)
        k_end = each(lambda x, p: x.astype(F32) * p, k, pc)
        g_mat = each(lambda p, x, y: eye * p + jnp.where(same, _mm_tn(x, y), 0.0), pc, a2, b_end)
        s0 = each(lambda u, x, y, z: jnp.where(same, _mm_tn(u, x) + _mm_tn(y, z), 0.0), u0, b_end, v, k_end)
        for n, ((b, _), sl) in enumerate(zip(items, sls)):
            s = s_scr[b]
            o = _stack_heads(_mm_nt(r2[n], s) + o0[n], same)
            s_scr[b] = _mm(s, g_mat[n]) + s0[n]
            mean = jnp.sum(o, axis=-1, keepdims=True) * (1.0 / HEAD_DIM)
            cen = jnp.where(same, o - mean, 0.0)
            var = jnp.sum(cen * cen, axis=-1, keepdims=True) * (1.0 / HEAD_DIM)
            o_ref[b, sl, :] = _unstack_heads(cen * lax.rsqrt(var + RWKV_GN_EPS)).astype(BF16)
        return carry

    lax.fori_loop(0, nchunk // CHUNK_UNROLL, chunks, 0)


def _rwkv_chunk(xs, pc, batch, seq, cblk):
    xs = xs.reshape(5, batch, seq, W_MIX)
    pc = pc.reshape(batch, seq // CHUNK, W_MIX)
    out = pl.pallas_call(
        functools.partial(_rwkv_chunk_kernel, nchunk=cblk // CHUNK, nbatch=batch),
        grid=(seq // cblk,),
        in_specs=[
            pl.BlockSpec((5, batch, cblk, W_MIX), lambda j: (0, 0, j, 0)),
            pl.BlockSpec((batch, cblk // CHUNK, W_MIX), lambda j: (0, j, 0)),
        ],
        out_specs=pl.BlockSpec((batch, cblk, W_MIX), lambda j: (0, j, 0)),
        out_shape=jax.ShapeDtypeStruct((batch, seq, W_MIX), BF16),
        scratch_shapes=[pltpu.VMEM((batch, W_MIX, W_MIX), F32)],
        compiler_params=_cparams(("arbitrary",)),
        name="rwkv_chunk",
    )(xs, pc)
    return out.reshape(batch * seq, W_MIX)


def _gdn_prep_kernel(u_ref, up_ref, cw_ref, o_ref, ext_scr, *, tiles_per_seq):
    tm = u_ref.shape[0]
    c3 = 3 * W_MIX
    prev = up_ref[...].astype(F32)[:, :c3]
    ext_scr[0:BF16_ROWS, :] = jnp.where(pl.program_id(0) % tiles_per_seq == 0, 0.0, prev)
    ext_scr[BF16_ROWS:, :] = u_ref[...].astype(F32)[:, :c3]
    y = jnp.zeros((tm, c3), F32)
    for j in range(GDN_CONV):
        y = y + ext_scr[pl.ds(BF16_ROWS - (GDN_CONV - 1) + j, tm), :] * cw_ref[j:j + 1, :]
    y = _silu(y)
    same = _block_masks()[0].astype(BF16)
    q = y[:, 0:W_MIX]
    k = y[:, W_MIX:2 * W_MIX]
    o_ref[0] = (q * lax.rsqrt(_head_sums(q * q, same) + L2_EPS) * (HEAD_DIM ** -0.5)).astype(BF16)
    o_ref[1] = (k * lax.rsqrt(_head_sums(k * k, same) + L2_EPS)).astype(BF16)
    o_ref[2] = y[:, 2 * W_MIX:].astype(BF16)


def _gdn_prep(u, conv_w, seq, tm):
    n = u.shape[0]
    ublk = U_GDN // D_MODEL
    return pl.pallas_call(
        functools.partial(_gdn_prep_kernel, tiles_per_seq=seq // tm),
        grid=(n // tm,),
        in_specs=[
            pl.BlockSpec((tm, D_MODEL), lambda i: (i, ublk)),
            pl.BlockSpec((BF16_ROWS, D_MODEL), lambda i: (jnp.maximum(i * (tm // BF16_ROWS) - 1, 0), ublk)),
            pl.BlockSpec((GDN_CONV, 3 * W_MIX), lambda i: (0, 0)),
        ],
        out_specs=pl.BlockSpec((3, tm, W_MIX), lambda i: (0, i, 0)),
        out_shape=jax.ShapeDtypeStruct((3, n, W_MIX), BF16),
        scratch_shapes=[pltpu.VMEM((tm + BF16_ROWS, 3 * W_MIX), F32)],
        compiler_params=_cparams(("parallel",)),
        name="gdn_prep",
    )(u, u, conv_w)


def _gdn_chunk_kernel(x_ref, gb_ref, nw_ref, o_ref, s_scr, *, nchunk, nbatch):
    @pl.when(pl.program_id(0) == 0)
    def _():
        s_scr[...] = jnp.zeros_like(s_scr)

    same, diag = _block_masks()
    eye = diag.astype(F32)
    strict_w, incl_w, diag_w = _wide_masks()
    eye_w = diag_w.astype(F32)
    lower = _tri_masks(CHUNK)[1].astype(BF16)
    stack = lambda x: _stack_heads(x.astype(BF16), same)

    def chunks(i, carry):
        items = [(b, i * CHUNK_UNROLL + j) for j in range(CHUNK_UNROLL) for b in range(nbatch)]
        each = lambda f, *lists: [f(*args) for args in zip(*lists)]
        sls = [pl.ds(pl.multiple_of(ci * CHUNK, CHUNK), CHUNK) for _, ci in items]
        q, k, v = ([x_ref[i, b, sl, :] for (b, _), sl in zip(items, sls)] for i in range(3))
        beta = [gb_ref[b, sl, 0:W_MIX] for (b, _), sl in zip(items, sls)]
        g = [gb_ref[b, sl, W_MIX:2 * W_MIX] for (b, _), sl in zip(items, sls)]
        k_s = each(stack, k)
        gam = [_mm_mask(lower, x) for x in g]
        gam_end = [x[CHUNK - 1:CHUNK, :] for x in gam]
        gdiff = [_mm_mask(lower, jnp.where(strict_w, x, 0.0)) for x in g]
        decay = [jnp.exp(jnp.where(incl_w, x, -jnp.inf)) for x in gdiff]
        a_mat = each(lambda bt, d, x, y: jnp.where(strict_w, bt * d * _mm_nt(x, y), 0.0), beta, decay, k, k_s)
        t_inv = _unit_lower_inverses([-a for a in a_mat], eye_w, same)
        e_gam = [jnp.exp(x) for x in gam]
        u0 = each(lambda t, bt, x: _mm(t, stack(bt * x.astype(F32))), t_inv, beta, v)
        wm = each(lambda t, bt, e, x: _mm(t, stack((bt * e) * x.astype(F32))), t_inv, beta, e_gam, k)
        qk = each(lambda x, y, d: _mm_nt(x, y) * d, q, k_s, decay)
        q2 = each(lambda e, x, a, w: e * x.astype(F32) - _mm(a, stack(w)), e_gam, q, qk, wm)
        o0 = each(lambda a, u: _mm(a, stack(u)), qk, u0)
        k_end = each(lambda x, ge, ga: x.astype(F32) * jnp.exp(ge - ga), k, gam_end, gam)
        g_mat = each(lambda ge, x, w: eye * jnp.exp(ge) - jnp.where(same, _mm_tn(x, w), 0.0), gam_end, k_end, wm)
        s0 = each(lambda x, u: jnp.where(same, _mm_tn(x, u), 0.0), k_end, u0)
        for n, ((b, _), sl) in enumerate(zip(items, sls)):
            s = s_scr[b]
            o = _stack_heads(_mm(q2[n], s) + o0[n], same)
            s_scr[b] = _mm(g_mat[n], s) + s0[n]
            ms = jnp.sum(o * o, axis=-1, keepdims=True) * (1.0 / HEAD_DIM)
            o_ref[b, sl, :] = (_unstack_heads(o * lax.rsqrt(ms + NORM_EPS)) * nw_ref[...]).astype(BF16)
        return carry

    lax.fori_loop(0, nchunk // CHUNK_UNROLL, chunks, 0)


def _gdn_chunk(xs, gb, norm_w, batch, seq, cblk):
    xs = xs.reshape(3, batch, seq, W_MIX)
    gb = gb.reshape(batch, seq, 2 * W_MIX)
    out = pl.pallas_call(
        functools.partial(_gdn_chunk_kernel, nchunk=cblk // CHUNK, nbatch=batch),
        grid=(seq // cblk,),
        in_specs=[
            pl.BlockSpec((3, batch, cblk, W_MIX), lambda j: (0, 0, j, 0)),
            pl.BlockSpec((batch, cblk, 2 * W_MIX), lambda j: (0, j, 0)),
            pl.BlockSpec((1, W_MIX), lambda j: (0, 0)),
        ],
        out_specs=pl.BlockSpec((batch, cblk, W_MIX), lambda j: (0, j, 0)),
        out_shape=jax.ShapeDtypeStruct((batch, seq, W_MIX), BF16),
        scratch_shapes=[pltpu.VMEM((batch, W_MIX, W_MIX), F32)],
        compiler_params=_cparams(("arbitrary",)),
        name="gdn_chunk",
    )(xs, gb, norm_w)
    return out.reshape(batch * seq, W_MIX)


def _causal_pairs(nq):
    pairs = [(i, j) for i in range(nq) for j in range(i + 1)]
    return jnp.asarray(np.array([p[0] for p in pairs], np.int32)), jnp.asarray(np.array([p[1] for p in pairs], np.int32))


def _softmax_updates(scores, vt_ones, m_scr, acc_scr):
    probs = []
    for i, s in enumerate(scores):
        m_old = m_scr[i]
        m_new = jnp.maximum(m_old, jnp.max(s, axis=0, keepdims=True))
        m_scr[i] = m_new
        probs.append((jnp.exp2(m_old - m_new), jnp.exp2((s - m_new).astype(BF16))))
    for i, (alpha, p) in enumerate(probs):
        acc_scr[i] = alpha * acc_scr[i] + jnp.dot(vt_ones[i], p, preferred_element_type=F32)


def _pair_lanes(h):
    p = h // 2
    return slice(p * LANES, (p + 1) * LANES), h % 2 == 0


ACC_ROWS = HEAD_DIM + BF16_ROWS


def _values_and_ones(vt, tk):
    ones = jnp.ones((BF16_ROWS, tk), vt.dtype)
    return [jnp.concatenate([vt[h * HEAD_DIM:(h + 1) * HEAD_DIM, :], ones], axis=0) for h in range(N_HEADS)]


def _normalized(acc):
    return acc[0:HEAD_DIM] / acc[HEAD_DIM:HEAD_DIM + 1]


def _key_after_query(tq):
    return lax.broadcasted_iota(jnp.int32, (tq, tq), 0) > lax.broadcasted_iota(jnp.int32, (tq, tq), 1)


def _fox_kernel(qi_ref, kj_ref, q_ref, k_ref, vt_ref, qb_ref, kb_ref, o_ref, qm_scr, m_scr, acc_scr):
    t = pl.program_id(1)
    qi = qi_ref[t]
    kj = kj_ref[t]
    tq = q_ref.shape[0]

    @pl.when(kj == 0)
    def _():
        m_scr[...] = jnp.full_like(m_scr, -jnp.inf)
        acc_scr[...] = jnp.zeros_like(acc_scr)
        q = q_ref[...]
        qb = qb_ref[...]
        lane = lax.broadcasted_iota(jnp.int32, (tq, LANES), 1)
        zero = jnp.zeros((), BF16)
        for h in range(N_HEADS):
            slab, low = _pair_lanes(h)
            mine = lane < HEAD_DIM if low else lane >= HEAD_DIM
            ck_lanes = jnp.logical_and(lane >= FOX_CK + 3 * h, lane < FOX_CK + 3 * h + 3)
            cq_lanes = jnp.logical_and(lane >= FOX_CQ + 3 * h, lane < FOX_CQ + 3 * h + 3)
            bias = jnp.where(ck_lanes, -jnp.ones((), BF16), jnp.where(cq_lanes, qb, zero))
            qm_scr[h] = jnp.concatenate([jnp.where(mine, q[:, slab], zero), bias], axis=1)

    def step(diagonal):
        k = k_ref[...]
        kb = kb_ref[...]
        vts = _values_and_ones(vt_ref[...], tq)
        if diagonal:
            masked = _key_after_query(tq)
        keys = [jnp.concatenate([k[:, p * LANES:(p + 1) * LANES], kb], axis=1) for p in range(N_HEADS // 2)]
        scores = []
        for h in range(N_HEADS):
            s = lax.dot_general(keys[h // 2], qm_scr[h], (((1,), (1,)), ((), ())), preferred_element_type=F32)
            if diagonal:
                s = jnp.where(masked, -jnp.inf, s)
            scores.append(s)
        _softmax_updates(scores, vts, m_scr, acc_scr)

    @pl.when(kj < qi)
    def _():
        step(False)

    @pl.when(kj == qi)
    def _():
        step(True)
        for p in range(N_HEADS // 2):
            pair = jnp.concatenate([_normalized(acc_scr[2 * p]), _normalized(acc_scr[2 * p + 1])], axis=0)
            o_ref[:, p * LANES:(p + 1) * LANES] = pair.T.astype(BF16)


def _attn_specs(nq, tq, ucol, vt_rows):
    cb = ucol // W_MIX
    q_spec = pl.BlockSpec((tq, W_MIX), lambda b, t, qi, kj: (b * nq + qi[t], cb))
    k_spec = pl.BlockSpec((tq, W_MIX), lambda b, t, qi, kj: (b * nq + kj[t], cb + 1))
    vt_spec = pl.BlockSpec((W_MIX, tq), lambda b, t, qi, kj: (vt_rows // W_MIX, b * nq + kj[t]))
    return q_spec, k_spec, vt_spec


def _fox_attention(u, vt, qb, kb, batch, seq, tq):
    nq = seq // tq
    qi, kj = _causal_pairs(nq)
    q_spec, k_spec, vt_spec = _attn_specs(nq, tq, U_FOX, W_MIX)
    stat = lambda: pltpu.VMEM((N_HEADS, 1, tq), F32)
    return pl.pallas_call(
        _fox_kernel,
        grid_spec=pltpu.PrefetchScalarGridSpec(
            num_scalar_prefetch=2,
            grid=(batch, qi.shape[0]),
            in_specs=[
                q_spec, k_spec, vt_spec,
                pl.BlockSpec((tq, LANES), lambda b, t, qi, kj: (b * nq + qi[t], 0)),
                pl.BlockSpec((tq, LANES), lambda b, t, qi, kj: (b * nq + kj[t], 0)),
            ],
            out_specs=pl.BlockSpec((tq, W_MIX), lambda b, t, qi, kj: (b * nq + qi[t], 0)),
            scratch_shapes=[pltpu.VMEM((N_HEADS, tq, 2 * LANES), BF16), stat(),
                            pltpu.VMEM((N_HEADS, ACC_ROWS, tq), F32)],
        ),
        out_shape=jax.ShapeDtypeStruct((batch * seq, W_MIX), BF16),
        compiler_params=_cparams(("parallel", "arbitrary")),
        name="fox_attention",
    )(qi, kj, u, u, vt, qb, kb)


def _diff_kernel(qi_ref, kj_ref, q_ref, k_ref, vt_ref, cq_ref, sq_ref, ck_ref, sk_ref, rot_ref, lam_ref, ln_ref, o_ref,
                 qm_scr, m_scr, acc_scr, *, lam_init):
    t = pl.program_id(1)
    qi = qi_ref[t]
    kj = kj_ref[t]
    tq = q_ref.shape[0]
    lane = lax.broadcasted_iota(jnp.int32, (tq, LANES), 1)

    def rope(x, cos, sin):
        return x.astype(F32) * cos + jnp.dot(x, rot_ref[...], preferred_element_type=F32) * sin

    @pl.when(kj == 0)
    def _():
        m_scr[...] = jnp.full_like(m_scr, -jnp.inf)
        acc_scr[...] = jnp.zeros_like(acc_scr)
        q = rope(q_ref[...], cq_ref[...], sq_ref[...])
        for h in range(N_HEADS):
            slab, low = _pair_lanes(h)
            base = 0 if low else HEAD_DIM
            for c in range(2):
                lo = base + c * DIFF_DH
                sel = jnp.logical_and(lane >= lo, lane < lo + DIFF_DH)
                qm_scr[2 * h + c] = jnp.where(sel, q[:, slab], 0.0).astype(BF16)

    def step(diagonal):
        k = rope(k_ref[...], ck_ref[...], sk_ref[...]).astype(BF16)
        vts = _values_and_ones(vt_ref[...], tq)
        if diagonal:
            masked = _key_after_query(tq)
        scores = []
        for h in range(N_HEADS):
            slab, _ = _pair_lanes(h)
            for c in range(2):
                i = 2 * h + c
                s = lax.dot_general(k[:, slab], qm_scr[i], (((1,), (1,)), ((), ())), preferred_element_type=F32)
                if diagonal:
                    s = jnp.where(masked, -jnp.inf, s)
                scores.append(s)
        _softmax_updates(scores, [vts[i // 2] for i in range(2 * N_HEADS)], m_scr, acc_scr)

    @pl.when(kj < qi)
    def _():
        step(False)

    @pl.when(kj == qi)
    def _():
        step(True)
        lp = lam_ref[...]
        lam = (jnp.exp(jnp.sum(lp[0:1] * lp[1:2], axis=-1, keepdims=True))
               - jnp.exp(jnp.sum(lp[2:3] * lp[3:4], axis=-1, keepdims=True)) + lam_init)
        head = lambda h: _normalized(acc_scr[2 * h]) - lam * _normalized(acc_scr[2 * h + 1])
        is_lo = lane < HEAD_DIM
        for p in range(N_HEADS // 2):
            o = jnp.concatenate([head(2 * p), head(2 * p + 1)], axis=0).T
            sq = o * o
            ms_lo = jnp.sum(jnp.where(is_lo, sq, 0.0), axis=-1, keepdims=True)
            ms_hi = jnp.sum(jnp.where(is_lo, 0.0, sq), axis=-1, keepdims=True)
            ms = jnp.where(is_lo, ms_lo, ms_hi) * (1.0 / HEAD_DIM)
            o_ref[:, p * LANES:(p + 1) * LANES] = (o * lax.rsqrt(ms + DIFF_LN_EPS) * ln_ref[...]
                                                   * (1.0 - lam_init)).astype(BF16)


def _diff_attention(u, vt, cos, sin, rot, lam_p, subln, batch, seq, tq, lam_init):
    nq = seq // tq
    qi, kj = _causal_pairs(nq)
    q_spec, k_spec, vt_spec = _attn_specs(nq, tq, U_DIFF, 0)
    tab_q = pl.BlockSpec((tq, W_MIX), lambda b, t, qi, kj: (qi[t], 0))
    tab_k = pl.BlockSpec((tq, W_MIX), lambda b, t, qi, kj: (kj[t], 0))
    const = lambda r, c: pl.BlockSpec((r, c), lambda b, t, qi, kj: (0, 0))
    stat = lambda: pltpu.VMEM((2 * N_HEADS, 1, tq), F32)
    return pl.pallas_call(
        functools.partial(_diff_kernel, lam_init=lam_init),
        grid_spec=pltpu.PrefetchScalarGridSpec(
            num_scalar_prefetch=2,
            grid=(batch, qi.shape[0]),
            in_specs=[q_spec, k_spec, vt_spec, tab_q, tab_q, tab_k, tab_k,
                      const(W_MIX, W_MIX), const(4, DIFF_DH), const(1, LANES)],
            out_specs=pl.BlockSpec((tq, W_MIX), lambda b, t, qi, kj: (b * nq + qi[t], 0)),
            scratch_shapes=[pltpu.VMEM((2 * N_HEADS, tq, LANES), BF16), stat(),
                            pltpu.VMEM((2 * N_HEADS, ACC_ROWS, tq), F32)],
        ),
        out_shape=jax.ShapeDtypeStruct((batch * seq, W_MIX), BF16),
        compiler_params=_cparams(("parallel", "arbitrary")),
        name="diff_attention",
    )(qi, kj, u, u, vt, cos, sin, cos, sin, rot, lam_p, subln)


def _merge_kernel(x_ref, gate_ref, oa_ref, post_ref, yb_ref, yc_ref, od_ref, gd_ref, lnw_ref, lnb_ref,
                  wbo_ref, wout_ref, o_ref):
    y_a = (oa_ref[...].astype(F32) * lnw_ref[...] + lnb_ref[...] + post_ref[1].astype(F32)) * post_ref[0].astype(F32)
    y_d = od_ref[...].astype(F32) * _silu(gd_ref[...].astype(F32))
    acc = jnp.zeros(x_ref.shape, F32)
    for b, y in enumerate((y_a, yb_ref[...], yc_ref[...], y_d)):
        gate = _sigmoid(gate_ref[:, b * D_MODEL:(b + 1) * D_MODEL].astype(F32))
        acc = acc + gate * _mm(y, wbo_ref[b])
    o_ref[...] = x_ref[...] + _mm(acc, wout_ref[...])


def _merge(x, u, o_a, post, y_b, y_c, o_d, ln_w, ln_b, w_bo, w_out, tm):
    n = x.shape[0]
    tok = lambda c: pl.BlockSpec((tm, c), lambda i: (i, 0))
    return pl.pallas_call(
        _merge_kernel,
        grid=(n // tm,),
        in_specs=[
            tok(D_MODEL),
            pl.BlockSpec((tm, 4 * D_MODEL), lambda i: (i, U_GATE // (4 * D_MODEL))),
            tok(W_MIX),
            pl.BlockSpec((2, tm, W_MIX), lambda i: (0, i, 0)),
            tok(W_MIX), tok(W_MIX), tok(W_MIX),
            pl.BlockSpec((tm, W_MIX), lambda i: (i, (U_GDN + 3 * W_MIX) // W_MIX)),
            pl.BlockSpec((1, W_MIX), lambda i: (0, 0)),
            pl.BlockSpec((1, W_MIX), lambda i: (0, 0)),
            pl.BlockSpec((4, W_MIX, D_MODEL), lambda i: (0, 0, 0)),
            pl.BlockSpec((D_MODEL, D_MODEL), lambda i: (0, 0)),
        ],
        out_specs=tok(D_MODEL),
        out_shape=jax.ShapeDtypeStruct((n, D_MODEL), F32),
        compiler_params=_cparams(("parallel",)),
        name="merge",
    )(x, u, o_a, post, y_b, y_c, o_d, u, ln_w, ln_b, w_bo, w_out)


def _ffn_kernel(x_ref, g_ref, wg_ref, wu_ref, wd_ref, o_ref, h_scr, acc_scr):
    f = pl.program_id(1)

    @pl.when(f == 0)
    def _():
        h_scr[...] = _rms(x_ref[...], g_ref[...], NORM_EPS).astype(BF16)
        acc_scr[...] = jnp.zeros_like(acc_scr)

    h = h_scr[...]
    act = _silu(jnp.dot(h, wg_ref[...], preferred_element_type=F32)) * jnp.dot(h, wu_ref[...], preferred_element_type=F32)
    acc_scr[...] += _mm(act, wd_ref[...])

    @pl.when(f == pl.num_programs(1) - 1)
    def _():
        o_ref[...] = x_ref[...] + acc_scr[...]


def _ffn(x, g, wg, wu, wd, tm, tf):
    n = x.shape[0]
    ff = wg.shape[1]
    return pl.pallas_call(
        _ffn_kernel,
        grid=(n // tm, ff // tf),
        in_specs=[
            pl.BlockSpec((tm, D_MODEL), lambda i, f: (i, 0)),
            pl.BlockSpec((1, D_MODEL), lambda i, f: (0, 0)),
            pl.BlockSpec((D_MODEL, tf), lambda i, f: (0, f)),
            pl.BlockSpec((D_MODEL, tf), lambda i, f: (0, f)),
            pl.BlockSpec((tf, D_MODEL), lambda i, f: (f, 0)),
        ],
        out_specs=pl.BlockSpec((tm, D_MODEL), lambda i, f: (i, 0)),
        out_shape=jax.ShapeDtypeStruct((n, D_MODEL), F32),
        scratch_shapes=[pltpu.VMEM((tm, D_MODEL), BF16), pltpu.VMEM((tm, D_MODEL), F32)],
        compiler_params=_cparams(("parallel", "arbitrary")),
        name="ffn",
    )(x, g, wg, wu, wd)


def _router_kernel(x_ref, g_ref, router_ref, h_ref, c_ref, rc_ref, rr_ref, cnt_ref):
    t = x_ref.shape[0]
    h = _rms(x_ref[...], g_ref[...], NORM_EPS)
    h_ref[...] = h.astype(BF16)
    logits = _mm_split(h, router_ref[...])
    lane = lax.broadcasted_iota(jnp.int32, logits.shape, 1).astype(F32)
    lg = jnp.where(lane < N_EXPERTS, logits, -jnp.inf)
    m1 = jnp.max(lg, axis=-1, keepdims=True)
    i1 = jnp.min(jnp.where(lg == m1, lane, float(LANES)), axis=-1, keepdims=True)
    lg2 = jnp.where(lane == i1, -jnp.inf, lg)
    m2 = jnp.max(lg2, axis=-1, keepdims=True)
    i2 = jnp.min(jnp.where(lg2 == m2, lane, float(LANES)), axis=-1, keepdims=True)
    e2 = jnp.exp(m2 - m1)
    c_ref[...] = jnp.where(lane == i1, 1.0 / (1.0 + e2), 0.0) + jnp.where(lane == i2, e2 / (1.0 + e2), 0.0)
    sel = jnp.logical_or(lane == i1, lane == i2)
    sel_f = jnp.where(sel, 1.0, 0.0)
    earlier = _tri_masks(t)[0].astype(BF16)
    rank = jnp.dot(earlier, sel_f.astype(BF16), preferred_element_type=F32)
    rc = jnp.where(sel, rank, -1.0)
    rc_ref[...] = rc
    rr_ref[...] = rc.T[0:BF16_ROWS, :]
    cnt_ref[...] = jnp.broadcast_to(jnp.sum(sel_f, axis=0, keepdims=True), cnt_ref.shape).astype(jnp.int32)


def _router(x, g, router, tm):
    n = x.shape[0]
    return pl.pallas_call(
        _router_kernel,
        grid=(n // tm,),
        in_specs=[
            pl.BlockSpec((tm, D_MODEL), lambda i: (i, 0)),
            pl.BlockSpec((1, D_MODEL), lambda i: (0, 0)),
            pl.BlockSpec((D_MODEL, LANES), lambda i: (0, 0)),
        ],
        out_specs=[
            pl.BlockSpec((tm, D_MODEL), lambda i: (i, 0)),
            pl.BlockSpec((tm, LANES), lambda i: (i, 0)),
            pl.BlockSpec((tm, LANES), lambda i: (i, 0)),
            pl.BlockSpec((BF16_ROWS, tm), lambda i: (0, i)),
            pl.BlockSpec((8, LANES), lambda i: (i, 0)),
        ],
        out_shape=[
            jax.ShapeDtypeStruct((n, D_MODEL), BF16),
            jax.ShapeDtypeStruct((n, LANES), F32),
            jax.ShapeDtypeStruct((n, LANES), F32),
            jax.ShapeDtypeStruct((BF16_ROWS, n), F32),
            jax.ShapeDtypeStruct((n // tm * 8, LANES), jnp.int32),
        ],
        compiler_params=_cparams(("parallel",)),
        name="router",
    )(x, g, router)


def _moe_kernel(cnt_ref, x_ref, h_ref, c_ref, rc_ref, rr_ref, wg_ref, wu_ref, wd_ref, o_ref, xg_scr, yg_scr, *, rows):
    i = pl.program_id(0)
    e = pl.program_id(1)
    f = pl.program_id(2)
    last_f = f == pl.num_programs(2) - 1
    t = h_ref.shape[0]
    nsub = xg_scr.shape[0] // rows
    cnt = cnt_ref[i * N_EXPERTS + e]
    blocks = [(sb, slice(sb * rows, (sb + 1) * rows)) for sb in range(nsub)]

    @pl.when(jnp.logical_and(e == 0, f == 0))
    def _():
        o_ref[...] = x_ref[...]

    @pl.when(f == 0)
    def _():
        rr = rr_ref[pl.ds(e, 1), :]
        for sb, rs in blocks:
            @pl.when(cnt > sb * rows)
            def _():
                slot = (lax.broadcasted_iota(jnp.int32, (rows, t), 0) + sb * rows).astype(F32)
                gather = jnp.where(rr == slot, 1.0, 0.0).astype(BF16)
                xg_scr[rs, :] = jnp.dot(gather, h_ref[...], preferred_element_type=F32).astype(BF16)
                yg_scr[rs, :] = jnp.zeros((rows, D_MODEL), F32)

    for sb, rs in blocks:
        @pl.when(cnt > sb * rows)
        def _():
            xb = xg_scr[rs, :]
            act = (_silu(jnp.dot(xb, wg_ref[0], preferred_element_type=F32))
                   * jnp.dot(xb, wu_ref[0], preferred_element_type=F32))
            yg_scr[rs, :] += _mm(act, wd_ref[0])

    @pl.when(last_f)
    def _():
        lane = lax.broadcasted_iota(jnp.int32, (t, LANES), 1)
        mine = lane == e
        rc = jnp.sum(jnp.where(mine, rc_ref[...], 0.0), axis=-1, keepdims=True)
        ce = jnp.sum(jnp.where(mine, c_ref[...], 0.0), axis=-1, keepdims=True)
        for sb, rs in blocks:
            @pl.when(cnt > sb * rows)
            def _():
                slot = (lax.broadcasted_iota(jnp.int32, (t, rows), 1) + sb * rows).astype(F32)
                scatter = jnp.where(rc == slot, 1.0, 0.0).astype(BF16)
                o_ref[...] += ce * jnp.dot(scatter, yg_scr[rs, :].astype(BF16), preferred_element_type=F32)


def _moe(x, g, router, wg, wu, wd, tm, tf, rows):
    n = x.shape[0]
    ff = wg.shape[2]
    h, c, rc, rr, cnt = _router(x, g, router, tm)
    cnt = cnt.reshape(n // tm, 8, LANES)[:, 0, :N_EXPERTS].reshape(-1)
    nsub = -(-tm // rows)
    tok = lambda cols, **kw: pl.BlockSpec((tm, cols), lambda i, e, f, cnt: (i, 0), **kw)
    once = dict(pipeline_mode=pl.Buffered(1))
    return pl.pallas_call(
        functools.partial(_moe_kernel, rows=rows),
        grid_spec=pltpu.PrefetchScalarGridSpec(
            num_scalar_prefetch=1,
            grid=(n // tm, N_EXPERTS, ff // tf),
            in_specs=[
                tok(D_MODEL, **once), tok(D_MODEL, **once), tok(LANES), tok(LANES),
                pl.BlockSpec((BF16_ROWS, tm), lambda i, e, f, cnt: (0, i)),
                pl.BlockSpec((1, D_MODEL, tf), lambda i, e, f, cnt: (e, 0, f)),
                pl.BlockSpec((1, D_MODEL, tf), lambda i, e, f, cnt: (e, 0, f)),
                pl.BlockSpec((1, tf, D_MODEL), lambda i, e, f, cnt: (e, f, 0)),
            ],
            out_specs=tok(D_MODEL),
            scratch_shapes=[pltpu.VMEM((nsub * rows, D_MODEL), BF16), pltpu.VMEM((nsub * rows, D_MODEL), F32)],
        ),
        out_shape=jax.ShapeDtypeStruct((n, D_MODEL), F32),
        compiler_params=_cparams(("parallel", "arbitrary", "arbitrary"), vmem_mb=56),
        name="moe",
    )(cnt, x, h, c, rc, rr, wg, wu, wd)


def _ple_kernel(x_ref, p_ref, g_ref, wgate_ref, wproj_ref, fin_ref, o_ref, *, final):
    x = x_ref[...]
    h = _rms(x, g_ref[...], NORM_EPS)
    y = x + _sigmoid(_mm(h, wgate_ref[...])) * _mm(p_ref[...], wproj_ref[...])
    o_ref[...] = _rms(y, fin_ref[...], NORM_EPS) if final else y


def _ple(x, p, g, wgate, wproj, fin, tm, final):
    n = x.shape[0]
    return pl.pallas_call(
        functools.partial(_ple_kernel, final=final),
        grid=(n // tm,),
        in_specs=[
            pl.BlockSpec((tm, D_MODEL), lambda i: (i, 0)),
            pl.BlockSpec((tm, P_DIM), lambda i: (i, 0)),
            pl.BlockSpec((1, D_MODEL), lambda i: (0, 0)),
            pl.BlockSpec((D_MODEL, D_MODEL), lambda i: (0, 0)),
            pl.BlockSpec((P_DIM, D_MODEL), lambda i: (0, 0)),
            pl.BlockSpec((1, D_MODEL), lambda i: (0, 0)),
        ],
        out_specs=pl.BlockSpec((tm, D_MODEL), lambda i: (i, 0)),
        out_shape=jax.ShapeDtypeStruct((n, D_MODEL), F32),
        compiler_params=_cparams(("parallel",)),
        name="ple",
    )(x, p, g, wgate, wproj, fin)


def _tiles(n, seq):
    tm = min(512, seq)
    tm_big = 1024 if n % 1024 == 0 else tm
    tm_in = 2048 if n % 2048 == 0 else tm_big
    moe_rows = tm_big // 4 + tm_big // 32
    return dict(tm=tm, tm_big=tm_big, tm_in=tm_in, tn_in=1024, tq=min(512, seq), cblk=min(512, seq), moe_rows=moe_rows)


def _rope_tables(seq):
    half = ROPE_DIMS // 2
    inv = ROPE_THETA ** (-jnp.arange(half, dtype=F32) * 2.0 / ROPE_DIMS)
    ang = jnp.arange(seq, dtype=F32)[:, None] * inv[None, :]
    pad = jnp.zeros((seq, DIFF_DH - ROPE_DIMS), F32)
    cos = jnp.concatenate([jnp.cos(ang), jnp.cos(ang), pad + 1.0], axis=-1)
    sin = jnp.concatenate([jnp.sin(ang), jnp.sin(ang), pad], axis=-1)
    d = jnp.arange(W_MIX)
    dd = d % DIFF_DH
    src = jnp.where(dd < half, d + half, d - half)
    sign = jnp.where(dd < half, -1.0, jnp.where(dd < ROPE_DIMS, 1.0, 0.0))
    rot = jnp.zeros((W_MIX, W_MIX), F32).at[src, d].set(sign)
    reps = W_MIX // DIFF_DH
    return jnp.tile(cos, (1, reps)), jnp.tile(sin, (1, reps)), rot.astype(BF16)


def _split_w_in(w):
    a0 = 0
    b0 = a0 + 4 * W_MIX
    c0 = b0 + 3 * W_MIX
    d0 = c0 + 3 * W_MIX + N_HEADS
    g0 = d0 + 4 * W_MIX + 2 * N_HEADS
    d_small = d0 + 3 * W_MIX
    diff_q = w[:, b0:b0 + W_MIX] * (DIFF_DH ** -0.5 * LOG2E)
    fox_q = w[:, c0:c0 + W_MIX] * (HEAD_DIM ** -0.5 * LOG2E)
    main = jnp.concatenate([
        w[:, g0:], w[:, a0:b0], w[:, d0:d_small], w[:, d_small + 2 * N_HEADS:g0],
        diff_q, w[:, b0 + W_MIX:b0 + 2 * W_MIX], fox_q, w[:, c0 + W_MIX:c0 + 2 * W_MIX],
    ], axis=1).astype(BF16)
    small = jnp.concatenate([
        w[:, c0 + 3 * W_MIX:d0], w[:, d_small:d_small + 2 * N_HEADS],
        jnp.zeros((D_MODEL, LANES - 3 * N_HEADS), w.dtype),
    ], axis=1).astype(BF16)
    v_t = jnp.concatenate([w[:, b0 + 2 * W_MIX:c0], w[:, c0 + 2 * W_MIX:c0 + 3 * W_MIX]], axis=1).T.astype(BF16)
    return main, small, v_t


def _small_params(fbias, a_log, dt_bias):
    zeros = jnp.zeros((N_HEADS,), F32)
    bias = jnp.concatenate([fbias, zeros, dt_bias, jnp.zeros((LANES - 3 * N_HEADS,), F32)])
    neg_a = jnp.concatenate([zeros, zeros, -jnp.exp(a_log), jnp.zeros((LANES - 3 * N_HEADS,), F32)])
    return bias.reshape(1, LANES), neg_a.reshape(1, LANES)


def _pad_rows(w, top, total):
    return jnp.concatenate([jnp.zeros((top, w.shape[1]), w.dtype), w,
                            jnp.zeros((total - top - w.shape[0], w.shape[1]), w.dtype)], axis=0)


def kernel(x, p, norm_mix, norm_ffn, norm_ple, w_in, w_bo, w_out, rwkv_mu, rwkv_w0, rwkv_w2, rwkv_a0, rwkv_a2, rwkv_g2, rwkv_kk, rwkv_ka, rwkv_rk, rwkv_ln_w, rwkv_ln_b, diff_lam, diff_subln, fox_fbias, gdn_conv, gdn_a_log, gdn_dt_bias, gdn_norm, ffn_w_gate, ffn_w_up, ffn_w_down, moe_router, moe_w_gate, moe_w_up, moe_w_down, ple_proj, ple_gate, final_norm):
    batch, seq, _ = x.shape
    depth = w_in.shape[0]
    n = batch * seq
    t = _tiles(n, seq)
    tm, tq, cblk = t["tm"], t["tq"], t["cblk"]
    row = lambda v: v.reshape(1, -1).astype(F32)
    cos, sin, rot = _rope_tables(seq)
    xf = x.reshape(n, D_MODEL)
    pf = p.reshape(depth, n, P_DIM)

    for i in range(depth):
        w_main, w_small, w_vt = _split_w_in(w_in[i])
        u, scol, vt = _inproj(xf, row(norm_mix[i]), w_main, w_small, w_vt, t["tm_in"], t["tn_in"])
        bias, neg_a = _small_params(fox_fbias[i], gdn_a_log[i], gdn_dt_bias[i])
        hcol, fox_kb, fox_qb = _small_prep(scol, bias, neg_a, batch, seq, tm)

        scan_in, pc, post = _rwkv_prep(
            u, row(rwkv_mu[i]), row(rwkv_w0[i]), _pad_rows(rwkv_w2[i], 0, LANES), row(rwkv_a0[i]),
            _pad_rows(rwkv_a2[i], LANES // 2, LANES), rwkv_g2[i], row(rwkv_kk[i]), row(rwkv_ka[i]), row(rwkv_rk[i]),
            seq, tm)
        o_a = _rwkv_chunk(scan_in, pc, batch, seq, cblk)

        o_d = _gdn_chunk(_gdn_prep(u, gdn_conv[i].T, seq, tm), hcol, jnp.tile(row(gdn_norm[i]), (1, N_HEADS)),
                         batch, seq, cblk)

        lam_init = 0.8 - 0.6 * math.exp(-0.3 * i)
        y_b = _diff_attention(u, vt, cos, sin, rot, diff_lam[i].astype(F32), jnp.tile(row(diff_subln[i]), (1, 2)),
                              batch, seq, tq, lam_init)
        y_c = _fox_attention(u, vt, fox_qb, fox_kb, batch, seq, tq)

        xf = _merge(xf, u, o_a, post, y_b, y_c, o_d, row(rwkv_ln_w[i]), row(rwkv_ln_b[i]),
                    w_bo[i].astype(BF16), w_out[i].astype(BF16), tm)

        j = i // 2
        if i % 2 == 0:
            xf = _ffn(xf, row(norm_ffn[i]), ffn_w_gate[j].astype(BF16), ffn_w_up[j].astype(BF16),
                      ffn_w_down[j].astype(BF16), tm, ffn_w_gate.shape[2] // 2)
        else:
            router = jnp.concatenate([moe_router[j], jnp.zeros((D_MODEL, LANES - N_EXPERTS), F32)], axis=1)
            xf = _moe(xf, row(norm_ffn[i]), router, moe_w_gate[j].astype(BF16), moe_w_up[j].astype(BF16),
                      moe_w_down[j].astype(BF16), t["tm_big"], moe_w_gate.shape[3] // 2, t["moe_rows"])
        xf = _ple(xf, pf[i], row(norm_ple[i]), ple_gate[i].astype(BF16), ple_proj[i].astype(BF16),
                  row(final_norm), tm, i == depth - 1)
    return xf.reshape(batch, seq, D_MODEL)
```

```python
import functools
import math

import jax
import jax.numpy as jnp
import numpy as np
from jax import lax
from jax.experimental import pallas as pl
from jax.experimental.pallas import tpu as pltpu

F32 = jnp.float32
BF16 = jnp.bfloat16

D_MODEL = 1024
P_DIM = 256
W_MIX = 256
HEAD_DIM = 64
N_HEADS = 4
DIFF_DH = 32
ROPE_THETA = 500000.0
ROPE_DIMS = 8
RWKV_GN_EPS = 64e-5
DIFF_LN_EPS = 1e-5
GDN_CONV = 4
CHUNK = 64
CHUNK_UNROLL = 4
N_EXPERTS = 8
NORM_EPS = 1e-6
L2_EPS = 1e-6
LOG2E = math.log2(math.e)
LANES = 128
BF16_ROWS = 16
assert CHUNK == HEAD_DIM

U_GATE = 0
U_RWKV = 4096
U_GDN = 5120
U_DIFF = 6144
U_FOX = 6656
U_COLS = 7168
SM_FOX, SM_BETA, SM_DEC = 0, 4, 8
FOX_CK, FOX_CQ = 0, 16


def _cparams(semantics, vmem_mb=48):
    return pltpu.CompilerParams(dimension_semantics=semantics, vmem_limit_bytes=vmem_mb * 1024 * 1024)


def _mm(a, b):
    return jnp.dot(a.astype(BF16), b.astype(BF16), preferred_element_type=F32)


def _mm_nt(a, b):
    return lax.dot_general(a.astype(BF16), b.astype(BF16), (((1,), (1,)), ((), ())), preferred_element_type=F32)


def _mm_tn(a, b):
    return lax.dot_general(a.astype(BF16), b.astype(BF16), (((0,), (0,)), ((), ())), preferred_element_type=F32)


def _mm_split(a, b):
    ah = a.astype(BF16)
    al = (a - ah.astype(F32)).astype(BF16)
    bh = b.astype(BF16)
    bl = (b - bh.astype(F32)).astype(BF16)
    dot = lambda x, y: jnp.dot(x, y, preferred_element_type=F32)
    return dot(ah, bh) + (dot(ah, bl) + dot(al, bh))


def _mm_mask(mask, x):
    hi = x.astype(BF16)
    r1 = x - hi.astype(F32)
    mid = r1.astype(BF16)
    lo = (r1 - mid.astype(F32)).astype(BF16)
    dot = lambda t: jnp.dot(mask, t, preferred_element_type=F32)
    return dot(hi) + dot(mid) + dot(lo)


def _head_sums(x, same):
    hi = x.astype(BF16)
    lo = (x - hi.astype(F32)).astype(BF16)
    return jnp.dot(hi, same, preferred_element_type=F32) + jnp.dot(lo, same, preferred_element_type=F32)


def _rms(x, g, eps):
    return x * lax.rsqrt(jnp.mean(x * x, axis=-1, keepdims=True) + eps) * g


def _sigmoid(x):
    return 0.5 * jnp.tanh(0.5 * x) + 0.5


def _silu(x):
    return x * _sigmoid(x)


def _softplus(x):
    return jnp.maximum(x, 0.0) + jnp.log(1.0 + jnp.exp(-jnp.abs(x)))


def _tri_masks(c):
    ii = lax.broadcasted_iota(jnp.int32, (c, c), 0)
    jj = lax.broadcasted_iota(jnp.int32, (c, c), 1)
    return ii > jj, ii >= jj, ii == jj


def _block_masks():
    ii = lax.broadcasted_iota(jnp.int32, (W_MIX, W_MIX), 0)
    jj = lax.broadcasted_iota(jnp.int32, (W_MIX, W_MIX), 1)
    return (ii // HEAD_DIM) == (jj // HEAD_DIM), ii == jj


def _wide_masks():
    ii = lax.broadcasted_iota(jnp.int32, (CHUNK, W_MIX), 0)
    jj = lax.broadcasted_iota(jnp.int32, (CHUNK, W_MIX), 1) % HEAD_DIM
    return ii > jj, ii >= jj, ii == jj


def _stack_heads(x, same):
    return jnp.where(same, jnp.concatenate([x, x, x, x], axis=0), jnp.zeros((), x.dtype))


def _unit_lower_inverses(ns, eye_w, same):
    rs = [eye_w + n for n in ns]
    ps = list(ns)
    for _ in range(int(math.log2(CHUNK)) - 1):
        ps = [_mm(p, _stack_heads(p.astype(BF16), same)) for p in ps]
        rs = [r + _mm(r, _stack_heads(p.astype(BF16), same)) for r, p in zip(rs, ps)]
    return rs


def _unstack_heads(x):
    return x[0:CHUNK] + x[CHUNK:2 * CHUNK] + x[2 * CHUNK:3 * CHUNK] + x[3 * CHUNK:4 * CHUNK]


def _inproj_kernel(x_ref, g_ref, w_ref, ws_ref, wvt_ref, u_ref, scol_ref, vt_ref, h_scr):
    @pl.when(pl.program_id(1) == 0)
    def _():
        hb = _rms(x_ref[...], g_ref[...], NORM_EPS).astype(BF16)
        h_scr[...] = hb
        scol_ref[...] = jnp.dot(hb, ws_ref[...], preferred_element_type=F32)
        vt_ref[...] = lax.dot_general(wvt_ref[...], hb, (((1,), (1,)), ((), ())),
                                      preferred_element_type=F32).astype(BF16)

    u_ref[...] = jnp.dot(h_scr[...], w_ref[...], preferred_element_type=F32).astype(BF16)


def _inproj(x, g, w, ws, wvt, tm, tn):
    n = x.shape[0]
    return pl.pallas_call(
        _inproj_kernel,
        grid=(n // tm, U_COLS // tn),
        in_specs=[
            pl.BlockSpec((tm, D_MODEL), lambda i, j: (i, 0)),
            pl.BlockSpec((1, D_MODEL), lambda i, j: (0, 0)),
            pl.BlockSpec((D_MODEL, tn), lambda i, j: (0, j)),
            pl.BlockSpec((D_MODEL, LANES), lambda i, j: (0, 0)),
            pl.BlockSpec((2 * W_MIX, D_MODEL), lambda i, j: (0, 0)),
        ],
        out_specs=[
            pl.BlockSpec((tm, tn), lambda i, j: (i, j)),
            pl.BlockSpec((tm, LANES), lambda i, j: (i, 0)),
            pl.BlockSpec((2 * W_MIX, tm), lambda i, j: (0, i)),
        ],
        out_shape=[
            jax.ShapeDtypeStruct((n, U_COLS), BF16),
            jax.ShapeDtypeStruct((n, LANES), F32),
            jax.ShapeDtypeStruct((2 * W_MIX, n), BF16),
        ],
        scratch_shapes=[pltpu.VMEM((tm, D_MODEL), BF16)],
        compiler_params=_cparams(("parallel", "arbitrary"), vmem_mb=56),
        name="inproj",
    )(x, g, w, ws, wvt)


def _lane_placement(base):
    r = lax.broadcasted_iota(jnp.int32, (3 * LANES, LANES), 0)
    m = lax.broadcasted_iota(jnp.int32, (3 * LANES, LANES), 1)
    head, part = r % LANES, r // LANES
    return jnp.logical_and(head < N_HEADS, m == base + 3 * head + part).astype(BF16)


def _split3(x):
    hi = x.astype(BF16)
    r1 = x - hi.astype(F32)
    mid = r1.astype(BF16)
    lo = (r1 - mid.astype(F32)).astype(BF16)
    return jnp.concatenate([hi, mid, lo], axis=1)


def _head_expansion():
    r = lax.broadcasted_iota(jnp.int32, (3 * LANES, 2 * W_MIX), 0) % LANES
    m = lax.broadcasted_iota(jnp.int32, (3 * LANES, 2 * W_MIX), 1)
    src = jnp.where(m < W_MIX, SM_BETA + m // HEAD_DIM, SM_DEC + (m - W_MIX) // HEAD_DIM)
    return (r == src).astype(BF16)


def _small_prep_kernel(scol_ref, bias_ref, nega_ref, ogb_ref, okb_ref, oqb_ref, carry_scr):
    @pl.when(pl.program_id(1) == 0)
    def _():
        carry_scr[...] = jnp.zeros_like(carry_scr)

    tm = scol_ref.shape[0]
    lower = _tri_masks(tm)[1].astype(BF16)
    z = scol_ref[...] + bias_ref[...]
    lane = lax.broadcasted_iota(jnp.int32, z.shape, 1)
    is_f = lane < SM_BETA
    is_b = jnp.logical_and(lane >= SM_BETA, lane < SM_DEC)
    is_d = jnp.logical_and(lane >= SM_DEC, lane < SM_DEC + N_HEADS)
    logf = jnp.where(is_f, jnp.minimum(z, 0.0) - jnp.log(1.0 + jnp.exp(-jnp.abs(z))), 0.0)
    gdn = jnp.where(is_b, _sigmoid(z), jnp.where(is_d, nega_ref[...] * _softplus(z), 0.0))
    ogb_ref[...] = jnp.dot(_split3(gdn), _head_expansion(), preferred_element_type=F32)
    cum = _mm_mask(lower, logf) + carry_scr[...]
    carry_scr[...] = cum[tm - 1:tm, :]
    parts = _split3(cum * LOG2E)
    ones_lanes = jnp.logical_and(lane >= FOX_CQ, lane < FOX_CQ + 3 * N_HEADS)
    okb_ref[...] = (jnp.dot(parts, _lane_placement(FOX_CK), preferred_element_type=F32)
                    + jnp.where(ones_lanes, 1.0, 0.0)).astype(BF16)
    oqb_ref[...] = jnp.dot(parts, _lane_placement(FOX_CQ), preferred_element_type=F32).astype(BF16)


def _small_prep(scol, bias, neg_a, batch, seq, tm):
    n = batch * seq
    nt = seq // tm
    tok = lambda: pl.BlockSpec((tm, LANES), lambda b, j: (b * nt + j, 0))
    const = lambda: pl.BlockSpec((1, LANES), lambda b, j: (0, 0))
    return pl.pallas_call(
        _small_prep_kernel,
        grid=(batch, nt),
        in_specs=[tok(), const(), const()],
        out_specs=[pl.BlockSpec((tm, 2 * W_MIX), lambda b, j: (b * nt + j, 0)), tok(), tok()],
        out_shape=[jax.ShapeDtypeStruct((n, 2 * W_MIX), F32), jax.ShapeDtypeStruct((n, LANES), BF16),
                   jax.ShapeDtypeStruct((n, LANES), BF16)],
        scratch_shapes=[pltpu.VMEM((1, LANES), F32)],
        compiler_params=_cparams(("parallel", "arbitrary")),
        name="small_prep",
    )(scol, bias, neg_a)


def _rwkv_prep_kernel(u_ref, up_ref, mu_ref, w0_ref, w2_ref, a0_ref, a2_ref, g2_ref, kk_ref, ka_ref, rk_ref,
                      scan_ref, pc_ref, post_ref, *, tiles_per_seq):
    tm = u_ref.shape[0]
    u = u_ref[...].astype(F32)
    prev = up_ref[...].astype(F32)[BF16_ROWS - 1:BF16_ROWS, :]
    prev = jnp.where(pl.program_id(0) % tiles_per_seq == 0, 0.0, prev)
    rows = lax.broadcasted_iota(jnp.int32, (tm, 1), 0)
    u_prev = jnp.where(rows == 0, prev, pltpu.roll(u, 1, 0))
    xm = u + (u_prev - u) * mu_ref[...]
    r = xm[:, 0:W_MIX]
    k = xm[:, W_MIX:2 * W_MIX]
    v = xm[:, 2 * W_MIX:3 * W_MIX]
    x_lora = xm[:, 3 * W_MIX:3 * W_MIX + LANES]
    xg = xm[:, 3 * W_MIX + LANES:]
    logw = -_softplus(-(w0_ref[...] + _mm(jnp.tanh(x_lora), w2_ref[...]))) - 0.5
    log_decay = -jnp.exp(logw)
    a = _sigmoid(a0_ref[...] + _mm(x_lora, a2_ref[...]))
    g = _mm(_sigmoid(xg), g2_ref[...])
    same = _block_masks()[0].astype(BF16)
    kk_raw = k * kk_ref[...]
    kk = kk_raw * lax.rsqrt(_head_sums(kk_raw * kk_raw, same) + L2_EPS)
    k2 = k * (1.0 + (a - 1.0) * ka_ref[...])
    bonus = _head_sums(r * k2 * rk_ref[...], same) * v
    ti = lax.broadcasted_iota(jnp.int32, (tm, tm), 0)
    tj = lax.broadcasted_iota(jnp.int32, (tm, tm), 1)
    in_chunk = jnp.logical_and(ti // CHUNK == tj // CHUNK, ti >= tj).astype(BF16)
    ci = lax.broadcasted_iota(jnp.int32, (tm // CHUNK, tm), 0)
    cj = lax.broadcasted_iota(jnp.int32, (tm // CHUNK, tm), 1)
    cum = _mm_mask(in_chunk, log_decay)
    cum_end = _mm_mask((ci == cj // CHUNK).astype(BF16), log_decay)
    inv = jnp.exp(-cum)
    scan_ref[0] = (-kk * jnp.exp(cum - log_decay)).astype(BF16)
    scan_ref[1] = (kk * a * inv).astype(BF16)
    scan_ref[2] = (k2 * inv).astype(BF16)
    scan_ref[3] = (r * jnp.exp(cum)).astype(BF16)
    scan_ref[4] = v.astype(BF16)
    pc_ref[...] = jnp.exp(cum_end)
    post_ref[0] = g.astype(BF16)
    post_ref[1] = bonus.astype(BF16)


def _rwkv_prep(u, mu, w0, w2p, a0, a2p, g2, k_k, k_a, r_k, seq, tm):
    n = u.shape[0]
    ublk = U_RWKV // D_MODEL
    row = lambda c: pl.BlockSpec((1, c), lambda i: (0, 0))
    mat = lambda r: pl.BlockSpec((r, W_MIX), lambda i: (0, 0))
    return pl.pallas_call(
        functools.partial(_rwkv_prep_kernel, tiles_per_seq=seq // tm),
        grid=(n // tm,),
        in_specs=[
            pl.BlockSpec((tm, D_MODEL), lambda i: (i, ublk)),
            pl.BlockSpec((BF16_ROWS, D_MODEL), lambda i: (jnp.maximum(i * (tm // BF16_ROWS) - 1, 0), ublk)),
            row(D_MODEL), row(W_MIX), mat(LANES), row(W_MIX), mat(LANES), mat(LANES), row(W_MIX), row(W_MIX), row(W_MIX),
        ],
        out_specs=[
            pl.BlockSpec((5, tm, W_MIX), lambda i: (0, i, 0)),
            pl.BlockSpec((tm // CHUNK, W_MIX), lambda i: (i, 0)),
            pl.BlockSpec((2, tm, W_MIX), lambda i: (0, i, 0)),
        ],
        out_shape=[jax.ShapeDtypeStruct((5, n, W_MIX), BF16), jax.ShapeDtypeStruct((n // CHUNK, W_MIX), F32),
                   jax.ShapeDtypeStruct((2, n, W_MIX), BF16)],
        compiler_params=_cparams(("parallel",)),
        name="rwkv_prep",
    )(u, u, mu, w0, w2p, a0, a2p, g2, k_k, k_a, r_k)


def _rwkv_chunk_kernel(x_ref, pc_ref, o_ref, s_scr, *, nchunk, nbatch):
    @pl.when(pl.program_id(0) == 0)
    def _():
        s_scr[...] = jnp.zeros_like(s_scr)

    same, diag = _block_masks()
    eye = diag.astype(F32)
    strict_w, incl_w, diag_w = _wide_masks()
    eye_w = diag_w.astype(F32)
    stack = lambda x: _stack_heads(x.astype(BF16), same)

    def chunks(i, carry):
        items = [(b, i * CHUNK_UNROLL + j) for j in range(CHUNK_UNROLL) for b in range(nbatch)]
        each = lambda f, *lists: [f(*args) for args in zip(*lists)]
        sls = [pl.ds(pl.multiple_of(ci * CHUNK, CHUNK), CHUNK) for _, ci in items]
        a, bb, k, r, v = ([x_ref[i, b, sl, :] for (b, _), sl in zip(items, sls)] for i in range(5))
        pc = [pc_ref[b, pl.ds(ci, 1), :] for b, ci in items]
        a_s, b_s, k_s, v_s = (each(stack, x) for x in (a, bb, k, v))
        m_ab = each(lambda x, y: jnp.where(strict_w, _mm_nt(x, y), 0.0), a, b_s)
        m_ak = each(lambda x, y: jnp.where(strict_w, _mm_nt(x, y), 0.0), a, k_s)
        n_rb = each(lambda x, y: jnp.where(incl_w, _mm_nt(x, y), 0.0), r, b_s)
        n_rk = each(lambda x, y: jnp.where(incl_w, _mm_nt(x, y), 0.0), r, k_s)
        t_inv = _unit_lower_inverses(m_ab, eye_w, same)
        a2 = each(_mm, t_inv, a_s)
        u0 = each(lambda t, m, x: _mm(t, stack(_mm(m, x))), t_inv, m_ak, v_s)
        r2 = each(lambda x, n, y: x.astype(F32) + _mm(n, stack(y)), r, n_rb, a2)
        o0 = each(lambda n1, u, n2, x: _mm(n1, stack(u)) + _mm(n2, x), n_rb, u0, n_rk, v_s)
        b_end = each(lambda x, p: x.astype(F32) * p, bb, pc)
        k_end = each(lambda x, p: x.astype(F32) * p, k, pc)
        g_mat = each(lambda p, x, y: eye * p + jnp.where(same, _mm_tn(x, y), 0.0), pc, a2, b_end)
        s0 = each(lambda u, x, y, z: jnp.where(same, _mm_tn(u, x) + _mm_tn(y, z), 0.0), u0, b_end, v, k_end)
        for n, ((b, _), sl) in enumerate(zip(items, sls)):
            s = s_scr[b]
            o = _stack_heads(_mm_nt(r2[n], s) + o0[n], same)
            s_scr[b] = _mm(s, g_mat[n]) + s0[n]
            mean = jnp.sum(o, axis=-1, keepdims=True) * (1.0 / HEAD_DIM)
            cen = jnp.where(same, o - mean, 0.0)
            var = jnp.sum(cen * cen, axis=-1, keepdims=True) * (1.0 / HEAD_DIM)
            o_ref[b, sl, :] = _unstack_heads(cen * lax.rsqrt(var + RWKV_GN_EPS)).astype(BF16)
        return carry

    lax.fori_loop(0, nchunk // CHUNK_UNROLL, chunks, 0)


def _rwkv_chunk(xs, pc, batch, seq, cblk):
    xs = xs.reshape(5, batch, seq, W_MIX)
    pc = pc.reshape(batch, seq // CHUNK, W_MIX)
    out = pl.pallas_call(
        functools.partial(_rwkv_chunk_kernel, nchunk=cblk // CHUNK, nbatch=batch),
        grid=(seq // cblk,),
        in_specs=[
            pl.BlockSpec((5, batch, cblk, W_MIX), lambda j: (0, 0, j, 0)),
            pl.BlockSpec((batch, cblk // CHUNK, W_MIX), lambda j: (0, j, 0)),
        ],
        out_specs=pl.BlockSpec((batch, cblk, W_MIX), lambda j: (0, j, 0)),
        out_shape=jax.ShapeDtypeStruct((batch, seq, W_MIX), BF16),
        scratch_shapes=[pltpu.VMEM((batch, W_MIX, W_MIX), F32)],
        compiler_params=_cparams(("arbitrary",)),
        name="rwkv_chunk",
    )(xs, pc)
    return out.reshape(batch * seq, W_MIX)


def _gdn_prep_kernel(u_ref, up_ref, cw_ref, o_ref, ext_scr, *, tiles_per_seq):
    tm = u_ref.shape[0]
    c3 = 3 * W_MIX
    prev = up_ref[...].astype(F32)[:, :c3]
    ext_scr[0:BF16_ROWS, :] = jnp.where(pl.program_id(0) % tiles_per_seq == 0, 0.0, prev)
    ext_scr[BF16_ROWS:, :] = u_ref[...].astype(F32)[:, :c3]
    y = jnp.zeros((tm, c3), F32)
    for j in range(GDN_CONV):
        y = y + ext_scr[pl.ds(BF16_ROWS - (GDN_CONV - 1) + j, tm), :] * cw_ref[j:j + 1, :]
    y = _silu(y)
    same = _block_masks()[0].astype(BF16)
    q = y[:, 0:W_MIX]
    k = y[:, W_MIX:2 * W_MIX]
    o_ref[0] = (q * lax.rsqrt(_head_sums(q * q, same) + L2_EPS) * (HEAD_DIM ** -0.5)).astype(BF16)
    o_ref[1] = (k * lax.rsqrt(_head_sums(k * k, same) + L2_EPS)).astype(BF16)
    o_ref[2] = y[:, 2 * W_MIX:].astype(BF16)


def _gdn_prep(u, conv_w, seq, tm):
    n = u.shape[0]
    ublk = U_GDN // D_MODEL
    return pl.pallas_call(
        functools.partial(_gdn_prep_kernel, tiles_per_seq=seq // tm),
        grid=(n // tm,),
        in_specs=[
            pl.BlockSpec((tm, D_MODEL), lambda i: (i, ublk)),
            pl.BlockSpec((BF16_ROWS, D_MODEL), lambda i: (jnp.maximum(i * (tm // BF16_ROWS) - 1, 0), ublk)),
            pl.BlockSpec((GDN_CONV, 3 * W_MIX), lambda i: (0, 0)),
        ],
        out_specs=pl.BlockSpec((3, tm, W_MIX), lambda i: (0, i, 0)),
        out_shape=jax.ShapeDtypeStruct((3, n, W_MIX), BF16),
        scratch_shapes=[pltpu.VMEM((tm + BF16_ROWS, 3 * W_MIX), F32)],
        compiler_params=_cparams(("parallel",)),
        name="gdn_prep",
    )(u, u, conv_w)


def _gdn_chunk_kernel(x_ref, gb_ref, nw_ref, o_ref, s_scr, *, nchunk, nbatch):
    @pl.when(pl.program_id(0) == 0)
    def _():
        s_scr[...] = jnp.zeros_like(s_scr)

    same, diag = _block_masks()
    eye = diag.astype(F32)
    strict_w, incl_w, diag_w = _wide_masks()
    eye_w = diag_w.astype(F32)
    lower = _tri_masks(CHUNK)[1].astype(BF16)
    stack = lambda x: _stack_heads(x.astype(BF16), same)

    def chunks(i, carry):
        items = [(b, i * CHUNK_UNROLL + j) for j in range(CHUNK_UNROLL) for b in range(nbatch)]
        each = lambda f, *lists: [f(*args) for args in zip(*lists)]
        sls = [pl.ds(pl.multiple_of(ci * CHUNK, CHUNK), CHUNK) for _, ci in items]
        q, k, v = ([x_ref[i, b, sl, :] for (b, _), sl in zip(items, sls)] for i in range(3))
        beta = [gb_ref[b, sl, 0:W_MIX] for (b, _), sl in zip(items, sls)]
        g = [gb_ref[b, sl, W_MIX:2 * W_MIX] for (b, _), sl in zip(items, sls)]
        k_s = each(stack, k)
        gam = [_mm_mask(lower, x) for x in g]
        gam_end = [x[CHUNK - 1:CHUNK, :] for x in gam]
        gdiff = [_mm_mask(lower, jnp.where(strict_w, x, 0.0)) for x in g]
        decay = [jnp.exp(jnp.where(incl_w, x, -jnp.inf)) for x in gdiff]
        a_mat = each(lambda bt, d, x, y: jnp.where(strict_w, bt * d * _mm_nt(x, y), 0.0), beta, decay, k, k_s)
        t_inv = _unit_lower_inverses([-a for a in a_mat], eye_w, same)
        e_gam = [jnp.exp(x) for x in gam]
        u0 = each(lambda t, bt, x: _mm(t, stack(bt * x.astype(F32))), t_inv, beta, v)
        wm = each(lambda t, bt, e, x: _mm(t, stack((bt * e) * x.astype(F32))), t_inv, beta, e_gam, k)
        qk = each(lambda x, y, d: _mm_nt(x, y) * d, q, k_s, decay)
        q2 = each(lambda e, x, a, w: e * x.astype(F32) - _mm(a, stack(w)), e_gam, q, qk, wm)
        o0 = each(lambda a, u: _mm(a, stack(u)), qk, u0)
        k_end = each(lambda x, ge, ga: x.astype(F32) * jnp.exp(ge - ga), k, gam_end, gam)
        g_mat = each(lambda ge, x, w: eye * jnp.exp(ge) - jnp.where(same, _mm_tn(x, w), 0.0), gam_end, k_end, wm)
        s0 = each(lambda x, u: jnp.where(same, _mm_tn(x, u), 0.0), k_end, u0)
        for n, ((b, _), sl) in enumerate(zip(items, sls)):
            s = s_scr[b]
            o = _stack_heads(_mm(q2[n], s) + o0[n], same)
            s_scr[b] = _mm(g_mat[n], s) + s0[n]
            ms = jnp.sum(o * o, axis=-1, keepdims=True) * (1.0 / HEAD_DIM)
            o_ref[b, sl, :] = (_unstack_heads(o * lax.rsqrt(ms + NORM_EPS)) * nw_ref[...]).astype(BF16)
        return carry

    lax.fori_loop(0, nchunk // CHUNK_UNROLL, chunks, 0)


def _gdn_chunk(xs, gb, norm_w, batch, seq, cblk):
    xs = xs.reshape(3, batch, seq, W_MIX)
    gb = gb.reshape(batch, seq, 2 * W_MIX)
    out = pl.pallas_call(
        functools.partial(_gdn_chunk_kernel, nchunk=cblk // CHUNK, nbatch=batch),
        grid=(seq // cblk,),
        in_specs=[
            pl.BlockSpec((3, batch, cblk, W_MIX), lambda j: (0, 0, j, 0)),
            pl.BlockSpec((batch, cblk, 2 * W_MIX), lambda j: (0, j, 0)),
            pl.BlockSpec((1, W_MIX), lambda j: (0, 0)),
        ],
        out_specs=pl.BlockSpec((batch, cblk, W_MIX), lambda j: (0, j, 0)),
        out_shape=jax.ShapeDtypeStruct((batch, seq, W_MIX), BF16),
        scratch_shapes=[pltpu.VMEM((batch, W_MIX, W_MIX), F32)],
        compiler_params=_cparams(("arbitrary",)),
        name="gdn_chunk",
    )(xs, gb, norm_w)
    return out.reshape(batch * seq, W_MIX)


def _causal_pairs(nq):
    pairs = [(i, j) for i in range(nq) for j in range(i + 1)]
    return jnp.asarray(np.array([p[0] for p in pairs], np.int32)), jnp.asarray(np.array([p[1] for p in pairs], np.int32))


def _softmax_updates(scores, vt_ones, m_scr, acc_scr):
    probs = []
    for i, s in enumerate(scores):
        m_old = m_scr[i]
        m_new = jnp.maximum(m_old, jnp.max(s, axis=0, keepdims=True))
        m_scr[i] = m_new
        probs.append((jnp.exp2(m_old - m_new), jnp.exp2((s - m_new).astype(BF16))))
    for i, (alpha, p) in enumerate(probs):
        acc_scr[i] = alpha * acc_scr[i] + jnp.dot(vt_ones[i], p, preferred_element_type=F32)


def _pair_lanes(h):
    p = h // 2
    return slice(p * LANES, (p + 1) * LANES), h % 2 == 0


ACC_ROWS = HEAD_DIM + BF16_ROWS


def _values_and_ones(vt, tk):
    ones = jnp.ones((BF16_ROWS, tk), vt.dtype)
    return [jnp.concatenate([vt[h * HEAD_DIM:(h + 1) * HEAD_DIM, :], ones], axis=0) for h in range(N_HEADS)]


def _normalized(acc):
    return acc[0:HEAD_DIM] / acc[HEAD_DIM:HEAD_DIM + 1]


def _key_after_query(tq):
    return lax.broadcasted_iota(jnp.int32, (tq, tq), 0) > lax.broadcasted_iota(jnp.int32, (tq, tq), 1)


def _fox_kernel(qi_ref, kj_ref, q_ref, k_ref, vt_ref, qb_ref, kb_ref, o_ref, qm_scr, m_scr, acc_scr):
    t = pl.program_id(1)
    qi = qi_ref[t]
    kj = kj_ref[t]
    tq = q_ref.shape[0]

    @pl.when(kj == 0)
    def _():
        m_scr[...] = jnp.full_like(m_scr, -jnp.inf)
        acc_scr[...] = jnp.zeros_like(acc_scr)
        q = q_ref[...]
        qb = qb_ref[...]
        lane = lax.broadcasted_iota(jnp.int32, (tq, LANES), 1)
        zero = jnp.zeros((), BF16)
        for h in range(N_HEADS):
            slab, low = _pair_lanes(h)
            mine = lane < HEAD_DIM if low else lane >= HEAD_DIM
            ck_lanes = jnp.logical_and(lane >= FOX_CK + 3 * h, lane < FOX_CK + 3 * h + 3)
            cq_lanes = jnp.logical_and(lane >= FOX_CQ + 3 * h, lane < FOX_CQ + 3 * h + 3)
            bias = jnp.where(ck_lanes, -jnp.ones((), BF16), jnp.where(cq_lanes, qb, zero))
            qm_scr[h] = jnp.concatenate([jnp.where(mine, q[:, slab], zero), bias], axis=1)

    def step(diagonal):
        k = k_ref[...]
        kb = kb_ref[...]
        vts = _values_and_ones(vt_ref[...], tq)
        if diagonal:
            masked = _key_after_query(tq)
        keys = [jnp.concatenate([k[:, p * LANES:(p + 1) * LANES], kb], axis=1) for p in range(N_HEADS // 2)]
        scores = []
        for h in range(N_HEADS):
            s = lax.dot_general(keys[h // 2], qm_scr[h], (((1,), (1,)), ((), ())), preferred_element_type=F32)
            if diagonal:
                s = jnp.where(masked, -jnp.inf, s)
            scores.append(s)
        _softmax_updates(scores, vts, m_scr, acc_scr)

    @pl.when(kj < qi)
    def _():
        step(False)

    @pl.when(kj == qi)
    def _():
        step(True)
        for p in range(N_HEADS // 2):
            pair = jnp.concatenate([_normalized(acc_scr[2 * p]), _normalized(acc_scr[2 * p + 1])], axis=0)
            o_ref[:, p * LANES:(p + 1) * LANES] = pair.T.astype(BF16)


def _attn_specs(nq, tq, ucol, vt_rows):
    cb = ucol // W_MIX
    q_spec = pl.BlockSpec((tq, W_MIX), lambda b, t, qi, kj: (b * nq + qi[t], cb))
    k_spec = pl.BlockSpec((tq, W_MIX), lambda b, t, qi, kj: (b * nq + kj[t], cb + 1))
    vt_spec = pl.BlockSpec((W_MIX, tq), lambda b, t, qi, kj: (vt_rows // W_MIX, b * nq + kj[t]))
    return q_spec, k_spec, vt_spec


def _fox_attention(u, vt, qb, kb, batch, seq, tq):
    nq = seq // tq
    qi, kj = _causal_pairs(nq)
    q_spec, k_spec, vt_spec = _attn_specs(nq, tq, U_FOX, W_MIX)
    stat = lambda: pltpu.VMEM((N_HEADS, 1, tq), F32)
    return pl.pallas_call(
        _fox_kernel,
        grid_spec=pltpu.PrefetchScalarGridSpec(
            num_scalar_prefetch=2,
            grid=(batch, qi.shape[0]),
            in_specs=[
                q_spec, k_spec, vt_spec,
                pl.BlockSpec((tq, LANES), lambda b, t, qi, kj: (b * nq + qi[t], 0)),
                pl.BlockSpec((tq, LANES), lambda b, t, qi, kj: (b * nq + kj[t], 0)),
            ],
            out_specs=pl.BlockSpec((tq, W_MIX), lambda b, t, qi, kj: (b * nq + qi[t], 0)),
            scratch_shapes=[pltpu.VMEM((N_HEADS, tq, 2 * LANES), BF16), stat(),
                            pltpu.VMEM((N_HEADS, ACC_ROWS, tq), F32)],
        ),
        out_shape=jax.ShapeDtypeStruct((batch * seq, W_MIX), BF16),
        compiler_params=_cparams(("parallel", "arbitrary")),
        name="fox_attention",
    )(qi, kj, u, u, vt, qb, kb)


def _diff_kernel(qi_ref, kj_ref, q_ref, k_ref, vt_ref, cq_ref, sq_ref, ck_ref, sk_ref, rot_ref, lam_ref, ln_ref, o_ref,
                 qm_scr, m_scr, acc_scr, *, lam_init):
    t = pl.program_id(1)
    qi = qi_ref[t]
    kj = kj_ref[t]
    tq = q_ref.shape[0]
    lane = lax.broadcasted_iota(jnp.int32, (tq, LANES), 1)

    def rope(x, cos, sin):
        return x.astype(F32) * cos + jnp.dot(x, rot_ref[...], preferred_element_type=F32) * sin

    @pl.when(kj == 0)
    def _():
        m_scr[...] = jnp.full_like(m_scr, -jnp.inf)
        acc_scr[...] = jnp.zeros_like(acc_scr)
        q = rope(q_ref[...], cq_ref[...], sq_ref[...])
        for h in range(N_HEADS):
            slab, low = _pair_lanes(h)
            base = 0 if low else HEAD_DIM
            for c in range(2):
                lo = base + c * DIFF_DH
                sel = jnp.logical_and(lane >= lo, lane < lo + DIFF_DH)
                qm_scr[2 * h + c] = jnp.where(sel, q[:, slab], 0.0).astype(BF16)

    def step(diagonal):
        k = rope(k_ref[...], ck_ref[...], sk_ref[...]).astype(BF16)
        vts = _values_and_ones(vt_ref[...], tq)
        if diagonal:
            masked = _key_after_query(tq)
        scores = []
        for h in range(N_HEADS):
            slab, _ = _pair_lanes(h)
            for c in range(2):
                i = 2 * h + c
                s = lax.dot_general(k[:, slab], qm_scr[i], (((1,), (1,)), ((), ())), preferred_element_type=F32)
                if diagonal:
                    s = jnp.where(masked, -jnp.inf, s)
                scores.append(s)
        _softmax_updates(scores, [vts[i // 2] for i in range(2 * N_HEADS)], m_scr, acc_scr)

    @pl.when(kj < qi)
    def _():
        step(False)

    @pl.when(kj == qi)
    def _():
        step(True)
        lp = lam_ref[...]
        lam = (jnp.exp(jnp.sum(lp[0:1] * lp[1:2], axis=-1, keepdims=True))
               - jnp.exp(jnp.sum(lp[2:3] * lp[3:4], axis=-1, keepdims=True)) + lam_init)
        head = lambda h: _normalized(acc_scr[2 * h]) - lam * _normalized(acc_scr[2 * h + 1])
        is_lo = lane < HEAD_DIM
        for p in range(N_HEADS // 2):
            o = jnp.concatenate([head(2 * p), head(2 * p + 1)], axis=0).T
            sq = o * o
            ms_lo = jnp.sum(jnp.where(is_lo, sq, 0.0), axis=-1, keepdims=True)
            ms_hi = jnp.sum(jnp.where(is_lo, 0.0, sq), axis=-1, keepdims=True)
            ms = jnp.where(is_lo, ms_lo, ms_hi) * (1.0 / HEAD_DIM)
            o_ref[:, p * LANES:(p + 1) * LANES] = (o * lax.rsqrt(ms + DIFF_LN_EPS) * ln_ref[...]
                                                   * (1.0 - lam_init)).astype(BF16)


def _diff_attention(u, vt, cos, sin, rot, lam_p, subln, batch, seq, tq, lam_init):
    nq = seq // tq
    qi, kj = _causal_pairs(nq)
    q_spec, k_spec, vt_spec = _attn_specs(nq, tq, U_DIFF, 0)
    tab_q = pl.BlockSpec((tq, W_MIX), lambda b, t, qi, kj: (qi[t], 0))
    tab_k = pl.BlockSpec((tq, W_MIX), lambda b, t, qi, kj: (kj[t], 0))
    const = lambda r, c: pl.BlockSpec((r, c), lambda b, t, qi, kj: (0, 0))
    stat = lambda: pltpu.VMEM((2 * N_HEADS, 1, tq), F32)
    return pl.pallas_call(
        functools.partial(_diff_kernel, lam_init=lam_init),
        grid_spec=pltpu.PrefetchScalarGridSpec(
            num_scalar_prefetch=2,
            grid=(batch, qi.shape[0]),
            in_specs=[q_spec, k_spec, vt_spec, tab_q, tab_q, tab_k, tab_k,
                      const(W_MIX, W_MIX), const(4, DIFF_DH), const(1, LANES)],
            out_specs=pl.BlockSpec((tq, W_MIX), lambda b, t, qi, kj: (b * nq + qi[t], 0)),
            scratch_shapes=[pltpu.VMEM((2 * N_HEADS, tq, LANES), BF16), stat(),
                            pltpu.VMEM((2 * N_HEADS, ACC_ROWS, tq), F32)],
        ),
        out_shape=jax.ShapeDtypeStruct((batch * seq, W_MIX), BF16),
        compiler_params=_cparams(("parallel", "arbitrary")),
        name="diff_attention",
    )(qi, kj, u, u, vt, cos, sin, cos, sin, rot, lam_p, subln)


def _merge_kernel(x_ref, gate_ref, oa_ref, post_ref, yb_ref, yc_ref, od_ref, gd_ref, lnw_ref, lnb_ref,
                  wbo_ref, wout_ref, o_ref):
    y_a = (oa_ref[...].astype(F32) * lnw_ref[...] + lnb_ref[...] + post_ref[1].astype(F32)) * post_ref[0].astype(F32)
    y_d = od_ref[...].astype(F32) * _silu(gd_ref[...].astype(F32))
    acc = jnp.zeros(x_ref.shape, F32)
    for b, y in enumerate((y_a, yb_ref[...], yc_ref[...], y_d)):
        gate = _sigmoid(gate_ref[:, b * D_MODEL:(b + 1) * D_MODEL].astype(F32))
        acc = acc + gate * _mm(y, wbo_ref[b])
    o_ref[...] = x_ref[...] + _mm(acc, wout_ref[...])


def _merge(x, u, o_a, post, y_b, y_c, o_d, ln_w, ln_b, w_bo, w_out, tm):
    n = x.shape[0]
    tok = lambda c: pl.BlockSpec((tm, c), lambda i: (i, 0))
    return pl.pallas_call(
        _merge_kernel,
        grid=(n // tm,),
        in_specs=[
            tok(D_MODEL),
            pl.BlockSpec((tm, 4 * D_MODEL), lambda i: (i, U_GATE // (4 * D_MODEL))),
            tok(W_MIX),
            pl.BlockSpec((2, tm, W_MIX), lambda i: (0, i, 0)),
            tok(W_MIX), tok(W_MIX), tok(W_MIX),
            pl.BlockSpec((tm, W_MIX), lambda i: (i, (U_GDN + 3 * W_MIX) // W_MIX)),
            pl.BlockSpec((1, W_MIX), lambda i: (0, 0)),
            pl.BlockSpec((1, W_MIX), lambda i: (0, 0)),
            pl.BlockSpec((4, W_MIX, D_MODEL), lambda i: (0, 0, 0)),
            pl.BlockSpec((D_MODEL, D_MODEL), lambda i: (0, 0)),
        ],
        out_specs=tok(D_MODEL),
        out_shape=jax.ShapeDtypeStruct((n, D_MODEL), F32),
        compiler_params=_cparams(("parallel",)),
        name="merge",
    )(x, u, o_a, post, y_b, y_c, o_d, u, ln_w, ln_b, w_bo, w_out)


def _ffn_kernel(x_ref, g_ref, wg_ref, wu_ref, wd_ref, o_ref):
    x = x_ref[...]
    h = _rms(x, g_ref[...], NORM_EPS).astype(BF16)
    act = _silu(jnp.dot(h, wg_ref[...], preferred_element_type=F32)) * jnp.dot(h, wu_ref[...], preferred_element_type=F32)
    o_ref[...] = x + _mm(act, wd_ref[...])


def _ffn(x, g, wg, wu, wd, tm):
    n = x.shape[0]
    ff = wg.shape[1]
    resident = lambda r, c: pl.BlockSpec((r, c), lambda i: (0, 0), pipeline_mode=pl.Buffered(1))
    return pl.pallas_call(
        _ffn_kernel,
        grid=(n // tm,),
        in_specs=[
            pl.BlockSpec((tm, D_MODEL), lambda i: (i, 0)),
            pl.BlockSpec((1, D_MODEL), lambda i: (0, 0)),
            resident(D_MODEL, ff), resident(D_MODEL, ff), resident(ff, D_MODEL),
        ],
        out_specs=pl.BlockSpec((tm, D_MODEL), lambda i: (i, 0)),
        out_shape=jax.ShapeDtypeStruct((n, D_MODEL), F32),
        compiler_params=_cparams(("parallel",), vmem_mb=56),
        name="ffn",
    )(x, g, wg, wu, wd)


def _router_kernel(x_ref, g_ref, router_ref, h_ref, c_ref, rc_ref, rr_ref, cnt_ref):
    t = x_ref.shape[0]
    h = _rms(x_ref[...], g_ref[...], NORM_EPS)
    h_ref[...] = h.astype(BF16)
    logits = _mm_split(h, router_ref[...])
    lane = lax.broadcasted_iota(jnp.int32, logits.shape, 1).astype(F32)
    lg = jnp.where(lane < N_EXPERTS, logits, -jnp.inf)
    m1 = jnp.max(lg, axis=-1, keepdims=True)
    i1 = jnp.min(jnp.where(lg == m1, lane, float(LANES)), axis=-1, keepdims=True)
    lg2 = jnp.where(lane == i1, -jnp.inf, lg)
    m2 = jnp.max(lg2, axis=-1, keepdims=True)
    i2 = jnp.min(jnp.where(lg2 == m2, lane, float(LANES)), axis=-1, keepdims=True)
    e2 = jnp.exp(m2 - m1)
    c_ref[...] = jnp.where(lane == i1, 1.0 / (1.0 + e2), 0.0) + jnp.where(lane == i2, e2 / (1.0 + e2), 0.0)
    sel = jnp.logical_or(lane == i1, lane == i2)
    sel_f = jnp.where(sel, 1.0, 0.0)
    earlier = _tri_masks(t)[0].astype(BF16)
    rank = jnp.dot(earlier, sel_f.astype(BF16), preferred_element_type=F32)
    rc = jnp.where(sel, rank, -1.0)
    rc_ref[...] = rc
    rr_ref[...] = rc.T[0:BF16_ROWS, :]
    cnt_ref[...] = jnp.broadcast_to(jnp.sum(sel_f, axis=0, keepdims=True), cnt_ref.shape).astype(jnp.int32)


def _router(x, g, router, tm):
    n = x.shape[0]
    return pl.pallas_call(
        _router_kernel,
        grid=(n // tm,),
        in_specs=[
            pl.BlockSpec((tm, D_MODEL), lambda i: (i, 0)),
            pl.BlockSpec((1, D_MODEL), lambda i: (0, 0)),
            pl.BlockSpec((D_MODEL, LANES), lambda i: (0, 0)),
        ],
        out_specs=[
            pl.BlockSpec((tm, D_MODEL), lambda i: (i, 0)),
            pl.BlockSpec((tm, LANES), lambda i: (i, 0)),
            pl.BlockSpec((tm, LANES), lambda i: (i, 0)),
            pl.BlockSpec((BF16_ROWS, tm), lambda i: (0, i)),
            pl.BlockSpec((8, LANES), lambda i: (i, 0)),
        ],
        out_shape=[
            jax.ShapeDtypeStruct((n, D_MODEL), BF16),
            jax.ShapeDtypeStruct((n, LANES), F32),
            jax.ShapeDtypeStruct((n, LANES), F32),
            jax.ShapeDtypeStruct((BF16_ROWS, n), F32),
            jax.ShapeDtypeStruct((n // tm * 8, LANES), jnp.int32),
        ],
        compiler_params=_cparams(("parallel",)),
        name="router",
    )(x, g, router)


def _moe_kernel(cnt_ref, x_ref, h_ref, c_ref, rc_ref, rr_ref, wg_ref, wu_ref, wd_ref, o_ref, xg_scr, yg_scr, *, rows):
    i = pl.program_id(0)
    e = pl.program_id(1)
    f = pl.program_id(2)
    last_f = f == pl.num_programs(2) - 1
    t = h_ref.shape[0]
    nsub = xg_scr.shape[0] // rows
    cnt = cnt_ref[i * N_EXPERTS + e]
    blocks = [(sb, slice(sb * rows, (sb + 1) * rows)) for sb in range(nsub)]

    @pl.when(jnp.logical_and(e == 0, f == 0))
    def _():
        o_ref[...] = x_ref[...]

    @pl.when(f == 0)
    def _():
        rr = rr_ref[pl.ds(e, 1), :]
        for sb, rs in blocks:
            @pl.when(cnt > sb * rows)
            def _():
                slot = (lax.broadcasted_iota(jnp.int32, (rows, t), 0) + sb * rows).astype(F32)
                gather = jnp.where(rr == slot, 1.0, 0.0).astype(BF16)
                xg_scr[rs, :] = jnp.dot(gather, h_ref[...], preferred_element_type=F32).astype(BF16)
                yg_scr[rs, :] = jnp.zeros((rows, D_MODEL), F32)

    for sb, rs in blocks:
        @pl.when(cnt > sb * rows)
        def _():
            xb = xg_scr[rs, :]
            act = (_silu(jnp.dot(xb, wg_ref[0], preferred_element_type=F32))
                   * jnp.dot(xb, wu_ref[0], preferred_element_type=F32))
            yg_scr[rs, :] += _mm(act, wd_ref[0])

    @pl.when(last_f)
    def _():
        lane = lax.broadcasted_iota(jnp.int32, (t, LANES), 1)
        mine = lane == e
        rc = jnp.sum(jnp.where(mine, rc_ref[...], 0.0), axis=-1, keepdims=True)
        ce = jnp.sum(jnp.where(mine, c_ref[...], 0.0), axis=-1, keepdims=True)
        for sb, rs in blocks:
            @pl.when(cnt > sb * rows)
            def _():
                slot = (lax.broadcasted_iota(jnp.int32, (t, rows), 1) + sb * rows).astype(F32)
                scatter = jnp.where(rc == slot, 1.0, 0.0).astype(BF16)
                o_ref[...] += ce * jnp.dot(scatter, yg_scr[rs, :].astype(BF16), preferred_element_type=F32)


def _moe(x, g, router, wg, wu, wd, tm, tf, rows):
    n = x.shape[0]
    ff = wg.shape[2]
    h, c, rc, rr, cnt = _router(x, g, router, tm)
    cnt = cnt.reshape(n // tm, 8, LANES)[:, 0, :N_EXPERTS].reshape(-1)
    nsub = -(-tm // rows)
    tok = lambda cols, **kw: pl.BlockSpec((tm, cols), lambda i, e, f, cnt: (i, 0), **kw)
    once = dict(pipeline_mode=pl.Buffered(1))
    return pl.pallas_call(
        functools.partial(_moe_kernel, rows=rows),
        grid_spec=pltpu.PrefetchScalarGridSpec(
            num_scalar_prefetch=1,
            grid=(n // tm, N_EXPERTS, ff // tf),
            in_specs=[
                tok(D_MODEL, **once), tok(D_MODEL, **once), tok(LANES), tok(LANES),
                pl.BlockSpec((BF16_ROWS, tm), lambda i, e, f, cnt: (0, i)),
                pl.BlockSpec((1, D_MODEL, tf), lambda i, e, f, cnt: (e, 0, f)),
                pl.BlockSpec((1, D_MODEL, tf), lambda i, e, f, cnt: (e, 0, f)),
                pl.BlockSpec((1, tf, D_MODEL), lambda i, e, f, cnt: (e, f, 0)),
            ],
            out_specs=tok(D_MODEL),
            scratch_shapes=[pltpu.VMEM((nsub * rows, D_MODEL), BF16), pltpu.VMEM((nsub * rows, D_MODEL), F32)],
        ),
        out_shape=jax.ShapeDtypeStruct((n, D_MODEL), F32),
        compiler_params=_cparams(("parallel", "arbitrary", "arbitrary"), vmem_mb=56),
        name="moe",
    )(cnt, x, h, c, rc, rr, wg, wu, wd)


def _ple_kernel(x_ref, p_ref, g_ref, wgate_ref, wproj_ref, fin_ref, o_ref, *, final):
    x = x_ref[...]
    h = _rms(x, g_ref[...], NORM_EPS)
    y = x + _sigmoid(_mm(h, wgate_ref[...])) * _mm(p_ref[...], wproj_ref[...])
    o_ref[...] = _rms(y, fin_ref[...], NORM_EPS) if final else y


def _ple(x, p, g, wgate, wproj, fin, tm, final):
    n = x.shape[0]
    return pl.pallas_call(
        functools.partial(_ple_kernel, final=final),
        grid=(n // tm,),
        in_specs=[
            pl.BlockSpec((tm, D_MODEL), lambda i: (i, 0)),
            pl.BlockSpec((tm, P_DIM), lambda i: (i, 0)),
            pl.BlockSpec((1, D_MODEL), lambda i: (0, 0)),
            pl.BlockSpec((D_MODEL, D_MODEL), lambda i: (0, 0)),
            pl.BlockSpec((P_DIM, D_MODEL), lambda i: (0, 0)),
            pl.BlockSpec((1, D_MODEL), lambda i: (0, 0)),
        ],
        out_specs=pl.BlockSpec((tm, D_MODEL), lambda i: (i, 0)),
        out_shape=jax.ShapeDtypeStruct((n, D_MODEL), F32),
        compiler_params=_cparams(("parallel",)),
        name="ple",
    )(x, p, g, wgate, wproj, fin)


def _tiles(n, seq):
    tm = min(512, seq)
    tm_big = 1024 if n % 1024 == 0 else tm
    tm_in = 2048 if n % 2048 == 0 else tm_big
    moe_rows = tm_big // 4 + tm_big // 32
    return dict(tm=tm, tm_big=tm_big, tm_in=tm_in, tn_in=1024, tq=min(512, seq), cblk=min(512, seq), moe_rows=moe_rows)


def _rope_tables(seq):
    half = ROPE_DIMS // 2
    inv = ROPE_THETA ** (-jnp.arange(half, dtype=F32) * 2.0 / ROPE_DIMS)
    ang = jnp.arange(seq, dtype=F32)[:, None] * inv[None, :]
    pad = jnp.zeros((seq, DIFF_DH - ROPE_DIMS), F32)
    cos = jnp.concatenate([jnp.cos(ang), jnp.cos(ang), pad + 1.0], axis=-1)
    sin = jnp.concatenate([jnp.sin(ang), jnp.sin(ang), pad], axis=-1)
    d = jnp.arange(W_MIX)
    dd = d % DIFF_DH
    src = jnp.where(dd < half, d + half, d - half)
    sign = jnp.where(dd < half, -1.0, jnp.where(dd < ROPE_DIMS, 1.0, 0.0))
    rot = jnp.zeros((W_MIX, W_MIX), F32).at[src, d].set(sign)
    reps = W_MIX // DIFF_DH
    return jnp.tile(cos, (1, reps)), jnp.tile(sin, (1, reps)), rot.astype(BF16)


def _split_w_in(w):
    a0 = 0
    b0 = a0 + 4 * W_MIX
    c0 = b0 + 3 * W_MIX
    d0 = c0 + 3 * W_MIX + N_HEADS
    g0 = d0 + 4 * W_MIX + 2 * N_HEADS
    d_small = d0 + 3 * W_MIX
    diff_q = w[:, b0:b0 + W_MIX] * (DIFF_DH ** -0.5 * LOG2E)
    fox_q = w[:, c0:c0 + W_MIX] * (HEAD_DIM ** -0.5 * LOG2E)
    main = jnp.concatenate([
        w[:, g0:], w[:, a0:b0], w[:, d0:d_small], w[:, d_small + 2 * N_HEADS:g0],
        diff_q, w[:, b0 + W_MIX:b0 + 2 * W_MIX], fox_q, w[:, c0 + W_MIX:c0 + 2 * W_MIX],
    ], axis=1).astype(BF16)
    small = jnp.concatenate([
        w[:, c0 + 3 * W_MIX:d0], w[:, d_small:d_small + 2 * N_HEADS],
        jnp.zeros((D_MODEL, LANES - 3 * N_HEADS), w.dtype),
    ], axis=1).astype(BF16)
    v_t = jnp.concatenate([w[:, b0 + 2 * W_MIX:c0], w[:, c0 + 2 * W_MIX:c0 + 3 * W_MIX]], axis=1).T.astype(BF16)
    return main, small, v_t


def _small_params(fbias, a_log, dt_bias):
    zeros = jnp.zeros((N_HEADS,), F32)
    bias = jnp.concatenate([fbias, zeros, dt_bias, jnp.zeros((LANES - 3 * N_HEADS,), F32)])
    neg_a = jnp.concatenate([zeros, zeros, -jnp.exp(a_log), jnp.zeros((LANES - 3 * N_HEADS,), F32)])
    return bias.reshape(1, LANES), neg_a.reshape(1, LANES)


def _pad_rows(w, top, total):
    return jnp.concatenate([jnp.zeros((top, w.shape[1]), w.dtype), w,
                            jnp.zeros((total - top - w.shape[0], w.shape[1]), w.dtype)], axis=0)


def kernel(x, p, norm_mix, norm_ffn, norm_ple, w_in, w_bo, w_out, rwkv_mu, rwkv_w0, rwkv_w2, rwkv_a0, rwkv_a2, rwkv_g2, rwkv_kk, rwkv_ka, rwkv_rk, rwkv_ln_w, rwkv_ln_b, diff_lam, diff_subln, fox_fbias, gdn_conv, gdn_a_log, gdn_dt_bias, gdn_norm, ffn_w_gate, ffn_w_up, ffn_w_down, moe_router, moe_w_gate, moe_w_up, moe_w_down, ple_proj, ple_gate, final_norm):
    batch, seq, _ = x.shape
    depth = w_in.shape[0]
    n = batch * seq
    t = _tiles(n, seq)
    tm, tq, cblk = t["tm"], t["tq"], t["cblk"]
    row = lambda v: v.reshape(1, -1).astype(F32)
    cos, sin, rot = _rope_tables(seq)
    xf = x.reshape(n, D_MODEL)
    pf = p.reshape(depth, n, P_DIM)

    for i in range(depth):
        w_main, w_small, w_vt = _split_w_in(w_in[i])
        u, scol, vt = _inproj(xf, row(norm_mix[i]), w_main, w_small, w_vt, t["tm_in"], t["tn_in"])
        bias, neg_a = _small_params(fox_fbias[i], gdn_a_log[i], gdn_dt_bias[i])
        hcol, fox_kb, fox_qb = _small_prep(scol, bias, neg_a, batch, seq, tm)

        scan_in, pc, post = _rwkv_prep(
            u, row(rwkv_mu[i]), row(rwkv_w0[i]), _pad_rows(rwkv_w2[i], 0, LANES), row(rwkv_a0[i]),
            _pad_rows(rwkv_a2[i], LANES // 2, LANES), rwkv_g2[i], row(rwkv_kk[i]), row(rwkv_ka[i]), row(rwkv_rk[i]),
            seq, tm)
        o_a = _rwkv_chunk(scan_in, pc, batch, seq, cblk)

        o_d = _gdn_chunk(_gdn_prep(u, gdn_conv[i].T, seq, tm), hcol, jnp.tile(row(gdn_norm[i]), (1, N_HEADS)),
                         batch, seq, cblk)

        lam_init = 0.8 - 0.6 * math.exp(-0.3 * i)
        y_b = _diff_attention(u, vt, cos, sin, rot, diff_lam[i].astype(F32), jnp.tile(row(diff_subln[i]), (1, 2)),
                              batch, seq, tq, lam_init)
        y_c = _fox_attention(u, vt, fox_qb, fox_kb, batch, seq, tq)

        xf = _merge(xf, u, o_a, post, y_b, y_c, o_d, row(rwkv_ln_w[i]), row(rwkv_ln_b[i]),
                    w_bo[i].astype(BF16), w_out[i].astype(BF16), tm)

        j = i // 2
        if i % 2 == 0:
            xf = _ffn(xf, row(norm_ffn[i]), ffn_w_gate[j].astype(BF16), ffn_w_up[j].astype(BF16),
                      ffn_w_down[j].astype(BF16), tm)
        else:
            router = jnp.concatenate([moe_router[j], jnp.zeros((D_MODEL, LANES - N_EXPERTS), F32)], axis=1)
            xf = _moe(xf, row(norm_ffn[i]), router, moe_w_gate[j].astype(BF16), moe_w_up[j].astype(BF16),
                      moe_w_down[j].astype(BF16), t["tm_big"], moe_w_gate.shape[3] // 2, t["moe_rows"])
        xf = _ple(xf, pf[i], row(norm_ple[i]), ple_gate[i].astype(BF16), ple_proj[i].astype(BF16),
                  row(final_norm), tm, i == depth - 1)
    return xf.reshape(batch, seq, D_MODEL)
```

```python
import functools
import math

import jax
import jax.numpy as jnp
import numpy as np
from jax import lax
from jax.experimental import pallas as pl
from jax.experimental.pallas import tpu as pltpu

F32 = jnp.float32
BF16 = jnp.bfloat16

D_MODEL = 1024
P_DIM = 256
W_MIX = 256
HEAD_DIM = 64
N_HEADS = 4
DIFF_DH = 32
ROPE_THETA = 500000.0
ROPE_DIMS = 8
RWKV_GN_EPS = 64e-5
DIFF_LN_EPS = 1e-5
GDN_CONV = 4
CHUNK = 64
CHUNK_UNROLL = 4
N_EXPERTS = 8
NORM_EPS = 1e-6
L2_EPS = 1e-6
LOG2E = math.log2(math.e)
LANES = 128
SUBLANES = 8
BF16_ROWS = 16
VMEM_MB = 48
VMEM_LARGE_MB = 56
assert CHUNK == HEAD_DIM

U_GATE = 0
U_RWKV = 4096
U_GDN = 5120
U_DIFF = 6144
U_FOX = 6656
U_COLS = 7168
SM_FOX, SM_BETA, SM_DEC = 0, 4, 8
FOX_CK, FOX_CQ = 0, 16


def _cparams(semantics, vmem_mb=VMEM_MB):
    return pltpu.CompilerParams(dimension_semantics=semantics, vmem_limit_bytes=vmem_mb * 1024 * 1024)


def _mm(a, b):
    return jnp.dot(a.astype(BF16), b.astype(BF16), preferred_element_type=F32)


def _mm_nt(a, b):
    return lax.dot_general(a.astype(BF16), b.astype(BF16), (((1,), (1,)), ((), ())), preferred_element_type=F32)


def _mm_tn(a, b):
    return lax.dot_general(a.astype(BF16), b.astype(BF16), (((0,), (0,)), ((), ())), preferred_element_type=F32)


def _mm_split(a, b):
    ah = a.astype(BF16)
    al = (a - ah.astype(F32)).astype(BF16)
    bh = b.astype(BF16)
    bl = (b - bh.astype(F32)).astype(BF16)
    dot = lambda x, y: jnp.dot(x, y, preferred_element_type=F32)
    return dot(ah, bh) + (dot(ah, bl) + dot(al, bh))


def _mm_mask(mask, x):
    hi = x.astype(BF16)
    r1 = x - hi.astype(F32)
    mid = r1.astype(BF16)
    lo = (r1 - mid.astype(F32)).astype(BF16)
    dot = lambda t: jnp.dot(mask, t, preferred_element_type=F32)
    return dot(hi) + dot(mid) + dot(lo)


def _head_sums(x, same):
    hi = x.astype(BF16)
    lo = (x - hi.astype(F32)).astype(BF16)
    return jnp.dot(hi, same, preferred_element_type=F32) + jnp.dot(lo, same, preferred_element_type=F32)


def _rms(x, g, eps):
    return x * lax.rsqrt(jnp.mean(x * x, axis=-1, keepdims=True) + eps) * g


def _sigmoid(x):
    return 0.5 * jnp.tanh(0.5 * x) + 0.5


def _silu(x):
    return x * _sigmoid(x)


def _softplus(x):
    return jnp.maximum(x, 0.0) + jnp.log(1.0 + jnp.exp(-jnp.abs(x)))


def _tri_masks(c):
    ii = lax.broadcasted_iota(jnp.int32, (c, c), 0)
    jj = lax.broadcasted_iota(jnp.int32, (c, c), 1)
    return ii > jj, ii >= jj, ii == jj


def _block_masks():
    ii = lax.broadcasted_iota(jnp.int32, (W_MIX, W_MIX), 0)
    jj = lax.broadcasted_iota(jnp.int32, (W_MIX, W_MIX), 1)
    return (ii // HEAD_DIM) == (jj // HEAD_DIM), ii == jj


def _wide_masks():
    ii = lax.broadcasted_iota(jnp.int32, (CHUNK, W_MIX), 0)
    jj = lax.broadcasted_iota(jnp.int32, (CHUNK, W_MIX), 1) % HEAD_DIM
    return ii > jj, ii >= jj, ii == jj


def _stack_heads(x, same):
    return jnp.where(same, jnp.concatenate([x, x, x, x], axis=0), jnp.zeros((), x.dtype))


def _unit_lower_inverses(ns, eye_w, same):
    rs = [eye_w + n for n in ns]
    ps = list(ns)
    for _ in range(int(math.log2(CHUNK)) - 1):
        ps = [_mm(p, _stack_heads(p.astype(BF16), same)) for p in ps]
        rs = [r + _mm(r, _stack_heads(p.astype(BF16), same)) for r, p in zip(rs, ps)]
    return rs


def _unstack_heads(x):
    return x[0:CHUNK] + x[CHUNK:2 * CHUNK] + x[2 * CHUNK:3 * CHUNK] + x[3 * CHUNK:4 * CHUNK]


def _inproj_kernel(x_ref, g_ref, w_ref, ws_ref, wvt_ref, u_ref, scol_ref, vt_ref, h_scr):
    @pl.when(pl.program_id(1) == 0)
    def _():
        hb = _rms(x_ref[...], g_ref[...], NORM_EPS).astype(BF16)
        h_scr[...] = hb
        scol_ref[...] = jnp.dot(hb, ws_ref[...], preferred_element_type=F32)
        vt_ref[...] = lax.dot_general(wvt_ref[...], hb, (((1,), (1,)), ((), ())),
                                      preferred_element_type=F32).astype(BF16)

    u_ref[...] = jnp.dot(h_scr[...], w_ref[...], preferred_element_type=F32).astype(BF16)


def _inproj(x, g, w, ws, wvt, tm, tn):
    n = x.shape[0]
    return pl.pallas_call(
        _inproj_kernel,
        grid=(n // tm, U_COLS // tn),
        in_specs=[
            pl.BlockSpec((tm, D_MODEL), lambda i, j: (i, 0)),
            pl.BlockSpec((1, D_MODEL), lambda i, j: (0, 0)),
            pl.BlockSpec((D_MODEL, tn), lambda i, j: (0, j)),
            pl.BlockSpec((D_MODEL, LANES), lambda i, j: (0, 0)),
            pl.BlockSpec((2 * W_MIX, D_MODEL), lambda i, j: (0, 0)),
        ],
        out_specs=[
            pl.BlockSpec((tm, tn), lambda i, j: (i, j)),
            pl.BlockSpec((tm, LANES), lambda i, j: (i, 0)),
            pl.BlockSpec((2 * W_MIX, tm), lambda i, j: (0, i)),
        ],
        out_shape=[
            jax.ShapeDtypeStruct((n, U_COLS), BF16),
            jax.ShapeDtypeStruct((n, LANES), F32),
            jax.ShapeDtypeStruct((2 * W_MIX, n), BF16),
        ],
        scratch_shapes=[pltpu.VMEM((tm, D_MODEL), BF16)],
        compiler_params=_cparams(("parallel", "arbitrary"), vmem_mb=VMEM_LARGE_MB),
        name="inproj",
    )(x, g, w, ws, wvt)


def _lane_placement(base):
    r = lax.broadcasted_iota(jnp.int32, (3 * LANES, LANES), 0)
    m = lax.broadcasted_iota(jnp.int32, (3 * LANES, LANES), 1)
    head, part = r % LANES, r // LANES
    return jnp.logical_and(head < N_HEADS, m == base + 3 * head + part).astype(BF16)


def _split3(x):
    hi = x.astype(BF16)
    r1 = x - hi.astype(F32)
    mid = r1.astype(BF16)
    lo = (r1 - mid.astype(F32)).astype(BF16)
    return jnp.concatenate([hi, mid, lo], axis=1)


def _head_expansion():
    r = lax.broadcasted_iota(jnp.int32, (3 * LANES, 2 * W_MIX), 0) % LANES
    m = lax.broadcasted_iota(jnp.int32, (3 * LANES, 2 * W_MIX), 1)
    src = jnp.where(m < W_MIX, SM_BETA + m // HEAD_DIM, SM_DEC + (m - W_MIX) // HEAD_DIM)
    return (r == src).astype(BF16)


def _small_prep_kernel(scol_ref, bias_ref, nega_ref, ogb_ref, okb_ref, oqb_ref, carry_scr):
    @pl.when(pl.program_id(1) == 0)
    def _():
        carry_scr[...] = jnp.zeros_like(carry_scr)

    tm = scol_ref.shape[0]
    lower = _tri_masks(tm)[1].astype(BF16)
    z = scol_ref[...] + bias_ref[...]
    lane = lax.broadcasted_iota(jnp.int32, z.shape, 1)
    is_f = lane < SM_BETA
    is_b = jnp.logical_and(lane >= SM_BETA, lane < SM_DEC)
    is_d = jnp.logical_and(lane >= SM_DEC, lane < SM_DEC + N_HEADS)
    logf = jnp.where(is_f, jnp.minimum(z, 0.0) - jnp.log(1.0 + jnp.exp(-jnp.abs(z))), 0.0)
    gdn = jnp.where(is_b, _sigmoid(z), jnp.where(is_d, nega_ref[...] * _softplus(z), 0.0))
    ogb_ref[...] = jnp.dot(_split3(gdn), _head_expansion(), preferred_element_type=F32)
    cum = _mm_mask(lower, logf) + carry_scr[...]
    carry_scr[...] = cum[tm - 1:tm, :]
    parts = _split3(cum * LOG2E)
    ones_lanes = jnp.logical_and(lane >= FOX_CQ, lane < FOX_CQ + 3 * N_HEADS)
    okb_ref[...] = (jnp.dot(parts, _lane_placement(FOX_CK), preferred_element_type=F32)
                    + jnp.where(ones_lanes, 1.0, 0.0)).astype(BF16)
    oqb_ref[...] = jnp.dot(parts, _lane_placement(FOX_CQ), preferred_element_type=F32).astype(BF16)


def _small_prep(scol, bias, neg_a, batch, seq, tm):
    n = batch * seq
    nt = seq // tm
    tok = lambda: pl.BlockSpec((tm, LANES), lambda b, j: (b * nt + j, 0))
    const = lambda: pl.BlockSpec((1, LANES), lambda b, j: (0, 0))
    return pl.pallas_call(
        _small_prep_kernel,
        grid=(batch, nt),
        in_specs=[tok(), const(), const()],
        out_specs=[pl.BlockSpec((tm, 2 * W_MIX), lambda b, j: (b * nt + j, 0)), tok(), tok()],
        out_shape=[jax.ShapeDtypeStruct((n, 2 * W_MIX), F32), jax.ShapeDtypeStruct((n, LANES), BF16),
                   jax.ShapeDtypeStruct((n, LANES), BF16)],
        scratch_shapes=[pltpu.VMEM((1, LANES), F32)],
        compiler_params=_cparams(("parallel", "arbitrary")),
        name="small_prep",
    )(scol, bias, neg_a)


def _rwkv_prep_kernel(u_ref, up_ref, mu_ref, w0_ref, w2_ref, a0_ref, a2_ref, g2_ref, kk_ref, ka_ref, rk_ref,
                      scan_ref, pc_ref, post_ref, *, tiles_per_seq):
    tm = u_ref.shape[0]
    u = u_ref[...].astype(F32)
    prev = up_ref[...].astype(F32)[BF16_ROWS - 1:BF16_ROWS, :]
    prev = jnp.where(pl.program_id(0) % tiles_per_seq == 0, 0.0, prev)
    rows = lax.broadcasted_iota(jnp.int32, (tm, 1), 0)
    u_prev = jnp.where(rows == 0, prev, pltpu.roll(u, 1, 0))
    xm = u + (u_prev - u) * mu_ref[...]
    r = xm[:, 0:W_MIX]
    k = xm[:, W_MIX:2 * W_MIX]
    v = xm[:, 2 * W_MIX:3 * W_MIX]
    x_lora = xm[:, 3 * W_MIX:3 * W_MIX + LANES]
    xg = xm[:, 3 * W_MIX + LANES:]
    logw = -_softplus(-(w0_ref[...] + _mm(jnp.tanh(x_lora), w2_ref[...]))) - 0.5
    log_decay = -jnp.exp(logw)
    a = _sigmoid(a0_ref[...] + _mm(x_lora, a2_ref[...]))
    g = _mm(_sigmoid(xg), g2_ref[...])
    same = _block_masks()[0].astype(BF16)
    kk_raw = k * kk_ref[...]
    kk = kk_raw * lax.rsqrt(_head_sums(kk_raw * kk_raw, same) + L2_EPS)
    k2 = k * (1.0 + (a - 1.0) * ka_ref[...])
    bonus = _head_sums(r * k2 * rk_ref[...], same) * v
    ti = lax.broadcasted_iota(jnp.int32, (tm, tm), 0)
    tj = lax.broadcasted_iota(jnp.int32, (tm, tm), 1)
    in_chunk = jnp.logical_and(ti // CHUNK == tj // CHUNK, ti >= tj).astype(BF16)
    ci = lax.broadcasted_iota(jnp.int32, (tm // CHUNK, tm), 0)
    cj = lax.broadcasted_iota(jnp.int32, (tm // CHUNK, tm), 1)
    cum = _mm_mask(in_chunk, log_decay)
    cum_end = _mm_mask((ci == cj // CHUNK).astype(BF16), log_decay)
    inv = jnp.exp(-cum)
    scan_ref[0] = (-kk * jnp.exp(cum - log_decay)).astype(BF16)
    scan_ref[1] = (kk * a * inv).astype(BF16)
    scan_ref[2] = (k2 * inv).astype(BF16)
    scan_ref[3] = (r * jnp.exp(cum)).astype(BF16)
    scan_ref[4] = v.astype(BF16)
    pc_ref[...] = jnp.exp(cum_end)
    post_ref[0] = g.astype(BF16)
    post_ref[1] = bonus.astype(BF16)


def _rwkv_prep(u, mu, w0, w2p, a0, a2p, g2, k_k, k_a, r_k, seq, tm):
    n = u.shape[0]
    ublk = U_RWKV // D_MODEL
    row = lambda c: pl.BlockSpec((1, c), lambda i: (0, 0))
    mat = lambda r: pl.BlockSpec((r, W_MIX), lambda i: (0, 0))
    return pl.pallas_call(
        functools.partial(_rwkv_prep_kernel, tiles_per_seq=seq // tm),
        grid=(n // tm,),
        in_specs=[
            pl.BlockSpec((tm, D_MODEL), lambda i: (i, ublk)),
            pl.BlockSpec((BF16_ROWS, D_MODEL), lambda i: (jnp.maximum(i * (tm // BF16_ROWS) - 1, 0), ublk)),
            row(D_MODEL), row(W_MIX), mat(LANES), row(W_MIX), mat(LANES), mat(LANES), row(W_MIX), row(W_MIX), row(W_MIX),
        ],
        out_specs=[
            pl.BlockSpec((5, tm, W_MIX), lambda i: (0, i, 0)),
            pl.BlockSpec((tm // CHUNK, W_MIX), lambda i: (i, 0)),
            pl.BlockSpec((2, tm, W_MIX), lambda i: (0, i, 0)),
        ],
        out_shape=[jax.ShapeDtypeStruct((5, n, W_MIX), BF16), jax.ShapeDtypeStruct((n // CHUNK, W_MIX), F32),
                   jax.ShapeDtypeStruct((2, n, W_MIX), BF16)],
        compiler_params=_cparams(("parallel",)),
        name="rwkv_prep",
    )(u, u, mu, w0, w2p, a0, a2p, g2, k_k, k_a, r_k)


def _rwkv_chunk_kernel(x_ref, pc_ref, o_ref, s_scr, *, nchunk, nbatch):
    @pl.when(pl.program_id(0) == 0)
    def _():
        s_scr[...] = jnp.zeros_like(s_scr)

    same, diag = _block_masks()
    eye = diag.astype(F32)
    strict_w, incl_w, diag_w = _wide_masks()
    eye_w = diag_w.astype(F32)
    stack = lambda x: _stack_heads(x.astype(BF16), same)

    def chunks(i, carry):
        items = [(b, i * CHUNK_UNROLL + j) for j in range(CHUNK_UNROLL) for b in range(nbatch)]
        each = lambda f, *lists: [f(*args) for args in zip(*lists)]
        sls = [pl.ds(pl.multiple_of(ci * CHUNK, CHUNK), CHUNK) for _, ci in items]
        a, bb, k, r, v = ([x_ref[i, b, sl, :] for (b, _), sl in zip(items, sls)] for i in range(5))
        pc = [pc_ref[b, pl.ds(ci, 1), :] for b, ci in items]
        a_s, b_s, k_s, v_s = (each(stack, x) for x in (a, bb, k, v))
        m_ab = each(lambda x, y: jnp.where(strict_w, _mm_nt(x, y), 0.0), a, b_s)
        m_ak = each(lambda x, y: jnp.where(strict_w, _mm_nt(x, y), 0.0), a, k_s)
        n_rb = each(lambda x, y: jnp.where(incl_w, _mm_nt(x, y), 0.0), r, b_s)
        n_rk = each(lambda x, y: jnp.where(incl_w, _mm_nt(x, y), 0.0), r, k_s)
        t_inv = _unit_lower_inverses(m_ab, eye_w, same)
        a2 = each(_mm, t_inv, a_s)
        u0 = each(lambda t, m, x: _mm(t, stack(_mm(m, x))), t_inv, m_ak, v_s)
        r2 = each(lambda x, n, y: x.astype(F32) + _mm(n, stack(y)), r, n_rb, a2)
        o0 = each(lambda n1, u, n2, x: _mm(n1, stack(u)) + _mm(n2, x), n_rb, u0, n_rk, v_s)
        b_end = each(lambda x, p: x.astype(F32) * p, bb, pc)
        k_end = each(lambda x, p: x.astype(F32) * p, k, pc)
        g_mat = each(lambda p, x, y: eye * p + jnp.where(same, _mm_tn(x, y), 0.0), pc, a2, b_end)
        s0 = each(lambda u, x, y, z: jnp.where(same, _mm_tn(u, x) + _mm_tn(y, z), 0.0), u0, b_end, v, k_end)
        for n, ((b, _), sl) in enumerate(zip(items, sls)):
            s = s_scr[b]
            o = _stack_heads(_mm_nt(r2[n], s) + o0[n], same)
            s_scr[b] = _mm(s, g_mat[n]) + s0[n]
            mean = jnp.sum(o, axis=-1, keepdims=True) * (1.0 / HEAD_DIM)
            cen = jnp.where(same, o - mean, 0.0)
            var = jnp.sum(cen * cen, axis=-1, keepdims=True) * (1.0 / HEAD_DIM)
            o_ref[b, sl, :] = _unstack_heads(cen * lax.rsqrt(var + RWKV_GN_EPS)).astype(BF16)
        return carry

    lax.fori_loop(0, nchunk // CHUNK_UNROLL, chunks, 0)


def _rwkv_chunk(xs, pc, batch, seq, cblk):
    xs = xs.reshape(5, batch, seq, W_MIX)
    pc = pc.reshape(batch, seq // CHUNK, W_MIX)
    out = pl.pallas_call(
        functools.partial(_rwkv_chunk_kernel, nchunk=cblk // CHUNK, nbatch=batch),
        grid=(seq // cblk,),
        in_specs=[
            pl.BlockSpec((5, batch, cblk, W_MIX), lambda j: (0, 0, j, 0)),
            pl.BlockSpec((batch, cblk // CHUNK, W_MIX), lambda j: (0, j, 0)),
        ],
        out_specs=pl.BlockSpec((batch, cblk, W_MIX), lambda j: (0, j, 0)),
        out_shape=jax.ShapeDtypeStruct((batch, seq, W_MIX), BF16),
        scratch_shapes=[pltpu.VMEM((batch, W_MIX, W_MIX), F32)],
        compiler_params=_cparams(("arbitrary",)),
        name="rwkv_chunk",
    )(xs, pc)
    return out.reshape(batch * seq, W_MIX)


def _gdn_prep_kernel(u_ref, up_ref, cw_ref, o_ref, ext_scr, *, tiles_per_seq):
    tm = u_ref.shape[0]
    c3 = 3 * W_MIX
    prev = up_ref[...].astype(F32)[:, :c3]
    ext_scr[0:BF16_ROWS, :] = jnp.where(pl.program_id(0) % tiles_per_seq == 0, 0.0, prev)
    ext_scr[BF16_ROWS:, :] = u_ref[...].astype(F32)[:, :c3]
    y = jnp.zeros((tm, c3), F32)
    for j in range(GDN_CONV):
        y = y + ext_scr[pl.ds(BF16_ROWS - (GDN_CONV - 1) + j, tm), :] * cw_ref[j:j + 1, :]
    y = _silu(y)
    same = _block_masks()[0].astype(BF16)
    q = y[:, 0:W_MIX]
    k = y[:, W_MIX:2 * W_MIX]
    o_ref[0] = (q * lax.rsqrt(_head_sums(q * q, same) + L2_EPS) * (HEAD_DIM ** -0.5)).astype(BF16)
    o_ref[1] = (k * lax.rsqrt(_head_sums(k * k, same) + L2_EPS)).astype(BF16)
    o_ref[2] = y[:, 2 * W_MIX:].astype(BF16)


def _gdn_prep(u, conv_w, seq, tm):
    n = u.shape[0]
    ublk = U_GDN // D_MODEL
    return pl.pallas_call(
        functools.partial(_gdn_prep_kernel, tiles_per_seq=seq // tm),
        grid=(n // tm,),
        in_specs=[
            pl.BlockSpec((tm, D_MODEL), lambda i: (i, ublk)),
            pl.BlockSpec((BF16_ROWS, D_MODEL), lambda i: (jnp.maximum(i * (tm // BF16_ROWS) - 1, 0), ublk)),
            pl.BlockSpec((GDN_CONV, 3 * W_MIX), lambda i: (0, 0)),
        ],
        out_specs=pl.BlockSpec((3, tm, W_MIX), lambda i: (0, i, 0)),
        out_shape=jax.ShapeDtypeStruct((3, n, W_MIX), BF16),
        scratch_shapes=[pltpu.VMEM((tm + BF16_ROWS, 3 * W_MIX), F32)],
        compiler_params=_cparams(("parallel",)),
        name="gdn_prep",
    )(u, u, conv_w)


def _gdn_chunk_kernel(x_ref, gb_ref, nw_ref, o_ref, s_scr, *, nchunk, nbatch):
    @pl.when(pl.program_id(0) == 0)
    def _():
        s_scr[...] = jnp.zeros_like(s_scr)

    same, diag = _block_masks()
    eye = diag.astype(F32)
    strict_w, incl_w, diag_w = _wide_masks()
    eye_w = diag_w.astype(F32)
    lower = _tri_masks(CHUNK)[1].astype(BF16)
    stack = lambda x: _stack_heads(x.astype(BF16), same)

    def chunks(i, carry):
        items = [(b, i * CHUNK_UNROLL + j) for j in range(CHUNK_UNROLL) for b in range(nbatch)]
        each = lambda f, *lists: [f(*args) for args in zip(*lists)]
        sls = [pl.ds(pl.multiple_of(ci * CHUNK, CHUNK), CHUNK) for _, ci in items]
        q, k, v = ([x_ref[i, b, sl, :] for (b, _), sl in zip(items, sls)] for i in range(3))
        beta = [gb_ref[b, sl, 0:W_MIX] for (b, _), sl in zip(items, sls)]
        g = [gb_ref[b, sl, W_MIX:2 * W_MIX] for (b, _), sl in zip(items, sls)]
        k_s = each(stack, k)
        gam = [_mm_mask(lower, x) for x in g]
        gam_end = [x[CHUNK - 1:CHUNK, :] for x in gam]
        gdiff = [_mm_mask(lower, jnp.where(strict_w, x, 0.0)) for x in g]
        decay = [jnp.exp(jnp.where(incl_w, x, -jnp.inf)) for x in gdiff]
        a_mat = each(lambda bt, d, x, y: jnp.where(strict_w, bt * d * _mm_nt(x, y), 0.0), beta, decay, k, k_s)
        t_inv = _unit_lower_inverses([-a for a in a_mat], eye_w, same)
        e_gam = [jnp.exp(x) for x in gam]
        u0 = each(lambda t, bt, x: _mm(t, stack(bt * x.astype(F32))), t_inv, beta, v)
        wm = each(lambda t, bt, e, x: _mm(t, stack((bt * e) * x.astype(F32))), t_inv, beta, e_gam, k)
        qk = each(lambda x, y, d: _mm_nt(x, y) * d, q, k_s, decay)
        q2 = each(lambda e, x, a, w: e * x.astype(F32) - _mm(a, stack(w)), e_gam, q, qk, wm)
        o0 = each(lambda a, u: _mm(a, stack(u)), qk, u0)
        k_end = each(lambda x, ge, ga: x.astype(F32) * jnp.exp(ge - ga), k, gam_end, gam)
        g_mat = each(lambda ge, x, w: eye * jnp.exp(ge) - jnp.where(same, _mm_tn(x, w), 0.0), gam_end, k_end, wm)
        s0 = each(lambda x, u: jnp.where(same, _mm_tn(x, u), 0.0), k_end, u0)
        for n, ((b, _), sl) in enumerate(zip(items, sls)):
            s = s_scr[b]
            o = _stack_heads(_mm(q2[n], s) + o0[n], same)
            s_scr[b] = _mm(g_mat[n], s) + s0[n]
            ms = jnp.sum(o * o, axis=-1, keepdims=True) * (1.0 / HEAD_DIM)
            o_ref[b, sl, :] = (_unstack_heads(o * lax.rsqrt(ms + NORM_EPS)) * nw_ref[...]).astype(BF16)
        return carry

    lax.fori_loop(0, nchunk // CHUNK_UNROLL, chunks, 0)


def _gdn_chunk(xs, gb, norm_w, batch, seq, cblk):
    xs = xs.reshape(3, batch, seq, W_MIX)
    gb = gb.reshape(batch, seq, 2 * W_MIX)
    out = pl.pallas_call(
        functools.partial(_gdn_chunk_kernel, nchunk=cblk // CHUNK, nbatch=batch),
        grid=(seq // cblk,),
        in_specs=[
            pl.BlockSpec((3, batch, cblk, W_MIX), lambda j: (0, 0, j, 0)),
            pl.BlockSpec((batch, cblk, 2 * W_MIX), lambda j: (0, j, 0)),
            pl.BlockSpec((1, W_MIX), lambda j: (0, 0)),
        ],
        out_specs=pl.BlockSpec((batch, cblk, W_MIX), lambda j: (0, j, 0)),
        out_shape=jax.ShapeDtypeStruct((batch, seq, W_MIX), BF16),
        scratch_shapes=[pltpu.VMEM((batch, W_MIX, W_MIX), F32)],
        compiler_params=_cparams(("arbitrary",)),
        name="gdn_chunk",
    )(xs, gb, norm_w)
    return out.reshape(batch * seq, W_MIX)


def _causal_pairs(nq):
    pairs = [(i, j) for i in range(nq) for j in range(i + 1)]
    return jnp.asarray(np.array([p[0] for p in pairs], np.int32)), jnp.asarray(np.array([p[1] for p in pairs], np.int32))


def _softmax_updates(scores, vt_ones, m_scr, acc_scr):
    probs = []
    for i, s in enumerate(scores):
        m_old = m_scr[i]
        m_new = jnp.maximum(m_old, jnp.max(s, axis=0, keepdims=True))
        m_scr[i] = m_new
        probs.append((jnp.exp2(m_old - m_new), jnp.exp2((s - m_new).astype(BF16))))
    for i, (alpha, p) in enumerate(probs):
        acc_scr[i] = alpha * acc_scr[i] + jnp.dot(vt_ones[i], p, preferred_element_type=F32)


def _pair_lanes(h):
    p = h // 2
    return slice(p * LANES, (p + 1) * LANES), h % 2 == 0


ACC_ROWS = HEAD_DIM + BF16_ROWS


def _values_and_ones(vt, tk):
    ones = jnp.ones((BF16_ROWS, tk), vt.dtype)
    return [jnp.concatenate([vt[h * HEAD_DIM:(h + 1) * HEAD_DIM, :], ones], axis=0) for h in range(N_HEADS)]


def _normalized(acc):
    return acc[0:HEAD_DIM] / acc[HEAD_DIM:HEAD_DIM + 1]


def _key_after_query(tq):
    return lax.broadcasted_iota(jnp.int32, (tq, tq), 0) > lax.broadcasted_iota(jnp.int32, (tq, tq), 1)


def _fox_kernel(qi_ref, kj_ref, q_ref, k_ref, vt_ref, qb_ref, kb_ref, o_ref, qm_scr, m_scr, acc_scr):
    t = pl.program_id(1)
    qi = qi_ref[t]
    kj = kj_ref[t]
    tq = q_ref.shape[0]

    @pl.when(kj == 0)
    def _():
        m_scr[...] = jnp.full_like(m_scr, -jnp.inf)
        acc_scr[...] = jnp.zeros_like(acc_scr)
        q = q_ref[...]
        qb = qb_ref[...]
        lane = lax.broadcasted_iota(jnp.int32, (tq, LANES), 1)
        zero = jnp.zeros((), BF16)
        for h in range(N_HEADS):
            slab, low = _pair_lanes(h)
            mine = lane < HEAD_DIM if low else lane >= HEAD_DIM
            ck_lanes = jnp.logical_and(lane >= FOX_CK + 3 * h, lane < FOX_CK + 3 * h + 3)
            cq_lanes = jnp.logical_and(lane >= FOX_CQ + 3 * h, lane < FOX_CQ + 3 * h + 3)
            bias = jnp.where(ck_lanes, -jnp.ones((), BF16), jnp.where(cq_lanes, qb, zero))
            qm_scr[h] = jnp.concatenate([jnp.where(mine, q[:, slab], zero), bias], axis=1)

    def step(diagonal):
        k = k_ref[...]
        kb = kb_ref[...]
        vts = _values_and_ones(vt_ref[...], tq)
        if diagonal:
            masked = _key_after_query(tq)
        keys = [jnp.concatenate([k[:, p * LANES:(p + 1) * LANES], kb], axis=1) for p in range(N_HEADS // 2)]
        scores = []
        for h in range(N_HEADS):
            s = lax.dot_general(keys[h // 2], qm_scr[h], (((1,), (1,)), ((), ())), preferred_element_type=F32)
            if diagonal:
                s = jnp.where(masked, -jnp.inf, s)
            scores.append(s)
        _softmax_updates(scores, vts, m_scr, acc_scr)

    @pl.when(kj < qi)
    def _():
        step(False)

    @pl.when(kj == qi)
    def _():
        step(True)
        for p in range(N_HEADS // 2):
            pair = jnp.concatenate([_normalized(acc_scr[2 * p]), _normalized(acc_scr[2 * p + 1])], axis=0)
            o_ref[:, p * LANES:(p + 1) * LANES] = pair.T.astype(BF16)


def _attn_specs(nq, tq, ucol, vt_rows):
    cb = ucol // W_MIX
    q_spec = pl.BlockSpec((tq, W_MIX), lambda b, t, qi, kj: (b * nq + qi[t], cb))
    k_spec = pl.BlockSpec((tq, W_MIX), lambda b, t, qi, kj: (b * nq + kj[t], cb + 1))
    vt_spec = pl.BlockSpec((W_MIX, tq), lambda b, t, qi, kj: (vt_rows // W_MIX, b * nq + kj[t]))
    return q_spec, k_spec, vt_spec


def _fox_attention(u, vt, qb, kb, batch, seq, tq):
    nq = seq // tq
    qi, kj = _causal_pairs(nq)
    q_spec, k_spec, vt_spec = _attn_specs(nq, tq, U_FOX, W_MIX)
    stat = lambda: pltpu.VMEM((N_HEADS, 1, tq), F32)
    return pl.pallas_call(
        _fox_kernel,
        grid_spec=pltpu.PrefetchScalarGridSpec(
            num_scalar_prefetch=2,
            grid=(batch, qi.shape[0]),
            in_specs=[
                q_spec, k_spec, vt_spec,
                pl.BlockSpec((tq, LANES), lambda b, t, qi, kj: (b * nq + qi[t], 0)),
                pl.BlockSpec((tq, LANES), lambda b, t, qi, kj: (b * nq + kj[t], 0)),
            ],
            out_specs=pl.BlockSpec((tq, W_MIX), lambda b, t, qi, kj: (b * nq + qi[t], 0)),
            scratch_shapes=[pltpu.VMEM((N_HEADS, tq, 2 * LANES), BF16), stat(),
                            pltpu.VMEM((N_HEADS, ACC_ROWS, tq), F32)],
        ),
        out_shape=jax.ShapeDtypeStruct((batch * seq, W_MIX), BF16),
        compiler_params=_cparams(("parallel", "arbitrary")),
        name="fox_attention",
    )(qi, kj, u, u, vt, qb, kb)


def _diff_kernel(qi_ref, kj_ref, q_ref, k_ref, vt_ref, cq_ref, sq_ref, ck_ref, sk_ref, rot_ref, lam_ref, ln_ref, o_ref,
                 qm_scr, m_scr, acc_scr, *, lam_init):
    t = pl.program_id(1)
    qi = qi_ref[t]
    kj = kj_ref[t]
    tq = q_ref.shape[0]
    lane = lax.broadcasted_iota(jnp.int32, (tq, LANES), 1)

    def rope(x, cos, sin):
        return x.astype(F32) * cos + jnp.dot(x, rot_ref[...], preferred_element_type=F32) * sin

    @pl.when(kj == 0)
    def _():
        m_scr[...] = jnp.full_like(m_scr, -jnp.inf)
        acc_scr[...] = jnp.zeros_like(acc_scr)
        q = rope(q_ref[...], cq_ref[...], sq_ref[...])
        for h in range(N_HEADS):
            slab, low = _pair_lanes(h)
            base = 0 if low else HEAD_DIM
            for c in range(2):
                lo = base + c * DIFF_DH
                sel = jnp.logical_and(lane >= lo, lane < lo + DIFF_DH)
                qm_scr[2 * h + c] = jnp.where(sel, q[:, slab], 0.0).astype(BF16)

    def step(diagonal):
        k = rope(k_ref[...], ck_ref[...], sk_ref[...]).astype(BF16)
        vts = _values_and_ones(vt_ref[...], tq)
        if diagonal:
            masked = _key_after_query(tq)
        scores = []
        for h in range(N_HEADS):
            slab, _ = _pair_lanes(h)
            for c in range(2):
                i = 2 * h + c
                s = lax.dot_general(k[:, slab], qm_scr[i], (((1,), (1,)), ((), ())), preferred_element_type=F32)
                if diagonal:
                    s = jnp.where(masked, -jnp.inf, s)
                scores.append(s)
        _softmax_updates(scores, [vts[i // 2] for i in range(2 * N_HEADS)], m_scr, acc_scr)

    @pl.when(kj < qi)
    def _():
        step(False)

    @pl.when(kj == qi)
    def _():
        step(True)
        lp = lam_ref[...]
        lam = (jnp.exp(jnp.sum(lp[0:1] * lp[1:2], axis=-1, keepdims=True))
               - jnp.exp(jnp.sum(lp[2:3] * lp[3:4], axis=-1, keepdims=True)) + lam_init)
        head = lambda h: _normalized(acc_scr[2 * h]) - lam * _normalized(acc_scr[2 * h + 1])
        is_lo = lane < HEAD_DIM
        for p in range(N_HEADS // 2):
            o = jnp.concatenate([head(2 * p), head(2 * p + 1)], axis=0).T
            sq = o * o
            ms_lo = jnp.sum(jnp.where(is_lo, sq, 0.0), axis=-1, keepdims=True)
            ms_hi = jnp.sum(jnp.where(is_lo, 0.0, sq), axis=-1, keepdims=True)
            ms = jnp.where(is_lo, ms_lo, ms_hi) * (1.0 / HEAD_DIM)
            o_ref[:, p * LANES:(p + 1) * LANES] = (o * lax.rsqrt(ms + DIFF_LN_EPS) * ln_ref[...]
                                                   * (1.0 - lam_init)).astype(BF16)


def _diff_attention(u, vt, cos, sin, rot, lam_p, subln, batch, seq, tq, lam_init):
    nq = seq // tq
    qi, kj = _causal_pairs(nq)
    q_spec, k_spec, vt_spec = _attn_specs(nq, tq, U_DIFF, 0)
    tab_q = pl.BlockSpec((tq, W_MIX), lambda b, t, qi, kj: (qi[t], 0))
    tab_k = pl.BlockSpec((tq, W_MIX), lambda b, t, qi, kj: (kj[t], 0))
    const = lambda r, c: pl.BlockSpec((r, c), lambda b, t, qi, kj: (0, 0))
    stat = lambda: pltpu.VMEM((2 * N_HEADS, 1, tq), F32)
    return pl.pallas_call(
        functools.partial(_diff_kernel, lam_init=lam_init),
        grid_spec=pltpu.PrefetchScalarGridSpec(
            num_scalar_prefetch=2,
            grid=(batch, qi.shape[0]),
            in_specs=[q_spec, k_spec, vt_spec, tab_q, tab_q, tab_k, tab_k,
                      const(W_MIX, W_MIX), const(4, DIFF_DH), const(1, LANES)],
            out_specs=pl.BlockSpec((tq, W_MIX), lambda b, t, qi, kj: (b * nq + qi[t], 0)),
            scratch_shapes=[pltpu.VMEM((2 * N_HEADS, tq, LANES), BF16), stat(),
                            pltpu.VMEM((2 * N_HEADS, ACC_ROWS, tq), F32)],
        ),
        out_shape=jax.ShapeDtypeStruct((batch * seq, W_MIX), BF16),
        compiler_params=_cparams(("parallel", "arbitrary")),
        name="diff_attention",
    )(qi, kj, u, u, vt, cos, sin, cos, sin, rot, lam_p, subln)


def _merge_kernel(x_ref, gate_ref, oa_ref, post_ref, yb_ref, yc_ref, od_ref, gd_ref, lnw_ref, lnb_ref,
                  wbo_ref, wout_ref, o_ref):
    y_a = (oa_ref[...].astype(F32) * lnw_ref[...] + lnb_ref[...] + post_ref[1].astype(F32)) * post_ref[0].astype(F32)
    y_d = od_ref[...].astype(F32) * _silu(gd_ref[...].astype(F32))
    acc = jnp.zeros(x_ref.shape, F32)
    for b, y in enumerate((y_a, yb_ref[...], yc_ref[...], y_d)):
        gate = _sigmoid(gate_ref[:, b * D_MODEL:(b + 1) * D_MODEL].astype(F32))
        acc = acc + gate * _mm(y, wbo_ref[b])
    o_ref[...] = x_ref[...] + _mm(acc, wout_ref[...])


def _merge(x, u, o_a, post, y_b, y_c, o_d, ln_w, ln_b, w_bo, w_out, tm):
    n = x.shape[0]
    tok = lambda c: pl.BlockSpec((tm, c), lambda i: (i, 0))
    return pl.pallas_call(
        _merge_kernel,
        grid=(n // tm,),
        in_specs=[
            tok(D_MODEL),
            pl.BlockSpec((tm, 4 * D_MODEL), lambda i: (i, U_GATE // (4 * D_MODEL))),
            tok(W_MIX),
            pl.BlockSpec((2, tm, W_MIX), lambda i: (0, i, 0)),
            tok(W_MIX), tok(W_MIX), tok(W_MIX),
            pl.BlockSpec((tm, W_MIX), lambda i: (i, (U_GDN + 3 * W_MIX) // W_MIX)),
            pl.BlockSpec((1, W_MIX), lambda i: (0, 0)),
            pl.BlockSpec((1, W_MIX), lambda i: (0, 0)),
            pl.BlockSpec((4, W_MIX, D_MODEL), lambda i: (0, 0, 0)),
            pl.BlockSpec((D_MODEL, D_MODEL), lambda i: (0, 0)),
        ],
        out_specs=tok(D_MODEL),
        out_shape=jax.ShapeDtypeStruct((n, D_MODEL), F32),
        compiler_params=_cparams(("parallel",)),
        name="merge",
    )(x, u, o_a, post, y_b, y_c, o_d, u, ln_w, ln_b, w_bo, w_out)


def _ffn_kernel(x_ref, g_ref, wg_ref, wu_ref, wd_ref, o_ref):
    x = x_ref[...]
    h = _rms(x, g_ref[...], NORM_EPS).astype(BF16)
    act = _silu(jnp.dot(h, wg_ref[...], preferred_element_type=F32)) * jnp.dot(h, wu_ref[...], preferred_element_type=F32)
    o_ref[...] = x + _mm(act, wd_ref[...])


def _ffn(x, g, wg, wu, wd, tm):
    n = x.shape[0]
    ff = wg.shape[1]
    resident = lambda r, c: pl.BlockSpec((r, c), lambda i: (0, 0), pipeline_mode=pl.Buffered(1))
    return pl.pallas_call(
        _ffn_kernel,
        grid=(n // tm,),
        in_specs=[
            pl.BlockSpec((tm, D_MODEL), lambda i: (i, 0)),
            pl.BlockSpec((1, D_MODEL), lambda i: (0, 0)),
            resident(D_MODEL, ff), resident(D_MODEL, ff), resident(ff, D_MODEL),
        ],
        out_specs=pl.BlockSpec((tm, D_MODEL), lambda i: (i, 0)),
        out_shape=jax.ShapeDtypeStruct((n, D_MODEL), F32),
        compiler_params=_cparams(("parallel",), vmem_mb=VMEM_LARGE_MB),
        name="ffn",
    )(x, g, wg, wu, wd)


def _router_kernel(x_ref, g_ref, router_ref, h_ref, c_ref, rc_ref, rr_ref, cnt_ref):
    t = x_ref.shape[0]
    h = _rms(x_ref[...], g_ref[...], NORM_EPS)
    h_ref[...] = h.astype(BF16)
    logits = _mm_split(h, router_ref[...])
    lane = lax.broadcasted_iota(jnp.int32, logits.shape, 1).astype(F32)
    lg = jnp.where(lane < N_EXPERTS, logits, -jnp.inf)
    m1 = jnp.max(lg, axis=-1, keepdims=True)
    i1 = jnp.min(jnp.where(lg == m1, lane, float(LANES)), axis=-1, keepdims=True)
    lg2 = jnp.where(lane == i1, -jnp.inf, lg)
    m2 = jnp.max(lg2, axis=-1, keepdims=True)
    i2 = jnp.min(jnp.where(lg2 == m2, lane, float(LANES)), axis=-1, keepdims=True)
    e2 = jnp.exp(m2 - m1)
    c_ref[...] = jnp.where(lane == i1, 1.0 / (1.0 + e2), 0.0) + jnp.where(lane == i2, e2 / (1.0 + e2), 0.0)
    sel = jnp.logical_or(lane == i1, lane == i2)
    sel_f = jnp.where(sel, 1.0, 0.0)
    earlier = _tri_masks(t)[0].astype(BF16)
    rank = jnp.dot(earlier, sel_f.astype(BF16), preferred_element_type=F32)
    rc = jnp.where(sel, rank, -1.0)
    rc_ref[...] = rc
    rr_ref[...] = rc.T[0:BF16_ROWS, :]
    cnt_ref[...] = jnp.broadcast_to(jnp.sum(sel_f, axis=0, keepdims=True), cnt_ref.shape).astype(jnp.int32)


def _router(x, g, router, tm):
    n = x.shape[0]
    return pl.pallas_call(
        _router_kernel,
        grid=(n // tm,),
        in_specs=[
            pl.BlockSpec((tm, D_MODEL), lambda i: (i, 0)),
            pl.BlockSpec((1, D_MODEL), lambda i: (0, 0)),
            pl.BlockSpec((D_MODEL, LANES), lambda i: (0, 0)),
        ],
        out_specs=[
            pl.BlockSpec((tm, D_MODEL), lambda i: (i, 0)),
            pl.BlockSpec((tm, LANES), lambda i: (i, 0)),
            pl.BlockSpec((tm, LANES), lambda i: (i, 0)),
            pl.BlockSpec((BF16_ROWS, tm), lambda i: (0, i)),
            pl.BlockSpec((SUBLANES, LANES), lambda i: (i, 0)),
        ],
        out_shape=[
            jax.ShapeDtypeStruct((n, D_MODEL), BF16),
            jax.ShapeDtypeStruct((n, LANES), F32),
            jax.ShapeDtypeStruct((n, LANES), F32),
            jax.ShapeDtypeStruct((BF16_ROWS, n), F32),
            jax.ShapeDtypeStruct((n // tm * SUBLANES, LANES), jnp.int32),
        ],
        compiler_params=_cparams(("parallel",)),
        name="router",
    )(x, g, router)


def _moe_kernel(cnt_ref, x_ref, h_ref, c_ref, rc_ref, rr_ref, wg_ref, wu_ref, wd_ref, o_ref, xg_scr, yg_scr, *, rows):
    i = pl.program_id(0)
    e = pl.program_id(1)
    f = pl.program_id(2)
    last_f = f == pl.num_programs(2) - 1
    t = h_ref.shape[0]
    nsub = xg_scr.shape[0] // rows
    cnt = cnt_ref[i * N_EXPERTS + e]
    blocks = [(sb, slice(sb * rows, (sb + 1) * rows)) for sb in range(nsub)]

    @pl.when(jnp.logical_and(e == 0, f == 0))
    def _():
        o_ref[...] = x_ref[...]

    @pl.when(f == 0)
    def _():
        rr = rr_ref[pl.ds(e, 1), :]
        for sb, rs in blocks:
            @pl.when(cnt > sb * rows)
            def _():
                slot = (lax.broadcasted_iota(jnp.int32, (rows, t), 0) + sb * rows).astype(F32)
                gather = jnp.where(rr == slot, 1.0, 0.0).astype(BF16)
                xg_scr[rs, :] = jnp.dot(gather, h_ref[...], preferred_element_type=F32).astype(BF16)
                yg_scr[rs, :] = jnp.zeros((rows, D_MODEL), F32)

    for sb, rs in blocks:
        @pl.when(cnt > sb * rows)
        def _():
            xb = xg_scr[rs, :]
            act = (_silu(jnp.dot(xb, wg_ref[0], preferred_element_type=F32))
                   * jnp.dot(xb, wu_ref[0], preferred_element_type=F32))
            yg_scr[rs, :] += _mm(act, wd_ref[0])

    @pl.when(last_f)
    def _():
        lane = lax.broadcasted_iota(jnp.int32, (t, LANES), 1)
        mine = lane == e
        rc = jnp.sum(jnp.where(mine, rc_ref[...], 0.0), axis=-1, keepdims=True)
        ce = jnp.sum(jnp.where(mine, c_ref[...], 0.0), axis=-1, keepdims=True)
        for sb, rs in blocks:
            @pl.when(cnt > sb * rows)
            def _():
                slot = (lax.broadcasted_iota(jnp.int32, (t, rows), 1) + sb * rows).astype(F32)
                scatter = jnp.where(rc == slot, 1.0, 0.0).astype(BF16)
                o_ref[...] += ce * jnp.dot(scatter, yg_scr[rs, :].astype(BF16), preferred_element_type=F32)


def _moe(x, g, router, wg, wu, wd, tm, tf, rows):
    n = x.shape[0]
    ff = wg.shape[2]
    h, c, rc, rr, cnt = _router(x, g, router, tm)
    cnt = cnt.reshape(n // tm, SUBLANES, LANES)[:, 0, :N_EXPERTS].reshape(-1)
    nsub = -(-tm // rows)
    tok = lambda cols, **kw: pl.BlockSpec((tm, cols), lambda i, e, f, cnt: (i, 0), **kw)
    once = dict(pipeline_mode=pl.Buffered(1))
    return pl.pallas_call(
        functools.partial(_moe_kernel, rows=rows),
        grid_spec=pltpu.PrefetchScalarGridSpec(
            num_scalar_prefetch=1,
            grid=(n // tm, N_EXPERTS, ff // tf),
            in_specs=[
                tok(D_MODEL, **once), tok(D_MODEL, **once), tok(LANES), tok(LANES),
                pl.BlockSpec((BF16_ROWS, tm), lambda i, e, f, cnt: (0, i)),
                pl.BlockSpec((1, D_MODEL, tf), lambda i, e, f, cnt: (e, 0, f)),
                pl.BlockSpec((1, D_MODEL, tf), lambda i, e, f, cnt: (e, 0, f)),
                pl.BlockSpec((1, tf, D_MODEL), lambda i, e, f, cnt: (e, f, 0)),
            ],
            out_specs=tok(D_MODEL),
            scratch_shapes=[pltpu.VMEM((nsub * rows, D_MODEL), BF16), pltpu.VMEM((nsub * rows, D_MODEL), F32)],
        ),
        out_shape=jax.ShapeDtypeStruct((n, D_MODEL), F32),
        compiler_params=_cparams(("parallel", "arbitrary", "arbitrary"), vmem_mb=VMEM_LARGE_MB),
        name="moe",
    )(cnt, x, h, c, rc, rr, wg, wu, wd)


def _ple_kernel(x_ref, p_ref, g_ref, wgate_ref, wproj_ref, fin_ref, o_ref, *, final):
    x = x_ref[...]
    h = _rms(x, g_ref[...], NORM_EPS)
    y = x + _sigmoid(_mm(h, wgate_ref[...])) * _mm(p_ref[...], wproj_ref[...])
    o_ref[...] = _rms(y, fin_ref[...], NORM_EPS) if final else y


def _ple(x, p, g, wgate, wproj, fin, tm, final):
    n = x.shape[0]
    return pl.pallas_call(
        functools.partial(_ple_kernel, final=final),
        grid=(n // tm,),
        in_specs=[
            pl.BlockSpec((tm, D_MODEL), lambda i: (i, 0)),
            pl.BlockSpec((tm, P_DIM), lambda i: (i, 0)),
            pl.BlockSpec((1, D_MODEL), lambda i: (0, 0)),
            pl.BlockSpec((D_MODEL, D_MODEL), lambda i: (0, 0)),
            pl.BlockSpec((P_DIM, D_MODEL), lambda i: (0, 0)),
            pl.BlockSpec((1, D_MODEL), lambda i: (0, 0)),
        ],
        out_specs=pl.BlockSpec((tm, D_MODEL), lambda i: (i, 0)),
        out_shape=jax.ShapeDtypeStruct((n, D_MODEL), F32),
        compiler_params=_cparams(("parallel",)),
        name="ple",
    )(x, p, g, wgate, wproj, fin)


def _tiles(n, seq):
    tm = min(512, seq)
    tm_big = 1024 if n % 1024 == 0 else tm
    tm_in = 2048 if n % 2048 == 0 else tm_big
    moe_rows = tm_big // 4 + tm_big // 32
    return dict(tm=tm, tm_big=tm_big, tm_in=tm_in, tn_in=1024, tq=min(512, seq), cblk=min(512, seq), moe_rows=moe_rows)


def _rope_tables(seq):
    half = ROPE_DIMS // 2
    inv = ROPE_THETA ** (-jnp.arange(half, dtype=F32) * 2.0 / ROPE_DIMS)
    ang = jnp.arange(seq, dtype=F32)[:, None] * inv[None, :]
    pad = jnp.zeros((seq, DIFF_DH - ROPE_DIMS), F32)
    cos = jnp.concatenate([jnp.cos(ang), jnp.cos(ang), pad + 1.0], axis=-1)
    sin = jnp.concatenate([jnp.sin(ang), jnp.sin(ang), pad], axis=-1)
    d = jnp.arange(W_MIX)
    dd = d % DIFF_DH
    src = jnp.where(dd < half, d + half, d - half)
    sign = jnp.where(dd < half, -1.0, jnp.where(dd < ROPE_DIMS, 1.0, 0.0))
    rot = jnp.zeros((W_MIX, W_MIX), F32).at[src, d].set(sign)
    reps = W_MIX // DIFF_DH
    return jnp.tile(cos, (1, reps)), jnp.tile(sin, (1, reps)), rot.astype(BF16)


def _split_w_in(w):
    a0 = 0
    b0 = a0 + 4 * W_MIX
    c0 = b0 + 3 * W_MIX
    d0 = c0 + 3 * W_MIX + N_HEADS
    g0 = d0 + 4 * W_MIX + 2 * N_HEADS
    d_small = d0 + 3 * W_MIX
    diff_q = w[:, b0:b0 + W_MIX] * (DIFF_DH ** -0.5 * LOG2E)
    fox_q = w[:, c0:c0 + W_MIX] * (HEAD_DIM ** -0.5 * LOG2E)
    main = jnp.concatenate([
        w[:, g0:], w[:, a0:b0], w[:, d0:d_small], w[:, d_small + 2 * N_HEADS:g0],
        diff_q, w[:, b0 + W_MIX:b0 + 2 * W_MIX], fox_q, w[:, c0 + W_MIX:c0 + 2 * W_MIX],
    ], axis=1).astype(BF16)
    small = jnp.concatenate([
        w[:, c0 + 3 * W_MIX:d0], w[:, d_small:d_small + 2 * N_HEADS],
        jnp.zeros((D_MODEL, LANES - 3 * N_HEADS), w.dtype),
    ], axis=1).astype(BF16)
    v_t = jnp.concatenate([w[:, b0 + 2 * W_MIX:c0], w[:, c0 + 2 * W_MIX:c0 + 3 * W_MIX]], axis=1).T.astype(BF16)
    return main, small, v_t


def _small_params(fbias, a_log, dt_bias):
    zeros = jnp.zeros((N_HEADS,), F32)
    bias = jnp.concatenate([fbias, zeros, dt_bias, jnp.zeros((LANES - 3 * N_HEADS,), F32)])
    neg_a = jnp.concatenate([zeros, zeros, -jnp.exp(a_log), jnp.zeros((LANES - 3 * N_HEADS,), F32)])
    return bias.reshape(1, LANES), neg_a.reshape(1, LANES)


def _pad_rows(w, top, total):
    return jnp.concatenate([jnp.zeros((top, w.shape[1]), w.dtype), w,
                            jnp.zeros((total - top - w.shape[0], w.shape[1]), w.dtype)], axis=0)


def kernel(x, p, norm_mix, norm_ffn, norm_ple, w_in, w_bo, w_out, rwkv_mu, rwkv_w0, rwkv_w2, rwkv_a0, rwkv_a2, rwkv_g2, rwkv_kk, rwkv_ka, rwkv_rk, rwkv_ln_w, rwkv_ln_b, diff_lam, diff_subln, fox_fbias, gdn_conv, gdn_a_log, gdn_dt_bias, gdn_norm, ffn_w_gate, ffn_w_up, ffn_w_down, moe_router, moe_w_gate, moe_w_up, moe_w_down, ple_proj, ple_gate, final_norm):
    batch, seq, _ = x.shape
    depth = w_in.shape[0]
    n = batch * seq
    t = _tiles(n, seq)
    tm, tq, cblk = t["tm"], t["tq"], t["cblk"]
    row = lambda v: v.reshape(1, -1).astype(F32)
    cos, sin, rot = _rope_tables(seq)
    xf = x.reshape(n, D_MODEL)
    pf = p.reshape(depth, n, P_DIM)

    for i in range(depth):
        w_main, w_small, w_vt = _split_w_in(w_in[i])
        u, scol, vt = _inproj(xf, row(norm_mix[i]), w_main, w_small, w_vt, t["tm_in"], t["tn_in"])
        bias, neg_a = _small_params(fox_fbias[i], gdn_a_log[i], gdn_dt_bias[i])
        hcol, fox_kb, fox_qb = _small_prep(scol, bias, neg_a, batch, seq, tm)

        scan_in, pc, post = _rwkv_prep(
            u, row(rwkv_mu[i]), row(rwkv_w0[i]), _pad_rows(rwkv_w2[i], 0, LANES), row(rwkv_a0[i]),
            _pad_rows(rwkv_a2[i], LANES // 2, LANES), rwkv_g2[i], row(rwkv_kk[i]), row(rwkv_ka[i]), row(rwkv_rk[i]),
            seq, tm)
        o_a = _rwkv_chunk(scan_in, pc, batch, seq, cblk)

        o_d = _gdn_chunk(_gdn_prep(u, gdn_conv[i].T, seq, tm), hcol, jnp.tile(row(gdn_norm[i]), (1, N_HEADS)),
                         batch, seq, cblk)

        lam_init = 0.8 - 0.6 * math.exp(-0.3 * i)
        y_b = _diff_attention(u, vt, cos, sin, rot, diff_lam[i].astype(F32), jnp.tile(row(diff_subln[i]), (1, 2)),
                              batch, seq, tq, lam_init)
        y_c = _fox_attention(u, vt, fox_qb, fox_kb, batch, seq, tq)

        xf = _merge(xf, u, o_a, post, y_b, y_c, o_d, row(rwkv_ln_w[i]), row(rwkv_ln_b[i]),
                    w_bo[i].astype(BF16), w_out[i].astype(BF16), tm)

        j = i // 2
        if i % 2 == 0:
            xf = _ffn(xf, row(norm_ffn[i]), ffn_w_gate[j].astype(BF16), ffn_w_up[j].astype(BF16),
                      ffn_w_down[j].astype(BF16), tm)
        else:
            router = jnp.concatenate([moe_router[j], jnp.zeros((D_MODEL, LANES - N_EXPERTS), F32)], axis=1)
            xf = _moe(xf, row(norm_ffn[i]), router, moe_w_gate[j].astype(BF16), moe_w_up[j].astype(BF16),
                      moe_w_down[j].astype(BF16), t["tm_big"], moe_w_gate.shape[3] // 2, t["moe_rows"])
        xf = _ple(xf, pf[i], row(norm_ple[i]), ple_gate[i].astype(BF16), ple_proj[i].astype(BF16),
                  row(final_norm), tm, i == depth - 1)
    return xf.reshape(batch, seq, D_MODEL)
```

```python
import functools
import math

import jax
import jax.numpy as jnp
import numpy as np
from jax import lax
from jax.experimental import pallas as pl
from jax.experimental.pallas import tpu as pltpu

F32 = jnp.float32
BF16 = jnp.bfloat16

D_MODEL = 1024
P_DIM = 256
W_MIX = 256
HEAD_DIM = 64
N_HEADS = 4
DIFF_DH = 32
ROPE_THETA = 500000.0
ROPE_DIMS = 8
RWKV_GN_EPS = 64e-5
DIFF_LN_EPS = 1e-5
GDN_CONV = 4
CHUNK = 64
CHUNK_UNROLL = 4
N_EXPERTS = 8
NORM_EPS = 1e-6
L2_EPS = 1e-6
LOG2E = math.log2(math.e)
LANES = 128
SUBLANES = 8
BF16_ROWS = 16
VMEM_MB = 48
VMEM_LARGE_MB = 56
assert CHUNK == HEAD_DIM

U_GATE = 0
U_RWKV = 4096
U_GDN = 5120
U_DIFF = 6144
U_FOX = 6656
U_COLS = 7168
SM_FOX, SM_BETA, SM_DEC = 0, 4, 8
FOX_CK, FOX_CQ = 0, 16


def _cparams(semantics, vmem_mb=VMEM_MB):
    return pltpu.CompilerParams(dimension_semantics=semantics, vmem_limit_bytes=vmem_mb * 1024 * 1024)


def _mm(a, b):
    return jnp.dot(a.astype(BF16), b.astype(BF16), preferred_element_type=F32)


def _mm_nt(a, b):
    return lax.dot_general(a.astype(BF16), b.astype(BF16), (((1,), (1,)), ((), ())), preferred_element_type=F32)


def _mm_tn(a, b):
    return lax.dot_general(a.astype(BF16), b.astype(BF16), (((0,), (0,)), ((), ())), preferred_element_type=F32)


def _mm_split(a, b):
    ah = a.astype(BF16)
    al = (a - ah.astype(F32)).astype(BF16)
    bh = b.astype(BF16)
    bl = (b - bh.astype(F32)).astype(BF16)
    dot = lambda x, y: jnp.dot(x, y, preferred_element_type=F32)
    return dot(ah, bh) + (dot(ah, bl) + dot(al, bh))


def _mm_mask(mask, x):
    hi = x.astype(BF16)
    r1 = x - hi.astype(F32)
    mid = r1.astype(BF16)
    lo = (r1 - mid.astype(F32)).astype(BF16)
    dot = lambda t: jnp.dot(mask, t, preferred_element_type=F32)
    return dot(hi) + dot(mid) + dot(lo)


def _head_sums(x, same):
    hi = x.astype(BF16)
    lo = (x - hi.astype(F32)).astype(BF16)
    return jnp.dot(hi, same, preferred_element_type=F32) + jnp.dot(lo, same, preferred_element_type=F32)


def _rms(x, g, eps):
    return x * lax.rsqrt(jnp.mean(x * x, axis=-1, keepdims=True) + eps) * g


def _sigmoid(x):
    return 0.5 * jnp.tanh(0.5 * x) + 0.5


def _silu(x):
    return x * _sigmoid(x)


def _softplus(x):
    return jnp.maximum(x, 0.0) + jnp.log(1.0 + jnp.exp(-jnp.abs(x)))


def _tri_masks(c):
    ii = lax.broadcasted_iota(jnp.int32, (c, c), 0)
    jj = lax.broadcasted_iota(jnp.int32, (c, c), 1)
    return ii > jj, ii >= jj, ii == jj


def _block_masks():
    ii = lax.broadcasted_iota(jnp.int32, (W_MIX, W_MIX), 0)
    jj = lax.broadcasted_iota(jnp.int32, (W_MIX, W_MIX), 1)
    return (ii // HEAD_DIM) == (jj // HEAD_DIM), ii == jj


def _wide_masks():
    ii = lax.broadcasted_iota(jnp.int32, (CHUNK, W_MIX), 0)
    jj = lax.broadcasted_iota(jnp.int32, (CHUNK, W_MIX), 1) % HEAD_DIM
    return ii > jj, ii >= jj, ii == jj


def _stack_heads(x, same):
    return jnp.where(same, jnp.concatenate([x, x, x, x], axis=0), jnp.zeros((), x.dtype))


def _unit_lower_inverses(ns, eye_w, same):
    rs = [eye_w + n for n in ns]
    ps = list(ns)
    for _ in range(int(math.log2(CHUNK)) - 1):
        ps = [_mm(p, _stack_heads(p.astype(BF16), same)) for p in ps]
        rs = [r + _mm(r, _stack_heads(p.astype(BF16), same)) for r, p in zip(rs, ps)]
    return rs


def _unstack_heads(x):
    return x[0:CHUNK] + x[CHUNK:2 * CHUNK] + x[2 * CHUNK:3 * CHUNK] + x[3 * CHUNK:4 * CHUNK]


def _inproj_kernel(x_ref, g_ref, w_ref, ws_ref, wvt_ref, u_ref, scol_ref, vt_ref, h_scr):
    @pl.when(pl.program_id(1) == 0)
    def _():
        hb = _rms(x_ref[...], g_ref[...], NORM_EPS).astype(BF16)
        h_scr[...] = hb
        scol_ref[...] = jnp.dot(hb, ws_ref[...], preferred_element_type=F32)
        vt_ref[...] = lax.dot_general(wvt_ref[...], hb, (((1,), (1,)), ((), ())),
                                      preferred_element_type=F32).astype(BF16)

    u_ref[...] = jnp.dot(h_scr[...], w_ref[...], preferred_element_type=F32).astype(BF16)


def _inproj(x, g, w, ws, wvt, tm, tn):
    n = x.shape[0]
    return pl.pallas_call(
        _inproj_kernel,
        grid=(n // tm, U_COLS // tn),
        in_specs=[
            pl.BlockSpec((tm, D_MODEL), lambda i, j: (i, 0)),
            pl.BlockSpec((1, D_MODEL), lambda i, j: (0, 0)),
            pl.BlockSpec((D_MODEL, tn), lambda i, j: (0, j)),
            pl.BlockSpec((D_MODEL, LANES), lambda i, j: (0, 0)),
            pl.BlockSpec((2 * W_MIX, D_MODEL), lambda i, j: (0, 0)),
        ],
        out_specs=[
            pl.BlockSpec((tm, tn), lambda i, j: (i, j)),
            pl.BlockSpec((tm, LANES), lambda i, j: (i, 0)),
            pl.BlockSpec((2 * W_MIX, tm), lambda i, j: (0, i)),
        ],
        out_shape=[
            jax.ShapeDtypeStruct((n, U_COLS), BF16),
            jax.ShapeDtypeStruct((n, LANES), F32),
            jax.ShapeDtypeStruct((2 * W_MIX, n), BF16),
        ],
        scratch_shapes=[pltpu.VMEM((tm, D_MODEL), BF16)],
        compiler_params=_cparams(("parallel", "arbitrary"), vmem_mb=VMEM_LARGE_MB),
        name="inproj",
    )(x, g, w, ws, wvt)


def _lane_placement(base):
    r = lax.broadcasted_iota(jnp.int32, (3 * LANES, LANES), 0)
    m = lax.broadcasted_iota(jnp.int32, (3 * LANES, LANES), 1)
    head, part = r % LANES, r // LANES
    return jnp.logical_and(head < N_HEADS, m == base + 3 * head + part).astype(BF16)


def _split3(x):
    hi = x.astype(BF16)
    r1 = x - hi.astype(F32)
    mid = r1.astype(BF16)
    lo = (r1 - mid.astype(F32)).astype(BF16)
    return jnp.concatenate([hi, mid, lo], axis=1)


def _head_expansion():
    r = lax.broadcasted_iota(jnp.int32, (3 * LANES, 2 * W_MIX), 0) % LANES
    m = lax.broadcasted_iota(jnp.int32, (3 * LANES, 2 * W_MIX), 1)
    src = jnp.where(m < W_MIX, SM_BETA + m // HEAD_DIM, SM_DEC + (m - W_MIX) // HEAD_DIM)
    return (r == src).astype(BF16)


def _small_prep_kernel(scol_ref, bias_ref, nega_ref, ogb_ref, okb_ref, oqb_ref, carry_scr):
    @pl.when(pl.program_id(1) == 0)
    def _():
        carry_scr[...] = jnp.zeros_like(carry_scr)

    tm = scol_ref.shape[0]
    lower = _tri_masks(tm)[1].astype(BF16)
    z = scol_ref[...] + bias_ref[...]
    lane = lax.broadcasted_iota(jnp.int32, z.shape, 1)
    is_f = lane < SM_BETA
    is_b = jnp.logical_and(lane >= SM_BETA, lane < SM_DEC)
    is_d = jnp.logical_and(lane >= SM_DEC, lane < SM_DEC + N_HEADS)
    logf = jnp.where(is_f, jnp.minimum(z, 0.0) - jnp.log(1.0 + jnp.exp(-jnp.abs(z))), 0.0)
    gdn = jnp.where(is_b, _sigmoid(z), jnp.where(is_d, nega_ref[...] * _softplus(z), 0.0))
    ogb_ref[...] = jnp.dot(_split3(gdn), _head_expansion(), preferred_element_type=F32)
    cum = _mm_mask(lower, logf) + carry_scr[...]
    carry_scr[...] = cum[tm - 1:tm, :]
    parts = _split3(cum * LOG2E)
    ones_lanes = jnp.logical_and(lane >= FOX_CQ, lane < FOX_CQ + 3 * N_HEADS)
    okb_ref[...] = (jnp.dot(parts, _lane_placement(FOX_CK), preferred_element_type=F32)
                    + jnp.where(ones_lanes, 1.0, 0.0)).astype(BF16)
    oqb_ref[...] = jnp.dot(parts, _lane_placement(FOX_CQ), preferred_element_type=F32).astype(BF16)


def _small_prep(scol, bias, neg_a, batch, seq, tm):
    n = batch * seq
    nt = seq // tm
    tok = lambda: pl.BlockSpec((tm, LANES), lambda b, j: (b * nt + j, 0))
    const = lambda: pl.BlockSpec((1, LANES), lambda b, j: (0, 0))
    return pl.pallas_call(
        _small_prep_kernel,
        grid=(batch, nt),
        in_specs=[tok(), const(), const()],
        out_specs=[pl.BlockSpec((tm, 2 * W_MIX), lambda b, j: (b * nt + j, 0)), tok(), tok()],
        out_shape=[jax.ShapeDtypeStruct((n, 2 * W_MIX), F32), jax.ShapeDtypeStruct((n, LANES), BF16),
                   jax.ShapeDtypeStruct((n, LANES), BF16)],
        scratch_shapes=[pltpu.VMEM((1, LANES), F32)],
        compiler_params=_cparams(("parallel", "arbitrary")),
        name="small_prep",
    )(scol, bias, neg_a)


def _rwkv_prep_kernel(u_ref, up_ref, mu_ref, w0_ref, w2_ref, a0_ref, a2_ref, g2_ref, kk_ref, ka_ref, rk_ref,
                      scan_ref, pc_ref, post_ref, *, tiles_per_seq):
    tm = u_ref.shape[0]
    u = u_ref[...].astype(F32)
    prev = up_ref[...].astype(F32)[BF16_ROWS - 1:BF16_ROWS, :]
    prev = jnp.where(pl.program_id(0) % tiles_per_seq == 0, 0.0, prev)
    rows = lax.broadcasted_iota(jnp.int32, (tm, 1), 0)
    u_prev = jnp.where(rows == 0, prev, pltpu.roll(u, 1, 0))
    xm = u + (u_prev - u) * mu_ref[...]
    r = xm[:, 0:W_MIX]
    k = xm[:, W_MIX:2 * W_MIX]
    v = xm[:, 2 * W_MIX:3 * W_MIX]
    x_lora = xm[:, 3 * W_MIX:3 * W_MIX + LANES]
    xg = xm[:, 3 * W_MIX + LANES:]
    logw = -_softplus(-(w0_ref[...] + _mm(jnp.tanh(x_lora), w2_ref[...]))) - 0.5
    log_decay = -jnp.exp(logw)
    a = _sigmoid(a0_ref[...] + _mm(x_lora, a2_ref[...]))
    g = _mm(_sigmoid(xg), g2_ref[...])
    same = _block_masks()[0].astype(BF16)
    kk_raw = k * kk_ref[...]
    kk = kk_raw * lax.rsqrt(_head_sums(kk_raw * kk_raw, same) + L2_EPS)
    k2 = k * (1.0 + (a - 1.0) * ka_ref[...])
    bonus = _head_sums(r * k2 * rk_ref[...], same) * v
    ti = lax.broadcasted_iota(jnp.int32, (tm, tm), 0)
    tj = lax.broadcasted_iota(jnp.int32, (tm, tm), 1)
    in_chunk = jnp.logical_and(ti // CHUNK == tj // CHUNK, ti >= tj).astype(BF16)
    ci = lax.broadcasted_iota(jnp.int32, (tm // CHUNK, tm), 0)
    cj = lax.broadcasted_iota(jnp.int32, (tm // CHUNK, tm), 1)
    cum = _mm_mask(in_chunk, log_decay)
    cum_end = _mm_mask((ci == cj // CHUNK).astype(BF16), log_decay)
    inv = jnp.exp(-cum)
    scan_ref[0] = (-kk * jnp.exp(cum - log_decay)).astype(BF16)
    scan_ref[1] = (kk * a * inv).astype(BF16)
    scan_ref[2] = (k2 * inv).astype(BF16)
    scan_ref[3] = (r * jnp.exp(cum)).astype(BF16)
    scan_ref[4] = v.astype(BF16)
    pc_ref[...] = jnp.exp(cum_end)
    post_ref[0] = g.astype(BF16)
    post_ref[1] = bonus.astype(BF16)


def _rwkv_prep(u, mu, w0, w2p, a0, a2p, g2, k_k, k_a, r_k, seq, tm):
    n = u.shape[0]
    ublk = U_RWKV // D_MODEL
    row = lambda c: pl.BlockSpec((1, c), lambda i: (0, 0))
    mat = lambda r: pl.BlockSpec((r, W_MIX), lambda i: (0, 0))
    return pl.pallas_call(
        functools.partial(_rwkv_prep_kernel, tiles_per_seq=seq // tm),
        grid=(n // tm,),
        in_specs=[
            pl.BlockSpec((tm, D_MODEL), lambda i: (i, ublk)),
            pl.BlockSpec((BF16_ROWS, D_MODEL), lambda i: (jnp.maximum(i * (tm // BF16_ROWS) - 1, 0), ublk)),
            row(D_MODEL), row(W_MIX), mat(LANES), row(W_MIX), mat(LANES), mat(LANES), row(W_MIX), row(W_MIX), row(W_MIX),
        ],
        out_specs=[
            pl.BlockSpec((5, tm, W_MIX), lambda i: (0, i, 0)),
            pl.BlockSpec((tm // CHUNK, W_MIX), lambda i: (i, 0)),
            pl.BlockSpec((2, tm, W_MIX), lambda i: (0, i, 0)),
        ],
        out_shape=[jax.ShapeDtypeStruct((5, n, W_MIX), BF16), jax.ShapeDtypeStruct((n // CHUNK, W_MIX), F32),
                   jax.ShapeDtypeStruct((2, n, W_MIX), BF16)],
        compiler_params=_cparams(("parallel",)),
        name="rwkv_prep",
    )(u, u, mu, w0, w2p, a0, a2p, g2, k_k, k_a, r_k)


def _rwkv_chunk_kernel(x_ref, pc_ref, o_ref, s_scr, *, nchunk, nbatch):
    @pl.when(pl.program_id(0) == 0)
    def _():
        s_scr[...] = jnp.zeros_like(s_scr)

    same, diag = _block_masks()
    eye = diag.astype(F32)
    strict_w, incl_w, diag_w = _wide_masks()
    eye_w = diag_w.astype(F32)
    stack = lambda x: _stack_heads(x.astype(BF16), same)

    def chunks(i, carry):
        items = [(b, i * CHUNK_UNROLL + j) for j in range(CHUNK_UNROLL) for b in range(nbatch)]
        each = lambda f, *lists: [f(*args) for args in zip(*lists)]
        sls = [pl.ds(pl.multiple_of(ci * CHUNK, CHUNK), CHUNK) for _, ci in items]
        a, bb, k, r, v = ([x_ref[i, b, sl, :] for (b, _), sl in zip(items, sls)] for i in range(5))
        pc = [pc_ref[b, pl.ds(ci, 1), :] for b, ci in items]
        a_s, b_s, k_s, v_s = (each(stack, x) for x in (a, bb, k, v))
        m_ab = each(lambda x, y: jnp.where(strict_w, _mm_nt(x, y), 0.0), a, b_s)
        m_ak = each(lambda x, y: jnp.where(strict_w, _mm_nt(x, y), 0.0), a, k_s)
        n_rb = each(lambda x, y: jnp.where(incl_w, _mm_nt(x, y), 0.0), r, b_s)
        n_rk = each(lambda x, y: jnp.where(incl_w, _mm_nt(x, y), 0.0), r, k_s)
        t_inv = _unit_lower_inverses(m_ab, eye_w, same)
        a2 = each(_mm, t_inv, a_s)
        u0 = each(lambda t, m, x: _mm(t, stack(_mm(m, x))), t_inv, m_ak, v_s)
        r2 = each(lambda x, n, y: x.astype(F32) + _mm(n, stack(y)), r, n_rb, a2)
        o0 = each(lambda n1, u, n2, x: _mm(n1, stack(u)) + _mm(n2, x), n_rb, u0, n_rk, v_s)
        b_end = each(lambda x, p: x.astype(F32) * p, bb, pc)
        k_end = each(lambda x, p: x.astype(F32) * p, k, pc)
        g_mat = each(lambda p, x, y: eye * p + jnp.where(same, _mm_tn(x, y), 0.0), pc, a2, b_end)
        s0 = each(lambda u, x, y, z: jnp.where(same, _mm_tn(u, x) + _mm_tn(y, z), 0.0), u0, b_end, v, k_end)
        for n, ((b, _), sl) in enumerate(zip(items, sls)):
            s = s_scr[b]
            o = _stack_heads(_mm_nt(r2[n], s) + o0[n], same)
            s_scr[b] = _mm(s, g_mat[n]) + s0[n]
            mean = jnp.sum(o, axis=-1, keepdims=True) * (1.0 / HEAD_DIM)
            cen = jnp.where(same, o - mean, 0.0)
            var = jnp.sum(cen * cen, axis=-1, keepdims=True) * (1.0 / HEAD_DIM)
            o_ref[b, sl, :] = _unstack_heads(cen * lax.rsqrt(var + RWKV_GN_EPS)).astype(BF16)
        return carry

    lax.fori_loop(0, nchunk // CHUNK_UNROLL, chunks, 0)


def _rwkv_chunk(xs, pc, batch, seq, cblk):
    xs = xs.reshape(5, batch, seq, W_MIX)
    pc = pc.reshape(batch, seq // CHUNK, W_MIX)
    out = pl.pallas_call(
        functools.partial(_rwkv_chunk_kernel, nchunk=cblk // CHUNK, nbatch=batch),
        grid=(seq // cblk,),
        in_specs=[
            pl.BlockSpec((5, batch, cblk, W_MIX), lambda j: (0, 0, j, 0)),
            pl.BlockSpec((batch, cblk // CHUNK, W_MIX), lambda j: (0, j, 0)),
        ],
        out_specs=pl.BlockSpec((batch, cblk, W_MIX), lambda j: (0, j, 0)),
        out_shape=jax.ShapeDtypeStruct((batch, seq, W_MIX), BF16),
        scratch_shapes=[pltpu.VMEM((batch, W_MIX, W_MIX), F32)],
        compiler_params=_cparams(("arbitrary",)),
        name="rwkv_chunk",
    )(xs, pc)
    return out.reshape(batch * seq, W_MIX)


def _gdn_prep_kernel(u_ref, up_ref, cw_ref, o_ref, ext_scr, *, tiles_per_seq):
    tm = u_ref.shape[0]
    c3 = 3 * W_MIX
    prev = up_ref[...].astype(F32)[:, :c3]
    ext_scr[0:BF16_ROWS, :] = jnp.where(pl.program_id(0) % tiles_per_seq == 0, 0.0, prev)
    ext_scr[BF16_ROWS:, :] = u_ref[...].astype(F32)[:, :c3]
    y = jnp.zeros((tm, c3), F32)
    for j in range(GDN_CONV):
        y = y + ext_scr[pl.ds(BF16_ROWS - (GDN_CONV - 1) + j, tm), :] * cw_ref[j:j + 1, :]
    y = _silu(y)
    same = _block_masks()[0].astype(BF16)
    q = y[:, 0:W_MIX]
    k = y[:, W_MIX:2 * W_MIX]
    o_ref[0] = (q * lax.rsqrt(_head_sums(q * q, same) + L2_EPS) * (HEAD_DIM ** -0.5)).astype(BF16)
    o_ref[1] = (k * lax.rsqrt(_head_sums(k * k, same) + L2_EPS)).astype(BF16)
    o_ref[2] = y[:, 2 * W_MIX:].astype(BF16)


def _gdn_prep(u, conv_w, seq, tm):
    n = u.shape[0]
    ublk = U_GDN // D_MODEL
    return pl.pallas_call(
        functools.partial(_gdn_prep_kernel, tiles_per_seq=seq // tm),
        grid=(n // tm,),
        in_specs=[
            pl.BlockSpec((tm, D_MODEL), lambda i: (i, ublk)),
            pl.BlockSpec((BF16_ROWS, D_MODEL), lambda i: (jnp.maximum(i * (tm // BF16_ROWS) - 1, 0), ublk)),
            pl.BlockSpec((GDN_CONV, 3 * W_MIX), lambda i: (0, 0)),
        ],
        out_specs=pl.BlockSpec((3, tm, W_MIX), lambda i: (0, i, 0)),
        out_shape=jax.ShapeDtypeStruct((3, n, W_MIX), BF16),
        scratch_shapes=[pltpu.VMEM((tm + BF16_ROWS, 3 * W_MIX), F32)],
        compiler_params=_cparams(("parallel",)),
        name="gdn_prep",
    )(u, u, conv_w)


def _gdn_chunk_kernel(x_ref, gb_ref, nw_ref, o_ref, s_scr, *, nchunk, nbatch):
    @pl.when(pl.program_id(0) == 0)
    def _():
        s_scr[...] = jnp.zeros_like(s_scr)

    same, diag = _block_masks()
    eye = diag.astype(F32)
    strict_w, incl_w, diag_w = _wide_masks()
    eye_w = diag_w.astype(F32)
    lower = _tri_masks(CHUNK)[1].astype(BF16)
    stack = lambda x: _stack_heads(x.astype(BF16), same)

    def chunks(i, carry):
        items = [(b, i * CHUNK_UNROLL + j) for j in range(CHUNK_UNROLL) for b in range(nbatch)]
        each = lambda f, *lists: [f(*args) for args in zip(*lists)]
        sls = [pl.ds(pl.multiple_of(ci * CHUNK, CHUNK), CHUNK) for _, ci in items]
        q, k, v = ([x_ref[i, b, sl, :] for (b, _), sl in zip(items, sls)] for i in range(3))
        beta = [gb_ref[b, sl, 0:W_MIX] for (b, _), sl in zip(items, sls)]
        g = [gb_ref[b, sl, W_MIX:2 * W_MIX] for (b, _), sl in zip(items, sls)]
        k_s = each(stack, k)
        gam = [_mm_mask(lower, x) for x in g]
        gam_end = [x[CHUNK - 1:CHUNK, :] for x in gam]
        gdiff = [_mm_mask(lower, jnp.where(strict_w, x, 0.0)) for x in g]
        decay = [jnp.exp(jnp.where(incl_w, x, -jnp.inf)) for x in gdiff]
        a_mat = each(lambda bt, d, x, y: jnp.where(strict_w, bt * d * _mm_nt(x, y), 0.0), beta, decay, k, k_s)
        t_inv = _unit_lower_inverses([-a for a in a_mat], eye_w, same)
        e_gam = [jnp.exp(x) for x in gam]
        u0 = each(lambda t, bt, x: _mm(t, stack(bt * x.astype(F32))), t_inv, beta, v)
        wm = each(lambda t, bt, e, x: _mm(t, stack((bt * e) * x.astype(F32))), t_inv, beta, e_gam, k)
        qk = each(lambda x, y, d: _mm_nt(x, y) * d, q, k_s, decay)
        q2 = each(lambda e, x, a, w: e * x.astype(F32) - _mm(a, stack(w)), e_gam, q, qk, wm)
        o0 = each(lambda a, u: _mm(a, stack(u)), qk, u0)
        k_end = each(lambda x, ge, ga: x.astype(F32) * jnp.exp(ge - ga), k, gam_end, gam)
        g_mat = each(lambda ge, x, w: eye * jnp.exp(ge) - jnp.where(same, _mm_tn(x, w), 0.0), gam_end, k_end, wm)
        s0 = each(lambda x, u: jnp.where(same, _mm_tn(x, u), 0.0), k_end, u0)
        for n, ((b, _), sl) in enumerate(zip(items, sls)):
            s = s_scr[b]
            o = _stack_heads(_mm(q2[n], s) + o0[n], same)
            s_scr[b] = _mm(g_mat[n], s) + s0[n]
            ms = jnp.sum(o * o, axis=-1, keepdims=True) * (1.0 / HEAD_DIM)
            o_ref[b, sl, :] = (_unstack_heads(o * lax.rsqrt(ms + NORM_EPS)) * nw_ref[...]).astype(BF16)
        return carry

    lax.fori_loop(0, nchunk // CHUNK_UNROLL, chunks, 0)


def _gdn_chunk(xs, gb, norm_w, batch, seq, cblk):
    xs = xs.reshape(3, batch, seq, W_MIX)
    gb = gb.reshape(batch, seq, 2 * W_MIX)
    out = pl.pallas_call(
        functools.partial(_gdn_chunk_kernel, nchunk=cblk // CHUNK, nbatch=batch),
        grid=(seq // cblk,),
        in_specs=[
            pl.BlockSpec((3, batch, cblk, W_MIX), lambda j: (0, 0, j, 0)),
            pl.BlockSpec((batch, cblk, 2 * W_MIX), lambda j: (0, j, 0)),
            pl.BlockSpec((1, W_MIX), lambda j: (0, 0)),
        ],
        out_specs=pl.BlockSpec((batch, cblk, W_MIX), lambda j: (0, j, 0)),
        out_shape=jax.ShapeDtypeStruct((batch, seq, W_MIX), BF16),
        scratch_shapes=[pltpu.VMEM((batch, W_MIX, W_MIX), F32)],
        compiler_params=_cparams(("arbitrary",)),
        name="gdn_chunk",
    )(xs, gb, norm_w)
    return out.reshape(batch * seq, W_MIX)


def _causal_pairs(nq):
    pairs = [(i, j) for i in range(nq) for j in range(i + 1)]
    return jnp.asarray(np.array([p[0] for p in pairs], np.int32)), jnp.asarray(np.array([p[1] for p in pairs], np.int32))


def _softmax_updates(scores, vt_ones, m_scr, acc_scr):
    probs = []
    for i, s in enumerate(scores):
        m_old = m_scr[i]
        m_new = jnp.maximum(m_old, jnp.max(s, axis=0, keepdims=True))
        m_scr[i] = m_new
        probs.append((jnp.exp2(m_old - m_new), jnp.exp2((s - m_new).astype(BF16))))
    for i, (alpha, p) in enumerate(probs):
        acc_scr[i] = alpha * acc_scr[i] + jnp.dot(vt_ones[i], p, preferred_element_type=F32)


def _pair_lanes(h):
    p = h // 2
    return slice(p * LANES, (p + 1) * LANES), h % 2 == 0


ACC_ROWS = HEAD_DIM + BF16_ROWS


def _values_and_ones(vt, tk):
    ones = jnp.ones((BF16_ROWS, tk), vt.dtype)
    return [jnp.concatenate([vt[h * HEAD_DIM:(h + 1) * HEAD_DIM, :], ones], axis=0) for h in range(N_HEADS)]


def _normalized(acc):
    return acc[0:HEAD_DIM] / acc[HEAD_DIM:HEAD_DIM + 1]


def _key_after_query(tq):
    return lax.broadcasted_iota(jnp.int32, (tq, tq), 0) > lax.broadcasted_iota(jnp.int32, (tq, tq), 1)


def _fox_kernel(qi_ref, kj_ref, q_ref, k_ref, vt_ref, qb_ref, kb_ref, o_ref, qm_scr, m_scr, acc_scr):
    t = pl.program_id(1)
    qi = qi_ref[t]
    kj = kj_ref[t]
    tq = q_ref.shape[0]

    @pl.when(kj == 0)
    def _():
        m_scr[...] = jnp.full_like(m_scr, -jnp.inf)
        acc_scr[...] = jnp.zeros_like(acc_scr)
        q = q_ref[...]
        qb = qb_ref[...]
        lane = lax.broadcasted_iota(jnp.int32, (tq, LANES), 1)
        zero = jnp.zeros((), BF16)
        for h in range(N_HEADS):
            slab, low = _pair_lanes(h)
            mine = lane < HEAD_DIM if low else lane >= HEAD_DIM
            ck_lanes = jnp.logical_and(lane >= FOX_CK + 3 * h, lane < FOX_CK + 3 * h + 3)
            cq_lanes = jnp.logical_and(lane >= FOX_CQ + 3 * h, lane < FOX_CQ + 3 * h + 3)
            bias = jnp.where(ck_lanes, -jnp.ones((), BF16), jnp.where(cq_lanes, qb, zero))
            qm_scr[h] = jnp.concatenate([jnp.where(mine, q[:, slab], zero), bias], axis=1)

    def step(diagonal):
        k = k_ref[...]
        kb = kb_ref[...]
        vts = _values_and_ones(vt_ref[...], tq)
        if diagonal:
            masked = _key_after_query(tq)
        keys = [jnp.concatenate([k[:, p * LANES:(p + 1) * LANES], kb], axis=1) for p in range(N_HEADS // 2)]
        scores = []
        for h in range(N_HEADS):
            s = lax.dot_general(keys[h // 2], qm_scr[h], (((1,), (1,)), ((), ())), preferred_element_type=F32)
            if diagonal:
                s = jnp.where(masked, -jnp.inf, s)
            scores.append(s)
        _softmax_updates(scores, vts, m_scr, acc_scr)

    @pl.when(kj < qi)
    def _():
        step(False)

    @pl.when(kj == qi)
    def _():
        step(True)
        for p in range(N_HEADS // 2):
            pair = jnp.concatenate([_normalized(acc_scr[2 * p]), _normalized(acc_scr[2 * p + 1])], axis=0)
            o_ref[:, p * LANES:(p + 1) * LANES] = pair.T.astype(BF16)


def _attn_specs(nq, tq, ucol, vt_rows):
    cb = ucol // W_MIX
    q_spec = pl.BlockSpec((tq, W_MIX), lambda b, t, qi, kj: (b * nq + qi[t], cb))
    k_spec = pl.BlockSpec((tq, W_MIX), lambda b, t, qi, kj: (b * nq + kj[t], cb + 1))
    vt_spec = pl.BlockSpec((W_MIX, tq), lambda b, t, qi, kj: (vt_rows // W_MIX, b * nq + kj[t]))
    return q_spec, k_spec, vt_spec


def _fox_attention(u, vt, qb, kb, batch, seq, tq):
    nq = seq // tq
    qi, kj = _causal_pairs(nq)
    q_spec, k_spec, vt_spec = _attn_specs(nq, tq, U_FOX, W_MIX)
    stat = lambda: pltpu.VMEM((N_HEADS, 1, tq), F32)
    return pl.pallas_call(
        _fox_kernel,
        grid_spec=pltpu.PrefetchScalarGridSpec(
            num_scalar_prefetch=2,
            grid=(batch, qi.shape[0]),
            in_specs=[
                q_spec, k_spec, vt_spec,
                pl.BlockSpec((tq, LANES), lambda b, t, qi, kj: (b * nq + qi[t], 0)),
                pl.BlockSpec((tq, LANES), lambda b, t, qi, kj: (b * nq + kj[t], 0)),
            ],
            out_specs=pl.BlockSpec((tq, W_MIX), lambda b, t, qi, kj: (b * nq + qi[t], 0)),
            scratch_shapes=[pltpu.VMEM((N_HEADS, tq, 2 * LANES), BF16), stat(),
                            pltpu.VMEM((N_HEADS, ACC_ROWS, tq), F32)],
        ),
        out_shape=jax.ShapeDtypeStruct((batch * seq, W_MIX), BF16),
        compiler_params=_cparams(("parallel", "arbitrary")),
        name="fox_attention",
    )(qi, kj, u, u, vt, qb, kb)


def _diff_kernel(qi_ref, kj_ref, q_ref, k_ref, vt_ref, cq_ref, sq_ref, ck_ref, sk_ref, rot_ref, lam_ref, ln_ref, o_ref,
                 qm_scr, m_scr, acc_scr, krot_scr, *, lam_init):
    t = pl.program_id(1)
    qi = qi_ref[t]
    kj = kj_ref[t]
    tq = q_ref.shape[0]
    lane = lax.broadcasted_iota(jnp.int32, (tq, LANES), 1)
    key_rows = pl.ds(pl.multiple_of(kj * tq, tq), tq)

    def rope(x, cos, sin):
        return x.astype(F32) * cos + jnp.dot(x, rot_ref[...], preferred_element_type=F32) * sin

    @pl.when(kj == 0)
    def _():
        m_scr[...] = jnp.full_like(m_scr, -jnp.inf)
        acc_scr[...] = jnp.zeros_like(acc_scr)
        q = rope(q_ref[...], cq_ref[...], sq_ref[...])
        for h in range(N_HEADS):
            slab, low = _pair_lanes(h)
            base = 0 if low else HEAD_DIM
            for c in range(2):
                lo = base + c * DIFF_DH
                sel = jnp.logical_and(lane >= lo, lane < lo + DIFF_DH)
                qm_scr[2 * h + c] = jnp.where(sel, q[:, slab], 0.0).astype(BF16)

    def step(diagonal):
        if diagonal:
            k = rope(k_ref[...], ck_ref[...], sk_ref[...]).astype(BF16)
            krot_scr[key_rows, :] = k
        else:
            k = krot_scr[key_rows, :]
        vts = _values_and_ones(vt_ref[...], tq)
        if diagonal:
            masked = _key_after_query(tq)
        scores = []
        for h in range(N_HEADS):
            slab, _ = _pair_lanes(h)
            for c in range(2):
                i = 2 * h + c
                s = lax.dot_general(k[:, slab], qm_scr[i], (((1,), (1,)), ((), ())), preferred_element_type=F32)
                if diagonal:
                    s = jnp.where(masked, -jnp.inf, s)
                scores.append(s)
        _softmax_updates(scores, [vts[i // 2] for i in range(2 * N_HEADS)], m_scr, acc_scr)

    @pl.when(kj < qi)
    def _():
        step(False)

    @pl.when(kj == qi)
    def _():
        step(True)
        lp = lam_ref[...]
        lam = (jnp.exp(jnp.sum(lp[0:1] * lp[1:2], axis=-1, keepdims=True))
               - jnp.exp(jnp.sum(lp[2:3] * lp[3:4], axis=-1, keepdims=True)) + lam_init)
        head = lambda h: _normalized(acc_scr[2 * h]) - lam * _normalized(acc_scr[2 * h + 1])
        is_lo = lane < HEAD_DIM
        for p in range(N_HEADS // 2):
            o = jnp.concatenate([head(2 * p), head(2 * p + 1)], axis=0).T
            sq = o * o
            ms_lo = jnp.sum(jnp.where(is_lo, sq, 0.0), axis=-1, keepdims=True)
            ms_hi = jnp.sum(jnp.where(is_lo, 0.0, sq), axis=-1, keepdims=True)
            ms = jnp.where(is_lo, ms_lo, ms_hi) * (1.0 / HEAD_DIM)
            o_ref[:, p * LANES:(p + 1) * LANES] = (o * lax.rsqrt(ms + DIFF_LN_EPS) * ln_ref[...]
                                                   * (1.0 - lam_init)).astype(BF16)


def _diff_attention(u, vt, cos, sin, rot, lam_p, subln, batch, seq, tq, lam_init):
    nq = seq // tq
    qi, kj = _causal_pairs(nq)
    q_spec, k_spec, vt_spec = _attn_specs(nq, tq, U_DIFF, 0)
    tab_q = pl.BlockSpec((tq, W_MIX), lambda b, t, qi, kj: (qi[t], 0))
    tab_k = pl.BlockSpec((tq, W_MIX), lambda b, t, qi, kj: (kj[t], 0))
    const = lambda r, c: pl.BlockSpec((r, c), lambda b, t, qi, kj: (0, 0))
    stat = lambda: pltpu.VMEM((2 * N_HEADS, 1, tq), F32)
    return pl.pallas_call(
        functools.partial(_diff_kernel, lam_init=lam_init),
        grid_spec=pltpu.PrefetchScalarGridSpec(
            num_scalar_prefetch=2,
            grid=(batch, qi.shape[0]),
            in_specs=[q_spec, k_spec, vt_spec, tab_q, tab_q, tab_k, tab_k,
                      const(W_MIX, W_MIX), const(4, DIFF_DH), const(1, LANES)],
            out_specs=pl.BlockSpec((tq, W_MIX), lambda b, t, qi, kj: (b * nq + qi[t], 0)),
            scratch_shapes=[pltpu.VMEM((2 * N_HEADS, tq, LANES), BF16), stat(),
                            pltpu.VMEM((2 * N_HEADS, ACC_ROWS, tq), F32), pltpu.VMEM((seq, W_MIX), BF16)],
        ),
        out_shape=jax.ShapeDtypeStruct((batch * seq, W_MIX), BF16),
        compiler_params=_cparams(("arbitrary", "arbitrary")),
        name="diff_attention",
    )(qi, kj, u, u, vt, cos, sin, cos, sin, rot, lam_p, subln)


def _merge_kernel(x_ref, gate_ref, oa_ref, post_ref, yb_ref, yc_ref, od_ref, gd_ref, lnw_ref, lnb_ref,
                  wbo_ref, wout_ref, o_ref):
    y_a = (oa_ref[...].astype(F32) * lnw_ref[...] + lnb_ref[...] + post_ref[1].astype(F32)) * post_ref[0].astype(F32)
    y_d = od_ref[...].astype(F32) * _silu(gd_ref[...].astype(F32))
    acc = jnp.zeros(x_ref.shape, F32)
    for b, y in enumerate((y_a, yb_ref[...], yc_ref[...], y_d)):
        gate = _sigmoid(gate_ref[:, b * D_MODEL:(b + 1) * D_MODEL].astype(F32))
        acc = acc + gate * _mm(y, wbo_ref[b])
    o_ref[...] = x_ref[...] + _mm(acc, wout_ref[...])


def _merge(x, u, o_a, post, y_b, y_c, o_d, ln_w, ln_b, w_bo, w_out, tm):
    n = x.shape[0]
    tok = lambda c: pl.BlockSpec((tm, c), lambda i: (i, 0))
    return pl.pallas_call(
        _merge_kernel,
        grid=(n // tm,),
        in_specs=[
            tok(D_MODEL),
            pl.BlockSpec((tm, 4 * D_MODEL), lambda i: (i, U_GATE // (4 * D_MODEL))),
            tok(W_MIX),
            pl.BlockSpec((2, tm, W_MIX), lambda i: (0, i, 0)),
            tok(W_MIX), tok(W_MIX), tok(W_MIX),
            pl.BlockSpec((tm, W_MIX), lambda i: (i, (U_GDN + 3 * W_MIX) // W_MIX)),
            pl.BlockSpec((1, W_MIX), lambda i: (0, 0)),
            pl.BlockSpec((1, W_MIX), lambda i: (0, 0)),
            pl.BlockSpec((4, W_MIX, D_MODEL), lambda i: (0, 0, 0)),
            pl.BlockSpec((D_MODEL, D_MODEL), lambda i: (0, 0)),
        ],
        out_specs=tok(D_MODEL),
        out_shape=jax.ShapeDtypeStruct((n, D_MODEL), F32),
        compiler_params=_cparams(("parallel",)),
        name="merge",
    )(x, u, o_a, post, y_b, y_c, o_d, u, ln_w, ln_b, w_bo, w_out)


def _ffn_kernel(x_ref, g_ref, wg_ref, wu_ref, wd_ref, o_ref):
    x = x_ref[...]
    h = _rms(x, g_ref[...], NORM_EPS).astype(BF16)
    act = _silu(jnp.dot(h, wg_ref[...], preferred_element_type=F32)) * jnp.dot(h, wu_ref[...], preferred_element_type=F32)
    o_ref[...] = x + _mm(act, wd_ref[...])


def _ffn(x, g, wg, wu, wd, tm):
    n = x.shape[0]
    ff = wg.shape[1]
    resident = lambda r, c: pl.BlockSpec((r, c), lambda i: (0, 0), pipeline_mode=pl.Buffered(1))
    return pl.pallas_call(
        _ffn_kernel,
        grid=(n // tm,),
        in_specs=[
            pl.BlockSpec((tm, D_MODEL), lambda i: (i, 0)),
            pl.BlockSpec((1, D_MODEL), lambda i: (0, 0)),
            resident(D_MODEL, ff), resident(D_MODEL, ff), resident(ff, D_MODEL),
        ],
        out_specs=pl.BlockSpec((tm, D_MODEL), lambda i: (i, 0)),
        out_shape=jax.ShapeDtypeStruct((n, D_MODEL), F32),
        compiler_params=_cparams(("parallel",), vmem_mb=VMEM_LARGE_MB),
        name="ffn",
    )(x, g, wg, wu, wd)


def _router_kernel(x_ref, g_ref, router_ref, h_ref, c_ref, rc_ref, rr_ref, cnt_ref):
    t = x_ref.shape[0]
    h = _rms(x_ref[...], g_ref[...], NORM_EPS)
    h_ref[...] = h.astype(BF16)
    logits = _mm_split(h, router_ref[...])
    lane = lax.broadcasted_iota(jnp.int32, logits.shape, 1).astype(F32)
    lg = jnp.where(lane < N_EXPERTS, logits, -jnp.inf)
    m1 = jnp.max(lg, axis=-1, keepdims=True)
    i1 = jnp.min(jnp.where(lg == m1, lane, float(LANES)), axis=-1, keepdims=True)
    lg2 = jnp.where(lane == i1, -jnp.inf, lg)
    m2 = jnp.max(lg2, axis=-1, keepdims=True)
    i2 = jnp.min(jnp.where(lg2 == m2, lane, float(LANES)), axis=-1, keepdims=True)
    e2 = jnp.exp(m2 - m1)
    c_ref[...] = jnp.where(lane == i1, 1.0 / (1.0 + e2), 0.0) + jnp.where(lane == i2, e2 / (1.0 + e2), 0.0)
    sel = jnp.logical_or(lane == i1, lane == i2)
    sel_f = jnp.where(sel, 1.0, 0.0)
    earlier = _tri_masks(t)[0].astype(BF16)
    rank = jnp.dot(earlier, sel_f.astype(BF16), preferred_element_type=F32)
    rc = jnp.where(sel, rank, -1.0)
    rc_ref[...] = rc
    rr_ref[...] = rc.T[0:BF16_ROWS, :]
    cnt_ref[...] = jnp.broadcast_to(jnp.sum(sel_f, axis=0, keepdims=True), cnt_ref.shape).astype(jnp.int32)


def _router(x, g, router, tm):
    n = x.shape[0]
    return pl.pallas_call(
        _router_kernel,
        grid=(n // tm,),
        in_specs=[
            pl.BlockSpec((tm, D_MODEL), lambda i: (i, 0)),
            pl.BlockSpec((1, D_MODEL), lambda i: (0, 0)),
            pl.BlockSpec((D_MODEL, LANES), lambda i: (0, 0)),
        ],
        out_specs=[
            pl.BlockSpec((tm, D_MODEL), lambda i: (i, 0)),
            pl.BlockSpec((tm, LANES), lambda i: (i, 0)),
            pl.BlockSpec((tm, LANES), lambda i: (i, 0)),
            pl.BlockSpec((BF16_ROWS, tm), lambda i: (0, i)),
            pl.BlockSpec((SUBLANES, LANES), lambda i: (i, 0)),
        ],
        out_shape=[
            jax.ShapeDtypeStruct((n, D_MODEL), BF16),
            jax.ShapeDtypeStruct((n, LANES), F32),
            jax.ShapeDtypeStruct((n, LANES), F32),
            jax.ShapeDtypeStruct((BF16_ROWS, n), F32),
            jax.ShapeDtypeStruct((n // tm * SUBLANES, LANES), jnp.int32),
        ],
        compiler_params=_cparams(("parallel",)),
        name="router",
    )(x, g, router)


def _moe_kernel(cnt_ref, x_ref, h_ref, c_ref, rc_ref, rr_ref, wg_ref, wu_ref, wd_ref, o_ref, xg_scr, yg_scr, *, rows):
    i = pl.program_id(0)
    e = pl.program_id(1)
    f = pl.program_id(2)
    last_f = f == pl.num_programs(2) - 1
    t = h_ref.shape[0]
    nsub = xg_scr.shape[0] // rows
    cnt = cnt_ref[i * N_EXPERTS + e]
    blocks = [(sb, slice(sb * rows, (sb + 1) * rows)) for sb in range(nsub)]

    @pl.when(jnp.logical_and(e == 0, f == 0))
    def _():
        o_ref[...] = x_ref[...]

    @pl.when(f == 0)
    def _():
        rr = rr_ref[pl.ds(e, 1), :]
        for sb, rs in blocks:
            @pl.when(cnt > sb * rows)
            def _():
                slot = (lax.broadcasted_iota(jnp.int32, (rows, t), 0) + sb * rows).astype(F32)
                gather = jnp.where(rr == slot, 1.0, 0.0).astype(BF16)
                xg_scr[rs, :] = jnp.dot(gather, h_ref[...], preferred_element_type=F32).astype(BF16)
                yg_scr[rs, :] = jnp.zeros((rows, D_MODEL), F32)

    for sb, rs in blocks:
        @pl.when(cnt > sb * rows)
        def _():
            xb = xg_scr[rs, :]
            act = (_silu(jnp.dot(xb, wg_ref[0], preferred_element_type=F32))
                   * jnp.dot(xb, wu_ref[0], preferred_element_type=F32))
            yg_scr[rs, :] += _mm(act, wd_ref[0])

    @pl.when(last_f)
    def _():
        lane = lax.broadcasted_iota(jnp.int32, (t, LANES), 1)
        mine = lane == e
        rc = jnp.sum(jnp.where(mine, rc_ref[...], 0.0), axis=-1, keepdims=True)
        ce = jnp.sum(jnp.where(mine, c_ref[...], 0.0), axis=-1, keepdims=True)
        for sb, rs in blocks:
            @pl.when(cnt > sb * rows)
            def _():
                slot = (lax.broadcasted_iota(jnp.int32, (t, rows), 1) + sb * rows).astype(F32)
                scatter = jnp.where(rc == slot, 1.0, 0.0).astype(BF16)
                o_ref[...] += ce * jnp.dot(scatter, yg_scr[rs, :].astype(BF16), preferred_element_type=F32)


def _moe(x, g, router, wg, wu, wd, tm, tf, rows):
    n = x.shape[0]
    ff = wg.shape[2]
    h, c, rc, rr, cnt = _router(x, g, router, tm)
    cnt = cnt.reshape(n // tm, SUBLANES, LANES)[:, 0, :N_EXPERTS].reshape(-1)
    nsub = -(-tm // rows)
    tok = lambda cols, **kw: pl.BlockSpec((tm, cols), lambda i, e, f, cnt: (i, 0), **kw)
    once = dict(pipeline_mode=pl.Buffered(1))
    return pl.pallas_call(
        functools.partial(_moe_kernel, rows=rows),
        grid_spec=pltpu.PrefetchScalarGridSpec(
            num_scalar_prefetch=1,
            grid=(n // tm, N_EXPERTS, ff // tf),
            in_specs=[
                tok(D_MODEL, **once), tok(D_MODEL, **once), tok(LANES), tok(LANES),
                pl.BlockSpec((BF16_ROWS, tm), lambda i, e, f, cnt: (0, i)),
                pl.BlockSpec((1, D_MODEL, tf), lambda i, e, f, cnt: (e, 0, f)),
                pl.BlockSpec((1, D_MODEL, tf), lambda i, e, f, cnt: (e, 0, f)),
                pl.BlockSpec((1, tf, D_MODEL), lambda i, e, f, cnt: (e, f, 0)),
            ],
            out_specs=tok(D_MODEL),
            scratch_shapes=[pltpu.VMEM((nsub * rows, D_MODEL), BF16), pltpu.VMEM((nsub * rows, D_MODEL), F32)],
        ),
        out_shape=jax.ShapeDtypeStruct((n, D_MODEL), F32),
        compiler_params=_cparams(("parallel", "arbitrary", "arbitrary"), vmem_mb=VMEM_LARGE_MB),
        name="moe",
    )(cnt, x, h, c, rc, rr, wg, wu, wd)


def _ple_kernel(x_ref, p_ref, g_ref, wgate_ref, wproj_ref, fin_ref, o_ref, *, final):
    x = x_ref[...]
    h = _rms(x, g_ref[...], NORM_EPS)
    y = x + _sigmoid(_mm(h, wgate_ref[...])) * _mm(p_ref[...], wproj_ref[...])
    o_ref[...] = _rms(y, fin_ref[...], NORM_EPS) if final else y


def _ple(x, p, g, wgate, wproj, fin, tm, final):
    n = x.shape[0]
    return pl.pallas_call(
        functools.partial(_ple_kernel, final=final),
        grid=(n // tm,),
        in_specs=[
            pl.BlockSpec((tm, D_MODEL), lambda i: (i, 0)),
            pl.BlockSpec((tm, P_DIM), lambda i: (i, 0)),
            pl.BlockSpec((1, D_MODEL), lambda i: (0, 0)),
            pl.BlockSpec((D_MODEL, D_MODEL), lambda i: (0, 0)),
            pl.BlockSpec((P_DIM, D_MODEL), lambda i: (0, 0)),
            pl.BlockSpec((1, D_MODEL), lambda i: (0, 0)),
        ],
        out_specs=pl.BlockSpec((tm, D_MODEL), lambda i: (i, 0)),
        out_shape=jax.ShapeDtypeStruct((n, D_MODEL), F32),
        compiler_params=_cparams(("parallel",)),
        name="ple",
    )(x, p, g, wgate, wproj, fin)


def _tiles(n, seq):
    tm = min(512, seq)
    tm_big = 1024 if n % 1024 == 0 else tm
    tm_in = 2048 if n % 2048 == 0 else tm_big
    moe_rows = tm_big // 4 + tm_big // 32
    return dict(tm=tm, tm_big=tm_big, tm_in=tm_in, tn_in=1024, tq=min(512, seq), cblk=min(512, seq), moe_rows=moe_rows)


def _rope_tables(seq):
    half = ROPE_DIMS // 2
    inv = ROPE_THETA ** (-jnp.arange(half, dtype=F32) * 2.0 / ROPE_DIMS)
    ang = jnp.arange(seq, dtype=F32)[:, None] * inv[None, :]
    pad = jnp.zeros((seq, DIFF_DH - ROPE_DIMS), F32)
    cos = jnp.concatenate([jnp.cos(ang), jnp.cos(ang), pad + 1.0], axis=-1)
    sin = jnp.concatenate([jnp.sin(ang), jnp.sin(ang), pad], axis=-1)
    d = jnp.arange(W_MIX)
    dd = d % DIFF_DH
    src = jnp.where(dd < half, d + half, d - half)
    sign = jnp.where(dd < half, -1.0, jnp.where(dd < ROPE_DIMS, 1.0, 0.0))
    rot = jnp.zeros((W_MIX, W_MIX), F32).at[src, d].set(sign)
    reps = W_MIX // DIFF_DH
    return jnp.tile(cos, (1, reps)), jnp.tile(sin, (1, reps)), rot.astype(BF16)


def _split_w_in(w):
    a0 = 0
    b0 = a0 + 4 * W_MIX
    c0 = b0 + 3 * W_MIX
    d0 = c0 + 3 * W_MIX + N_HEADS
    g0 = d0 + 4 * W_MIX + 2 * N_HEADS
    d_small = d0 + 3 * W_MIX
    diff_q = w[:, b0:b0 + W_MIX] * (DIFF_DH ** -0.5 * LOG2E)
    fox_q = w[:, c0:c0 + W_MIX] * (HEAD_DIM ** -0.5 * LOG2E)
    main = jnp.concatenate([
        w[:, g0:], w[:, a0:b0], w[:, d0:d_small], w[:, d_small + 2 * N_HEADS:g0],
        diff_q, w[:, b0 + W_MIX:b0 + 2 * W_MIX], fox_q, w[:, c0 + W_MIX:c0 + 2 * W_MIX],
    ], axis=1).astype(BF16)
    small = jnp.concatenate([
        w[:, c0 + 3 * W_MIX:d0], w[:, d_small:d_small + 2 * N_HEADS],
        jnp.zeros((D_MODEL, LANES - 3 * N_HEADS), w.dtype),
    ], axis=1).astype(BF16)
    v_t = jnp.concatenate([w[:, b0 + 2 * W_MIX:c0], w[:, c0 + 2 * W_MIX:c0 + 3 * W_MIX]], axis=1).T.astype(BF16)
    return main, small, v_t


def _small_params(fbias, a_log, dt_bias):
    zeros = jnp.zeros((N_HEADS,), F32)
    bias = jnp.concatenate([fbias, zeros, dt_bias, jnp.zeros((LANES - 3 * N_HEADS,), F32)])
    neg_a = jnp.concatenate([zeros, zeros, -jnp.exp(a_log), jnp.zeros((LANES - 3 * N_HEADS,), F32)])
    return bias.reshape(1, LANES), neg_a.reshape(1, LANES)


def _pad_rows(w, top, total):
    return jnp.concatenate([jnp.zeros((top, w.shape[1]), w.dtype), w,
                            jnp.zeros((total - top - w.shape[0], w.shape[1]), w.dtype)], axis=0)


def kernel(x, p, norm_mix, norm_ffn, norm_ple, w_in, w_bo, w_out, rwkv_mu, rwkv_w0, rwkv_w2, rwkv_a0, rwkv_a2, rwkv_g2, rwkv_kk, rwkv_ka, rwkv_rk, rwkv_ln_w, rwkv_ln_b, diff_lam, diff_subln, fox_fbias, gdn_conv, gdn_a_log, gdn_dt_bias, gdn_norm, ffn_w_gate, ffn_w_up, ffn_w_down, moe_router, moe_w_gate, moe_w_up, moe_w_down, ple_proj, ple_gate, final_norm):
    batch, seq, _ = x.shape
    depth = w_in.shape[0]
    n = batch * seq
    t = _tiles(n, seq)
    tm, tq, cblk = t["tm"], t["tq"], t["cblk"]
    row = lambda v: v.reshape(1, -1).astype(F32)
    cos, sin, rot = _rope_tables(seq)
    xf = x.reshape(n, D_MODEL)
    pf = p.reshape(depth, n, P_DIM)

    for i in range(depth):
        w_main, w_small, w_vt = _split_w_in(w_in[i])
        u, scol, vt = _inproj(xf, row(norm_mix[i]), w_main, w_small, w_vt, t["tm_in"], t["tn_in"])
        bias, neg_a = _small_params(fox_fbias[i], gdn_a_log[i], gdn_dt_bias[i])
        hcol, fox_kb, fox_qb = _small_prep(scol, bias, neg_a, batch, seq, tm)

        scan_in, pc, post = _rwkv_prep(
            u, row(rwkv_mu[i]), row(rwkv_w0[i]), _pad_rows(rwkv_w2[i], 0, LANES), row(rwkv_a0[i]),
            _pad_rows(rwkv_a2[i], LANES // 2, LANES), rwkv_g2[i], row(rwkv_kk[i]), row(rwkv_ka[i]), row(rwkv_rk[i]),
            seq, tm)
        o_a = _rwkv_chunk(scan_in, pc, batch, seq, cblk)

        o_d = _gdn_chunk(_gdn_prep(u, gdn_conv[i].T, seq, tm), hcol, jnp.tile(row(gdn_norm[i]), (1, N_HEADS)),
                         batch, seq, cblk)

        lam_init = 0.8 - 0.6 * math.exp(-0.3 * i)
        y_b = _diff_attention(u, vt, cos, sin, rot, diff_lam[i].astype(F32), jnp.tile(row(diff_subln[i]), (1, 2)),
                              batch, seq, tq, lam_init)
        y_c = _fox_attention(u, vt, fox_qb, fox_kb, batch, seq, tq)

        xf = _merge(xf, u, o_a, post, y_b, y_c, o_d, row(rwkv_ln_w[i]), row(rwkv_ln_b[i]),
                    w_bo[i].astype(BF16), w_out[i].astype(BF16), tm)

        j = i // 2
        if i % 2 == 0:
            xf = _ffn(xf, row(norm_ffn[i]), ffn_w_gate[j].astype(BF16), ffn_w_up[j].astype(BF16),
                      ffn_w_down[j].astype(BF16), tm)
        else:
            router = jnp.concatenate([moe_router[j], jnp.zeros((D_MODEL, LANES - N_EXPERTS), F32)], axis=1)
            xf = _moe(xf, row(norm_ffn[i]), router, moe_w_gate[j].astype(BF16), moe_w_up[j].astype(BF16),
                      moe_w_down[j].astype(BF16), t["tm_big"], moe_w_gate.shape[3] // 2, t["moe_rows"])
        xf = _ple(xf, pf[i], row(norm_ple[i]), ple_gate[i].astype(BF16), ple_proj[i].astype(BF16),
                  row(final_norm), tm, i == depth - 1)
    return xf.reshape(batch, seq, D_MODEL)
```

```python
import functools
import math

import jax
import jax.numpy as jnp
import numpy as np
from jax import lax
from jax.experimental import pallas as pl
from jax.experimental.pallas import tpu as pltpu

F32 = jnp.float32
BF16 = jnp.bfloat16

D_MODEL = 1024
P_DIM = 256
W_MIX = 256
HEAD_DIM = 64
N_HEADS = 4
DIFF_DH = 32
ROPE_THETA = 500000.0
ROPE_DIMS = 8
RWKV_GN_EPS = 64e-5
DIFF_LN_EPS = 1e-5
GDN_CONV = 4
CHUNK = 64
CHUNK_UNROLL = 8
N_EXPERTS = 8
NORM_EPS = 1e-6
L2_EPS = 1e-6
LOG2E = math.log2(math.e)
LANES = 128
SUBLANES = 8
BF16_ROWS = 16
VMEM_MB = 48
VMEM_LARGE_MB = 56
assert CHUNK == HEAD_DIM

U_GATE = 0
U_RWKV = 4096
U_GDN = 5120
U_DIFF = 6144
U_FOX = 6656
U_COLS = 7168
SM_FOX, SM_BETA, SM_DEC = 0, 4, 8
FOX_CK, FOX_CQ = 0, 16


def _cparams(semantics, vmem_mb=VMEM_MB):
    return pltpu.CompilerParams(dimension_semantics=semantics, vmem_limit_bytes=vmem_mb * 1024 * 1024)


def _mm(a, b):
    return jnp.dot(a.astype(BF16), b.astype(BF16), preferred_element_type=F32)


def _mm_nt(a, b):
    return lax.dot_general(a.astype(BF16), b.astype(BF16), (((1,), (1,)), ((), ())), preferred_element_type=F32)


def _mm_tn(a, b):
    return lax.dot_general(a.astype(BF16), b.astype(BF16), (((0,), (0,)), ((), ())), preferred_element_type=F32)


def _mm_split(a, b):
    ah = a.astype(BF16)
    al = (a - ah.astype(F32)).astype(BF16)
    bh = b.astype(BF16)
    bl = (b - bh.astype(F32)).astype(BF16)
    dot = lambda x, y: jnp.dot(x, y, preferred_element_type=F32)
    return dot(ah, bh) + (dot(ah, bl) + dot(al, bh))


def _mm_mask(mask, x):
    hi = x.astype(BF16)
    r1 = x - hi.astype(F32)
    mid = r1.astype(BF16)
    lo = (r1 - mid.astype(F32)).astype(BF16)
    dot = lambda t: jnp.dot(mask, t, preferred_element_type=F32)
    return dot(hi) + dot(mid) + dot(lo)


def _head_sums(x, same):
    hi = x.astype(BF16)
    lo = (x - hi.astype(F32)).astype(BF16)
    return jnp.dot(hi, same, preferred_element_type=F32) + jnp.dot(lo, same, preferred_element_type=F32)


def _rms(x, g, eps):
    return x * lax.rsqrt(jnp.mean(x * x, axis=-1, keepdims=True) + eps) * g


def _sigmoid(x):
    return 0.5 * jnp.tanh(0.5 * x) + 0.5


def _silu(x):
    return x * _sigmoid(x)


def _softplus(x):
    return jnp.maximum(x, 0.0) + jnp.log(1.0 + jnp.exp(-jnp.abs(x)))


def _tri_masks(c):
    ii = lax.broadcasted_iota(jnp.int32, (c, c), 0)
    jj = lax.broadcasted_iota(jnp.int32, (c, c), 1)
    return ii > jj, ii >= jj, ii == jj


def _block_masks():
    ii = lax.broadcasted_iota(jnp.int32, (W_MIX, W_MIX), 0)
    jj = lax.broadcasted_iota(jnp.int32, (W_MIX, W_MIX), 1)
    return (ii // HEAD_DIM) == (jj // HEAD_DIM), ii == jj


def _wide_masks():
    ii = lax.broadcasted_iota(jnp.int32, (CHUNK, W_MIX), 0)
    jj = lax.broadcasted_iota(jnp.int32, (CHUNK, W_MIX), 1) % HEAD_DIM
    return ii > jj, ii >= jj, ii == jj


def _stack_heads(x, same):
    return jnp.where(same, jnp.concatenate([x, x, x, x], axis=0), jnp.zeros((), x.dtype))


def _unit_lower_inverses(ns, eye_w, same):
    rs = [eye_w + n for n in ns]
    ps = list(ns)
    for _ in range(int(math.log2(CHUNK)) - 1):
        ps = [_mm(p, _stack_heads(p.astype(BF16), same)) for p in ps]
        rs = [r + _mm(r, _stack_heads(p.astype(BF16), same)) for r, p in zip(rs, ps)]
    return rs


def _unstack_heads(x):
    return x[0:CHUNK] + x[CHUNK:2 * CHUNK] + x[2 * CHUNK:3 * CHUNK] + x[3 * CHUNK:4 * CHUNK]


def _inproj_kernel(x_ref, g_ref, w_ref, ws_ref, wvt_ref, u_ref, scol_ref, vt_ref, h_scr):
    @pl.when(pl.program_id(1) == 0)
    def _():
        hb = _rms(x_ref[...], g_ref[...], NORM_EPS).astype(BF16)
        h_scr[...] = hb
        scol_ref[...] = jnp.dot(hb, ws_ref[...], preferred_element_type=F32)
        vt_ref[...] = lax.dot_general(wvt_ref[...], hb, (((1,), (1,)), ((), ())),
                                      preferred_element_type=F32).astype(BF16)

    u_ref[...] = jnp.dot(h_scr[...], w_ref[...], preferred_element_type=F32).astype(BF16)


def _inproj(x, g, w, ws, wvt, tm, tn):
    n = x.shape[0]
    return pl.pallas_call(
        _inproj_kernel,
        grid=(n // tm, U_COLS // tn),
        in_specs=[
            pl.BlockSpec((tm, D_MODEL), lambda i, j: (i, 0)),
            pl.BlockSpec((1, D_MODEL), lambda i, j: (0, 0)),
            pl.BlockSpec((D_MODEL, tn), lambda i, j: (0, j)),
            pl.BlockSpec((D_MODEL, LANES), lambda i, j: (0, 0)),
            pl.BlockSpec((2 * W_MIX, D_MODEL), lambda i, j: (0, 0)),
        ],
        out_specs=[
            pl.BlockSpec((tm, tn), lambda i, j: (i, j)),
            pl.BlockSpec((tm, LANES), lambda i, j: (i, 0)),
            pl.BlockSpec((2 * W_MIX, tm), lambda i, j: (0, i)),
        ],
        out_shape=[
            jax.ShapeDtypeStruct((n, U_COLS), BF16),
            jax.ShapeDtypeStruct((n, LANES), F32),
            jax.ShapeDtypeStruct((2 * W_MIX, n), BF16),
        ],
        scratch_shapes=[pltpu.VMEM((tm, D_MODEL), BF16)],
        compiler_params=_cparams(("parallel", "arbitrary"), vmem_mb=VMEM_LARGE_MB),
        name="inproj",
    )(x, g, w, ws, wvt)


def _lane_placement(base):
    r = lax.broadcasted_iota(jnp.int32, (3 * LANES, LANES), 0)
    m = lax.broadcasted_iota(jnp.int32, (3 * LANES, LANES), 1)
    head, part = r % LANES, r // LANES
    return jnp.logical_and(head < N_HEADS, m == base + 3 * head + part).astype(BF16)


def _split3(x):
    hi = x.astype(BF16)
    r1 = x - hi.astype(F32)
    mid = r1.astype(BF16)
    lo = (r1 - mid.astype(F32)).astype(BF16)
    return jnp.concatenate([hi, mid, lo], axis=1)


def _head_expansion():
    r = lax.broadcasted_iota(jnp.int32, (3 * LANES, 2 * W_MIX), 0) % LANES
    m = lax.broadcasted_iota(jnp.int32, (3 * LANES, 2 * W_MIX), 1)
    src = jnp.where(m < W_MIX, SM_BETA + m // HEAD_DIM, SM_DEC + (m - W_MIX) // HEAD_DIM)
    return (r == src).astype(BF16)


def _small_prep_kernel(scol_ref, bias_ref, nega_ref, ogb_ref, okb_ref, oqb_ref, carry_scr):
    @pl.when(pl.program_id(1) == 0)
    def _():
        carry_scr[...] = jnp.zeros_like(carry_scr)

    tm = scol_ref.shape[0]
    lower = _tri_masks(tm)[1].astype(BF16)
    z = scol_ref[...] + bias_ref[...]
    lane = lax.broadcasted_iota(jnp.int32, z.shape, 1)
    is_f = lane < SM_BETA
    is_b = jnp.logical_and(lane >= SM_BETA, lane < SM_DEC)
    is_d = jnp.logical_and(lane >= SM_DEC, lane < SM_DEC + N_HEADS)
    logf = jnp.where(is_f, jnp.minimum(z, 0.0) - jnp.log(1.0 + jnp.exp(-jnp.abs(z))), 0.0)
    gdn = jnp.where(is_b, _sigmoid(z), jnp.where(is_d, nega_ref[...] * _softplus(z), 0.0))
    ogb_ref[...] = jnp.dot(_split3(gdn), _head_expansion(), preferred_element_type=F32)
    cum = _mm_mask(lower, logf) + carry_scr[...]
    carry_scr[...] = cum[tm - 1:tm, :]
    parts = _split3(cum * LOG2E)
    ones_lanes = jnp.logical_and(lane >= FOX_CQ, lane < FOX_CQ + 3 * N_HEADS)
    okb_ref[...] = (jnp.dot(parts, _lane_placement(FOX_CK), preferred_element_type=F32)
                    + jnp.where(ones_lanes, 1.0, 0.0)).astype(BF16)
    oqb_ref[...] = jnp.dot(parts, _lane_placement(FOX_CQ), preferred_element_type=F32).astype(BF16)


def _small_prep(scol, bias, neg_a, batch, seq, tm):
    n = batch * seq
    nt = seq // tm
    tok = lambda: pl.BlockSpec((tm, LANES), lambda b, j: (b * nt + j, 0))
    const = lambda: pl.BlockSpec((1, LANES), lambda b, j: (0, 0))
    return pl.pallas_call(
        _small_prep_kernel,
        grid=(batch, nt),
        in_specs=[tok(), const(), const()],
        out_specs=[pl.BlockSpec((tm, 2 * W_MIX), lambda b, j: (b * nt + j, 0)), tok(), tok()],
        out_shape=[jax.ShapeDtypeStruct((n, 2 * W_MIX), F32), jax.ShapeDtypeStruct((n, LANES), BF16),
                   jax.ShapeDtypeStruct((n, LANES), BF16)],
        scratch_shapes=[pltpu.VMEM((1, LANES), F32)],
        compiler_params=_cparams(("parallel", "arbitrary")),
        name="small_prep",
    )(scol, bias, neg_a)


def _rwkv_prep_kernel(u_ref, up_ref, mu_ref, w0_ref, w2_ref, a0_ref, a2_ref, g2_ref, kk_ref, ka_ref, rk_ref,
                      scan_ref, pc_ref, post_ref, *, tiles_per_seq):
    tm = u_ref.shape[0]
    u = u_ref[...].astype(F32)
    prev = up_ref[...].astype(F32)[BF16_ROWS - 1:BF16_ROWS, :]
    prev = jnp.where(pl.program_id(0) % tiles_per_seq == 0, 0.0, prev)
    rows = lax.broadcasted_iota(jnp.int32, (tm, 1), 0)
    u_prev = jnp.where(rows == 0, prev, pltpu.roll(u, 1, 0))
    xm = u + (u_prev - u) * mu_ref[...]
    r = xm[:, 0:W_MIX]
    k = xm[:, W_MIX:2 * W_MIX]
    v = xm[:, 2 * W_MIX:3 * W_MIX]
    x_lora = xm[:, 3 * W_MIX:3 * W_MIX + LANES]
    xg = xm[:, 3 * W_MIX + LANES:]
    logw = -_softplus(-(w0_ref[...] + _mm(jnp.tanh(x_lora), w2_ref[...]))) - 0.5
    log_decay = -jnp.exp(logw)
    a = _sigmoid(a0_ref[...] + _mm(x_lora, a2_ref[...]))
    g = _mm(_sigmoid(xg), g2_ref[...])
    same = _block_masks()[0].astype(BF16)
    kk_raw = k * kk_ref[...]
    kk = kk_raw * lax.rsqrt(_head_sums(kk_raw * kk_raw, same) + L2_EPS)
    k2 = k * (1.0 + (a - 1.0) * ka_ref[...])
    bonus = _head_sums(r * k2 * rk_ref[...], same) * v
    ti = lax.broadcasted_iota(jnp.int32, (tm, tm), 0)
    tj = lax.broadcasted_iota(jnp.int32, (tm, tm), 1)
    in_chunk = jnp.logical_and(ti // CHUNK == tj // CHUNK, ti >= tj).astype(BF16)
    ci = lax.broadcasted_iota(jnp.int32, (tm // CHUNK, tm), 0)
    cj = lax.broadcasted_iota(jnp.int32, (tm // CHUNK, tm), 1)
    cum = _mm_mask(in_chunk, log_decay)
    cum_end = _mm_mask((ci == cj // CHUNK).astype(BF16), log_decay)
    inv = jnp.exp(-cum)
    scan_ref[0] = (-kk * jnp.exp(cum - log_decay)).astype(BF16)
    scan_ref[1] = (kk * a * inv).astype(BF16)
    scan_ref[2] = (k2 * inv).astype(BF16)
    scan_ref[3] = (r * jnp.exp(cum)).astype(BF16)
    scan_ref[4] = v.astype(BF16)
    pc_ref[...] = jnp.exp(cum_end)
    post_ref[0] = g.astype(BF16)
    post_ref[1] = bonus.astype(BF16)


def _rwkv_prep(u, mu, w0, w2p, a0, a2p, g2, k_k, k_a, r_k, seq, tm):
    n = u.shape[0]
    ublk = U_RWKV // D_MODEL
    row = lambda c: pl.BlockSpec((1, c), lambda i: (0, 0))
    mat = lambda r: pl.BlockSpec((r, W_MIX), lambda i: (0, 0))
    return pl.pallas_call(
        functools.partial(_rwkv_prep_kernel, tiles_per_seq=seq // tm),
        grid=(n // tm,),
        in_specs=[
            pl.BlockSpec((tm, D_MODEL), lambda i: (i, ublk)),
            pl.BlockSpec((BF16_ROWS, D_MODEL), lambda i: (jnp.maximum(i * (tm // BF16_ROWS) - 1, 0), ublk)),
            row(D_MODEL), row(W_MIX), mat(LANES), row(W_MIX), mat(LANES), mat(LANES), row(W_MIX), row(W_MIX), row(W_MIX),
        ],
        out_specs=[
            pl.BlockSpec((5, tm, W_MIX), lambda i: (0, i, 0)),
            pl.BlockSpec((tm // CHUNK, W_MIX), lambda i: (i, 0)),
            pl.BlockSpec((2, tm, W_MIX), lambda i: (0, i, 0)),
        ],
        out_shape=[jax.ShapeDtypeStruct((5, n, W_MIX), BF16), jax.ShapeDtypeStruct((n // CHUNK, W_MIX), F32),
                   jax.ShapeDtypeStruct((2, n, W_MIX), BF16)],
        compiler_params=_cparams(("parallel",)),
        name="rwkv_prep",
    )(u, u, mu, w0, w2p, a0, a2p, g2, k_k, k_a, r_k)


def _rwkv_chunk_kernel(x_ref, pc_ref, o_ref, s_scr, *, nchunk, nbatch):
    @pl.when(pl.program_id(0) == 0)
    def _():
        s_scr[...] = jnp.zeros_like(s_scr)

    same, diag = _block_masks()
    eye = diag.astype(F32)
    strict_w, incl_w, diag_w = _wide_masks()
    eye_w = diag_w.astype(F32)
    stack = lambda x: _stack_heads(x.astype(BF16), same)

    def chunks(i, carry):
        items = [(b, i * CHUNK_UNROLL + j) for j in range(CHUNK_UNROLL) for b in range(nbatch)]
        each = lambda f, *lists: [f(*args) for args in zip(*lists)]
        sls = [pl.ds(pl.multiple_of(ci * CHUNK, CHUNK), CHUNK) for _, ci in items]
        a, bb, k, r, v = ([x_ref[i, b, sl, :] for (b, _), sl in zip(items, sls)] for i in range(5))
        pc = [pc_ref[b, pl.ds(ci, 1), :] for b, ci in items]
        a_s, b_s, k_s, v_s = (each(stack, x) for x in (a, bb, k, v))
        m_ab = each(lambda x, y: jnp.where(strict_w, _mm_nt(x, y), 0.0), a, b_s)
        m_ak = each(lambda x, y: jnp.where(strict_w, _mm_nt(x, y), 0.0), a, k_s)
        n_rb = each(lambda x, y: jnp.where(incl_w, _mm_nt(x, y), 0.0), r, b_s)
        n_rk = each(lambda x, y: jnp.where(incl_w, _mm_nt(x, y), 0.0), r, k_s)
        t_inv = _unit_lower_inverses(m_ab, eye_w, same)
        a2 = each(_mm, t_inv, a_s)
        u0 = each(lambda t, m, x: _mm(t, stack(_mm(m, x))), t_inv, m_ak, v_s)
        r2 = each(lambda x, n, y: x.astype(F32) + _mm(n, stack(y)), r, n_rb, a2)
        o0 = each(lambda n1, u, n2, x: _mm(n1, stack(u)) + _mm(n2, x), n_rb, u0, n_rk, v_s)
        b_end = each(lambda x, p: x.astype(F32) * p, bb, pc)
        k_end = each(lambda x, p: x.astype(F32) * p, k, pc)
        g_mat = each(lambda p, x, y: eye * p + jnp.where(same, _mm_tn(x, y), 0.0), pc, a2, b_end)
        s0 = each(lambda u, x, y, z: jnp.where(same, _mm_tn(u, x) + _mm_tn(y, z), 0.0), u0, b_end, v, k_end)
        for n, ((b, _), sl) in enumerate(zip(items, sls)):
            s = s_scr[b]
            o = _stack_heads(_mm_nt(r2[n], s) + o0[n], same)
            s_scr[b] = _mm(s, g_mat[n]) + s0[n]
            mean = jnp.sum(o, axis=-1, keepdims=True) * (1.0 / HEAD_DIM)
            cen = jnp.where(same, o - mean, 0.0)
            var = jnp.sum(cen * cen, axis=-1, keepdims=True) * (1.0 / HEAD_DIM)
            o_ref[b, sl, :] = _unstack_heads(cen * lax.rsqrt(var + RWKV_GN_EPS)).astype(BF16)
        return carry

    lax.fori_loop(0, nchunk // CHUNK_UNROLL, chunks, 0)


def _rwkv_chunk(xs, pc, batch, seq, cblk):
    xs = xs.reshape(5, batch, seq, W_MIX)
    pc = pc.reshape(batch, seq // CHUNK, W_MIX)
    out = pl.pallas_call(
        functools.partial(_rwkv_chunk_kernel, nchunk=cblk // CHUNK, nbatch=batch),
        grid=(seq // cblk,),
        in_specs=[
            pl.BlockSpec((5, batch, cblk, W_MIX), lambda j: (0, 0, j, 0)),
            pl.BlockSpec((batch, cblk // CHUNK, W_MIX), lambda j: (0, j, 0)),
        ],
        out_specs=pl.BlockSpec((batch, cblk, W_MIX), lambda j: (0, j, 0)),
        out_shape=jax.ShapeDtypeStruct((batch, seq, W_MIX), BF16),
        scratch_shapes=[pltpu.VMEM((batch, W_MIX, W_MIX), F32)],
        compiler_params=_cparams(("arbitrary",)),
        name="rwkv_chunk",
    )(xs, pc)
    return out.reshape(batch * seq, W_MIX)


def _gdn_prep_kernel(u_ref, up_ref, cw_ref, o_ref, ext_scr, *, tiles_per_seq):
    tm = u_ref.shape[0]
    c3 = 3 * W_MIX
    prev = up_ref[...].astype(F32)[:, :c3]
    ext_scr[0:BF16_ROWS, :] = jnp.where(pl.program_id(0) % tiles_per_seq == 0, 0.0, prev)
    ext_scr[BF16_ROWS:, :] = u_ref[...].astype(F32)[:, :c3]
    y = jnp.zeros((tm, c3), F32)
    for j in range(GDN_CONV):
        y = y + ext_scr[pl.ds(BF16_ROWS - (GDN_CONV - 1) + j, tm), :] * cw_ref[j:j + 1, :]
    y = _silu(y)
    same = _block_masks()[0].astype(BF16)
    q = y[:, 0:W_MIX]
    k = y[:, W_MIX:2 * W_MIX]
    o_ref[0] = (q * lax.rsqrt(_head_sums(q * q, same) + L2_EPS) * (HEAD_DIM ** -0.5)).astype(BF16)
    o_ref[1] = (k * lax.rsqrt(_head_sums(k * k, same) + L2_EPS)).astype(BF16)
    o_ref[2] = y[:, 2 * W_MIX:].astype(BF16)


def _gdn_prep(u, conv_w, seq, tm):
    n = u.shape[0]
    ublk = U_GDN // D_MODEL
    return pl.pallas_call(
        functools.partial(_gdn_prep_kernel, tiles_per_seq=seq // tm),
        grid=(n // tm,),
        in_specs=[
            pl.BlockSpec((tm, D_MODEL), lambda i: (i, ublk)),
            pl.BlockSpec((BF16_ROWS, D_MODEL), lambda i: (jnp.maximum(i * (tm // BF16_ROWS) - 1, 0), ublk)),
            pl.BlockSpec((GDN_CONV, 3 * W_MIX), lambda i: (0, 0)),
        ],
        out_specs=pl.BlockSpec((3, tm, W_MIX), lambda i: (0, i, 0)),
        out_shape=jax.ShapeDtypeStruct((3, n, W_MIX), BF16),
        scratch_shapes=[pltpu.VMEM((tm + BF16_ROWS, 3 * W_MIX), F32)],
        compiler_params=_cparams(("parallel",)),
        name="gdn_prep",
    )(u, u, conv_w)


def _gdn_chunk_kernel(x_ref, gb_ref, nw_ref, o_ref, s_scr, *, nchunk, nbatch):
    @pl.when(pl.program_id(0) == 0)
    def _():
        s_scr[...] = jnp.zeros_like(s_scr)

    same, diag = _block_masks()
    eye = diag.astype(F32)
    strict_w, incl_w, diag_w = _wide_masks()
    eye_w = diag_w.astype(F32)
    lower = _tri_masks(CHUNK)[1].astype(BF16)
    stack = lambda x: _stack_heads(x.astype(BF16), same)

    def chunks(i, carry):
        items = [(b, i * CHUNK_UNROLL + j) for j in range(CHUNK_UNROLL) for b in range(nbatch)]
        each = lambda f, *lists: [f(*args) for args in zip(*lists)]
        sls = [pl.ds(pl.multiple_of(ci * CHUNK, CHUNK), CHUNK) for _, ci in items]
        q, k, v = ([x_ref[i, b, sl, :] for (b, _), sl in zip(items, sls)] for i in range(3))
        beta = [gb_ref[b, sl, 0:W_MIX] for (b, _), sl in zip(items, sls)]
        g = [gb_ref[b, sl, W_MIX:2 * W_MIX] for (b, _), sl in zip(items, sls)]
        k_s = each(stack, k)
        gam = [_mm_mask(lower, x) for x in g]
        gam_end = [x[CHUNK - 1:CHUNK, :] for x in gam]
        gdiff = [_mm_mask(lower, jnp.where(strict_w, x, 0.0)) for x in g]
        decay = [jnp.exp(jnp.where(incl_w, x, -jnp.inf)) for x in gdiff]
        a_mat = each(lambda bt, d, x, y: jnp.where(strict_w, bt * d * _mm_nt(x, y), 0.0), beta, decay, k, k_s)
        t_inv = _unit_lower_inverses([-a for a in a_mat], eye_w, same)
        e_gam = [jnp.exp(x) for x in gam]
        u0 = each(lambda t, bt, x: _mm(t, stack(bt * x.astype(F32))), t_inv, beta, v)
        wm = each(lambda t, bt, e, x: _mm(t, stack((bt * e) * x.astype(F32))), t_inv, beta, e_gam, k)
        qk = each(lambda x, y, d: _mm_nt(x, y) * d, q, k_s, decay)
        q2 = each(lambda e, x, a, w: e * x.astype(F32) - _mm(a, stack(w)), e_gam, q, qk, wm)
        o0 = each(lambda a, u: _mm(a, stack(u)), qk, u0)
        k_end = each(lambda x, ge, ga: x.astype(F32) * jnp.exp(ge - ga), k, gam_end, gam)
        g_mat = each(lambda ge, x, w: eye * jnp.exp(ge) - jnp.where(same, _mm_tn(x, w), 0.0), gam_end, k_end, wm)
        s0 = each(lambda x, u: jnp.where(same, _mm_tn(x, u), 0.0), k_end, u0)
        for n, ((b, _), sl) in enumerate(zip(items, sls)):
            s = s_scr[b]
            o = _stack_heads(_mm(q2[n], s) + o0[n], same)
            s_scr[b] = _mm(g_mat[n], s) + s0[n]
            ms = jnp.sum(o * o, axis=-1, keepdims=True) * (1.0 / HEAD_DIM)
            o_ref[b, sl, :] = (_unstack_heads(o * lax.rsqrt(ms + NORM_EPS)) * nw_ref[...]).astype(BF16)
        return carry

    lax.fori_loop(0, nchunk // CHUNK_UNROLL, chunks, 0)


def _gdn_chunk(xs, gb, norm_w, batch, seq, cblk):
    xs = xs.reshape(3, batch, seq, W_MIX)
    gb = gb.reshape(batch, seq, 2 * W_MIX)
    out = pl.pallas_call(
        functools.partial(_gdn_chunk_kernel, nchunk=cblk // CHUNK, nbatch=batch),
        grid=(seq // cblk,),
        in_specs=[
            pl.BlockSpec((3, batch, cblk, W_MIX), lambda j: (0, 0, j, 0)),
            pl.BlockSpec((batch, cblk, 2 * W_MIX), lambda j: (0, j, 0)),
            pl.BlockSpec((1, W_MIX), lambda j: (0, 0)),
        ],
        out_specs=pl.BlockSpec((batch, cblk, W_MIX), lambda j: (0, j, 0)),
        out_shape=jax.ShapeDtypeStruct((batch, seq, W_MIX), BF16),
        scratch_shapes=[pltpu.VMEM((batch, W_MIX, W_MIX), F32)],
        compiler_params=_cparams(("arbitrary",)),
        name="gdn_chunk",
    )(xs, gb, norm_w)
    return out.reshape(batch * seq, W_MIX)


def _causal_pairs(nq):
    pairs = [(i, j) for i in range(nq) for j in range(i + 1)]
    return jnp.asarray(np.array([p[0] for p in pairs], np.int32)), jnp.asarray(np.array([p[1] for p in pairs], np.int32))


def _softmax_updates(scores, vt_ones, m_scr, acc_scr):
    probs = []
    for i, s in enumerate(scores):
        m_old = m_scr[i]
        m_new = jnp.maximum(m_old, jnp.max(s, axis=0, keepdims=True))
        m_scr[i] = m_new
        probs.append((jnp.exp2(m_old - m_new), jnp.exp2((s - m_new).astype(BF16))))
    for i, (alpha, p) in enumerate(probs):
        acc_scr[i] = alpha * acc_scr[i] + jnp.dot(vt_ones[i], p, preferred_element_type=F32)


def _pair_lanes(h):
    p = h // 2
    return slice(p * LANES, (p + 1) * LANES), h % 2 == 0


ACC_ROWS = HEAD_DIM + BF16_ROWS


def _values_and_ones(vt, tk):
    ones = jnp.ones((BF16_ROWS, tk), vt.dtype)
    return [jnp.concatenate([vt[h * HEAD_DIM:(h + 1) * HEAD_DIM, :], ones], axis=0) for h in range(N_HEADS)]


def _normalized(acc):
    return acc[0:HEAD_DIM] / acc[HEAD_DIM:HEAD_DIM + 1]


def _key_after_query(tq):
    return lax.broadcasted_iota(jnp.int32, (tq, tq), 0) > lax.broadcasted_iota(jnp.int32, (tq, tq), 1)


def _fox_kernel(qi_ref, kj_ref, q_ref, k_ref, vt_ref, qb_ref, kb_ref, o_ref, qm_scr, m_scr, acc_scr):
    t = pl.program_id(1)
    qi = qi_ref[t]
    kj = kj_ref[t]
    tq = q_ref.shape[0]

    @pl.when(kj == 0)
    def _():
        m_scr[...] = jnp.full_like(m_scr, -jnp.inf)
        acc_scr[...] = jnp.zeros_like(acc_scr)
        q = q_ref[...]
        qb = qb_ref[...]
        lane = lax.broadcasted_iota(jnp.int32, (tq, LANES), 1)
        zero = jnp.zeros((), BF16)
        for h in range(N_HEADS):
            slab, low = _pair_lanes(h)
            mine = lane < HEAD_DIM if low else lane >= HEAD_DIM
            ck_lanes = jnp.logical_and(lane >= FOX_CK + 3 * h, lane < FOX_CK + 3 * h + 3)
            cq_lanes = jnp.logical_and(lane >= FOX_CQ + 3 * h, lane < FOX_CQ + 3 * h + 3)
            bias = jnp.where(ck_lanes, -jnp.ones((), BF16), jnp.where(cq_lanes, qb, zero))
            qm_scr[h] = jnp.concatenate([jnp.where(mine, q[:, slab], zero), bias], axis=1)

    def step(diagonal):
        k = k_ref[...]
        kb = kb_ref[...]
        vts = _values_and_ones(vt_ref[...], tq)
        if diagonal:
            masked = _key_after_query(tq)
        keys = [jnp.concatenate([k[:, p * LANES:(p + 1) * LANES], kb], axis=1) for p in range(N_HEADS // 2)]
        scores = []
        for h in range(N_HEADS):
            s = lax.dot_general(keys[h // 2], qm_scr[h], (((1,), (1,)), ((), ())), preferred_element_type=F32)
            if diagonal:
                s = jnp.where(masked, -jnp.inf, s)
            scores.append(s)
        _softmax_updates(scores, vts, m_scr, acc_scr)

    @pl.when(kj < qi)
    def _():
        step(False)

    @pl.when(kj == qi)
    def _():
        step(True)
        for p in range(N_HEADS // 2):
            pair = jnp.concatenate([_normalized(acc_scr[2 * p]), _normalized(acc_scr[2 * p + 1])], axis=0)
            o_ref[:, p * LANES:(p + 1) * LANES] = pair.T.astype(BF16)


def _attn_specs(nq, tq, ucol, vt_rows):
    cb = ucol // W_MIX
    q_spec = pl.BlockSpec((tq, W_MIX), lambda b, t, qi, kj: (b * nq + qi[t], cb))
    k_spec = pl.BlockSpec((tq, W_MIX), lambda b, t, qi, kj: (b * nq + kj[t], cb + 1))
    vt_spec = pl.BlockSpec((W_MIX, tq), lambda b, t, qi, kj: (vt_rows // W_MIX, b * nq + kj[t]))
    return q_spec, k_spec, vt_spec


def _fox_attention(u, vt, qb, kb, batch, seq, tq):
    nq = seq // tq
    qi, kj = _causal_pairs(nq)
    q_spec, k_spec, vt_spec = _attn_specs(nq, tq, U_FOX, W_MIX)
    stat = lambda: pltpu.VMEM((N_HEADS, 1, tq), F32)
    return pl.pallas_call(
        _fox_kernel,
        grid_spec=pltpu.PrefetchScalarGridSpec(
            num_scalar_prefetch=2,
            grid=(batch, qi.shape[0]),
            in_specs=[
                q_spec, k_spec, vt_spec,
                pl.BlockSpec((tq, LANES), lambda b, t, qi, kj: (b * nq + qi[t], 0)),
                pl.BlockSpec((tq, LANES), lambda b, t, qi, kj: (b * nq + kj[t], 0)),
            ],
            out_specs=pl.BlockSpec((tq, W_MIX), lambda b, t, qi, kj: (b * nq + qi[t], 0)),
            scratch_shapes=[pltpu.VMEM((N_HEADS, tq, 2 * LANES), BF16), stat(),
                            pltpu.VMEM((N_HEADS, ACC_ROWS, tq), F32)],
        ),
        out_shape=jax.ShapeDtypeStruct((batch * seq, W_MIX), BF16),
        compiler_params=_cparams(("parallel", "arbitrary")),
        name="fox_attention",
    )(qi, kj, u, u, vt, qb, kb)


def _diff_kernel(qi_ref, kj_ref, q_ref, k_ref, vt_ref, cq_ref, sq_ref, ck_ref, sk_ref, rot_ref, lam_ref, ln_ref, o_ref,
                 qm_scr, m_scr, acc_scr, *, lam_init):
    t = pl.program_id(1)
    qi = qi_ref[t]
    kj = kj_ref[t]
    tq = q_ref.shape[0]
    lane = lax.broadcasted_iota(jnp.int32, (tq, LANES), 1)

    def rope(x, cos, sin):
        return x.astype(F32) * cos + jnp.dot(x, rot_ref[...], preferred_element_type=F32) * sin

    @pl.when(kj == 0)
    def _():
        m_scr[...] = jnp.full_like(m_scr, -jnp.inf)
        acc_scr[...] = jnp.zeros_like(acc_scr)
        q = rope(q_ref[...], cq_ref[...], sq_ref[...])
        for h in range(N_HEADS):
            slab, low = _pair_lanes(h)
            base = 0 if low else HEAD_DIM
            for c in range(2):
                lo = base + c * DIFF_DH
                sel = jnp.logical_and(lane >= lo, lane < lo + DIFF_DH)
                qm_scr[2 * h + c] = jnp.where(sel, q[:, slab], 0.0).astype(BF16)

    def step(diagonal):
        k = rope(k_ref[...], ck_ref[...], sk_ref[...]).astype(BF16)
        vts = _values_and_ones(vt_ref[...], tq)
        if diagonal:
            masked = _key_after_query(tq)
        scores = []
        for h in range(N_HEADS):
            slab, _ = _pair_lanes(h)
            for c in range(2):
                i = 2 * h + c
                s = lax.dot_general(k[:, slab], qm_scr[i], (((1,), (1,)), ((), ())), preferred_element_type=F32)
                if diagonal:
                    s = jnp.where(masked, -jnp.inf, s)
                scores.append(s)
        _softmax_updates(scores, [vts[i // 2] for i in range(2 * N_HEADS)], m_scr, acc_scr)

    @pl.when(kj < qi)
    def _():
        step(False)

    @pl.when(kj == qi)
    def _():
        step(True)
        lp = lam_ref[...]
        lam = (jnp.exp(jnp.sum(lp[0:1] * lp[1:2], axis=-1, keepdims=True))
               - jnp.exp(jnp.sum(lp[2:3] * lp[3:4], axis=-1, keepdims=True)) + lam_init)
        head = lambda h: _normalized(acc_scr[2 * h]) - lam * _normalized(acc_scr[2 * h + 1])
        is_lo = lane < HEAD_DIM
        for p in range(N_HEADS // 2):
            o = jnp.concatenate([head(2 * p), head(2 * p + 1)], axis=0).T
            sq = o * o
            ms_lo = jnp.sum(jnp.where(is_lo, sq, 0.0), axis=-1, keepdims=True)
            ms_hi = jnp.sum(jnp.where(is_lo, 0.0, sq), axis=-1, keepdims=True)
            ms = jnp.where(is_lo, ms_lo, ms_hi) * (1.0 / HEAD_DIM)
            o_ref[:, p * LANES:(p + 1) * LANES] = (o * lax.rsqrt(ms + DIFF_LN_EPS) * ln_ref[...]
                                                   * (1.0 - lam_init)).astype(BF16)


def _diff_attention(u, vt, cos, sin, rot, lam_p, subln, batch, seq, tq, lam_init):
    nq = seq // tq
    qi, kj = _causal_pairs(nq)
    q_spec, k_spec, vt_spec = _attn_specs(nq, tq, U_DIFF, 0)
    tab_q = pl.BlockSpec((tq, W_MIX), lambda b, t, qi, kj: (qi[t], 0))
    tab_k = pl.BlockSpec((tq, W_MIX), lambda b, t, qi, kj: (kj[t], 0))
    const = lambda r, c: pl.BlockSpec((r, c), lambda b, t, qi, kj: (0, 0))
    stat = lambda: pltpu.VMEM((2 * N_HEADS, 1, tq), F32)
    return pl.pallas_call(
        functools.partial(_diff_kernel, lam_init=lam_init),
        grid_spec=pltpu.PrefetchScalarGridSpec(
            num_scalar_prefetch=2,
            grid=(batch, qi.shape[0]),
            in_specs=[q_spec, k_spec, vt_spec, tab_q, tab_q, tab_k, tab_k,
                      const(W_MIX, W_MIX), const(4, DIFF_DH), const(1, LANES)],
            out_specs=pl.BlockSpec((tq, W_MIX), lambda b, t, qi, kj: (b * nq + qi[t], 0)),
            scratch_shapes=[pltpu.VMEM((2 * N_HEADS, tq, LANES), BF16), stat(),
                            pltpu.VMEM((2 * N_HEADS, ACC_ROWS, tq), F32)],
        ),
        out_shape=jax.ShapeDtypeStruct((batch * seq, W_MIX), BF16),
        compiler_params=_cparams(("parallel", "arbitrary")),
        name="diff_attention",
    )(qi, kj, u, u, vt, cos, sin, cos, sin, rot, lam_p, subln)


def _merge_kernel(x_ref, gate_ref, oa_ref, post_ref, yb_ref, yc_ref, od_ref, gd_ref, lnw_ref, lnb_ref,
                  wbo_ref, wout_ref, o_ref):
    y_a = (oa_ref[...].astype(F32) * lnw_ref[...] + lnb_ref[...] + post_ref[1].astype(F32)) * post_ref[0].astype(F32)
    y_d = od_ref[...].astype(F32) * _silu(gd_ref[...].astype(F32))
    acc = jnp.zeros(x_ref.shape, F32)
    for b, y in enumerate((y_a, yb_ref[...], yc_ref[...], y_d)):
        gate = _sigmoid(gate_ref[:, b * D_MODEL:(b + 1) * D_MODEL].astype(F32))
        acc = acc + gate * _mm(y, wbo_ref[b])
    o_ref[...] = x_ref[...] + _mm(acc, wout_ref[...])


def _merge(x, u, o_a, post, y_b, y_c, o_d, ln_w, ln_b, w_bo, w_out, tm):
    n = x.shape[0]
    tok = lambda c: pl.BlockSpec((tm, c), lambda i: (i, 0))
    return pl.pallas_call(
        _merge_kernel,
        grid=(n // tm,),
        in_specs=[
            tok(D_MODEL),
            pl.BlockSpec((tm, 4 * D_MODEL), lambda i: (i, U_GATE // (4 * D_MODEL))),
            tok(W_MIX),
            pl.BlockSpec((2, tm, W_MIX), lambda i: (0, i, 0)),
            tok(W_MIX), tok(W_MIX), tok(W_MIX),
            pl.BlockSpec((tm, W_MIX), lambda i: (i, (U_GDN + 3 * W_MIX) // W_MIX)),
            pl.BlockSpec((1, W_MIX), lambda i: (0, 0)),
            pl.BlockSpec((1, W_MIX), lambda i: (0, 0)),
            pl.BlockSpec((4, W_MIX, D_MODEL), lambda i: (0, 0, 0)),
            pl.BlockSpec((D_MODEL, D_MODEL), lambda i: (0, 0)),
        ],
        out_specs=tok(D_MODEL),
        out_shape=jax.ShapeDtypeStruct((n, D_MODEL), F32),
        compiler_params=_cparams(("parallel",)),
        name="merge",
    )(x, u, o_a, post, y_b, y_c, o_d, u, ln_w, ln_b, w_bo, w_out)


def _ffn_kernel(x_ref, g_ref, wg_ref, wu_ref, wd_ref, o_ref):
    x = x_ref[...]
    h = _rms(x, g_ref[...], NORM_EPS).astype(BF16)
    act = _silu(jnp.dot(h, wg_ref[...], preferred_element_type=F32)) * jnp.dot(h, wu_ref[...], preferred_element_type=F32)
    o_ref[...] = x + _mm(act, wd_ref[...])


def _ffn(x, g, wg, wu, wd, tm):
    n = x.shape[0]
    ff = wg.shape[1]
    resident = lambda r, c: pl.BlockSpec((r, c), lambda i: (0, 0), pipeline_mode=pl.Buffered(1))
    return pl.pallas_call(
        _ffn_kernel,
        grid=(n // tm,),
        in_specs=[
            pl.BlockSpec((tm, D_MODEL), lambda i: (i, 0)),
            pl.BlockSpec((1, D_MODEL), lambda i: (0, 0)),
            resident(D_MODEL, ff), resident(D_MODEL, ff), resident(ff, D_MODEL),
        ],
        out_specs=pl.BlockSpec((tm, D_MODEL), lambda i: (i, 0)),
        out_shape=jax.ShapeDtypeStruct((n, D_MODEL), F32),
        compiler_params=_cparams(("parallel",), vmem_mb=VMEM_LARGE_MB),
        name="ffn",
    )(x, g, wg, wu, wd)


def _router_kernel(x_ref, g_ref, router_ref, h_ref, c_ref, rc_ref, rr_ref, cnt_ref):
    t = x_ref.shape[0]
    h = _rms(x_ref[...], g_ref[...], NORM_EPS)
    h_ref[...] = h.astype(BF16)
    logits = _mm_split(h, router_ref[...])
    lane = lax.broadcasted_iota(jnp.int32, logits.shape, 1).astype(F32)
    lg = jnp.where(lane < N_EXPERTS, logits, -jnp.inf)
    m1 = jnp.max(lg, axis=-1, keepdims=True)
    i1 = jnp.min(jnp.where(lg == m1, lane, float(LANES)), axis=-1, keepdims=True)
    lg2 = jnp.where(lane == i1, -jnp.inf, lg)
    m2 = jnp.max(lg2, axis=-1, keepdims=True)
    i2 = jnp.min(jnp.where(lg2 == m2, lane, float(LANES)), axis=-1, keepdims=True)
    e2 = jnp.exp(m2 - m1)
    c_ref[...] = jnp.where(lane == i1, 1.0 / (1.0 + e2), 0.0) + jnp.where(lane == i2, e2 / (1.0 + e2), 0.0)
    sel = jnp.logical_or(lane == i1, lane == i2)
    sel_f = jnp.where(sel, 1.0, 0.0)
    earlier = _tri_masks(t)[0].astype(BF16)
    rank = jnp.dot(earlier, sel_f.astype(BF16), preferred_element_type=F32)
    rc = jnp.where(sel, rank, -1.0)
    rc_ref[...] = rc
    rr_ref[...] = rc.T[0:BF16_ROWS, :]
    cnt_ref[...] = jnp.broadcast_to(jnp.sum(sel_f, axis=0, keepdims=True), cnt_ref.shape).astype(jnp.int32)


def _router(x, g, router, tm):
    n = x.shape[0]
    return pl.pallas_call(
        _router_kernel,
        grid=(n // tm,),
        in_specs=[
            pl.BlockSpec((tm, D_MODEL), lambda i: (i, 0)),
            pl.BlockSpec((1, D_MODEL), lambda i: (0, 0)),
            pl.BlockSpec((D_MODEL, LANES), lambda i: (0, 0)),
        ],
        out_specs=[
            pl.BlockSpec((tm, D_MODEL), lambda i: (i, 0)),
            pl.BlockSpec((tm, LANES), lambda i: (i, 0)),
            pl.BlockSpec((tm, LANES), lambda i: (i, 0)),
            pl.BlockSpec((BF16_ROWS, tm), lambda i: (0, i)),
            pl.BlockSpec((SUBLANES, LANES), lambda i: (i, 0)),
        ],
        out_shape=[
            jax.ShapeDtypeStruct((n, D_MODEL), BF16),
            jax.ShapeDtypeStruct((n, LANES), F32),
            jax.ShapeDtypeStruct((n, LANES), F32),
            jax.ShapeDtypeStruct((BF16_ROWS, n), F32),
            jax.ShapeDtypeStruct((n // tm * SUBLANES, LANES), jnp.int32),
        ],
        compiler_params=_cparams(("parallel",)),
        name="router",
    )(x, g, router)


def _moe_kernel(cnt_ref, x_ref, h_ref, c_ref, rc_ref, rr_ref, wg_ref, wu_ref, wd_ref, o_ref, xg_scr, yg_scr, *, rows):
    i = pl.program_id(0)
    e = pl.program_id(1)
    f = pl.program_id(2)
    last_f = f == pl.num_programs(2) - 1
    t = h_ref.shape[0]
    nsub = xg_scr.shape[0] // rows
    cnt = cnt_ref[i * N_EXPERTS + e]
    blocks = [(sb, slice(sb * rows, (sb + 1) * rows)) for sb in range(nsub)]

    @pl.when(jnp.logical_and(e == 0, f == 0))
    def _():
        o_ref[...] = x_ref[...]

    @pl.when(f == 0)
    def _():
        rr = rr_ref[pl.ds(e, 1), :]
        for sb, rs in blocks:
            @pl.when(cnt > sb * rows)
            def _():
                slot = (lax.broadcasted_iota(jnp.int32, (rows, t), 0) + sb * rows).astype(F32)
                gather = jnp.where(rr == slot, 1.0, 0.0).astype(BF16)
                xg_scr[rs, :] = jnp.dot(gather, h_ref[...], preferred_element_type=F32).astype(BF16)
                yg_scr[rs, :] = jnp.zeros((rows, D_MODEL), F32)

    for sb, rs in blocks:
        @pl.when(cnt > sb * rows)
        def _():
            xb = xg_scr[rs, :]
            act = (_silu(jnp.dot(xb, wg_ref[0], preferred_element_type=F32))
                   * jnp.dot(xb, wu_ref[0], preferred_element_type=F32))
            yg_scr[rs, :] += _mm(act, wd_ref[0])

    @pl.when(last_f)
    def _():
        lane = lax.broadcasted_iota(jnp.int32, (t, LANES), 1)
        mine = lane == e
        rc = jnp.sum(jnp.where(mine, rc_ref[...], 0.0), axis=-1, keepdims=True)
        ce = jnp.sum(jnp.where(mine, c_ref[...], 0.0), axis=-1, keepdims=True)
        for sb, rs in blocks:
            @pl.when(cnt > sb * rows)
            def _():
                slot = (lax.broadcasted_iota(jnp.int32, (t, rows), 1) + sb * rows).astype(F32)
                scatter = jnp.where(rc == slot, 1.0, 0.0).astype(BF16)
                o_ref[...] += ce * jnp.dot(scatter, yg_scr[rs, :].astype(BF16), preferred_element_type=F32)


def _moe(x, g, router, wg, wu, wd, tm, tf, rows):
    n = x.shape[0]
    ff = wg.shape[2]
    h, c, rc, rr, cnt = _router(x, g, router, tm)
    cnt = cnt.reshape(n // tm, SUBLANES, LANES)[:, 0, :N_EXPERTS].reshape(-1)
    nsub = -(-tm // rows)
    tok = lambda cols, **kw: pl.BlockSpec((tm, cols), lambda i, e, f, cnt: (i, 0), **kw)
    once = dict(pipeline_mode=pl.Buffered(1))
    return pl.pallas_call(
        functools.partial(_moe_kernel, rows=rows),
        grid_spec=pltpu.PrefetchScalarGridSpec(
            num_scalar_prefetch=1,
            grid=(n // tm, N_EXPERTS, ff // tf),
            in_specs=[
                tok(D_MODEL, **once), tok(D_MODEL, **once), tok(LANES), tok(LANES),
                pl.BlockSpec((BF16_ROWS, tm), lambda i, e, f, cnt: (0, i)),
                pl.BlockSpec((1, D_MODEL, tf), lambda i, e, f, cnt: (e, 0, f)),
                pl.BlockSpec((1, D_MODEL, tf), lambda i, e, f, cnt: (e, 0, f)),
                pl.BlockSpec((1, tf, D_MODEL), lambda i, e, f, cnt: (e, f, 0)),
            ],
            out_specs=tok(D_MODEL),
            scratch_shapes=[pltpu.VMEM((nsub * rows, D_MODEL), BF16), pltpu.VMEM((nsub * rows, D_MODEL), F32)],
        ),
        out_shape=jax.ShapeDtypeStruct((n, D_MODEL), F32),
        compiler_params=_cparams(("parallel", "arbitrary", "arbitrary"), vmem_mb=VMEM_LARGE_MB),
        name="moe",
    )(cnt, x, h, c, rc, rr, wg, wu, wd)


def _ple_kernel(x_ref, p_ref, g_ref, wgate_ref, wproj_ref, fin_ref, o_ref, *, final):
    x = x_ref[...]
    h = _rms(x, g_ref[...], NORM_EPS)
    y = x + _sigmoid(_mm(h, wgate_ref[...])) * _mm(p_ref[...], wproj_ref[...])
    o_ref[...] = _rms(y, fin_ref[...], NORM_EPS) if final else y


def _ple(x, p, g, wgate, wproj, fin, tm, final):
    n = x.shape[0]
    return pl.pallas_call(
        functools.partial(_ple_kernel, final=final),
        grid=(n // tm,),
        in_specs=[
            pl.BlockSpec((tm, D_MODEL), lambda i: (i, 0)),
            pl.BlockSpec((tm, P_DIM), lambda i: (i, 0)),
            pl.BlockSpec((1, D_MODEL), lambda i: (0, 0)),
            pl.BlockSpec((D_MODEL, D_MODEL), lambda i: (0, 0)),
            pl.BlockSpec((P_DIM, D_MODEL), lambda i: (0, 0)),
            pl.BlockSpec((1, D_MODEL), lambda i: (0, 0)),
        ],
        out_specs=pl.BlockSpec((tm, D_MODEL), lambda i: (i, 0)),
        out_shape=jax.ShapeDtypeStruct((n, D_MODEL), F32),
        compiler_params=_cparams(("parallel",)),
        name="ple",
    )(x, p, g, wgate, wproj, fin)


def _tiles(n, seq):
    tm = min(512, seq)
    tm_big = 1024 if n % 1024 == 0 else tm
    tm_in = 2048 if n % 2048 == 0 else tm_big
    moe_rows = tm_big // 4 + tm_big // 32
    return dict(tm=tm, tm_big=tm_big, tm_in=tm_in, tn_in=1024, tq=min(512, seq), cblk=min(512, seq), moe_rows=moe_rows)


def _rope_tables(seq):
    half = ROPE_DIMS // 2
    inv = ROPE_THETA ** (-jnp.arange(half, dtype=F32) * 2.0 / ROPE_DIMS)
    ang = jnp.arange(seq, dtype=F32)[:, None] * inv[None, :]
    pad = jnp.zeros((seq, DIFF_DH - ROPE_DIMS), F32)
    cos = jnp.concatenate([jnp.cos(ang), jnp.cos(ang), pad + 1.0], axis=-1)
    sin = jnp.concatenate([jnp.sin(ang), jnp.sin(ang), pad], axis=-1)
    d = jnp.arange(W_MIX)
    dd = d % DIFF_DH
    src = jnp.where(dd < half, d + half, d - half)
    sign = jnp.where(dd < half, -1.0, jnp.where(dd < ROPE_DIMS, 1.0, 0.0))
    rot = jnp.zeros((W_MIX, W_MIX), F32).at[src, d].set(sign)
    reps = W_MIX // DIFF_DH
    return jnp.tile(cos, (1, reps)), jnp.tile(sin, (1, reps)), rot.astype(BF16)


def _split_w_in(w):
    a0 = 0
    b0 = a0 + 4 * W_MIX
    c0 = b0 + 3 * W_MIX
    d0 = c0 + 3 * W_MIX + N_HEADS
    g0 = d0 + 4 * W_MIX + 2 * N_HEADS
    d_small = d0 + 3 * W_MIX
    diff_q = w[:, b0:b0 + W_MIX] * (DIFF_DH ** -0.5 * LOG2E)
    fox_q = w[:, c0:c0 + W_MIX] * (HEAD_DIM ** -0.5 * LOG2E)
    main = jnp.concatenate([
        w[:, g0:], w[:, a0:b0], w[:, d0:d_small], w[:, d_small + 2 * N_HEADS:g0],
        diff_q, w[:, b0 + W_MIX:b0 + 2 * W_MIX], fox_q, w[:, c0 + W_MIX:c0 + 2 * W_MIX],
    ], axis=1).astype(BF16)
    small = jnp.concatenate([
        w[:, c0 + 3 * W_MIX:d0], w[:, d_small:d_small + 2 * N_HEADS],
        jnp.zeros((D_MODEL, LANES - 3 * N_HEADS), w.dtype),
    ], axis=1).astype(BF16)
    v_t = jnp.concatenate([w[:, b0 + 2 * W_MIX:c0], w[:, c0 + 2 * W_MIX:c0 + 3 * W_MIX]], axis=1).T.astype(BF16)
    return main, small, v_t


def _small_params(fbias, a_log, dt_bias):
    zeros = jnp.zeros((N_HEADS,), F32)
    bias = jnp.concatenate([fbias, zeros, dt_bias, jnp.zeros((LANES - 3 * N_HEADS,), F32)])
    neg_a = jnp.concatenate([zeros, zeros, -jnp.exp(a_log), jnp.zeros((LANES - 3 * N_HEADS,), F32)])
    return bias.reshape(1, LANES), neg_a.reshape(1, LANES)


def _pad_rows(w, top, total):
    return jnp.concatenate([jnp.zeros((top, w.shape[1]), w.dtype), w,
                            jnp.zeros((total - top - w.shape[0], w.shape[1]), w.dtype)], axis=0)


def kernel(x, p, norm_mix, norm_ffn, norm_ple, w_in, w_bo, w_out, rwkv_mu, rwkv_w0, rwkv_w2, rwkv_a0, rwkv_a2, rwkv_g2, rwkv_kk, rwkv_ka, rwkv_rk, rwkv_ln_w, rwkv_ln_b, diff_lam, diff_subln, fox_fbias, gdn_conv, gdn_a_log, gdn_dt_bias, gdn_norm, ffn_w_gate, ffn_w_up, ffn_w_down, moe_router, moe_w_gate, moe_w_up, moe_w_down, ple_proj, ple_gate, final_norm):
    batch, seq, _ = x.shape
    depth = w_in.shape[0]
    n = batch * seq
    t = _tiles(n, seq)
    tm, tq, cblk = t["tm"], t["tq"], t["cblk"]
    row = lambda v: v.reshape(1, -1).astype(F32)
    cos, sin, rot = _rope_tables(seq)
    xf = x.reshape(n, D_MODEL)
    pf = p.reshape(depth, n, P_DIM)

    for i in range(depth):
        w_main, w_small, w_vt = _split_w_in(w_in[i])
        u, scol, vt = _inproj(xf, row(norm_mix[i]), w_main, w_small, w_vt, t["tm_in"], t["tn_in"])
        bias, neg_a = _small_params(fox_fbias[i], gdn_a_log[i], gdn_dt_bias[i])
        hcol, fox_kb, fox_qb = _small_prep(scol, bias, neg_a, batch, seq, tm)

        scan_in, pc, post = _rwkv_prep(
            u, row(rwkv_mu[i]), row(rwkv_w0[i]), _pad_rows(rwkv_w2[i], 0, LANES), row(rwkv_a0[i]),
            _pad_rows(rwkv_a2[i], LANES // 2, LANES), rwkv_g2[i], row(rwkv_kk[i]), row(rwkv_ka[i]), row(rwkv_rk[i]),
            seq, tm)
        o_a = _rwkv_chunk(scan_in, pc, batch, seq, cblk)

        o_d = _gdn_chunk(_gdn_prep(u, gdn_conv[i].T, seq, tm), hcol, jnp.tile(row(gdn_norm[i]), (1, N_HEADS)),
                         batch, seq, cblk)

        lam_init = 0.8 - 0.6 * math.exp(-0.3 * i)
        y_b = _diff_attention(u, vt, cos, sin, rot, diff_lam[i].astype(F32), jnp.tile(row(diff_subln[i]), (1, 2)),
                              batch, seq, tq, lam_init)
        y_c = _fox_attention(u, vt, fox_qb, fox_kb, batch, seq, tq)

        xf = _merge(xf, u, o_a, post, y_b, y_c, o_d, row(rwkv_ln_w[i]), row(rwkv_ln_b[i]),
                    w_bo[i].astype(BF16), w_out[i].astype(BF16), tm)

        j = i // 2
        if i % 2 == 0:
            xf = _ffn(xf, row(norm_ffn[i]), ffn_w_gate[j].astype(BF16), ffn_w_up[j].astype(BF16),
                      ffn_w_down[j].astype(BF16), tm)
        else:
            router = jnp.concatenate([moe_router[j], jnp.zeros((D_MODEL, LANES - N_EXPERTS), F32)], axis=1)
            xf = _moe(xf, row(norm_ffn[i]), router, moe_w_gate[j].astype(BF16), moe_w_up[j].astype(BF16),
                      moe_w_down[j].astype(BF16), t["tm_big"], moe_w_gate.shape[3] // 2, t["moe_rows"])
        xf = _ple(xf, pf[i], row(norm_ple[i]), ple_gate[i].astype(BF16), ple_proj[i].astype(BF16),
                  row(final_norm), tm, i == depth - 1)
    return xf.reshape(batch, seq, D_MODEL)
```

```python
import functools
import math

import jax
import jax.numpy as jnp
import numpy as np
from jax import lax
from jax.experimental import pallas as pl
from jax.experimental.pallas import tpu as pltpu

F32 = jnp.float32
BF16 = jnp.bfloat16

D_MODEL = 1024
P_DIM = 256
W_MIX = 256
HEAD_DIM = 64
N_HEADS = 4
DIFF_DH = 32
ROPE_THETA = 500000.0
ROPE_DIMS = 8
RWKV_GN_EPS = 64e-5
DIFF_LN_EPS = 1e-5
GDN_CONV = 4
CHUNK = 64
CHUNK_UNROLL = 8
N_EXPERTS = 8
NORM_EPS = 1e-6
L2_EPS = 1e-6
LOG2E = math.log2(math.e)
LANES = 128
SUBLANES = 8
BF16_ROWS = 16
VMEM_MB = 48
VMEM_LARGE_MB = 56
assert CHUNK == HEAD_DIM

U_GATE = 0
U_RWKV = 4096
U_GDN = 5120
U_DIFF = 6144
U_FOX = 6656
U_COLS = 7168
SM_FOX, SM_BETA, SM_DEC = 0, 4, 8
FOX_CK, FOX_CQ = 0, 16


def _cparams(semantics, vmem_mb=VMEM_MB):
    return pltpu.CompilerParams(dimension_semantics=semantics, vmem_limit_bytes=vmem_mb * 1024 * 1024)


def _mm(a, b):
    return jnp.dot(a.astype(BF16), b.astype(BF16), preferred_element_type=F32)


def _mm_nt(a, b):
    return lax.dot_general(a.astype(BF16), b.astype(BF16), (((1,), (1,)), ((), ())), preferred_element_type=F32)


def _mm_tn(a, b):
    return lax.dot_general(a.astype(BF16), b.astype(BF16), (((0,), (0,)), ((), ())), preferred_element_type=F32)


def _mm_split(a, b):
    ah = a.astype(BF16)
    al = (a - ah.astype(F32)).astype(BF16)
    bh = b.astype(BF16)
    bl = (b - bh.astype(F32)).astype(BF16)
    dot = lambda x, y: jnp.dot(x, y, preferred_element_type=F32)
    return dot(ah, bh) + (dot(ah, bl) + dot(al, bh))


def _mm_mask(mask, x):
    hi = x.astype(BF16)
    r1 = x - hi.astype(F32)
    mid = r1.astype(BF16)
    lo = (r1 - mid.astype(F32)).astype(BF16)
    dot = lambda t: jnp.dot(mask, t, preferred_element_type=F32)
    return dot(hi) + dot(mid) + dot(lo)


def _head_sums(x, same):
    hi = x.astype(BF16)
    lo = (x - hi.astype(F32)).astype(BF16)
    return jnp.dot(hi, same, preferred_element_type=F32) + jnp.dot(lo, same, preferred_element_type=F32)


def _rms(x, g, eps):
    return x * lax.rsqrt(jnp.mean(x * x, axis=-1, keepdims=True) + eps) * g


def _sigmoid(x):
    return 0.5 * jnp.tanh(0.5 * x) + 0.5


def _silu(x):
    return x * _sigmoid(x)


def _softplus(x):
    return jnp.maximum(x, 0.0) + jnp.log(1.0 + jnp.exp(-jnp.abs(x)))


def _tri_masks(c):
    ii = lax.broadcasted_iota(jnp.int32, (c, c), 0)
    jj = lax.broadcasted_iota(jnp.int32, (c, c), 1)
    return ii > jj, ii >= jj, ii == jj


def _block_masks():
    ii = lax.broadcasted_iota(jnp.int32, (W_MIX, W_MIX), 0)
    jj = lax.broadcasted_iota(jnp.int32, (W_MIX, W_MIX), 1)
    return (ii // HEAD_DIM) == (jj // HEAD_DIM), ii == jj


def _wide_masks():
    ii = lax.broadcasted_iota(jnp.int32, (CHUNK, W_MIX), 0)
    jj = lax.broadcasted_iota(jnp.int32, (CHUNK, W_MIX), 1) % HEAD_DIM
    return ii > jj, ii >= jj, ii == jj


def _stack_heads(x, same):
    return jnp.where(same, jnp.concatenate([x, x, x, x], axis=0), jnp.zeros((), x.dtype))


def _unit_lower_inverses(ns, eye_w, same):
    rs = [eye_w + n for n in ns]
    ps = list(ns)
    for _ in range(int(math.log2(CHUNK)) - 1):
        ps = [_mm(p, _stack_heads(p.astype(BF16), same)) for p in ps]
        rs = [r + _mm(r, _stack_heads(p.astype(BF16), same)) for r, p in zip(rs, ps)]
    return rs


def _unstack_heads(x):
    return x[0:CHUNK] + x[CHUNK:2 * CHUNK] + x[2 * CHUNK:3 * CHUNK] + x[3 * CHUNK:4 * CHUNK]


def _inproj_kernel(x_ref, g_ref, w_ref, ws_ref, wvt_ref, u_ref, scol_ref, vt_ref, *, tn):
    hb = _rms(x_ref[...], g_ref[...], NORM_EPS).astype(BF16)
    scol_ref[...] = jnp.dot(hb, ws_ref[...], preferred_element_type=F32)
    vt_ref[...] = lax.dot_general(wvt_ref[...], hb, (((1,), (1,)), ((), ())), preferred_element_type=F32).astype(BF16)
    for j in range(U_COLS // tn):
        cols = slice(j * tn, (j + 1) * tn)
        u_ref[:, cols] = jnp.dot(hb, w_ref[:, cols], preferred_element_type=F32).astype(BF16)


def _inproj(x, g, w, ws, wvt, tm, tn):
    n = x.shape[0]
    resident = lambda r, c: pl.BlockSpec((r, c), lambda i: (0, 0), pipeline_mode=pl.Buffered(1))
    return pl.pallas_call(
        functools.partial(_inproj_kernel, tn=tn),
        grid=(n // tm,),
        in_specs=[
            pl.BlockSpec((tm, D_MODEL), lambda i: (i, 0)),
            pl.BlockSpec((1, D_MODEL), lambda i: (0, 0)),
            resident(D_MODEL, U_COLS), resident(D_MODEL, LANES), resident(2 * W_MIX, D_MODEL),
        ],
        out_specs=[
            pl.BlockSpec((tm, U_COLS), lambda i: (i, 0)),
            pl.BlockSpec((tm, LANES), lambda i: (i, 0)),
            pl.BlockSpec((2 * W_MIX, tm), lambda i: (0, i)),
        ],
        out_shape=[
            jax.ShapeDtypeStruct((n, U_COLS), BF16),
            jax.ShapeDtypeStruct((n, LANES), F32),
            jax.ShapeDtypeStruct((2 * W_MIX, n), BF16),
        ],
        compiler_params=_cparams(("parallel",), vmem_mb=VMEM_LARGE_MB),
        name="inproj",
    )(x, g, w, ws, wvt)


def _lane_placement(base):
    r = lax.broadcasted_iota(jnp.int32, (3 * LANES, LANES), 0)
    m = lax.broadcasted_iota(jnp.int32, (3 * LANES, LANES), 1)
    head, part = r % LANES, r // LANES
    return jnp.logical_and(head < N_HEADS, m == base + 3 * head + part).astype(BF16)


def _split3(x):
    hi = x.astype(BF16)
    r1 = x - hi.astype(F32)
    mid = r1.astype(BF16)
    lo = (r1 - mid.astype(F32)).astype(BF16)
    return jnp.concatenate([hi, mid, lo], axis=1)


def _head_expansion():
    r = lax.broadcasted_iota(jnp.int32, (3 * LANES, 2 * W_MIX), 0) % LANES
    m = lax.broadcasted_iota(jnp.int32, (3 * LANES, 2 * W_MIX), 1)
    src = jnp.where(m < W_MIX, SM_BETA + m // HEAD_DIM, SM_DEC + (m - W_MIX) // HEAD_DIM)
    return (r == src).astype(BF16)


def _small_prep_kernel(scol_ref, bias_ref, nega_ref, ogb_ref, okb_ref, oqb_ref, carry_scr):
    @pl.when(pl.program_id(1) == 0)
    def _():
        carry_scr[...] = jnp.zeros_like(carry_scr)

    tm = scol_ref.shape[0]
    lower = _tri_masks(tm)[1].astype(BF16)
    z = scol_ref[...] + bias_ref[...]
    lane = lax.broadcasted_iota(jnp.int32, z.shape, 1)
    is_f = lane < SM_BETA
    is_b = jnp.logical_and(lane >= SM_BETA, lane < SM_DEC)
    is_d = jnp.logical_and(lane >= SM_DEC, lane < SM_DEC + N_HEADS)
    logf = jnp.where(is_f, jnp.minimum(z, 0.0) - jnp.log(1.0 + jnp.exp(-jnp.abs(z))), 0.0)
    gdn = jnp.where(is_b, _sigmoid(z), jnp.where(is_d, nega_ref[...] * _softplus(z), 0.0))
    ogb_ref[...] = jnp.dot(_split3(gdn), _head_expansion(), preferred_element_type=F32)
    cum = _mm_mask(lower, logf) + carry_scr[...]
    carry_scr[...] = cum[tm - 1:tm, :]
    parts = _split3(cum * LOG2E)
    ones_lanes = jnp.logical_and(lane >= FOX_CQ, lane < FOX_CQ + 3 * N_HEADS)
    okb_ref[...] = (jnp.dot(parts, _lane_placement(FOX_CK), preferred_element_type=F32)
                    + jnp.where(ones_lanes, 1.0, 0.0)).astype(BF16)
    oqb_ref[...] = jnp.dot(parts, _lane_placement(FOX_CQ), preferred_element_type=F32).astype(BF16)


def _small_prep(scol, bias, neg_a, batch, seq, tm):
    n = batch * seq
    nt = seq // tm
    tok = lambda: pl.BlockSpec((tm, LANES), lambda b, j: (b * nt + j, 0))
    const = lambda: pl.BlockSpec((1, LANES), lambda b, j: (0, 0))
    return pl.pallas_call(
        _small_prep_kernel,
        grid=(batch, nt),
        in_specs=[tok(), const(), const()],
        out_specs=[pl.BlockSpec((tm, 2 * W_MIX), lambda b, j: (b * nt + j, 0)), tok(), tok()],
        out_shape=[jax.ShapeDtypeStruct((n, 2 * W_MIX), F32), jax.ShapeDtypeStruct((n, LANES), BF16),
                   jax.ShapeDtypeStruct((n, LANES), BF16)],
        scratch_shapes=[pltpu.VMEM((1, LANES), F32)],
        compiler_params=_cparams(("parallel", "arbitrary")),
        name="small_prep",
    )(scol, bias, neg_a)


def _rwkv_prep_kernel(u_ref, up_ref, mu_ref, w0_ref, w2_ref, a0_ref, a2_ref, g2_ref, kk_ref, ka_ref, rk_ref,
                      scan_ref, pc_ref, post_ref, *, tiles_per_seq):
    tm = u_ref.shape[0]
    u = u_ref[...].astype(F32)
    prev = up_ref[...].astype(F32)[BF16_ROWS - 1:BF16_ROWS, :]
    prev = jnp.where(pl.program_id(0) % tiles_per_seq == 0, 0.0, prev)
    rows = lax.broadcasted_iota(jnp.int32, (tm, 1), 0)
    u_prev = jnp.where(rows == 0, prev, pltpu.roll(u, 1, 0))
    xm = u + (u_prev - u) * mu_ref[...]
    r = xm[:, 0:W_MIX]
    k = xm[:, W_MIX:2 * W_MIX]
    v = xm[:, 2 * W_MIX:3 * W_MIX]
    x_lora = xm[:, 3 * W_MIX:3 * W_MIX + LANES]
    xg = xm[:, 3 * W_MIX + LANES:]
    logw = -_softplus(-(w0_ref[...] + _mm(jnp.tanh(x_lora), w2_ref[...]))) - 0.5
    log_decay = -jnp.exp(logw)
    a = _sigmoid(a0_ref[...] + _mm(x_lora, a2_ref[...]))
    g = _mm(_sigmoid(xg), g2_ref[...])
    same = _block_masks()[0].astype(BF16)
    kk_raw = k * kk_ref[...]
    kk = kk_raw * lax.rsqrt(_head_sums(kk_raw * kk_raw, same) + L2_EPS)
    k2 = k * (1.0 + (a - 1.0) * ka_ref[...])
    bonus = _head_sums(r * k2 * rk_ref[...], same) * v
    ti = lax.broadcasted_iota(jnp.int32, (tm, tm), 0)
    tj = lax.broadcasted_iota(jnp.int32, (tm, tm), 1)
    in_chunk = jnp.logical_and(ti // CHUNK == tj // CHUNK, ti >= tj).astype(BF16)
    ci = lax.broadcasted_iota(jnp.int32, (tm // CHUNK, tm), 0)
    cj = lax.broadcasted_iota(jnp.int32, (tm // CHUNK, tm), 1)
    cum = _mm_mask(in_chunk, log_decay)
    cum_end = _mm_mask((ci == cj // CHUNK).astype(BF16), log_decay)
    inv = jnp.exp(-cum)
    scan_ref[0] = (-kk * jnp.exp(cum - log_decay)).astype(BF16)
    scan_ref[1] = (kk * a * inv).astype(BF16)
    scan_ref[2] = (k2 * inv).astype(BF16)
    scan_ref[3] = (r * jnp.exp(cum)).astype(BF16)
    scan_ref[4] = v.astype(BF16)
    pc_ref[...] = jnp.exp(cum_end)
    post_ref[0] = g.astype(BF16)
    post_ref[1] = bonus.astype(BF16)


def _rwkv_prep(u, mu, w0, w2p, a0, a2p, g2, k_k, k_a, r_k, seq, tm):
    n = u.shape[0]
    ublk = U_RWKV // D_MODEL
    row = lambda c: pl.BlockSpec((1, c), lambda i: (0, 0))
    mat = lambda r: pl.BlockSpec((r, W_MIX), lambda i: (0, 0))
    return pl.pallas_call(
        functools.partial(_rwkv_prep_kernel, tiles_per_seq=seq // tm),
        grid=(n // tm,),
        in_specs=[
            pl.BlockSpec((tm, D_MODEL), lambda i: (i, ublk)),
            pl.BlockSpec((BF16_ROWS, D_MODEL), lambda i: (jnp.maximum(i * (tm // BF16_ROWS) - 1, 0), ublk)),
            row(D_MODEL), row(W_MIX), mat(LANES), row(W_MIX), mat(LANES), mat(LANES), row(W_MIX), row(W_MIX), row(W_MIX),
        ],
        out_specs=[
            pl.BlockSpec((5, tm, W_MIX), lambda i: (0, i, 0)),
            pl.BlockSpec((tm // CHUNK, W_MIX), lambda i: (i, 0)),
            pl.BlockSpec((2, tm, W_MIX), lambda i: (0, i, 0)),
        ],
        out_shape=[jax.ShapeDtypeStruct((5, n, W_MIX), BF16), jax.ShapeDtypeStruct((n // CHUNK, W_MIX), F32),
                   jax.ShapeDtypeStruct((2, n, W_MIX), BF16)],
        compiler_params=_cparams(("parallel",)),
        name="rwkv_prep",
    )(u, u, mu, w0, w2p, a0, a2p, g2, k_k, k_a, r_k)


def _rwkv_chunk_kernel(x_ref, pc_ref, o_ref, s_scr, *, nchunk, nbatch):
    @pl.when(pl.program_id(0) == 0)
    def _():
        s_scr[...] = jnp.zeros_like(s_scr)

    same, diag = _block_masks()
    eye = diag.astype(F32)
    strict_w, incl_w, diag_w = _wide_masks()
    eye_w = diag_w.astype(F32)
    stack = lambda x: _stack_heads(x.astype(BF16), same)

    def chunks(i, carry):
        items = [(b, i * CHUNK_UNROLL + j) for j in range(CHUNK_UNROLL) for b in range(nbatch)]
        each = lambda f, *lists: [f(*args) for args in zip(*lists)]
        sls = [pl.ds(pl.multiple_of(ci * CHUNK, CHUNK), CHUNK) for _, ci in items]
        a, bb, k, r, v = ([x_ref[i, b, sl, :] for (b, _), sl in zip(items, sls)] for i in range(5))
        pc = [pc_ref[b, pl.ds(ci, 1), :] for b, ci in items]
        a_s, b_s, k_s, v_s = (each(stack, x) for x in (a, bb, k, v))
        m_ab = each(lambda x, y: jnp.where(strict_w, _mm_nt(x, y), 0.0), a, b_s)
        m_ak = each(lambda x, y: jnp.where(strict_w, _mm_nt(x, y), 0.0), a, k_s)
        n_rb = each(lambda x, y: jnp.where(incl_w, _mm_nt(x, y), 0.0), r, b_s)
        n_rk = each(lambda x, y: jnp.where(incl_w, _mm_nt(x, y), 0.0), r, k_s)
        t_inv = _unit_lower_inverses(m_ab, eye_w, same)
        a2 = each(_mm, t_inv, a_s)
        u0 = each(lambda t, m, x: _mm(t, stack(_mm(m, x))), t_inv, m_ak, v_s)
        r2 = each(lambda x, n, y: x.astype(F32) + _mm(n, stack(y)), r, n_rb, a2)
        o0 = each(lambda n1, u, n2, x: _mm(n1, stack(u)) + _mm(n2, x), n_rb, u0, n_rk, v_s)
        b_end = each(lambda x, p: x.astype(F32) * p, bb, pc)
        k_end = each(lambda x, p: x.astype(F32) * p, k, pc)
        g_mat = each(lambda p, x, y: eye * p + jnp.where(same, _mm_tn(x, y), 0.0), pc, a2, b_end)
        s0 = each(lambda u, x, y, z: jnp.where(same, _mm_tn(u, x) + _mm_tn(y, z), 0.0), u0, b_end, v, k_end)
        for n, ((b, _), sl) in enumerate(zip(items, sls)):
            s = s_scr[b]
            o = _stack_heads(_mm_nt(r2[n], s) + o0[n], same)
            s_scr[b] = _mm(s, g_mat[n]) + s0[n]
            mean = jnp.sum(o, axis=-1, keepdims=True) * (1.0 / HEAD_DIM)
            cen = jnp.where(same, o - mean, 0.0)
            var = jnp.sum(cen * cen, axis=-1, keepdims=True) * (1.0 / HEAD_DIM)
            o_ref[b, sl, :] = _unstack_heads(cen * lax.rsqrt(var + RWKV_GN_EPS)).astype(BF16)
        return carry

    lax.fori_loop(0, nchunk // CHUNK_UNROLL, chunks, 0)


def _rwkv_chunk(xs, pc, batch, seq, cblk):
    xs = xs.reshape(5, batch, seq, W_MIX)
    pc = pc.reshape(batch, seq // CHUNK, W_MIX)
    out = pl.pallas_call(
        functools.partial(_rwkv_chunk_kernel, nchunk=cblk // CHUNK, nbatch=batch),
        grid=(seq // cblk,),
        in_specs=[
            pl.BlockSpec((5, batch, cblk, W_MIX), lambda j: (0, 0, j, 0)),
            pl.BlockSpec((batch, cblk // CHUNK, W_MIX), lambda j: (0, j, 0)),
        ],
        out_specs=pl.BlockSpec((batch, cblk, W_MIX), lambda j: (0, j, 0)),
        out_shape=jax.ShapeDtypeStruct((batch, seq, W_MIX), BF16),
        scratch_shapes=[pltpu.VMEM((batch, W_MIX, W_MIX), F32)],
        compiler_params=_cparams(("arbitrary",)),
        name="rwkv_chunk",
    )(xs, pc)
    return out.reshape(batch * seq, W_MIX)


def _gdn_prep_kernel(u_ref, up_ref, cw_ref, o_ref, ext_scr, *, tiles_per_seq):
    tm = u_ref.shape[0]
    c3 = 3 * W_MIX
    prev = up_ref[...].astype(F32)[:, :c3]
    ext_scr[0:BF16_ROWS, :] = jnp.where(pl.program_id(0) % tiles_per_seq == 0, 0.0, prev)
    ext_scr[BF16_ROWS:, :] = u_ref[...].astype(F32)[:, :c3]
    y = jnp.zeros((tm, c3), F32)
    for j in range(GDN_CONV):
        y = y + ext_scr[pl.ds(BF16_ROWS - (GDN_CONV - 1) + j, tm), :] * cw_ref[j:j + 1, :]
    y = _silu(y)
    same = _block_masks()[0].astype(BF16)
    q = y[:, 0:W_MIX]
    k = y[:, W_MIX:2 * W_MIX]
    o_ref[0] = (q * lax.rsqrt(_head_sums(q * q, same) + L2_EPS) * (HEAD_DIM ** -0.5)).astype(BF16)
    o_ref[1] = (k * lax.rsqrt(_head_sums(k * k, same) + L2_EPS)).astype(BF16)
    o_ref[2] = y[:, 2 * W_MIX:].astype(BF16)


def _gdn_prep(u, conv_w, seq, tm):
    n = u.shape[0]
    ublk = U_GDN // D_MODEL
    return pl.pallas_call(
        functools.partial(_gdn_prep_kernel, tiles_per_seq=seq // tm),
        grid=(n // tm,),
        in_specs=[
            pl.BlockSpec((tm, D_MODEL), lambda i: (i, ublk)),
            pl.BlockSpec((BF16_ROWS, D_MODEL), lambda i: (jnp.maximum(i * (tm // BF16_ROWS) - 1, 0), ublk)),
            pl.BlockSpec((GDN_CONV, 3 * W_MIX), lambda i: (0, 0)),
        ],
        out_specs=pl.BlockSpec((3, tm, W_MIX), lambda i: (0, i, 0)),
        out_shape=jax.ShapeDtypeStruct((3, n, W_MIX), BF16),
        scratch_shapes=[pltpu.VMEM((tm + BF16_ROWS, 3 * W_MIX), F32)],
        compiler_params=_cparams(("parallel",)),
        name="gdn_prep",
    )(u, u, conv_w)


def _gdn_chunk_kernel(x_ref, gb_ref, nw_ref, o_ref, s_scr, *, nchunk, nbatch):
    @pl.when(pl.program_id(0) == 0)
    def _():
        s_scr[...] = jnp.zeros_like(s_scr)

    same, diag = _block_masks()
    eye = diag.astype(F32)
    strict_w, incl_w, diag_w = _wide_masks()
    eye_w = diag_w.astype(F32)
    lower = _tri_masks(CHUNK)[1].astype(BF16)
    stack = lambda x: _stack_heads(x.astype(BF16), same)

    def chunks(i, carry):
        items = [(b, i * CHUNK_UNROLL + j) for j in range(CHUNK_UNROLL) for b in range(nbatch)]
        each = lambda f, *lists: [f(*args) for args in zip(*lists)]
        sls = [pl.ds(pl.multiple_of(ci * CHUNK, CHUNK), CHUNK) for _, ci in items]
        q, k, v = ([x_ref[i, b, sl, :] for (b, _), sl in zip(items, sls)] for i in range(3))
        beta = [gb_ref[b, sl, 0:W_MIX] for (b, _), sl in zip(items, sls)]
        g = [gb_ref[b, sl, W_MIX:2 * W_MIX] for (b, _), sl in zip(items, sls)]
        k_s = each(stack, k)
        gam = [_mm_mask(lower, x) for x in g]
        gam_end = [x[CHUNK - 1:CHUNK, :] for x in gam]
        gdiff = [_mm_mask(lower, jnp.where(strict_w, x, 0.0)) for x in g]
        decay = [jnp.exp(jnp.where(incl_w, x, -jnp.inf)) for x in gdiff]
        a_mat = each(lambda bt, d, x, y: jnp.where(strict_w, bt * d * _mm_nt(x, y), 0.0), beta, decay, k, k_s)
        t_inv = _unit_lower_inverses([-a for a in a_mat], eye_w, same)
        e_gam = [jnp.exp(x) for x in gam]
        u0 = each(lambda t, bt, x: _mm(t, stack(bt * x.astype(F32))), t_inv, beta, v)
        wm = each(lambda t, bt, e, x: _mm(t, stack((bt * e) * x.astype(F32))), t_inv, beta, e_gam, k)
        qk = each(lambda x, y, d: _mm_nt(x, y) * d, q, k_s, decay)
        q2 = each(lambda e, x, a, w: e * x.astype(F32) - _mm(a, stack(w)), e_gam, q, qk, wm)
        o0 = each(lambda a, u: _mm(a, stack(u)), qk, u0)
        k_end = each(lambda x, ge, ga: x.astype(F32) * jnp.exp(ge - ga), k, gam_end, gam)
        g_mat = each(lambda ge, x, w: eye * jnp.exp(ge) - jnp.where(same, _mm_tn(x, w), 0.0), gam_end, k_end, wm)
        s0 = each(lambda x, u: jnp.where(same, _mm_tn(x, u), 0.0), k_end, u0)
        for n, ((b, _), sl) in enumerate(zip(items, sls)):
            s = s_scr[b]
            o = _stack_heads(_mm(q2[n], s) + o0[n], same)
            s_scr[b] = _mm(g_mat[n], s) + s0[n]
            ms = jnp.sum(o * o, axis=-1, keepdims=True) * (1.0 / HEAD_DIM)
            o_ref[b, sl, :] = (_unstack_heads(o * lax.rsqrt(ms + NORM_EPS)) * nw_ref[...]).astype(BF16)
        return carry

    lax.fori_loop(0, nchunk // CHUNK_UNROLL, chunks, 0)


def _gdn_chunk(xs, gb, norm_w, batch, seq, cblk):
    xs = xs.reshape(3, batch, seq, W_MIX)
    gb = gb.reshape(batch, seq, 2 * W_MIX)
    out = pl.pallas_call(
        functools.partial(_gdn_chunk_kernel, nchunk=cblk // CHUNK, nbatch=batch),
        grid=(seq // cblk,),
        in_specs=[
            pl.BlockSpec((3, batch, cblk, W_MIX), lambda j: (0, 0, j, 0)),
            pl.BlockSpec((batch, cblk, 2 * W_MIX), lambda j: (0, j, 0)),
            pl.BlockSpec((1, W_MIX), lambda j: (0, 0)),
        ],
        out_specs=pl.BlockSpec((batch, cblk, W_MIX), lambda j: (0, j, 0)),
        out_shape=jax.ShapeDtypeStruct((batch, seq, W_MIX), BF16),
        scratch_shapes=[pltpu.VMEM((batch, W_MIX, W_MIX), F32)],
        compiler_params=_cparams(("arbitrary",)),
        name="gdn_chunk",
    )(xs, gb, norm_w)
    return out.reshape(batch * seq, W_MIX)


def _causal_pairs(nq):
    pairs = [(i, j) for i in range(nq) for j in range(i + 1)]
    return jnp.asarray(np.array([p[0] for p in pairs], np.int32)), jnp.asarray(np.array([p[1] for p in pairs], np.int32))


def _softmax_updates(scores, vt_ones, m_scr, acc_scr):
    probs = []
    for i, s in enumerate(scores):
        m_old = m_scr[i]
        m_new = jnp.maximum(m_old, jnp.max(s, axis=0, keepdims=True))
        m_scr[i] = m_new
        probs.append((jnp.exp2(m_old - m_new), jnp.exp2((s - m_new).astype(BF16))))
    for i, (alpha, p) in enumerate(probs):
        acc_scr[i] = alpha * acc_scr[i] + jnp.dot(vt_ones[i], p, preferred_element_type=F32)


def _pair_lanes(h):
    p = h // 2
    return slice(p * LANES, (p + 1) * LANES), h % 2 == 0


ACC_ROWS = HEAD_DIM + BF16_ROWS


def _values_and_ones(vt, tk):
    ones = jnp.ones((BF16_ROWS, tk), vt.dtype)
    return [jnp.concatenate([vt[h * HEAD_DIM:(h + 1) * HEAD_DIM, :], ones], axis=0) for h in range(N_HEADS)]


def _normalized(acc):
    return acc[0:HEAD_DIM] / acc[HEAD_DIM:HEAD_DIM + 1]


def _key_after_query(tq):
    return lax.broadcasted_iota(jnp.int32, (tq, tq), 0) > lax.broadcasted_iota(jnp.int32, (tq, tq), 1)


def _fox_kernel(qi_ref, kj_ref, q_ref, k_ref, vt_ref, qb_ref, kb_ref, o_ref, qm_scr, m_scr, acc_scr):
    t = pl.program_id(1)
    qi = qi_ref[t]
    kj = kj_ref[t]
    tq = q_ref.shape[0]

    @pl.when(kj == 0)
    def _():
        m_scr[...] = jnp.full_like(m_scr, -jnp.inf)
        acc_scr[...] = jnp.zeros_like(acc_scr)
        q = q_ref[...]
        qb = qb_ref[...]
        lane = lax.broadcasted_iota(jnp.int32, (tq, LANES), 1)
        zero = jnp.zeros((), BF16)
        for h in range(N_HEADS):
            slab, low = _pair_lanes(h)
            mine = lane < HEAD_DIM if low else lane >= HEAD_DIM
            ck_lanes = jnp.logical_and(lane >= FOX_CK + 3 * h, lane < FOX_CK + 3 * h + 3)
            cq_lanes = jnp.logical_and(lane >= FOX_CQ + 3 * h, lane < FOX_CQ + 3 * h + 3)
            bias = jnp.where(ck_lanes, -jnp.ones((), BF16), jnp.where(cq_lanes, qb, zero))
            qm_scr[h] = jnp.concatenate([jnp.where(mine, q[:, slab], zero), bias], axis=1)

    def step(diagonal):
        k = k_ref[...]
        kb = kb_ref[...]
        vts = _values_and_ones(vt_ref[...], tq)
        if diagonal:
            masked = _key_after_query(tq)
        keys = [jnp.concatenate([k[:, p * LANES:(p + 1) * LANES], kb], axis=1) for p in range(N_HEADS // 2)]
        scores = []
        for h in range(N_HEADS):
            s = lax.dot_general(keys[h // 2], qm_scr[h], (((1,), (1,)), ((), ())), preferred_element_type=F32)
            if diagonal:
                s = jnp.where(masked, -jnp.inf, s)
            scores.append(s)
        _softmax_updates(scores, vts, m_scr, acc_scr)

    @pl.when(kj < qi)
    def _():
        step(False)

    @pl.when(kj == qi)
    def _():
        step(True)
        for p in range(N_HEADS // 2):
            pair = jnp.concatenate([_normalized(acc_scr[2 * p]), _normalized(acc_scr[2 * p + 1])], axis=0)
            o_ref[:, p * LANES:(p + 1) * LANES] = pair.T.astype(BF16)


def _attn_specs(nq, tq, ucol, vt_rows):
    cb = ucol // W_MIX
    q_spec = pl.BlockSpec((tq, W_MIX), lambda b, t, qi, kj: (b * nq + qi[t], cb))
    k_spec = pl.BlockSpec((tq, W_MIX), lambda b, t, qi, kj: (b * nq + kj[t], cb + 1))
    vt_spec = pl.BlockSpec((W_MIX, tq), lambda b, t, qi, kj: (vt_rows // W_MIX, b * nq + kj[t]))
    return q_spec, k_spec, vt_spec


def _fox_attention(u, vt, qb, kb, batch, seq, tq):
    nq = seq // tq
    qi, kj = _causal_pairs(nq)
    q_spec, k_spec, vt_spec = _attn_specs(nq, tq, U_FOX, W_MIX)
    stat = lambda: pltpu.VMEM((N_HEADS, 1, tq), F32)
    return pl.pallas_call(
        _fox_kernel,
        grid_spec=pltpu.PrefetchScalarGridSpec(
            num_scalar_prefetch=2,
            grid=(batch, qi.shape[0]),
            in_specs=[
                q_spec, k_spec, vt_spec,
                pl.BlockSpec((tq, LANES), lambda b, t, qi, kj: (b * nq + qi[t], 0)),
                pl.BlockSpec((tq, LANES), lambda b, t, qi, kj: (b * nq + kj[t], 0)),
            ],
            out_specs=pl.BlockSpec((tq, W_MIX), lambda b, t, qi, kj: (b * nq + qi[t], 0)),
            scratch_shapes=[pltpu.VMEM((N_HEADS, tq, 2 * LANES), BF16), stat(),
                            pltpu.VMEM((N_HEADS, ACC_ROWS, tq), F32)],
        ),
        out_shape=jax.ShapeDtypeStruct((batch * seq, W_MIX), BF16),
        compiler_params=_cparams(("parallel", "arbitrary")),
        name="fox_attention",
    )(qi, kj, u, u, vt, qb, kb)


def _diff_kernel(qi_ref, kj_ref, q_ref, k_ref, vt_ref, cq_ref, sq_ref, ck_ref, sk_ref, rot_ref, lam_ref, ln_ref, o_ref,
                 qm_scr, m_scr, acc_scr, *, lam_init):
    t = pl.program_id(1)
    qi = qi_ref[t]
    kj = kj_ref[t]
    tq = q_ref.shape[0]
    lane = lax.broadcasted_iota(jnp.int32, (tq, LANES), 1)

    def rope(x, cos, sin):
        return x.astype(F32) * cos + jnp.dot(x, rot_ref[...], preferred_element_type=F32) * sin

    @pl.when(kj == 0)
    def _():
        m_scr[...] = jnp.full_like(m_scr, -jnp.inf)
        acc_scr[...] = jnp.zeros_like(acc_scr)
        q = rope(q_ref[...], cq_ref[...], sq_ref[...])
        for h in range(N_HEADS):
            slab, low = _pair_lanes(h)
            base = 0 if low else HEAD_DIM
            for c in range(2):
                lo = base + c * DIFF_DH
                sel = jnp.logical_and(lane >= lo, lane < lo + DIFF_DH)
                qm_scr[2 * h + c] = jnp.where(sel, q[:, slab], 0.0).astype(BF16)

    def step(diagonal):
        k = rope(k_ref[...], ck_ref[...], sk_ref[...]).astype(BF16)
        vts = _values_and_ones(vt_ref[...], tq)
        if diagonal:
            masked = _key_after_query(tq)
        scores = []
        for h in range(N_HEADS):
            slab, _ = _pair_lanes(h)
            for c in range(2):
                i = 2 * h + c
                s = lax.dot_general(k[:, slab], qm_scr[i], (((1,), (1,)), ((), ())), preferred_element_type=F32)
                if diagonal:
                    s = jnp.where(masked, -jnp.inf, s)
                scores.append(s)
        _softmax_updates(scores, [vts[i // 2] for i in range(2 * N_HEADS)], m_scr, acc_scr)

    @pl.when(kj < qi)
    def _():
        step(False)

    @pl.when(kj == qi)
    def _():
        step(True)
        lp = lam_ref[...]
        lam = (jnp.exp(jnp.sum(lp[0:1] * lp[1:2], axis=-1, keepdims=True))
               - jnp.exp(jnp.sum(lp[2:3] * lp[3:4], axis=-1, keepdims=True)) + lam_init)
        head = lambda h: _normalized(acc_scr[2 * h]) - lam * _normalized(acc_scr[2 * h + 1])
        is_lo = lane < HEAD_DIM
        for p in range(N_HEADS // 2):
            o = jnp.concatenate([head(2 * p), head(2 * p + 1)], axis=0).T
            sq = o * o
            ms_lo = jnp.sum(jnp.where(is_lo, sq, 0.0), axis=-1, keepdims=True)
            ms_hi = jnp.sum(jnp.where(is_lo, 0.0, sq), axis=-1, keepdims=True)
            ms = jnp.where(is_lo, ms_lo, ms_hi) * (1.0 / HEAD_DIM)
            o_ref[:, p * LANES:(p + 1) * LANES] = (o * lax.rsqrt(ms + DIFF_LN_EPS) * ln_ref[...]
                                                   * (1.0 - lam_init)).astype(BF16)


def _diff_attention(u, vt, cos, sin, rot, lam_p, subln, batch, seq, tq, lam_init):
    nq = seq // tq
    qi, kj = _causal_pairs(nq)
    q_spec, k_spec, vt_spec = _attn_specs(nq, tq, U_DIFF, 0)
    tab_q = pl.BlockSpec((tq, W_MIX), lambda b, t, qi, kj: (qi[t], 0))
    tab_k = pl.BlockSpec((tq, W_MIX), lambda b, t, qi, kj: (kj[t], 0))
    const = lambda r, c: pl.BlockSpec((r, c), lambda b, t, qi, kj: (0, 0))
    stat = lambda: pltpu.VMEM((2 * N_HEADS, 1, tq), F32)
    return pl.pallas_call(
        functools.partial(_diff_kernel, lam_init=lam_init),
        grid_spec=pltpu.PrefetchScalarGridSpec(
            num_scalar_prefetch=2,
            grid=(batch, qi.shape[0]),
            in_specs=[q_spec, k_spec, vt_spec, tab_q, tab_q, tab_k, tab_k,
                      const(W_MIX, W_MIX), const(4, DIFF_DH), const(1, LANES)],
            out_specs=pl.BlockSpec((tq, W_MIX), lambda b, t, qi, kj: (b * nq + qi[t], 0)),
            scratch_shapes=[pltpu.VMEM((2 * N_HEADS, tq, LANES), BF16), stat(),
                            pltpu.VMEM((2 * N_HEADS, ACC_ROWS, tq), F32)],
        ),
        out_shape=jax.ShapeDtypeStruct((batch * seq, W_MIX), BF16),
        compiler_params=_cparams(("parallel", "arbitrary")),
        name="diff_attention",
    )(qi, kj, u, u, vt, cos, sin, cos, sin, rot, lam_p, subln)


def _merge_kernel(x_ref, gate_ref, oa_ref, post_ref, yb_ref, yc_ref, od_ref, gd_ref, lnw_ref, lnb_ref,
                  wbo_ref, wout_ref, o_ref):
    y_a = (oa_ref[...].astype(F32) * lnw_ref[...] + lnb_ref[...] + post_ref[1].astype(F32)) * post_ref[0].astype(F32)
    y_d = od_ref[...].astype(F32) * _silu(gd_ref[...].astype(F32))
    acc = jnp.zeros(x_ref.shape, F32)
    for b, y in enumerate((y_a, yb_ref[...], yc_ref[...], y_d)):
        gate = _sigmoid(gate_ref[:, b * D_MODEL:(b + 1) * D_MODEL].astype(F32))
        acc = acc + gate * _mm(y, wbo_ref[b])
    o_ref[...] = x_ref[...] + _mm(acc, wout_ref[...])


def _merge(x, u, o_a, post, y_b, y_c, o_d, ln_w, ln_b, w_bo, w_out, tm):
    n = x.shape[0]
    tok = lambda c: pl.BlockSpec((tm, c), lambda i: (i, 0))
    return pl.pallas_call(
        _merge_kernel,
        grid=(n // tm,),
        in_specs=[
            tok(D_MODEL),
            pl.BlockSpec((tm, 4 * D_MODEL), lambda i: (i, U_GATE // (4 * D_MODEL))),
            tok(W_MIX),
            pl.BlockSpec((2, tm, W_MIX), lambda i: (0, i, 0)),
            tok(W_MIX), tok(W_MIX), tok(W_MIX),
            pl.BlockSpec((tm, W_MIX), lambda i: (i, (U_GDN + 3 * W_MIX) // W_MIX)),
            pl.BlockSpec((1, W_MIX), lambda i: (0, 0)),
            pl.BlockSpec((1, W_MIX), lambda i: (0, 0)),
            pl.BlockSpec((4, W_MIX, D_MODEL), lambda i: (0, 0, 0)),
            pl.BlockSpec((D_MODEL, D_MODEL), lambda i: (0, 0)),
        ],
        out_specs=tok(D_MODEL),
        out_shape=jax.ShapeDtypeStruct((n, D_MODEL), F32),
        compiler_params=_cparams(("parallel",)),
        name="merge",
    )(x, u, o_a, post, y_b, y_c, o_d, u, ln_w, ln_b, w_bo, w_out)


def _ffn_kernel(x_ref, g_ref, wg_ref, wu_ref, wd_ref, o_ref):
    x = x_ref[...]
    h = _rms(x, g_ref[...], NORM_EPS).astype(BF16)
    act = _silu(jnp.dot(h, wg_ref[...], preferred_element_type=F32)) * jnp.dot(h, wu_ref[...], preferred_element_type=F32)
    o_ref[...] = x + _mm(act, wd_ref[...])


def _ffn(x, g, wg, wu, wd, tm):
    n = x.shape[0]
    ff = wg.shape[1]
    resident = lambda r, c: pl.BlockSpec((r, c), lambda i: (0, 0), pipeline_mode=pl.Buffered(1))
    return pl.pallas_call(
        _ffn_kernel,
        grid=(n // tm,),
        in_specs=[
            pl.BlockSpec((tm, D_MODEL), lambda i: (i, 0)),
            pl.BlockSpec((1, D_MODEL), lambda i: (0, 0)),
            resident(D_MODEL, ff), resident(D_MODEL, ff), resident(ff, D_MODEL),
        ],
        out_specs=pl.BlockSpec((tm, D_MODEL), lambda i: (i, 0)),
        out_shape=jax.ShapeDtypeStruct((n, D_MODEL), F32),
        compiler_params=_cparams(("parallel",), vmem_mb=VMEM_LARGE_MB),
        name="ffn",
    )(x, g, wg, wu, wd)


def _router_kernel(x_ref, g_ref, router_ref, h_ref, c_ref, rc_ref, rr_ref, cnt_ref):
    t = x_ref.shape[0]
    h = _rms(x_ref[...], g_ref[...], NORM_EPS)
    h_ref[...] = h.astype(BF16)
    logits = _mm_split(h, router_ref[...])
    lane = lax.broadcasted_iota(jnp.int32, logits.shape, 1).astype(F32)
    lg = jnp.where(lane < N_EXPERTS, logits, -jnp.inf)
    m1 = jnp.max(lg, axis=-1, keepdims=True)
    i1 = jnp.min(jnp.where(lg == m1, lane, float(LANES)), axis=-1, keepdims=True)
    lg2 = jnp.where(lane == i1, -jnp.inf, lg)
    m2 = jnp.max(lg2, axis=-1, keepdims=True)
    i2 = jnp.min(jnp.where(lg2 == m2, lane, float(LANES)), axis=-1, keepdims=True)
    e2 = jnp.exp(m2 - m1)
    c_ref[...] = jnp.where(lane == i1, 1.0 / (1.0 + e2), 0.0) + jnp.where(lane == i2, e2 / (1.0 + e2), 0.0)
    sel = jnp.logical_or(lane == i1, lane == i2)
    sel_f = jnp.where(sel, 1.0, 0.0)
    earlier = _tri_masks(t)[0].astype(BF16)
    rank = jnp.dot(earlier, sel_f.astype(BF16), preferred_element_type=F32)
    rc = jnp.where(sel, rank, -1.0)
    rc_ref[...] = rc
    rr_ref[...] = rc.T[0:BF16_ROWS, :]
    cnt_ref[...] = jnp.broadcast_to(jnp.sum(sel_f, axis=0, keepdims=True), cnt_ref.shape).astype(jnp.int32)


def _router(x, g, router, tm):
    n = x.shape[0]
    return pl.pallas_call(
        _router_kernel,
        grid=(n // tm,),
        in_specs=[
            pl.BlockSpec((tm, D_MODEL), lambda i: (i, 0)),
            pl.BlockSpec((1, D_MODEL), lambda i: (0, 0)),
            pl.BlockSpec((D_MODEL, LANES), lambda i: (0, 0)),
        ],
        out_specs=[
            pl.BlockSpec((tm, D_MODEL), lambda i: (i, 0)),
            pl.BlockSpec((tm, LANES), lambda i: (i, 0)),
            pl.BlockSpec((tm, LANES), lambda i: (i, 0)),
            pl.BlockSpec((BF16_ROWS, tm), lambda i: (0, i)),
            pl.BlockSpec((SUBLANES, LANES), lambda i: (i, 0)),
        ],
        out_shape=[
            jax.ShapeDtypeStruct((n, D_MODEL), BF16),
            jax.ShapeDtypeStruct((n, LANES), F32),
            jax.ShapeDtypeStruct((n, LANES), F32),
            jax.ShapeDtypeStruct((BF16_ROWS, n), F32),
            jax.ShapeDtypeStruct((n // tm * SUBLANES, LANES), jnp.int32),
        ],
        compiler_params=_cparams(("parallel",)),
        name="router",
    )(x, g, router)


def _moe_kernel(cnt_ref, x_ref, h_ref, c_ref, rc_ref, rr_ref, wg_ref, wu_ref, wd_ref, o_ref, xg_scr, yg_scr, *, rows):
    i = pl.program_id(0)
    e = pl.program_id(1)
    f = pl.program_id(2)
    last_f = f == pl.num_programs(2) - 1
    t = h_ref.shape[0]
    nsub = xg_scr.shape[0] // rows
    cnt = cnt_ref[i * N_EXPERTS + e]
    blocks = [(sb, slice(sb * rows, (sb + 1) * rows)) for sb in range(nsub)]

    @pl.when(jnp.logical_and(e == 0, f == 0))
    def _():
        o_ref[...] = x_ref[...]

    @pl.when(f == 0)
    def _():
        rr = rr_ref[pl.ds(e, 1), :]
        for sb, rs in blocks:
            @pl.when(cnt > sb * rows)
            def _():
                slot = (lax.broadcasted_iota(jnp.int32, (rows, t), 0) + sb * rows).astype(F32)
                gather = jnp.where(rr == slot, 1.0, 0.0).astype(BF16)
                xg_scr[rs, :] = jnp.dot(gather, h_ref[...], preferred_element_type=F32).astype(BF16)
                yg_scr[rs, :] = jnp.zeros((rows, D_MODEL), F32)

    for sb, rs in blocks:
        @pl.when(cnt > sb * rows)
        def _():
            xb = xg_scr[rs, :]
            act = (_silu(jnp.dot(xb, wg_ref[0], preferred_element_type=F32))
                   * jnp.dot(xb, wu_ref[0], preferred_element_type=F32))
            yg_scr[rs, :] += _mm(act, wd_ref[0])

    @pl.when(last_f)
    def _():
        lane = lax.broadcasted_iota(jnp.int32, (t, LANES), 1)
        mine = lane == e
        rc = jnp.sum(jnp.where(mine, rc_ref[...], 0.0), axis=-1, keepdims=True)
        ce = jnp.sum(jnp.where(mine, c_ref[...], 0.0), axis=-1, keepdims=True)
        for sb, rs in blocks:
            @pl.when(cnt > sb * rows)
            def _():
                slot = (lax.broadcasted_iota(jnp.int32, (t, rows), 1) + sb * rows).astype(F32)
                scatter = jnp.where(rc == slot, 1.0, 0.0).astype(BF16)
                o_ref[...] += ce * jnp.dot(scatter, yg_scr[rs, :].astype(BF16), preferred_element_type=F32)


def _moe(x, g, router, wg, wu, wd, tm, tf, rows):
    n = x.shape[0]
    ff = wg.shape[2]
    h, c, rc, rr, cnt = _router(x, g, router, tm)
    cnt = cnt.reshape(n // tm, SUBLANES, LANES)[:, 0, :N_EXPERTS].reshape(-1)
    nsub = -(-tm // rows)
    tok = lambda cols, **kw: pl.BlockSpec((tm, cols), lambda i, e, f, cnt: (i, 0), **kw)
    once = dict(pipeline_mode=pl.Buffered(1))
    return pl.pallas_call(
        functools.partial(_moe_kernel, rows=rows),
        grid_spec=pltpu.PrefetchScalarGridSpec(
            num_scalar_prefetch=1,
            grid=(n // tm, N_EXPERTS, ff // tf),
            in_specs=[
                tok(D_MODEL, **once), tok(D_MODEL, **once), tok(LANES), tok(LANES),
                pl.BlockSpec((BF16_ROWS, tm), lambda i, e, f, cnt: (0, i)),
                pl.BlockSpec((1, D_MODEL, tf), lambda i, e, f, cnt: (e, 0, f)),
                pl.BlockSpec((1, D_MODEL, tf), lambda i, e, f, cnt: (e, 0, f)),
                pl.BlockSpec((1, tf, D_MODEL), lambda i, e, f, cnt: (e, f, 0)),
            ],
            out_specs=tok(D_MODEL),
            scratch_shapes=[pltpu.VMEM((nsub * rows, D_MODEL), BF16), pltpu.VMEM((nsub * rows, D_MODEL), F32)],
        ),
        out_shape=jax.ShapeDtypeStruct((n, D_MODEL), F32),
        compiler_params=_cparams(("parallel", "arbitrary", "arbitrary"), vmem_mb=VMEM_LARGE_MB),
        name="moe",
    )(cnt, x, h, c, rc, rr, wg, wu, wd)


def _ple_kernel(x_ref, p_ref, g_ref, wgate_ref, wproj_ref, fin_ref, o_ref, *, final):
    x = x_ref[...]
    h = _rms(x, g_ref[...], NORM_EPS)
    y = x + _sigmoid(_mm(h, wgate_ref[...])) * _mm(p_ref[...], wproj_ref[...])
    o_ref[...] = _rms(y, fin_ref[...], NORM_EPS) if final else y


def _ple(x, p, g, wgate, wproj, fin, tm, final):
    n = x.shape[0]
    return pl.pallas_call(
        functools.partial(_ple_kernel, final=final),
        grid=(n // tm,),
        in_specs=[
            pl.BlockSpec((tm, D_MODEL), lambda i: (i, 0)),
            pl.BlockSpec((tm, P_DIM), lambda i: (i, 0)),
            pl.BlockSpec((1, D_MODEL), lambda i: (0, 0)),
            pl.BlockSpec((D_MODEL, D_MODEL), lambda i: (0, 0)),
            pl.BlockSpec((P_DIM, D_MODEL), lambda i: (0, 0)),
            pl.BlockSpec((1, D_MODEL), lambda i: (0, 0)),
        ],
        out_specs=pl.BlockSpec((tm, D_MODEL), lambda i: (i, 0)),
        out_shape=jax.ShapeDtypeStruct((n, D_MODEL), F32),
        compiler_params=_cparams(("parallel",)),
        name="ple",
    )(x, p, g, wgate, wproj, fin)


def _tiles(n, seq):
    tm = min(512, seq)
    tm_big = 1024 if n % 1024 == 0 else tm
    tm_in = 2048 if n % 2048 == 0 else tm_big
    moe_rows = tm_big // 4 + tm_big // 32
    return dict(tm=tm, tm_big=tm_big, tm_in=tm_in, tn_in=1024, tq=min(512, seq), cblk=min(512, seq), moe_rows=moe_rows)


def _rope_tables(seq):
    half = ROPE_DIMS // 2
    inv = ROPE_THETA ** (-jnp.arange(half, dtype=F32) * 2.0 / ROPE_DIMS)
    ang = jnp.arange(seq, dtype=F32)[:, None] * inv[None, :]
    pad = jnp.zeros((seq, DIFF_DH - ROPE_DIMS), F32)
    cos = jnp.concatenate([jnp.cos(ang), jnp.cos(ang), pad + 1.0], axis=-1)
    sin = jnp.concatenate([jnp.sin(ang), jnp.sin(ang), pad], axis=-1)
    d = jnp.arange(W_MIX)
    dd = d % DIFF_DH
    src = jnp.where(dd < half, d + half, d - half)
    sign = jnp.where(dd < half, -1.0, jnp.where(dd < ROPE_DIMS, 1.0, 0.0))
    rot = jnp.zeros((W_MIX, W_MIX), F32).at[src, d].set(sign)
    reps = W_MIX // DIFF_DH
    return jnp.tile(cos, (1, reps)), jnp.tile(sin, (1, reps)), rot.astype(BF16)


def _split_w_in(w):
    a0 = 0
    b0 = a0 + 4 * W_MIX
    c0 = b0 + 3 * W_MIX
    d0 = c0 + 3 * W_MIX + N_HEADS
    g0 = d0 + 4 * W_MIX + 2 * N_HEADS
    d_small = d0 + 3 * W_MIX
    diff_q = w[:, b0:b0 + W_MIX] * (DIFF_DH ** -0.5 * LOG2E)
    fox_q = w[:, c0:c0 + W_MIX] * (HEAD_DIM ** -0.5 * LOG2E)
    main = jnp.concatenate([
        w[:, g0:], w[:, a0:b0], w[:, d0:d_small], w[:, d_small + 2 * N_HEADS:g0],
        diff_q, w[:, b0 + W_MIX:b0 + 2 * W_MIX], fox_q, w[:, c0 + W_MIX:c0 + 2 * W_MIX],
    ], axis=1).astype(BF16)
    small = jnp.concatenate([
        w[:, c0 + 3 * W_MIX:d0], w[:, d_small:d_small + 2 * N_HEADS],
        jnp.zeros((D_MODEL, LANES - 3 * N_HEADS), w.dtype),
    ], axis=1).astype(BF16)
    v_t = jnp.concatenate([w[:, b0 + 2 * W_MIX:c0], w[:, c0 + 2 * W_MIX:c0 + 3 * W_MIX]], axis=1).T.astype(BF16)
    return main, small, v_t


def _small_params(fbias, a_log, dt_bias):
    zeros = jnp.zeros((N_HEADS,), F32)
    bias = jnp.concatenate([fbias, zeros, dt_bias, jnp.zeros((LANES - 3 * N_HEADS,), F32)])
    neg_a = jnp.concatenate([zeros, zeros, -jnp.exp(a_log), jnp.zeros((LANES - 3 * N_HEADS,), F32)])
    return bias.reshape(1, LANES), neg_a.reshape(1, LANES)


def _pad_rows(w, top, total):
    return jnp.concatenate([jnp.zeros((top, w.shape[1]), w.dtype), w,
                            jnp.zeros((total - top - w.shape[0], w.shape[1]), w.dtype)], axis=0)


def kernel(x, p, norm_mix, norm_ffn, norm_ple, w_in, w_bo, w_out, rwkv_mu, rwkv_w0, rwkv_w2, rwkv_a0, rwkv_a2, rwkv_g2, rwkv_kk, rwkv_ka, rwkv_rk, rwkv_ln_w, rwkv_ln_b, diff_lam, diff_subln, fox_fbias, gdn_conv, gdn_a_log, gdn_dt_bias, gdn_norm, ffn_w_gate, ffn_w_up, ffn_w_down, moe_router, moe_w_gate, moe_w_up, moe_w_down, ple_proj, ple_gate, final_norm):
    batch, seq, _ = x.shape
    depth = w_in.shape[0]
    n = batch * seq
    t = _tiles(n, seq)
    tm, tq, cblk = t["tm"], t["tq"], t["cblk"]
    row = lambda v: v.reshape(1, -1).astype(F32)
    cos, sin, rot = _rope_tables(seq)
    xf = x.reshape(n, D_MODEL)
    pf = p.reshape(depth, n, P_DIM)

    for i in range(depth):
        w_main, w_small, w_vt = _split_w_in(w_in[i])
        u, scol, vt = _inproj(xf, row(norm_mix[i]), w_main, w_small, w_vt, tm, t["tn_in"])
        bias, neg_a = _small_params(fox_fbias[i], gdn_a_log[i], gdn_dt_bias[i])
        hcol, fox_kb, fox_qb = _small_prep(scol, bias, neg_a, batch, seq, tm)

        scan_in, pc, post = _rwkv_prep(
            u, row(rwkv_mu[i]), row(rwkv_w0[i]), _pad_rows(rwkv_w2[i], 0, LANES), row(rwkv_a0[i]),
            _pad_rows(rwkv_a2[i], LANES // 2, LANES), rwkv_g2[i], row(rwkv_kk[i]), row(rwkv_ka[i]), row(rwkv_rk[i]),
            seq, tm)
        o_a = _rwkv_chunk(scan_in, pc, batch, seq, cblk)

        o_d = _gdn_chunk(_gdn_prep(u, gdn_conv[i].T, seq, tm), hcol, jnp.tile(row(gdn_norm[i]), (1, N_HEADS)),
                         batch, seq, cblk)

        lam_init = 0.8 - 0.6 * math.exp(-0.3 * i)
        y_b = _diff_attention(u, vt, cos, sin, rot, diff_lam[i].astype(F32), jnp.tile(row(diff_subln[i]), (1, 2)),
                              batch, seq, tq, lam_init)
        y_c = _fox_attention(u, vt, fox_qb, fox_kb, batch, seq, tq)

        xf = _merge(xf, u, o_a, post, y_b, y_c, o_d, row(rwkv_ln_w[i]), row(rwkv_ln_b[i]),
                    w_bo[i].astype(BF16), w_out[i].astype(BF16), tm)

        j = i // 2
        if i % 2 == 0:
            xf = _ffn(xf, row(norm_ffn[i]), ffn_w_gate[j].astype(BF16), ffn_w_up[j].astype(BF16),
                      ffn_w_down[j].astype(BF16), tm)
        else:
            router = jnp.concatenate([moe_router[j], jnp.zeros((D_MODEL, LANES - N_EXPERTS), F32)], axis=1)
            xf = _moe(xf, row(norm_ffn[i]), router, moe_w_gate[j].astype(BF16), moe_w_up[j].astype(BF16),
                      moe_w_down[j].astype(BF16), t["tm_big"], moe_w_gate.shape[3] // 2, t["moe_rows"])
        xf = _ple(xf, pf[i], row(norm_ple[i]), ple_gate[i].astype(BF16), ple_proj[i].astype(BF16),
                  row(final_norm), tm, i == depth - 1)
    return xf.reshape(batch, seq, D_MODEL)
```
